```python
import math
import jax, jax.numpy as jnp
from jax import lax
import numpy as np

D_MODEL = 1024
BATCH = 1
SEQ = 16384
DEPTH = 2

DA_HEADS = 4
DA_WIDTH = D_MODEL // 2
DA_HEAD_DIM = DA_WIDTH // (2 * DA_HEADS)
HG_HEADS = 4
HG_WIDTH = D_MODEL // 4
HG_KEY_DIM = HG_WIDTH // HG_HEADS
HG_VAL_DIM = HG_WIDTH // HG_HEADS
HG_CHUNK = 64
POOL_WINDOWS = (2, 4, 8, 16)
POOL_WIDTH = D_MODEL // 4
POOL_GROUP = POOL_WIDTH // len(POOL_WINDOWS)
D_MIX = DA_WIDTH + HG_WIDTH + POOL_WIDTH
PROJ_SIZES = (DA_WIDTH, DA_WIDTH, DA_WIDTH, HG_HEADS * HG_KEY_DIM, HG_HEADS * HG_KEY_DIM, HG_WIDTH, HG_WIDTH, POOL_WIDTH)
D_IN = sum(PROJ_SIZES)
ATTN_BLOCK = 128
ROPE_THETA = 10000.0
MASK_VALUE = -1e30
TINY = 1e-30
FFN_DIM = 11 * D_MODEL // 4
N_EXPERTS = 8
TOP_K = 2
EXPERT_DIM = 7 * D_MODEL // 2
N_DENSE = (DEPTH + 1) // 2
N_MOE = DEPTH // 2
ALPHA = (2 * DEPTH) ** 0.25
BETA = (8 * DEPTH) ** -0.25
EPS = 1e-5

kernel_name = 'hybrid_diffattn_hgrn2_pool_moe_deepnorm_adaln'


def layer_norm(x, g, b):
    xf = x.astype(jnp.float32)
    mu = jnp.mean(xf, axis=-1, keepdims=True)
    var = jnp.mean(jnp.square(xf - mu), axis=-1, keepdims=True)
    return ((xf - mu) * lax.rsqrt(var + EPS)).astype(x.dtype) * g + b


def rms_norm(x, g):
    xf = x.astype(jnp.float32)
    return (xf * lax.rsqrt(jnp.mean(xf * xf, axis=-1, keepdims=True) + EPS)).astype(x.dtype) * g


def rotary(x, pos):
    d = x.shape[-1]
    inv_freq = 1.0 / (ROPE_THETA ** (jnp.arange(0, d, 2, dtype=jnp.float32) / d))
    ang = pos.astype(jnp.float32)[:, None] * inv_freq[None, :]
    cos = jnp.cos(ang)[None, :, None, :]
    sin = jnp.sin(ang)[None, :, None, :]
    xf = x.astype(jnp.float32)
    x1, x2 = xf[..., : d // 2], xf[..., d // 2:]
    return jnp.concatenate([x1 * cos - x2 * sin, x2 * cos + x1 * sin], axis=-1).astype(x.dtype)


def diff_attention(q, k, v, lam, lam_init, norm_g):
    B, S, NH2, d = q.shape
    H = NH2 // 2
    nblk = S // ATTN_BLOCK
    kT = k.transpose(0, 2, 1, 3)
    vT = v.transpose(0, 2, 1, 3)
    qb = (q * (d ** -0.5)).reshape(B, nblk, ATTN_BLOCK, NH2, d).transpose(1, 0, 3, 2, 4)
    key_pos = jnp.arange(S)

    def block(args):
        q_blk, start = args
        s = jnp.einsum('bnqd,bnkd->bnqk', q_blk, kT).astype(jnp.float32)
        q_pos = start + jnp.arange(ATTN_BLOCK)
        s = jnp.where(key_pos[None, :] <= q_pos[:, None], s, MASK_VALUE)
        p = jax.nn.softmax(s, axis=-1).reshape(B, H, 2, ATTN_BLOCK, S)
        a = p[:, :, 0] - lam * p[:, :, 1]
        return jnp.einsum('bhqk,bhkd->bhqd', a.astype(v.dtype), vT)

    starts = jnp.arange(nblk) * ATTN_BLOCK
    o = lax.map(block, (qb, starts))
    o = o.transpose(1, 0, 3, 2, 4).reshape(B, S, H, 2 * d)
    o = rms_norm(o, norm_g) * (1.0 - lam_init)
    return o.reshape(B, S, H * 2 * d)


def hgrn2(q, f_logit, i, g, lb, norm_g):
    B, S, H, dk = q.shape
    dv = i.shape[-1]
    C = HG_CHUNK
    nc = S // C
    zf = f_logit.astype(jnp.float32)
    lbh = lb.reshape(H, dk).astype(jnp.float32)
    f = lbh + (1.0 - lbh) * jax.nn.sigmoid(zf)
    log_f = jnp.log(jnp.maximum(f, TINY))
    key = (1.0 - lbh) * jax.nn.sigmoid(-zf)
    val = jax.nn.silu(i.astype(jnp.float32))
    qf = q.astype(jnp.float32)

    def to_chunks(t):
        return t.reshape(B, nc, C, H, t.shape[-1]).transpose(1, 0, 3, 2, 4)

    causal = jnp.tril(jnp.ones((C, C), dtype=bool))[None, None, :, :, None]

    def step(state, inp):
        qc, kc, vc, lfc = inp
        b = jnp.cumsum(lfc, axis=2)
        diff = b[:, :, :, None, :] - b[:, :, None, :, :]
        decay = jnp.where(causal, jnp.exp(jnp.where(causal, diff, 0.0)), 0.0)
        scores = jnp.sum(qc[:, :, :, None, :] * kc[:, :, None, :, :] * decay, axis=-1)
        o = jnp.einsum('bhts,bhsv->bhtv', scores, vc) + jnp.einsum('bhtk,bhkv->bhtv', qc * jnp.exp(b), state)
        b_last = b[:, :, -1:, :]
        new_state = jnp.exp(b_last[:, :, 0, :])[..., None] * state + jnp.einsum('bhsk,bhsv->bhkv', kc * jnp.exp(b_last - b), vc)
        return new_state, o

    state0 = jnp.zeros((B, H, dk, dv), jnp.float32)
    _, o = lax.scan(step, state0, (to_chunks(qf), to_chunks(key), to_chunks(val), to_chunks(log_f)))
    o = o.transpose(1, 0, 3, 2, 4).reshape(B, S, H, dv).astype(g.dtype)
    o = rms_norm(o, norm_g) * jax.nn.silu(g)
    return o.reshape(B, S, H * dv)


def multiscale_pool(u, w, scale):
    B, S, _ = u.shape
    uf = u.astype(jnp.float32)
    csum = jnp.pad(jnp.cumsum(uf, axis=1), ((0, 0), (1, 0), (0, 0)))
    t = jnp.arange(S)
    outs = []
    for gi, win in enumerate(POOL_WINDOWS):
        sl = slice(gi * POOL_GROUP, (gi + 1) * POOL_GROUP)
        cs = csum[:, :, sl]
        lo = jnp.maximum(t + 1 - win, 0)
        total = cs[:, 1:] - cs[:, lo]
        count = (t + 1 - lo).astype(jnp.float32)
        outs.append(total / count[None, :, None] - uf[:, :, sl])
    pooled = jnp.stack(outs, axis=2).astype(u.dtype)
    y = jnp.einsum('bsgc,gcd->bsgd', pooled, w)
    return y.reshape(B, S, POOL_WIDTH) * scale


def token_mixer(h, w_in, lam_qk, attn_g, lb, hg_g, pool_w, pool_scale, w_out, layer_idx, pos):
    B, S, _ = h.shape
    proj = h @ w_in
    splits = [int(v) for v in np.cumsum(PROJ_SIZES)[:-1]]
    q, k, v, hq, hf, hi, hgate, pu = jnp.split(proj, splits, axis=-1)
    q = rotary(q.reshape(B, S, 2 * DA_HEADS, DA_HEAD_DIM), pos)
    k = rotary(k.reshape(B, S, 2 * DA_HEADS, DA_HEAD_DIM), pos)
    v = v.reshape(B, S, DA_HEADS, 2 * DA_HEAD_DIM)
    lam_init = 0.8 - 0.6 * math.exp(-0.3 * layer_idx)
    lq = lam_qk.astype(jnp.float32)
    lam = jnp.exp(jnp.sum(lq[0] * lq[1])) - jnp.exp(jnp.sum(lq[2] * lq[3])) + lam_init
    a = diff_attention(q, k, v, lam, lam_init, attn_g)
    r = hgrn2(hq.reshape(B, S, HG_HEADS, HG_KEY_DIM), hf.reshape(B, S, HG_HEADS, HG_KEY_DIM),
              hi.reshape(B, S, HG_HEADS, HG_VAL_DIM), hgate.reshape(B, S, HG_HEADS, HG_VAL_DIM), lb, hg_g)
    p = multiscale_pool(pu, pool_w, pool_scale)
    return jnp.concatenate([a, r, p], axis=-1) @ w_out


def swiglu(h, w1, w3, w2):
    return (jax.nn.silu(h @ w1) * (h @ w3)) @ w2


def moe_ffn(h, router_w, w1, w3, w2):
    logits = (h @ router_w).astype(jnp.float32)
    top_v, top_i = lax.top_k(logits, TOP_K)
    gates = jax.nn.softmax(top_v, axis=-1)
    combine = jnp.sum(jax.nn.one_hot(top_i, N_EXPERTS, dtype=jnp.float32) * gates[..., None], axis=-2)
    y = jnp.zeros_like(h)
    for e in range(N_EXPERTS):
        y = y + combine[..., e:e + 1].astype(h.dtype) * swiglu(h, w1[e], w3[e], w2[e])
    return y


def setup_inputs(seed: int = 0) -> dict:
    key = jax.random.key(seed)
    ks = jax.random.split(key, 24)
    D = D_MODEL

    def nrm(k, shape, std):
        return jax.random.normal(k, shape, jnp.float32) * std

    return {
        'x': nrm(ks[0], (BATCH, SEQ, D), 1.0),
        'c': nrm(ks[1], (BATCH, D), 1.0),
        'w_ada': nrm(ks[2], (DEPTH, D, 6 * D), 0.1 * D ** -0.5),
        'b_ada': nrm(ks[3], (DEPTH, 6 * D), 0.02),
        'w_in': nrm(ks[4], (DEPTH, D, D_IN), D ** -0.5),
        'lam_qk': nrm(ks[5], (DEPTH, 4, DA_HEAD_DIM), 0.1),
        'attn_norm_g': 1.0 + nrm(ks[6], (DEPTH, 2 * DA_HEAD_DIM), 0.02),
        'hg_lb_logits': nrm(ks[7], (DEPTH, HG_HEADS * HG_KEY_DIM), 0.5),
        'hg_norm_g': 1.0 + nrm(ks[8], (DEPTH, HG_VAL_DIM), 0.02),
        'pool_w': nrm(ks[9], (DEPTH, len(POOL_WINDOWS), POOL_GROUP, POOL_GROUP), POOL_GROUP ** -0.5),
        'pool_scale': 1.0 + nrm(ks[10], (DEPTH, POOL_WIDTH), 0.02),
        'w_out': nrm(ks[11], (DEPTH, D_MIX, D), BETA * D_MIX ** -0.5),
        'ln1_g': 1.0 + nrm(ks[12], (DEPTH, D), 0.02),
        'ln1_b': nrm(ks[13], (DEPTH, D), 0.02),
        'ln2_g': 1.0 + nrm(ks[14], (DEPTH, D), 0.02),
        'ln2_b': nrm(ks[15], (DEPTH, D), 0.02),
        'ffn_w1': nrm(ks[16], (N_DENSE, D, FFN_DIM), D ** -0.5),
        'ffn_w3': nrm(ks[17], (N_DENSE, D, FFN_DIM), D ** -0.5),
        'ffn_w2': nrm(ks[18], (N_DENSE, FFN_DIM, D), BETA * FFN_DIM ** -0.5),
        'router_w': nrm(ks[19], (N_MOE, D, N_EXPERTS), D ** -0.5),
        'exp_w1': nrm(ks[20], (N_MOE, N_EXPERTS, D, EXPERT_DIM), D ** -0.5),
        'exp_w3': nrm(ks[21], (N_MOE, N_EXPERTS, D, EXPERT_DIM), D ** -0.5),
        'exp_w2': nrm(ks[22], (N_MOE, N_EXPERTS, EXPERT_DIM, D), BETA * EXPERT_DIM ** -0.5),
    }


def reference(x, c, w_ada, b_ada, w_in, lam_qk, attn_norm_g, hg_lb_logits, hg_norm_g, pool_w, pool_scale,
              w_out, ln1_g, ln1_b, ln2_g, ln2_b, ffn_w1, ffn_w3, ffn_w2, router_w, exp_w1, exp_w3, exp_w2):
    S = x.shape[1]
    pos = jnp.arange(S)
    P = jax.nn.softmax(hg_lb_logits.astype(jnp.float32), axis=0)
    lbs = jnp.cumsum(P, axis=0) - P[0:1]
    cond = jax.nn.silu(c)
    for l in range(DEPTH):
        mod = (cond @ w_ada[l] + b_ada[l])[:, None, :]
        sh1, sc1, g1, sh2, sc2, g2 = jnp.split(mod, 6, axis=-1)
        h = x * (1.0 + sc1) + sh1
        m = token_mixer(h, w_in[l], lam_qk[l], attn_norm_g[l], lbs[l], hg_norm_g[l], pool_w[l], pool_scale[l],
                        w_out[l], l, pos)
        x = layer_norm(ALPHA * x + (1.0 + g1) * m, ln1_g[l], ln1_b[l])
        h = x * (1.0 + sc2) + sh2
        if l % 2 == 0:
            f = swiglu(h, ffn_w1[l // 2], ffn_w3[l // 2], ffn_w2[l // 2])
        else:
            f = moe_ffn(h, router_w[l // 2], exp_w1[l // 2], exp_w3[l // 2], exp_w2[l // 2])
        x = layer_norm(ALPHA * x + (1.0 + g2) * f, ln2_g[l], ln2_b[l])
    return x
```

```python
import functools
import math

import jax
import jax.numpy as jnp
from jax import lax
from jax.experimental import pallas as pl
from jax.experimental.pallas import tpu as pltpu

F32 = jnp.float32
BF16 = jnp.bfloat16
HIGHEST = lax.Precision.HIGHEST

D_MODEL = 1024
DEPTH = 2
DA_HEADS = 4
DA_WIDTH = D_MODEL // 2
DA_HEAD_DIM = DA_WIDTH // (2 * DA_HEADS)
DA_VAL_DIM = 2 * DA_HEAD_DIM
HG_HEADS = 4
HG_WIDTH = D_MODEL // 4
HG_DIM = HG_WIDTH // HG_HEADS
POOL_WINDOWS = (2, 4, 8, 16)
POOL_WIDTH = D_MODEL // 4
POOL_GROUP = POOL_WIDTH // len(POOL_WINDOWS)
D_MIX = DA_WIDTH + HG_WIDTH + POOL_WIDTH
D_IN = 3 * DA_WIDTH + 4 * HG_WIDTH + POOL_WIDTH
ROPE_THETA = 10000.0
MASK_VALUE = -1e30
TINY = 1e-30
FFN_DIM = 11 * D_MODEL // 4
N_EXPERTS = 8
EXPERT_DIM = 7 * D_MODEL // 2
ALPHA = (2 * DEPTH) ** 0.25
EPS = 1e-5

LANES = 128
MXU_DIM = 256
VMEM_LIMIT = 60 * 1024 * 1024

ROW_TILE = 512
ATTN_BLOCK = 512
HG_BLOCK = 512
HG_CHUNK = 16
POOL_HALO = 16
MOE_BLOCK = 2048
MOE_TILE = 256
MOE_CHUNK = 256
MOE_FSPLIT = 2


def _params(sem, vmem=VMEM_LIMIT):
    return pltpu.CompilerParams(dimension_semantics=sem, vmem_limit_bytes=vmem)


def _layer_norm(y, g, b):
    mu = jnp.mean(y, axis=-1, keepdims=True)
    d = y - mu
    var = jnp.mean(d * d, axis=-1, keepdims=True)
    return d * lax.rsqrt(var + EPS) * g + b


def _silu(x):
    return x * jax.nn.sigmoid(x)


def _head_block_mask(n, group):
    r = lax.broadcasted_iota(jnp.int32, (n, n), 0) // group
    c = lax.broadcasted_iota(jnp.int32, (n, n), 1) // group
    return r == c


def _mod_kernel(c_ref, w_ref, b_ref, o_ref):
    cond = _silu(c_ref[...])
    cond8 = jnp.broadcast_to(cond, (8, D_MODEL))
    r = jnp.dot(cond8, w_ref[0], preferred_element_type=F32, precision=HIGHEST)
    o_ref[0] = r[0:1] + b_ref[0]


def _modulation(c, w_ada, b_ada):
    n = 6 * DEPTH
    b3 = b_ada.reshape(n, 1, D_MODEL)
    out = pl.pallas_call(
        _mod_kernel,
        grid=(DEPTH, 6),
        in_specs=[
            pl.BlockSpec((1, D_MODEL), lambda l, j: (0, 0)),
            pl.BlockSpec((1, D_MODEL, D_MODEL), lambda l, j: (l, 0, j)),
            pl.BlockSpec((1, 1, D_MODEL), lambda l, j: (l * 6 + j, 0, 0)),
        ],
        out_specs=pl.BlockSpec((1, 1, D_MODEL), lambda l, j: (l * 6 + j, 0, 0)),
        out_shape=jax.ShapeDtypeStruct((n, 1, D_MODEL), F32),
        compiler_params=_params(("parallel", "parallel")),
        name="adaln_mod",
    )(c, w_ada, b3)
    return out.reshape(DEPTH, 6, 1, D_MODEL)


def _inproj_kernel(x_ref, sc_ref, sh_ref, w_ref, cos_ref, sin_ref,
                   q_ref, kt_ref, v_ref, hg_ref, pu_ref):
    h = (x_ref[...] * (1.0 + sc_ref[...]) + sh_ref[...]).astype(BF16)
    qk = jnp.dot(h, w_ref[:, : 2 * DA_WIDTH], preferred_element_type=F32)
    reps = 2 * DA_WIDTH // LANES
    cos = jnp.tile(cos_ref[...], (1, reps))
    sin = jnp.tile(sin_ref[...], (1, reps))
    lane = lax.broadcasted_iota(jnp.int32, qk.shape, 1)
    half = DA_HEAD_DIM // 2
    first_half = (lane & half) == 0
    width = 2 * DA_WIDTH
    partner = jnp.where(first_half, pltpu.roll(qk, width - half, 1), pltpu.roll(qk, half, 1))
    rot = qk * cos + partner * sin
    q_ref[...] = (rot[:, :DA_WIDTH] * (DA_HEAD_DIM ** -0.5)).astype(BF16)
    kt_ref[0] = rot[:, DA_WIDTH:].T.astype(BF16)
    o = 2 * DA_WIDTH
    v_ref[...] = jnp.dot(h, w_ref[:, o:o + DA_WIDTH], preferred_element_type=F32).astype(BF16)
    o += DA_WIDTH
    hg_ref[...] = jnp.dot(h, w_ref[:, o:o + 4 * HG_WIDTH], preferred_element_type=F32)
    o += 4 * HG_WIDTH
    pu_ref[...] = jnp.dot(h, w_ref[:, o:o + POOL_WIDTH], preferred_element_type=F32)


def _in_projection(x, sc, sh, w_in_bf, cos_t, sin_t):
    S = x.shape[0]
    tm = ATTN_BLOCK
    row = lambda i: (i, 0)
    fixed = lambda i: (0, 0)
    return pl.pallas_call(
        _inproj_kernel,
        grid=(S // tm,),
        in_specs=[
            pl.BlockSpec((tm, D_MODEL), row),
            pl.BlockSpec((1, D_MODEL), fixed),
            pl.BlockSpec((1, D_MODEL), fixed),
            pl.BlockSpec((D_MODEL, D_IN), fixed),
            pl.BlockSpec((tm, LANES), row),
            pl.BlockSpec((tm, LANES), row),
        ],
        out_specs=[
            pl.BlockSpec((tm, DA_WIDTH), row),
            pl.BlockSpec((1, DA_WIDTH, tm), lambda i: (i, 0, 0)),
            pl.BlockSpec((tm, DA_WIDTH), row),
            pl.BlockSpec((tm, 4 * HG_WIDTH), row),
            pl.BlockSpec((tm, POOL_WIDTH), row),
        ],
        out_shape=[
            jax.ShapeDtypeStruct((S, DA_WIDTH), BF16),
            jax.ShapeDtypeStruct((S // tm, DA_WIDTH, tm), BF16),
            jax.ShapeDtypeStruct((S, DA_WIDTH), BF16),
            jax.ShapeDtypeStruct((S, 4 * HG_WIDTH), F32),
            jax.ShapeDtypeStruct((S, POOL_WIDTH), F32),
        ],
        compiler_params=_params(("parallel",)),
        name="in_proj",
    )(x, sc, sh, w_in_bf, cos_t, sin_t)


def _attn_kernel(lam_ref, g_ref, q_ref, kt_ref, v_ref, o_ref, m_scr, l_scr, acc_scr, *, lam_init):
    bq = q_ref.shape[0]
    bk = kt_ref.shape[2]
    i = pl.program_id(1)
    q = q_ref[...]
    lane = lax.broadcasted_iota(jnp.int32, q.shape, 1)
    zero = jnp.zeros_like(q)
    qs = (jnp.where(lane < DA_HEAD_DIM, q, zero), jnp.where(lane >= DA_HEAD_DIM, q, zero))
    m_scr[...] = jnp.full(m_scr.shape, -jnp.inf, F32)
    l_scr[...] = jnp.zeros(l_scr.shape, F32)
    acc_scr[...] = jnp.zeros(acc_scr.shape, F32)
    row = lax.broadcasted_iota(jnp.int32, (bq, bk), 0)
    col = lax.broadcasted_iota(jnp.int32, (bq, bk), 1)

    def step(j, masked):
        kt = kt_ref[j]
        vv = v_ref[pl.ds(pl.multiple_of(j * bk, bk), bk), :]
        for a in range(2):
            s = jnp.dot(qs[a], kt, preferred_element_type=F32)
            if masked:
                s = jnp.where(col <= row, s, MASK_VALUE)
            m_prev = m_scr[a]
            m_new = jnp.maximum(m_prev, jnp.max(s, axis=1, keepdims=True))
            alpha = jnp.exp(m_prev - m_new)
            p = jnp.exp(s - m_new)
            l_scr[a] = alpha * l_scr[a] + jnp.sum(p, axis=1, keepdims=True)
            acc_scr[a] = alpha * acc_scr[a] + jnp.dot(p.astype(BF16), vv, preferred_element_type=F32)
            m_scr[a] = m_new

    def body(j, carry):
        step(j, False)
        return carry

    lax.fori_loop(0, i, body, 0)
    step(i, True)

    lq = lam_ref[...]
    lam = (jnp.exp(jnp.sum(lq[0:1] * lq[1:2], axis=1, keepdims=True))
           - jnp.exp(jnp.sum(lq[2:3] * lq[3:4], axis=1, keepdims=True)) + lam_init)
    o = acc_scr[0] / l_scr[0] - lam * (acc_scr[1] / l_scr[1])
    ms = jnp.mean(o * o, axis=1, keepdims=True)
    o_ref[...] = (o * lax.rsqrt(ms + EPS) * g_ref[...] * (1.0 - lam_init)).astype(BF16)


def _diff_attention(q, kt, v, lam_qk, attn_g, layer):
    S = q.shape[0]
    bq = bk = ATTN_BLOCK
    nk = S // bk
    lam_init = 0.8 - 0.6 * math.exp(-0.3 * layer)
    return pl.pallas_call(
        functools.partial(_attn_kernel, lam_init=lam_init),
        grid=(DA_HEADS, S // bq),
        in_specs=[
            pl.BlockSpec((4, DA_HEAD_DIM), lambda h, i: (0, 0)),
            pl.BlockSpec((1, DA_VAL_DIM), lambda h, i: (0, 0)),
            pl.BlockSpec((bq, LANES), lambda h, i: (i, h)),
            pl.BlockSpec((nk, LANES, bk), lambda h, i: (0, h, 0)),
            pl.BlockSpec((S, DA_VAL_DIM), lambda h, i: (0, h)),
        ],
        out_specs=pl.BlockSpec((bq, DA_VAL_DIM), lambda h, i: (i, h)),
        out_shape=jax.ShapeDtypeStruct((S, DA_WIDTH), BF16),
        scratch_shapes=[
            pltpu.VMEM((2, bq, 1), F32),
            pltpu.VMEM((2, bq, 1), F32),
            pltpu.VMEM((2, bq, DA_VAL_DIM), F32),
        ],
        compiler_params=_params(("parallel", "parallel")),
        name="diff_attn",
    )(lam_qk, attn_g.reshape(1, DA_VAL_DIM), q, kt, v)


def _hgrn_kernel(lbl_ref, g_ref, hg_ref, o_ref, st_scr, o_scr, *, layer):
    T = hg_ref.shape[0]
    c = HG_CHUNK
    W = HG_WIDTH

    @pl.when(pl.program_id(0) == 0)
    def _():
        st_scr[...] = jnp.zeros(st_scr.shape, F32)

    lg = lbl_ref[...]
    ex = jnp.exp(lg - jnp.max(lg, axis=0, keepdims=True))
    prob = ex / jnp.sum(ex, axis=0, keepdims=True)
    lb = jnp.zeros((1, W), F32)
    for li in range(1, layer + 1):
        lb = lb + prob[li:li + 1]

    same_head = _head_block_mask(W, HG_DIM)
    ones_bd = same_head.astype(BF16)
    rowi = lax.broadcasted_iota(jnp.int32, (c, W), 0)

    def body(s, carry):
        sl = pl.ds(pl.multiple_of(s * c, c), c)
        qf = hg_ref[sl, 0:W]
        z = hg_ref[sl, W:2 * W]
        iv = hg_ref[sl, 2 * W:3 * W]
        f = lb + (1.0 - lb) * jax.nn.sigmoid(z)
        logf = jnp.log(jnp.maximum(f, TINY))
        key = (1.0 - lb) * jax.nn.sigmoid(-z)
        val = _silu(iv)
        b = logf
        sft = 1
        while sft < c:
            b = b + jnp.where(rowi >= sft, pltpu.roll(b, sft, 0), 0.0)
            sft *= 2
        es = [(qf * key).astype(BF16)]
        vrs = [val]
        for d in range(1, c):
            kr = pltpu.roll(key, d, 0)
            br = pltpu.roll(b, d, 0)
            e = jnp.where(rowi >= d, qf * kr * jnp.exp(b - br), 0.0)
            es.append(e.astype(BF16))
            vrs.append(pltpu.roll(val, d, 0))
        estack = jnp.concatenate(es, axis=0)
        r = jnp.dot(estack, ones_bd, preferred_element_type=F32)
        o = r[0:c] * vrs[0]
        for d in range(1, c):
            o = o + r[d * c:(d + 1) * c] * vrs[d]
        st = st_scr[...]
        qd = (qf * jnp.exp(b)).astype(BF16)
        o = o + lax.dot_general(qd, st.astype(BF16), (((1,), (1,)), ((), ())),
                                preferred_element_type=F32)
        o_scr[sl, :] = o
        b_last = b[c - 1:c]
        kd = (key * jnp.exp(b_last - b)).astype(BF16)
        upd = lax.dot_general(val.astype(BF16), kd, (((0,), (0,)), ((), ())),
                              preferred_element_type=F32)
        st_scr[...] = jnp.exp(b_last) * st + jnp.where(same_head, upd, 0.0)
        return carry

    lax.fori_loop(0, T // c, body, 0)

    o = o_scr[...]
    ms = jnp.dot(o * o, same_head.astype(F32), preferred_element_type=F32,
                 precision=HIGHEST) * (1.0 / HG_DIM)
    gate = hg_ref[:, 3 * W:4 * W]
    o_ref[...] = (o * lax.rsqrt(ms + EPS) * g_ref[...] * _silu(gate)).astype(BF16)


def _hgrn2(hg, lb_logits, norm_g, layer):
    S = hg.shape[0]
    T = HG_BLOCK
    g_t = jnp.tile(norm_g.reshape(1, HG_DIM), (1, HG_HEADS))
    return pl.pallas_call(
        functools.partial(_hgrn_kernel, layer=layer),
        grid=(S // T,),
        in_specs=[
            pl.BlockSpec((DEPTH, HG_WIDTH), lambda i: (0, 0)),
            pl.BlockSpec((1, HG_WIDTH), lambda i: (0, 0)),
            pl.BlockSpec((T, 4 * HG_WIDTH), lambda i: (i, 0)),
        ],
        out_specs=pl.BlockSpec((T, HG_WIDTH), lambda i: (i, 0)),
        out_shape=jax.ShapeDtypeStruct((S, HG_WIDTH), BF16),
        scratch_shapes=[
            pltpu.VMEM((HG_WIDTH, HG_WIDTH), F32),
            pltpu.VMEM((T, HG_WIDTH), F32),
        ],
        compiler_params=_params(("arbitrary",)),
        name="hgrn2",
    )(lb_logits, g_t, hg)


def _pool_kernel(u_ref, w_ref, sc_ref, o_ref, halo_scr):
    T = u_ref.shape[0]
    H = POOL_HALO
    blk = pl.program_id(0)

    @pl.when(blk == 0)
    def _():
        halo_scr[...] = jnp.zeros(halo_scr.shape, F32)

    u = u_ref[...]
    ext = jnp.concatenate([halo_scr[...], u], axis=0)
    halo_scr[...] = u[T - H:, :]
    sums = []
    s = ext
    w = 1
    while w < POOL_WINDOWS[-1]:
        s = s + pltpu.roll(s, w, 0)
        w *= 2
        sums.append(s[H:, :])
    t1 = (blk * T + 1 + lax.broadcasted_iota(jnp.int32, (T, POOL_WIDTH), 0)).astype(F32)
    lane = lax.broadcasted_iota(jnp.int32, (T, POOL_WIDTH), 1)
    pooled = None
    for gi, win in enumerate(POOL_WINDOWS):
        mean = sums[gi] / jnp.minimum(t1, float(win))
        pooled = mean if pooled is None else jnp.where(lane >= gi * POOL_GROUP, mean, pooled)
    pooled = pooled - u
    y = jnp.dot(pooled.astype(BF16), w_ref[...], preferred_element_type=F32)
    o_ref[...] = (y * sc_ref[...]).astype(BF16)


def _multiscale_pool(pu, pool_w, pool_scale):
    S = pu.shape[0]
    T = ROW_TILE
    w_bd = jax.scipy.linalg.block_diag(*[pool_w[g] for g in range(len(POOL_WINDOWS))]).astype(BF16)
    return pl.pallas_call(
        _pool_kernel,
        grid=(S // T,),
        in_specs=[
            pl.BlockSpec((T, POOL_WIDTH), lambda i: (i, 0)),
            pl.BlockSpec((POOL_WIDTH, POOL_WIDTH), lambda i: (0, 0)),
            pl.BlockSpec((1, POOL_WIDTH), lambda i: (0, 0)),
        ],
        out_specs=pl.BlockSpec((T, POOL_WIDTH), lambda i: (i, 0)),
        out_shape=jax.ShapeDtypeStruct((S, POOL_WIDTH), BF16),
        scratch_shapes=[pltpu.VMEM((POOL_HALO, POOL_WIDTH), F32)],
        compiler_params=_params(("arbitrary",)),
        name="ms_pool",
    )(pu, w_bd, pool_scale.reshape(1, POOL_WIDTH))


def _outproj_kernel(a_ref, r_ref, p_ref, x_ref, w_ref, g1_ref, lg_ref, lb_ref, o_ref):
    m = jnp.dot(a_ref[...], w_ref[0:DA_WIDTH, :], preferred_element_type=F32)
    m += jnp.dot(r_ref[...], w_ref[DA_WIDTH:DA_WIDTH + HG_WIDTH, :], preferred_element_type=F32)
    m += jnp.dot(p_ref[...], w_ref[DA_WIDTH + HG_WIDTH:, :], preferred_element_type=F32)
    y = ALPHA * x_ref[...] + (1.0 + g1_ref[...]) * m
    o_ref[...] = _layer_norm(y, lg_ref[...], lb_ref[...])


def _out_projection(a, r, p, x, w_out_bf, g1, ln_g, ln_b):
    S = x.shape[0]
    tm = ROW_TILE
    row = lambda i: (i, 0)
    fixed = lambda i: (0, 0)
    vec = pl.BlockSpec((1, D_MODEL), fixed)
    return pl.pallas_call(
        _outproj_kernel,
        grid=(S // tm,),
        in_specs=[
            pl.BlockSpec((tm, DA_WIDTH), row),
            pl.BlockSpec((tm, HG_WIDTH), row),
            pl.BlockSpec((tm, POOL_WIDTH), row),
            pl.BlockSpec((tm, D_MODEL), row),
            pl.BlockSpec((D_MIX, D_MODEL), fixed),
            vec, vec, vec,
        ],
        out_specs=pl.BlockSpec((tm, D_MODEL), row),
        out_shape=jax.ShapeDtypeStruct((S, D_MODEL), F32),
        compiler_params=_params(("parallel",)),
        name="out_proj_ln",
    )(a, r, p, x, w_out_bf, g1, ln_g.reshape(1, D_MODEL), ln_b.reshape(1, D_MODEL))


def _ffn_kernel(x_ref, sc_ref, sh_ref, g2_ref, w1_ref, w3_ref, w2_ref, lg_ref, lb_ref, o_ref):
    x = x_ref[...]
    h = (x * (1.0 + sc_ref[...]) + sh_ref[...]).astype(BF16)
    acc = jnp.zeros(x.shape, F32)
    step = 4 * MXU_DIM
    for lo in range(0, FFN_DIM, step):
        hi = min(lo + step, FFN_DIM)
        a = jnp.dot(h, w1_ref[:, lo:hi], preferred_element_type=F32)
        b = jnp.dot(h, w3_ref[:, lo:hi], preferred_element_type=F32)
        acc += jnp.dot((_silu(a) * b).astype(BF16), w2_ref[lo:hi, :], preferred_element_type=F32)
    y = ALPHA * x + (1.0 + g2_ref[...]) * acc
    o_ref[...] = _layer_norm(y, lg_ref[...], lb_ref[...])


def _dense_ffn(x, sc, sh, g2, w1, w3, w2, ln_g, ln_b):
    S = x.shape[0]
    tm = ROW_TILE
    row = lambda i: (i, 0)
    fixed = lambda i: (0, 0)
    vec = pl.BlockSpec((1, D_MODEL), fixed)
    once = pl.Buffered(1)
    return pl.pallas_call(
        _ffn_kernel,
        grid=(S // tm,),
        in_specs=[
            pl.BlockSpec((tm, D_MODEL), row),
            vec, vec, vec,
            pl.BlockSpec((D_MODEL, FFN_DIM), fixed, pipeline_mode=once),
            pl.BlockSpec((D_MODEL, FFN_DIM), fixed, pipeline_mode=once),
            pl.BlockSpec((FFN_DIM, D_MODEL), fixed, pipeline_mode=once),
            vec, vec,
        ],
        out_specs=pl.BlockSpec((tm, D_MODEL), row),
        out_shape=jax.ShapeDtypeStruct((S, D_MODEL), F32),
        compiler_params=_params(("parallel",)),
        name="dense_ffn_ln",
    )(x, sc, sh, g2, w1, w3, w2, ln_g.reshape(1, D_MODEL), ln_b.reshape(1, D_MODEL))


def _router_kernel(x_ref, sc_ref, sh_ref, rw_ref, h_ref, rank_ref, comb_ref, cum_ref):
    T = x_ref.shape[0]
    tc = MOE_CHUNK
    E = N_EXPERTS
    h = x_ref[...] * (1.0 + sc_ref[...]) + sh_ref[...]
    h_ref[...] = h.astype(BF16)
    logits = lax.dot_general(rw_ref[...], h, (((1,), (1,)), ((), ())),
                             preferred_element_type=F32, precision=HIGHEST)
    eidx = lax.broadcasted_iota(jnp.int32, (E, T), 0)
    v1 = jnp.max(logits, axis=0, keepdims=True)
    i1 = jnp.min(jnp.where(logits == v1, eidx, E), axis=0, keepdims=True)
    m1 = eidx == i1
    rest = jnp.where(m1, -jnp.inf, logits)
    v2 = jnp.max(rest, axis=0, keepdims=True)
    i2 = jnp.min(jnp.where(rest == v2, eidx, E), axis=0, keepdims=True)
    m2 = eidx == i2
    e2 = jnp.exp(v2 - v1)
    g1 = 1.0 / (1.0 + e2)
    g2 = e2 / (1.0 + e2)
    comb_ref[...] = (jnp.where(m1, g1, 0.0) + jnp.where(m2, g2, 0.0)).reshape(E, 1, T)
    routed = jnp.logical_or(m1, m2)
    ind = routed.astype(BF16)
    r_i = lax.broadcasted_iota(jnp.int32, (tc, tc), 0)
    c_i = lax.broadcasted_iota(jnp.int32, (tc, tc), 1)
    strict = (r_i < c_i).astype(BF16)
    lane = lax.broadcasted_iota(jnp.int32, (E, LANES), 1)
    running = jnp.zeros((E, 1), F32)
    cum = jnp.zeros((E, LANES), F32)
    ranks = []
    for ci in range(T // tc):
        ind_c = ind[:, ci * tc:(ci + 1) * tc]
        ranks.append(jnp.dot(ind_c, strict, preferred_element_type=F32) + running)
        cum = jnp.where(lane == ci, running, cum)
        running = running + jnp.sum(ind_c.astype(F32), axis=1, keepdims=True)
    cum = jnp.where(lane == T // tc, running, cum)
    rank = jnp.concatenate(ranks, axis=1)
    rank_ref[...] = jnp.where(routed, rank, -1.0).reshape(E, 1, T)
    cum_ref[0] = cum.astype(jnp.int32)


def _router(x, sc, sh, router_w):
    S = x.shape[0]
    T = MOE_BLOCK
    E = N_EXPERTS
    fixed = lambda i: (0, 0)
    vec = pl.BlockSpec((1, D_MODEL), fixed)
    return pl.pallas_call(
        _router_kernel,
        grid=(S // T,),
        in_specs=[
            pl.BlockSpec((T, D_MODEL), lambda i: (i, 0)),
            vec, vec,
            pl.BlockSpec((E, D_MODEL), fixed),
        ],
        out_specs=[
            pl.BlockSpec((T, D_MODEL), lambda i: (i, 0)),
            pl.BlockSpec((E, 1, T), lambda i: (0, 0, i)),
            pl.BlockSpec((E, 1, T), lambda i: (0, 0, i)),
            pl.BlockSpec((1, E, LANES), lambda i: (i, 0, 0)),
        ],
        out_shape=[
            jax.ShapeDtypeStruct((S, D_MODEL), BF16),
            jax.ShapeDtypeStruct((E, 1, S), F32),
            jax.ShapeDtypeStruct((E, 1, S), F32),
            jax.ShapeDtypeStruct((S // T, E, LANES), jnp.int32),
        ],
        compiler_params=_params(("parallel",)),
        name="moe_router",
    )(x, sc, sh, router_w.T)


def _moe_kernel(cum_ref, h_ref, rank_ref, comb_ref, w1_ref, w3_ref, w2_ref, y_ref,
                xe_scr, gate_scr, xacc_scr, gacc_scr):
    T = h_ref.shape[0]
    tr, tc = MOE_TILE, MOE_CHUNK
    nchunk = T // tc
    b = pl.program_id(0)
    e = pl.program_id(1)
    f = pl.program_id(2)

    @pl.when(jnp.logical_and(e == 0, f == 0))
    def _():
        y_ref[...] = jnp.zeros(y_ref.shape, F32)

    base = (b * N_EXPERTS + e) * LANES
    count = cum_ref[base + nchunk]
    ntiles = lax.shift_right_logical(count + (tr - 1), int(math.log2(tr)))
    row_id = lax.broadcasted_iota(jnp.int32, (tr, tc), 0).astype(F32)

    def tile(r, carry):
        lo = r * tr
        rows = pl.ds(pl.multiple_of(lo, tr), tr)
        want = row_id + lo.astype(F32)

        def chunk_overlaps(ci):
            return jnp.logical_and(cum_ref[base + ci] < lo + tr, cum_ref[base + ci + 1] > lo)

        def onehot(ci):
            return rank_ref[0, :, ci * tc:(ci + 1) * tc] == want

        @pl.when(f == 0)
        def _():
            xacc_scr[...] = jnp.zeros(xacc_scr.shape, F32)
            gacc_scr[...] = jnp.zeros(gacc_scr.shape, F32)
            for ci in range(nchunk):
                @pl.when(chunk_overlaps(ci))
                def _():
                    hit = onehot(ci)
                    xacc_scr[...] += jnp.dot(hit.astype(BF16), h_ref[ci * tc:(ci + 1) * tc, :],
                                             preferred_element_type=F32)
                    gacc_scr[...] += jnp.sum(
                        jnp.where(hit, comb_ref[0, :, ci * tc:(ci + 1) * tc], 0.0),
                        axis=1, keepdims=True)
            xe_scr[rows, :] = xacc_scr[...].astype(BF16)
            gate_scr[rows, :] = gacc_scr[...]

        xe = xe_scr[rows, :]
        a = jnp.dot(xe, w1_ref[0], preferred_element_type=F32)
        g = jnp.dot(xe, w3_ref[0], preferred_element_type=F32)
        mid = (_silu(a) * g).astype(BF16)
        out = jnp.dot(mid, w2_ref[0], preferred_element_type=F32) * gate_scr[rows, :]
        out_bf = out.astype(BF16)
        for ci in range(nchunk):
            @pl.when(chunk_overlaps(ci))
            def _():
                hit = onehot(ci).astype(BF16)
                y_ref[ci * tc:(ci + 1) * tc, :] += lax.dot_general(
                    hit, out_bf, (((0,), (0,)), ((), ())), preferred_element_type=F32)
        return carry

    lax.fori_loop(0, ntiles, tile, 0)


def _moe_ffn(h_bf, rank, comb, cum, w1, w3, w2):
    S = h_bf.shape[0]
    T = MOE_BLOCK
    fw = EXPERT_DIM // MOE_FSPLIT
    grid_spec = pltpu.PrefetchScalarGridSpec(
        num_scalar_prefetch=1,
        grid=(S // T, N_EXPERTS, MOE_FSPLIT),
        in_specs=[
            pl.BlockSpec((T, D_MODEL), lambda b, e, f, c: (b, 0)),
            pl.BlockSpec((1, 1, T), lambda b, e, f, c: (e, 0, b)),
            pl.BlockSpec((1, 1, T), lambda b, e, f, c: (e, 0, b)),
            pl.BlockSpec((1, D_MODEL, fw), lambda b, e, f, c: (e, 0, f)),
            pl.BlockSpec((1, D_MODEL, fw), lambda b, e, f, c: (e, 0, f)),
            pl.BlockSpec((1, fw, D_MODEL), lambda b, e, f, c: (e, f, 0)),
        ],
        out_specs=pl.BlockSpec((T, D_MODEL), lambda b, e, f, c: (b, 0)),
        scratch_shapes=[
            pltpu.VMEM((T, D_MODEL), BF16),
            pltpu.VMEM((T, 1), F32),
            pltpu.VMEM((MOE_TILE, D_MODEL), F32),
            pltpu.VMEM((MOE_TILE, 1), F32),
        ],
    )
    return pl.pallas_call(
        _moe_kernel,
        grid_spec=grid_spec,
        out_shape=jax.ShapeDtypeStruct((S, D_MODEL), F32),
        compiler_params=_params(("arbitrary", "arbitrary", "arbitrary")),
        name="moe_experts",
    )(cum.reshape(-1), h_bf, rank, comb, w1, w3, w2)


def _ln_kernel(x_ref, y_ref, g2_ref, lg_ref, lb_ref, o_ref):
    y = ALPHA * x_ref[...] + (1.0 + g2_ref[...]) * y_ref[...]
    o_ref[...] = _layer_norm(y, lg_ref[...], lb_ref[...])


def _residual_ln(x, y, g2, ln_g, ln_b):
    S = x.shape[0]
    tm = ROW_TILE
    row = lambda i: (i, 0)
    vec = pl.BlockSpec((1, D_MODEL), lambda i: (0, 0))
    return pl.pallas_call(
        _ln_kernel,
        grid=(S // tm,),
        in_specs=[pl.BlockSpec((tm, D_MODEL), row), pl.BlockSpec((tm, D_MODEL), row), vec, vec, vec],
        out_specs=pl.BlockSpec((tm, D_MODEL), row),
        out_shape=jax.ShapeDtypeStruct((S, D_MODEL), F32),
        compiler_params=_params(("parallel",)),
        name="residual_ln",
    )(x, y, g2, ln_g.reshape(1, D_MODEL), ln_b.reshape(1, D_MODEL))


def _rope_tables(S):
    d = DA_HEAD_DIM
    inv_freq = 1.0 / (ROPE_THETA ** (jnp.arange(0, d, 2, dtype=F32) / d))
    ang = jnp.arange(S, dtype=F32)[:, None] * inv_freq[None, :]
    cos = jnp.cos(ang)
    sin = jnp.sin(ang)
    reps = LANES // d
    cos_t = jnp.tile(jnp.concatenate([cos, cos], axis=1), (1, reps))
    sin_t = jnp.tile(jnp.concatenate([-sin, sin], axis=1), (1, reps))
    return cos_t, sin_t


def kernel(x, c, w_ada, b_ada, w_in, lam_qk, attn_norm_g, hg_lb_logits, hg_norm_g, pool_w, pool_scale, w_out, ln1_g, ln1_b, ln2_g, ln2_b, ffn_w1, ffn_w3, ffn_w2, router_w, exp_w1, exp_w3, exp_w2):
    B, S, D = x.shape
    assert B == 1 and D == D_MODEL and S % MOE_BLOCK == 0
    xs = x.reshape(S, D)
    mod = _modulation(c, w_ada, b_ada)
    cos_t, sin_t = _rope_tables(S)
    for l in range(DEPTH):
        sh1, sc1, g1, sh2, sc2, g2 = (mod[l, j] for j in range(6))
        q, kt, v, hg, pu = _in_projection(xs, sc1, sh1, w_in[l].astype(BF16), cos_t, sin_t)
        a = _diff_attention(q, kt, v, lam_qk[l], attn_norm_g[l], l)
        r = _hgrn2(hg, hg_lb_logits, hg_norm_g[l], l)
        p = _multiscale_pool(pu, pool_w[l], pool_scale[l])
        xs = _out_projection(a, r, p, xs, w_out[l].astype(BF16), g1, ln1_g[l], ln1_b[l])
        if l % 2 == 0:
            i = l // 2
            xs = _dense_ffn(xs, sc2, sh2, g2, ffn_w1[i].astype(BF16), ffn_w3[i].astype(BF16),
                            ffn_w2[i].astype(BF16), ln2_g[l], ln2_b[l])
        else:
            i = l // 2
            h_bf, rank, comb, cum = _router(xs, sc2, sh2, router_w[i])
            y = _moe_ffn(h_bf, rank, comb, cum, exp_w1[i].astype(BF16), exp_w3[i].astype(BF16),
                         exp_w2[i].astype(BF16))
            xs = _residual_ln(xs, y, g2, ln2_g[l], ln2_b[l])
    return xs.reshape(B, S, D)
```

```python
import functools
import math

import jax
import jax.numpy as jnp
from jax import lax
from jax.experimental import pallas as pl
from jax.experimental.pallas import tpu as pltpu

F32 = jnp.float32
BF16 = jnp.bfloat16
HIGHEST = lax.Precision.HIGHEST

D_MODEL = 1024
DEPTH = 2
DA_HEADS = 4
DA_WIDTH = D_MODEL // 2
DA_HEAD_DIM = DA_WIDTH // (2 * DA_HEADS)
DA_VAL_DIM = 2 * DA_HEAD_DIM
HG_HEADS = 4
HG_WIDTH = D_MODEL // 4
HG_DIM = HG_WIDTH // HG_HEADS
POOL_WINDOWS = (2, 4, 8, 16)
POOL_WIDTH = D_MODEL // 4
POOL_GROUP = POOL_WIDTH // len(POOL_WINDOWS)
D_MIX = DA_WIDTH + HG_WIDTH + POOL_WIDTH
D_IN = 3 * DA_WIDTH + 4 * HG_WIDTH + POOL_WIDTH
ROPE_THETA = 10000.0
MASK_VALUE = -1e30
TINY = 1e-30
FFN_DIM = 11 * D_MODEL // 4
N_EXPERTS = 8
EXPERT_DIM = 7 * D_MODEL // 2
ALPHA = (2 * DEPTH) ** 0.25
EPS = 1e-5
LOG2_E = math.log2(math.e)

LANES = 128
MXU_DIM = 256
VMEM_LIMIT = 60 * 1024 * 1024

ROW_TILE = 512
ATTN_BLOCK = 512
ATTN_QUERY_GROUP = 256
HG_BLOCK = 512
HG_CHUNK = 16
POOL_HALO = 16
MOE_BLOCK = 2048
MOE_TILE = 256
MOE_CHUNK = 256
MOE_FSPLIT = 2


def _params(sem, vmem=VMEM_LIMIT):
    return pltpu.CompilerParams(dimension_semantics=sem, vmem_limit_bytes=vmem)


def _layer_norm(y, g, b):
    mu = jnp.mean(y, axis=-1, keepdims=True)
    d = y - mu
    var = jnp.mean(d * d, axis=-1, keepdims=True)
    return d * lax.rsqrt(var + EPS) * g + b


def _silu(x):
    return x * jax.nn.sigmoid(x)


def _head_block_mask(n, group):
    r = lax.broadcasted_iota(jnp.int32, (n, n), 0) // group
    c = lax.broadcasted_iota(jnp.int32, (n, n), 1) // group
    return r == c


def _mod_kernel(c_ref, w_ref, b_ref, o_ref):
    cond = _silu(c_ref[...])
    cond8 = jnp.broadcast_to(cond, (8, D_MODEL))
    r = jnp.dot(cond8, w_ref[0], preferred_element_type=F32, precision=HIGHEST)
    o_ref[0] = r[0:1] + b_ref[0]


def _modulation(c, w_ada, b_ada):
    n = 6 * DEPTH
    b3 = b_ada.reshape(n, 1, D_MODEL)
    out = pl.pallas_call(
        _mod_kernel,
        grid=(DEPTH, 6),
        in_specs=[
            pl.BlockSpec((1, D_MODEL), lambda l, j: (0, 0)),
            pl.BlockSpec((1, D_MODEL, D_MODEL), lambda l, j: (l, 0, j)),
            pl.BlockSpec((1, 1, D_MODEL), lambda l, j: (l * 6 + j, 0, 0)),
        ],
        out_specs=pl.BlockSpec((1, 1, D_MODEL), lambda l, j: (l * 6 + j, 0, 0)),
        out_shape=jax.ShapeDtypeStruct((n, 1, D_MODEL), F32),
        compiler_params=_params(("parallel", "parallel")),
        name="adaln_mod",
    )(c, w_ada, b3)
    return out.reshape(DEPTH, 6, 1, D_MODEL)


def _inproj_kernel(x_ref, sc_ref, sh_ref, w_ref, wvt_ref, cos_ref, sin_ref,
                   qt_ref, k_ref, vt_ref, hg_ref, pu_ref):
    h = (x_ref[...] * (1.0 + sc_ref[...]) + sh_ref[...]).astype(BF16)
    qk = jnp.dot(h, w_ref[:, : 2 * DA_WIDTH], preferred_element_type=F32)
    reps = 2 * DA_WIDTH // LANES
    cos = jnp.tile(cos_ref[...], (1, reps))
    sin = jnp.tile(sin_ref[...], (1, reps))
    lane = lax.broadcasted_iota(jnp.int32, qk.shape, 1)
    half = DA_HEAD_DIM // 2
    first_half = (lane & half) == 0
    width = 2 * DA_WIDTH
    partner = jnp.where(first_half, pltpu.roll(qk, width - half, 1), pltpu.roll(qk, half, 1))
    rot = qk * cos + partner * sin
    qt_ref[0] = (rot[:, :DA_WIDTH] * (DA_HEAD_DIM ** -0.5 * LOG2_E)).T.astype(BF16)
    k_ref[...] = rot[:, DA_WIDTH:].astype(BF16)
    o = 2 * DA_WIDTH
    vt_ref[0] = lax.dot_general(wvt_ref[...], h, (((1,), (1,)), ((), ())),
                                preferred_element_type=F32).astype(BF16)
    o += DA_WIDTH
    hg_ref[...] = jnp.dot(h, w_ref[:, o:o + 4 * HG_WIDTH], preferred_element_type=F32)
    o += 4 * HG_WIDTH
    pu_ref[...] = jnp.dot(h, w_ref[:, o:o + POOL_WIDTH], preferred_element_type=F32)


def _in_projection(x, sc, sh, w_in_bf, cos_t, sin_t):
    S = x.shape[0]
    tm = ATTN_BLOCK
    row = lambda i: (i, 0)
    fixed = lambda i: (0, 0)
    return pl.pallas_call(
        _inproj_kernel,
        grid=(S // tm,),
        in_specs=[
            pl.BlockSpec((tm, D_MODEL), row),
            pl.BlockSpec((1, D_MODEL), fixed),
            pl.BlockSpec((1, D_MODEL), fixed),
            pl.BlockSpec((D_MODEL, D_IN), fixed),
            pl.BlockSpec((DA_WIDTH, D_MODEL), fixed),
            pl.BlockSpec((tm, LANES), row),
            pl.BlockSpec((tm, LANES), row),
        ],
        out_specs=[
            pl.BlockSpec((1, DA_WIDTH, tm), lambda i: (i, 0, 0)),
            pl.BlockSpec((tm, DA_WIDTH), row),
            pl.BlockSpec((1, DA_WIDTH, tm), lambda i: (i, 0, 0)),
            pl.BlockSpec((tm, 4 * HG_WIDTH), row),
            pl.BlockSpec((tm, POOL_WIDTH), row),
        ],
        out_shape=[
            jax.ShapeDtypeStruct((S // tm, DA_WIDTH, tm), BF16),
            jax.ShapeDtypeStruct((S, DA_WIDTH), BF16),
            jax.ShapeDtypeStruct((S // tm, DA_WIDTH, tm), BF16),
            jax.ShapeDtypeStruct((S, 4 * HG_WIDTH), F32),
            jax.ShapeDtypeStruct((S, POOL_WIDTH), F32),
        ],
        compiler_params=_params(("parallel",)),
        name="in_proj",
    )(x, sc, sh, w_in_bf, w_in_bf[:, 2 * DA_WIDTH:3 * DA_WIDTH].T, cos_t, sin_t)


def _attn_kernel(lam_ref, g_ref, qt_ref, k_ref, vt_ref, o_ref, m_scr, l_scr, acc_scr, s_scr, p_scr,
                 *, lam_init):
    bq = qt_ref.shape[2]
    bk = vt_ref.shape[2]
    cg = ATTN_QUERY_GROUP
    i = pl.program_id(1)
    qt = qt_ref[0]
    rowq = lax.broadcasted_iota(jnp.int32, qt.shape, 0)
    zero = jnp.zeros_like(qt)
    maps = (jnp.where(rowq < DA_HEAD_DIM, qt, zero), jnp.where(rowq >= DA_HEAD_DIM, qt, zero))
    groups = [(a * bq + c0, c0, maps[a][:, c0:c0 + cg]) for a in range(2) for c0 in range(0, bq, cg)]
    m_scr[...] = jnp.full(m_scr.shape, -jnp.inf, F32)
    l_scr[...] = jnp.zeros(l_scr.shape, F32)
    acc_scr[...] = jnp.zeros(acc_scr.shape, F32)
    key_i = lax.broadcasted_iota(jnp.int32, (bk, cg), 0)
    qry_i = lax.broadcasted_iota(jnp.int32, (bk, cg), 1)
    last = len(groups) - 1
    last_cols = slice(groups[last][0], groups[last][0] + cg)

    def scores(j, g):
        kb = k_ref[pl.ds(pl.multiple_of(j * bk, bk), bk), :]
        return jnp.dot(kb, groups[g][2], preferred_element_type=F32)

    s_scr[...] = scores(0, 0)
    p_scr[...] = jnp.zeros(p_scr.shape, BF16)

    def step(j, masked):
        vt = vt_ref[j]
        pv_carried = jnp.dot(vt_ref[jnp.maximum(j - 1, 0)], p_scr[...], preferred_element_type=F32)
        s = s_scr[...]
        for g, (off, c0, _) in enumerate(groups):
            cols = slice(off, off + cg)
            if g < last:
                s_next = scores(j, g + 1)
            elif not masked:
                s_next = scores(j + 1, 0)
            if masked:
                s = jnp.where(key_i <= qry_i + c0, s, MASK_VALUE)
            m_prev = m_scr[:, cols]
            m_new = jnp.maximum(m_prev, jnp.max(s, axis=0, keepdims=True))
            alpha = jnp.exp2(m_prev - m_new)
            p = jnp.exp2(s - m_new)
            l_scr[:, cols] = alpha * l_scr[:, cols] + jnp.sum(p, axis=0, keepdims=True)
            m_scr[:, cols] = m_new
            pb = p.astype(BF16)
            if g < last:
                acc_scr[:, cols] = alpha * acc_scr[:, cols] + jnp.dot(vt, pb, preferred_element_type=F32)
            elif masked:
                acc_scr[:, cols] = alpha * (acc_scr[:, cols] + pv_carried) + jnp.dot(
                    vt, pb, preferred_element_type=F32)
            else:
                acc_scr[:, cols] = alpha * (acc_scr[:, cols] + pv_carried)
                p_scr[...] = pb
            s = s_next
        if not masked:
            s_scr[...] = s

    def body(j, carry):
        step(j, False)
        return carry

    lax.fori_loop(0, i, body, 0)
    step(i, True)

    lq = lam_ref[...]
    lam = (jnp.exp(jnp.sum(lq[0:1] * lq[1:2], axis=1, keepdims=True))
           - jnp.exp(jnp.sum(lq[2:3] * lq[3:4], axis=1, keepdims=True)) + lam_init)
    o = acc_scr[...] / l_scr[...]
    o = o[:, :bq] - lam * o[:, bq:]
    ms = jnp.mean(o * o, axis=0, keepdims=True)
    o = o * lax.rsqrt(ms + EPS) * g_ref[...] * (1.0 - lam_init)
    o_ref[...] = o.T.astype(BF16)


def _diff_attention(qt, k, vt, lam_qk, attn_g, layer):
    S = k.shape[0]
    bq = bk = ATTN_BLOCK
    nk = S // bk
    lam_init = 0.8 - 0.6 * math.exp(-0.3 * layer)
    return pl.pallas_call(
        functools.partial(_attn_kernel, lam_init=lam_init),
        grid=(DA_HEADS, S // bq),
        in_specs=[
            pl.BlockSpec((4, DA_HEAD_DIM), lambda h, i: (0, 0)),
            pl.BlockSpec((DA_VAL_DIM, 1), lambda h, i: (0, 0)),
            pl.BlockSpec((1, LANES, bq), lambda h, i: (i, h, 0)),
            pl.BlockSpec((S, LANES), lambda h, i: (0, h)),
            pl.BlockSpec((nk, DA_VAL_DIM, bk), lambda h, i: (0, h, 0)),
        ],
        out_specs=pl.BlockSpec((bq, DA_VAL_DIM), lambda h, i: (i, h)),
        out_shape=jax.ShapeDtypeStruct((S, DA_WIDTH), BF16),
        scratch_shapes=[
            pltpu.VMEM((1, 2 * bq), F32),
            pltpu.VMEM((1, 2 * bq), F32),
            pltpu.VMEM((DA_VAL_DIM, 2 * bq), F32),
            pltpu.VMEM((bk, ATTN_QUERY_GROUP), F32),
            pltpu.VMEM((bk, ATTN_QUERY_GROUP), BF16),
        ],
        compiler_params=_params(("parallel", "parallel")),
        name="diff_attn",
    )(lam_qk, attn_g.reshape(DA_VAL_DIM, 1), qt, k, vt)


def _hgrn_kernel(lbl_ref, g_ref, hg_ref, o_ref, st_scr, o_scr, *, layer):
    T = hg_ref.shape[0]
    c = HG_CHUNK
    W = HG_WIDTH

    @pl.when(pl.program_id(0) == 0)
    def _():
        st_scr[...] = jnp.zeros(st_scr.shape, F32)

    lg = lbl_ref[...]
    ex = jnp.exp(lg - jnp.max(lg, axis=0, keepdims=True))
    prob = ex / jnp.sum(ex, axis=0, keepdims=True)
    lb = jnp.zeros((1, W), F32)
    for li in range(1, layer + 1):
        lb = lb + prob[li:li + 1]

    same_head = _head_block_mask(W, HG_DIM)
    ones_bd = same_head.astype(BF16)
    rowi = lax.broadcasted_iota(jnp.int32, (c, W), 0)

    def body(s, carry):
        sl = pl.ds(pl.multiple_of(s * c, c), c)
        qf = hg_ref[sl, 0:W]
        z = hg_ref[sl, W:2 * W]
        iv = hg_ref[sl, 2 * W:3 * W]
        f = lb + (1.0 - lb) * jax.nn.sigmoid(z)
        logf = jnp.log(jnp.maximum(f, TINY))
        key = (1.0 - lb) * jax.nn.sigmoid(-z)
        val = _silu(iv)
        b = logf
        sft = 1
        while sft < c:
            b = b + jnp.where(rowi >= sft, pltpu.roll(b, sft, 0), 0.0)
            sft *= 2
        es = [(qf * key).astype(BF16)]
        vrs = [val]
        for d in range(1, c):
            kr = pltpu.roll(key, d, 0)
            br = pltpu.roll(b, d, 0)
            e = jnp.where(rowi >= d, qf * kr * jnp.exp(b - br), 0.0)
            es.append(e.astype(BF16))
            vrs.append(pltpu.roll(val, d, 0))
        estack = jnp.concatenate(es, axis=0)
        r = jnp.dot(estack, ones_bd, preferred_element_type=F32)
        o = r[0:c] * vrs[0]
        for d in range(1, c):
            o = o + r[d * c:(d + 1) * c] * vrs[d]
        st = st_scr[...]
        qd = (qf * jnp.exp(b)).astype(BF16)
        o = o + lax.dot_general(qd, st.astype(BF16), (((1,), (1,)), ((), ())),
                                preferred_element_type=F32)
        o_scr[sl, :] = o
        b_last = b[c - 1:c]
        kd = (key * jnp.exp(b_last - b)).astype(BF16)
        upd = lax.dot_general(val.astype(BF16), kd, (((0,), (0,)), ((), ())),
                              preferred_element_type=F32)
        st_scr[...] = jnp.exp(b_last) * st + jnp.where(same_head, upd, 0.0)
        return carry

    lax.fori_loop(0, T // c, body, 0)

    o = o_scr[...]
    ms = jnp.dot(o * o, same_head.astype(F32), preferred_element_type=F32,
                 precision=HIGHEST) * (1.0 / HG_DIM)
    gate = hg_ref[:, 3 * W:4 * W]
    o_ref[...] = (o * lax.rsqrt(ms + EPS) * g_ref[...] * _silu(gate)).astype(BF16)


def _hgrn2(hg, lb_logits, norm_g, layer):
    S = hg.shape[0]
    T = HG_BLOCK
    g_t = jnp.tile(norm_g.reshape(1, HG_DIM), (1, HG_HEADS))
    return pl.pallas_call(
        functools.partial(_hgrn_kernel, layer=layer),
        grid=(S // T,),
        in_specs=[
            pl.BlockSpec((DEPTH, HG_WIDTH), lambda i: (0, 0)),
            pl.BlockSpec((1, HG_WIDTH), lambda i: (0, 0)),
            pl.BlockSpec((T, 4 * HG_WIDTH), lambda i: (i, 0)),
        ],
        out_specs=pl.BlockSpec((T, HG_WIDTH), lambda i: (i, 0)),
        out_shape=jax.ShapeDtypeStruct((S, HG_WIDTH), BF16),
        scratch_shapes=[
            pltpu.VMEM((HG_WIDTH, HG_WIDTH), F32),
            pltpu.VMEM((T, HG_WIDTH), F32),
        ],
        compiler_params=_params(("arbitrary",)),
        name="hgrn2",
    )(lb_logits, g_t, hg)


def _pool_kernel(u_ref, w_ref, sc_ref, o_ref, halo_scr):
    T = u_ref.shape[0]
    H = POOL_HALO
    blk = pl.program_id(0)

    @pl.when(blk == 0)
    def _():
        halo_scr[...] = jnp.zeros(halo_scr.shape, F32)

    u = u_ref[...]
    ext = jnp.concatenate([halo_scr[...], u], axis=0)
    halo_scr[...] = u[T - H:, :]
    sums = []
    s = ext
    w = 1
    while w < POOL_WINDOWS[-1]:
        s = s + pltpu.roll(s, w, 0)
        w *= 2
        sums.append(s[H:, :])
    t1 = (blk * T + 1 + lax.broadcasted_iota(jnp.int32, (T, POOL_WIDTH), 0)).astype(F32)
    lane = lax.broadcasted_iota(jnp.int32, (T, POOL_WIDTH), 1)
    pooled = None
    for gi, win in enumerate(POOL_WINDOWS):
        mean = sums[gi] / jnp.minimum(t1, float(win))
        pooled = mean if pooled is None else jnp.where(lane >= gi * POOL_GROUP, mean, pooled)
    pooled = pooled - u
    y = jnp.dot(pooled.astype(BF16), w_ref[...], preferred_element_type=F32)
    o_ref[...] = (y * sc_ref[...]).astype(BF16)


def _multiscale_pool(pu, pool_w, pool_scale):
    S = pu.shape[0]
    T = ROW_TILE
    w_bd = jax.scipy.linalg.block_diag(*[pool_w[g] for g in range(len(POOL_WINDOWS))]).astype(BF16)
    return pl.pallas_call(
        _pool_kernel,
        grid=(S // T,),
        in_specs=[
            pl.BlockSpec((T, POOL_WIDTH), lambda i: (i, 0)),
            pl.BlockSpec((POOL_WIDTH, POOL_WIDTH), lambda i: (0, 0)),
            pl.BlockSpec((1, POOL_WIDTH), lambda i: (0, 0)),
        ],
        out_specs=pl.BlockSpec((T, POOL_WIDTH), lambda i: (i, 0)),
        out_shape=jax.ShapeDtypeStruct((S, POOL_WIDTH), BF16),
        scratch_shapes=[pltpu.VMEM((POOL_HALO, POOL_WIDTH), F32)],
        compiler_params=_params(("arbitrary",)),
        name="ms_pool",
    )(pu, w_bd, pool_scale.reshape(1, POOL_WIDTH))


def _outproj_kernel(a_ref, r_ref, p_ref, x_ref, w_ref, g1_ref, lg_ref, lb_ref, o_ref):
    m = jnp.dot(a_ref[...], w_ref[0:DA_WIDTH, :], preferred_element_type=F32)
    m += jnp.dot(r_ref[...], w_ref[DA_WIDTH:DA_WIDTH + HG_WIDTH, :], preferred_element_type=F32)
    m += jnp.dot(p_ref[...], w_ref[DA_WIDTH + HG_WIDTH:, :], preferred_element_type=F32)
    y = ALPHA * x_ref[...] + (1.0 + g1_ref[...]) * m
    o_ref[...] = _layer_norm(y, lg_ref[...], lb_ref[...])


def _out_projection(a, r, p, x, w_out_bf, g1, ln_g, ln_b):
    S = x.shape[0]
    tm = ROW_TILE
    row = lambda i: (i, 0)
    fixed = lambda i: (0, 0)
    vec = pl.BlockSpec((1, D_MODEL), fixed)
    return pl.pallas_call(
        _outproj_kernel,
        grid=(S // tm,),
        in_specs=[
            pl.BlockSpec((tm, DA_WIDTH), row),
            pl.BlockSpec((tm, HG_WIDTH), row),
            pl.BlockSpec((tm, POOL_WIDTH), row),
            pl.BlockSpec((tm, D_MODEL), row),
            pl.BlockSpec((D_MIX, D_MODEL), fixed),
            vec, vec, vec,
        ],
        out_specs=pl.BlockSpec((tm, D_MODEL), row),
        out_shape=jax.ShapeDtypeStruct((S, D_MODEL), F32),
        compiler_params=_params(("parallel",)),
        name="out_proj_ln",
    )(a, r, p, x, w_out_bf, g1, ln_g.reshape(1, D_MODEL), ln_b.reshape(1, D_MODEL))


def _ffn_kernel(x_ref, sc_ref, sh_ref, g2_ref, w1_ref, w3_ref, w2_ref, lg_ref, lb_ref, o_ref):
    x = x_ref[...]
    h = (x * (1.0 + sc_ref[...]) + sh_ref[...]).astype(BF16)
    acc = jnp.zeros(x.shape, F32)
    step = 4 * MXU_DIM
    for lo in range(0, FFN_DIM, step):
        hi = min(lo + step, FFN_DIM)
        a = jnp.dot(h, w1_ref[:, lo:hi], preferred_element_type=F32)
        b = jnp.dot(h, w3_ref[:, lo:hi], preferred_element_type=F32)
        acc += jnp.dot((_silu(a) * b).astype(BF16), w2_ref[lo:hi, :], preferred_element_type=F32)
    y = ALPHA * x + (1.0 + g2_ref[...]) * acc
    o_ref[...] = _layer_norm(y, lg_ref[...], lb_ref[...])


def _dense_ffn(x, sc, sh, g2, w1, w3, w2, ln_g, ln_b):
    S = x.shape[0]
    tm = ROW_TILE
    row = lambda i: (i, 0)
    fixed = lambda i: (0, 0)
    vec = pl.BlockSpec((1, D_MODEL), fixed)
    once = pl.Buffered(1)
    return pl.pallas_call(
        _ffn_kernel,
        grid=(S // tm,),
        in_specs=[
            pl.BlockSpec((tm, D_MODEL), row),
            vec, vec, vec,
            pl.BlockSpec((D_MODEL, FFN_DIM), fixed, pipeline_mode=once),
            pl.BlockSpec((D_MODEL, FFN_DIM), fixed, pipeline_mode=once),
            pl.BlockSpec((FFN_DIM, D_MODEL), fixed, pipeline_mode=once),
            vec, vec,
        ],
        out_specs=pl.BlockSpec((tm, D_MODEL), row),
        out_shape=jax.ShapeDtypeStruct((S, D_MODEL), F32),
        compiler_params=_params(("parallel",)),
        name="dense_ffn_ln",
    )(x, sc, sh, g2, w1, w3, w2, ln_g.reshape(1, D_MODEL), ln_b.reshape(1, D_MODEL))


def _router_kernel(x_ref, sc_ref, sh_ref, rw_ref, h_ref, rank_ref, comb_ref, cum_ref):
    T = x_ref.shape[0]
    tc = MOE_CHUNK
    E = N_EXPERTS
    h = x_ref[...] * (1.0 + sc_ref[...]) + sh_ref[...]
    h_ref[...] = h.astype(BF16)
    logits = lax.dot_general(rw_ref[...], h, (((1,), (1,)), ((), ())),
                             preferred_element_type=F32, precision=HIGHEST)
    eidx = lax.broadcasted_iota(jnp.int32, (E, T), 0)
    v1 = jnp.max(logits, axis=0, keepdims=True)
    i1 = jnp.min(jnp.where(logits == v1, eidx, E), axis=0, keepdims=True)
    m1 = eidx == i1
    rest = jnp.where(m1, -jnp.inf, logits)
    v2 = jnp.max(rest, axis=0, keepdims=True)
    i2 = jnp.min(jnp.where(rest == v2, eidx, E), axis=0, keepdims=True)
    m2 = eidx == i2
    e2 = jnp.exp(v2 - v1)
    g1 = 1.0 / (1.0 + e2)
    g2 = e2 / (1.0 + e2)
    comb_ref[...] = (jnp.where(m1, g1, 0.0) + jnp.where(m2, g2, 0.0)).reshape(E, 1, T)
    routed = jnp.logical_or(m1, m2)
    ind = routed.astype(BF16)
    r_i = lax.broadcasted_iota(jnp.int32, (tc, tc), 0)
    c_i = lax.broadcasted_iota(jnp.int32, (tc, tc), 1)
    strict = (r_i < c_i).astype(BF16)
    lane = lax.broadcasted_iota(jnp.int32, (E, LANES), 1)
    running = jnp.zeros((E, 1), F32)
    cum = jnp.zeros((E, LANES), F32)
    ranks = []
    for ci in range(T // tc):
        ind_c = ind[:, ci * tc:(ci + 1) * tc]
        ranks.append(jnp.dot(ind_c, strict, preferred_element_type=F32) + running)
        cum = jnp.where(lane == ci, running, cum)
        running = running + jnp.sum(ind_c.astype(F32), axis=1, keepdims=True)
    cum = jnp.where(lane == T // tc, running, cum)
    rank = jnp.concatenate(ranks, axis=1)
    rank_ref[...] = jnp.where(routed, rank, -1.0).reshape(E, 1, T)
    cum_ref[0] = cum.astype(jnp.int32)


def _router(x, sc, sh, router_w):
    S = x.shape[0]
    T = MOE_BLOCK
    E = N_EXPERTS
    fixed = lambda i: (0, 0)
    vec = pl.BlockSpec((1, D_MODEL), fixed)
    return pl.pallas_call(
        _router_kernel,
        grid=(S // T,),
        in_specs=[
            pl.BlockSpec((T, D_MODEL), lambda i: (i, 0)),
            vec, vec,
            pl.BlockSpec((E, D_MODEL), fixed),
        ],
        out_specs=[
            pl.BlockSpec((T, D_MODEL), lambda i: (i, 0)),
            pl.BlockSpec((E, 1, T), lambda i: (0, 0, i)),
            pl.BlockSpec((E, 1, T), lambda i: (0, 0, i)),
            pl.BlockSpec((1, E, LANES), lambda i: (i, 0, 0)),
        ],
        out_shape=[
            jax.ShapeDtypeStruct((S, D_MODEL), BF16),
            jax.ShapeDtypeStruct((E, 1, S), F32),
            jax.ShapeDtypeStruct((E, 1, S), F32),
            jax.ShapeDtypeStruct((S // T, E, LANES), jnp.int32),
        ],
        compiler_params=_params(("parallel",)),
        name="moe_router",
    )(x, sc, sh, router_w.T)


def _moe_kernel(cum_ref, h_ref, rank_ref, comb_ref, w1_ref, w3_ref, w2_ref, y_ref,
                xe_scr, gate_scr, xacc_scr, gacc_scr):
    T = h_ref.shape[0]
    tr, tc = MOE_TILE, MOE_CHUNK
    nchunk = T // tc
    b = pl.program_id(0)
    e = pl.program_id(1)
    f = pl.program_id(2)

    @pl.when(jnp.logical_and(e == 0, f == 0))
    def _():
        y_ref[...] = jnp.zeros(y_ref.shape, F32)

    base = (b * N_EXPERTS + e) * LANES
    count = cum_ref[base + nchunk]
    ntiles = lax.shift_right_logical(count + (tr - 1), int(math.log2(tr)))
    row_id = lax.broadcasted_iota(jnp.int32, (tr, tc), 0).astype(F32)

    def tile(r, carry):
        lo = r * tr
        rows = pl.ds(pl.multiple_of(lo, tr), tr)
        want = row_id + lo.astype(F32)

        def chunk_overlaps(ci):
            return jnp.logical_and(cum_ref[base + ci] < lo + tr, cum_ref[base + ci + 1] > lo)

        def onehot(ci):
            return rank_ref[0, :, ci * tc:(ci + 1) * tc] == want

        @pl.when(f == 0)
        def _():
            xacc_scr[...] = jnp.zeros(xacc_scr.shape, F32)
            gacc_scr[...] = jnp.zeros(gacc_scr.shape, F32)
            for ci in range(nchunk):
                @pl.when(chunk_overlaps(ci))
                def _():
                    hit = onehot(ci)
                    xacc_scr[...] += jnp.dot(hit.astype(BF16), h_ref[ci * tc:(ci + 1) * tc, :],
                                             preferred_element_type=F32)
                    gacc_scr[...] += jnp.sum(
                        jnp.where(hit, comb_ref[0, :, ci * tc:(ci + 1) * tc], 0.0),
                        axis=1, keepdims=True)
            xe_scr[rows, :] = xacc_scr[...].astype(BF16)
            gate_scr[rows, :] = gacc_scr[...]

        xe = xe_scr[rows, :]
        a = jnp.dot(xe, w1_ref[0], preferred_element_type=F32)
        g = jnp.dot(xe, w3_ref[0], preferred_element_type=F32)
        mid = (_silu(a) * g).astype(BF16)
        out = jnp.dot(mid, w2_ref[0], preferred_element_type=F32) * gate_scr[rows, :]
        out_bf = out.astype(BF16)
        for ci in range(nchunk):
            @pl.when(chunk_overlaps(ci))
            def _():
                hit = onehot(ci).astype(BF16)
                y_ref[ci * tc:(ci + 1) * tc, :] += lax.dot_general(
                    hit, out_bf, (((0,), (0,)), ((), ())), preferred_element_type=F32)
        return carry

    lax.fori_loop(0, ntiles, tile, 0)


def _moe_ffn(h_bf, rank, comb, cum, w1, w3, w2):
    S = h_bf.shape[0]
    T = MOE_BLOCK
    fw = EXPERT_DIM // MOE_FSPLIT
    grid_spec = pltpu.PrefetchScalarGridSpec(
        num_scalar_prefetch=1,
        grid=(S // T, N_EXPERTS, MOE_FSPLIT),
        in_specs=[
            pl.BlockSpec((T, D_MODEL), lambda b, e, f, c: (b, 0)),
            pl.BlockSpec((1, 1, T), lambda b, e, f, c: (e, 0, b)),
            pl.BlockSpec((1, 1, T), lambda b, e, f, c: (e, 0, b)),
            pl.BlockSpec((1, D_MODEL, fw), lambda b, e, f, c: (e, 0, f)),
            pl.BlockSpec((1, D_MODEL, fw), lambda b, e, f, c: (e, 0, f)),
            pl.BlockSpec((1, fw, D_MODEL), lambda b, e, f, c: (e, f, 0)),
        ],
        out_specs=pl.BlockSpec((T, D_MODEL), lambda b, e, f, c: (b, 0)),
        scratch_shapes=[
            pltpu.VMEM((T, D_MODEL), BF16),
            pltpu.VMEM((T, 1), F32),
            pltpu.VMEM((MOE_TILE, D_MODEL), F32),
            pltpu.VMEM((MOE_TILE, 1), F32),
        ],
    )
    return pl.pallas_call(
        _moe_kernel,
        grid_spec=grid_spec,
        out_shape=jax.ShapeDtypeStruct((S, D_MODEL), F32),
        compiler_params=_params(("arbitrary", "arbitrary", "arbitrary")),
        name="moe_experts",
    )(cum.reshape(-1), h_bf, rank, comb, w1, w3, w2)


def _ln_kernel(x_ref, y_ref, g2_ref, lg_ref, lb_ref, o_ref):
    y = ALPHA * x_ref[...] + (1.0 + g2_ref[...]) * y_ref[...]
    o_ref[...] = _layer_norm(y, lg_ref[...], lb_ref[...])


def _residual_ln(x, y, g2, ln_g, ln_b):
    S = x.shape[0]
    tm = ROW_TILE
    row = lambda i: (i, 0)
    vec = pl.BlockSpec((1, D_MODEL), lambda i: (0, 0))
    return pl.pallas_call(
        _ln_kernel,
        grid=(S // tm,),
        in_specs=[pl.BlockSpec((tm, D_MODEL), row), pl.BlockSpec((tm, D_MODEL), row), vec, vec, vec],
        out_specs=pl.BlockSpec((tm, D_MODEL), row),
        out_shape=jax.ShapeDtypeStruct((S, D_MODEL), F32),
        compiler_params=_params(("parallel",)),
        name="residual_ln",
    )(x, y, g2, ln_g.reshape(1, D_MODEL), ln_b.reshape(1, D_MODEL))


def _rope_tables(S):
    d = DA_HEAD_DIM
    inv_freq = 1.0 / (ROPE_THETA ** (jnp.arange(0, d, 2, dtype=F32) / d))
    ang = jnp.arange(S, dtype=F32)[:, None] * inv_freq[None, :]
    cos = jnp.cos(ang)
    sin = jnp.sin(ang)
    reps = LANES // d
    cos_t = jnp.tile(jnp.concatenate([cos, cos], axis=1), (1, reps))
    sin_t = jnp.tile(jnp.concatenate([-sin, sin], axis=1), (1, reps))
    return cos_t, sin_t


def kernel(x, c, w_ada, b_ada, w_in, lam_qk, attn_norm_g, hg_lb_logits, hg_norm_g, pool_w, pool_scale, w_out, ln1_g, ln1_b, ln2_g, ln2_b, ffn_w1, ffn_w3, ffn_w2, router_w, exp_w1, exp_w3, exp_w2):
    B, S, D = x.shape
    assert B == 1 and D == D_MODEL and S % MOE_BLOCK == 0
    xs = x.reshape(S, D)
    mod = _modulation(c, w_ada, b_ada)
    cos_t, sin_t = _rope_tables(S)
    for l in range(DEPTH):
        sh1, sc1, g1, sh2, sc2, g2 = (mod[l, j] for j in range(6))
        qt, k, vt, hg, pu = _in_projection(xs, sc1, sh1, w_in[l].astype(BF16), cos_t, sin_t)
        a = _diff_attention(qt, k, vt, lam_qk[l], attn_norm_g[l], l)
        r = _hgrn2(hg, hg_lb_logits, hg_norm_g[l], l)
        p = _multiscale_pool(pu, pool_w[l], pool_scale[l])
        xs = _out_projection(a, r, p, xs, w_out[l].astype(BF16), g1, ln1_g[l], ln1_b[l])
        if l % 2 == 0:
            i = l // 2
            xs = _dense_ffn(xs, sc2, sh2, g2, ffn_w1[i].astype(BF16), ffn_w3[i].astype(BF16),
                            ffn_w2[i].astype(BF16), ln2_g[l], ln2_b[l])
        else:
            i = l // 2
            h_bf, rank, comb, cum = _router(xs, sc2, sh2, router_w[i])
            y = _moe_ffn(h_bf, rank, comb, cum, exp_w1[i].astype(BF16), exp_w3[i].astype(BF16),
                         exp_w2[i].astype(BF16))
            xs = _residual_ln(xs, y, g2, ln2_g[l], ln2_b[l])
    return xs.reshape(B, S, D)
```

```python
import functools
import math

import jax
import jax.numpy as jnp
from jax import lax
from jax.experimental import pallas as pl
from jax.experimental.pallas import tpu as pltpu

F32 = jnp.float32
BF16 = jnp.bfloat16
HIGHEST = lax.Precision.HIGHEST

D_MODEL = 1024
DEPTH = 2
DA_HEADS = 4
DA_WIDTH = D_MODEL // 2
DA_HEAD_DIM = DA_WIDTH // (2 * DA_HEADS)
DA_VAL_DIM = 2 * DA_HEAD_DIM
HG_HEADS = 4
HG_WIDTH = D_MODEL // 4
HG_DIM = HG_WIDTH // HG_HEADS
POOL_WINDOWS = (2, 4, 8, 16)
POOL_WIDTH = D_MODEL // 4
POOL_GROUP = POOL_WIDTH // len(POOL_WINDOWS)
D_MIX = DA_WIDTH + HG_WIDTH + POOL_WIDTH
D_IN = 3 * DA_WIDTH + 4 * HG_WIDTH + POOL_WIDTH
ROPE_THETA = 10000.0
MASK_VALUE = -1e30
TINY = 1e-30
FFN_DIM = 11 * D_MODEL // 4
N_EXPERTS = 8
EXPERT_DIM = 7 * D_MODEL // 2
ALPHA = (2 * DEPTH) ** 0.25
EPS = 1e-5
LOG2_E = math.log2(math.e)

LANES = 128
MXU_DIM = 256
VMEM_LIMIT = 60 * 1024 * 1024

ROW_TILE = 512
ATTN_BLOCK = 512
ATTN_QUERY_GROUP = 256
HG_BLOCK = 512
HG_CHUNK = 16
POOL_HALO = 16
MOE_BLOCK = 2048
MOE_TILE = 256
MOE_CHUNK = 256
MOE_FSPLIT = 2


def _params(sem, vmem=VMEM_LIMIT):
    return pltpu.CompilerParams(dimension_semantics=sem, vmem_limit_bytes=vmem)


def _layer_norm(y, g, b):
    mu = jnp.mean(y, axis=-1, keepdims=True)
    d = y - mu
    var = jnp.mean(d * d, axis=-1, keepdims=True)
    return d * lax.rsqrt(var + EPS) * g + b


def _silu(x):
    return x * jax.nn.sigmoid(x)


def _head_block_mask(n, group):
    r = lax.broadcasted_iota(jnp.int32, (n, n), 0) // group
    c = lax.broadcasted_iota(jnp.int32, (n, n), 1) // group
    return r == c


def _mod_kernel(c_ref, w_ref, b_ref, o_ref):
    cond = _silu(c_ref[...])
    cond8 = jnp.broadcast_to(cond, (8, D_MODEL))
    r = jnp.dot(cond8, w_ref[0], preferred_element_type=F32, precision=HIGHEST)
    o_ref[0] = r[0:1] + b_ref[0]


def _modulation(c, w_ada, b_ada):
    n = 6 * DEPTH
    b3 = b_ada.reshape(n, 1, D_MODEL)
    out = pl.pallas_call(
        _mod_kernel,
        grid=(DEPTH, 6),
        in_specs=[
            pl.BlockSpec((1, D_MODEL), lambda l, j: (0, 0)),
            pl.BlockSpec((1, D_MODEL, D_MODEL), lambda l, j: (l, 0, j)),
            pl.BlockSpec((1, 1, D_MODEL), lambda l, j: (l * 6 + j, 0, 0)),
        ],
        out_specs=pl.BlockSpec((1, 1, D_MODEL), lambda l, j: (l * 6 + j, 0, 0)),
        out_shape=jax.ShapeDtypeStruct((n, 1, D_MODEL), F32),
        compiler_params=_params(("parallel", "parallel")),
        name="adaln_mod",
    )(c, w_ada, b3)
    return out.reshape(DEPTH, 6, 1, D_MODEL)


def _inproj_kernel(x_ref, sc_ref, sh_ref, w_ref, wvt_ref, cos_ref, sin_ref,
                   qt_ref, k_ref, vt_ref, hg_ref, pu_ref):
    h = (x_ref[...] * (1.0 + sc_ref[...]) + sh_ref[...]).astype(BF16)
    qk = jnp.dot(h, w_ref[:, : 2 * DA_WIDTH], preferred_element_type=F32)
    reps = 2 * DA_WIDTH // LANES
    cos = jnp.tile(cos_ref[...], (1, reps))
    sin = jnp.tile(sin_ref[...], (1, reps))
    lane = lax.broadcasted_iota(jnp.int32, qk.shape, 1)
    half = DA_HEAD_DIM // 2
    first_half = (lane & half) == 0
    width = 2 * DA_WIDTH
    partner = jnp.where(first_half, pltpu.roll(qk, width - half, 1), pltpu.roll(qk, half, 1))
    rot = qk * cos + partner * sin
    qt_ref[0] = (rot[:, :DA_WIDTH] * (DA_HEAD_DIM ** -0.5 * LOG2_E)).T.astype(BF16)
    k_ref[...] = rot[:, DA_WIDTH:].astype(BF16)
    o = 2 * DA_WIDTH
    vt_ref[0] = lax.dot_general(wvt_ref[...], h, (((1,), (1,)), ((), ())),
                                preferred_element_type=F32).astype(BF16)
    o += DA_WIDTH
    hg_ref[...] = jnp.dot(h, w_ref[:, o:o + 4 * HG_WIDTH], preferred_element_type=F32)
    o += 4 * HG_WIDTH
    pu_ref[...] = jnp.dot(h, w_ref[:, o:o + POOL_WIDTH], preferred_element_type=F32)


def _in_projection(x, sc, sh, w_in_bf, cos_t, sin_t):
    S = x.shape[0]
    tm = ATTN_BLOCK
    row = lambda i: (i, 0)
    fixed = lambda i: (0, 0)
    return pl.pallas_call(
        _inproj_kernel,
        grid=(S // tm,),
        in_specs=[
            pl.BlockSpec((tm, D_MODEL), row),
            pl.BlockSpec((1, D_MODEL), fixed),
            pl.BlockSpec((1, D_MODEL), fixed),
            pl.BlockSpec((D_MODEL, D_IN), fixed),
            pl.BlockSpec((DA_WIDTH, D_MODEL), fixed),
            pl.BlockSpec((tm, LANES), row),
            pl.BlockSpec((tm, LANES), row),
        ],
        out_specs=[
            pl.BlockSpec((1, DA_WIDTH, tm), lambda i: (i, 0, 0)),
            pl.BlockSpec((tm, DA_WIDTH), row),
            pl.BlockSpec((1, DA_WIDTH, tm), lambda i: (i, 0, 0)),
            pl.BlockSpec((tm, 4 * HG_WIDTH), row),
            pl.BlockSpec((tm, POOL_WIDTH), row),
        ],
        out_shape=[
            jax.ShapeDtypeStruct((S // tm, DA_WIDTH, tm), BF16),
            jax.ShapeDtypeStruct((S, DA_WIDTH), BF16),
            jax.ShapeDtypeStruct((S // tm, DA_WIDTH, tm), BF16),
            jax.ShapeDtypeStruct((S, 4 * HG_WIDTH), F32),
            jax.ShapeDtypeStruct((S, POOL_WIDTH), F32),
        ],
        compiler_params=_params(("parallel",)),
        name="in_proj",
    )(x, sc, sh, w_in_bf, w_in_bf[:, 2 * DA_WIDTH:3 * DA_WIDTH].T, cos_t, sin_t)


def _attn_kernel(lam_ref, g_ref, qt_ref, k_ref, vt_ref, o_ref, m_scr, l_scr, a_scr, acc_scr, s_scr, p_scr,
                 *, lam_init):
    bq = qt_ref.shape[2]
    bk = vt_ref.shape[2]
    cg = ATTN_QUERY_GROUP
    i = pl.program_id(1)
    qt = qt_ref[0]
    rowq = lax.broadcasted_iota(jnp.int32, qt.shape, 0)
    zero = jnp.zeros_like(qt)
    maps = (jnp.where(rowq < DA_HEAD_DIM, qt, zero), jnp.where(rowq >= DA_HEAD_DIM, qt, zero))
    groups = [(a * bq + c0, c0, maps[a][:, c0:c0 + cg]) for a in range(2) for c0 in range(0, bq, cg)]
    m_scr[...] = jnp.full(m_scr.shape, -jnp.inf, F32)
    l_scr[...] = jnp.zeros(l_scr.shape, F32)
    acc_scr[...] = jnp.zeros(acc_scr.shape, F32)
    key_i = lax.broadcasted_iota(jnp.int32, (bk, cg), 0)
    qry_i = lax.broadcasted_iota(jnp.int32, (bk, cg), 1)
    ng = len(groups)
    last = ng - 1

    def slot(j, g):
        return g

    def score_stage(j, g):
        kb = k_ref[pl.ds(pl.multiple_of(j * bk, bk), bk), :]
        s_scr[slot(j, g)] = jnp.dot(kb, groups[g][2], preferred_element_type=F32)

    def visible(g, key_offset):
        return key_i + key_offset <= qry_i + groups[g][1]

    def max_stage(j, g, key_offset):
        off = groups[g][0]
        cols = slice(off, off + cg)
        s = s_scr[slot(j, g)]
        if key_offset is not None:
            s = jnp.where(visible(g, key_offset), s, MASK_VALUE)
        m_prev = m_scr[:, cols]
        m_new = jnp.maximum(m_prev, jnp.max(s, axis=0, keepdims=True))
        a_scr[:, cols] = jnp.exp2(m_prev - m_new)
        m_scr[:, cols] = m_new

    def exp_stage(j, g, masked, vt, pv_carried):
        off = groups[g][0]
        cols = slice(off, off + cg)
        s = s_scr[slot(j, g)]
        if masked:
            s = jnp.where(visible(g, 0), s, MASK_VALUE)
        p = jnp.exp2(s - m_scr[:, cols])
        alpha = a_scr[:, cols]
        l_scr[:, cols] = alpha * l_scr[:, cols] + jnp.sum(p, axis=0, keepdims=True)
        pb = p.astype(BF16)
        if g < last and not masked:
            score_stage(j + 1, g)
        if g < last:
            acc_scr[:, cols] = alpha * acc_scr[:, cols] + jnp.dot(vt, pb, preferred_element_type=F32)
        elif masked:
            acc_scr[:, cols] = alpha * (acc_scr[:, cols] + pv_carried) + jnp.dot(
                vt, pb, preferred_element_type=F32)
        else:
            acc_scr[:, cols] = alpha * (acc_scr[:, cols] + pv_carried)
            p_scr[...] = pb

    def step(j, masked):
        vt = vt_ref[j]
        score_stage(j, last)
        pv_carried = jnp.dot(vt_ref[jnp.maximum(j - 1, 0)], p_scr[...], preferred_element_type=F32)
        for g in range(ng):
            if g + 1 < ng:
                max_stage(j, g + 1, 0 if masked else None)
            elif not masked:
                max_stage(j + 1, 0, (j + 1 - i) * bk)
            exp_stage(j, g, masked, vt, pv_carried)

    p_scr[...] = jnp.zeros(p_scr.shape, BF16)
    for g in range(last):
        score_stage(0, g)
    max_stage(0, 0, (0 - i) * bk)

    def body(j, carry):
        step(j, False)
        return carry

    lax.fori_loop(0, i, body, 0)
    step(i, True)

    lq = lam_ref[...]
    lam = (jnp.exp(jnp.sum(lq[0:1] * lq[1:2], axis=1, keepdims=True))
           - jnp.exp(jnp.sum(lq[2:3] * lq[3:4], axis=1, keepdims=True)) + lam_init)
    o = acc_scr[...] / l_scr[...]
    o = o[:, :bq] - lam * o[:, bq:]
    ms = jnp.mean(o * o, axis=0, keepdims=True)
    o = o * lax.rsqrt(ms + EPS) * g_ref[...] * (1.0 - lam_init)
    o_ref[...] = o.T.astype(BF16)


def _diff_attention(qt, k, vt, lam_qk, attn_g, layer):
    S = k.shape[0]
    bq = bk = ATTN_BLOCK
    nk = S // bk
    lam_init = 0.8 - 0.6 * math.exp(-0.3 * layer)
    return pl.pallas_call(
        functools.partial(_attn_kernel, lam_init=lam_init),
        grid=(DA_HEADS, S // bq),
        in_specs=[
            pl.BlockSpec((4, DA_HEAD_DIM), lambda h, i: (0, 0)),
            pl.BlockSpec((DA_VAL_DIM, 1), lambda h, i: (0, 0)),
            pl.BlockSpec((1, LANES, bq), lambda h, i: (i, h, 0)),
            pl.BlockSpec((S, LANES), lambda h, i: (0, h)),
            pl.BlockSpec((nk, DA_VAL_DIM, bk), lambda h, i: (0, h, 0)),
        ],
        out_specs=pl.BlockSpec((bq, DA_VAL_DIM), lambda h, i: (i, h)),
        out_shape=jax.ShapeDtypeStruct((S, DA_WIDTH), BF16),
        scratch_shapes=[
            pltpu.VMEM((1, 2 * bq), F32),
            pltpu.VMEM((1, 2 * bq), F32),
            pltpu.VMEM((1, 2 * bq), F32),
            pltpu.VMEM((DA_VAL_DIM, 2 * bq), F32),
            pltpu.VMEM((2 * bq // ATTN_QUERY_GROUP, bk, ATTN_QUERY_GROUP), F32),
            pltpu.VMEM((bk, ATTN_QUERY_GROUP), BF16),
        ],
        compiler_params=_params(("parallel", "parallel")),
        name="diff_attn",
    )(lam_qk, attn_g.reshape(DA_VAL_DIM, 1), qt, k, vt)


def _hgrn_kernel(lbl_ref, g_ref, hg_ref, o_ref, st_scr, o_scr, *, layer):
    T = hg_ref.shape[0]
    c = HG_CHUNK
    W = HG_WIDTH

    @pl.when(pl.program_id(0) == 0)
    def _():
        st_scr[...] = jnp.zeros(st_scr.shape, F32)

    lg = lbl_ref[...]
    ex = jnp.exp(lg - jnp.max(lg, axis=0, keepdims=True))
    prob = ex / jnp.sum(ex, axis=0, keepdims=True)
    lb = jnp.zeros((1, W), F32)
    for li in range(1, layer + 1):
        lb = lb + prob[li:li + 1]

    same_head = _head_block_mask(W, HG_DIM)
    ones_bd = same_head.astype(BF16)
    rowi = lax.broadcasted_iota(jnp.int32, (c, W), 0)

    def body(s, carry):
        sl = pl.ds(pl.multiple_of(s * c, c), c)
        qf = hg_ref[sl, 0:W]
        z = hg_ref[sl, W:2 * W]
        iv = hg_ref[sl, 2 * W:3 * W]
        f = lb + (1.0 - lb) * jax.nn.sigmoid(z)
        logf = jnp.log(jnp.maximum(f, TINY))
        key = (1.0 - lb) * jax.nn.sigmoid(-z)
        val = _silu(iv)
        b = logf
        sft = 1
        while sft < c:
            b = b + jnp.where(rowi >= sft, pltpu.roll(b, sft, 0), 0.0)
            sft *= 2
        es = [(qf * key).astype(BF16)]
        vrs = [val]
        for d in range(1, c):
            kr = pltpu.roll(key, d, 0)
            br = pltpu.roll(b, d, 0)
            e = jnp.where(rowi >= d, qf * kr * jnp.exp(b - br), 0.0)
            es.append(e.astype(BF16))
            vrs.append(pltpu.roll(val, d, 0))
        estack = jnp.concatenate(es, axis=0)
        r = jnp.dot(estack, ones_bd, preferred_element_type=F32)
        o = r[0:c] * vrs[0]
        for d in range(1, c):
            o = o + r[d * c:(d + 1) * c] * vrs[d]
        st = st_scr[...]
        qd = (qf * jnp.exp(b)).astype(BF16)
        o = o + lax.dot_general(qd, st.astype(BF16), (((1,), (1,)), ((), ())),
                                preferred_element_type=F32)
        o_scr[sl, :] = o
        b_last = b[c - 1:c]
        kd = (key * jnp.exp(b_last - b)).astype(BF16)
        upd = lax.dot_general(val.astype(BF16), kd, (((0,), (0,)), ((), ())),
                              preferred_element_type=F32)
        st_scr[...] = jnp.exp(b_last) * st + jnp.where(same_head, upd, 0.0)
        return carry

    lax.fori_loop(0, T // c, body, 0)

    o = o_scr[...]
    ms = jnp.dot(o * o, same_head.astype(F32), preferred_element_type=F32,
                 precision=HIGHEST) * (1.0 / HG_DIM)
    gate = hg_ref[:, 3 * W:4 * W]
    o_ref[...] = (o * lax.rsqrt(ms + EPS) * g_ref[...] * _silu(gate)).astype(BF16)


def _hgrn2(hg, lb_logits, norm_g, layer):
    S = hg.shape[0]
    T = HG_BLOCK
    g_t = jnp.tile(norm_g.reshape(1, HG_DIM), (1, HG_HEADS))
    return pl.pallas_call(
        functools.partial(_hgrn_kernel, layer=layer),
        grid=(S // T,),
        in_specs=[
            pl.BlockSpec((DEPTH, HG_WIDTH), lambda i: (0, 0)),
            pl.BlockSpec((1, HG_WIDTH), lambda i: (0, 0)),
            pl.BlockSpec((T, 4 * HG_WIDTH), lambda i: (i, 0)),
        ],
        out_specs=pl.BlockSpec((T, HG_WIDTH), lambda i: (i, 0)),
        out_shape=jax.ShapeDtypeStruct((S, HG_WIDTH), BF16),
        scratch_shapes=[
            pltpu.VMEM((HG_WIDTH, HG_WIDTH), F32),
            pltpu.VMEM((T, HG_WIDTH), F32),
        ],
        compiler_params=_params(("arbitrary",)),
        name="hgrn2",
    )(lb_logits, g_t, hg)


def _pool_kernel(u_ref, w_ref, sc_ref, o_ref, halo_scr):
    T = u_ref.shape[0]
    H = POOL_HALO
    blk = pl.program_id(0)

    @pl.when(blk == 0)
    def _():
        halo_scr[...] = jnp.zeros(halo_scr.shape, F32)

    u = u_ref[...]
    ext = jnp.concatenate([halo_scr[...], u], axis=0)
    halo_scr[...] = u[T - H:, :]
    sums = []
    s = ext
    w = 1
    while w < POOL_WINDOWS[-1]:
        s = s + pltpu.roll(s, w, 0)
        w *= 2
        sums.append(s[H:, :])
    t1 = (blk * T + 1 + lax.broadcasted_iota(jnp.int32, (T, POOL_WIDTH), 0)).astype(F32)
    lane = lax.broadcasted_iota(jnp.int32, (T, POOL_WIDTH), 1)
    pooled = None
    for gi, win in enumerate(POOL_WINDOWS):
        mean = sums[gi] / jnp.minimum(t1, float(win))
        pooled = mean if pooled is None else jnp.where(lane >= gi * POOL_GROUP, mean, pooled)
    pooled = pooled - u
    y = jnp.dot(pooled.astype(BF16), w_ref[...], preferred_element_type=F32)
    o_ref[...] = (y * sc_ref[...]).astype(BF16)


def _multiscale_pool(pu, pool_w, pool_scale):
    S = pu.shape[0]
    T = ROW_TILE
    w_bd = jax.scipy.linalg.block_diag(*[pool_w[g] for g in range(len(POOL_WINDOWS))]).astype(BF16)
    return pl.pallas_call(
        _pool_kernel,
        grid=(S // T,),
        in_specs=[
            pl.BlockSpec((T, POOL_WIDTH), lambda i: (i, 0)),
            pl.BlockSpec((POOL_WIDTH, POOL_WIDTH), lambda i: (0, 0)),
            pl.BlockSpec((1, POOL_WIDTH), lambda i: (0, 0)),
        ],
        out_specs=pl.BlockSpec((T, POOL_WIDTH), lambda i: (i, 0)),
        out_shape=jax.ShapeDtypeStruct((S, POOL_WIDTH), BF16),
        scratch_shapes=[pltpu.VMEM((POOL_HALO, POOL_WIDTH), F32)],
        compiler_params=_params(("arbitrary",)),
        name="ms_pool",
    )(pu, w_bd, pool_scale.reshape(1, POOL_WIDTH))


def _outproj_kernel(a_ref, r_ref, p_ref, x_ref, w_ref, g1_ref, lg_ref, lb_ref, o_ref):
    m = jnp.dot(a_ref[...], w_ref[0:DA_WIDTH, :], preferred_element_type=F32)
    m += jnp.dot(r_ref[...], w_ref[DA_WIDTH:DA_WIDTH + HG_WIDTH, :], preferred_element_type=F32)
    m += jnp.dot(p_ref[...], w_ref[DA_WIDTH + HG_WIDTH:, :], preferred_element_type=F32)
    y = ALPHA * x_ref[...] + (1.0 + g1_ref[...]) * m
    o_ref[...] = _layer_norm(y, lg_ref[...], lb_ref[...])


def _out_projection(a, r, p, x, w_out_bf, g1, ln_g, ln_b):
    S = x.shape[0]
    tm = ROW_TILE
    row = lambda i: (i, 0)
    fixed = lambda i: (0, 0)
    vec = pl.BlockSpec((1, D_MODEL), fixed)
    return pl.pallas_call(
        _outproj_kernel,
        grid=(S // tm,),
        in_specs=[
            pl.BlockSpec((tm, DA_WIDTH), row),
            pl.BlockSpec((tm, HG_WIDTH), row),
            pl.BlockSpec((tm, POOL_WIDTH), row),
            pl.BlockSpec((tm, D_MODEL), row),
            pl.BlockSpec((D_MIX, D_MODEL), fixed),
            vec, vec, vec,
        ],
        out_specs=pl.BlockSpec((tm, D_MODEL), row),
        out_shape=jax.ShapeDtypeStruct((S, D_MODEL), F32),
        compiler_params=_params(("parallel",)),
        name="out_proj_ln",
    )(a, r, p, x, w_out_bf, g1, ln_g.reshape(1, D_MODEL), ln_b.reshape(1, D_MODEL))


def _ffn_kernel(x_ref, sc_ref, sh_ref, g2_ref, w1_ref, w3_ref, w2_ref, lg_ref, lb_ref, o_ref):
    x = x_ref[...]
    h = (x * (1.0 + sc_ref[...]) + sh_ref[...]).astype(BF16)
    acc = jnp.zeros(x.shape, F32)
    step = 4 * MXU_DIM
    for lo in range(0, FFN_DIM, step):
        hi = min(lo + step, FFN_DIM)
        a = jnp.dot(h, w1_ref[:, lo:hi], preferred_element_type=F32)
        b = jnp.dot(h, w3_ref[:, lo:hi], preferred_element_type=F32)
        acc += jnp.dot((_silu(a) * b).astype(BF16), w2_ref[lo:hi, :], preferred_element_type=F32)
    y = ALPHA * x + (1.0 + g2_ref[...]) * acc
    o_ref[...] = _layer_norm(y, lg_ref[...], lb_ref[...])


def _dense_ffn(x, sc, sh, g2, w1, w3, w2, ln_g, ln_b):
    S = x.shape[0]
    tm = ROW_TILE
    row = lambda i: (i, 0)
    fixed = lambda i: (0, 0)
    vec = pl.BlockSpec((1, D_MODEL), fixed)
    once = pl.Buffered(1)
    return pl.pallas_call(
        _ffn_kernel,
        grid=(S // tm,),
        in_specs=[
            pl.BlockSpec((tm, D_MODEL), row),
            vec, vec, vec,
            pl.BlockSpec((D_MODEL, FFN_DIM), fixed, pipeline_mode=once),
            pl.BlockSpec((D_MODEL, FFN_DIM), fixed, pipeline_mode=once),
            pl.BlockSpec((FFN_DIM, D_MODEL), fixed, pipeline_mode=once),
            vec, vec,
        ],
        out_specs=pl.BlockSpec((tm, D_MODEL), row),
        out_shape=jax.ShapeDtypeStruct((S, D_MODEL), F32),
        compiler_params=_params(("parallel",)),
        name="dense_ffn_ln",
    )(x, sc, sh, g2, w1, w3, w2, ln_g.reshape(1, D_MODEL), ln_b.reshape(1, D_MODEL))


def _router_kernel(x_ref, sc_ref, sh_ref, rw_ref, h_ref, rank_ref, comb_ref, cum_ref):
    T = x_ref.shape[0]
    tc = MOE_CHUNK
    E = N_EXPERTS
    h = x_ref[...] * (1.0 + sc_ref[...]) + sh_ref[...]
    h_ref[...] = h.astype(BF16)
    logits = lax.dot_general(rw_ref[...], h, (((1,), (1,)), ((), ())),
                             preferred_element_type=F32, precision=HIGHEST)
    eidx = lax.broadcasted_iota(jnp.int32, (E, T), 0)
    v1 = jnp.max(logits, axis=0, keepdims=True)
    i1 = jnp.min(jnp.where(logits == v1, eidx, E), axis=0, keepdims=True)
    m1 = eidx == i1
    rest = jnp.where(m1, -jnp.inf, logits)
    v2 = jnp.max(rest, axis=0, keepdims=True)
    i2 = jnp.min(jnp.where(rest == v2, eidx, E), axis=0, keepdims=True)
    m2 = eidx == i2
    e2 = jnp.exp(v2 - v1)
    g1 = 1.0 / (1.0 + e2)
    g2 = e2 / (1.0 + e2)
    comb_ref[...] = (jnp.where(m1, g1, 0.0) + jnp.where(m2, g2, 0.0)).reshape(E, 1, T)
    routed = jnp.logical_or(m1, m2)
    ind = routed.astype(BF16)
    r_i = lax.broadcasted_iota(jnp.int32, (tc, tc), 0)
    c_i = lax.broadcasted_iota(jnp.int32, (tc, tc), 1)
    strict = (r_i < c_i).astype(BF16)
    lane = lax.broadcasted_iota(jnp.int32, (E, LANES), 1)
    running = jnp.zeros((E, 1), F32)
    cum = jnp.zeros((E, LANES), F32)
    ranks = []
    for ci in range(T // tc):
        ind_c = ind[:, ci * tc:(ci + 1) * tc]
        ranks.append(jnp.dot(ind_c, strict, preferred_element_type=F32) + running)
        cum = jnp.where(lane == ci, running, cum)
        running = running + jnp.sum(ind_c.astype(F32), axis=1, keepdims=True)
    cum = jnp.where(lane == T // tc, running, cum)
    rank = jnp.concatenate(ranks, axis=1)
    rank_ref[...] = jnp.where(routed, rank, -1.0).reshape(E, 1, T)
    cum_ref[0] = cum.astype(jnp.int32)


def _router(x, sc, sh, router_w):
    S = x.shape[0]
    T = MOE_BLOCK
    E = N_EXPERTS
    fixed = lambda i: (0, 0)
    vec = pl.BlockSpec((1, D_MODEL), fixed)
    return pl.pallas_call(
        _router_kernel,
        grid=(S // T,),
        in_specs=[
            pl.BlockSpec((T, D_MODEL), lambda i: (i, 0)),
            vec, vec,
            pl.BlockSpec((E, D_MODEL), fixed),
        ],
        out_specs=[
            pl.BlockSpec((T, D_MODEL), lambda i: (i, 0)),
            pl.BlockSpec((E, 1, T), lambda i: (0, 0, i)),
            pl.BlockSpec((E, 1, T), lambda i: (0, 0, i)),
            pl.BlockSpec((1, E, LANES), lambda i: (i, 0, 0)),
        ],
        out_shape=[
            jax.ShapeDtypeStruct((S, D_MODEL), BF16),
            jax.ShapeDtypeStruct((E, 1, S), F32),
            jax.ShapeDtypeStruct((E, 1, S), F32),
            jax.ShapeDtypeStruct((S // T, E, LANES), jnp.int32),
        ],
        compiler_params=_params(("parallel",)),
        name="moe_router",
    )(x, sc, sh, router_w.T)


def _moe_kernel(cum_ref, h_ref, rank_ref, comb_ref, w1_ref, w3_ref, w2_ref, y_ref,
                xe_scr, gate_scr, xacc_scr, gacc_scr):
    T = h_ref.shape[0]
    tr, tc = MOE_TILE, MOE_CHUNK
    nchunk = T // tc
    b = pl.program_id(0)
    e = pl.program_id(1)
    f = pl.program_id(2)

    @pl.when(jnp.logical_and(e == 0, f == 0))
    def _():
        y_ref[...] = jnp.zeros(y_ref.shape, F32)

    base = (b * N_EXPERTS + e) * LANES
    count = cum_ref[base + nchunk]
    ntiles = lax.shift_right_logical(count + (tr - 1), int(math.log2(tr)))
    row_id = lax.broadcasted_iota(jnp.int32, (tr, tc), 0).astype(F32)

    def tile(r, carry):
        lo = r * tr
        rows = pl.ds(pl.multiple_of(lo, tr), tr)
        want = row_id + lo.astype(F32)

        def chunk_overlaps(ci):
            return jnp.logical_and(cum_ref[base + ci] < lo + tr, cum_ref[base + ci + 1] > lo)

        def onehot(ci):
            return rank_ref[0, :, ci * tc:(ci + 1) * tc] == want

        @pl.when(f == 0)
        def _():
            xacc_scr[...] = jnp.zeros(xacc_scr.shape, F32)
            gacc_scr[...] = jnp.zeros(gacc_scr.shape, F32)
            for ci in range(nchunk):
                @pl.when(chunk_overlaps(ci))
                def _():
                    hit = onehot(ci)
                    xacc_scr[...] += jnp.dot(hit.astype(BF16), h_ref[ci * tc:(ci + 1) * tc, :],
                                             preferred_element_type=F32)
                    gacc_scr[...] += jnp.sum(
                        jnp.where(hit, comb_ref[0, :, ci * tc:(ci + 1) * tc], 0.0),
                        axis=1, keepdims=True)
            xe_scr[rows, :] = xacc_scr[...].astype(BF16)
            gate_scr[rows, :] = gacc_scr[...]

        xe = xe_scr[rows, :]
        a = jnp.dot(xe, w1_ref[0], preferred_element_type=F32)
        g = jnp.dot(xe, w3_ref[0], preferred_element_type=F32)
        mid = (_silu(a) * g).astype(BF16)
        out = jnp.dot(mid, w2_ref[0], preferred_element_type=F32) * gate_scr[rows, :]
        out_bf = out.astype(BF16)
        for ci in range(nchunk):
            @pl.when(chunk_overlaps(ci))
            def _():
                hit = onehot(ci).astype(BF16)
                y_ref[ci * tc:(ci + 1) * tc, :] += lax.dot_general(
                    hit, out_bf, (((0,), (0,)), ((), ())), preferred_element_type=F32)
        return carry

    lax.fori_loop(0, ntiles, tile, 0)


def _moe_ffn(h_bf, rank, comb, cum, w1, w3, w2):
    S = h_bf.shape[0]
    T = MOE_BLOCK
    fw = EXPERT_DIM // MOE_FSPLIT
    grid_spec = pltpu.PrefetchScalarGridSpec(
        num_scalar_prefetch=1,
        grid=(S // T, N_EXPERTS, MOE_FSPLIT),
        in_specs=[
            pl.BlockSpec((T, D_MODEL), lambda b, e, f, c: (b, 0)),
            pl.BlockSpec((1, 1, T), lambda b, e, f, c: (e, 0, b)),
            pl.BlockSpec((1, 1, T), lambda b, e, f, c: (e, 0, b)),
            pl.BlockSpec((1, D_MODEL, fw), lambda b, e, f, c: (e, 0, f)),
            pl.BlockSpec((1, D_MODEL, fw), lambda b, e, f, c: (e, 0, f)),
            pl.BlockSpec((1, fw, D_MODEL), lambda b, e, f, c: (e, f, 0)),
        ],
        out_specs=pl.BlockSpec((T, D_MODEL), lambda b, e, f, c: (b, 0)),
        scratch_shapes=[
            pltpu.VMEM((T, D_MODEL), BF16),
            pltpu.VMEM((T, 1), F32),
            pltpu.VMEM((MOE_TILE, D_MODEL), F32),
            pltpu.VMEM((MOE_TILE, 1), F32),
        ],
    )
    return pl.pallas_call(
        _moe_kernel,
        grid_spec=grid_spec,
        out_shape=jax.ShapeDtypeStruct((S, D_MODEL), F32),
        compiler_params=_params(("arbitrary", "arbitrary", "arbitrary")),
        name="moe_experts",
    )(cum.reshape(-1), h_bf, rank, comb, w1, w3, w2)


def _ln_kernel(x_ref, y_ref, g2_ref, lg_ref, lb_ref, o_ref):
    y = ALPHA * x_ref[...] + (1.0 + g2_ref[...]) * y_ref[...]
    o_ref[...] = _layer_norm(y, lg_ref[...], lb_ref[...])


def _residual_ln(x, y, g2, ln_g, ln_b):
    S = x.shape[0]
    tm = ROW_TILE
    row = lambda i: (i, 0)
    vec = pl.BlockSpec((1, D_MODEL), lambda i: (0, 0))
    return pl.pallas_call(
        _ln_kernel,
        grid=(S // tm,),
        in_specs=[pl.BlockSpec((tm, D_MODEL), row), pl.BlockSpec((tm, D_MODEL), row), vec, vec, vec],
        out_specs=pl.BlockSpec((tm, D_MODEL), row),
        out_shape=jax.ShapeDtypeStruct((S, D_MODEL), F32),
        compiler_params=_params(("parallel",)),
        name="residual_ln",
    )(x, y, g2, ln_g.reshape(1, D_MODEL), ln_b.reshape(1, D_MODEL))


def _rope_tables(S):
    d = DA_HEAD_DIM
    inv_freq = 1.0 / (ROPE_THETA ** (jnp.arange(0, d, 2, dtype=F32) / d))
    ang = jnp.arange(S, dtype=F32)[:, None] * inv_freq[None, :]
    cos = jnp.cos(ang)
    sin = jnp.sin(ang)
    reps = LANES // d
    cos_t = jnp.tile(jnp.concatenate([cos, cos], axis=1), (1, reps))
    sin_t = jnp.tile(jnp.concatenate([-sin, sin], axis=1), (1, reps))
    return cos_t, sin_t


def kernel(x, c, w_ada, b_ada, w_in, lam_qk, attn_norm_g, hg_lb_logits, hg_norm_g, pool_w, pool_scale, w_out, ln1_g, ln1_b, ln2_g, ln2_b, ffn_w1, ffn_w3, ffn_w2, router_w, exp_w1, exp_w3, exp_w2):
    B, S, D = x.shape
    assert B == 1 and D == D_MODEL and S % MOE_BLOCK == 0
    xs = x.reshape(S, D)
    mod = _modulation(c, w_ada, b_ada)
    cos_t, sin_t = _rope_tables(S)
    for l in range(DEPTH):
        sh1, sc1, g1, sh2, sc2, g2 = (mod[l, j] for j in range(6))
        qt, k, vt, hg, pu = _in_projection(xs, sc1, sh1, w_in[l].astype(BF16), cos_t, sin_t)
        a = _diff_attention(qt, k, vt, lam_qk[l], attn_norm_g[l], l)
        r = _hgrn2(hg, hg_lb_logits, hg_norm_g[l], l)
        p = _multiscale_pool(pu, pool_w[l], pool_scale[l])
        xs = _out_projection(a, r, p, xs, w_out[l].astype(BF16), g1, ln1_g[l], ln1_b[l])
        if l % 2 == 0:
            i = l // 2
            xs = _dense_ffn(xs, sc2, sh2, g2, ffn_w1[i].astype(BF16), ffn_w3[i].astype(BF16),
                            ffn_w2[i].astype(BF16), ln2_g[l], ln2_b[l])
        else:
            i = l // 2
            h_bf, rank, comb, cum = _router(xs, sc2, sh2, router_w[i])
            y = _moe_ffn(h_bf, rank, comb, cum, exp_w1[i].astype(BF16), exp_w3[i].astype(BF16),
                         exp_w2[i].astype(BF16))
            xs = _residual_ln(xs, y, g2, ln2_g[l], ln2_b[l])
    return xs.reshape(B, S, D)
```

```python
import functools
import math

import jax
import jax.numpy as jnp
from jax import lax
from jax.experimental import pallas as pl
from jax.experimental.pallas import tpu as pltpu

F32 = jnp.float32
BF16 = jnp.bfloat16
HIGHEST = lax.Precision.HIGHEST

D_MODEL = 1024
DEPTH = 2
DA_HEADS = 4
DA_WIDTH = D_MODEL // 2
DA_HEAD_DIM = DA_WIDTH // (2 * DA_HEADS)
DA_VAL_DIM = 2 * DA_HEAD_DIM
HG_HEADS = 4
HG_WIDTH = D_MODEL // 4
HG_DIM = HG_WIDTH // HG_HEADS
POOL_WINDOWS = (2, 4, 8, 16)
POOL_WIDTH = D_MODEL // 4
POOL_GROUP = POOL_WIDTH // len(POOL_WINDOWS)
D_MIX = DA_WIDTH + HG_WIDTH + POOL_WIDTH
D_IN = 3 * DA_WIDTH + 4 * HG_WIDTH + POOL_WIDTH
ROPE_THETA = 10000.0
MASK_VALUE = -1e30
TINY = 1e-30
FFN_DIM = 11 * D_MODEL // 4
N_EXPERTS = 8
EXPERT_DIM = 7 * D_MODEL // 2
ALPHA = (2 * DEPTH) ** 0.25
EPS = 1e-5
LOG2_E = math.log2(math.e)

LANES = 128
MXU_DIM = 256
VMEM_LIMIT = 60 * 1024 * 1024

ROW_TILE = 512
ATTN_BLOCK = 512
ATTN_QUERY_GROUP = 256
HG_BLOCK = 512
HG_CHUNK = 16
POOL_HALO = 16
MOE_BLOCK = 2048
MOE_TILE = 192
MOE_CHUNK = 256
MOE_WINDOW = 5
MOE_FSPLIT = 2


def _params(sem, vmem=VMEM_LIMIT):
    return pltpu.CompilerParams(dimension_semantics=sem, vmem_limit_bytes=vmem)


def _layer_norm(y, g, b):
    mu = jnp.mean(y, axis=-1, keepdims=True)
    d = y - mu
    var = jnp.mean(d * d, axis=-1, keepdims=True)
    return d * lax.rsqrt(var + EPS) * g + b


def _silu(x):
    return x * jax.nn.sigmoid(x)


def _head_block_mask(n, group):
    r = lax.broadcasted_iota(jnp.int32, (n, n), 0) // group
    c = lax.broadcasted_iota(jnp.int32, (n, n), 1) // group
    return r == c


def _mod_kernel(c_ref, w_ref, b_ref, o_ref):
    cond = _silu(c_ref[...])
    cond8 = jnp.broadcast_to(cond, (8, D_MODEL))
    r = jnp.dot(cond8, w_ref[0], preferred_element_type=F32, precision=HIGHEST)
    o_ref[0] = r[0:1] + b_ref[0]


def _modulation(c, w_ada, b_ada):
    n = 6 * DEPTH
    b3 = b_ada.reshape(n, 1, D_MODEL)
    out = pl.pallas_call(
        _mod_kernel,
        grid=(DEPTH, 6),
        in_specs=[
            pl.BlockSpec((1, D_MODEL), lambda l, j: (0, 0)),
            pl.BlockSpec((1, D_MODEL, D_MODEL), lambda l, j: (l, 0, j)),
            pl.BlockSpec((1, 1, D_MODEL), lambda l, j: (l * 6 + j, 0, 0)),
        ],
        out_specs=pl.BlockSpec((1, 1, D_MODEL), lambda l, j: (l * 6 + j, 0, 0)),
        out_shape=jax.ShapeDtypeStruct((n, 1, D_MODEL), F32),
        compiler_params=_params(("parallel", "parallel")),
        name="adaln_mod",
    )(c, w_ada, b3)
    return out.reshape(DEPTH, 6, 1, D_MODEL)


def _inproj_kernel(x_ref, sc_ref, sh_ref, w_ref, wvt_ref, cos_ref, sin_ref,
                   qt_ref, k_ref, vt_ref, hg_ref, pu_ref):
    h = (x_ref[...] * (1.0 + sc_ref[...]) + sh_ref[...]).astype(BF16)
    qk = jnp.dot(h, w_ref[:, : 2 * DA_WIDTH], preferred_element_type=F32)
    reps = 2 * DA_WIDTH // LANES
    cos = jnp.tile(cos_ref[...], (1, reps))
    sin = jnp.tile(sin_ref[...], (1, reps))
    lane = lax.broadcasted_iota(jnp.int32, qk.shape, 1)
    half = DA_HEAD_DIM // 2
    first_half = (lane & half) == 0
    width = 2 * DA_WIDTH
    partner = jnp.where(first_half, pltpu.roll(qk, width - half, 1), pltpu.roll(qk, half, 1))
    rot = qk * cos + partner * sin
    qt_ref[0] = (rot[:, :DA_WIDTH] * (DA_HEAD_DIM ** -0.5 * LOG2_E)).T.astype(BF16)
    k_ref[...] = rot[:, DA_WIDTH:].astype(BF16)
    o = 2 * DA_WIDTH
    vt_ref[0] = lax.dot_general(wvt_ref[...], h, (((1,), (1,)), ((), ())),
                                preferred_element_type=F32).astype(BF16)
    o += DA_WIDTH
    hg_ref[...] = jnp.dot(h, w_ref[:, o:o + 4 * HG_WIDTH], preferred_element_type=F32)
    o += 4 * HG_WIDTH
    pu_ref[...] = jnp.dot(h, w_ref[:, o:o + POOL_WIDTH], preferred_element_type=F32)


def _in_projection(x, sc, sh, w_in_bf, cos_t, sin_t):
    S = x.shape[0]
    tm = ATTN_BLOCK
    row = lambda i: (i, 0)
    fixed = lambda i: (0, 0)
    return pl.pallas_call(
        _inproj_kernel,
        grid=(S // tm,),
        in_specs=[
            pl.BlockSpec((tm, D_MODEL), row),
            pl.BlockSpec((1, D_MODEL), fixed),
            pl.BlockSpec((1, D_MODEL), fixed),
            pl.BlockSpec((D_MODEL, D_IN), fixed),
            pl.BlockSpec((DA_WIDTH, D_MODEL), fixed),
            pl.BlockSpec((tm, LANES), row),
            pl.BlockSpec((tm, LANES), row),
        ],
        out_specs=[
            pl.BlockSpec((1, DA_WIDTH, tm), lambda i: (i, 0, 0)),
            pl.BlockSpec((tm, DA_WIDTH), row),
            pl.BlockSpec((1, DA_WIDTH, tm), lambda i: (i, 0, 0)),
            pl.BlockSpec((tm, 4 * HG_WIDTH), row),
            pl.BlockSpec((tm, POOL_WIDTH), row),
        ],
        out_shape=[
            jax.ShapeDtypeStruct((S // tm, DA_WIDTH, tm), BF16),
            jax.ShapeDtypeStruct((S, DA_WIDTH), BF16),
            jax.ShapeDtypeStruct((S // tm, DA_WIDTH, tm), BF16),
            jax.ShapeDtypeStruct((S, 4 * HG_WIDTH), F32),
            jax.ShapeDtypeStruct((S, POOL_WIDTH), F32),
        ],
        compiler_params=_params(("parallel",)),
        name="in_proj",
    )(x, sc, sh, w_in_bf, w_in_bf[:, 2 * DA_WIDTH:3 * DA_WIDTH].T, cos_t, sin_t)


def _attn_kernel(lam_ref, g_ref, qt_ref, k_ref, vt_ref, o_ref, m_scr, l_scr, a_scr, acc_scr, s_scr, p_scr,
                 *, lam_init):
    bq = qt_ref.shape[2]
    bk = vt_ref.shape[2]
    cg = ATTN_QUERY_GROUP
    i = pl.program_id(1)
    qt = qt_ref[0]
    rowq = lax.broadcasted_iota(jnp.int32, qt.shape, 0)
    zero = jnp.zeros_like(qt)
    maps = (jnp.where(rowq < DA_HEAD_DIM, qt, zero), jnp.where(rowq >= DA_HEAD_DIM, qt, zero))
    groups = [(a * bq + c0, c0, maps[a][:, c0:c0 + cg]) for a in range(2) for c0 in range(0, bq, cg)]
    m_scr[...] = jnp.full(m_scr.shape, -jnp.inf, F32)
    l_scr[...] = jnp.zeros(l_scr.shape, F32)
    acc_scr[...] = jnp.zeros(acc_scr.shape, F32)
    key_i = lax.broadcasted_iota(jnp.int32, (bk, cg), 0)
    qry_i = lax.broadcasted_iota(jnp.int32, (bk, cg), 1)
    ng = len(groups)
    last = ng - 1

    def slot(j, g):
        return g

    def score_stage(j, g):
        kb = k_ref[pl.ds(pl.multiple_of(j * bk, bk), bk), :]
        s_scr[slot(j, g)] = jnp.dot(kb, groups[g][2], preferred_element_type=F32)

    def visible(g, key_offset):
        return key_i + key_offset <= qry_i + groups[g][1]

    def max_stage(j, g, key_offset):
        off = groups[g][0]
        cols = slice(off, off + cg)
        s = s_scr[slot(j, g)]
        if key_offset is not None:
            s = jnp.where(visible(g, key_offset), s, MASK_VALUE)
        m_prev = m_scr[:, cols]
        m_new = jnp.maximum(m_prev, jnp.max(s, axis=0, keepdims=True))
        a_scr[:, cols] = jnp.exp2(m_prev - m_new)
        m_scr[:, cols] = m_new

    def exp_stage(j, g, masked, vt, pv_carried):
        off = groups[g][0]
        cols = slice(off, off + cg)
        s = s_scr[slot(j, g)]
        if masked:
            s = jnp.where(visible(g, 0), s, MASK_VALUE)
        p = jnp.exp2(s - m_scr[:, cols])
        alpha = a_scr[:, cols]
        l_scr[:, cols] = alpha * l_scr[:, cols] + jnp.sum(p, axis=0, keepdims=True)
        pb = p.astype(BF16)
        if g < last and not masked:
            score_stage(j + 1, g)
        if g < last:
            acc_scr[:, cols] = alpha * acc_scr[:, cols] + jnp.dot(vt, pb, preferred_element_type=F32)
        elif masked:
            acc_scr[:, cols] = alpha * (acc_scr[:, cols] + pv_carried) + jnp.dot(
                vt, pb, preferred_element_type=F32)
        else:
            acc_scr[:, cols] = alpha * (acc_scr[:, cols] + pv_carried)
            p_scr[...] = pb

    def step(j, masked):
        vt = vt_ref[j]
        score_stage(j, last)
        pv_carried = jnp.dot(vt_ref[jnp.maximum(j - 1, 0)], p_scr[...], preferred_element_type=F32)
        for g in range(ng):
            if g + 1 < ng:
                max_stage(j, g + 1, 0 if masked else None)
            elif not masked:
                max_stage(j + 1, 0, (j + 1 - i) * bk)
            exp_stage(j, g, masked, vt, pv_carried)

    p_scr[...] = jnp.zeros(p_scr.shape, BF16)
    for g in range(last):
        score_stage(0, g)
    max_stage(0, 0, (0 - i) * bk)

    def body(j, carry):
        step(j, False)
        return carry

    lax.fori_loop(0, i, body, 0)
    step(i, True)

    lq = lam_ref[...]
    lam = (jnp.exp(jnp.sum(lq[0:1] * lq[1:2], axis=1, keepdims=True))
           - jnp.exp(jnp.sum(lq[2:3] * lq[3:4], axis=1, keepdims=True)) + lam_init)
    o = acc_scr[...] / l_scr[...]
    o = o[:, :bq] - lam * o[:, bq:]
    ms = jnp.mean(o * o, axis=0, keepdims=True)
    o = o * lax.rsqrt(ms + EPS) * g_ref[...] * (1.0 - lam_init)
    o_ref[...] = o.T.astype(BF16)


def _diff_attention(qt, k, vt, lam_qk, attn_g, layer):
    S = k.shape[0]
    bq = bk = ATTN_BLOCK
    nk = S // bk
    lam_init = 0.8 - 0.6 * math.exp(-0.3 * layer)
    return pl.pallas_call(
        functools.partial(_attn_kernel, lam_init=lam_init),
        grid=(DA_HEADS, S // bq),
        in_specs=[
            pl.BlockSpec((4, DA_HEAD_DIM), lambda h, i: (0, 0)),
            pl.BlockSpec((DA_VAL_DIM, 1), lambda h, i: (0, 0)),
            pl.BlockSpec((1, LANES, bq), lambda h, i: (i, h, 0)),
            pl.BlockSpec((S, LANES), lambda h, i: (0, h)),
            pl.BlockSpec((nk, DA_VAL_DIM, bk), lambda h, i: (0, h, 0)),
        ],
        out_specs=pl.BlockSpec((bq, DA_VAL_DIM), lambda h, i: (i, h)),
        out_shape=jax.ShapeDtypeStruct((S, DA_WIDTH), BF16),
        scratch_shapes=[
            pltpu.VMEM((1, 2 * bq), F32),
            pltpu.VMEM((1, 2 * bq), F32),
            pltpu.VMEM((1, 2 * bq), F32),
            pltpu.VMEM((DA_VAL_DIM, 2 * bq), F32),
            pltpu.VMEM((2 * bq // ATTN_QUERY_GROUP, bk, ATTN_QUERY_GROUP), F32),
            pltpu.VMEM((bk, ATTN_QUERY_GROUP), BF16),
        ],
        compiler_params=_params(("parallel", "parallel")),
        name="diff_attn",
    )(lam_qk, attn_g.reshape(DA_VAL_DIM, 1), qt, k, vt)


def _hgrn_kernel(lbl_ref, g_ref, hg_ref, o_ref, st_scr, o_scr, *, layer):
    T = hg_ref.shape[0]
    c = HG_CHUNK
    W = HG_WIDTH

    @pl.when(pl.program_id(0) == 0)
    def _():
        st_scr[...] = jnp.zeros(st_scr.shape, F32)

    lg = lbl_ref[...]
    ex = jnp.exp(lg - jnp.max(lg, axis=0, keepdims=True))
    prob = ex / jnp.sum(ex, axis=0, keepdims=True)
    lb = jnp.zeros((1, W), F32)
    for li in range(1, layer + 1):
        lb = lb + prob[li:li + 1]

    same_head = _head_block_mask(W, HG_DIM)
    ones_bd = same_head.astype(BF16)
    rowi = lax.broadcasted_iota(jnp.int32, (c, W), 0)

    def body(s, carry):
        sl = pl.ds(pl.multiple_of(s * c, c), c)
        qf = hg_ref[sl, 0:W]
        z = hg_ref[sl, W:2 * W]
        iv = hg_ref[sl, 2 * W:3 * W]
        f = lb + (1.0 - lb) * jax.nn.sigmoid(z)
        logf = jnp.log(jnp.maximum(f, TINY))
        key = (1.0 - lb) * jax.nn.sigmoid(-z)
        val = _silu(iv)
        b = logf
        sft = 1
        while sft < c:
            b = b + jnp.where(rowi >= sft, pltpu.roll(b, sft, 0), 0.0)
            sft *= 2
        es = [(qf * key).astype(BF16)]
        vrs = [val]
        for d in range(1, c):
            kr = pltpu.roll(key, d, 0)
            br = pltpu.roll(b, d, 0)
            e = jnp.where(rowi >= d, qf * kr * jnp.exp(b - br), 0.0)
            es.append(e.astype(BF16))
            vrs.append(pltpu.roll(val, d, 0))
        estack = jnp.concatenate(es, axis=0)
        r = jnp.dot(estack, ones_bd, preferred_element_type=F32)
        o = r[0:c] * vrs[0]
        for d in range(1, c):
            o = o + r[d * c:(d + 1) * c] * vrs[d]
        st = st_scr[...]
        qd = (qf * jnp.exp(b)).astype(BF16)
        o = o + lax.dot_general(qd, st.astype(BF16), (((1,), (1,)), ((), ())),
                                preferred_element_type=F32)
        o_scr[sl, :] = o
        b_last = b[c - 1:c]
        kd = (key * jnp.exp(b_last - b)).astype(BF16)
        upd = lax.dot_general(val.astype(BF16), kd, (((0,), (0,)), ((), ())),
                              preferred_element_type=F32)
        st_scr[...] = jnp.exp(b_last) * st + jnp.where(same_head, upd, 0.0)
        return carry

    lax.fori_loop(0, T // c, body, 0)

    o = o_scr[...]
    ms = jnp.dot(o * o, same_head.astype(F32), preferred_element_type=F32,
                 precision=HIGHEST) * (1.0 / HG_DIM)
    gate = hg_ref[:, 3 * W:4 * W]
    o_ref[...] = (o * lax.rsqrt(ms + EPS) * g_ref[...] * _silu(gate)).astype(BF16)


def _hgrn2(hg, lb_logits, norm_g, layer):
    S = hg.shape[0]
    T = HG_BLOCK
    g_t = jnp.tile(norm_g.reshape(1, HG_DIM), (1, HG_HEADS))
    return pl.pallas_call(
        functools.partial(_hgrn_kernel, layer=layer),
        grid=(S // T,),
        in_specs=[
            pl.BlockSpec((DEPTH, HG_WIDTH), lambda i: (0, 0)),
            pl.BlockSpec((1, HG_WIDTH), lambda i: (0, 0)),
            pl.BlockSpec((T, 4 * HG_WIDTH), lambda i: (i, 0)),
        ],
        out_specs=pl.BlockSpec((T, HG_WIDTH), lambda i: (i, 0)),
        out_shape=jax.ShapeDtypeStruct((S, HG_WIDTH), BF16),
        scratch_shapes=[
            pltpu.VMEM((HG_WIDTH, HG_WIDTH), F32),
            pltpu.VMEM((T, HG_WIDTH), F32),
        ],
        compiler_params=_params(("arbitrary",)),
        name="hgrn2",
    )(lb_logits, g_t, hg)


def _pool_kernel(u_ref, w_ref, sc_ref, o_ref, halo_scr):
    T = u_ref.shape[0]
    H = POOL_HALO
    blk = pl.program_id(0)

    @pl.when(blk == 0)
    def _():
        halo_scr[...] = jnp.zeros(halo_scr.shape, F32)

    u = u_ref[...]
    ext = jnp.concatenate([halo_scr[...], u], axis=0)
    halo_scr[...] = u[T - H:, :]
    sums = []
    s = ext
    w = 1
    while w < POOL_WINDOWS[-1]:
        s = s + pltpu.roll(s, w, 0)
        w *= 2
        sums.append(s[H:, :])
    t1 = (blk * T + 1 + lax.broadcasted_iota(jnp.int32, (T, POOL_WIDTH), 0)).astype(F32)
    lane = lax.broadcasted_iota(jnp.int32, (T, POOL_WIDTH), 1)
    pooled = None
    for gi, win in enumerate(POOL_WINDOWS):
        mean = sums[gi] / jnp.minimum(t1, float(win))
        pooled = mean if pooled is None else jnp.where(lane >= gi * POOL_GROUP, mean, pooled)
    pooled = pooled - u
    y = jnp.dot(pooled.astype(BF16), w_ref[...], preferred_element_type=F32)
    o_ref[...] = (y * sc_ref[...]).astype(BF16)


def _multiscale_pool(pu, pool_w, pool_scale):
    S = pu.shape[0]
    T = ROW_TILE
    w_bd = jax.scipy.linalg.block_diag(*[pool_w[g] for g in range(len(POOL_WINDOWS))]).astype(BF16)
    return pl.pallas_call(
        _pool_kernel,
        grid=(S // T,),
        in_specs=[
            pl.BlockSpec((T, POOL_WIDTH), lambda i: (i, 0)),
            pl.BlockSpec((POOL_WIDTH, POOL_WIDTH), lambda i: (0, 0)),
            pl.BlockSpec((1, POOL_WIDTH), lambda i: (0, 0)),
        ],
        out_specs=pl.BlockSpec((T, POOL_WIDTH), lambda i: (i, 0)),
        out_shape=jax.ShapeDtypeStruct((S, POOL_WIDTH), BF16),
        scratch_shapes=[pltpu.VMEM((POOL_HALO, POOL_WIDTH), F32)],
        compiler_params=_params(("arbitrary",)),
        name="ms_pool",
    )(pu, w_bd, pool_scale.reshape(1, POOL_WIDTH))


def _outproj_kernel(a_ref, r_ref, p_ref, x_ref, w_ref, g1_ref, lg_ref, lb_ref, o_ref):
    m = jnp.dot(a_ref[...], w_ref[0:DA_WIDTH, :], preferred_element_type=F32)
    m += jnp.dot(r_ref[...], w_ref[DA_WIDTH:DA_WIDTH + HG_WIDTH, :], preferred_element_type=F32)
    m += jnp.dot(p_ref[...], w_ref[DA_WIDTH + HG_WIDTH:, :], preferred_element_type=F32)
    y = ALPHA * x_ref[...] + (1.0 + g1_ref[...]) * m
    o_ref[...] = _layer_norm(y, lg_ref[...], lb_ref[...])


def _out_projection(a, r, p, x, w_out_bf, g1, ln_g, ln_b):
    S = x.shape[0]
    tm = ROW_TILE
    row = lambda i: (i, 0)
    fixed = lambda i: (0, 0)
    vec = pl.BlockSpec((1, D_MODEL), fixed)
    return pl.pallas_call(
        _outproj_kernel,
        grid=(S // tm,),
        in_specs=[
            pl.BlockSpec((tm, DA_WIDTH), row),
            pl.BlockSpec((tm, HG_WIDTH), row),
            pl.BlockSpec((tm, POOL_WIDTH), row),
            pl.BlockSpec((tm, D_MODEL), row),
            pl.BlockSpec((D_MIX, D_MODEL), fixed),
            vec, vec, vec,
        ],
        out_specs=pl.BlockSpec((tm, D_MODEL), row),
        out_shape=jax.ShapeDtypeStruct((S, D_MODEL), F32),
        compiler_params=_params(("parallel",)),
        name="out_proj_ln",
    )(a, r, p, x, w_out_bf, g1, ln_g.reshape(1, D_MODEL), ln_b.reshape(1, D_MODEL))


def _ffn_kernel(x_ref, sc_ref, sh_ref, g2_ref, w1_ref, w3_ref, w2_ref, lg_ref, lb_ref, o_ref):
    x = x_ref[...]
    h = (x * (1.0 + sc_ref[...]) + sh_ref[...]).astype(BF16)
    acc = jnp.zeros(x.shape, F32)
    step = 4 * MXU_DIM
    for lo in range(0, FFN_DIM, step):
        hi = min(lo + step, FFN_DIM)
        a = jnp.dot(h, w1_ref[:, lo:hi], preferred_element_type=F32)
        b = jnp.dot(h, w3_ref[:, lo:hi], preferred_element_type=F32)
        acc += jnp.dot((_silu(a) * b).astype(BF16), w2_ref[lo:hi, :], preferred_element_type=F32)
    y = ALPHA * x + (1.0 + g2_ref[...]) * acc
    o_ref[...] = _layer_norm(y, lg_ref[...], lb_ref[...])


def _dense_ffn(x, sc, sh, g2, w1, w3, w2, ln_g, ln_b):
    S = x.shape[0]
    tm = ROW_TILE
    row = lambda i: (i, 0)
    fixed = lambda i: (0, 0)
    vec = pl.BlockSpec((1, D_MODEL), fixed)
    once = pl.Buffered(1)
    return pl.pallas_call(
        _ffn_kernel,
        grid=(S // tm,),
        in_specs=[
            pl.BlockSpec((tm, D_MODEL), row),
            vec, vec, vec,
            pl.BlockSpec((D_MODEL, FFN_DIM), fixed, pipeline_mode=once),
            pl.BlockSpec((D_MODEL, FFN_DIM), fixed, pipeline_mode=once),
            pl.BlockSpec((FFN_DIM, D_MODEL), fixed, pipeline_mode=once),
            vec, vec,
        ],
        out_specs=pl.BlockSpec((tm, D_MODEL), row),
        out_shape=jax.ShapeDtypeStruct((S, D_MODEL), F32),
        compiler_params=_params(("parallel",)),
        name="dense_ffn_ln",
    )(x, sc, sh, g2, w1, w3, w2, ln_g.reshape(1, D_MODEL), ln_b.reshape(1, D_MODEL))


def _router_kernel(x_ref, sc_ref, sh_ref, rw_ref, h_ref, rank_ref, comb_ref, cum_ref):
    T = x_ref.shape[0]
    tc = MOE_CHUNK
    E = N_EXPERTS
    h = x_ref[...] * (1.0 + sc_ref[...]) + sh_ref[...]
    h_ref[...] = h.astype(BF16)
    logits = lax.dot_general(rw_ref[...], h, (((1,), (1,)), ((), ())),
                             preferred_element_type=F32, precision=HIGHEST)
    eidx = lax.broadcasted_iota(jnp.int32, (E, T), 0)
    v1 = jnp.max(logits, axis=0, keepdims=True)
    i1 = jnp.min(jnp.where(logits == v1, eidx, E), axis=0, keepdims=True)
    m1 = eidx == i1
    rest = jnp.where(m1, -jnp.inf, logits)
    v2 = jnp.max(rest, axis=0, keepdims=True)
    i2 = jnp.min(jnp.where(rest == v2, eidx, E), axis=0, keepdims=True)
    m2 = eidx == i2
    e2 = jnp.exp(v2 - v1)
    g1 = 1.0 / (1.0 + e2)
    g2 = e2 / (1.0 + e2)
    comb = jnp.where(m1, g1, 0.0) + jnp.where(m2, g2, 0.0)
    routed = jnp.logical_or(m1, m2)
    ind = routed.astype(BF16)
    r_i = lax.broadcasted_iota(jnp.int32, (tc, tc), 0)
    c_i = lax.broadcasted_iota(jnp.int32, (tc, tc), 1)
    strict = (r_i < c_i).astype(BF16)
    lane = lax.broadcasted_iota(jnp.int32, (E, LANES), 1)
    running = jnp.zeros((E, 1), F32)
    cum = jnp.zeros((E, LANES), F32)
    for ci in range(T // tc):
        sl = slice(ci * tc, (ci + 1) * tc)
        ind_c = ind[:, sl]
        rank_c = jnp.dot(ind_c, strict, preferred_element_type=F32) + running
        rank_ref[:, ci] = jnp.where(routed[:, sl], rank_c, -1.0).reshape(E, 1, tc)
        comb_ref[:, ci] = comb[:, sl].reshape(E, 1, tc)
        cum = jnp.where(lane == ci, running, cum)
        running = running + jnp.sum(ind_c.astype(F32), axis=1, keepdims=True)
    cum = jnp.where(lane == T // tc, running, cum)
    cum_ref[0] = cum.astype(jnp.int32)


def _router(x, sc, sh, router_w):
    S = x.shape[0]
    T = MOE_BLOCK
    E = N_EXPERTS
    fixed = lambda i: (0, 0)
    vec = pl.BlockSpec((1, D_MODEL), fixed)
    return pl.pallas_call(
        _router_kernel,
        grid=(S // T,),
        in_specs=[
            pl.BlockSpec((T, D_MODEL), lambda i: (i, 0)),
            vec, vec,
            pl.BlockSpec((E, D_MODEL), fixed),
        ],
        out_specs=[
            pl.BlockSpec((T, D_MODEL), lambda i: (i, 0)),
            pl.BlockSpec((E, T // MOE_CHUNK, 1, MOE_CHUNK), lambda i: (0, i, 0, 0)),
            pl.BlockSpec((E, T // MOE_CHUNK, 1, MOE_CHUNK), lambda i: (0, i, 0, 0)),
            pl.BlockSpec((1, E, LANES), lambda i: (i, 0, 0)),
        ],
        out_shape=[
            jax.ShapeDtypeStruct((S, D_MODEL), BF16),
            jax.ShapeDtypeStruct((E, S // MOE_CHUNK, 1, MOE_CHUNK), F32),
            jax.ShapeDtypeStruct((E, S // MOE_CHUNK, 1, MOE_CHUNK), F32),
            jax.ShapeDtypeStruct((S // T, E, LANES), jnp.int32),
        ],
        compiler_params=_params(("parallel",)),
        name="moe_router",
    )(x, sc, sh, router_w.T)


def _moe_kernel(cum_ref, h_ref, rank_ref, comb_ref, w1_ref, w3_ref, w2_ref, y_ref, xe_scr, gate_scr):
    T = h_ref.shape[0]
    tr, tc, win = MOE_TILE, MOE_CHUNK, MOE_WINDOW
    nchunk = T // tc
    b = pl.program_id(0)
    e = pl.program_id(1)
    f = pl.program_id(2)

    @pl.when(jnp.logical_and(e == 0, f == 0))
    def _():
        y_ref[...] = jnp.zeros(y_ref.shape, F32)

    base = (b * N_EXPERTS + e) * LANES
    count = cum_ref[base + nchunk]
    ntiles = (count + (tr - 1)) // tr

    def token_row(ref, first, n):
        blk = ref[0, pl.ds(first, n)]
        return jnp.concatenate([blk[c] for c in range(n)], axis=1)

    def tile(r, carry):
        lo = r * tr
        rows = pl.ds(pl.multiple_of(lo, 16), tr)

        def hits(first, n):
            want = (lo + lax.broadcasted_iota(jnp.int32, (tr, n * tc), 0)).astype(F32)
            return token_row(rank_ref, first, n) == want

        @pl.when(f == 0)
        def _():
            hit = hits(0, nchunk)
            xe_scr[rows, :] = jnp.dot(hit.astype(BF16), h_ref[...],
                                      preferred_element_type=F32).astype(BF16)
            gate_scr[rows, :] = jnp.sum(jnp.where(hit, token_row(comb_ref, 0, nchunk), 0.0),
                                        axis=1, keepdims=True)

        xe = xe_scr[rows, :]
        a = jnp.dot(xe, w1_ref[0], preferred_element_type=F32)
        g = jnp.dot(xe, w3_ref[0], preferred_element_type=F32)
        mid = (_silu(a) * g).astype(BF16)
        out = jnp.dot(mid, w2_ref[0], preferred_element_type=F32) * gate_scr[rows, :]
        out_bf = out.astype(BF16)

        def scatter(first, n):
            start = first * tc
            if not isinstance(start, int):
                start = pl.multiple_of(start, tc)
            y_ref[pl.ds(start, n * tc), :] += lax.dot_general(
                hits(first, n).astype(BF16), out_bf, (((0,), (0,)), ((), ())),
                preferred_element_type=F32)

        c_lo = jnp.int32(0)
        c_hi = jnp.int32(-1)
        for c in range(nchunk):
            c_lo += (cum_ref[base + c + 1] <= lo).astype(jnp.int32)
            c_hi += (cum_ref[base + c] < lo + tr).astype(jnp.int32)
        fits = c_hi - c_lo < win

        @pl.when(fits)
        def _():
            scatter(jnp.minimum(c_lo, nchunk - win), win)

        @pl.when(jnp.logical_not(fits))
        def _():
            scatter(0, nchunk)

        return carry

    lax.fori_loop(0, ntiles, tile, 0)


def _moe_ffn(h_bf, rank, comb, cum, w1, w3, w2):
    S = h_bf.shape[0]
    T = MOE_BLOCK
    fw = EXPERT_DIM // MOE_FSPLIT
    grid_spec = pltpu.PrefetchScalarGridSpec(
        num_scalar_prefetch=1,
        grid=(S // T, N_EXPERTS, MOE_FSPLIT),
        in_specs=[
            pl.BlockSpec((T, D_MODEL), lambda b, e, f, c: (b, 0)),
            pl.BlockSpec((1, T // MOE_CHUNK, 1, MOE_CHUNK), lambda b, e, f, c: (e, b, 0, 0)),
            pl.BlockSpec((1, T // MOE_CHUNK, 1, MOE_CHUNK), lambda b, e, f, c: (e, b, 0, 0)),
            pl.BlockSpec((1, D_MODEL, fw), lambda b, e, f, c: (e, 0, f)),
            pl.BlockSpec((1, D_MODEL, fw), lambda b, e, f, c: (e, 0, f)),
            pl.BlockSpec((1, fw, D_MODEL), lambda b, e, f, c: (e, f, 0)),
        ],
        out_specs=pl.BlockSpec((T, D_MODEL), lambda b, e, f, c: (b, 0)),
        scratch_shapes=[
            pltpu.VMEM((T + MOE_TILE, D_MODEL), BF16),
            pltpu.VMEM((T + MOE_TILE, 1), F32),
        ],
    )
    return pl.pallas_call(
        _moe_kernel,
        grid_spec=grid_spec,
        out_shape=jax.ShapeDtypeStruct((S, D_MODEL), F32),
        compiler_params=_params(("arbitrary", "arbitrary", "arbitrary")),
        name="moe_experts",
    )(cum.reshape(-1), h_bf, rank, comb, w1, w3, w2)


def _ln_kernel(x_ref, y_ref, g2_ref, lg_ref, lb_ref, o_ref):
    y = ALPHA * x_ref[...] + (1.0 + g2_ref[...]) * y_ref[...]
    o_ref[...] = _layer_norm(y, lg_ref[...], lb_ref[...])


def _residual_ln(x, y, g2, ln_g, ln_b):
    S = x.shape[0]
    tm = ROW_TILE
    row = lambda i: (i, 0)
    vec = pl.BlockSpec((1, D_MODEL), lambda i: (0, 0))
    return pl.pallas_call(
        _ln_kernel,
        grid=(S // tm,),
        in_specs=[pl.BlockSpec((tm, D_MODEL), row), pl.BlockSpec((tm, D_MODEL), row), vec, vec, vec],
        out_specs=pl.BlockSpec((tm, D_MODEL), row),
        out_shape=jax.ShapeDtypeStruct((S, D_MODEL), F32),
        compiler_params=_params(("parallel",)),
        name="residual_ln",
    )(x, y, g2, ln_g.reshape(1, D_MODEL), ln_b.reshape(1, D_MODEL))


def _rope_tables(S):
    d = DA_HEAD_DIM
    inv_freq = 1.0 / (ROPE_THETA ** (jnp.arange(0, d, 2, dtype=F32) / d))
    ang = jnp.arange(S, dtype=F32)[:, None] * inv_freq[None, :]
    cos = jnp.cos(ang)
    sin = jnp.sin(ang)
    reps = LANES // d
    cos_t = jnp.tile(jnp.concatenate([cos, cos], axis=1), (1, reps))
    sin_t = jnp.tile(jnp.concatenate([-sin, sin], axis=1), (1, reps))
    return cos_t, sin_t


def kernel(x, c, w_ada, b_ada, w_in, lam_qk, attn_norm_g, hg_lb_logits, hg_norm_g, pool_w, pool_scale, w_out, ln1_g, ln1_b, ln2_g, ln2_b, ffn_w1, ffn_w3, ffn_w2, router_w, exp_w1, exp_w3, exp_w2):
    B, S, D = x.shape
    assert B == 1 and D == D_MODEL and S % MOE_BLOCK == 0
    xs = x.reshape(S, D)
    mod = _modulation(c, w_ada, b_ada)
    cos_t, sin_t = _rope_tables(S)
    for l in range(DEPTH):
        sh1, sc1, g1, sh2, sc2, g2 = (mod[l, j] for j in range(6))
        qt, k, vt, hg, pu = _in_projection(xs, sc1, sh1, w_in[l].astype(BF16), cos_t, sin_t)
        a = _diff_attention(qt, k, vt, lam_qk[l], attn_norm_g[l], l)
        r = _hgrn2(hg, hg_lb_logits, hg_norm_g[l], l)
        p = _multiscale_pool(pu, pool_w[l], pool_scale[l])
        xs = _out_projection(a, r, p, xs, w_out[l].astype(BF16), g1, ln1_g[l], ln1_b[l])
        if l % 2 == 0:
            i = l // 2
            xs = _dense_ffn(xs, sc2, sh2, g2, ffn_w1[i].astype(BF16), ffn_w3[i].astype(BF16),
                            ffn_w2[i].astype(BF16), ln2_g[l], ln2_b[l])
        else:
            i = l // 2
            h_bf, rank, comb, cum = _router(xs, sc2, sh2, router_w[i])
            y = _moe_ffn(h_bf, rank, comb, cum, exp_w1[i].astype(BF16), exp_w3[i].astype(BF16),
                         exp_w2[i].astype(BF16))
            xs = _residual_ln(xs, y, g2, ln2_g[l], ln2_b[l])
    return xs.reshape(B, S, D)
```

```python
import functools
import math

import jax
import jax.numpy as jnp
from jax import lax
from jax.experimental import pallas as pl
from jax.experimental.pallas import tpu as pltpu

F32 = jnp.float32
BF16 = jnp.bfloat16
HIGHEST = lax.Precision.HIGHEST

D_MODEL = 1024
DEPTH = 2
DA_HEADS = 4
DA_WIDTH = D_MODEL // 2
DA_HEAD_DIM = DA_WIDTH // (2 * DA_HEADS)
DA_VAL_DIM = 2 * DA_HEAD_DIM
HG_HEADS = 4
HG_WIDTH = D_MODEL // 4
HG_DIM = HG_WIDTH // HG_HEADS
POOL_WINDOWS = (2, 4, 8, 16)
POOL_WIDTH = D_MODEL // 4
POOL_GROUP = POOL_WIDTH // len(POOL_WINDOWS)
D_MIX = DA_WIDTH + HG_WIDTH + POOL_WIDTH
D_IN = 3 * DA_WIDTH + 4 * HG_WIDTH + POOL_WIDTH
ROPE_THETA = 10000.0
MASK_VALUE = -1e30
TINY = 1e-30
FFN_DIM = 11 * D_MODEL // 4
N_EXPERTS = 8
EXPERT_DIM = 7 * D_MODEL // 2
ALPHA = (2 * DEPTH) ** 0.25
EPS = 1e-5
LOG2_E = math.log2(math.e)

LANES = 128
MXU_DIM = 256
VMEM_LIMIT = 60 * 1024 * 1024

ROW_TILE = 512
ATTN_BLOCK = 512
ATTN_QUERY_GROUP = 256
HG_BLOCK = 512
HG_CHUNK = 16
POOL_HALO = 16
MOE_BLOCK = 2048
MOE_TILE = 192
MOE_CHUNK = 256
MOE_WINDOW = 5
MOE_FSPLIT = 2


def _params(sem, vmem=VMEM_LIMIT):
    return pltpu.CompilerParams(dimension_semantics=sem, vmem_limit_bytes=vmem)


def _layer_norm(y, g, b):
    mu = jnp.mean(y, axis=-1, keepdims=True)
    d = y - mu
    var = jnp.mean(d * d, axis=-1, keepdims=True)
    return d * lax.rsqrt(var + EPS) * g + b


def _silu(x):
    return x * jax.nn.sigmoid(x)


def _head_block_mask(n, group):
    r = lax.broadcasted_iota(jnp.int32, (n, n), 0) // group
    c = lax.broadcasted_iota(jnp.int32, (n, n), 1) // group
    return r == c


def _mod_kernel(c_ref, w_ref, b_ref, o_ref):
    cond = _silu(c_ref[...])
    cond8 = jnp.broadcast_to(cond, (8, D_MODEL))
    r = jnp.dot(cond8, w_ref[0], preferred_element_type=F32, precision=HIGHEST)
    o_ref[0] = r[0:1] + b_ref[0]


def _modulation(c, w_ada, b_ada):
    n = 6 * DEPTH
    b3 = b_ada.reshape(n, 1, D_MODEL)
    out = pl.pallas_call(
        _mod_kernel,
        grid=(DEPTH, 6),
        in_specs=[
            pl.BlockSpec((1, D_MODEL), lambda l, j: (0, 0)),
            pl.BlockSpec((1, D_MODEL, D_MODEL), lambda l, j: (l, 0, j)),
            pl.BlockSpec((1, 1, D_MODEL), lambda l, j: (l * 6 + j, 0, 0)),
        ],
        out_specs=pl.BlockSpec((1, 1, D_MODEL), lambda l, j: (l * 6 + j, 0, 0)),
        out_shape=jax.ShapeDtypeStruct((n, 1, D_MODEL), F32),
        compiler_params=_params(("parallel", "parallel")),
        name="adaln_mod",
    )(c, w_ada, b3)
    return out.reshape(DEPTH, 6, 1, D_MODEL)


def _inproj_kernel(x_ref, sc_ref, sh_ref, w_ref, wvt_ref, cos_ref, sin_ref,
                   qt_ref, k_ref, vt_ref, hg_ref, pu_ref):
    h = (x_ref[...] * (1.0 + sc_ref[...]) + sh_ref[...]).astype(BF16)
    qk = jnp.dot(h, w_ref[:, : 2 * DA_WIDTH], preferred_element_type=F32)
    reps = 2 * DA_WIDTH // LANES
    cos = jnp.tile(cos_ref[...], (1, reps))
    sin = jnp.tile(sin_ref[...], (1, reps))
    lane = lax.broadcasted_iota(jnp.int32, qk.shape, 1)
    half = DA_HEAD_DIM // 2
    first_half = (lane & half) == 0
    width = 2 * DA_WIDTH
    partner = jnp.where(first_half, pltpu.roll(qk, width - half, 1), pltpu.roll(qk, half, 1))
    rot = qk * cos + partner * sin
    qt_ref[0] = (rot[:, :DA_WIDTH] * (DA_HEAD_DIM ** -0.5 * LOG2_E)).T.astype(BF16)
    k_ref[...] = rot[:, DA_WIDTH:].astype(BF16)
    o = 2 * DA_WIDTH
    vt_ref[0] = lax.dot_general(wvt_ref[...], h, (((1,), (1,)), ((), ())),
                                preferred_element_type=F32).astype(BF16)
    o += DA_WIDTH
    hg_ref[...] = jnp.dot(h, w_ref[:, o:o + 4 * HG_WIDTH], preferred_element_type=F32)
    o += 4 * HG_WIDTH
    pu_ref[...] = jnp.dot(h, w_ref[:, o:o + POOL_WIDTH], preferred_element_type=F32)


def _in_projection(x, sc, sh, w_in_bf, cos_t, sin_t):
    S = x.shape[0]
    tm = ATTN_BLOCK
    row = lambda i: (i, 0)
    fixed = lambda i: (0, 0)
    return pl.pallas_call(
        _inproj_kernel,
        grid=(S // tm,),
        in_specs=[
            pl.BlockSpec((tm, D_MODEL), row),
            pl.BlockSpec((1, D_MODEL), fixed),
            pl.BlockSpec((1, D_MODEL), fixed),
            pl.BlockSpec((D_MODEL, D_IN), fixed),
            pl.BlockSpec((DA_WIDTH, D_MODEL), fixed),
            pl.BlockSpec((tm, LANES), row),
            pl.BlockSpec((tm, LANES), row),
        ],
        out_specs=[
            pl.BlockSpec((1, DA_WIDTH, tm), lambda i: (i, 0, 0)),
            pl.BlockSpec((tm, DA_WIDTH), row),
            pl.BlockSpec((1, DA_WIDTH, tm), lambda i: (i, 0, 0)),
            pl.BlockSpec((tm, 4 * HG_WIDTH), row),
            pl.BlockSpec((tm, POOL_WIDTH), row),
        ],
        out_shape=[
            jax.ShapeDtypeStruct((S // tm, DA_WIDTH, tm), BF16),
            jax.ShapeDtypeStruct((S, DA_WIDTH), BF16),
            jax.ShapeDtypeStruct((S // tm, DA_WIDTH, tm), BF16),
            jax.ShapeDtypeStruct((S, 4 * HG_WIDTH), F32),
            jax.ShapeDtypeStruct((S, POOL_WIDTH), F32),
        ],
        compiler_params=_params(("parallel",)),
        name="in_proj",
    )(x, sc, sh, w_in_bf, w_in_bf[:, 2 * DA_WIDTH:3 * DA_WIDTH].T, cos_t, sin_t)


def _attn_kernel(lam_ref, g_ref, qt_ref, k_ref, vt_ref, o_ref, m_scr, l_scr, a_scr, c_scr, acc_scr, s_scr,
                 p_scr, *, lam_init):
    bq = qt_ref.shape[2]
    bk = vt_ref.shape[2]
    cg = ATTN_QUERY_GROUP
    i = pl.program_id(1)
    qt = qt_ref[0]
    rowq = lax.broadcasted_iota(jnp.int32, qt.shape, 0)
    zero = jnp.zeros_like(qt)
    maps = (jnp.where(rowq < DA_HEAD_DIM, qt, zero), jnp.where(rowq >= DA_HEAD_DIM, qt, zero))
    groups = [(a * bq + c0, c0, maps[a][:, c0:c0 + cg]) for a in range(2) for c0 in range(0, bq, cg)]
    m_scr[...] = jnp.full(m_scr.shape, -jnp.inf, F32)
    l_scr[...] = jnp.zeros(l_scr.shape, F32)
    acc_scr[...] = jnp.zeros(acc_scr.shape, F32)
    key_i = lax.broadcasted_iota(jnp.int32, (bk, cg), 0)
    qry_i = lax.broadcasted_iota(jnp.int32, (bk, cg), 1)
    ng = len(groups)
    last = ng - 1

    def cols_of(g):
        return slice(groups[g][0], groups[g][0] + cg)

    def score_stage(j, g, key_offset):
        kb = k_ref[pl.ds(pl.multiple_of(j * bk, bk), bk), :]
        s = jnp.dot(kb, groups[g][2], preferred_element_type=F32)
        if key_offset is not None:
            s = jnp.where(key_i + key_offset <= qry_i + groups[g][1], s, MASK_VALUE)
        s_scr[g] = s
        c_scr[:, cols_of(g)] = jnp.max(s, axis=0, keepdims=True)

    def max_stage(g):
        cols = cols_of(g)
        m_prev = m_scr[:, cols]
        m_new = jnp.maximum(m_prev, c_scr[:, cols])
        a_scr[:, cols] = jnp.exp2(m_prev - m_new)
        m_scr[:, cols] = m_new

    def exp_stage(j, g, vt, pv_carried, next_offset, diagonal):
        cols = cols_of(g)
        p = jnp.exp2(s_scr[g] - m_scr[:, cols])
        alpha = a_scr[:, cols]
        l_scr[:, cols] = alpha * l_scr[:, cols] + jnp.sum(p, axis=0, keepdims=True)
        pb = p.astype(BF16)
        if g < last and not diagonal:
            score_stage(j + 1, g, next_offset)
        if g < last:
            acc_scr[:, cols] = alpha * acc_scr[:, cols] + jnp.dot(vt, pb, preferred_element_type=F32)
        elif diagonal:
            acc_scr[:, cols] = alpha * (acc_scr[:, cols] + pv_carried) + jnp.dot(
                vt, pb, preferred_element_type=F32)
        else:
            acc_scr[:, cols] = alpha * (acc_scr[:, cols] + pv_carried)
            p_scr[...] = pb

    def step(j, next_offset, diagonal):
        vt = vt_ref[j]
        score_stage(j, last, 0 if diagonal else None)
        pv_carried = jnp.dot(vt_ref[jnp.maximum(j - 1, 0)], p_scr[...], preferred_element_type=F32)
        for g in range(ng):
            if g + 1 < ng:
                max_stage(g + 1)
            elif not diagonal:
                max_stage(0)
            exp_stage(j, g, vt, pv_carried, next_offset, diagonal)

    p_scr[...] = jnp.zeros(p_scr.shape, BF16)
    for g in range(last):
        score_stage(0, g, (0 - i) * bk)
    max_stage(0)

    def body(j, carry):
        step(j, None, False)
        return carry

    lax.fori_loop(0, i - 1, body, 0)

    @pl.when(i > 0)
    def _():
        step(i - 1, 0, False)

    step(i, None, True)

    lq = lam_ref[...]
    lam = (jnp.exp(jnp.sum(lq[0:1] * lq[1:2], axis=1, keepdims=True))
           - jnp.exp(jnp.sum(lq[2:3] * lq[3:4], axis=1, keepdims=True)) + lam_init)
    o = acc_scr[...] / l_scr[...]
    o = o[:, :bq] - lam * o[:, bq:]
    ms = jnp.mean(o * o, axis=0, keepdims=True)
    o = o * lax.rsqrt(ms + EPS) * g_ref[...] * (1.0 - lam_init)
    o_ref[...] = o.T.astype(BF16)


def _diff_attention(qt, k, vt, lam_qk, attn_g, layer):
    S = k.shape[0]
    bq = bk = ATTN_BLOCK
    nk = S // bk
    lam_init = 0.8 - 0.6 * math.exp(-0.3 * layer)
    return pl.pallas_call(
        functools.partial(_attn_kernel, lam_init=lam_init),
        grid=(DA_HEADS, S // bq),
        in_specs=[
            pl.BlockSpec((4, DA_HEAD_DIM), lambda h, i: (0, 0)),
            pl.BlockSpec((DA_VAL_DIM, 1), lambda h, i: (0, 0)),
            pl.BlockSpec((1, LANES, bq), lambda h, i: (i, h, 0)),
            pl.BlockSpec((S, LANES), lambda h, i: (0, h)),
            pl.BlockSpec((nk, DA_VAL_DIM, bk), lambda h, i: (0, h, 0)),
        ],
        out_specs=pl.BlockSpec((bq, DA_VAL_DIM), lambda h, i: (i, h)),
        out_shape=jax.ShapeDtypeStruct((S, DA_WIDTH), BF16),
        scratch_shapes=[
            pltpu.VMEM((1, 2 * bq), F32),
            pltpu.VMEM((1, 2 * bq), F32),
            pltpu.VMEM((1, 2 * bq), F32),
            pltpu.VMEM((1, 2 * bq), F32),
            pltpu.VMEM((DA_VAL_DIM, 2 * bq), F32),
            pltpu.VMEM((2 * bq // ATTN_QUERY_GROUP, bk, ATTN_QUERY_GROUP), F32),
            pltpu.VMEM((bk, ATTN_QUERY_GROUP), BF16),
        ],
        compiler_params=_params(("parallel", "parallel")),
        name="diff_attn",
    )(lam_qk, attn_g.reshape(DA_VAL_DIM, 1), qt, k, vt)


def _hgrn_kernel(lbl_ref, g_ref, hg_ref, o_ref, st_scr, o_scr, *, layer):
    T = hg_ref.shape[0]
    c = HG_CHUNK
    W = HG_WIDTH

    @pl.when(pl.program_id(0) == 0)
    def _():
        st_scr[...] = jnp.zeros(st_scr.shape, F32)

    lg = lbl_ref[...]
    ex = jnp.exp(lg - jnp.max(lg, axis=0, keepdims=True))
    prob = ex / jnp.sum(ex, axis=0, keepdims=True)
    lb = jnp.zeros((1, W), F32)
    for li in range(1, layer + 1):
        lb = lb + prob[li:li + 1]

    same_head = _head_block_mask(W, HG_DIM)
    ones_bd = same_head.astype(BF16)
    rowi = lax.broadcasted_iota(jnp.int32, (c, W), 0)

    def body(s, carry):
        sl = pl.ds(pl.multiple_of(s * c, c), c)
        qf = hg_ref[sl, 0:W]
        z = hg_ref[sl, W:2 * W]
        iv = hg_ref[sl, 2 * W:3 * W]
        f = lb + (1.0 - lb) * jax.nn.sigmoid(z)
        logf = jnp.log(jnp.maximum(f, TINY))
        key = (1.0 - lb) * jax.nn.sigmoid(-z)
        val = _silu(iv)
        b = logf
        sft = 1
        while sft < c:
            b = b + jnp.where(rowi >= sft, pltpu.roll(b, sft, 0), 0.0)
            sft *= 2
        es = [(qf * key).astype(BF16)]
        vrs = [val]
        for d in range(1, c):
            kr = pltpu.roll(key, d, 0)
            br = pltpu.roll(b, d, 0)
            e = jnp.where(rowi >= d, qf * kr * jnp.exp(b - br), 0.0)
            es.append(e.astype(BF16))
            vrs.append(pltpu.roll(val, d, 0))
        estack = jnp.concatenate(es, axis=0)
        r = jnp.dot(estack, ones_bd, preferred_element_type=F32)
        o = r[0:c] * vrs[0]
        for d in range(1, c):
            o = o + r[d * c:(d + 1) * c] * vrs[d]
        st = st_scr[...]
        qd = (qf * jnp.exp(b)).astype(BF16)
        o = o + lax.dot_general(qd, st.astype(BF16), (((1,), (1,)), ((), ())),
                                preferred_element_type=F32)
        o_scr[sl, :] = o
        b_last = b[c - 1:c]
        kd = (key * jnp.exp(b_last - b)).astype(BF16)
        upd = lax.dot_general(val.astype(BF16), kd, (((0,), (0,)), ((), ())),
                              preferred_element_type=F32)
        st_scr[...] = jnp.exp(b_last) * st + jnp.where(same_head, upd, 0.0)
        return carry

    lax.fori_loop(0, T // c, body, 0)

    o = o_scr[...]
    ms = jnp.dot(o * o, same_head.astype(F32), preferred_element_type=F32,
                 precision=HIGHEST) * (1.0 / HG_DIM)
    gate = hg_ref[:, 3 * W:4 * W]
    o_ref[...] = (o * lax.rsqrt(ms + EPS) * g_ref[...] * _silu(gate)).astype(BF16)


def _hgrn2(hg, lb_logits, norm_g, layer):
    S = hg.shape[0]
    T = HG_BLOCK
    g_t = jnp.tile(norm_g.reshape(1, HG_DIM), (1, HG_HEADS))
    return pl.pallas_call(
        functools.partial(_hgrn_kernel, layer=layer),
        grid=(S // T,),
        in_specs=[
            pl.BlockSpec((DEPTH, HG_WIDTH), lambda i: (0, 0)),
            pl.BlockSpec((1, HG_WIDTH), lambda i: (0, 0)),
            pl.BlockSpec((T, 4 * HG_WIDTH), lambda i: (i, 0)),
        ],
        out_specs=pl.BlockSpec((T, HG_WIDTH), lambda i: (i, 0)),
        out_shape=jax.ShapeDtypeStruct((S, HG_WIDTH), BF16),
        scratch_shapes=[
            pltpu.VMEM((HG_WIDTH, HG_WIDTH), F32),
            pltpu.VMEM((T, HG_WIDTH), F32),
        ],
        compiler_params=_params(("arbitrary",)),
        name="hgrn2",
    )(lb_logits, g_t, hg)


def _pool_kernel(u_ref, w_ref, sc_ref, o_ref, halo_scr):
    T = u_ref.shape[0]
    H = POOL_HALO
    blk = pl.program_id(0)

    @pl.when(blk == 0)
    def _():
        halo_scr[...] = jnp.zeros(halo_scr.shape, F32)

    u = u_ref[...]
    ext = jnp.concatenate([halo_scr[...], u], axis=0)
    halo_scr[...] = u[T - H:, :]
    sums = []
    s = ext
    w = 1
    while w < POOL_WINDOWS[-1]:
        s = s + pltpu.roll(s, w, 0)
        w *= 2
        sums.append(s[H:, :])
    t1 = (blk * T + 1 + lax.broadcasted_iota(jnp.int32, (T, POOL_WIDTH), 0)).astype(F32)
    lane = lax.broadcasted_iota(jnp.int32, (T, POOL_WIDTH), 1)
    pooled = None
    for gi, win in enumerate(POOL_WINDOWS):
        mean = sums[gi] / jnp.minimum(t1, float(win))
        pooled = mean if pooled is None else jnp.where(lane >= gi * POOL_GROUP, mean, pooled)
    pooled = pooled - u
    y = jnp.dot(pooled.astype(BF16), w_ref[...], preferred_element_type=F32)
    o_ref[...] = (y * sc_ref[...]).astype(BF16)


def _multiscale_pool(pu, pool_w, pool_scale):
    S = pu.shape[0]
    T = ROW_TILE
    w_bd = jax.scipy.linalg.block_diag(*[pool_w[g] for g in range(len(POOL_WINDOWS))]).astype(BF16)
    return pl.pallas_call(
        _pool_kernel,
        grid=(S // T,),
        in_specs=[
            pl.BlockSpec((T, POOL_WIDTH), lambda i: (i, 0)),
            pl.BlockSpec((POOL_WIDTH, POOL_WIDTH), lambda i: (0, 0)),
            pl.BlockSpec((1, POOL_WIDTH), lambda i: (0, 0)),
        ],
        out_specs=pl.BlockSpec((T, POOL_WIDTH), lambda i: (i, 0)),
        out_shape=jax.ShapeDtypeStruct((S, POOL_WIDTH), BF16),
        scratch_shapes=[pltpu.VMEM((POOL_HALO, POOL_WIDTH), F32)],
        compiler_params=_params(("arbitrary",)),
        name="ms_pool",
    )(pu, w_bd, pool_scale.reshape(1, POOL_WIDTH))


def _outproj_kernel(a_ref, r_ref, p_ref, x_ref, w_ref, g1_ref, lg_ref, lb_ref, o_ref):
    m = jnp.dot(a_ref[...], w_ref[0:DA_WIDTH, :], preferred_element_type=F32)
    m += jnp.dot(r_ref[...], w_ref[DA_WIDTH:DA_WIDTH + HG_WIDTH, :], preferred_element_type=F32)
    m += jnp.dot(p_ref[...], w_ref[DA_WIDTH + HG_WIDTH:, :], preferred_element_type=F32)
    y = ALPHA * x_ref[...] + (1.0 + g1_ref[...]) * m
    o_ref[...] = _layer_norm(y, lg_ref[...], lb_ref[...])


def _out_projection(a, r, p, x, w_out_bf, g1, ln_g, ln_b):
    S = x.shape[0]
    tm = ROW_TILE
    row = lambda i: (i, 0)
    fixed = lambda i: (0, 0)
    vec = pl.BlockSpec((1, D_MODEL), fixed)
    return pl.pallas_call(
        _outproj_kernel,
        grid=(S // tm,),
        in_specs=[
            pl.BlockSpec((tm, DA_WIDTH), row),
            pl.BlockSpec((tm, HG_WIDTH), row),
            pl.BlockSpec((tm, POOL_WIDTH), row),
            pl.BlockSpec((tm, D_MODEL), row),
            pl.BlockSpec((D_MIX, D_MODEL), fixed),
            vec, vec, vec,
        ],
        out_specs=pl.BlockSpec((tm, D_MODEL), row),
        out_shape=jax.ShapeDtypeStruct((S, D_MODEL), F32),
        compiler_params=_params(("parallel",)),
        name="out_proj_ln",
    )(a, r, p, x, w_out_bf, g1, ln_g.reshape(1, D_MODEL), ln_b.reshape(1, D_MODEL))


def _ffn_kernel(x_ref, sc_ref, sh_ref, g2_ref, w1_ref, w3_ref, w2_ref, lg_ref, lb_ref, o_ref):
    x = x_ref[...]
    h = (x * (1.0 + sc_ref[...]) + sh_ref[...]).astype(BF16)
    acc = jnp.zeros(x.shape, F32)
    step = 4 * MXU_DIM
    for lo in range(0, FFN_DIM, step):
        hi = min(lo + step, FFN_DIM)
        a = jnp.dot(h, w1_ref[:, lo:hi], preferred_element_type=F32)
        b = jnp.dot(h, w3_ref[:, lo:hi], preferred_element_type=F32)
        acc += jnp.dot((_silu(a) * b).astype(BF16), w2_ref[lo:hi, :], preferred_element_type=F32)
    y = ALPHA * x + (1.0 + g2_ref[...]) * acc
    o_ref[...] = _layer_norm(y, lg_ref[...], lb_ref[...])


def _dense_ffn(x, sc, sh, g2, w1, w3, w2, ln_g, ln_b):
    S = x.shape[0]
    tm = ROW_TILE
    row = lambda i: (i, 0)
    fixed = lambda i: (0, 0)
    vec = pl.BlockSpec((1, D_MODEL), fixed)
    once = pl.Buffered(1)
    return pl.pallas_call(
        _ffn_kernel,
        grid=(S // tm,),
        in_specs=[
            pl.BlockSpec((tm, D_MODEL), row),
            vec, vec, vec,
            pl.BlockSpec((D_MODEL, FFN_DIM), fixed, pipeline_mode=once),
            pl.BlockSpec((D_MODEL, FFN_DIM), fixed, pipeline_mode=once),
            pl.BlockSpec((FFN_DIM, D_MODEL), fixed, pipeline_mode=once),
            vec, vec,
        ],
        out_specs=pl.BlockSpec((tm, D_MODEL), row),
        out_shape=jax.ShapeDtypeStruct((S, D_MODEL), F32),
        compiler_params=_params(("parallel",)),
        name="dense_ffn_ln",
    )(x, sc, sh, g2, w1, w3, w2, ln_g.reshape(1, D_MODEL), ln_b.reshape(1, D_MODEL))


def _router_kernel(x_ref, sc_ref, sh_ref, rw_ref, h_ref, rank_ref, comb_ref, cum_ref):
    T = x_ref.shape[0]
    tc = MOE_CHUNK
    E = N_EXPERTS
    h = x_ref[...] * (1.0 + sc_ref[...]) + sh_ref[...]
    h_ref[...] = h.astype(BF16)
    logits = lax.dot_general(rw_ref[...], h, (((1,), (1,)), ((), ())),
                             preferred_element_type=F32, precision=HIGHEST)
    eidx = lax.broadcasted_iota(jnp.int32, (E, T), 0)
    v1 = jnp.max(logits, axis=0, keepdims=True)
    i1 = jnp.min(jnp.where(logits == v1, eidx, E), axis=0, keepdims=True)
    m1 = eidx == i1
    rest = jnp.where(m1, -jnp.inf, logits)
    v2 = jnp.max(rest, axis=0, keepdims=True)
    i2 = jnp.min(jnp.where(rest == v2, eidx, E), axis=0, keepdims=True)
    m2 = eidx == i2
    e2 = jnp.exp(v2 - v1)
    g1 = 1.0 / (1.0 + e2)
    g2 = e2 / (1.0 + e2)
    comb = jnp.where(m1, g1, 0.0) + jnp.where(m2, g2, 0.0)
    routed = jnp.logical_or(m1, m2)
    ind = routed.astype(BF16)
    r_i = lax.broadcasted_iota(jnp.int32, (tc, tc), 0)
    c_i = lax.broadcasted_iota(jnp.int32, (tc, tc), 1)
    strict = (r_i < c_i).astype(BF16)
    lane = lax.broadcasted_iota(jnp.int32, (E, LANES), 1)
    running = jnp.zeros((E, 1), F32)
    cum = jnp.zeros((E, LANES), F32)
    for ci in range(T // tc):
        sl = slice(ci * tc, (ci + 1) * tc)
        ind_c = ind[:, sl]
        rank_c = jnp.dot(ind_c, strict, preferred_element_type=F32) + running
        rank_ref[:, ci] = jnp.where(routed[:, sl], rank_c, -1.0).reshape(E, 1, tc)
        comb_ref[:, ci] = comb[:, sl].reshape(E, 1, tc)
        cum = jnp.where(lane == ci, running, cum)
        running = running + jnp.sum(ind_c.astype(F32), axis=1, keepdims=True)
    cum = jnp.where(lane == T // tc, running, cum)
    cum_ref[0] = cum.astype(jnp.int32)


def _router(x, sc, sh, router_w):
    S = x.shape[0]
    T = MOE_BLOCK
    E = N_EXPERTS
    fixed = lambda i: (0, 0)
    vec = pl.BlockSpec((1, D_MODEL), fixed)
    return pl.pallas_call(
        _router_kernel,
        grid=(S // T,),
        in_specs=[
            pl.BlockSpec((T, D_MODEL), lambda i: (i, 0)),
            vec, vec,
            pl.BlockSpec((E, D_MODEL), fixed),
        ],
        out_specs=[
            pl.BlockSpec((T, D_MODEL), lambda i: (i, 0)),
            pl.BlockSpec((E, T // MOE_CHUNK, 1, MOE_CHUNK), lambda i: (0, i, 0, 0)),
            pl.BlockSpec((E, T // MOE_CHUNK, 1, MOE_CHUNK), lambda i: (0, i, 0, 0)),
            pl.BlockSpec((1, E, LANES), lambda i: (i, 0, 0)),
        ],
        out_shape=[
            jax.ShapeDtypeStruct((S, D_MODEL), BF16),
            jax.ShapeDtypeStruct((E, S // MOE_CHUNK, 1, MOE_CHUNK), F32),
            jax.ShapeDtypeStruct((E, S // MOE_CHUNK, 1, MOE_CHUNK), F32),
            jax.ShapeDtypeStruct((S // T, E, LANES), jnp.int32),
        ],
        compiler_params=_params(("parallel",)),
        name="moe_router",
    )(x, sc, sh, router_w.T)


def _moe_kernel(cum_ref, h_ref, rank_ref, comb_ref, w1_ref, w3_ref, w2_ref, y_ref, xe_scr, gate_scr):
    T = h_ref.shape[0]
    tr, tc, win = MOE_TILE, MOE_CHUNK, MOE_WINDOW
    nchunk = T // tc
    b = pl.program_id(0)
    e = pl.program_id(1)
    f = pl.program_id(2)

    @pl.when(jnp.logical_and(e == 0, f == 0))
    def _():
        y_ref[...] = jnp.zeros(y_ref.shape, F32)

    base = (b * N_EXPERTS + e) * LANES
    count = cum_ref[base + nchunk]
    ntiles = (count + (tr - 1)) // tr

    def token_row(ref, first, n):
        blk = ref[0, pl.ds(first, n)]
        return jnp.concatenate([blk[c] for c in range(n)], axis=1)

    def tile(r, carry):
        lo = r * tr
        rows = pl.ds(pl.multiple_of(lo, 16), tr)

        def hits(first, n):
            want = (lo + lax.broadcasted_iota(jnp.int32, (tr, n * tc), 0)).astype(F32)
            return token_row(rank_ref, first, n) == want

        @pl.when(f == 0)
        def _():
            hit = hits(0, nchunk)
            xe_scr[rows, :] = jnp.dot(hit.astype(BF16), h_ref[...],
                                      preferred_element_type=F32).astype(BF16)
            gate_scr[rows, :] = jnp.sum(jnp.where(hit, token_row(comb_ref, 0, nchunk), 0.0),
                                        axis=1, keepdims=True)

        xe = xe_scr[rows, :]
        a = jnp.dot(xe, w1_ref[0], preferred_element_type=F32)
        g = jnp.dot(xe, w3_ref[0], preferred_element_type=F32)
        mid = (_silu(a) * g).astype(BF16)
        out = jnp.dot(mid, w2_ref[0], preferred_element_type=F32) * gate_scr[rows, :]
        out_bf = out.astype(BF16)

        def scatter(first, n):
            start = first * tc
            if not isinstance(start, int):
                start = pl.multiple_of(start, tc)
            y_ref[pl.ds(start, n * tc), :] += lax.dot_general(
                hits(first, n).astype(BF16), out_bf, (((0,), (0,)), ((), ())),
                preferred_element_type=F32)

        c_lo = jnp.int32(0)
        c_hi = jnp.int32(-1)
        for c in range(nchunk):
            c_lo += (cum_ref[base + c + 1] <= lo).astype(jnp.int32)
            c_hi += (cum_ref[base + c] < lo + tr).astype(jnp.int32)
        fits = c_hi - c_lo < win

        @pl.when(fits)
        def _():
            scatter(jnp.minimum(c_lo, nchunk - win), win)

        @pl.when(jnp.logical_not(fits))
        def _():
            scatter(0, nchunk)

        return carry

    lax.fori_loop(0, ntiles, tile, 0)


def _moe_ffn(h_bf, rank, comb, cum, w1, w3, w2):
    S = h_bf.shape[0]
    T = MOE_BLOCK
    fw = EXPERT_DIM // MOE_FSPLIT
    grid_spec = pltpu.PrefetchScalarGridSpec(
        num_scalar_prefetch=1,
        grid=(S // T, N_EXPERTS, MOE_FSPLIT),
        in_specs=[
            pl.BlockSpec((T, D_MODEL), lambda b, e, f, c: (b, 0)),
            pl.BlockSpec((1, T // MOE_CHUNK, 1, MOE_CHUNK), lambda b, e, f, c: (e, b, 0, 0)),
            pl.BlockSpec((1, T // MOE_CHUNK, 1, MOE_CHUNK), lambda b, e, f, c: (e, b, 0, 0)),
            pl.BlockSpec((1, D_MODEL, fw), lambda b, e, f, c: (e, 0, f)),
            pl.BlockSpec((1, D_MODEL, fw), lambda b, e, f, c: (e, 0, f)),
            pl.BlockSpec((1, fw, D_MODEL), lambda b, e, f, c: (e, f, 0)),
        ],
        out_specs=pl.BlockSpec((T, D_MODEL), lambda b, e, f, c: (b, 0)),
        scratch_shapes=[
            pltpu.VMEM((T + MOE_TILE, D_MODEL), BF16),
            pltpu.VMEM((T + MOE_TILE, 1), F32),
        ],
    )
    return pl.pallas_call(
        _moe_kernel,
        grid_spec=grid_spec,
        out_shape=jax.ShapeDtypeStruct((S, D_MODEL), F32),
        compiler_params=_params(("arbitrary", "arbitrary", "arbitrary")),
        name="moe_experts",
    )(cum.reshape(-1), h_bf, rank, comb, w1, w3, w2)


def _ln_kernel(x_ref, y_ref, g2_ref, lg_ref, lb_ref, o_ref):
    y = ALPHA * x_ref[...] + (1.0 + g2_ref[...]) * y_ref[...]
    o_ref[...] = _layer_norm(y, lg_ref[...], lb_ref[...])


def _residual_ln(x, y, g2, ln_g, ln_b):
    S = x.shape[0]
    tm = ROW_TILE
    row = lambda i: (i, 0)
    vec = pl.BlockSpec((1, D_MODEL), lambda i: (0, 0))
    return pl.pallas_call(
        _ln_kernel,
        grid=(S // tm,),
        in_specs=[pl.BlockSpec((tm, D_MODEL), row), pl.BlockSpec((tm, D_MODEL), row), vec, vec, vec],
        out_specs=pl.BlockSpec((tm, D_MODEL), row),
        out_shape=jax.ShapeDtypeStruct((S, D_MODEL), F32),
        compiler_params=_params(("parallel",)),
        name="residual_ln",
    )(x, y, g2, ln_g.reshape(1, D_MODEL), ln_b.reshape(1, D_MODEL))


def _rope_tables(S):
    d = DA_HEAD_DIM
    inv_freq = 1.0 / (ROPE_THETA ** (jnp.arange(0, d, 2, dtype=F32) / d))
    ang = jnp.arange(S, dtype=F32)[:, None] * inv_freq[None, :]
    cos = jnp.cos(ang)
    sin = jnp.sin(ang)
    reps = LANES // d
    cos_t = jnp.tile(jnp.concatenate([cos, cos], axis=1), (1, reps))
    sin_t = jnp.tile(jnp.concatenate([-sin, sin], axis=1), (1, reps))
    return cos_t, sin_t


def kernel(x, c, w_ada, b_ada, w_in, lam_qk, attn_norm_g, hg_lb_logits, hg_norm_g, pool_w, pool_scale, w_out, ln1_g, ln1_b, ln2_g, ln2_b, ffn_w1, ffn_w3, ffn_w2, router_w, exp_w1, exp_w3, exp_w2):
    B, S, D = x.shape
    assert B == 1 and D == D_MODEL and S % MOE_BLOCK == 0
    xs = x.reshape(S, D)
    mod = _modulation(c, w_ada, b_ada)
    cos_t, sin_t = _rope_tables(S)
    for l in range(DEPTH):
        sh1, sc1, g1, sh2, sc2, g2 = (mod[l, j] for j in range(6))
        qt, k, vt, hg, pu = _in_projection(xs, sc1, sh1, w_in[l].astype(BF16), cos_t, sin_t)
        a = _diff_attention(qt, k, vt, lam_qk[l], attn_norm_g[l], l)
        r = _hgrn2(hg, hg_lb_logits, hg_norm_g[l], l)
        p = _multiscale_pool(pu, pool_w[l], pool_scale[l])
        xs = _out_projection(a, r, p, xs, w_out[l].astype(BF16), g1, ln1_g[l], ln1_b[l])
        if l % 2 == 0:
            i = l // 2
            xs = _dense_ffn(xs, sc2, sh2, g2, ffn_w1[i].astype(BF16), ffn_w3[i].astype(BF16),
                            ffn_w2[i].astype(BF16), ln2_g[l], ln2_b[l])
        else:
            i = l // 2
            h_bf, rank, comb, cum = _router(xs, sc2, sh2, router_w[i])
            y = _moe_ffn(h_bf, rank, comb, cum, exp_w1[i].astype(BF16), exp_w3[i].astype(BF16),
                         exp_w2[i].astype(BF16))
            xs = _residual_ln(xs, y, g2, ln2_g[l], ln2_b[l])
    return xs.reshape(B, S, D)
```

```python
import functools
import math

import jax
import jax.numpy as jnp
from jax import lax
from jax.experimental import pallas as pl
from jax.experimental.pallas import tpu as pltpu

F32 = jnp.float32
BF16 = jnp.bfloat16
HIGHEST = lax.Precision.HIGHEST

D_MODEL = 1024
DEPTH = 2
DA_HEADS = 4
DA_WIDTH = D_MODEL // 2
DA_HEAD_DIM = DA_WIDTH // (2 * DA_HEADS)
DA_VAL_DIM = 2 * DA_HEAD_DIM
HG_HEADS = 4
HG_WIDTH = D_MODEL // 4
HG_DIM = HG_WIDTH // HG_HEADS
POOL_WINDOWS = (2, 4, 8, 16)
POOL_WIDTH = D_MODEL // 4
POOL_GROUP = POOL_WIDTH // len(POOL_WINDOWS)
D_MIX = DA_WIDTH + HG_WIDTH + POOL_WIDTH
D_IN = 3 * DA_WIDTH + 4 * HG_WIDTH + POOL_WIDTH
ROPE_THETA = 10000.0
MASK_VALUE = -1e30
TINY = 1e-30
FFN_DIM = 11 * D_MODEL // 4
N_EXPERTS = 8
EXPERT_DIM = 7 * D_MODEL // 2
ALPHA = (2 * DEPTH) ** 0.25
EPS = 1e-5
LOG2_E = math.log2(math.e)

LANES = 128
MXU_DIM = 256
VMEM_LIMIT = 60 * 1024 * 1024

ROW_TILE = 512
ATTN_BLOCK = 512
ATTN_QUERY_GROUP = 256
HG_BLOCK = 512
HG_CHUNK = 16
POOL_HALO = 16
MOE_BLOCK = 2048
MOE_TILE = 192
MOE_CHUNK = 256
MOE_WINDOW = 5
MOE_FSPLIT = 2


def _params(sem, vmem=VMEM_LIMIT):
    return pltpu.CompilerParams(dimension_semantics=sem, vmem_limit_bytes=vmem)


def _layer_norm(y, g, b):
    mu = jnp.mean(y, axis=-1, keepdims=True)
    d = y - mu
    var = jnp.mean(d * d, axis=-1, keepdims=True)
    return d * lax.rsqrt(var + EPS) * g + b


def _silu(x):
    return x * jax.nn.sigmoid(x)


def _head_block_mask(n, group):
    r = lax.broadcasted_iota(jnp.int32, (n, n), 0) // group
    c = lax.broadcasted_iota(jnp.int32, (n, n), 1) // group
    return r == c


def _mod_kernel(c_ref, w_ref, b_ref, o_ref):
    cond = _silu(c_ref[...])
    cond8 = jnp.broadcast_to(cond, (8, D_MODEL))
    r = jnp.dot(cond8, w_ref[0], preferred_element_type=F32, precision=HIGHEST)
    o_ref[0] = r[0:1] + b_ref[0]


def _modulation(c, w_ada, b_ada):
    n = 6 * DEPTH
    b3 = b_ada.reshape(n, 1, D_MODEL)
    out = pl.pallas_call(
        _mod_kernel,
        grid=(DEPTH, 6),
        in_specs=[
            pl.BlockSpec((1, D_MODEL), lambda l, j: (0, 0)),
            pl.BlockSpec((1, D_MODEL, D_MODEL), lambda l, j: (l, 0, j)),
            pl.BlockSpec((1, 1, D_MODEL), lambda l, j: (l * 6 + j, 0, 0)),
        ],
        out_specs=pl.BlockSpec((1, 1, D_MODEL), lambda l, j: (l * 6 + j, 0, 0)),
        out_shape=jax.ShapeDtypeStruct((n, 1, D_MODEL), F32),
        compiler_params=_params(("parallel", "parallel")),
        name="adaln_mod",
    )(c, w_ada, b3)
    return out.reshape(DEPTH, 6, 1, D_MODEL)


def _inproj_kernel(x_ref, sc_ref, sh_ref, w_ref, wvt_ref, cos_ref, sin_ref,
                   qt_ref, k_ref, vt_ref, hg_ref, pu_ref):
    h = (x_ref[...] * (1.0 + sc_ref[...]) + sh_ref[...]).astype(BF16)
    qk = jnp.dot(h, w_ref[:, : 2 * DA_WIDTH], preferred_element_type=F32)
    reps = 2 * DA_WIDTH // LANES
    cos = jnp.tile(cos_ref[...], (1, reps))
    sin = jnp.tile(sin_ref[...], (1, reps))
    lane = lax.broadcasted_iota(jnp.int32, qk.shape, 1)
    half = DA_HEAD_DIM // 2
    first_half = (lane & half) == 0
    width = 2 * DA_WIDTH
    partner = jnp.where(first_half, pltpu.roll(qk, width - half, 1), pltpu.roll(qk, half, 1))
    rot = qk * cos + partner * sin
    qt_ref[0] = (rot[:, :DA_WIDTH] * (DA_HEAD_DIM ** -0.5 * LOG2_E)).T.astype(BF16)
    k_ref[...] = rot[:, DA_WIDTH:].astype(BF16)
    o = 2 * DA_WIDTH
    vt_ref[0] = lax.dot_general(wvt_ref[...], h, (((1,), (1,)), ((), ())),
                                preferred_element_type=F32).astype(BF16)
    o += DA_WIDTH
    hg_ref[...] = jnp.dot(h, w_ref[:, o:o + 4 * HG_WIDTH], preferred_element_type=F32)
    o += 4 * HG_WIDTH
    pu_ref[...] = jnp.dot(h, w_ref[:, o:o + POOL_WIDTH], preferred_element_type=F32)


def _in_projection(x, sc, sh, w_in_bf, cos_t, sin_t):
    S = x.shape[0]
    tm = ATTN_BLOCK
    row = lambda i: (i, 0)
    fixed = lambda i: (0, 0)
    return pl.pallas_call(
        _inproj_kernel,
        grid=(S // tm,),
        in_specs=[
            pl.BlockSpec((tm, D_MODEL), row),
            pl.BlockSpec((1, D_MODEL), fixed),
            pl.BlockSpec((1, D_MODEL), fixed),
            pl.BlockSpec((D_MODEL, D_IN), fixed),
            pl.BlockSpec((DA_WIDTH, D_MODEL), fixed),
            pl.BlockSpec((tm, LANES), row),
            pl.BlockSpec((tm, LANES), row),
        ],
        out_specs=[
            pl.BlockSpec((1, DA_WIDTH, tm), lambda i: (i, 0, 0)),
            pl.BlockSpec((tm, DA_WIDTH), row),
            pl.BlockSpec((1, DA_WIDTH, tm), lambda i: (i, 0, 0)),
            pl.BlockSpec((tm, 4 * HG_WIDTH), row),
            pl.BlockSpec((tm, POOL_WIDTH), row),
        ],
        out_shape=[
            jax.ShapeDtypeStruct((S // tm, DA_WIDTH, tm), BF16),
            jax.ShapeDtypeStruct((S, DA_WIDTH), BF16),
            jax.ShapeDtypeStruct((S // tm, DA_WIDTH, tm), BF16),
            jax.ShapeDtypeStruct((S, 4 * HG_WIDTH), F32),
            jax.ShapeDtypeStruct((S, POOL_WIDTH), F32),
        ],
        compiler_params=_params(("parallel",)),
        name="in_proj",
    )(x, sc, sh, w_in_bf, w_in_bf[:, 2 * DA_WIDTH:3 * DA_WIDTH].T, cos_t, sin_t)


def _attn_kernel(lam_ref, g_ref, qt_ref, k_ref, vt_ref, o_ref, m_scr, l_scr, a_scr, c_scr, acc_scr, s_scr,
                 p_scr, *, lam_init):
    bq = qt_ref.shape[2]
    bk = vt_ref.shape[2]
    cg = ATTN_QUERY_GROUP
    i = pl.program_id(1)
    qt = qt_ref[0]
    rowq = lax.broadcasted_iota(jnp.int32, qt.shape, 0)
    zero = jnp.zeros_like(qt)
    maps = (jnp.where(rowq < DA_HEAD_DIM, qt, zero), jnp.where(rowq >= DA_HEAD_DIM, qt, zero))
    groups = [(a * bq + c0, c0, maps[a][:, c0:c0 + cg]) for a in range(2) for c0 in range(0, bq, cg)]
    m_scr[...] = jnp.full(m_scr.shape, -jnp.inf, F32)
    l_scr[...] = jnp.zeros(l_scr.shape, F32)
    acc_scr[...] = jnp.zeros(acc_scr.shape, F32)
    hk = bk // 2
    key_i = lax.broadcasted_iota(jnp.int32, (hk, cg), 0)
    qry_i = lax.broadcasted_iota(jnp.int32, (hk, cg), 1)
    ng = len(groups)
    last = ng - 1

    def cols_of(g):
        return slice(groups[g][0], groups[g][0] + cg)

    def ccols_of(sset, g):
        return slice(sset * 2 * bq + groups[g][0], sset * 2 * bq + groups[g][0] + cg)

    def mask_mode(key_offset, g):
        c0 = groups[g][1]
        if key_offset + hk - 1 <= c0:
            return None
        if key_offset > c0 + cg - 1:
            return "skip"
        return key_offset

    def score_stage(hb, sset, g, key_offset):
        kb = k_ref[pl.ds(pl.multiple_of(hb * hk, hk), hk), :]
        s = jnp.dot(kb, groups[g][2], preferred_element_type=F32)
        if key_offset is not None:
            s = jnp.where(key_i + key_offset <= qry_i + groups[g][1], s, MASK_VALUE)
        s_scr[sset * ng + g] = s
        c_scr[:, ccols_of(sset, g)] = jnp.max(s, axis=0, keepdims=True)

    def max_stage(sset, g):
        cols = cols_of(g)
        m_prev = m_scr[:, cols]
        m_new = jnp.maximum(m_prev, c_scr[:, ccols_of(sset, g)])
        a_scr[:, cols] = jnp.exp2(m_prev - m_new)
        m_scr[:, cols] = m_new

    def exp_stage(sset, g, vt_half, carried, defer):
        cols = cols_of(g)
        p = jnp.exp2(s_scr[sset * ng + g] - m_scr[:, cols])
        alpha = a_scr[:, cols]
        l_scr[:, cols] = alpha * l_scr[:, cols] + jnp.sum(p, axis=0, keepdims=True)
        pb = p.astype(BF16)
        acc = acc_scr[:, cols]
        if carried is not None:
            acc = acc + carried
        if defer:
            acc_scr[:, cols] = alpha * acc
            p_scr[...] = pb
        else:
            acc_scr[:, cols] = alpha * acc + jnp.dot(vt_half, pb, preferred_element_type=F32)

    def iteration(t, look, diagonal):
        vt_a = vt_ref[t, :, 0:hk]
        vt_b = vt_ref[t, :, hk:bk]
        carried = jnp.dot(vt_ref[jnp.maximum(t - 1, 0), :, hk:bk], p_scr[...],
                          preferred_element_type=F32)
        if diagonal:
            b_items = [(g, mask_mode(hk, g)) for g in range(ng) if mask_mode(hk, g) != "skip"]
        else:
            b_items = [(g, None) for g in range(ng)]
        score_stage(2 * t + 1, 1, *b_items[0])
        for g in range(ng):
            if g + 1 < len(b_items):
                score_stage(2 * t + 1, 1, *b_items[g + 1])
            if g + 1 < ng:
                max_stage(0, g + 1)
            else:
                max_stage(1, b_items[0][0])
            exp_stage(0, g, vt_a, carried if g == last else None, False)
        if not diagonal:
            score_stage(2 * t + 2, 0, 0, mask_mode(0, 0) if look == "diag" else None)
        for n, (g, _) in enumerate(b_items):
            if not diagonal and g + 1 < ng:
                score_stage(2 * t + 2, 0, g + 1, mask_mode(0, g + 1) if look == "diag" else None)
            if n + 1 < len(b_items):
                max_stage(1, b_items[n + 1][0])
            elif not diagonal:
                max_stage(0, 0)
            exp_stage(1, g, vt_b, None, not diagonal and g == last)

    p_scr[...] = jnp.zeros(p_scr.shape, BF16)
    for g in range(ng):
        score_stage(0, 0, g, (0 - i) * bq)
    max_stage(0, 0)

    def body(t, carry):
        iteration(t, None, False)
        return carry

    lax.fori_loop(0, i - 1, body, 0)

    @pl.when(i > 0)
    def _():
        iteration(i - 1, "diag", False)

    iteration(i, None, True)

    lq = lam_ref[...]
    lam = (jnp.exp(jnp.sum(lq[0:1] * lq[1:2], axis=1, keepdims=True))
           - jnp.exp(jnp.sum(lq[2:3] * lq[3:4], axis=1, keepdims=True)) + lam_init)
    o = acc_scr[...] / l_scr[...]
    o = o[:, :bq] - lam * o[:, bq:]
    ms = jnp.mean(o * o, axis=0, keepdims=True)
    o = o * lax.rsqrt(ms + EPS) * g_ref[...] * (1.0 - lam_init)
    o_ref[...] = o.T.astype(BF16)


def _diff_attention(qt, k, vt, lam_qk, attn_g, layer):
    S = k.shape[0]
    bq = bk = ATTN_BLOCK
    nk = S // bk
    lam_init = 0.8 - 0.6 * math.exp(-0.3 * layer)
    return pl.pallas_call(
        functools.partial(_attn_kernel, lam_init=lam_init),
        grid=(DA_HEADS, S // bq),
        in_specs=[
            pl.BlockSpec((4, DA_HEAD_DIM), lambda h, i: (0, 0)),
            pl.BlockSpec((DA_VAL_DIM, 1), lambda h, i: (0, 0)),
            pl.BlockSpec((1, LANES, bq), lambda h, i: (i, h, 0)),
            pl.BlockSpec((S, LANES), lambda h, i: (0, h)),
            pl.BlockSpec((nk, DA_VAL_DIM, bk), lambda h, i: (0, h, 0)),
        ],
        out_specs=pl.BlockSpec((bq, DA_VAL_DIM), lambda h, i: (i, h)),
        out_shape=jax.ShapeDtypeStruct((S, DA_WIDTH), BF16),
        scratch_shapes=[
            pltpu.VMEM((1, 2 * bq), F32),
            pltpu.VMEM((1, 2 * bq), F32),
            pltpu.VMEM((1, 2 * bq), F32),
            pltpu.VMEM((1, 2 * 2 * bq), F32),
            pltpu.VMEM((DA_VAL_DIM, 2 * bq), F32),
            pltpu.VMEM((2 * (2 * bq // ATTN_QUERY_GROUP), bk // 2, ATTN_QUERY_GROUP), F32),
            pltpu.VMEM((bk // 2, ATTN_QUERY_GROUP), BF16),
        ],
        compiler_params=_params(("parallel", "parallel")),
        name="diff_attn",
    )(lam_qk, attn_g.reshape(DA_VAL_DIM, 1), qt, k, vt)


def _hgrn_kernel(lbl_ref, g_ref, hg_ref, o_ref, st_scr, o_scr, *, layer):
    T = hg_ref.shape[0]
    c = HG_CHUNK
    W = HG_WIDTH

    @pl.when(pl.program_id(0) == 0)
    def _():
        st_scr[...] = jnp.zeros(st_scr.shape, F32)

    lg = lbl_ref[...]
    ex = jnp.exp(lg - jnp.max(lg, axis=0, keepdims=True))
    prob = ex / jnp.sum(ex, axis=0, keepdims=True)
    lb = jnp.zeros((1, W), F32)
    for li in range(1, layer + 1):
        lb = lb + prob[li:li + 1]

    same_head = _head_block_mask(W, HG_DIM)
    ones_bd = same_head.astype(BF16)
    rowi = lax.broadcasted_iota(jnp.int32, (c, W), 0)

    def body(s, carry):
        sl = pl.ds(pl.multiple_of(s * c, c), c)
        qf = hg_ref[sl, 0:W]
        z = hg_ref[sl, W:2 * W]
        iv = hg_ref[sl, 2 * W:3 * W]
        f = lb + (1.0 - lb) * jax.nn.sigmoid(z)
        logf = jnp.log(jnp.maximum(f, TINY))
        key = (1.0 - lb) * jax.nn.sigmoid(-z)
        val = _silu(iv)
        b = logf
        sft = 1
        while sft < c:
            b = b + jnp.where(rowi >= sft, pltpu.roll(b, sft, 0), 0.0)
            sft *= 2
        es = [(qf * key).astype(BF16)]
        vrs = [val]
        for d in range(1, c):
            kr = pltpu.roll(key, d, 0)
            br = pltpu.roll(b, d, 0)
            e = jnp.where(rowi >= d, qf * kr * jnp.exp(b - br), 0.0)
            es.append(e.astype(BF16))
            vrs.append(pltpu.roll(val, d, 0))
        estack = jnp.concatenate(es, axis=0)
        r = jnp.dot(estack, ones_bd, preferred_element_type=F32)
        o = r[0:c] * vrs[0]
        for d in range(1, c):
            o = o + r[d * c:(d + 1) * c] * vrs[d]
        st = st_scr[...]
        qd = (qf * jnp.exp(b)).astype(BF16)
        o = o + lax.dot_general(qd, st.astype(BF16), (((1,), (1,)), ((), ())),
                                preferred_element_type=F32)
        o_scr[sl, :] = o
        b_last = b[c - 1:c]
        kd = (key * jnp.exp(b_last - b)).astype(BF16)
        upd = lax.dot_general(val.astype(BF16), kd, (((0,), (0,)), ((), ())),
                              preferred_element_type=F32)
        st_scr[...] = jnp.exp(b_last) * st + jnp.where(same_head, upd, 0.0)
        return carry

    lax.fori_loop(0, T // c, body, 0)

    o = o_scr[...]
    ms = jnp.dot(o * o, same_head.astype(F32), preferred_element_type=F32,
                 precision=HIGHEST) * (1.0 / HG_DIM)
    gate = hg_ref[:, 3 * W:4 * W]
    o_ref[...] = (o * lax.rsqrt(ms + EPS) * g_ref[...] * _silu(gate)).astype(BF16)


def _hgrn2(hg, lb_logits, norm_g, layer):
    S = hg.shape[0]
    T = HG_BLOCK
    g_t = jnp.tile(norm_g.reshape(1, HG_DIM), (1, HG_HEADS))
    return pl.pallas_call(
        functools.partial(_hgrn_kernel, layer=layer),
        grid=(S // T,),
        in_specs=[
            pl.BlockSpec((DEPTH, HG_WIDTH), lambda i: (0, 0)),
            pl.BlockSpec((1, HG_WIDTH), lambda i: (0, 0)),
            pl.BlockSpec((T, 4 * HG_WIDTH), lambda i: (i, 0)),
        ],
        out_specs=pl.BlockSpec((T, HG_WIDTH), lambda i: (i, 0)),
        out_shape=jax.ShapeDtypeStruct((S, HG_WIDTH), BF16),
        scratch_shapes=[
            pltpu.VMEM((HG_WIDTH, HG_WIDTH), F32),
            pltpu.VMEM((T, HG_WIDTH), F32),
        ],
        compiler_params=_params(("arbitrary",)),
        name="hgrn2",
    )(lb_logits, g_t, hg)


def _pool_kernel(u_ref, w_ref, sc_ref, o_ref, halo_scr):
    T = u_ref.shape[0]
    H = POOL_HALO
    blk = pl.program_id(0)

    @pl.when(blk == 0)
    def _():
        halo_scr[...] = jnp.zeros(halo_scr.shape, F32)

    u = u_ref[...]
    ext = jnp.concatenate([halo_scr[...], u], axis=0)
    halo_scr[...] = u[T - H:, :]
    sums = []
    s = ext
    w = 1
    while w < POOL_WINDOWS[-1]:
        s = s + pltpu.roll(s, w, 0)
        w *= 2
        sums.append(s[H:, :])
    t1 = (blk * T + 1 + lax.broadcasted_iota(jnp.int32, (T, POOL_WIDTH), 0)).astype(F32)
    lane = lax.broadcasted_iota(jnp.int32, (T, POOL_WIDTH), 1)
    pooled = None
    for gi, win in enumerate(POOL_WINDOWS):
        mean = sums[gi] / jnp.minimum(t1, float(win))
        pooled = mean if pooled is None else jnp.where(lane >= gi * POOL_GROUP, mean, pooled)
    pooled = pooled - u
    y = jnp.dot(pooled.astype(BF16), w_ref[...], preferred_element_type=F32)
    o_ref[...] = (y * sc_ref[...]).astype(BF16)


def _multiscale_pool(pu, pool_w, pool_scale):
    S = pu.shape[0]
    T = ROW_TILE
    w_bd = jax.scipy.linalg.block_diag(*[pool_w[g] for g in range(len(POOL_WINDOWS))]).astype(BF16)
    return pl.pallas_call(
        _pool_kernel,
        grid=(S // T,),
        in_specs=[
            pl.BlockSpec((T, POOL_WIDTH), lambda i: (i, 0)),
            pl.BlockSpec((POOL_WIDTH, POOL_WIDTH), lambda i: (0, 0)),
            pl.BlockSpec((1, POOL_WIDTH), lambda i: (0, 0)),
        ],
        out_specs=pl.BlockSpec((T, POOL_WIDTH), lambda i: (i, 0)),
        out_shape=jax.ShapeDtypeStruct((S, POOL_WIDTH), BF16),
        scratch_shapes=[pltpu.VMEM((POOL_HALO, POOL_WIDTH), F32)],
        compiler_params=_params(("arbitrary",)),
        name="ms_pool",
    )(pu, w_bd, pool_scale.reshape(1, POOL_WIDTH))


def _outproj_kernel(a_ref, r_ref, p_ref, x_ref, w_ref, g1_ref, lg_ref, lb_ref, o_ref):
    m = jnp.dot(a_ref[...], w_ref[0:DA_WIDTH, :], preferred_element_type=F32)
    m += jnp.dot(r_ref[...], w_ref[DA_WIDTH:DA_WIDTH + HG_WIDTH, :], preferred_element_type=F32)
    m += jnp.dot(p_ref[...], w_ref[DA_WIDTH + HG_WIDTH:, :], preferred_element_type=F32)
    y = ALPHA * x_ref[...] + (1.0 + g1_ref[...]) * m
    o_ref[...] = _layer_norm(y, lg_ref[...], lb_ref[...])


def _out_projection(a, r, p, x, w_out_bf, g1, ln_g, ln_b):
    S = x.shape[0]
    tm = ROW_TILE
    row = lambda i: (i, 0)
    fixed = lambda i: (0, 0)
    vec = pl.BlockSpec((1, D_MODEL), fixed)
    return pl.pallas_call(
        _outproj_kernel,
        grid=(S // tm,),
        in_specs=[
            pl.BlockSpec((tm, DA_WIDTH), row),
            pl.BlockSpec((tm, HG_WIDTH), row),
            pl.BlockSpec((tm, POOL_WIDTH), row),
            pl.BlockSpec((tm, D_MODEL), row),
            pl.BlockSpec((D_MIX, D_MODEL), fixed),
            vec, vec, vec,
        ],
        out_specs=pl.BlockSpec((tm, D_MODEL), row),
        out_shape=jax.ShapeDtypeStruct((S, D_MODEL), F32),
        compiler_params=_params(("parallel",)),
        name="out_proj_ln",
    )(a, r, p, x, w_out_bf, g1, ln_g.reshape(1, D_MODEL), ln_b.reshape(1, D_MODEL))


def _ffn_kernel(x_ref, sc_ref, sh_ref, g2_ref, w1_ref, w3_ref, w2_ref, lg_ref, lb_ref, o_ref):
    x = x_ref[...]
    h = (x * (1.0 + sc_ref[...]) + sh_ref[...]).astype(BF16)
    acc = jnp.zeros(x.shape, F32)
    step = 4 * MXU_DIM
    for lo in range(0, FFN_DIM, step):
        hi = min(lo + step, FFN_DIM)
        a = jnp.dot(h, w1_ref[:, lo:hi], preferred_element_type=F32)
        b = jnp.dot(h, w3_ref[:, lo:hi], preferred_element_type=F32)
        acc += jnp.dot((_silu(a) * b).astype(BF16), w2_ref[lo:hi, :], preferred_element_type=F32)
    y = ALPHA * x + (1.0 + g2_ref[...]) * acc
    o_ref[...] = _layer_norm(y, lg_ref[...], lb_ref[...])


def _dense_ffn(x, sc, sh, g2, w1, w3, w2, ln_g, ln_b):
    S = x.shape[0]
    tm = ROW_TILE
    row = lambda i: (i, 0)
    fixed = lambda i: (0, 0)
    vec = pl.BlockSpec((1, D_MODEL), fixed)
    once = pl.Buffered(1)
    return pl.pallas_call(
        _ffn_kernel,
        grid=(S // tm,),
        in_specs=[
            pl.BlockSpec((tm, D_MODEL), row),
            vec, vec, vec,
            pl.BlockSpec((D_MODEL, FFN_DIM), fixed, pipeline_mode=once),
            pl.BlockSpec((D_MODEL, FFN_DIM), fixed, pipeline_mode=once),
            pl.BlockSpec((FFN_DIM, D_MODEL), fixed, pipeline_mode=once),
            vec, vec,
        ],
        out_specs=pl.BlockSpec((tm, D_MODEL), row),
        out_shape=jax.ShapeDtypeStruct((S, D_MODEL), F32),
        compiler_params=_params(("parallel",)),
        name="dense_ffn_ln",
    )(x, sc, sh, g2, w1, w3, w2, ln_g.reshape(1, D_MODEL), ln_b.reshape(1, D_MODEL))


def _router_kernel(x_ref, sc_ref, sh_ref, rw_ref, h_ref, rank_ref, comb_ref, cum_ref):
    T = x_ref.shape[0]
    tc = MOE_CHUNK
    E = N_EXPERTS
    h = x_ref[...] * (1.0 + sc_ref[...]) + sh_ref[...]
    h_ref[...] = h.astype(BF16)
    logits = lax.dot_general(rw_ref[...], h, (((1,), (1,)), ((), ())),
                             preferred_element_type=F32, precision=HIGHEST)
    eidx = lax.broadcasted_iota(jnp.int32, (E, T), 0)
    v1 = jnp.max(logits, axis=0, keepdims=True)
    i1 = jnp.min(jnp.where(logits == v1, eidx, E), axis=0, keepdims=True)
    m1 = eidx == i1
    rest = jnp.where(m1, -jnp.inf, logits)
    v2 = jnp.max(rest, axis=0, keepdims=True)
    i2 = jnp.min(jnp.where(rest == v2, eidx, E), axis=0, keepdims=True)
    m2 = eidx == i2
    e2 = jnp.exp(v2 - v1)
    g1 = 1.0 / (1.0 + e2)
    g2 = e2 / (1.0 + e2)
    comb = jnp.where(m1, g1, 0.0) + jnp.where(m2, g2, 0.0)
    routed = jnp.logical_or(m1, m2)
    ind = routed.astype(BF16)
    r_i = lax.broadcasted_iota(jnp.int32, (tc, tc), 0)
    c_i = lax.broadcasted_iota(jnp.int32, (tc, tc), 1)
    strict = (r_i < c_i).astype(BF16)
    lane = lax.broadcasted_iota(jnp.int32, (E, LANES), 1)
    running = jnp.zeros((E, 1), F32)
    cum = jnp.zeros((E, LANES), F32)
    for ci in range(T // tc):
        sl = slice(ci * tc, (ci + 1) * tc)
        ind_c = ind[:, sl]
        rank_c = jnp.dot(ind_c, strict, preferred_element_type=F32) + running
        rank_ref[:, ci] = jnp.where(routed[:, sl], rank_c, -1.0).reshape(E, 1, tc)
        comb_ref[:, ci] = comb[:, sl].reshape(E, 1, tc)
        cum = jnp.where(lane == ci, running, cum)
        running = running + jnp.sum(ind_c.astype(F32), axis=1, keepdims=True)
    cum = jnp.where(lane == T // tc, running, cum)
    cum_ref[0] = cum.astype(jnp.int32)


def _router(x, sc, sh, router_w):
    S = x.shape[0]
    T = MOE_BLOCK
    E = N_EXPERTS
    fixed = lambda i: (0, 0)
    vec = pl.BlockSpec((1, D_MODEL), fixed)
    return pl.pallas_call(
        _router_kernel,
        grid=(S // T,),
        in_specs=[
            pl.BlockSpec((T, D_MODEL), lambda i: (i, 0)),
            vec, vec,
            pl.BlockSpec((E, D_MODEL), fixed),
        ],
        out_specs=[
            pl.BlockSpec((T, D_MODEL), lambda i: (i, 0)),
            pl.BlockSpec((E, T // MOE_CHUNK, 1, MOE_CHUNK), lambda i: (0, i, 0, 0)),
            pl.BlockSpec((E, T // MOE_CHUNK, 1, MOE_CHUNK), lambda i: (0, i, 0, 0)),
            pl.BlockSpec((1, E, LANES), lambda i: (i, 0, 0)),
        ],
        out_shape=[
            jax.ShapeDtypeStruct((S, D_MODEL), BF16),
            jax.ShapeDtypeStruct((E, S // MOE_CHUNK, 1, MOE_CHUNK), F32),
            jax.ShapeDtypeStruct((E, S // MOE_CHUNK, 1, MOE_CHUNK), F32),
            jax.ShapeDtypeStruct((S // T, E, LANES), jnp.int32),
        ],
        compiler_params=_params(("parallel",)),
        name="moe_router",
    )(x, sc, sh, router_w.T)


def _moe_kernel(cum_ref, h_ref, rank_ref, comb_ref, w1_ref, w3_ref, w2_ref, y_ref, xe_scr, gate_scr):
    T = h_ref.shape[0]
    tr, tc, win = MOE_TILE, MOE_CHUNK, MOE_WINDOW
    nchunk = T // tc
    b = pl.program_id(0)
    e = pl.program_id(1)
    f = pl.program_id(2)

    @pl.when(jnp.logical_and(e == 0, f == 0))
    def _():
        y_ref[...] = jnp.zeros(y_ref.shape, F32)

    base = (b * N_EXPERTS + e) * LANES
    count = cum_ref[base + nchunk]
    ntiles = (count + (tr - 1)) // tr

    def token_row(ref, first, n):
        blk = ref[0, pl.ds(first, n)]
        return jnp.concatenate([blk[c] for c in range(n)], axis=1)

    def tile(r, carry):
        lo = r * tr
        rows = pl.ds(pl.multiple_of(lo, 16), tr)

        def hits(first, n):
            want = (lo + lax.broadcasted_iota(jnp.int32, (tr, n * tc), 0)).astype(F32)
            return token_row(rank_ref, first, n) == want

        @pl.when(f == 0)
        def _():
            hit = hits(0, nchunk)
            xe_scr[rows, :] = jnp.dot(hit.astype(BF16), h_ref[...],
                                      preferred_element_type=F32).astype(BF16)
            gate_scr[rows, :] = jnp.sum(jnp.where(hit, token_row(comb_ref, 0, nchunk), 0.0),
                                        axis=1, keepdims=True)

        xe = xe_scr[rows, :]
        a = jnp.dot(xe, w1_ref[0], preferred_element_type=F32)
        g = jnp.dot(xe, w3_ref[0], preferred_element_type=F32)
        mid = (_silu(a) * g).astype(BF16)
        out = jnp.dot(mid, w2_ref[0], preferred_element_type=F32) * gate_scr[rows, :]
        out_bf = out.astype(BF16)

        def scatter(first, n):
            start = first * tc
            if not isinstance(start, int):
                start = pl.multiple_of(start, tc)
            y_ref[pl.ds(start, n * tc), :] += lax.dot_general(
                hits(first, n).astype(BF16), out_bf, (((0,), (0,)), ((), ())),
                preferred_element_type=F32)

        c_lo = jnp.int32(0)
        c_hi = jnp.int32(-1)
        for c in range(nchunk):
            c_lo += (cum_ref[base + c + 1] <= lo).astype(jnp.int32)
            c_hi += (cum_ref[base + c] < lo + tr).astype(jnp.int32)
        fits = c_hi - c_lo < win

        @pl.when(fits)
        def _():
            scatter(jnp.minimum(c_lo, nchunk - win), win)

        @pl.when(jnp.logical_not(fits))
        def _():
            scatter(0, nchunk)

        return carry

    lax.fori_loop(0, ntiles, tile, 0)


def _moe_ffn(h_bf, rank, comb, cum, w1, w3, w2):
    S = h_bf.shape[0]
    T = MOE_BLOCK
    fw = EXPERT_DIM // MOE_FSPLIT
    grid_spec = pltpu.PrefetchScalarGridSpec(
        num_scalar_prefetch=1,
        grid=(S // T, N_EXPERTS, MOE_FSPLIT),
        in_specs=[
            pl.BlockSpec((T, D_MODEL), lambda b, e, f, c: (b, 0)),
            pl.BlockSpec((1, T // MOE_CHUNK, 1, MOE_CHUNK), lambda b, e, f, c: (e, b, 0, 0)),
            pl.BlockSpec((1, T // MOE_CHUNK, 1, MOE_CHUNK), lambda b, e, f, c: (e, b, 0, 0)),
            pl.BlockSpec((1, D_MODEL, fw), lambda b, e, f, c: (e, 0, f)),
            pl.BlockSpec((1, D_MODEL, fw), lambda b, e, f, c: (e, 0, f)),
            pl.BlockSpec((1, fw, D_MODEL), lambda b, e, f, c: (e, f, 0)),
        ],
        out_specs=pl.BlockSpec((T, D_MODEL), lambda b, e, f, c: (b, 0)),
        scratch_shapes=[
            pltpu.VMEM((T + MOE_TILE, D_MODEL), BF16),
            pltpu.VMEM((T + MOE_TILE, 1), F32),
        ],
    )
    return pl.pallas_call(
        _moe_kernel,
        grid_spec=grid_spec,
        out_shape=jax.ShapeDtypeStruct((S, D_MODEL), F32),
        compiler_params=_params(("arbitrary", "arbitrary", "arbitrary")),
        name="moe_experts",
    )(cum.reshape(-1), h_bf, rank, comb, w1, w3, w2)


def _ln_kernel(x_ref, y_ref, g2_ref, lg_ref, lb_ref, o_ref):
    y = ALPHA * x_ref[...] + (1.0 + g2_ref[...]) * y_ref[...]
    o_ref[...] = _layer_norm(y, lg_ref[...], lb_ref[...])


def _residual_ln(x, y, g2, ln_g, ln_b):
    S = x.shape[0]
    tm = ROW_TILE
    row = lambda i: (i, 0)
    vec = pl.BlockSpec((1, D_MODEL), lambda i: (0, 0))
    return pl.pallas_call(
        _ln_kernel,
        grid=(S // tm,),
        in_specs=[pl.BlockSpec((tm, D_MODEL), row), pl.BlockSpec((tm, D_MODEL), row), vec, vec, vec],
        out_specs=pl.BlockSpec((tm, D_MODEL), row),
        out_shape=jax.ShapeDtypeStruct((S, D_MODEL), F32),
        compiler_params=_params(("parallel",)),
        name="residual_ln",
    )(x, y, g2, ln_g.reshape(1, D_MODEL), ln_b.reshape(1, D_MODEL))


def _rope_tables(S):
    d = DA_HEAD_DIM
    inv_freq = 1.0 / (ROPE_THETA ** (jnp.arange(0, d, 2, dtype=F32) / d))
    ang = jnp.arange(S, dtype=F32)[:, None] * inv_freq[None, :]
    cos = jnp.cos(ang)
    sin = jnp.sin(ang)
    reps = LANES // d
    cos_t = jnp.tile(jnp.concatenate([cos, cos], axis=1), (1, reps))
    sin_t = jnp.tile(jnp.concatenate([-sin, sin], axis=1), (1, reps))
    return cos_t, sin_t


def kernel(x, c, w_ada, b_ada, w_in, lam_qk, attn_norm_g, hg_lb_logits, hg_norm_g, pool_w, pool_scale, w_out, ln1_g, ln1_b, ln2_g, ln2_b, ffn_w1, ffn_w3, ffn_w2, router_w, exp_w1, exp_w3, exp_w2):
    B, S, D = x.shape
    assert B == 1 and D == D_MODEL and S % MOE_BLOCK == 0
    xs = x.reshape(S, D)
    mod = _modulation(c, w_ada, b_ada)
    cos_t, sin_t = _rope_tables(S)
    for l in range(DEPTH):
        sh1, sc1, g1, sh2, sc2, g2 = (mod[l, j] for j in range(6))
        qt, k, vt, hg, pu = _in_projection(xs, sc1, sh1, w_in[l].astype(BF16), cos_t, sin_t)
        a = _diff_attention(qt, k, vt, lam_qk[l], attn_norm_g[l], l)
        r = _hgrn2(hg, hg_lb_logits, hg_norm_g[l], l)
        p = _multiscale_pool(pu, pool_w[l], pool_scale[l])
        xs = _out_projection(a, r, p, xs, w_out[l].astype(BF16), g1, ln1_g[l], ln1_b[l])
        if l % 2 == 0:
            i = l // 2
            xs = _dense_ffn(xs, sc2, sh2, g2, ffn_w1[i].astype(BF16), ffn_w3[i].astype(BF16),
                            ffn_w2[i].astype(BF16), ln2_g[l], ln2_b[l])
        else:
            i = l // 2
            h_bf, rank, comb, cum = _router(xs, sc2, sh2, router_w[i])
            y = _moe_ffn(h_bf, rank, comb, cum, exp_w1[i].astype(BF16), exp_w3[i].astype(BF16),
                         exp_w2[i].astype(BF16))
            xs = _residual_ln(xs, y, g2, ln2_g[l], ln2_b[l])
    return xs.reshape(B, S, D)
```

```python
import functools
import math

import jax
import jax.numpy as jnp
from jax import lax
from jax.experimental import pallas as pl
from jax.experimental.pallas import tpu as pltpu

F32 = jnp.float32
BF16 = jnp.bfloat16
HIGHEST = lax.Precision.HIGHEST

D_MODEL = 1024
DEPTH = 2
DA_HEADS = 4
DA_WIDTH = D_MODEL // 2
DA_HEAD_DIM = DA_WIDTH // (2 * DA_HEADS)
DA_VAL_DIM = 2 * DA_HEAD_DIM
HG_HEADS = 4
HG_WIDTH = D_MODEL // 4
HG_DIM = HG_WIDTH // HG_HEADS
POOL_WINDOWS = (2, 4, 8, 16)
POOL_WIDTH = D_MODEL // 4
POOL_GROUP = POOL_WIDTH // len(POOL_WINDOWS)
D_MIX = DA_WIDTH + HG_WIDTH + POOL_WIDTH
D_IN = 3 * DA_WIDTH + 4 * HG_WIDTH + POOL_WIDTH
ROPE_THETA = 10000.0
MASK_VALUE = -1e30
TINY = 1e-30
FFN_DIM = 11 * D_MODEL // 4
N_EXPERTS = 8
EXPERT_DIM = 7 * D_MODEL // 2
ALPHA = (2 * DEPTH) ** 0.25
EPS = 1e-5
LOG2_E = math.log2(math.e)

LANES = 128
MXU_DIM = 256
VMEM_LIMIT = 60 * 1024 * 1024

ROW_TILE = 512
ATTN_BLOCK = 512
ATTN_QUERY_GROUP = 256
ATTN_SUM_ROWS = 16
ATTN_VT_ROWS = DA_VAL_DIM + ATTN_SUM_ROWS
HG_BLOCK = 512
HG_CHUNK = 16
POOL_HALO = 16
MOE_BLOCK = 2048
MOE_TILE = 192
MOE_CHUNK = 256
MOE_WINDOW = 5
MOE_FSPLIT = 2


def _params(sem, vmem=VMEM_LIMIT):
    return pltpu.CompilerParams(dimension_semantics=sem, vmem_limit_bytes=vmem)


def _layer_norm(y, g, b):
    mu = jnp.mean(y, axis=-1, keepdims=True)
    d = y - mu
    var = jnp.mean(d * d, axis=-1, keepdims=True)
    return d * lax.rsqrt(var + EPS) * g + b


def _silu(x):
    return x * jax.nn.sigmoid(x)


def _head_block_mask(n, group):
    r = lax.broadcasted_iota(jnp.int32, (n, n), 0) // group
    c = lax.broadcasted_iota(jnp.int32, (n, n), 1) // group
    return r == c


def _mod_kernel(c_ref, w_ref, b_ref, o_ref):
    cond = _silu(c_ref[...])
    cond8 = jnp.broadcast_to(cond, (8, D_MODEL))
    r = jnp.dot(cond8, w_ref[0], preferred_element_type=F32, precision=HIGHEST)
    o_ref[0] = r[0:1] + b_ref[0]


def _modulation(c, w_ada, b_ada):
    n = 6 * DEPTH
    b3 = b_ada.reshape(n, 1, D_MODEL)
    out = pl.pallas_call(
        _mod_kernel,
        grid=(DEPTH, 6),
        in_specs=[
            pl.BlockSpec((1, D_MODEL), lambda l, j: (0, 0)),
            pl.BlockSpec((1, D_MODEL, D_MODEL), lambda l, j: (l, 0, j)),
            pl.BlockSpec((1, 1, D_MODEL), lambda l, j: (l * 6 + j, 0, 0)),
        ],
        out_specs=pl.BlockSpec((1, 1, D_MODEL), lambda l, j: (l * 6 + j, 0, 0)),
        out_shape=jax.ShapeDtypeStruct((n, 1, D_MODEL), F32),
        compiler_params=_params(("parallel", "parallel")),
        name="adaln_mod",
    )(c, w_ada, b3)
    return out.reshape(DEPTH, 6, 1, D_MODEL)


def _inproj_kernel(x_ref, sc_ref, sh_ref, w_ref, wvt_ref, cos_ref, sin_ref,
                   qt_ref, k_ref, vt_ref, hg_ref, pu_ref):
    h = (x_ref[...] * (1.0 + sc_ref[...]) + sh_ref[...]).astype(BF16)
    qk = jnp.dot(h, w_ref[:, : 2 * DA_WIDTH], preferred_element_type=F32)
    reps = 2 * DA_WIDTH // LANES
    cos = jnp.tile(cos_ref[...], (1, reps))
    sin = jnp.tile(sin_ref[...], (1, reps))
    lane = lax.broadcasted_iota(jnp.int32, qk.shape, 1)
    half = DA_HEAD_DIM // 2
    first_half = (lane & half) == 0
    width = 2 * DA_WIDTH
    partner = jnp.where(first_half, pltpu.roll(qk, width - half, 1), pltpu.roll(qk, half, 1))
    rot = qk * cos + partner * sin
    qt_ref[0] = (rot[:, :DA_WIDTH] * (DA_HEAD_DIM ** -0.5 * LOG2_E)).T.astype(BF16)
    k_ref[...] = rot[:, DA_WIDTH:].astype(BF16)
    o = 2 * DA_WIDTH
    vt = lax.dot_general(wvt_ref[...], h, (((1,), (1,)), ((), ())),
                         preferred_element_type=F32).astype(BF16)
    ones = jnp.ones((ATTN_SUM_ROWS, vt.shape[1]), BF16)
    pieces = []
    for hd in range(DA_HEADS):
        pieces += [vt[hd * DA_VAL_DIM:(hd + 1) * DA_VAL_DIM], ones]
    vt_ref[0] = jnp.concatenate(pieces, axis=0)
    o += DA_WIDTH
    hg_ref[...] = jnp.dot(h, w_ref[:, o:o + 4 * HG_WIDTH], preferred_element_type=F32)
    o += 4 * HG_WIDTH
    pu_ref[...] = jnp.dot(h, w_ref[:, o:o + POOL_WIDTH], preferred_element_type=F32)


def _in_projection(x, sc, sh, w_in_bf, cos_t, sin_t):
    S = x.shape[0]
    tm = ATTN_BLOCK
    row = lambda i: (i, 0)
    fixed = lambda i: (0, 0)
    return pl.pallas_call(
        _inproj_kernel,
        grid=(S // tm,),
        in_specs=[
            pl.BlockSpec((tm, D_MODEL), row),
            pl.BlockSpec((1, D_MODEL), fixed),
            pl.BlockSpec((1, D_MODEL), fixed),
            pl.BlockSpec((D_MODEL, D_IN), fixed),
            pl.BlockSpec((DA_WIDTH, D_MODEL), fixed),
            pl.BlockSpec((tm, LANES), row),
            pl.BlockSpec((tm, LANES), row),
        ],
        out_specs=[
            pl.BlockSpec((1, DA_WIDTH, tm), lambda i: (i, 0, 0)),
            pl.BlockSpec((tm, DA_WIDTH), row),
            pl.BlockSpec((1, DA_HEADS * ATTN_VT_ROWS, tm), lambda i: (i, 0, 0)),
            pl.BlockSpec((tm, 4 * HG_WIDTH), row),
            pl.BlockSpec((tm, POOL_WIDTH), row),
        ],
        out_shape=[
            jax.ShapeDtypeStruct((S // tm, DA_WIDTH, tm), BF16),
            jax.ShapeDtypeStruct((S, DA_WIDTH), BF16),
            jax.ShapeDtypeStruct((S // tm, DA_HEADS * ATTN_VT_ROWS, tm), BF16),
            jax.ShapeDtypeStruct((S, 4 * HG_WIDTH), F32),
            jax.ShapeDtypeStruct((S, POOL_WIDTH), F32),
        ],
        compiler_params=_params(("parallel",)),
        name="in_proj",
    )(x, sc, sh, w_in_bf, w_in_bf[:, 2 * DA_WIDTH:3 * DA_WIDTH].T, cos_t, sin_t)


def _attn_kernel(lam_ref, g_ref, qt_ref, k_ref, vt_ref, o_ref, m_scr, a_scr, c_scr, acc_scr, s_scr,
                 p_scr, *, lam_init):
    bq = qt_ref.shape[2]
    bk = vt_ref.shape[2]
    cg = ATTN_QUERY_GROUP
    i = pl.program_id(1)
    qt = qt_ref[0]
    rowq = lax.broadcasted_iota(jnp.int32, qt.shape, 0)
    zero = jnp.zeros_like(qt)
    maps = (jnp.where(rowq < DA_HEAD_DIM, qt, zero), jnp.where(rowq >= DA_HEAD_DIM, qt, zero))
    groups = [(a * bq + c0, c0, maps[a][:, c0:c0 + cg]) for a in range(2) for c0 in range(0, bq, cg)]
    m_scr[...] = jnp.full(m_scr.shape, -jnp.inf, F32)
    acc_scr[...] = jnp.zeros(acc_scr.shape, F32)
    hk = bk // 2
    key_i = lax.broadcasted_iota(jnp.int32, (hk, cg), 0)
    qry_i = lax.broadcasted_iota(jnp.int32, (hk, cg), 1)
    ng = len(groups)
    last = ng - 1

    def cols_of(g):
        return slice(groups[g][0], groups[g][0] + cg)

    def ccols_of(sset, g):
        return slice(sset * 2 * bq + groups[g][0], sset * 2 * bq + groups[g][0] + cg)

    def mask_mode(key_offset, g):
        c0 = groups[g][1]
        if key_offset + hk - 1 <= c0:
            return None
        if key_offset > c0 + cg - 1:
            return "skip"
        return key_offset

    def score_stage(hb, sset, g, key_offset):
        kb = k_ref[pl.ds(pl.multiple_of(hb * hk, hk), hk), :]
        s = jnp.dot(kb, groups[g][2], preferred_element_type=F32)
        if key_offset is not None:
            s = jnp.where(key_i + key_offset <= qry_i + groups[g][1], s, MASK_VALUE)
        s_scr[sset * ng + g] = s
        c_scr[:, ccols_of(sset, g)] = jnp.max(s, axis=0, keepdims=True)

    def max_stage(sset, g):
        cols = cols_of(g)
        m_prev = m_scr[:, cols]
        m_new = jnp.maximum(m_prev, c_scr[:, ccols_of(sset, g)])
        a_scr[:, cols] = jnp.exp2(m_prev - m_new)
        m_scr[:, cols] = m_new

    def exp_stage(sset, g, vt_half, carried, defer):
        cols = cols_of(g)
        p = jnp.exp2(s_scr[sset * ng + g] - m_scr[:, cols])
        alpha = a_scr[:, cols]
        pb = p.astype(BF16)
        acc = acc_scr[:, cols]
        if carried is not None:
            acc = acc + carried
        if defer:
            acc_scr[:, cols] = alpha * acc
            p_scr[...] = pb
        else:
            acc_scr[:, cols] = alpha * acc + jnp.dot(vt_half, pb, preferred_element_type=F32)

    def iteration(t, look, diagonal):
        vt_a = vt_ref[t, :, 0:hk]
        vt_b = vt_ref[t, :, hk:bk]
        carried = jnp.dot(vt_ref[jnp.maximum(t - 1, 0), :, hk:bk], p_scr[...],
                          preferred_element_type=F32)
        if diagonal:
            b_items = [(g, mask_mode(hk, g)) for g in range(ng) if mask_mode(hk, g) != "skip"]
        else:
            b_items = [(g, None) for g in range(ng)]
        score_stage(2 * t + 1, 1, *b_items[0])
        for g in range(ng):
            if g + 1 < len(b_items):
                score_stage(2 * t + 1, 1, *b_items[g + 1])
            if g + 1 < ng:
                max_stage(0, g + 1)
            else:
                max_stage(1, b_items[0][0])
            exp_stage(0, g, vt_a, carried if g == last else None, False)
        if not diagonal:
            score_stage(2 * t + 2, 0, 0, mask_mode(0, 0) if look == "diag" else None)
        for n, (g, _) in enumerate(b_items):
            if not diagonal and g + 1 < ng:
                score_stage(2 * t + 2, 0, g + 1, mask_mode(0, g + 1) if look == "diag" else None)
            if n + 1 < len(b_items):
                max_stage(1, b_items[n + 1][0])
            elif not diagonal:
                max_stage(0, 0)
            exp_stage(1, g, vt_b, None, not diagonal and g == last)

    p_scr[...] = jnp.zeros(p_scr.shape, BF16)
    for g in range(ng):
        score_stage(0, 0, g, (0 - i) * bq)
    max_stage(0, 0)

    def body(t, carry):
        iteration(t, None, False)
        return carry

    lax.fori_loop(0, i - 1, body, 0)

    @pl.when(i > 0)
    def _():
        iteration(i - 1, "diag", False)

    iteration(i, None, True)

    lq = lam_ref[...]
    lam = (jnp.exp(jnp.sum(lq[0:1] * lq[1:2], axis=1, keepdims=True))
           - jnp.exp(jnp.sum(lq[2:3] * lq[3:4], axis=1, keepdims=True)) + lam_init)
    o = acc_scr[0:DA_VAL_DIM, :] / acc_scr[DA_VAL_DIM:DA_VAL_DIM + 1, :]
    o = o[:, :bq] - lam * o[:, bq:]
    ms = jnp.mean(o * o, axis=0, keepdims=True)
    o = o * lax.rsqrt(ms + EPS) * g_ref[...] * (1.0 - lam_init)
    o_ref[...] = o.T.astype(BF16)


def _diff_attention(qt, k, vt, lam_qk, attn_g, layer):
    S = k.shape[0]
    bq = bk = ATTN_BLOCK
    nk = S // bk
    lam_init = 0.8 - 0.6 * math.exp(-0.3 * layer)
    return pl.pallas_call(
        functools.partial(_attn_kernel, lam_init=lam_init),
        grid=(DA_HEADS, S // bq),
        in_specs=[
            pl.BlockSpec((4, DA_HEAD_DIM), lambda h, i: (0, 0)),
            pl.BlockSpec((DA_VAL_DIM, 1), lambda h, i: (0, 0)),
            pl.BlockSpec((1, LANES, bq), lambda h, i: (i, h, 0)),
            pl.BlockSpec((S, LANES), lambda h, i: (0, h)),
            pl.BlockSpec((nk, ATTN_VT_ROWS, bk), lambda h, i: (0, h, 0)),
        ],
        out_specs=pl.BlockSpec((bq, DA_VAL_DIM), lambda h, i: (i, h)),
        out_shape=jax.ShapeDtypeStruct((S, DA_WIDTH), BF16),
        scratch_shapes=[
            pltpu.VMEM((1, 2 * bq), F32),
            pltpu.VMEM((1, 2 * bq), F32),
            pltpu.VMEM((1, 2 * 2 * bq), F32),
            pltpu.VMEM((ATTN_VT_ROWS, 2 * bq), F32),
            pltpu.VMEM((2 * (2 * bq // ATTN_QUERY_GROUP), bk // 2, ATTN_QUERY_GROUP), F32),
            pltpu.VMEM((bk // 2, ATTN_QUERY_GROUP), BF16),
        ],
        compiler_params=_params(("parallel", "parallel")),
        name="diff_attn",
    )(lam_qk, attn_g.reshape(DA_VAL_DIM, 1), qt, k, vt)


def _hgrn_kernel(lbl_ref, g_ref, hg_ref, o_ref, st_scr, o_scr, *, layer):
    T = hg_ref.shape[0]
    c = HG_CHUNK
    W = HG_WIDTH

    @pl.when(pl.program_id(0) == 0)
    def _():
        st_scr[...] = jnp.zeros(st_scr.shape, F32)

    lg = lbl_ref[...]
    ex = jnp.exp(lg - jnp.max(lg, axis=0, keepdims=True))
    prob = ex / jnp.sum(ex, axis=0, keepdims=True)
    lb = jnp.zeros((1, W), F32)
    for li in range(1, layer + 1):
        lb = lb + prob[li:li + 1]

    same_head = _head_block_mask(W, HG_DIM)
    ones_bd = same_head.astype(BF16)
    rowi = lax.broadcasted_iota(jnp.int32, (c, W), 0)

    def body(s, carry):
        sl = pl.ds(pl.multiple_of(s * c, c), c)
        qf = hg_ref[sl, 0:W]
        z = hg_ref[sl, W:2 * W]
        iv = hg_ref[sl, 2 * W:3 * W]
        f = lb + (1.0 - lb) * jax.nn.sigmoid(z)
        logf = jnp.log(jnp.maximum(f, TINY))
        key = (1.0 - lb) * jax.nn.sigmoid(-z)
        val = _silu(iv)
        b = logf
        sft = 1
        while sft < c:
            b = b + jnp.where(rowi >= sft, pltpu.roll(b, sft, 0), 0.0)
            sft *= 2
        es = [(qf * key).astype(BF16)]
        vrs = [val]
        for d in range(1, c):
            kr = pltpu.roll(key, d, 0)
            br = pltpu.roll(b, d, 0)
            e = jnp.where(rowi >= d, qf * kr * jnp.exp(b - br), 0.0)
            es.append(e.astype(BF16))
            vrs.append(pltpu.roll(val, d, 0))
        estack = jnp.concatenate(es, axis=0)
        r = jnp.dot(estack, ones_bd, preferred_element_type=F32)
        o = r[0:c] * vrs[0]
        for d in range(1, c):
            o = o + r[d * c:(d + 1) * c] * vrs[d]
        st = st_scr[...]
        qd = (qf * jnp.exp(b)).astype(BF16)
        o = o + lax.dot_general(qd, st.astype(BF16), (((1,), (1,)), ((), ())),
                                preferred_element_type=F32)
        o_scr[sl, :] = o
        b_last = b[c - 1:c]
        kd = (key * jnp.exp(b_last - b)).astype(BF16)
        upd = lax.dot_general(val.astype(BF16), kd, (((0,), (0,)), ((), ())),
                              preferred_element_type=F32)
        st_scr[...] = jnp.exp(b_last) * st + jnp.where(same_head, upd, 0.0)
        return carry

    lax.fori_loop(0, T // c, body, 0)

    o = o_scr[...]
    ms = jnp.dot(o * o, same_head.astype(F32), preferred_element_type=F32,
                 precision=HIGHEST) * (1.0 / HG_DIM)
    gate = hg_ref[:, 3 * W:4 * W]
    o_ref[...] = (o * lax.rsqrt(ms + EPS) * g_ref[...] * _silu(gate)).astype(BF16)


def _hgrn2(hg, lb_logits, norm_g, layer):
    S = hg.shape[0]
    T = HG_BLOCK
    g_t = jnp.tile(norm_g.reshape(1, HG_DIM), (1, HG_HEADS))
    return pl.pallas_call(
        functools.partial(_hgrn_kernel, layer=layer),
        grid=(S // T,),
        in_specs=[
            pl.BlockSpec((DEPTH, HG_WIDTH), lambda i: (0, 0)),
            pl.BlockSpec((1, HG_WIDTH), lambda i: (0, 0)),
            pl.BlockSpec((T, 4 * HG_WIDTH), lambda i: (i, 0)),
        ],
        out_specs=pl.BlockSpec((T, HG_WIDTH), lambda i: (i, 0)),
        out_shape=jax.ShapeDtypeStruct((S, HG_WIDTH), BF16),
        scratch_shapes=[
            pltpu.VMEM((HG_WIDTH, HG_WIDTH), F32),
            pltpu.VMEM((T, HG_WIDTH), F32),
        ],
        compiler_params=_params(("arbitrary",)),
        name="hgrn2",
    )(lb_logits, g_t, hg)


def _pool_kernel(u_ref, w_ref, sc_ref, o_ref, halo_scr):
    T = u_ref.shape[0]
    H = POOL_HALO
    blk = pl.program_id(0)

    @pl.when(blk == 0)
    def _():
        halo_scr[...] = jnp.zeros(halo_scr.shape, F32)

    u = u_ref[...]
    ext = jnp.concatenate([halo_scr[...], u], axis=0)
    halo_scr[...] = u[T - H:, :]
    sums = []
    s = ext
    w = 1
    while w < POOL_WINDOWS[-1]:
        s = s + pltpu.roll(s, w, 0)
        w *= 2
        sums.append(s[H:, :])
    t1 = (blk * T + 1 + lax.broadcasted_iota(jnp.int32, (T, POOL_WIDTH), 0)).astype(F32)
    lane = lax.broadcasted_iota(jnp.int32, (T, POOL_WIDTH), 1)
    pooled = None
    for gi, win in enumerate(POOL_WINDOWS):
        mean = sums[gi] / jnp.minimum(t1, float(win))
        pooled = mean if pooled is None else jnp.where(lane >= gi * POOL_GROUP, mean, pooled)
    pooled = pooled - u
    y = jnp.dot(pooled.astype(BF16), w_ref[...], preferred_element_type=F32)
    o_ref[...] = (y * sc_ref[...]).astype(BF16)


def _multiscale_pool(pu, pool_w, pool_scale):
    S = pu.shape[0]
    T = ROW_TILE
    w_bd = jax.scipy.linalg.block_diag(*[pool_w[g] for g in range(len(POOL_WINDOWS))]).astype(BF16)
    return pl.pallas_call(
        _pool_kernel,
        grid=(S // T,),
        in_specs=[
            pl.BlockSpec((T, POOL_WIDTH), lambda i: (i, 0)),
            pl.BlockSpec((POOL_WIDTH, POOL_WIDTH), lambda i: (0, 0)),
            pl.BlockSpec((1, POOL_WIDTH), lambda i: (0, 0)),
        ],
        out_specs=pl.BlockSpec((T, POOL_WIDTH), lambda i: (i, 0)),
        out_shape=jax.ShapeDtypeStruct((S, POOL_WIDTH), BF16),
        scratch_shapes=[pltpu.VMEM((POOL_HALO, POOL_WIDTH), F32)],
        compiler_params=_params(("arbitrary",)),
        name="ms_pool",
    )(pu, w_bd, pool_scale.reshape(1, POOL_WIDTH))


def _outproj_kernel(a_ref, r_ref, p_ref, x_ref, w_ref, g1_ref, lg_ref, lb_ref, o_ref):
    m = jnp.dot(a_ref[...], w_ref[0:DA_WIDTH, :], preferred_element_type=F32)
    m += jnp.dot(r_ref[...], w_ref[DA_WIDTH:DA_WIDTH + HG_WIDTH, :], preferred_element_type=F32)
    m += jnp.dot(p_ref[...], w_ref[DA_WIDTH + HG_WIDTH:, :], preferred_element_type=F32)
    y = ALPHA * x_ref[...] + (1.0 + g1_ref[...]) * m
    o_ref[...] = _layer_norm(y, lg_ref[...], lb_ref[...])


def _out_projection(a, r, p, x, w_out_bf, g1, ln_g, ln_b):
    S = x.shape[0]
    tm = ROW_TILE
    row = lambda i: (i, 0)
    fixed = lambda i: (0, 0)
    vec = pl.BlockSpec((1, D_MODEL), fixed)
    return pl.pallas_call(
        _outproj_kernel,
        grid=(S // tm,),
        in_specs=[
            pl.BlockSpec((tm, DA_WIDTH), row),
            pl.BlockSpec((tm, HG_WIDTH), row),
            pl.BlockSpec((tm, POOL_WIDTH), row),
            pl.BlockSpec((tm, D_MODEL), row),
            pl.BlockSpec((D_MIX, D_MODEL), fixed),
            vec, vec, vec,
        ],
        out_specs=pl.BlockSpec((tm, D_MODEL), row),
        out_shape=jax.ShapeDtypeStruct((S, D_MODEL), F32),
        compiler_params=_params(("parallel",)),
        name="out_proj_ln",
    )(a, r, p, x, w_out_bf, g1, ln_g.reshape(1, D_MODEL), ln_b.reshape(1, D_MODEL))


def _ffn_kernel(x_ref, sc_ref, sh_ref, g2_ref, w1_ref, w3_ref, w2_ref, lg_ref, lb_ref, o_ref):
    x = x_ref[...]
    h = (x * (1.0 + sc_ref[...]) + sh_ref[...]).astype(BF16)
    acc = jnp.zeros(x.shape, F32)
    step = 4 * MXU_DIM
    for lo in range(0, FFN_DIM, step):
        hi = min(lo + step, FFN_DIM)
        a = jnp.dot(h, w1_ref[:, lo:hi], preferred_element_type=F32)
        b = jnp.dot(h, w3_ref[:, lo:hi], preferred_element_type=F32)
        acc += jnp.dot((_silu(a) * b).astype(BF16), w2_ref[lo:hi, :], preferred_element_type=F32)
    y = ALPHA * x + (1.0 + g2_ref[...]) * acc
    o_ref[...] = _layer_norm(y, lg_ref[...], lb_ref[...])


def _dense_ffn(x, sc, sh, g2, w1, w3, w2, ln_g, ln_b):
    S = x.shape[0]
    tm = ROW_TILE
    row = lambda i: (i, 0)
    fixed = lambda i: (0, 0)
    vec = pl.BlockSpec((1, D_MODEL), fixed)
    once = pl.Buffered(1)
    return pl.pallas_call(
        _ffn_kernel,
        grid=(S // tm,),
        in_specs=[
            pl.BlockSpec((tm, D_MODEL), row),
            vec, vec, vec,
            pl.BlockSpec((D_MODEL, FFN_DIM), fixed, pipeline_mode=once),
            pl.BlockSpec((D_MODEL, FFN_DIM), fixed, pipeline_mode=once),
            pl.BlockSpec((FFN_DIM, D_MODEL), fixed, pipeline_mode=once),
            vec, vec,
        ],
        out_specs=pl.BlockSpec((tm, D_MODEL), row),
        out_shape=jax.ShapeDtypeStruct((S, D_MODEL), F32),
        compiler_params=_params(("parallel",)),
        name="dense_ffn_ln",
    )(x, sc, sh, g2, w1, w3, w2, ln_g.reshape(1, D_MODEL), ln_b.reshape(1, D_MODEL))


def _router_kernel(x_ref, sc_ref, sh_ref, rw_ref, h_ref, rank_ref, comb_ref, cum_ref):
    T = x_ref.shape[0]
    tc = MOE_CHUNK
    E = N_EXPERTS
    h = x_ref[...] * (1.0 + sc_ref[...]) + sh_ref[...]
    h_ref[...] = h.astype(BF16)
    logits = lax.dot_general(rw_ref[...], h, (((1,), (1,)), ((), ())),
                             preferred_element_type=F32, precision=HIGHEST)
    eidx = lax.broadcasted_iota(jnp.int32, (E, T), 0)
    v1 = jnp.max(logits, axis=0, keepdims=True)
    i1 = jnp.min(jnp.where(logits == v1, eidx, E), axis=0, keepdims=True)
    m1 = eidx == i1
    rest = jnp.where(m1, -jnp.inf, logits)
    v2 = jnp.max(rest, axis=0, keepdims=True)
    i2 = jnp.min(jnp.where(rest == v2, eidx, E), axis=0, keepdims=True)
    m2 = eidx == i2
    e2 = jnp.exp(v2 - v1)
    g1 = 1.0 / (1.0 + e2)
    g2 = e2 / (1.0 + e2)
    comb = jnp.where(m1, g1, 0.0) + jnp.where(m2, g2, 0.0)
    routed = jnp.logical_or(m1, m2)
    ind = routed.astype(BF16)
    r_i = lax.broadcasted_iota(jnp.int32, (tc, tc), 0)
    c_i = lax.broadcasted_iota(jnp.int32, (tc, tc), 1)
    strict = (r_i < c_i).astype(BF16)
    lane = lax.broadcasted_iota(jnp.int32, (E, LANES), 1)
    running = jnp.zeros((E, 1), F32)
    cum = jnp.zeros((E, LANES), F32)
    for ci in range(T // tc):
        sl = slice(ci * tc, (ci + 1) * tc)
        ind_c = ind[:, sl]
        rank_c = jnp.dot(ind_c, strict, preferred_element_type=F32) + running
        rank_ref[:, ci] = jnp.where(routed[:, sl], rank_c, -1.0).reshape(E, 1, tc)
        comb_ref[:, ci] = comb[:, sl].reshape(E, 1, tc)
        cum = jnp.where(lane == ci, running, cum)
        running = running + jnp.sum(ind_c.astype(F32), axis=1, keepdims=True)
    cum = jnp.where(lane == T // tc, running, cum)
    cum_ref[0] = cum.astype(jnp.int32)


def _router(x, sc, sh, router_w):
    S = x.shape[0]
    T = MOE_BLOCK
    E = N_EXPERTS
    fixed = lambda i: (0, 0)
    vec = pl.BlockSpec((1, D_MODEL), fixed)
    return pl.pallas_call(
        _router_kernel,
        grid=(S // T,),
        in_specs=[
            pl.BlockSpec((T, D_MODEL), lambda i: (i, 0)),
            vec, vec,
            pl.BlockSpec((E, D_MODEL), fixed),
        ],
        out_specs=[
            pl.BlockSpec((T, D_MODEL), lambda i: (i, 0)),
            pl.BlockSpec((E, T // MOE_CHUNK, 1, MOE_CHUNK), lambda i: (0, i, 0, 0)),
            pl.BlockSpec((E, T // MOE_CHUNK, 1, MOE_CHUNK), lambda i: (0, i, 0, 0)),
            pl.BlockSpec((1, E, LANES), lambda i: (i, 0, 0)),
        ],
        out_shape=[
            jax.ShapeDtypeStruct((S, D_MODEL), BF16),
            jax.ShapeDtypeStruct((E, S // MOE_CHUNK, 1, MOE_CHUNK), F32),
            jax.ShapeDtypeStruct((E, S // MOE_CHUNK, 1, MOE_CHUNK), F32),
            jax.ShapeDtypeStruct((S // T, E, LANES), jnp.int32),
        ],
        compiler_params=_params(("parallel",)),
        name="moe_router",
    )(x, sc, sh, router_w.T)


def _moe_kernel(cum_ref, h_ref, rank_ref, comb_ref, w1_ref, w3_ref, w2_ref, y_ref, xe_scr, gate_scr):
    T = h_ref.shape[0]
    tr, tc, win = MOE_TILE, MOE_CHUNK, MOE_WINDOW
    nchunk = T // tc
    b = pl.program_id(0)
    e = pl.program_id(1)
    f = pl.program_id(2)

    @pl.when(jnp.logical_and(e == 0, f == 0))
    def _():
        y_ref[...] = jnp.zeros(y_ref.shape, F32)

    base = (b * N_EXPERTS + e) * LANES
    count = cum_ref[base + nchunk]
    ntiles = (count + (tr - 1)) // tr

    def token_row(ref, first, n):
        blk = ref[0, pl.ds(first, n)]
        return jnp.concatenate([blk[c] for c in range(n)], axis=1)

    def tile(r, carry):
        lo = r * tr
        rows = pl.ds(pl.multiple_of(lo, 16), tr)

        def hits(first, n):
            want = (lo + lax.broadcasted_iota(jnp.int32, (tr, n * tc), 0)).astype(F32)
            return token_row(rank_ref, first, n) == want

        @pl.when(f == 0)
        def _():
            hit = hits(0, nchunk)
            xe_scr[rows, :] = jnp.dot(hit.astype(BF16), h_ref[...],
                                      preferred_element_type=F32).astype(BF16)
            gate_scr[rows, :] = jnp.sum(jnp.where(hit, token_row(comb_ref, 0, nchunk), 0.0),
                                        axis=1, keepdims=True)

        xe = xe_scr[rows, :]
        a = jnp.dot(xe, w1_ref[0], preferred_element_type=F32)
        g = jnp.dot(xe, w3_ref[0], preferred_element_type=F32)
        mid = (_silu(a) * g).astype(BF16)
        out = jnp.dot(mid, w2_ref[0], preferred_element_type=F32) * gate_scr[rows, :]
        out_bf = out.astype(BF16)

        def scatter(first, n):
            start = first * tc
            if not isinstance(start, int):
                start = pl.multiple_of(start, tc)
            y_ref[pl.ds(start, n * tc), :] += lax.dot_general(
                hits(first, n).astype(BF16), out_bf, (((0,), (0,)), ((), ())),
                preferred_element_type=F32)

        c_lo = jnp.int32(0)
        c_hi = jnp.int32(-1)
        for c in range(nchunk):
            c_lo += (cum_ref[base + c + 1] <= lo).astype(jnp.int32)
            c_hi += (cum_ref[base + c] < lo + tr).astype(jnp.int32)
        fits = c_hi - c_lo < win

        @pl.when(fits)
        def _():
            scatter(jnp.minimum(c_lo, nchunk - win), win)

        @pl.when(jnp.logical_not(fits))
        def _():
            scatter(0, nchunk)

        return carry

    lax.fori_loop(0, ntiles, tile, 0)


def _moe_ffn(h_bf, rank, comb, cum, w1, w3, w2):
    S = h_bf.shape[0]
    T = MOE_BLOCK
    fw = EXPERT_DIM // MOE_FSPLIT
    grid_spec = pltpu.PrefetchScalarGridSpec(
        num_scalar_prefetch=1,
        grid=(S // T, N_EXPERTS, MOE_FSPLIT),
        in_specs=[
            pl.BlockSpec((T, D_MODEL), lambda b, e, f, c: (b, 0)),
            pl.BlockSpec((1, T // MOE_CHUNK, 1, MOE_CHUNK), lambda b, e, f, c: (e, b, 0, 0)),
            pl.BlockSpec((1, T // MOE_CHUNK, 1, MOE_CHUNK), lambda b, e, f, c: (e, b, 0, 0)),
            pl.BlockSpec((1, D_MODEL, fw), lambda b, e, f, c: (e, 0, f)),
            pl.BlockSpec((1, D_MODEL, fw), lambda b, e, f, c: (e, 0, f)),
            pl.BlockSpec((1, fw, D_MODEL), lambda b, e, f, c: (e, f, 0)),
        ],
        out_specs=pl.BlockSpec((T, D_MODEL), lambda b, e, f, c: (b, 0)),
        scratch_shapes=[
            pltpu.VMEM((T + MOE_TILE, D_MODEL), BF16),
            pltpu.VMEM((T + MOE_TILE, 1), F32),
        ],
    )
    return pl.pallas_call(
        _moe_kernel,
        grid_spec=grid_spec,
        out_shape=jax.ShapeDtypeStruct((S, D_MODEL), F32),
        compiler_params=_params(("arbitrary", "arbitrary", "arbitrary")),
        name="moe_experts",
    )(cum.reshape(-1), h_bf, rank, comb, w1, w3, w2)


def _ln_kernel(x_ref, y_ref, g2_ref, lg_ref, lb_ref, o_ref):
    y = ALPHA * x_ref[...] + (1.0 + g2_ref[...]) * y_ref[...]
    o_ref[...] = _layer_norm(y, lg_ref[...], lb_ref[...])


def _residual_ln(x, y, g2, ln_g, ln_b):
    S = x.shape[0]
    tm = ROW_TILE
    row = lambda i: (i, 0)
    vec = pl.BlockSpec((1, D_MODEL), lambda i: (0, 0))
    return pl.pallas_call(
        _ln_kernel,
        grid=(S // tm,),
        in_specs=[pl.BlockSpec((tm, D_MODEL), row), pl.BlockSpec((tm, D_MODEL), row), vec, vec, vec],
        out_specs=pl.BlockSpec((tm, D_MODEL), row),
        out_shape=jax.ShapeDtypeStruct((S, D_MODEL), F32),
        compiler_params=_params(("parallel",)),
        name="residual_ln",
    )(x, y, g2, ln_g.reshape(1, D_MODEL), ln_b.reshape(1, D_MODEL))


def _rope_tables(S):
    d = DA_HEAD_DIM
    inv_freq = 1.0 / (ROPE_THETA ** (jnp.arange(0, d, 2, dtype=F32) / d))
    ang = jnp.arange(S, dtype=F32)[:, None] * inv_freq[None, :]
    cos = jnp.cos(ang)
    sin = jnp.sin(ang)
    reps = LANES // d
    cos_t = jnp.tile(jnp.concatenate([cos, cos], axis=1), (1, reps))
    sin_t = jnp.tile(jnp.concatenate([-sin, sin], axis=1), (1, reps))
    return cos_t, sin_t


def kernel(x, c, w_ada, b_ada, w_in, lam_qk, attn_norm_g, hg_lb_logits, hg_norm_g, pool_w, pool_scale, w_out, ln1_g, ln1_b, ln2_g, ln2_b, ffn_w1, ffn_w3, ffn_w2, router_w, exp_w1, exp_w3, exp_w2):
    B, S, D = x.shape
    assert B == 1 and D == D_MODEL and S % MOE_BLOCK == 0
    xs = x.reshape(S, D)
    mod = _modulation(c, w_ada, b_ada)
    cos_t, sin_t = _rope_tables(S)
    for l in range(DEPTH):
        sh1, sc1, g1, sh2, sc2, g2 = (mod[l, j] for j in range(6))
        qt, k, vt, hg, pu = _in_projection(xs, sc1, sh1, w_in[l].astype(BF16), cos_t, sin_t)
        a = _diff_attention(qt, k, vt, lam_qk[l], attn_norm_g[l], l)
        r = _hgrn2(hg, hg_lb_logits, hg_norm_g[l], l)
        p = _multiscale_pool(pu, pool_w[l], pool_scale[l])
        xs = _out_projection(a, r, p, xs, w_out[l].astype(BF16), g1, ln1_g[l], ln1_b[l])
        if l % 2 == 0:
            i = l // 2
            xs = _dense_ffn(xs, sc2, sh2, g2, ffn_w1[i].astype(BF16), ffn_w3[i].astype(BF16),
                            ffn_w2[i].astype(BF16), ln2_g[l], ln2_b[l])
        else:
            i = l // 2
            h_bf, rank, comb, cum = _router(xs, sc2, sh2, router_w[i])
            y = _moe_ffn(h_bf, rank, comb, cum, exp_w1[i].astype(BF16), exp_w3[i].astype(BF16),
                         exp_w2[i].astype(BF16))
            xs = _residual_ln(xs, y, g2, ln2_g[l], ln2_b[l])
    return xs.reshape(B, S, D)
```

```python
import functools
import math

import jax
import jax.numpy as jnp
from jax import lax
from jax.experimental import pallas as pl
from jax.experimental.pallas import tpu as pltpu

F32 = jnp.float32
BF16 = jnp.bfloat16
HIGHEST = lax.Precision.HIGHEST

D_MODEL = 1024
DEPTH = 2
DA_HEADS = 4
DA_WIDTH = D_MODEL // 2
DA_HEAD_DIM = DA_WIDTH // (2 * DA_HEADS)
DA_VAL_DIM = 2 * DA_HEAD_DIM
HG_HEADS = 4
HG_WIDTH = D_MODEL // 4
HG_DIM = HG_WIDTH // HG_HEADS
POOL_WINDOWS = (2, 4, 8, 16)
POOL_WIDTH = D_MODEL // 4
POOL_GROUP = POOL_WIDTH // len(POOL_WINDOWS)
D_MIX = DA_WIDTH + HG_WIDTH + POOL_WIDTH
D_IN = 3 * DA_WIDTH + 4 * HG_WIDTH + POOL_WIDTH
ROPE_THETA = 10000.0
MASK_VALUE = -1e30
TINY = 1e-30
FFN_DIM = 11 * D_MODEL // 4
N_EXPERTS = 8
EXPERT_DIM = 7 * D_MODEL // 2
ALPHA = (2 * DEPTH) ** 0.25
EPS = 1e-5
LOG2_E = math.log2(math.e)

LANES = 128
MXU_DIM = 256
VMEM_LIMIT = 60 * 1024 * 1024

ROW_TILE = 512
ATTN_BLOCK = 512
ATTN_QUERY_GROUP = 256
ATTN_SUM_ROWS = 16
ATTN_VT_ROWS = DA_VAL_DIM + ATTN_SUM_ROWS
HG_BLOCK = 512
HG_CHUNK = 16
POOL_HALO = 16
MOE_BLOCK = 2048
MOE_TILE = 192
MOE_CHUNK = 256
MOE_WINDOW = 5
MOE_FSPLIT = 2


def _params(sem, vmem=VMEM_LIMIT):
    return pltpu.CompilerParams(dimension_semantics=sem, vmem_limit_bytes=vmem)


def _layer_norm(y, g, b):
    mu = jnp.mean(y, axis=-1, keepdims=True)
    d = y - mu
    var = jnp.mean(d * d, axis=-1, keepdims=True)
    return d * lax.rsqrt(var + EPS) * g + b


def _silu(x):
    return x * jax.nn.sigmoid(x)


def _head_block_mask(n, group):
    r = lax.broadcasted_iota(jnp.int32, (n, n), 0) // group
    c = lax.broadcasted_iota(jnp.int32, (n, n), 1) // group
    return r == c


def _mod_kernel(c_ref, w_ref, b_ref, o_ref):
    cond = _silu(c_ref[...])
    cond8 = jnp.broadcast_to(cond, (8, D_MODEL))
    r = jnp.dot(cond8, w_ref[0], preferred_element_type=F32, precision=HIGHEST)
    o_ref[0] = r[0:1] + b_ref[0]


def _modulation(c, w_ada, b_ada):
    n = 6 * DEPTH
    b3 = b_ada.reshape(n, 1, D_MODEL)
    out = pl.pallas_call(
        _mod_kernel,
        grid=(DEPTH, 6),
        in_specs=[
            pl.BlockSpec((1, D_MODEL), lambda l, j: (0, 0)),
            pl.BlockSpec((1, D_MODEL, D_MODEL), lambda l, j: (l, 0, j)),
            pl.BlockSpec((1, 1, D_MODEL), lambda l, j: (l * 6 + j, 0, 0)),
        ],
        out_specs=pl.BlockSpec((1, 1, D_MODEL), lambda l, j: (l * 6 + j, 0, 0)),
        out_shape=jax.ShapeDtypeStruct((n, 1, D_MODEL), F32),
        compiler_params=_params(("parallel", "parallel")),
        name="adaln_mod",
    )(c, w_ada, b3)
    return out.reshape(DEPTH, 6, 1, D_MODEL)


def _inproj_kernel(x_ref, sc_ref, sh_ref, w_ref, wvt_ref, cos_ref, sin_ref,
                   qt_ref, k_ref, vt_ref, hg_ref, pu_ref):
    h = (x_ref[...] * (1.0 + sc_ref[...]) + sh_ref[...]).astype(BF16)
    qk = jnp.dot(h, w_ref[:, : 2 * DA_WIDTH], preferred_element_type=F32)
    reps = 2 * DA_WIDTH // LANES
    cos = jnp.tile(cos_ref[...], (1, reps))
    sin = jnp.tile(sin_ref[...], (1, reps))
    lane = lax.broadcasted_iota(jnp.int32, qk.shape, 1)
    half = DA_HEAD_DIM // 2
    first_half = (lane & half) == 0
    width = 2 * DA_WIDTH
    partner = jnp.where(first_half, pltpu.roll(qk, width - half, 1), pltpu.roll(qk, half, 1))
    rot = qk * cos + partner * sin
    qt_ref[0] = (rot[:, :DA_WIDTH] * (DA_HEAD_DIM ** -0.5 * LOG2_E)).T.astype(BF16)
    k_ref[...] = rot[:, DA_WIDTH:].astype(BF16)
    o = 2 * DA_WIDTH
    vt = lax.dot_general(wvt_ref[...], h, (((1,), (1,)), ((), ())),
                         preferred_element_type=F32).astype(BF16)
    ones = jnp.ones((ATTN_SUM_ROWS, vt.shape[1]), BF16)
    pieces = []
    for hd in range(DA_HEADS):
        pieces += [vt[hd * DA_VAL_DIM:(hd + 1) * DA_VAL_DIM], ones]
    vt_ref[0] = jnp.concatenate(pieces, axis=0)
    o += DA_WIDTH
    hg_ref[...] = jnp.dot(h, w_ref[:, o:o + 4 * HG_WIDTH], preferred_element_type=F32)
    o += 4 * HG_WIDTH
    pu_ref[...] = jnp.dot(h, w_ref[:, o:o + POOL_WIDTH], preferred_element_type=F32)


def _in_projection(x, sc, sh, w_in_bf, cos_t, sin_t):
    S = x.shape[0]
    tm = ATTN_BLOCK
    row = lambda i: (i, 0)
    fixed = lambda i: (0, 0)
    return pl.pallas_call(
        _inproj_kernel,
        grid=(S // tm,),
        in_specs=[
            pl.BlockSpec((tm, D_MODEL), row),
            pl.BlockSpec((1, D_MODEL), fixed),
            pl.BlockSpec((1, D_MODEL), fixed),
            pl.BlockSpec((D_MODEL, D_IN), fixed),
            pl.BlockSpec((DA_WIDTH, D_MODEL), fixed),
            pl.BlockSpec((tm, LANES), row),
            pl.BlockSpec((tm, LANES), row),
        ],
        out_specs=[
            pl.BlockSpec((1, DA_WIDTH, tm), lambda i: (i, 0, 0)),
            pl.BlockSpec((tm, DA_WIDTH), row),
            pl.BlockSpec((1, DA_HEADS * ATTN_VT_ROWS, tm), lambda i: (i, 0, 0)),
            pl.BlockSpec((tm, 4 * HG_WIDTH), row),
            pl.BlockSpec((tm, POOL_WIDTH), row),
        ],
        out_shape=[
            jax.ShapeDtypeStruct((S // tm, DA_WIDTH, tm), BF16),
            jax.ShapeDtypeStruct((S, DA_WIDTH), BF16),
            jax.ShapeDtypeStruct((S // tm, DA_HEADS * ATTN_VT_ROWS, tm), BF16),
            jax.ShapeDtypeStruct((S, 4 * HG_WIDTH), F32),
            jax.ShapeDtypeStruct((S, POOL_WIDTH), F32),
        ],
        compiler_params=_params(("parallel",)),
        name="in_proj",
    )(x, sc, sh, w_in_bf, w_in_bf[:, 2 * DA_WIDTH:3 * DA_WIDTH].T, cos_t, sin_t)


def _attn_kernel(lam_ref, g_ref, qt_ref, k_ref, vt_ref, o_ref, m_scr, a_scr, c_scr, acc_scr, s_scr,
                 p_scr, *, lam_init):
    bq = qt_ref.shape[2]
    bk = vt_ref.shape[2]
    cg = ATTN_QUERY_GROUP
    i = pl.program_id(1)
    qt = qt_ref[0]
    rowq = lax.broadcasted_iota(jnp.int32, qt.shape, 0)
    zero = jnp.zeros_like(qt)
    maps = (jnp.where(rowq < DA_HEAD_DIM, qt, zero), jnp.where(rowq >= DA_HEAD_DIM, qt, zero))
    groups = [(a * bq + c0, c0, maps[a][:, c0:c0 + cg]) for a in range(2) for c0 in range(0, bq, cg)]
    m_scr[...] = jnp.full(m_scr.shape, -jnp.inf, F32)
    acc_scr[...] = jnp.zeros(acc_scr.shape, F32)
    hk = bk // 2
    key_i = lax.broadcasted_iota(jnp.int32, (hk, cg), 0)
    qry_i = lax.broadcasted_iota(jnp.int32, (hk, cg), 1)
    ng = len(groups)
    last = ng - 1

    def cols_of(g):
        return slice(groups[g][0], groups[g][0] + cg)

    def ccols_of(sset, g):
        return slice(sset * 2 * bq + groups[g][0], sset * 2 * bq + groups[g][0] + cg)

    def mask_mode(key_offset, g):
        c0 = groups[g][1]
        if key_offset + hk - 1 <= c0:
            return None
        if key_offset > c0 + cg - 1:
            return "skip"
        return key_offset

    def score_stage(hb, sset, g, key_offset):
        kb = k_ref[pl.ds(pl.multiple_of(hb * hk, hk), hk), :]
        s = jnp.dot(kb, groups[g][2], preferred_element_type=F32)
        if key_offset is not None:
            s = jnp.where(key_i + key_offset <= qry_i + groups[g][1], s, MASK_VALUE)
        s_scr[sset * ng + g] = s
        c_scr[:, ccols_of(sset, g)] = jnp.max(s, axis=0, keepdims=True)

    def max_stage(sset, g):
        cols = cols_of(g)
        m_prev = m_scr[:, cols]
        m_new = jnp.maximum(m_prev, c_scr[:, ccols_of(sset, g)])
        a_scr[:, cols] = jnp.exp2(m_prev - m_new)
        m_scr[:, cols] = m_new

    def exp_stage(sset, g, vt_half, carried, defer):
        cols = cols_of(g)
        p = jnp.exp2(s_scr[sset * ng + g] - m_scr[:, cols])
        alpha = a_scr[:, cols]
        pb = p.astype(BF16)
        acc = acc_scr[:, cols]
        if carried is not None:
            acc = acc + carried
        if defer:
            acc_scr[:, cols] = alpha * acc
            p_scr[...] = pb
        else:
            acc_scr[:, cols] = alpha * acc + jnp.dot(vt_half, pb, preferred_element_type=F32)

    def iteration(t, look, diagonal):
        vt_a = vt_ref[t, :, 0:hk]
        vt_b = vt_ref[t, :, hk:bk]
        carried = jnp.dot(vt_ref[jnp.maximum(t - 1, 0), :, hk:bk], p_scr[...],
                          preferred_element_type=F32)
        if diagonal:
            b_items = [(g, mask_mode(hk, g)) for g in range(ng) if mask_mode(hk, g) != "skip"]
        else:
            b_items = [(g, None) for g in range(ng)]
        score_stage(2 * t + 1, 1, *b_items[0])
        for g in range(ng):
            if g + 1 < len(b_items):
                score_stage(2 * t + 1, 1, *b_items[g + 1])
            if g + 1 < ng:
                max_stage(0, g + 1)
            else:
                max_stage(1, b_items[0][0])
            exp_stage(0, g, vt_a, carried if g == last else None, False)
        if not diagonal:
            score_stage(2 * t + 2, 0, 0, mask_mode(0, 0) if look == "diag" else None)
        for n, (g, _) in enumerate(b_items):
            if not diagonal and g + 1 < ng:
                score_stage(2 * t + 2, 0, g + 1, mask_mode(0, g + 1) if look == "diag" else None)
            if n + 1 < len(b_items):
                max_stage(1, b_items[n + 1][0])
            elif not diagonal:
                max_stage(0, 0)
            exp_stage(1, g, vt_b, None, not diagonal and g == last)

    p_scr[...] = jnp.zeros(p_scr.shape, BF16)
    for g in range(ng):
        score_stage(0, 0, g, (0 - i) * bq)
    max_stage(0, 0)

    def body(t, carry):
        iteration(t, None, False)
        return carry

    lax.fori_loop(0, i - 1, body, 0)

    @pl.when(i > 0)
    def _():
        iteration(i - 1, "diag", False)

    iteration(i, None, True)

    lq = lam_ref[...]
    lam = (jnp.exp(jnp.sum(lq[0:1] * lq[1:2], axis=1, keepdims=True))
           - jnp.exp(jnp.sum(lq[2:3] * lq[3:4], axis=1, keepdims=True)) + lam_init)
    o = acc_scr[0:DA_VAL_DIM, :] / acc_scr[DA_VAL_DIM:DA_VAL_DIM + 1, :]
    o = o[:, :bq] - lam * o[:, bq:]
    ms = jnp.mean(o * o, axis=0, keepdims=True)
    o = o * lax.rsqrt(ms + EPS) * g_ref[...] * (1.0 - lam_init)
    o_ref[...] = o.T.astype(BF16)


def _diff_attention(qt, k, vt, lam_qk, attn_g, layer):
    S = k.shape[0]
    bq = bk = ATTN_BLOCK
    nk = S // bk
    lam_init = 0.8 - 0.6 * math.exp(-0.3 * layer)
    return pl.pallas_call(
        functools.partial(_attn_kernel, lam_init=lam_init),
        grid=(DA_HEADS, S // bq),
        in_specs=[
            pl.BlockSpec((4, DA_HEAD_DIM), lambda h, i: (0, 0)),
            pl.BlockSpec((DA_VAL_DIM, 1), lambda h, i: (0, 0)),
            pl.BlockSpec((1, LANES, bq), lambda h, i: (i, h, 0)),
            pl.BlockSpec((S, LANES), lambda h, i: (0, h)),
            pl.BlockSpec((nk, ATTN_VT_ROWS, bk), lambda h, i: (0, h, 0)),
        ],
        out_specs=pl.BlockSpec((bq, DA_VAL_DIM), lambda h, i: (i, h)),
        out_shape=jax.ShapeDtypeStruct((S, DA_WIDTH), BF16),
        scratch_shapes=[
            pltpu.VMEM((1, 2 * bq), F32),
            pltpu.VMEM((1, 2 * bq), F32),
            pltpu.VMEM((1, 2 * 2 * bq), F32),
            pltpu.VMEM((ATTN_VT_ROWS, 2 * bq), F32),
            pltpu.VMEM((2 * (2 * bq // ATTN_QUERY_GROUP), bk // 2, ATTN_QUERY_GROUP), F32),
            pltpu.VMEM((bk // 2, ATTN_QUERY_GROUP), BF16),
        ],
        compiler_params=_params(("parallel", "parallel")),
        name="diff_attn",
    )(lam_qk, attn_g.reshape(DA_VAL_DIM, 1), qt, k, vt)


def _hgrn_kernel(lbl_ref, g_ref, hg_ref, o_ref, st_scr, o_scr, *, layer):
    T = hg_ref.shape[0]
    c = HG_CHUNK
    W = HG_WIDTH

    @pl.when(pl.program_id(0) == 0)
    def _():
        st_scr[...] = jnp.zeros(st_scr.shape, F32)

    lg = lbl_ref[...]
    ex = jnp.exp(lg - jnp.max(lg, axis=0, keepdims=True))
    prob = ex / jnp.sum(ex, axis=0, keepdims=True)
    lb = jnp.zeros((1, W), F32)
    for li in range(1, layer + 1):
        lb = lb + prob[li:li + 1]

    same_head = _head_block_mask(W, HG_DIM)
    ones_bd = same_head.astype(BF16)
    rowi = lax.broadcasted_iota(jnp.int32, (c, W), 0)

    def body(s, carry):
        sl = pl.ds(pl.multiple_of(s * c, c), c)
        qf = hg_ref[sl, 0:W]
        z = hg_ref[sl, W:2 * W]
        iv = hg_ref[sl, 2 * W:3 * W]
        f = lb + (1.0 - lb) * jax.nn.sigmoid(z)
        logf = jnp.log(jnp.maximum(f, TINY))
        key = (1.0 - lb) * jax.nn.sigmoid(-z)
        val = _silu(iv)
        b = logf
        sft = 1
        while sft < c:
            b = b + jnp.where(rowi >= sft, pltpu.roll(b, sft, 0), 0.0)
            sft *= 2
        es = [(qf * key).astype(BF16)]
        vrs = [val]
        for d in range(1, c):
            kr = pltpu.roll(key, d, 0)
            br = pltpu.roll(b, d, 0)
            e = jnp.where(rowi >= d, qf * kr * jnp.exp(b - br), 0.0)
            es.append(e.astype(BF16))
            vrs.append(pltpu.roll(val, d, 0))
        estack = jnp.concatenate(es, axis=0)
        r = jnp.dot(estack, ones_bd, preferred_element_type=F32)
        o = r[0:c] * vrs[0]
        for d in range(1, c):
            o = o + r[d * c:(d + 1) * c] * vrs[d]
        st = st_scr[...]
        qd = (qf * jnp.exp(b)).astype(BF16)
        o = o + lax.dot_general(qd, st.astype(BF16), (((1,), (1,)), ((), ())),
                                preferred_element_type=F32)
        o_scr[sl, :] = o
        b_last = b[c - 1:c]
        kd = (key * jnp.exp(b_last - b)).astype(BF16)
        upd = lax.dot_general(val.astype(BF16), kd, (((0,), (0,)), ((), ())),
                              preferred_element_type=F32)
        st_scr[...] = jnp.exp(b_last) * st + jnp.where(same_head, upd, 0.0)
        return carry

    lax.fori_loop(0, T // c, body, 0)

    o = o_scr[...]
    ms = jnp.dot(o * o, same_head.astype(F32), preferred_element_type=F32,
                 precision=HIGHEST) * (1.0 / HG_DIM)
    gate = hg_ref[:, 3 * W:4 * W]
    o_ref[...] = (o * lax.rsqrt(ms + EPS) * g_ref[...] * _silu(gate)).astype(BF16)


def _hgrn2(hg, lb_logits, norm_g, layer):
    S = hg.shape[0]
    T = HG_BLOCK
    g_t = jnp.tile(norm_g.reshape(1, HG_DIM), (1, HG_HEADS))
    return pl.pallas_call(
        functools.partial(_hgrn_kernel, layer=layer),
        grid=(S // T,),
        in_specs=[
            pl.BlockSpec((DEPTH, HG_WIDTH), lambda i: (0, 0)),
            pl.BlockSpec((1, HG_WIDTH), lambda i: (0, 0)),
            pl.BlockSpec((T, 4 * HG_WIDTH), lambda i: (i, 0)),
        ],
        out_specs=pl.BlockSpec((T, HG_WIDTH), lambda i: (i, 0)),
        out_shape=jax.ShapeDtypeStruct((S, HG_WIDTH), BF16),
        scratch_shapes=[
            pltpu.VMEM((HG_WIDTH, HG_WIDTH), F32),
            pltpu.VMEM((T, HG_WIDTH), F32),
        ],
        compiler_params=_params(("arbitrary",)),
        name="hgrn2",
    )(lb_logits, g_t, hg)


def _pool_kernel(u_ref, w_ref, sc_ref, o_ref, halo_scr):
    T = u_ref.shape[0]
    H = POOL_HALO
    blk = pl.program_id(0)

    @pl.when(blk == 0)
    def _():
        halo_scr[...] = jnp.zeros(halo_scr.shape, F32)

    u = u_ref[...]
    ext = jnp.concatenate([halo_scr[...], u], axis=0)
    halo_scr[...] = u[T - H:, :]
    sums = []
    s = ext
    w = 1
    while w < POOL_WINDOWS[-1]:
        s = s + pltpu.roll(s, w, 0)
        w *= 2
        sums.append(s[H:, :])
    t1 = (blk * T + 1 + lax.broadcasted_iota(jnp.int32, (T, POOL_WIDTH), 0)).astype(F32)
    lane = lax.broadcasted_iota(jnp.int32, (T, POOL_WIDTH), 1)
    pooled = None
    for gi, win in enumerate(POOL_WINDOWS):
        mean = sums[gi] / jnp.minimum(t1, float(win))
        pooled = mean if pooled is None else jnp.where(lane >= gi * POOL_GROUP, mean, pooled)
    pooled = pooled - u
    y = jnp.dot(pooled.astype(BF16), w_ref[...], preferred_element_type=F32)
    o_ref[...] = (y * sc_ref[...]).astype(BF16)


def _multiscale_pool(pu, pool_w, pool_scale):
    S = pu.shape[0]
    T = ROW_TILE
    w_bd = jax.scipy.linalg.block_diag(*[pool_w[g] for g in range(len(POOL_WINDOWS))]).astype(BF16)
    return pl.pallas_call(
        _pool_kernel,
        grid=(S // T,),
        in_specs=[
            pl.BlockSpec((T, POOL_WIDTH), lambda i: (i, 0)),
            pl.BlockSpec((POOL_WIDTH, POOL_WIDTH), lambda i: (0, 0)),
            pl.BlockSpec((1, POOL_WIDTH), lambda i: (0, 0)),
        ],
        out_specs=pl.BlockSpec((T, POOL_WIDTH), lambda i: (i, 0)),
        out_shape=jax.ShapeDtypeStruct((S, POOL_WIDTH), BF16),
        scratch_shapes=[pltpu.VMEM((POOL_HALO, POOL_WIDTH), F32)],
        compiler_params=_params(("arbitrary",)),
        name="ms_pool",
    )(pu, w_bd, pool_scale.reshape(1, POOL_WIDTH))


def _outproj_kernel(a_ref, r_ref, p_ref, x_ref, w_ref, g1_ref, lg_ref, lb_ref, o_ref):
    m = jnp.dot(a_ref[...], w_ref[0:DA_WIDTH, :], preferred_element_type=F32)
    m += jnp.dot(r_ref[...], w_ref[DA_WIDTH:DA_WIDTH + HG_WIDTH, :], preferred_element_type=F32)
    m += jnp.dot(p_ref[...], w_ref[DA_WIDTH + HG_WIDTH:, :], preferred_element_type=F32)
    y = ALPHA * x_ref[...] + (1.0 + g1_ref[...]) * m
    o_ref[...] = _layer_norm(y, lg_ref[...], lb_ref[...])


def _out_projection(a, r, p, x, w_out_bf, g1, ln_g, ln_b):
    S = x.shape[0]
    tm = ROW_TILE
    row = lambda i: (i, 0)
    fixed = lambda i: (0, 0)
    vec = pl.BlockSpec((1, D_MODEL), fixed)
    return pl.pallas_call(
        _outproj_kernel,
        grid=(S // tm,),
        in_specs=[
            pl.BlockSpec((tm, DA_WIDTH), row),
            pl.BlockSpec((tm, HG_WIDTH), row),
            pl.BlockSpec((tm, POOL_WIDTH), row),
            pl.BlockSpec((tm, D_MODEL), row),
            pl.BlockSpec((D_MIX, D_MODEL), fixed),
            vec, vec, vec,
        ],
        out_specs=pl.BlockSpec((tm, D_MODEL), row),
        out_shape=jax.ShapeDtypeStruct((S, D_MODEL), F32),
        compiler_params=_params(("parallel",)),
        name="out_proj_ln",
    )(a, r, p, x, w_out_bf, g1, ln_g.reshape(1, D_MODEL), ln_b.reshape(1, D_MODEL))


def _ffn_kernel(x_ref, sc_ref, sh_ref, g2_ref, w1_ref, w3_ref, w2_ref, lg_ref, lb_ref, o_ref):
    x = x_ref[...]
    h = (x * (1.0 + sc_ref[...]) + sh_ref[...]).astype(BF16)
    acc = jnp.zeros(x.shape, F32)
    step = 4 * MXU_DIM
    for lo in range(0, FFN_DIM, step):
        hi = min(lo + step, FFN_DIM)
        a = jnp.dot(h, w1_ref[:, lo:hi], preferred_element_type=F32)
        b = jnp.dot(h, w3_ref[:, lo:hi], preferred_element_type=F32)
        acc += jnp.dot((_silu(a) * b).astype(BF16), w2_ref[lo:hi, :], preferred_element_type=F32)
    y = ALPHA * x + (1.0 + g2_ref[...]) * acc
    o_ref[...] = _layer_norm(y, lg_ref[...], lb_ref[...])


def _dense_ffn(x, sc, sh, g2, w1, w3, w2, ln_g, ln_b):
    S = x.shape[0]
    tm = ROW_TILE
    row = lambda i: (i, 0)
    fixed = lambda i: (0, 0)
    vec = pl.BlockSpec((1, D_MODEL), fixed)
    once = pl.Buffered(1)
    return pl.pallas_call(
        _ffn_kernel,
        grid=(S // tm,),
        in_specs=[
            pl.BlockSpec((tm, D_MODEL), row),
            vec, vec, vec,
            pl.BlockSpec((D_MODEL, FFN_DIM), fixed, pipeline_mode=once),
            pl.BlockSpec((D_MODEL, FFN_DIM), fixed, pipeline_mode=once),
            pl.BlockSpec((FFN_DIM, D_MODEL), fixed, pipeline_mode=once),
            vec, vec,
        ],
        out_specs=pl.BlockSpec((tm, D_MODEL), row),
        out_shape=jax.ShapeDtypeStruct((S, D_MODEL), F32),
        compiler_params=_params(("parallel",)),
        name="dense_ffn_ln",
    )(x, sc, sh, g2, w1, w3, w2, ln_g.reshape(1, D_MODEL), ln_b.reshape(1, D_MODEL))


def _router_kernel(x_ref, sc_ref, sh_ref, rw_ref, h_ref, rank_ref, comb_ref, cum_ref):
    T = x_ref.shape[0]
    tc = MOE_CHUNK
    E = N_EXPERTS
    h = x_ref[...] * (1.0 + sc_ref[...]) + sh_ref[...]
    h_ref[...] = h.astype(BF16)
    logits = lax.dot_general(rw_ref[...], h, (((1,), (1,)), ((), ())),
                             preferred_element_type=F32, precision=HIGHEST)
    eidx = lax.broadcasted_iota(jnp.int32, (E, T), 0)
    v1 = jnp.max(logits, axis=0, keepdims=True)
    i1 = jnp.min(jnp.where(logits == v1, eidx, E), axis=0, keepdims=True)
    m1 = eidx == i1
    rest = jnp.where(m1, -jnp.inf, logits)
    v2 = jnp.max(rest, axis=0, keepdims=True)
    i2 = jnp.min(jnp.where(rest == v2, eidx, E), axis=0, keepdims=True)
    m2 = eidx == i2
    e2 = jnp.exp(v2 - v1)
    g1 = 1.0 / (1.0 + e2)
    g2 = e2 / (1.0 + e2)
    comb = jnp.where(m1, g1, 0.0) + jnp.where(m2, g2, 0.0)
    routed = jnp.logical_or(m1, m2)
    ind = routed.astype(BF16)
    r_i = lax.broadcasted_iota(jnp.int32, (tc, tc), 0)
    c_i = lax.broadcasted_iota(jnp.int32, (tc, tc), 1)
    strict = (r_i < c_i).astype(BF16)
    lane = lax.broadcasted_iota(jnp.int32, (E, LANES), 1)
    running = jnp.zeros((E, 1), F32)
    cum = jnp.zeros((E, LANES), F32)
    for ci in range(T // tc):
        sl = slice(ci * tc, (ci + 1) * tc)
        ind_c = ind[:, sl]
        rank_c = jnp.dot(ind_c, strict, preferred_element_type=F32) + running
        rank_ref[:, ci] = jnp.where(routed[:, sl], rank_c, -1.0).reshape(E, 1, tc)
        comb_ref[:, ci] = comb[:, sl].reshape(E, 1, tc)
        cum = jnp.where(lane == ci, running, cum)
        running = running + jnp.sum(ind_c.astype(F32), axis=1, keepdims=True)
    cum = jnp.where(lane == T // tc, running, cum)
    cum_ref[0] = cum.astype(jnp.int32)


def _router(x, sc, sh, router_w):
    S = x.shape[0]
    T = MOE_BLOCK
    E = N_EXPERTS
    fixed = lambda i: (0, 0)
    vec = pl.BlockSpec((1, D_MODEL), fixed)
    return pl.pallas_call(
        _router_kernel,
        grid=(S // T,),
        in_specs=[
            pl.BlockSpec((T, D_MODEL), lambda i: (i, 0)),
            vec, vec,
            pl.BlockSpec((E, D_MODEL), fixed),
        ],
        out_specs=[
            pl.BlockSpec((T, D_MODEL), lambda i: (i, 0)),
            pl.BlockSpec((E, T // MOE_CHUNK, 1, MOE_CHUNK), lambda i: (0, i, 0, 0)),
            pl.BlockSpec((E, T // MOE_CHUNK, 1, MOE_CHUNK), lambda i: (0, i, 0, 0)),
            pl.BlockSpec((1, E, LANES), lambda i: (i, 0, 0)),
        ],
        out_shape=[
            jax.ShapeDtypeStruct((S, D_MODEL), BF16),
            jax.ShapeDtypeStruct((E, S // MOE_CHUNK, 1, MOE_CHUNK), F32),
            jax.ShapeDtypeStruct((E, S // MOE_CHUNK, 1, MOE_CHUNK), F32),
            jax.ShapeDtypeStruct((S // T, E, LANES), jnp.int32),
        ],
        compiler_params=_params(("parallel",)),
        name="moe_router",
    )(x, sc, sh, router_w.T)


def _moe_kernel(cum_ref, h_ref, rank_ref, comb_ref, w1_ref, w3_ref, w2_ref, y_ref, xe_scr, gate_scr,
                out_scr):
    T = h_ref.shape[0]
    tr, tc, win = MOE_TILE, MOE_CHUNK, MOE_WINDOW
    nchunk = T // tc
    b = pl.program_id(0)
    e = pl.program_id(1)
    f = pl.program_id(2)

    @pl.when(jnp.logical_and(e == 0, f == 0))
    def _():
        y_ref[...] = jnp.zeros(y_ref.shape, F32)

    base = (b * N_EXPERTS + e) * LANES
    count = cum_ref[base + nchunk]
    ntiles = (count + (tr - 1)) // tr

    def token_row(ref, first, n):
        blk = ref[0, pl.ds(first, n)]
        return jnp.concatenate([blk[c] for c in range(n)], axis=1)

    def tile(r, carry):
        lo = r * tr
        rows = pl.ds(pl.multiple_of(lo, 16), tr)

        def hits(first, n):
            want = (lo + lax.broadcasted_iota(jnp.int32, (tr, n * tc), 0)).astype(F32)
            return token_row(rank_ref, first, n) == want

        @pl.when(f == 0)
        def _():
            hit = hits(0, nchunk)
            xe_scr[rows, :] = jnp.dot(hit.astype(BF16), h_ref[...],
                                      preferred_element_type=F32).astype(BF16)
            gate_scr[rows, :] = jnp.sum(jnp.where(hit, token_row(comb_ref, 0, nchunk), 0.0),
                                        axis=1, keepdims=True)

        xe = xe_scr[rows, :]
        a = jnp.dot(xe, w1_ref[0], preferred_element_type=F32)
        g = jnp.dot(xe, w3_ref[0], preferred_element_type=F32)
        mid = (_silu(a) * g).astype(BF16)
        part = jnp.dot(mid, w2_ref[0], preferred_element_type=F32)

        @pl.when(f == 0)
        def _():
            out_scr[rows, :] = part

        if MOE_FSPLIT > 2:
            @pl.when(jnp.logical_and(f > 0, f < MOE_FSPLIT - 1))
            def _():
                out_scr[rows, :] += part

        @pl.when(f == MOE_FSPLIT - 1)
        def _():
            out_bf = ((out_scr[rows, :] + part) * gate_scr[rows, :]).astype(BF16)

            def scatter(first, n):
                start = first * tc
                if not isinstance(start, int):
                    start = pl.multiple_of(start, tc)
                y_ref[pl.ds(start, n * tc), :] += lax.dot_general(
                    hits(first, n).astype(BF16), out_bf, (((0,), (0,)), ((), ())),
                    preferred_element_type=F32)

            c_lo = jnp.int32(0)
            c_hi = jnp.int32(-1)
            for c in range(nchunk):
                c_lo += (cum_ref[base + c + 1] <= lo).astype(jnp.int32)
                c_hi += (cum_ref[base + c] < lo + tr).astype(jnp.int32)
            fits = c_hi - c_lo < win

            @pl.when(fits)
            def _():
                scatter(jnp.minimum(c_lo, nchunk - win), win)

            @pl.when(jnp.logical_not(fits))
            def _():
                scatter(0, nchunk)

        return carry

    lax.fori_loop(0, ntiles, tile, 0)


def _moe_ffn(h_bf, rank, comb, cum, w1, w3, w2):
    S = h_bf.shape[0]
    T = MOE_BLOCK
    fw = EXPERT_DIM // MOE_FSPLIT
    grid_spec = pltpu.PrefetchScalarGridSpec(
        num_scalar_prefetch=1,
        grid=(S // T, N_EXPERTS, MOE_FSPLIT),
        in_specs=[
            pl.BlockSpec((T, D_MODEL), lambda b, e, f, c: (b, 0), pipeline_mode=pl.Buffered(1)),
            pl.BlockSpec((1, T // MOE_CHUNK, 1, MOE_CHUNK), lambda b, e, f, c: (e, b, 0, 0)),
            pl.BlockSpec((1, T // MOE_CHUNK, 1, MOE_CHUNK), lambda b, e, f, c: (e, b, 0, 0)),
            pl.BlockSpec((1, D_MODEL, fw), lambda b, e, f, c: (e, 0, f)),
            pl.BlockSpec((1, D_MODEL, fw), lambda b, e, f, c: (e, 0, f)),
            pl.BlockSpec((1, fw, D_MODEL), lambda b, e, f, c: (e, f, 0)),
        ],
        out_specs=pl.BlockSpec((T, D_MODEL), lambda b, e, f, c: (b, 0)),
        scratch_shapes=[
            pltpu.VMEM((T + MOE_TILE, D_MODEL), BF16),
            pltpu.VMEM((T + MOE_TILE, 1), F32),
            pltpu.VMEM((T + MOE_TILE, D_MODEL), F32),
        ],
    )
    return pl.pallas_call(
        _moe_kernel,
        grid_spec=grid_spec,
        out_shape=jax.ShapeDtypeStruct((S, D_MODEL), F32),
        compiler_params=_params(("arbitrary", "arbitrary", "arbitrary")),
        name="moe_experts",
    )(cum.reshape(-1), h_bf, rank, comb, w1, w3, w2)


def _ln_kernel(x_ref, y_ref, g2_ref, lg_ref, lb_ref, o_ref):
    y = ALPHA * x_ref[...] + (1.0 + g2_ref[...]) * y_ref[...]
    o_ref[...] = _layer_norm(y, lg_ref[...], lb_ref[...])


def _residual_ln(x, y, g2, ln_g, ln_b):
    S = x.shape[0]
    tm = ROW_TILE
    row = lambda i: (i, 0)
    vec = pl.BlockSpec((1, D_MODEL), lambda i: (0, 0))
    return pl.pallas_call(
        _ln_kernel,
        grid=(S // tm,),
        in_specs=[pl.BlockSpec((tm, D_MODEL), row), pl.BlockSpec((tm, D_MODEL), row), vec, vec, vec],
        out_specs=pl.BlockSpec((tm, D_MODEL), row),
        out_shape=jax.ShapeDtypeStruct((S, D_MODEL), F32),
        compiler_params=_params(("parallel",)),
        name="residual_ln",
    )(x, y, g2, ln_g.reshape(1, D_MODEL), ln_b.reshape(1, D_MODEL))


def _rope_tables(S):
    d = DA_HEAD_DIM
    inv_freq = 1.0 / (ROPE_THETA ** (jnp.arange(0, d, 2, dtype=F32) / d))
    ang = jnp.arange(S, dtype=F32)[:, None] * inv_freq[None, :]
    cos = jnp.cos(ang)
    sin = jnp.sin(ang)
    reps = LANES // d
    cos_t = jnp.tile(jnp.concatenate([cos, cos], axis=1), (1, reps))
    sin_t = jnp.tile(jnp.concatenate([-sin, sin], axis=1), (1, reps))
    return cos_t, sin_t


def kernel(x, c, w_ada, b_ada, w_in, lam_qk, attn_norm_g, hg_lb_logits, hg_norm_g, pool_w, pool_scale, w_out, ln1_g, ln1_b, ln2_g, ln2_b, ffn_w1, ffn_w3, ffn_w2, router_w, exp_w1, exp_w3, exp_w2):
    B, S, D = x.shape
    assert B == 1 and D == D_MODEL and S % MOE_BLOCK == 0
    xs = x.reshape(S, D)
    mod = _modulation(c, w_ada, b_ada)
    cos_t, sin_t = _rope_tables(S)
    for l in range(DEPTH):
        sh1, sc1, g1, sh2, sc2, g2 = (mod[l, j] for j in range(6))
        qt, k, vt, hg, pu = _in_projection(xs, sc1, sh1, w_in[l].astype(BF16), cos_t, sin_t)
        a = _diff_attention(qt, k, vt, lam_qk[l], attn_norm_g[l], l)
        r = _hgrn2(hg, hg_lb_logits, hg_norm_g[l], l)
        p = _multiscale_pool(pu, pool_w[l], pool_scale[l])
        xs = _out_projection(a, r, p, xs, w_out[l].astype(BF16), g1, ln1_g[l], ln1_b[l])
        if l % 2 == 0:
            i = l // 2
            xs = _dense_ffn(xs, sc2, sh2, g2, ffn_w1[i].astype(BF16), ffn_w3[i].astype(BF16),
                            ffn_w2[i].astype(BF16), ln2_g[l], ln2_b[l])
        else:
            i = l // 2
            h_bf, rank, comb, cum = _router(xs, sc2, sh2, router_w[i])
            y = _moe_ffn(h_bf, rank, comb, cum, exp_w1[i].astype(BF16), exp_w3[i].astype(BF16),
                         exp_w2[i].astype(BF16))
            xs = _residual_ln(xs, y, g2, ln2_g[l], ln2_b[l])
    return xs.reshape(B, S, D)
```

```python
import functools
import math

import jax
import jax.numpy as jnp
from jax import lax
from jax.experimental import pallas as pl
from jax.experimental.pallas import tpu as pltpu

F32 = jnp.float32
BF16 = jnp.bfloat16
HIGHEST = lax.Precision.HIGHEST

D_MODEL = 1024
DEPTH = 2
DA_HEADS = 4
DA_WIDTH = D_MODEL // 2
DA_HEAD_DIM = DA_WIDTH // (2 * DA_HEADS)
DA_VAL_DIM = 2 * DA_HEAD_DIM
HG_HEADS = 4
HG_WIDTH = D_MODEL // 4
HG_DIM = HG_WIDTH // HG_HEADS
POOL_WINDOWS = (2, 4, 8, 16)
POOL_WIDTH = D_MODEL // 4
POOL_GROUP = POOL_WIDTH // len(POOL_WINDOWS)
D_MIX = DA_WIDTH + HG_WIDTH + POOL_WIDTH
D_IN = 3 * DA_WIDTH + 4 * HG_WIDTH + POOL_WIDTH
ROPE_THETA = 10000.0
MASK_VALUE = -1e30
TINY = 1e-30
FFN_DIM = 11 * D_MODEL // 4
N_EXPERTS = 8
EXPERT_DIM = 7 * D_MODEL // 2
ALPHA = (2 * DEPTH) ** 0.25
EPS = 1e-5
LOG2_E = math.log2(math.e)

LANES = 128
MXU_DIM = 256
VMEM_LIMIT = 60 * 1024 * 1024

ROW_TILE = 512
ATTN_BLOCK = 512
ATTN_QUERY_GROUP = 256
ATTN_SUM_ROWS = 16
ATTN_VT_ROWS = DA_VAL_DIM + ATTN_SUM_ROWS
HG_BLOCK = 512
HG_CHUNK = 16
HG_UNROLL = 4
POOL_HALO = 16
MOE_BLOCK = 2048
MOE_TILE = 192
MOE_CHUNK = 256
MOE_WINDOW = 5
MOE_FSPLIT = 2


def _params(sem, vmem=VMEM_LIMIT):
    return pltpu.CompilerParams(dimension_semantics=sem, vmem_limit_bytes=vmem)


def _layer_norm(y, g, b):
    mu = jnp.mean(y, axis=-1, keepdims=True)
    d = y - mu
    var = jnp.mean(d * d, axis=-1, keepdims=True)
    return d * lax.rsqrt(var + EPS) * g + b


def _silu(x):
    return x * jax.nn.sigmoid(x)


def _head_block_mask(n, group):
    r = lax.broadcasted_iota(jnp.int32, (n, n), 0) // group
    c = lax.broadcasted_iota(jnp.int32, (n, n), 1) // group
    return r == c


def _mod_kernel(c_ref, w_ref, b_ref, o_ref):
    cond = _silu(c_ref[...])
    cond8 = jnp.broadcast_to(cond, (8, D_MODEL))
    r = jnp.dot(cond8, w_ref[0], preferred_element_type=F32, precision=HIGHEST)
    o_ref[0] = r[0:1] + b_ref[0]


def _modulation(c, w_ada, b_ada):
    n = 6 * DEPTH
    b3 = b_ada.reshape(n, 1, D_MODEL)
    out = pl.pallas_call(
        _mod_kernel,
        grid=(DEPTH, 6),
        in_specs=[
            pl.BlockSpec((1, D_MODEL), lambda l, j: (0, 0)),
            pl.BlockSpec((1, D_MODEL, D_MODEL), lambda l, j: (l, 0, j)),
            pl.BlockSpec((1, 1, D_MODEL), lambda l, j: (l * 6 + j, 0, 0)),
        ],
        out_specs=pl.BlockSpec((1, 1, D_MODEL), lambda l, j: (l * 6 + j, 0, 0)),
        out_shape=jax.ShapeDtypeStruct((n, 1, D_MODEL), F32),
        compiler_params=_params(("parallel", "parallel")),
        name="adaln_mod",
    )(c, w_ada, b3)
    return out.reshape(DEPTH, 6, 1, D_MODEL)


def _inproj_kernel(x_ref, sc_ref, sh_ref, w_ref, wvt_ref, cos_ref, sin_ref,
                   qt_ref, k_ref, vt_ref, hg_ref, pu_ref):
    h = (x_ref[...] * (1.0 + sc_ref[...]) + sh_ref[...]).astype(BF16)
    qk = jnp.dot(h, w_ref[:, : 2 * DA_WIDTH], preferred_element_type=F32)
    reps = 2 * DA_WIDTH // LANES
    cos = jnp.tile(cos_ref[...], (1, reps))
    sin = jnp.tile(sin_ref[...], (1, reps))
    lane = lax.broadcasted_iota(jnp.int32, qk.shape, 1)
    half = DA_HEAD_DIM // 2
    first_half = (lane & half) == 0
    width = 2 * DA_WIDTH
    partner = jnp.where(first_half, pltpu.roll(qk, width - half, 1), pltpu.roll(qk, half, 1))
    rot = qk * cos + partner * sin
    qt_ref[0] = (rot[:, :DA_WIDTH] * (DA_HEAD_DIM ** -0.5 * LOG2_E)).T.astype(BF16)
    k_ref[...] = rot[:, DA_WIDTH:].astype(BF16)
    o = 2 * DA_WIDTH
    vt = lax.dot_general(wvt_ref[...], h, (((1,), (1,)), ((), ())),
                         preferred_element_type=F32).astype(BF16)
    ones = jnp.ones((ATTN_SUM_ROWS, vt.shape[1]), BF16)
    pieces = []
    for hd in range(DA_HEADS):
        pieces += [vt[hd * DA_VAL_DIM:(hd + 1) * DA_VAL_DIM], ones]
    vt_ref[0] = jnp.concatenate(pieces, axis=0)
    o += DA_WIDTH
    hg_ref[...] = jnp.dot(h, w_ref[:, o:o + 4 * HG_WIDTH], preferred_element_type=F32)
    o += 4 * HG_WIDTH
    pu_ref[...] = jnp.dot(h, w_ref[:, o:o + POOL_WIDTH], preferred_element_type=F32)


def _in_projection(x, sc, sh, w_in_bf, cos_t, sin_t):
    S = x.shape[0]
    tm = ATTN_BLOCK
    row = lambda i: (i, 0)
    fixed = lambda i: (0, 0)
    return pl.pallas_call(
        _inproj_kernel,
        grid=(S // tm,),
        in_specs=[
            pl.BlockSpec((tm, D_MODEL), row),
            pl.BlockSpec((1, D_MODEL), fixed),
            pl.BlockSpec((1, D_MODEL), fixed),
            pl.BlockSpec((D_MODEL, D_IN), fixed),
            pl.BlockSpec((DA_WIDTH, D_MODEL), fixed),
            pl.BlockSpec((tm, LANES), row),
            pl.BlockSpec((tm, LANES), row),
        ],
        out_specs=[
            pl.BlockSpec((1, DA_WIDTH, tm), lambda i: (i, 0, 0)),
            pl.BlockSpec((tm, DA_WIDTH), row),
            pl.BlockSpec((1, DA_HEADS * ATTN_VT_ROWS, tm), lambda i: (i, 0, 0)),
            pl.BlockSpec((tm, 4 * HG_WIDTH), row),
            pl.BlockSpec((tm, POOL_WIDTH), row),
        ],
        out_shape=[
            jax.ShapeDtypeStruct((S // tm, DA_WIDTH, tm), BF16),
            jax.ShapeDtypeStruct((S, DA_WIDTH), BF16),
            jax.ShapeDtypeStruct((S // tm, DA_HEADS * ATTN_VT_ROWS, tm), BF16),
            jax.ShapeDtypeStruct((S, 4 * HG_WIDTH), F32),
            jax.ShapeDtypeStruct((S, POOL_WIDTH), F32),
        ],
        compiler_params=_params(("parallel",)),
        name="in_proj",
    )(x, sc, sh, w_in_bf, w_in_bf[:, 2 * DA_WIDTH:3 * DA_WIDTH].T, cos_t, sin_t)


def _attn_kernel(lam_ref, g_ref, qt_ref, k_ref, vt_ref, o_ref, m_scr, a_scr, c_scr, acc_scr, s_scr,
                 p_scr, *, lam_init):
    bq = qt_ref.shape[2]
    bk = vt_ref.shape[2]
    cg = ATTN_QUERY_GROUP
    i = pl.program_id(1)
    qt = qt_ref[0]
    rowq = lax.broadcasted_iota(jnp.int32, qt.shape, 0)
    zero = jnp.zeros_like(qt)
    maps = (jnp.where(rowq < DA_HEAD_DIM, qt, zero), jnp.where(rowq >= DA_HEAD_DIM, qt, zero))
    groups = [(a * bq + c0, c0, maps[a][:, c0:c0 + cg]) for a in range(2) for c0 in range(0, bq, cg)]
    m_scr[...] = jnp.full(m_scr.shape, -jnp.inf, F32)
    acc_scr[...] = jnp.zeros(acc_scr.shape, F32)
    hk = bk // 2
    key_i = lax.broadcasted_iota(jnp.int32, (hk, cg), 0)
    qry_i = lax.broadcasted_iota(jnp.int32, (hk, cg), 1)
    ng = len(groups)
    last = ng - 1

    def cols_of(g):
        return slice(groups[g][0], groups[g][0] + cg)

    def ccols_of(sset, g):
        return slice(sset * 2 * bq + groups[g][0], sset * 2 * bq + groups[g][0] + cg)

    def mask_mode(key_offset, g):
        c0 = groups[g][1]
        if key_offset + hk - 1 <= c0:
            return None
        if key_offset > c0 + cg - 1:
            return "skip"
        return key_offset

    def score_stage(hb, sset, g, key_offset):
        kb = k_ref[pl.ds(pl.multiple_of(hb * hk, hk), hk), :]
        s = jnp.dot(kb, groups[g][2], preferred_element_type=F32)
        if key_offset is not None:
            s = jnp.where(key_i + key_offset <= qry_i + groups[g][1], s, MASK_VALUE)
        s_scr[sset * ng + g] = s
        c_scr[:, ccols_of(sset, g)] = jnp.max(s, axis=0, keepdims=True)

    def max_stage(sset, g):
        cols = cols_of(g)
        m_prev = m_scr[:, cols]
        m_new = jnp.maximum(m_prev, c_scr[:, ccols_of(sset, g)])
        a_scr[:, cols] = jnp.exp2(m_prev - m_new)
        m_scr[:, cols] = m_new

    def exp_stage(sset, g, vt_half, carried, defer):
        cols = cols_of(g)
        p = jnp.exp2(s_scr[sset * ng + g] - m_scr[:, cols])
        alpha = a_scr[:, cols]
        pb = p.astype(BF16)
        acc = acc_scr[:, cols]
        if carried is not None:
            acc = acc + carried
        if defer:
            acc_scr[:, cols] = alpha * acc
            p_scr[...] = pb
        else:
            acc_scr[:, cols] = alpha * acc + jnp.dot(vt_half, pb, preferred_element_type=F32)

    def iteration(t, look, diagonal):
        vt_a = vt_ref[t, :, 0:hk]
        vt_b = vt_ref[t, :, hk:bk]
        carried = jnp.dot(vt_ref[jnp.maximum(t - 1, 0), :, hk:bk], p_scr[...],
                          preferred_element_type=F32)
        if diagonal:
            b_items = [(g, mask_mode(hk, g)) for g in range(ng) if mask_mode(hk, g) != "skip"]
        else:
            b_items = [(g, None) for g in range(ng)]
        score_stage(2 * t + 1, 1, *b_items[0])
        for g in range(ng):
            if g + 1 < len(b_items):
                score_stage(2 * t + 1, 1, *b_items[g + 1])
            if g + 1 < ng:
                max_stage(0, g + 1)
            else:
                max_stage(1, b_items[0][0])
            exp_stage(0, g, vt_a, carried if g == last else None, False)
        if not diagonal:
            score_stage(2 * t + 2, 0, 0, mask_mode(0, 0) if look == "diag" else None)
        for n, (g, _) in enumerate(b_items):
            if not diagonal and g + 1 < ng:
                score_stage(2 * t + 2, 0, g + 1, mask_mode(0, g + 1) if look == "diag" else None)
            if n + 1 < len(b_items):
                max_stage(1, b_items[n + 1][0])
            elif not diagonal:
                max_stage(0, 0)
            exp_stage(1, g, vt_b, None, not diagonal and g == last)

    p_scr[...] = jnp.zeros(p_scr.shape, BF16)
    for g in range(ng):
        score_stage(0, 0, g, (0 - i) * bq)
    max_stage(0, 0)

    def body(t, carry):
        iteration(t, None, False)
        return carry

    lax.fori_loop(0, i - 1, body, 0)

    @pl.when(i > 0)
    def _():
        iteration(i - 1, "diag", False)

    iteration(i, None, True)

    lq = lam_ref[...]
    lam = (jnp.exp(jnp.sum(lq[0:1] * lq[1:2], axis=1, keepdims=True))
           - jnp.exp(jnp.sum(lq[2:3] * lq[3:4], axis=1, keepdims=True)) + lam_init)
    o = acc_scr[0:DA_VAL_DIM, :] / acc_scr[DA_VAL_DIM:DA_VAL_DIM + 1, :]
    o = o[:, :bq] - lam * o[:, bq:]
    ms = jnp.mean(o * o, axis=0, keepdims=True)
    o = o * lax.rsqrt(ms + EPS) * g_ref[...] * (1.0 - lam_init)
    o_ref[...] = o.T.astype(BF16)


def _diff_attention(qt, k, vt, lam_qk, attn_g, layer):
    S = k.shape[0]
    bq = bk = ATTN_BLOCK
    nk = S // bk
    lam_init = 0.8 - 0.6 * math.exp(-0.3 * layer)
    return pl.pallas_call(
        functools.partial(_attn_kernel, lam_init=lam_init),
        grid=(DA_HEADS, S // bq),
        in_specs=[
            pl.BlockSpec((4, DA_HEAD_DIM), lambda h, i: (0, 0)),
            pl.BlockSpec((DA_VAL_DIM, 1), lambda h, i: (0, 0)),
            pl.BlockSpec((1, LANES, bq), lambda h, i: (i, h, 0)),
            pl.BlockSpec((S, LANES), lambda h, i: (0, h)),
            pl.BlockSpec((nk, ATTN_VT_ROWS, bk), lambda h, i: (0, h, 0)),
        ],
        out_specs=pl.BlockSpec((bq, DA_VAL_DIM), lambda h, i: (i, h)),
        out_shape=jax.ShapeDtypeStruct((S, DA_WIDTH), BF16),
        scratch_shapes=[
            pltpu.VMEM((1, 2 * bq), F32),
            pltpu.VMEM((1, 2 * bq), F32),
            pltpu.VMEM((1, 2 * 2 * bq), F32),
            pltpu.VMEM((ATTN_VT_ROWS, 2 * bq), F32),
            pltpu.VMEM((2 * (2 * bq // ATTN_QUERY_GROUP), bk // 2, ATTN_QUERY_GROUP), F32),
            pltpu.VMEM((bk // 2, ATTN_QUERY_GROUP), BF16),
        ],
        compiler_params=_params(("parallel", "parallel")),
        name="diff_attn",
    )(lam_qk, attn_g.reshape(DA_VAL_DIM, 1), qt, k, vt)


def _hgrn_kernel(lbl_ref, g_ref, hg_ref, o_ref, st_scr, o_scr, *, layer):
    T = hg_ref.shape[0]
    c = HG_CHUNK
    W = HG_WIDTH

    @pl.when(pl.program_id(0) == 0)
    def _():
        st_scr[...] = jnp.zeros(st_scr.shape, F32)

    lg = lbl_ref[...]
    ex = jnp.exp(lg - jnp.max(lg, axis=0, keepdims=True))
    prob = ex / jnp.sum(ex, axis=0, keepdims=True)
    lb = jnp.zeros((1, W), F32)
    for li in range(1, layer + 1):
        lb = lb + prob[li:li + 1]

    same_head = _head_block_mask(W, HG_DIM)
    ones_bd = same_head.astype(BF16)
    rowi = lax.broadcasted_iota(jnp.int32, (c, W), 0)

    def body(s, carry):
        sl = pl.ds(pl.multiple_of(s * c, c), c)
        qf = hg_ref[sl, 0:W]
        z = hg_ref[sl, W:2 * W]
        iv = hg_ref[sl, 2 * W:3 * W]
        f = lb + (1.0 - lb) * jax.nn.sigmoid(z)
        logf = jnp.log(jnp.maximum(f, TINY))
        key = (1.0 - lb) * jax.nn.sigmoid(-z)
        val = _silu(iv)
        b = logf
        sft = 1
        while sft < c:
            b = b + jnp.where(rowi >= sft, pltpu.roll(b, sft, 0), 0.0)
            sft *= 2
        es = [(qf * key).astype(BF16)]
        vrs = [val]
        for d in range(1, c):
            kr = pltpu.roll(key, d, 0)
            br = pltpu.roll(b, d, 0)
            e = jnp.where(rowi >= d, qf * kr * jnp.exp(b - br), 0.0)
            es.append(e.astype(BF16))
            vrs.append(pltpu.roll(val, d, 0))
        estack = jnp.concatenate(es, axis=0)
        r = jnp.dot(estack, ones_bd, preferred_element_type=F32)
        o = r[0:c] * vrs[0]
        for d in range(1, c):
            o = o + r[d * c:(d + 1) * c] * vrs[d]
        st = st_scr[...]
        qd = (qf * jnp.exp(b)).astype(BF16)
        o = o + lax.dot_general(qd, st.astype(BF16), (((1,), (1,)), ((), ())),
                                preferred_element_type=F32)
        o_scr[sl, :] = o
        b_last = b[c - 1:c]
        kd = (key * jnp.exp(b_last - b)).astype(BF16)
        upd = lax.dot_general(val.astype(BF16), kd, (((0,), (0,)), ((), ())),
                              preferred_element_type=F32)
        st_scr[...] = jnp.exp(b_last) * st + jnp.where(same_head, upd, 0.0)
        return carry

    lax.fori_loop(0, T // c, body, 0, unroll=HG_UNROLL)

    o = o_scr[...]
    ms = jnp.dot(o * o, same_head.astype(F32), preferred_element_type=F32,
                 precision=HIGHEST) * (1.0 / HG_DIM)
    gate = hg_ref[:, 3 * W:4 * W]
    o_ref[...] = (o * lax.rsqrt(ms + EPS) * g_ref[...] * _silu(gate)).astype(BF16)


def _hgrn2(hg, lb_logits, norm_g, layer):
    S = hg.shape[0]
    T = HG_BLOCK
    g_t = jnp.tile(norm_g.reshape(1, HG_DIM), (1, HG_HEADS))
    return pl.pallas_call(
        functools.partial(_hgrn_kernel, layer=layer),
        grid=(S // T,),
        in_specs=[
            pl.BlockSpec((DEPTH, HG_WIDTH), lambda i: (0, 0)),
            pl.BlockSpec((1, HG_WIDTH), lambda i: (0, 0)),
            pl.BlockSpec((T, 4 * HG_WIDTH), lambda i: (i, 0)),
        ],
        out_specs=pl.BlockSpec((T, HG_WIDTH), lambda i: (i, 0)),
        out_shape=jax.ShapeDtypeStruct((S, HG_WIDTH), BF16),
        scratch_shapes=[
            pltpu.VMEM((HG_WIDTH, HG_WIDTH), F32),
            pltpu.VMEM((T, HG_WIDTH), F32),
        ],
        compiler_params=_params(("arbitrary",)),
        name="hgrn2",
    )(lb_logits, g_t, hg)


def _pool_kernel(u_ref, w_ref, sc_ref, o_ref, halo_scr):
    T = u_ref.shape[0]
    H = POOL_HALO
    blk = pl.program_id(0)

    @pl.when(blk == 0)
    def _():
        halo_scr[...] = jnp.zeros(halo_scr.shape, F32)

    u = u_ref[...]
    ext = jnp.concatenate([halo_scr[...], u], axis=0)
    halo_scr[...] = u[T - H:, :]
    sums = []
    s = ext
    w = 1
    while w < POOL_WINDOWS[-1]:
        s = s + pltpu.roll(s, w, 0)
        w *= 2
        sums.append(s[H:, :])
    t1 = (blk * T + 1 + lax.broadcasted_iota(jnp.int32, (T, POOL_WIDTH), 0)).astype(F32)
    lane = lax.broadcasted_iota(jnp.int32, (T, POOL_WIDTH), 1)
    pooled = None
    for gi, win in enumerate(POOL_WINDOWS):
        mean = sums[gi] / jnp.minimum(t1, float(win))
        pooled = mean if pooled is None else jnp.where(lane >= gi * POOL_GROUP, mean, pooled)
    pooled = pooled - u
    y = jnp.dot(pooled.astype(BF16), w_ref[...], preferred_element_type=F32)
    o_ref[...] = (y * sc_ref[...]).astype(BF16)


def _multiscale_pool(pu, pool_w, pool_scale):
    S = pu.shape[0]
    T = ROW_TILE
    w_bd = jax.scipy.linalg.block_diag(*[pool_w[g] for g in range(len(POOL_WINDOWS))]).astype(BF16)
    return pl.pallas_call(
        _pool_kernel,
        grid=(S // T,),
        in_specs=[
            pl.BlockSpec((T, POOL_WIDTH), lambda i: (i, 0)),
            pl.BlockSpec((POOL_WIDTH, POOL_WIDTH), lambda i: (0, 0)),
            pl.BlockSpec((1, POOL_WIDTH), lambda i: (0, 0)),
        ],
        out_specs=pl.BlockSpec((T, POOL_WIDTH), lambda i: (i, 0)),
        out_shape=jax.ShapeDtypeStruct((S, POOL_WIDTH), BF16),
        scratch_shapes=[pltpu.VMEM((POOL_HALO, POOL_WIDTH), F32)],
        compiler_params=_params(("arbitrary",)),
        name="ms_pool",
    )(pu, w_bd, pool_scale.reshape(1, POOL_WIDTH))


def _outproj_kernel(a_ref, r_ref, p_ref, x_ref, w_ref, g1_ref, lg_ref, lb_ref, o_ref):
    m = jnp.dot(a_ref[...], w_ref[0:DA_WIDTH, :], preferred_element_type=F32)
    m += jnp.dot(r_ref[...], w_ref[DA_WIDTH:DA_WIDTH + HG_WIDTH, :], preferred_element_type=F32)
    m += jnp.dot(p_ref[...], w_ref[DA_WIDTH + HG_WIDTH:, :], preferred_element_type=F32)
    y = ALPHA * x_ref[...] + (1.0 + g1_ref[...]) * m
    o_ref[...] = _layer_norm(y, lg_ref[...], lb_ref[...])


def _out_projection(a, r, p, x, w_out_bf, g1, ln_g, ln_b):
    S = x.shape[0]
    tm = ROW_TILE
    row = lambda i: (i, 0)
    fixed = lambda i: (0, 0)
    vec = pl.BlockSpec((1, D_MODEL), fixed)
    return pl.pallas_call(
        _outproj_kernel,
        grid=(S // tm,),
        in_specs=[
            pl.BlockSpec((tm, DA_WIDTH), row),
            pl.BlockSpec((tm, HG_WIDTH), row),
            pl.BlockSpec((tm, POOL_WIDTH), row),
            pl.BlockSpec((tm, D_MODEL), row),
            pl.BlockSpec((D_MIX, D_MODEL), fixed),
            vec, vec, vec,
        ],
        out_specs=pl.BlockSpec((tm, D_MODEL), row),
        out_shape=jax.ShapeDtypeStruct((S, D_MODEL), F32),
        compiler_params=_params(("parallel",)),
        name="out_proj_ln",
    )(a, r, p, x, w_out_bf, g1, ln_g.reshape(1, D_MODEL), ln_b.reshape(1, D_MODEL))


def _ffn_kernel(x_ref, sc_ref, sh_ref, g2_ref, w1_ref, w3_ref, w2_ref, lg_ref, lb_ref, o_ref):
    x = x_ref[...]
    h = (x * (1.0 + sc_ref[...]) + sh_ref[...]).astype(BF16)
    acc = jnp.zeros(x.shape, F32)
    step = 4 * MXU_DIM
    for lo in range(0, FFN_DIM, step):
        hi = min(lo + step, FFN_DIM)
        a = jnp.dot(h, w1_ref[:, lo:hi], preferred_element_type=F32)
        b = jnp.dot(h, w3_ref[:, lo:hi], preferred_element_type=F32)
        acc += jnp.dot((_silu(a) * b).astype(BF16), w2_ref[lo:hi, :], preferred_element_type=F32)
    y = ALPHA * x + (1.0 + g2_ref[...]) * acc
    o_ref[...] = _layer_norm(y, lg_ref[...], lb_ref[...])


def _dense_ffn(x, sc, sh, g2, w1, w3, w2, ln_g, ln_b):
    S = x.shape[0]
    tm = ROW_TILE
    row = lambda i: (i, 0)
    fixed = lambda i: (0, 0)
    vec = pl.BlockSpec((1, D_MODEL), fixed)
    once = pl.Buffered(1)
    return pl.pallas_call(
        _ffn_kernel,
        grid=(S // tm,),
        in_specs=[
            pl.BlockSpec((tm, D_MODEL), row),
            vec, vec, vec,
            pl.BlockSpec((D_MODEL, FFN_DIM), fixed, pipeline_mode=once),
            pl.BlockSpec((D_MODEL, FFN_DIM), fixed, pipeline_mode=once),
            pl.BlockSpec((FFN_DIM, D_MODEL), fixed, pipeline_mode=once),
            vec, vec,
        ],
        out_specs=pl.BlockSpec((tm, D_MODEL), row),
        out_shape=jax.ShapeDtypeStruct((S, D_MODEL), F32),
        compiler_params=_params(("parallel",)),
        name="dense_ffn_ln",
    )(x, sc, sh, g2, w1, w3, w2, ln_g.reshape(1, D_MODEL), ln_b.reshape(1, D_MODEL))


def _router_kernel(x_ref, sc_ref, sh_ref, rw_ref, h_ref, rank_ref, comb_ref, cum_ref):
    T = x_ref.shape[0]
    tc = MOE_CHUNK
    E = N_EXPERTS
    h = x_ref[...] * (1.0 + sc_ref[...]) + sh_ref[...]
    h_ref[...] = h.astype(BF16)
    logits = lax.dot_general(rw_ref[...], h, (((1,), (1,)), ((), ())),
                             preferred_element_type=F32, precision=HIGHEST)
    eidx = lax.broadcasted_iota(jnp.int32, (E, T), 0)
    v1 = jnp.max(logits, axis=0, keepdims=True)
    i1 = jnp.min(jnp.where(logits == v1, eidx, E), axis=0, keepdims=True)
    m1 = eidx == i1
    rest = jnp.where(m1, -jnp.inf, logits)
    v2 = jnp.max(rest, axis=0, keepdims=True)
    i2 = jnp.min(jnp.where(rest == v2, eidx, E), axis=0, keepdims=True)
    m2 = eidx == i2
    e2 = jnp.exp(v2 - v1)
    g1 = 1.0 / (1.0 + e2)
    g2 = e2 / (1.0 + e2)
    comb = jnp.where(m1, g1, 0.0) + jnp.where(m2, g2, 0.0)
    routed = jnp.logical_or(m1, m2)
    ind = routed.astype(BF16)
    r_i = lax.broadcasted_iota(jnp.int32, (tc, tc), 0)
    c_i = lax.broadcasted_iota(jnp.int32, (tc, tc), 1)
    strict = (r_i < c_i).astype(BF16)
    lane = lax.broadcasted_iota(jnp.int32, (E, LANES), 1)
    running = jnp.zeros((E, 1), F32)
    cum = jnp.zeros((E, LANES), F32)
    for ci in range(T // tc):
        sl = slice(ci * tc, (ci + 1) * tc)
        ind_c = ind[:, sl]
        rank_c = jnp.dot(ind_c, strict, preferred_element_type=F32) + running
        rank_ref[:, ci] = jnp.where(routed[:, sl], rank_c, -1.0).reshape(E, 1, tc)
        comb_ref[:, ci] = comb[:, sl].reshape(E, 1, tc)
        cum = jnp.where(lane == ci, running, cum)
        running = running + jnp.sum(ind_c.astype(F32), axis=1, keepdims=True)
    cum = jnp.where(lane == T // tc, running, cum)
    cum_ref[0] = cum.astype(jnp.int32)


def _router(x, sc, sh, router_w):
    S = x.shape[0]
    T = MOE_BLOCK
    E = N_EXPERTS
    fixed = lambda i: (0, 0)
    vec = pl.BlockSpec((1, D_MODEL), fixed)
    return pl.pallas_call(
        _router_kernel,
        grid=(S // T,),
        in_specs=[
            pl.BlockSpec((T, D_MODEL), lambda i: (i, 0)),
            vec, vec,
            pl.BlockSpec((E, D_MODEL), fixed),
        ],
        out_specs=[
            pl.BlockSpec((T, D_MODEL), lambda i: (i, 0)),
            pl.BlockSpec((E, T // MOE_CHUNK, 1, MOE_CHUNK), lambda i: (0, i, 0, 0)),
            pl.BlockSpec((E, T // MOE_CHUNK, 1, MOE_CHUNK), lambda i: (0, i, 0, 0)),
            pl.BlockSpec((1, E, LANES), lambda i: (i, 0, 0)),
        ],
        out_shape=[
            jax.ShapeDtypeStruct((S, D_MODEL), BF16),
            jax.ShapeDtypeStruct((E, S // MOE_CHUNK, 1, MOE_CHUNK), F32),
            jax.ShapeDtypeStruct((E, S // MOE_CHUNK, 1, MOE_CHUNK), F32),
            jax.ShapeDtypeStruct((S // T, E, LANES), jnp.int32),
        ],
        compiler_params=_params(("parallel",)),
        name="moe_router",
    )(x, sc, sh, router_w.T)


def _moe_kernel(cum_ref, h_ref, rank_ref, comb_ref, w1_ref, w3_ref, w2_ref, y_ref, xe_scr, gate_scr,
                out_scr):
    T = h_ref.shape[0]
    tr, tc, win = MOE_TILE, MOE_CHUNK, MOE_WINDOW
    nchunk = T // tc
    b = pl.program_id(0)
    e = pl.program_id(1)
    f = pl.program_id(2)

    @pl.when(jnp.logical_and(e == 0, f == 0))
    def _():
        y_ref[...] = jnp.zeros(y_ref.shape, F32)

    base = (b * N_EXPERTS + e) * LANES
    count = cum_ref[base + nchunk]
    ntiles = (count + (tr - 1)) // tr

    def token_row(ref, first, n):
        blk = ref[0, pl.ds(first, n)]
        return jnp.concatenate([blk[c] for c in range(n)], axis=1)

    def tile(r, carry):
        lo = r * tr
        rows = pl.ds(pl.multiple_of(lo, 16), tr)

        def hits(first, n):
            want = (lo + lax.broadcasted_iota(jnp.int32, (tr, n * tc), 0)).astype(F32)
            return token_row(rank_ref, first, n) == want

        @pl.when(f == 0)
        def _():
            hit = hits(0, nchunk)
            xe_scr[rows, :] = jnp.dot(hit.astype(BF16), h_ref[...],
                                      preferred_element_type=F32).astype(BF16)
            gate_scr[rows, :] = jnp.sum(jnp.where(hit, token_row(comb_ref, 0, nchunk), 0.0),
                                        axis=1, keepdims=True)

        xe = xe_scr[rows, :]
        a = jnp.dot(xe, w1_ref[0], preferred_element_type=F32)
        g = jnp.dot(xe, w3_ref[0], preferred_element_type=F32)
        mid = (_silu(a) * g).astype(BF16)
        part = jnp.dot(mid, w2_ref[0], preferred_element_type=F32)

        @pl.when(f == 0)
        def _():
            out_scr[rows, :] = part

        if MOE_FSPLIT > 2:
            @pl.when(jnp.logical_and(f > 0, f < MOE_FSPLIT - 1))
            def _():
                out_scr[rows, :] += part

        @pl.when(f == MOE_FSPLIT - 1)
        def _():
            out_bf = ((out_scr[rows, :] + part) * gate_scr[rows, :]).astype(BF16)

            def scatter(first, n):
                start = first * tc
                if not isinstance(start, int):
                    start = pl.multiple_of(start, tc)
                y_ref[pl.ds(start, n * tc), :] += lax.dot_general(
                    hits(first, n).astype(BF16), out_bf, (((0,), (0,)), ((), ())),
                    preferred_element_type=F32)

            c_lo = jnp.int32(0)
            c_hi = jnp.int32(-1)
            for c in range(nchunk):
                c_lo += (cum_ref[base + c + 1] <= lo).astype(jnp.int32)
                c_hi += (cum_ref[base + c] < lo + tr).astype(jnp.int32)
            fits = c_hi - c_lo < win

            @pl.when(fits)
            def _():
                scatter(jnp.minimum(c_lo, nchunk - win), win)

            @pl.when(jnp.logical_not(fits))
            def _():
                scatter(0, nchunk)

        return carry

    lax.fori_loop(0, ntiles, tile, 0)


def _moe_ffn(h_bf, rank, comb, cum, w1, w3, w2):
    S = h_bf.shape[0]
    T = MOE_BLOCK
    fw = EXPERT_DIM // MOE_FSPLIT
    grid_spec = pltpu.PrefetchScalarGridSpec(
        num_scalar_prefetch=1,
        grid=(S // T, N_EXPERTS, MOE_FSPLIT),
        in_specs=[
            pl.BlockSpec((T, D_MODEL), lambda b, e, f, c: (b, 0), pipeline_mode=pl.Buffered(1)),
            pl.BlockSpec((1, T // MOE_CHUNK, 1, MOE_CHUNK), lambda b, e, f, c: (e, b, 0, 0)),
            pl.BlockSpec((1, T // MOE_CHUNK, 1, MOE_CHUNK), lambda b, e, f, c: (e, b, 0, 0)),
            pl.BlockSpec((1, D_MODEL, fw), lambda b, e, f, c: (e, 0, f)),
            pl.BlockSpec((1, D_MODEL, fw), lambda b, e, f, c: (e, 0, f)),
            pl.BlockSpec((1, fw, D_MODEL), lambda b, e, f, c: (e, f, 0)),
        ],
        out_specs=pl.BlockSpec((T, D_MODEL), lambda b, e, f, c: (b, 0)),
        scratch_shapes=[
            pltpu.VMEM((T + MOE_TILE, D_MODEL), BF16),
            pltpu.VMEM((T + MOE_TILE, 1), F32),
            pltpu.VMEM((T + MOE_TILE, D_MODEL), F32),
        ],
    )
    return pl.pallas_call(
        _moe_kernel,
        grid_spec=grid_spec,
        out_shape=jax.ShapeDtypeStruct((S, D_MODEL), F32),
        compiler_params=_params(("arbitrary", "arbitrary", "arbitrary")),
        name="moe_experts",
    )(cum.reshape(-1), h_bf, rank, comb, w1, w3, w2)


def _ln_kernel(x_ref, y_ref, g2_ref, lg_ref, lb_ref, o_ref):
    y = ALPHA * x_ref[...] + (1.0 + g2_ref[...]) * y_ref[...]
    o_ref[...] = _layer_norm(y, lg_ref[...], lb_ref[...])


def _residual_ln(x, y, g2, ln_g, ln_b):
    S = x.shape[0]
    tm = ROW_TILE
    row = lambda i: (i, 0)
    vec = pl.BlockSpec((1, D_MODEL), lambda i: (0, 0))
    return pl.pallas_call(
        _ln_kernel,
        grid=(S // tm,),
        in_specs=[pl.BlockSpec((tm, D_MODEL), row), pl.BlockSpec((tm, D_MODEL), row), vec, vec, vec],
        out_specs=pl.BlockSpec((tm, D_MODEL), row),
        out_shape=jax.ShapeDtypeStruct((S, D_MODEL), F32),
        compiler_params=_params(("parallel",)),
        name="residual_ln",
    )(x, y, g2, ln_g.reshape(1, D_MODEL), ln_b.reshape(1, D_MODEL))


def _rope_tables(S):
    d = DA_HEAD_DIM
    inv_freq = 1.0 / (ROPE_THETA ** (jnp.arange(0, d, 2, dtype=F32) / d))
    ang = jnp.arange(S, dtype=F32)[:, None] * inv_freq[None, :]
    cos = jnp.cos(ang)
    sin = jnp.sin(ang)
    reps = LANES // d
    cos_t = jnp.tile(jnp.concatenate([cos, cos], axis=1), (1, reps))
    sin_t = jnp.tile(jnp.concatenate([-sin, sin], axis=1), (1, reps))
    return cos_t, sin_t


def kernel(x, c, w_ada, b_ada, w_in, lam_qk, attn_norm_g, hg_lb_logits, hg_norm_g, pool_w, pool_scale, w_out, ln1_g, ln1_b, ln2_g, ln2_b, ffn_w1, ffn_w3, ffn_w2, router_w, exp_w1, exp_w3, exp_w2):
    B, S, D = x.shape
    assert B == 1 and D == D_MODEL and S % MOE_BLOCK == 0
    xs = x.reshape(S, D)
    mod = _modulation(c, w_ada, b_ada)
    cos_t, sin_t = _rope_tables(S)
    for l in range(DEPTH):
        sh1, sc1, g1, sh2, sc2, g2 = (mod[l, j] for j in range(6))
        qt, k, vt, hg, pu = _in_projection(xs, sc1, sh1, w_in[l].astype(BF16), cos_t, sin_t)
        a = _diff_attention(qt, k, vt, lam_qk[l], attn_norm_g[l], l)
        r = _hgrn2(hg, hg_lb_logits, hg_norm_g[l], l)
        p = _multiscale_pool(pu, pool_w[l], pool_scale[l])
        xs = _out_projection(a, r, p, xs, w_out[l].astype(BF16), g1, ln1_g[l], ln1_b[l])
        if l % 2 == 0:
            i = l // 2
            xs = _dense_ffn(xs, sc2, sh2, g2, ffn_w1[i].astype(BF16), ffn_w3[i].astype(BF16),
                            ffn_w2[i].astype(BF16), ln2_g[l], ln2_b[l])
        else:
            i = l // 2
            h_bf, rank, comb, cum = _router(xs, sc2, sh2, router_w[i])
            y = _moe_ffn(h_bf, rank, comb, cum, exp_w1[i].astype(BF16), exp_w3[i].astype(BF16),
                         exp_w2[i].astype(BF16))
            xs = _residual_ln(xs, y, g2, ln2_g[l], ln2_b[l])
    return xs.reshape(B, S, D)
```

```python
import functools
import math

import jax
import jax.numpy as jnp
from jax import lax
from jax.experimental import pallas as pl
from jax.experimental.pallas import tpu as pltpu

F32 = jnp.float32
BF16 = jnp.bfloat16
HIGHEST = lax.Precision.HIGHEST

D_MODEL = 1024
DEPTH = 2
DA_HEADS = 4
DA_WIDTH = D_MODEL // 2
DA_HEAD_DIM = DA_WIDTH // (2 * DA_HEADS)
DA_VAL_DIM = 2 * DA_HEAD_DIM
HG_HEADS = 4
HG_WIDTH = D_MODEL // 4
HG_DIM = HG_WIDTH // HG_HEADS
POOL_WINDOWS = (2, 4, 8, 16)
POOL_WIDTH = D_MODEL // 4
POOL_GROUP = POOL_WIDTH // len(POOL_WINDOWS)
D_MIX = DA_WIDTH + HG_WIDTH + POOL_WIDTH
D_IN = 3 * DA_WIDTH + 4 * HG_WIDTH + POOL_WIDTH
ROPE_THETA = 10000.0
MASK_VALUE = -1e30
TINY = 1e-30
FFN_DIM = 11 * D_MODEL // 4
N_EXPERTS = 8
EXPERT_DIM = 7 * D_MODEL // 2
ALPHA = (2 * DEPTH) ** 0.25
EPS = 1e-5
LOG2_E = math.log2(math.e)

LANES = 128
MXU_DIM = 256
VMEM_LIMIT = 60 * 1024 * 1024

ROW_TILE = 512
ATTN_BLOCK = 512
ATTN_QUERY_GROUP = 256
ATTN_SUM_ROWS = 16
ATTN_VT_ROWS = DA_VAL_DIM + ATTN_SUM_ROWS
HG_BLOCK = 512
HG_CHUNK = 16
HG_UNROLL = 4
POOL_HALO = 16
MOE_BLOCK = 2048
MOE_TILE = 192
MOE_CHUNK = 256
MOE_WINDOW = 5
MOE_FSPLIT = 2


def _params(sem, vmem=VMEM_LIMIT):
    return pltpu.CompilerParams(dimension_semantics=sem, vmem_limit_bytes=vmem)


def _layer_norm(y, g, b):
    mu = jnp.mean(y, axis=-1, keepdims=True)
    d = y - mu
    var = jnp.mean(d * d, axis=-1, keepdims=True)
    return d * lax.rsqrt(var + EPS) * g + b


def _silu(x):
    return x * jax.nn.sigmoid(x)


def _head_block_mask(n, group):
    r = lax.broadcasted_iota(jnp.int32, (n, n), 0) // group
    c = lax.broadcasted_iota(jnp.int32, (n, n), 1) // group
    return r == c


def _mod_kernel(c_ref, w_ref, b_ref, o_ref):
    cond = _silu(c_ref[...])
    cond8 = jnp.broadcast_to(cond, (8, D_MODEL))
    r = jnp.dot(cond8, w_ref[0], preferred_element_type=F32, precision=HIGHEST)
    o_ref[0] = r[0:1] + b_ref[0]


def _modulation(c, w_ada, b_ada):
    n = 6 * DEPTH
    b3 = b_ada.reshape(n, 1, D_MODEL)
    out = pl.pallas_call(
        _mod_kernel,
        grid=(DEPTH, 6),
        in_specs=[
            pl.BlockSpec((1, D_MODEL), lambda l, j: (0, 0)),
            pl.BlockSpec((1, D_MODEL, D_MODEL), lambda l, j: (l, 0, j)),
            pl.BlockSpec((1, 1, D_MODEL), lambda l, j: (l * 6 + j, 0, 0)),
        ],
        out_specs=pl.BlockSpec((1, 1, D_MODEL), lambda l, j: (l * 6 + j, 0, 0)),
        out_shape=jax.ShapeDtypeStruct((n, 1, D_MODEL), F32),
        compiler_params=_params(("parallel", "parallel")),
        name="adaln_mod",
    )(c, w_ada, b3)
    return out.reshape(DEPTH, 6, 1, D_MODEL)


def _inproj_kernel(x_ref, sc_ref, sh_ref, w_ref, wvt_ref, cos_ref, sin_ref,
                   qt_ref, k_ref, vt_ref, hg_ref, pu_ref):
    h = (x_ref[...] * (1.0 + sc_ref[...]) + sh_ref[...]).astype(BF16)
    qk = jnp.dot(h, w_ref[:, : 2 * DA_WIDTH], preferred_element_type=F32)
    reps = 2 * DA_WIDTH // LANES
    cos = jnp.tile(cos_ref[...], (1, reps))
    sin = jnp.tile(sin_ref[...], (1, reps))
    lane = lax.broadcasted_iota(jnp.int32, qk.shape, 1)
    half = DA_HEAD_DIM // 2
    first_half = (lane & half) == 0
    width = 2 * DA_WIDTH
    partner = jnp.where(first_half, pltpu.roll(qk, width - half, 1), pltpu.roll(qk, half, 1))
    rot = qk * cos + partner * sin
    qt_ref[0] = (rot[:, :DA_WIDTH] * (DA_HEAD_DIM ** -0.5 * LOG2_E)).T.astype(BF16)
    k_ref[...] = rot[:, DA_WIDTH:].astype(BF16)
    o = 2 * DA_WIDTH
    vt = lax.dot_general(wvt_ref[...], h, (((1,), (1,)), ((), ())),
                         preferred_element_type=F32).astype(BF16)
    ones = jnp.ones((ATTN_SUM_ROWS, vt.shape[1]), BF16)
    pieces = []
    for hd in range(DA_HEADS):
        pieces += [vt[hd * DA_VAL_DIM:(hd + 1) * DA_VAL_DIM], ones]
    vt_ref[0] = jnp.concatenate(pieces, axis=0)
    o += DA_WIDTH
    hg_ref[...] = jnp.dot(h, w_ref[:, o:o + 4 * HG_WIDTH], preferred_element_type=F32)
    o += 4 * HG_WIDTH
    pu_ref[...] = jnp.dot(h, w_ref[:, o:o + POOL_WIDTH], preferred_element_type=F32)


def _in_projection(x, sc, sh, w_in_bf, cos_t, sin_t):
    S = x.shape[0]
    tm = ATTN_BLOCK
    row = lambda i: (i, 0)
    fixed = lambda i: (0, 0)
    return pl.pallas_call(
        _inproj_kernel,
        grid=(S // tm,),
        in_specs=[
            pl.BlockSpec((tm, D_MODEL), row),
            pl.BlockSpec((1, D_MODEL), fixed),
            pl.BlockSpec((1, D_MODEL), fixed),
            pl.BlockSpec((D_MODEL, D_IN), fixed),
            pl.BlockSpec((DA_WIDTH, D_MODEL), fixed),
            pl.BlockSpec((tm, LANES), row),
            pl.BlockSpec((tm, LANES), row),
        ],
        out_specs=[
            pl.BlockSpec((1, DA_WIDTH, tm), lambda i: (i, 0, 0)),
            pl.BlockSpec((tm, DA_WIDTH), row),
            pl.BlockSpec((1, DA_HEADS * ATTN_VT_ROWS, tm), lambda i: (i, 0, 0)),
            pl.BlockSpec((tm, 4 * HG_WIDTH), row),
            pl.BlockSpec((tm, POOL_WIDTH), row),
        ],
        out_shape=[
            jax.ShapeDtypeStruct((S // tm, DA_WIDTH, tm), BF16),
            jax.ShapeDtypeStruct((S, DA_WIDTH), BF16),
            jax.ShapeDtypeStruct((S // tm, DA_HEADS * ATTN_VT_ROWS, tm), BF16),
            jax.ShapeDtypeStruct((S, 4 * HG_WIDTH), F32),
            jax.ShapeDtypeStruct((S, POOL_WIDTH), F32),
        ],
        compiler_params=_params(("parallel",)),
        name="in_proj",
    )(x, sc, sh, w_in_bf, w_in_bf[:, 2 * DA_WIDTH:3 * DA_WIDTH].T, cos_t, sin_t)


def _attn_kernel(lam_ref, g_ref, qt_ref, k_ref, vt_ref, o_ref, m_scr, a_scr, c_scr, acc_scr, s_scr,
                 p_scr, *, lam_init):
    bq = qt_ref.shape[2]
    bk = vt_ref.shape[2]
    cg = ATTN_QUERY_GROUP
    i = pl.program_id(1)
    qt = qt_ref[0]
    rowq = lax.broadcasted_iota(jnp.int32, qt.shape, 0)
    zero = jnp.zeros_like(qt)
    maps = (jnp.where(rowq < DA_HEAD_DIM, qt, zero), jnp.where(rowq >= DA_HEAD_DIM, qt, zero))
    groups = [(a * bq + c0, c0, maps[a][:, c0:c0 + cg]) for a in range(2) for c0 in range(0, bq, cg)]
    m_scr[...] = jnp.full(m_scr.shape, -jnp.inf, F32)
    acc_scr[...] = jnp.zeros(acc_scr.shape, F32)
    hk = bk // 2
    key_i = lax.broadcasted_iota(jnp.int32, (hk, cg), 0)
    qry_i = lax.broadcasted_iota(jnp.int32, (hk, cg), 1)
    ng = len(groups)
    last = ng - 1

    def cols_of(g):
        return slice(groups[g][0], groups[g][0] + cg)

    def ccols_of(sset, g):
        return slice(sset * 2 * bq + groups[g][0], sset * 2 * bq + groups[g][0] + cg)

    def mask_mode(key_offset, g):
        c0 = groups[g][1]
        if key_offset + hk - 1 <= c0:
            return None
        if key_offset > c0 + cg - 1:
            return "skip"
        return key_offset

    def score_stage(hb, sset, g, key_offset):
        kb = k_ref[pl.ds(pl.multiple_of(hb * hk, hk), hk), :]
        s = jnp.dot(kb, groups[g][2], preferred_element_type=F32)
        if key_offset is not None:
            s = jnp.where(key_i + key_offset <= qry_i + groups[g][1], s, MASK_VALUE)
        s_scr[sset * ng + g] = s
        c_scr[:, ccols_of(sset, g)] = jnp.max(s, axis=0, keepdims=True)

    def max_stage(sset, g):
        cols = cols_of(g)
        m_prev = m_scr[:, cols]
        m_new = jnp.maximum(m_prev, c_scr[:, ccols_of(sset, g)])
        a_scr[:, cols] = jnp.exp2(m_prev - m_new)
        m_scr[:, cols] = m_new

    def exp_stage(sset, g, vt_half, carried, defer):
        cols = cols_of(g)
        p = jnp.exp2(s_scr[sset * ng + g] - m_scr[:, cols])
        alpha = a_scr[:, cols]
        pb = p.astype(BF16)
        acc = acc_scr[:, cols]
        if carried is not None:
            acc = acc + carried
        if defer:
            acc_scr[:, cols] = alpha * acc
            p_scr[...] = pb
        else:
            acc_scr[:, cols] = alpha * acc + jnp.dot(vt_half, pb, preferred_element_type=F32)

    def iteration(t, look, diagonal):
        vt_a = vt_ref[t, :, 0:hk]
        vt_b = vt_ref[t, :, hk:bk]
        carried = jnp.dot(vt_ref[jnp.maximum(t - 1, 0), :, hk:bk], p_scr[...],
                          preferred_element_type=F32)
        if diagonal:
            b_items = [(g, mask_mode(hk, g)) for g in range(ng) if mask_mode(hk, g) != "skip"]
        else:
            b_items = [(g, None) for g in range(ng)]
        score_stage(2 * t + 1, 1, *b_items[0])
        for g in range(ng):
            if g + 1 < len(b_items):
                score_stage(2 * t + 1, 1, *b_items[g + 1])
            if g + 1 < ng:
                max_stage(0, g + 1)
            else:
                max_stage(1, b_items[0][0])
            exp_stage(0, g, vt_a, carried if g == last else None, False)
        if not diagonal:
            score_stage(2 * t + 2, 0, 0, mask_mode(0, 0) if look == "diag" else None)
        for n, (g, _) in enumerate(b_items):
            if not diagonal and g + 1 < ng:
                score_stage(2 * t + 2, 0, g + 1, mask_mode(0, g + 1) if look == "diag" else None)
            if n + 1 < len(b_items):
                max_stage(1, b_items[n + 1][0])
            elif not diagonal:
                max_stage(0, 0)
            exp_stage(1, g, vt_b, None, not diagonal and g == last)

    p_scr[...] = jnp.zeros(p_scr.shape, BF16)
    for g in range(ng):
        score_stage(0, 0, g, (0 - i) * bq)
    max_stage(0, 0)

    def body(t, carry):
        iteration(t, None, False)
        return carry

    lax.fori_loop(0, i - 1, body, 0)

    @pl.when(i > 0)
    def _():
        iteration(i - 1, "diag", False)

    iteration(i, None, True)

    lq = lam_ref[...]
    lam = (jnp.exp(jnp.sum(lq[0:1] * lq[1:2], axis=1, keepdims=True))
           - jnp.exp(jnp.sum(lq[2:3] * lq[3:4], axis=1, keepdims=True)) + lam_init)
    o = acc_scr[0:DA_VAL_DIM, :] / acc_scr[DA_VAL_DIM:DA_VAL_DIM + 1, :]
    o = o[:, :bq] - lam * o[:, bq:]
    ms = jnp.mean(o * o, axis=0, keepdims=True)
    o = o * lax.rsqrt(ms + EPS) * g_ref[...] * (1.0 - lam_init)
    o_ref[...] = o.T.astype(BF16)


def _diff_attention(qt, k, vt, lam_qk, attn_g, layer):
    S = k.shape[0]
    bq = bk = ATTN_BLOCK
    nk = S // bk
    lam_init = 0.8 - 0.6 * math.exp(-0.3 * layer)
    return pl.pallas_call(
        functools.partial(_attn_kernel, lam_init=lam_init),
        grid=(DA_HEADS, S // bq),
        in_specs=[
            pl.BlockSpec((4, DA_HEAD_DIM), lambda h, i: (0, 0)),
            pl.BlockSpec((DA_VAL_DIM, 1), lambda h, i: (0, 0)),
            pl.BlockSpec((1, LANES, bq), lambda h, i: (i, h, 0)),
            pl.BlockSpec((S, LANES), lambda h, i: (0, h)),
            pl.BlockSpec((nk, ATTN_VT_ROWS, bk), lambda h, i: (0, h, 0)),
        ],
        out_specs=pl.BlockSpec((bq, DA_VAL_DIM), lambda h, i: (i, h)),
        out_shape=jax.ShapeDtypeStruct((S, DA_WIDTH), BF16),
        scratch_shapes=[
            pltpu.VMEM((1, 2 * bq), F32),
            pltpu.VMEM((1, 2 * bq), F32),
            pltpu.VMEM((1, 2 * 2 * bq), F32),
            pltpu.VMEM((ATTN_VT_ROWS, 2 * bq), F32),
            pltpu.VMEM((2 * (2 * bq // ATTN_QUERY_GROUP), bk // 2, ATTN_QUERY_GROUP), F32),
            pltpu.VMEM((bk // 2, ATTN_QUERY_GROUP), BF16),
        ],
        compiler_params=_params(("parallel", "parallel")),
        name="diff_attn",
    )(lam_qk, attn_g.reshape(DA_VAL_DIM, 1), qt, k, vt)


def _hgrn_kernel(lbl_ref, g_ref, hg_ref, o_ref, st_scr, stb_scr, o_scr, *, layer):
    T = hg_ref.shape[0]
    c = HG_CHUNK
    W = HG_WIDTH

    @pl.when(pl.program_id(0) == 0)
    def _():
        st_scr[...] = jnp.zeros(st_scr.shape, F32)
        stb_scr[...] = jnp.zeros(stb_scr.shape, BF16)

    lg = lbl_ref[...]
    ex = jnp.exp(lg - jnp.max(lg, axis=0, keepdims=True))
    prob = ex / jnp.sum(ex, axis=0, keepdims=True)
    lb = jnp.zeros((1, W), F32)
    for li in range(1, layer + 1):
        lb = lb + prob[li:li + 1]

    same_head = _head_block_mask(W, HG_DIM)
    ones_bd = same_head.astype(BF16)
    rowi = lax.broadcasted_iota(jnp.int32, (c, W), 0)

    def body(s, carry):
        sl = pl.ds(pl.multiple_of(s * c, c), c)
        qf = hg_ref[sl, 0:W]
        z = hg_ref[sl, W:2 * W]
        iv = hg_ref[sl, 2 * W:3 * W]
        f = lb + (1.0 - lb) * jax.nn.sigmoid(z)
        logf = jnp.log(jnp.maximum(f, TINY))
        key = (1.0 - lb) * jax.nn.sigmoid(-z)
        val = _silu(iv)
        b = logf
        sft = 1
        while sft < c:
            b = b + jnp.where(rowi >= sft, pltpu.roll(b, sft, 0), 0.0)
            sft *= 2
        b_last = b[c - 1:c]
        kd = (key * jnp.exp(b_last - b)).astype(BF16)
        upd = lax.dot_general(val.astype(BF16), kd, (((0,), (0,)), ((), ())),
                              preferred_element_type=F32)
        es = [(qf * key).astype(BF16)]
        vrs = [val]
        for d in range(1, c):
            kr = pltpu.roll(key, d, 0)
            br = pltpu.roll(b, d, 0)
            e = jnp.where(rowi >= d, qf * kr * jnp.exp(b - br), 0.0)
            es.append(e.astype(BF16))
            vrs.append(pltpu.roll(val, d, 0))
        estack = jnp.concatenate(es, axis=0)
        r = jnp.dot(estack, ones_bd, preferred_element_type=F32)
        o = r[0:c] * vrs[0]
        for d in range(1, c):
            o = o + r[d * c:(d + 1) * c] * vrs[d]
        qd = (qf * jnp.exp(b)).astype(BF16)
        o = o + lax.dot_general(qd, stb_scr[...], (((1,), (1,)), ((), ())),
                                preferred_element_type=F32)
        o_scr[sl, :] = o
        decay = jnp.exp(b_last)
        for hd in range(HG_HEADS):
            rws = slice(hd * HG_DIM, (hd + 1) * HG_DIM)
            lns = slice(hd * HG_DIM // LANES * LANES, (hd * HG_DIM // LANES + 1) * LANES)
            new = decay[:, lns] * st_scr[rws, lns] + jnp.where(same_head[rws, lns], upd[rws, lns], 0.0)
            st_scr[rws, lns] = new
            stb_scr[rws, lns] = new.astype(BF16)
        return carry

    lax.fori_loop(0, T // c, body, 0, unroll=HG_UNROLL)

    o = o_scr[...]
    ms = jnp.dot(o * o, same_head.astype(F32), preferred_element_type=F32,
                 precision=HIGHEST) * (1.0 / HG_DIM)
    gate = hg_ref[:, 3 * W:4 * W]
    o_ref[...] = (o * lax.rsqrt(ms + EPS) * g_ref[...] * _silu(gate)).astype(BF16)


def _hgrn2(hg, lb_logits, norm_g, layer):
    S = hg.shape[0]
    T = HG_BLOCK
    g_t = jnp.tile(norm_g.reshape(1, HG_DIM), (1, HG_HEADS))
    return pl.pallas_call(
        functools.partial(_hgrn_kernel, layer=layer),
        grid=(S // T,),
        in_specs=[
            pl.BlockSpec((DEPTH, HG_WIDTH), lambda i: (0, 0)),
            pl.BlockSpec((1, HG_WIDTH), lambda i: (0, 0)),
            pl.BlockSpec((T, 4 * HG_WIDTH), lambda i: (i, 0)),
        ],
        out_specs=pl.BlockSpec((T, HG_WIDTH), lambda i: (i, 0)),
        out_shape=jax.ShapeDtypeStruct((S, HG_WIDTH), BF16),
        scratch_shapes=[
            pltpu.VMEM((HG_WIDTH, HG_WIDTH), F32),
            pltpu.VMEM((HG_WIDTH, HG_WIDTH), BF16),
            pltpu.VMEM((T, HG_WIDTH), F32),
        ],
        compiler_params=_params(("arbitrary",)),
        name="hgrn2",
    )(lb_logits, g_t, hg)


def _pool_kernel(u_ref, w_ref, sc_ref, o_ref, halo_scr):
    T = u_ref.shape[0]
    H = POOL_HALO
    blk = pl.program_id(0)

    @pl.when(blk == 0)
    def _():
        halo_scr[...] = jnp.zeros(halo_scr.shape, F32)

    u = u_ref[...]
    ext = jnp.concatenate([halo_scr[...], u], axis=0)
    halo_scr[...] = u[T - H:, :]
    sums = []
    s = ext
    w = 1
    while w < POOL_WINDOWS[-1]:
        s = s + pltpu.roll(s, w, 0)
        w *= 2
        sums.append(s[H:, :])
    t1 = (blk * T + 1 + lax.broadcasted_iota(jnp.int32, (T, POOL_WIDTH), 0)).astype(F32)
    lane = lax.broadcasted_iota(jnp.int32, (T, POOL_WIDTH), 1)
    pooled = None
    for gi, win in enumerate(POOL_WINDOWS):
        mean = sums[gi] / jnp.minimum(t1, float(win))
        pooled = mean if pooled is None else jnp.where(lane >= gi * POOL_GROUP, mean, pooled)
    pooled = pooled - u
    y = jnp.dot(pooled.astype(BF16), w_ref[...], preferred_element_type=F32)
    o_ref[...] = (y * sc_ref[...]).astype(BF16)


def _multiscale_pool(pu, pool_w, pool_scale):
    S = pu.shape[0]
    T = ROW_TILE
    w_bd = jax.scipy.linalg.block_diag(*[pool_w[g] for g in range(len(POOL_WINDOWS))]).astype(BF16)
    return pl.pallas_call(
        _pool_kernel,
        grid=(S // T,),
        in_specs=[
            pl.BlockSpec((T, POOL_WIDTH), lambda i: (i, 0)),
            pl.BlockSpec((POOL_WIDTH, POOL_WIDTH), lambda i: (0, 0)),
            pl.BlockSpec((1, POOL_WIDTH), lambda i: (0, 0)),
        ],
        out_specs=pl.BlockSpec((T, POOL_WIDTH), lambda i: (i, 0)),
        out_shape=jax.ShapeDtypeStruct((S, POOL_WIDTH), BF16),
        scratch_shapes=[pltpu.VMEM((POOL_HALO, POOL_WIDTH), F32)],
        compiler_params=_params(("arbitrary",)),
        name="ms_pool",
    )(pu, w_bd, pool_scale.reshape(1, POOL_WIDTH))


def _outproj_kernel(a_ref, r_ref, p_ref, x_ref, w_ref, g1_ref, lg_ref, lb_ref, o_ref):
    m = jnp.dot(a_ref[...], w_ref[0:DA_WIDTH, :], preferred_element_type=F32)
    m += jnp.dot(r_ref[...], w_ref[DA_WIDTH:DA_WIDTH + HG_WIDTH, :], preferred_element_type=F32)
    m += jnp.dot(p_ref[...], w_ref[DA_WIDTH + HG_WIDTH:, :], preferred_element_type=F32)
    y = ALPHA * x_ref[...] + (1.0 + g1_ref[...]) * m
    o_ref[...] = _layer_norm(y, lg_ref[...], lb_ref[...])


def _out_projection(a, r, p, x, w_out_bf, g1, ln_g, ln_b):
    S = x.shape[0]
    tm = ROW_TILE
    row = lambda i: (i, 0)
    fixed = lambda i: (0, 0)
    vec = pl.BlockSpec((1, D_MODEL), fixed)
    return pl.pallas_call(
        _outproj_kernel,
        grid=(S // tm,),
        in_specs=[
            pl.BlockSpec((tm, DA_WIDTH), row),
            pl.BlockSpec((tm, HG_WIDTH), row),
            pl.BlockSpec((tm, POOL_WIDTH), row),
            pl.BlockSpec((tm, D_MODEL), row),
            pl.BlockSpec((D_MIX, D_MODEL), fixed),
            vec, vec, vec,
        ],
        out_specs=pl.BlockSpec((tm, D_MODEL), row),
        out_shape=jax.ShapeDtypeStruct((S, D_MODEL), F32),
        compiler_params=_params(("parallel",)),
        name="out_proj_ln",
    )(a, r, p, x, w_out_bf, g1, ln_g.reshape(1, D_MODEL), ln_b.reshape(1, D_MODEL))


def _ffn_kernel(x_ref, sc_ref, sh_ref, g2_ref, w1_ref, w3_ref, w2_ref, lg_ref, lb_ref, o_ref):
    x = x_ref[...]
    h = (x * (1.0 + sc_ref[...]) + sh_ref[...]).astype(BF16)
    acc = jnp.zeros(x.shape, F32)
    step = 4 * MXU_DIM
    for lo in range(0, FFN_DIM, step):
        hi = min(lo + step, FFN_DIM)
        a = jnp.dot(h, w1_ref[:, lo:hi], preferred_element_type=F32)
        b = jnp.dot(h, w3_ref[:, lo:hi], preferred_element_type=F32)
        acc += jnp.dot((_silu(a) * b).astype(BF16), w2_ref[lo:hi, :], preferred_element_type=F32)
    y = ALPHA * x + (1.0 + g2_ref[...]) * acc
    o_ref[...] = _layer_norm(y, lg_ref[...], lb_ref[...])


def _dense_ffn(x, sc, sh, g2, w1, w3, w2, ln_g, ln_b):
    S = x.shape[0]
    tm = ROW_TILE
    row = lambda i: (i, 0)
    fixed = lambda i: (0, 0)
    vec = pl.BlockSpec((1, D_MODEL), fixed)
    once = pl.Buffered(1)
    return pl.pallas_call(
        _ffn_kernel,
        grid=(S // tm,),
        in_specs=[
            pl.BlockSpec((tm, D_MODEL), row),
            vec, vec, vec,
            pl.BlockSpec((D_MODEL, FFN_DIM), fixed, pipeline_mode=once),
            pl.BlockSpec((D_MODEL, FFN_DIM), fixed, pipeline_mode=once),
            pl.BlockSpec((FFN_DIM, D_MODEL), fixed, pipeline_mode=once),
            vec, vec,
        ],
        out_specs=pl.BlockSpec((tm, D_MODEL), row),
        out_shape=jax.ShapeDtypeStruct((S, D_MODEL), F32),
        compiler_params=_params(("parallel",)),
        name="dense_ffn_ln",
    )(x, sc, sh, g2, w1, w3, w2, ln_g.reshape(1, D_MODEL), ln_b.reshape(1, D_MODEL))


def _router_kernel(x_ref, sc_ref, sh_ref, rw_ref, h_ref, rank_ref, comb_ref, cum_ref):
    T = x_ref.shape[0]
    tc = MOE_CHUNK
    E = N_EXPERTS
    h = x_ref[...] * (1.0 + sc_ref[...]) + sh_ref[...]
    h_ref[...] = h.astype(BF16)
    logits = lax.dot_general(rw_ref[...], h, (((1,), (1,)), ((), ())),
                             preferred_element_type=F32, precision=HIGHEST)
    eidx = lax.broadcasted_iota(jnp.int32, (E, T), 0)
    v1 = jnp.max(logits, axis=0, keepdims=True)
    i1 = jnp.min(jnp.where(logits == v1, eidx, E), axis=0, keepdims=True)
    m1 = eidx == i1
    rest = jnp.where(m1, -jnp.inf, logits)
    v2 = jnp.max(rest, axis=0, keepdims=True)
    i2 = jnp.min(jnp.where(rest == v2, eidx, E), axis=0, keepdims=True)
    m2 = eidx == i2
    e2 = jnp.exp(v2 - v1)
    g1 = 1.0 / (1.0 + e2)
    g2 = e2 / (1.0 + e2)
    comb = jnp.where(m1, g1, 0.0) + jnp.where(m2, g2, 0.0)
    routed = jnp.logical_or(m1, m2)
    ind = routed.astype(BF16)
    r_i = lax.broadcasted_iota(jnp.int32, (tc, tc), 0)
    c_i = lax.broadcasted_iota(jnp.int32, (tc, tc), 1)
    strict = (r_i < c_i).astype(BF16)
    lane = lax.broadcasted_iota(jnp.int32, (E, LANES), 1)
    running = jnp.zeros((E, 1), F32)
    cum = jnp.zeros((E, LANES), F32)
    for ci in range(T // tc):
        sl = slice(ci * tc, (ci + 1) * tc)
        ind_c = ind[:, sl]
        rank_c = jnp.dot(ind_c, strict, preferred_element_type=F32) + running
        rank_ref[:, ci] = jnp.where(routed[:, sl], rank_c, -1.0).reshape(E, 1, tc)
        comb_ref[:, ci] = comb[:, sl].reshape(E, 1, tc)
        cum = jnp.where(lane == ci, running, cum)
        running = running + jnp.sum(ind_c.astype(F32), axis=1, keepdims=True)
    cum = jnp.where(lane == T // tc, running, cum)
    cum_ref[0] = cum.astype(jnp.int32)


def _router(x, sc, sh, router_w):
    S = x.shape[0]
    T = MOE_BLOCK
    E = N_EXPERTS
    fixed = lambda i: (0, 0)
    vec = pl.BlockSpec((1, D_MODEL), fixed)
    return pl.pallas_call(
        _router_kernel,
        grid=(S // T,),
        in_specs=[
            pl.BlockSpec((T, D_MODEL), lambda i: (i, 0)),
            vec, vec,
            pl.BlockSpec((E, D_MODEL), fixed),
        ],
        out_specs=[
            pl.BlockSpec((T, D_MODEL), lambda i: (i, 0)),
            pl.BlockSpec((E, T // MOE_CHUNK, 1, MOE_CHUNK), lambda i: (0, i, 0, 0)),
            pl.BlockSpec((E, T // MOE_CHUNK, 1, MOE_CHUNK), lambda i: (0, i, 0, 0)),
            pl.BlockSpec((1, E, LANES), lambda i: (i, 0, 0)),
        ],
        out_shape=[
            jax.ShapeDtypeStruct((S, D_MODEL), BF16),
            jax.ShapeDtypeStruct((E, S // MOE_CHUNK, 1, MOE_CHUNK), F32),
            jax.ShapeDtypeStruct((E, S // MOE_CHUNK, 1, MOE_CHUNK), F32),
            jax.ShapeDtypeStruct((S // T, E, LANES), jnp.int32),
        ],
        compiler_params=_params(("parallel",)),
        name="moe_router",
    )(x, sc, sh, router_w.T)


def _moe_kernel(cum_ref, h_ref, rank_ref, comb_ref, w1_ref, w3_ref, w2_ref, y_ref, xe_scr, gate_scr,
                out_scr):
    T = h_ref.shape[0]
    tr, tc, win = MOE_TILE, MOE_CHUNK, MOE_WINDOW
    nchunk = T // tc
    b = pl.program_id(0)
    e = pl.program_id(1)
    f = pl.program_id(2)

    @pl.when(jnp.logical_and(e == 0, f == 0))
    def _():
        y_ref[...] = jnp.zeros(y_ref.shape, F32)

    base = (b * N_EXPERTS + e) * LANES
    count = cum_ref[base + nchunk]
    ntiles = (count + (tr - 1)) // tr

    def token_row(ref, first, n):
        blk = ref[0, pl.ds(first, n)]
        return jnp.concatenate([blk[c] for c in range(n)], axis=1)

    def tile(r, carry):
        lo = r * tr
        rows = pl.ds(pl.multiple_of(lo, 16), tr)

        def hits(first, n):
            want = (lo + lax.broadcasted_iota(jnp.int32, (tr, n * tc), 0)).astype(F32)
            return token_row(rank_ref, first, n) == want

        def token_span(first, n):
            start = first * tc
            if not isinstance(start, int):
                start = pl.multiple_of(start, tc)
            return pl.ds(start, n * tc)

        c_lo = jnp.int32(0)
        c_hi = jnp.int32(-1)
        for c in range(nchunk):
            c_lo += (cum_ref[base + c + 1] <= lo).astype(jnp.int32)
            c_hi += (cum_ref[base + c] < lo + tr).astype(jnp.int32)
        fits = c_hi - c_lo < win
        window = (jnp.minimum(c_lo, nchunk - win), win)
        block = (0, nchunk)

        def gather(first, n):
            hit = hits(first, n)
            xe_scr[rows, :] = jnp.dot(hit.astype(BF16), h_ref[token_span(first, n), :],
                                      preferred_element_type=F32).astype(BF16)
            gate_scr[rows, :] = jnp.sum(jnp.where(hit, token_row(comb_ref, first, n), 0.0),
                                        axis=1, keepdims=True)

        @pl.when(jnp.logical_and(f == 0, fits))
        def _():
            gather(*window)

        @pl.when(jnp.logical_and(f == 0, jnp.logical_not(fits)))
        def _():
            gather(*block)

        xe = xe_scr[rows, :]
        a = jnp.dot(xe, w1_ref[0], preferred_element_type=F32)
        g = jnp.dot(xe, w3_ref[0], preferred_element_type=F32)
        mid = (_silu(a) * g).astype(BF16)
        part = jnp.dot(mid, w2_ref[0], preferred_element_type=F32)

        @pl.when(f == 0)
        def _():
            out_scr[rows, :] = part

        if MOE_FSPLIT > 2:
            @pl.when(jnp.logical_and(f > 0, f < MOE_FSPLIT - 1))
            def _():
                out_scr[rows, :] += part

        @pl.when(f == MOE_FSPLIT - 1)
        def _():
            out_bf = ((out_scr[rows, :] + part) * gate_scr[rows, :]).astype(BF16)

            def scatter(first, n):
                y_ref[token_span(first, n), :] += lax.dot_general(
                    hits(first, n).astype(BF16), out_bf, (((0,), (0,)), ((), ())),
                    preferred_element_type=F32)

            @pl.when(fits)
            def _():
                scatter(*window)

            @pl.when(jnp.logical_not(fits))
            def _():
                scatter(*block)

        return carry

    lax.fori_loop(0, ntiles, tile, 0)


def _moe_ffn(h_bf, rank, comb, cum, w1, w3, w2):
    S = h_bf.shape[0]
    T = MOE_BLOCK
    fw = EXPERT_DIM // MOE_FSPLIT
    grid_spec = pltpu.PrefetchScalarGridSpec(
        num_scalar_prefetch=1,
        grid=(S // T, N_EXPERTS, MOE_FSPLIT),
        in_specs=[
            pl.BlockSpec((T, D_MODEL), lambda b, e, f, c: (b, 0), pipeline_mode=pl.Buffered(1)),
            pl.BlockSpec((1, T // MOE_CHUNK, 1, MOE_CHUNK), lambda b, e, f, c: (e, b, 0, 0)),
            pl.BlockSpec((1, T // MOE_CHUNK, 1, MOE_CHUNK), lambda b, e, f, c: (e, b, 0, 0)),
            pl.BlockSpec((1, D_MODEL, fw), lambda b, e, f, c: (e, 0, f)),
            pl.BlockSpec((1, D_MODEL, fw), lambda b, e, f, c: (e, 0, f)),
            pl.BlockSpec((1, fw, D_MODEL), lambda b, e, f, c: (e, f, 0)),
        ],
        out_specs=pl.BlockSpec((T, D_MODEL), lambda b, e, f, c: (b, 0)),
        scratch_shapes=[
            pltpu.VMEM((T + MOE_TILE, D_MODEL), BF16),
            pltpu.VMEM((T + MOE_TILE, 1), F32),
            pltpu.VMEM((T + MOE_TILE, D_MODEL), F32),
        ],
    )
    return pl.pallas_call(
        _moe_kernel,
        grid_spec=grid_spec,
        out_shape=jax.ShapeDtypeStruct((S, D_MODEL), F32),
        compiler_params=_params(("arbitrary", "arbitrary", "arbitrary")),
        name="moe_experts",
    )(cum.reshape(-1), h_bf, rank, comb, w1, w3, w2)


def _ln_kernel(x_ref, y_ref, g2_ref, lg_ref, lb_ref, o_ref):
    y = ALPHA * x_ref[...] + (1.0 + g2_ref[...]) * y_ref[...]
    o_ref[...] = _layer_norm(y, lg_ref[...], lb_ref[...])


def _residual_ln(x, y, g2, ln_g, ln_b):
    S = x.shape[0]
    tm = ROW_TILE
    row = lambda i: (i, 0)
    vec = pl.BlockSpec((1, D_MODEL), lambda i: (0, 0))
    return pl.pallas_call(
        _ln_kernel,
        grid=(S // tm,),
        in_specs=[pl.BlockSpec((tm, D_MODEL), row), pl.BlockSpec((tm, D_MODEL), row), vec, vec, vec],
        out_specs=pl.BlockSpec((tm, D_MODEL), row),
        out_shape=jax.ShapeDtypeStruct((S, D_MODEL), F32),
        compiler_params=_params(("parallel",)),
        name="residual_ln",
    )(x, y, g2, ln_g.reshape(1, D_MODEL), ln_b.reshape(1, D_MODEL))


def _rope_tables(S):
    d = DA_HEAD_DIM
    inv_freq = 1.0 / (ROPE_THETA ** (jnp.arange(0, d, 2, dtype=F32) / d))
    ang = jnp.arange(S, dtype=F32)[:, None] * inv_freq[None, :]
    cos = jnp.cos(ang)
    sin = jnp.sin(ang)
    reps = LANES // d
    cos_t = jnp.tile(jnp.concatenate([cos, cos], axis=1), (1, reps))
    sin_t = jnp.tile(jnp.concatenate([-sin, sin], axis=1), (1, reps))
    return cos_t, sin_t


def kernel(x, c, w_ada, b_ada, w_in, lam_qk, attn_norm_g, hg_lb_logits, hg_norm_g, pool_w, pool_scale, w_out, ln1_g, ln1_b, ln2_g, ln2_b, ffn_w1, ffn_w3, ffn_w2, router_w, exp_w1, exp_w3, exp_w2):
    B, S, D = x.shape
    assert B == 1 and D == D_MODEL and S % MOE_BLOCK == 0
    xs = x.reshape(S, D)
    mod = _modulation(c, w_ada, b_ada)
    cos_t, sin_t = _rope_tables(S)
    for l in range(DEPTH):
        sh1, sc1, g1, sh2, sc2, g2 = (mod[l, j] for j in range(6))
        qt, k, vt, hg, pu = _in_projection(xs, sc1, sh1, w_in[l].astype(BF16), cos_t, sin_t)
        a = _diff_attention(qt, k, vt, lam_qk[l], attn_norm_g[l], l)
        r = _hgrn2(hg, hg_lb_logits, hg_norm_g[l], l)
        p = _multiscale_pool(pu, pool_w[l], pool_scale[l])
        xs = _out_projection(a, r, p, xs, w_out[l].astype(BF16), g1, ln1_g[l], ln1_b[l])
        if l % 2 == 0:
            i = l // 2
            xs = _dense_ffn(xs, sc2, sh2, g2, ffn_w1[i].astype(BF16), ffn_w3[i].astype(BF16),
                            ffn_w2[i].astype(BF16), ln2_g[l], ln2_b[l])
        else:
            i = l // 2
            h_bf, rank, comb, cum = _router(xs, sc2, sh2, router_w[i])
            y = _moe_ffn(h_bf, rank, comb, cum, exp_w1[i].astype(BF16), exp_w3[i].astype(BF16),
                         exp_w2[i].astype(BF16))
            xs = _residual_ln(xs, y, g2, ln2_g[l], ln2_b[l])
    return xs.reshape(B, S, D)
```

```python
import functools
import math

import jax
import jax.numpy as jnp
from jax import lax
from jax.experimental import pallas as pl
from jax.experimental.pallas import tpu as pltpu

F32 = jnp.float32
BF16 = jnp.bfloat16
HIGHEST = lax.Precision.HIGHEST

D_MODEL = 1024
DEPTH = 2
DA_HEADS = 4
DA_WIDTH = D_MODEL // 2
DA_HEAD_DIM = DA_WIDTH // (2 * DA_HEADS)
DA_VAL_DIM = 2 * DA_HEAD_DIM
HG_HEADS = 4
HG_WIDTH = D_MODEL // 4
HG_DIM = HG_WIDTH // HG_HEADS
POOL_WINDOWS = (2, 4, 8, 16)
POOL_WIDTH = D_MODEL // 4
POOL_GROUP = POOL_WIDTH // len(POOL_WINDOWS)
D_MIX = DA_WIDTH + HG_WIDTH + POOL_WIDTH
D_IN = 3 * DA_WIDTH + 4 * HG_WIDTH + POOL_WIDTH
ROPE_THETA = 10000.0
MASK_VALUE = -1e30
TINY = 1e-30
FFN_DIM = 11 * D_MODEL // 4
N_EXPERTS = 8
EXPERT_DIM = 7 * D_MODEL // 2
ALPHA = (2 * DEPTH) ** 0.25
EPS = 1e-5
LOG2_E = math.log2(math.e)

LANES = 128
MXU_DIM = 256
VMEM_LIMIT = 60 * 1024 * 1024

ROW_TILE = 512
FFN_CAST_TILE = 256
ATTN_BLOCK = 512
ATTN_QUERY_GROUP = 256
ATTN_SUM_ROWS = 16
ATTN_VT_ROWS = DA_VAL_DIM + ATTN_SUM_ROWS
HG_BLOCK = 512
HG_CHUNK = 16
HG_UNROLL = 4
POOL_HALO = 16
MOE_BLOCK = 2048
MOE_TILE = 192
MOE_CHUNK = 256
MOE_WINDOW = 5
MOE_FSPLIT = 2


def _params(sem, vmem=VMEM_LIMIT):
    return pltpu.CompilerParams(dimension_semantics=sem, vmem_limit_bytes=vmem)


def _layer_norm(y, g, b):
    mu = jnp.mean(y, axis=-1, keepdims=True)
    d = y - mu
    var = jnp.mean(d * d, axis=-1, keepdims=True)
    return d * lax.rsqrt(var + EPS) * g + b


def _silu(x):
    return x * jax.nn.sigmoid(x)


def _head_block_mask(n, group):
    r = lax.broadcasted_iota(jnp.int32, (n, n), 0) // group
    c = lax.broadcasted_iota(jnp.int32, (n, n), 1) // group
    return r == c


def _mod_kernel(c_ref, w_ref, b_ref, o_ref):
    cond = _silu(c_ref[...])
    cond8 = jnp.broadcast_to(cond, (8, D_MODEL))
    r = jnp.dot(cond8, w_ref[0], preferred_element_type=F32, precision=HIGHEST)
    o_ref[0] = r[0:1] + b_ref[0]


def _modulation(c, w_ada, b_ada):
    n = 6 * DEPTH
    b3 = b_ada.reshape(n, 1, D_MODEL)
    out = pl.pallas_call(
        _mod_kernel,
        grid=(DEPTH, 6),
        in_specs=[
            pl.BlockSpec((1, D_MODEL), lambda l, j: (0, 0)),
            pl.BlockSpec((1, D_MODEL, D_MODEL), lambda l, j: (l, 0, j)),
            pl.BlockSpec((1, 1, D_MODEL), lambda l, j: (l * 6 + j, 0, 0)),
        ],
        out_specs=pl.BlockSpec((1, 1, D_MODEL), lambda l, j: (l * 6 + j, 0, 0)),
        out_shape=jax.ShapeDtypeStruct((n, 1, D_MODEL), F32),
        compiler_params=_params(("parallel", "parallel")),
        name="adaln_mod",
    )(c, w_ada, b3)
    return out.reshape(DEPTH, 6, 1, D_MODEL)


def _inproj_kernel(x_ref, sc_ref, sh_ref, w_ref, wvt_ref, cos_ref, sin_ref,
                   qt_ref, k_ref, vt_ref, hg_ref, pu_ref):
    h = (x_ref[...] * (1.0 + sc_ref[...]) + sh_ref[...]).astype(BF16)
    qk = jnp.dot(h, w_ref[:, : 2 * DA_WIDTH], preferred_element_type=F32)
    reps = 2 * DA_WIDTH // LANES
    cos = jnp.tile(cos_ref[...], (1, reps))
    sin = jnp.tile(sin_ref[...], (1, reps))
    lane = lax.broadcasted_iota(jnp.int32, qk.shape, 1)
    half = DA_HEAD_DIM // 2
    first_half = (lane & half) == 0
    width = 2 * DA_WIDTH
    partner = jnp.where(first_half, pltpu.roll(qk, width - half, 1), pltpu.roll(qk, half, 1))
    rot = qk * cos + partner * sin
    qt_ref[0] = (rot[:, :DA_WIDTH] * (DA_HEAD_DIM ** -0.5 * LOG2_E)).T.astype(BF16)
    k_ref[...] = rot[:, DA_WIDTH:].astype(BF16)
    o = 2 * DA_WIDTH
    vt = lax.dot_general(wvt_ref[...], h, (((1,), (1,)), ((), ())),
                         preferred_element_type=F32).astype(BF16)
    ones = jnp.ones((ATTN_SUM_ROWS, vt.shape[1]), BF16)
    pieces = []
    for hd in range(DA_HEADS):
        pieces += [vt[hd * DA_VAL_DIM:(hd + 1) * DA_VAL_DIM], ones]
    vt_ref[0] = jnp.concatenate(pieces, axis=0)
    o += DA_WIDTH
    hg_ref[...] = jnp.dot(h, w_ref[:, o:o + 4 * HG_WIDTH], preferred_element_type=F32)
    o += 4 * HG_WIDTH
    pu_ref[...] = jnp.dot(h, w_ref[:, o:o + POOL_WIDTH], preferred_element_type=F32)


def _in_projection(x, sc, sh, w_in_bf, cos_t, sin_t):
    S = x.shape[0]
    tm = ATTN_BLOCK
    row = lambda i: (i, 0)
    fixed = lambda i: (0, 0)
    return pl.pallas_call(
        _inproj_kernel,
        grid=(S // tm,),
        in_specs=[
            pl.BlockSpec((tm, D_MODEL), row),
            pl.BlockSpec((1, D_MODEL), fixed),
            pl.BlockSpec((1, D_MODEL), fixed),
            pl.BlockSpec((D_MODEL, D_IN), fixed),
            pl.BlockSpec((DA_WIDTH, D_MODEL), fixed),
            pl.BlockSpec((tm, LANES), row),
            pl.BlockSpec((tm, LANES), row),
        ],
        out_specs=[
            pl.BlockSpec((1, DA_WIDTH, tm), lambda i: (i, 0, 0)),
            pl.BlockSpec((tm, DA_WIDTH), row),
            pl.BlockSpec((1, DA_HEADS * ATTN_VT_ROWS, tm), lambda i: (i, 0, 0)),
            pl.BlockSpec((tm, 4 * HG_WIDTH), row),
            pl.BlockSpec((tm, POOL_WIDTH), row),
        ],
        out_shape=[
            jax.ShapeDtypeStruct((S // tm, DA_WIDTH, tm), BF16),
            jax.ShapeDtypeStruct((S, DA_WIDTH), BF16),
            jax.ShapeDtypeStruct((S // tm, DA_HEADS * ATTN_VT_ROWS, tm), BF16),
            jax.ShapeDtypeStruct((S, 4 * HG_WIDTH), F32),
            jax.ShapeDtypeStruct((S, POOL_WIDTH), F32),
        ],
        compiler_params=_params(("parallel",)),
        name="in_proj",
    )(x, sc, sh, w_in_bf, w_in_bf[:, 2 * DA_WIDTH:3 * DA_WIDTH].T, cos_t, sin_t)


def _attn_kernel(lam_ref, g_ref, qt_ref, k_ref, vt_ref, o_ref, m_scr, a_scr, c_scr, acc_scr, s_scr,
                 p_scr, *, lam_init):
    bq = qt_ref.shape[2]
    bk = vt_ref.shape[2]
    cg = ATTN_QUERY_GROUP
    i = pl.program_id(1)
    qt = qt_ref[0]
    rowq = lax.broadcasted_iota(jnp.int32, qt.shape, 0)
    zero = jnp.zeros_like(qt)
    maps = (jnp.where(rowq < DA_HEAD_DIM, qt, zero), jnp.where(rowq >= DA_HEAD_DIM, qt, zero))
    groups = [(a * bq + c0, c0, maps[a][:, c0:c0 + cg]) for a in range(2) for c0 in range(0, bq, cg)]
    m_scr[...] = jnp.full(m_scr.shape, -jnp.inf, F32)
    acc_scr[...] = jnp.zeros(acc_scr.shape, F32)
    hk = bk // 2
    key_i = lax.broadcasted_iota(jnp.int32, (hk, cg), 0)
    qry_i = lax.broadcasted_iota(jnp.int32, (hk, cg), 1)
    ng = len(groups)
    last = ng - 1

    def cols_of(g):
        return slice(groups[g][0], groups[g][0] + cg)

    def ccols_of(sset, g):
        return slice(sset * 2 * bq + groups[g][0], sset * 2 * bq + groups[g][0] + cg)

    def mask_mode(key_offset, g):
        c0 = groups[g][1]
        if key_offset + hk - 1 <= c0:
            return None
        if key_offset > c0 + cg - 1:
            return "skip"
        return key_offset

    def score_stage(hb, sset, g, key_offset):
        kb = k_ref[pl.ds(pl.multiple_of(hb * hk, hk), hk), :]
        s = jnp.dot(kb, groups[g][2], preferred_element_type=F32)
        if key_offset is not None:
            s = jnp.where(key_i + key_offset <= qry_i + groups[g][1], s, MASK_VALUE)
        s_scr[sset * ng + g] = s
        c_scr[:, ccols_of(sset, g)] = jnp.max(s, axis=0, keepdims=True)

    def max_stage(sset, g):
        cols = cols_of(g)
        m_prev = m_scr[:, cols]
        m_new = jnp.maximum(m_prev, c_scr[:, ccols_of(sset, g)])
        a_scr[:, cols] = jnp.exp2(m_prev - m_new)
        m_scr[:, cols] = m_new

    def exp_stage(sset, g, vt_half, carried, defer):
        cols = cols_of(g)
        p = jnp.exp2(s_scr[sset * ng + g] - m_scr[:, cols])
        alpha = a_scr[:, cols]
        pb = p.astype(BF16)
        acc = acc_scr[:, cols]
        if carried is not None:
            acc = acc + carried
        if defer:
            acc_scr[:, cols] = alpha * acc
            p_scr[...] = pb
        else:
            acc_scr[:, cols] = alpha * acc + jnp.dot(vt_half, pb, preferred_element_type=F32)

    def iteration(t, look, diagonal):
        vt_a = vt_ref[t, :, 0:hk]
        vt_b = vt_ref[t, :, hk:bk]
        carried = jnp.dot(vt_ref[jnp.maximum(t - 1, 0), :, hk:bk], p_scr[...],
                          preferred_element_type=F32)
        if diagonal:
            b_items = [(g, mask_mode(hk, g)) for g in range(ng) if mask_mode(hk, g) != "skip"]
        else:
            b_items = [(g, None) for g in range(ng)]
        score_stage(2 * t + 1, 1, *b_items[0])
        for g in range(ng):
            if g + 1 < len(b_items):
                score_stage(2 * t + 1, 1, *b_items[g + 1])
            if g + 1 < ng:
                max_stage(0, g + 1)
            else:
                max_stage(1, b_items[0][0])
            exp_stage(0, g, vt_a, carried if g == last else None, False)
        if not diagonal:
            score_stage(2 * t + 2, 0, 0, mask_mode(0, 0) if look == "diag" else None)
        for n, (g, _) in enumerate(b_items):
            if not diagonal and g + 1 < ng:
                score_stage(2 * t + 2, 0, g + 1, mask_mode(0, g + 1) if look == "diag" else None)
            if n + 1 < len(b_items):
                max_stage(1, b_items[n + 1][0])
            elif not diagonal:
                max_stage(0, 0)
            exp_stage(1, g, vt_b, None, not diagonal and g == last)

    p_scr[...] = jnp.zeros(p_scr.shape, BF16)
    for g in range(ng):
        score_stage(0, 0, g, (0 - i) * bq)
    max_stage(0, 0)

    def body(t, carry):
        iteration(t, None, False)
        return carry

    lax.fori_loop(0, i - 1, body, 0)

    @pl.when(i > 0)
    def _():
        iteration(i - 1, "diag", False)

    iteration(i, None, True)

    lq = lam_ref[...]
    lam = (jnp.exp(jnp.sum(lq[0:1] * lq[1:2], axis=1, keepdims=True))
           - jnp.exp(jnp.sum(lq[2:3] * lq[3:4], axis=1, keepdims=True)) + lam_init)
    o = acc_scr[0:DA_VAL_DIM, :] / acc_scr[DA_VAL_DIM:DA_VAL_DIM + 1, :]
    o = o[:, :bq] - lam * o[:, bq:]
    ms = jnp.mean(o * o, axis=0, keepdims=True)
    o = o * lax.rsqrt(ms + EPS) * g_ref[...] * (1.0 - lam_init)
    o_ref[...] = o.T.astype(BF16)


def _diff_attention(qt, k, vt, lam_qk, attn_g, layer):
    S = k.shape[0]
    bq = bk = ATTN_BLOCK
    nk = S // bk
    lam_init = 0.8 - 0.6 * math.exp(-0.3 * layer)
    return pl.pallas_call(
        functools.partial(_attn_kernel, lam_init=lam_init),
        grid=(DA_HEADS, S // bq),
        in_specs=[
            pl.BlockSpec((4, DA_HEAD_DIM), lambda h, i: (0, 0)),
            pl.BlockSpec((DA_VAL_DIM, 1), lambda h, i: (0, 0)),
            pl.BlockSpec((1, LANES, bq), lambda h, i: (i, h, 0)),
            pl.BlockSpec((S, LANES), lambda h, i: (0, h)),
            pl.BlockSpec((nk, ATTN_VT_ROWS, bk), lambda h, i: (0, h, 0)),
        ],
        out_specs=pl.BlockSpec((bq, DA_VAL_DIM), lambda h, i: (i, h)),
        out_shape=jax.ShapeDtypeStruct((S, DA_WIDTH), BF16),
        scratch_shapes=[
            pltpu.VMEM((1, 2 * bq), F32),
            pltpu.VMEM((1, 2 * bq), F32),
            pltpu.VMEM((1, 2 * 2 * bq), F32),
            pltpu.VMEM((ATTN_VT_ROWS, 2 * bq), F32),
            pltpu.VMEM((2 * (2 * bq // ATTN_QUERY_GROUP), bk // 2, ATTN_QUERY_GROUP), F32),
            pltpu.VMEM((bk // 2, ATTN_QUERY_GROUP), BF16),
        ],
        compiler_params=_params(("parallel", "parallel")),
        name="diff_attn",
    )(lam_qk, attn_g.reshape(DA_VAL_DIM, 1), qt, k, vt)


def _hgrn_kernel(lbl_ref, g_ref, hg_ref, o_ref, st_scr, stb_scr, o_scr, *, layer):
    T = hg_ref.shape[0]
    c = HG_CHUNK
    W = HG_WIDTH

    @pl.when(pl.program_id(0) == 0)
    def _():
        st_scr[...] = jnp.zeros(st_scr.shape, F32)
        stb_scr[...] = jnp.zeros(stb_scr.shape, BF16)

    lg = lbl_ref[...]
    ex = jnp.exp(lg - jnp.max(lg, axis=0, keepdims=True))
    prob = ex / jnp.sum(ex, axis=0, keepdims=True)
    lb = jnp.zeros((1, W), F32)
    for li in range(1, layer + 1):
        lb = lb + prob[li:li + 1]

    same_head = _head_block_mask(W, HG_DIM)
    ones_bd = same_head.astype(BF16)
    rowi = lax.broadcasted_iota(jnp.int32, (c, W), 0)

    def body(s, carry):
        sl = pl.ds(pl.multiple_of(s * c, c), c)
        qf = hg_ref[sl, 0:W]
        z = hg_ref[sl, W:2 * W]
        iv = hg_ref[sl, 2 * W:3 * W]
        f = lb + (1.0 - lb) * jax.nn.sigmoid(z)
        logf = jnp.log(jnp.maximum(f, TINY))
        key = (1.0 - lb) * jax.nn.sigmoid(-z)
        val = _silu(iv)
        b = logf
        sft = 1
        while sft < c:
            b = b + jnp.where(rowi >= sft, pltpu.roll(b, sft, 0), 0.0)
            sft *= 2
        b_last = b[c - 1:c]
        kd = (key * jnp.exp(b_last - b)).astype(BF16)
        upd = lax.dot_general(val.astype(BF16), kd, (((0,), (0,)), ((), ())),
                              preferred_element_type=F32)
        es = [(qf * key).astype(BF16)]
        vrs = [val]
        for d in range(1, c):
            kr = pltpu.roll(key, d, 0)
            br = pltpu.roll(b, d, 0)
            e = jnp.where(rowi >= d, qf * kr * jnp.exp(b - br), 0.0)
            es.append(e.astype(BF16))
            vrs.append(pltpu.roll(val, d, 0))
        estack = jnp.concatenate(es, axis=0)
        r = jnp.dot(estack, ones_bd, preferred_element_type=F32)
        o = r[0:c] * vrs[0]
        for d in range(1, c):
            o = o + r[d * c:(d + 1) * c] * vrs[d]
        qd = (qf * jnp.exp(b)).astype(BF16)
        o = o + lax.dot_general(qd, stb_scr[...], (((1,), (1,)), ((), ())),
                                preferred_element_type=F32)
        o_scr[sl, :] = o
        decay = jnp.exp(b_last)
        for hd in range(HG_HEADS):
            rws = slice(hd * HG_DIM, (hd + 1) * HG_DIM)
            lns = slice(hd * HG_DIM // LANES * LANES, (hd * HG_DIM // LANES + 1) * LANES)
            new = decay[:, lns] * st_scr[rws, lns] + jnp.where(same_head[rws, lns], upd[rws, lns], 0.0)
            st_scr[rws, lns] = new
            stb_scr[rws, lns] = new.astype(BF16)
        return carry

    lax.fori_loop(0, T // c, body, 0, unroll=HG_UNROLL)

    o = o_scr[...]
    ms = jnp.dot(o * o, same_head.astype(F32), preferred_element_type=F32,
                 precision=HIGHEST) * (1.0 / HG_DIM)
    gate = hg_ref[:, 3 * W:4 * W]
    o_ref[...] = (o * lax.rsqrt(ms + EPS) * g_ref[...] * _silu(gate)).astype(BF16)


def _hgrn2(hg, lb_logits, norm_g, layer):
    S = hg.shape[0]
    T = HG_BLOCK
    g_t = jnp.tile(norm_g.reshape(1, HG_DIM), (1, HG_HEADS))
    return pl.pallas_call(
        functools.partial(_hgrn_kernel, layer=layer),
        grid=(S // T,),
        in_specs=[
            pl.BlockSpec((DEPTH, HG_WIDTH), lambda i: (0, 0)),
            pl.BlockSpec((1, HG_WIDTH), lambda i: (0, 0)),
            pl.BlockSpec((T, 4 * HG_WIDTH), lambda i: (i, 0)),
        ],
        out_specs=pl.BlockSpec((T, HG_WIDTH), lambda i: (i, 0)),
        out_shape=jax.ShapeDtypeStruct((S, HG_WIDTH), BF16),
        scratch_shapes=[
            pltpu.VMEM((HG_WIDTH, HG_WIDTH), F32),
            pltpu.VMEM((HG_WIDTH, HG_WIDTH), BF16),
            pltpu.VMEM((T, HG_WIDTH), F32),
        ],
        compiler_params=_params(("arbitrary",)),
        name="hgrn2",
    )(lb_logits, g_t, hg)


def _pool_kernel(u_ref, w_ref, sc_ref, o_ref, halo_scr):
    T = u_ref.shape[0]
    H = POOL_HALO
    blk = pl.program_id(0)

    @pl.when(blk == 0)
    def _():
        halo_scr[...] = jnp.zeros(halo_scr.shape, F32)

    u = u_ref[...]
    ext = jnp.concatenate([halo_scr[...], u], axis=0)
    halo_scr[...] = u[T - H:, :]
    sums = []
    s = ext
    w = 1
    while w < POOL_WINDOWS[-1]:
        s = s + pltpu.roll(s, w, 0)
        w *= 2
        sums.append(s[H:, :])
    t1 = (blk * T + 1 + lax.broadcasted_iota(jnp.int32, (T, POOL_WIDTH), 0)).astype(F32)
    lane = lax.broadcasted_iota(jnp.int32, (T, POOL_WIDTH), 1)
    pooled = None
    for gi, win in enumerate(POOL_WINDOWS):
        mean = sums[gi] / jnp.minimum(t1, float(win))
        pooled = mean if pooled is None else jnp.where(lane >= gi * POOL_GROUP, mean, pooled)
    pooled = pooled - u
    y = jnp.dot(pooled.astype(BF16), w_ref[...], preferred_element_type=F32)
    o_ref[...] = (y * sc_ref[...]).astype(BF16)


def _multiscale_pool(pu, pool_w, pool_scale):
    S = pu.shape[0]
    T = ROW_TILE
    w_bd = jax.scipy.linalg.block_diag(*[pool_w[g] for g in range(len(POOL_WINDOWS))]).astype(BF16)
    return pl.pallas_call(
        _pool_kernel,
        grid=(S // T,),
        in_specs=[
            pl.BlockSpec((T, POOL_WIDTH), lambda i: (i, 0)),
            pl.BlockSpec((POOL_WIDTH, POOL_WIDTH), lambda i: (0, 0)),
            pl.BlockSpec((1, POOL_WIDTH), lambda i: (0, 0)),
        ],
        out_specs=pl.BlockSpec((T, POOL_WIDTH), lambda i: (i, 0)),
        out_shape=jax.ShapeDtypeStruct((S, POOL_WIDTH), BF16),
        scratch_shapes=[pltpu.VMEM((POOL_HALO, POOL_WIDTH), F32)],
        compiler_params=_params(("arbitrary",)),
        name="ms_pool",
    )(pu, w_bd, pool_scale.reshape(1, POOL_WIDTH))


def _outproj_kernel(a_ref, r_ref, p_ref, x_ref, w_ref, g1_ref, lg_ref, lb_ref, o_ref):
    m = jnp.dot(a_ref[...], w_ref[0:DA_WIDTH, :], preferred_element_type=F32)
    m += jnp.dot(r_ref[...], w_ref[DA_WIDTH:DA_WIDTH + HG_WIDTH, :], preferred_element_type=F32)
    m += jnp.dot(p_ref[...], w_ref[DA_WIDTH + HG_WIDTH:, :], preferred_element_type=F32)
    y = ALPHA * x_ref[...] + (1.0 + g1_ref[...]) * m
    o_ref[...] = _layer_norm(y, lg_ref[...], lb_ref[...])


def _out_projection(a, r, p, x, w_out_bf, g1, ln_g, ln_b):
    S = x.shape[0]
    tm = ROW_TILE
    row = lambda i: (i, 0)
    fixed = lambda i: (0, 0)
    vec = pl.BlockSpec((1, D_MODEL), fixed)
    return pl.pallas_call(
        _outproj_kernel,
        grid=(S // tm,),
        in_specs=[
            pl.BlockSpec((tm, DA_WIDTH), row),
            pl.BlockSpec((tm, HG_WIDTH), row),
            pl.BlockSpec((tm, POOL_WIDTH), row),
            pl.BlockSpec((tm, D_MODEL), row),
            pl.BlockSpec((D_MIX, D_MODEL), fixed),
            vec, vec, vec,
        ],
        out_specs=pl.BlockSpec((tm, D_MODEL), row),
        out_shape=jax.ShapeDtypeStruct((S, D_MODEL), F32),
        compiler_params=_params(("parallel",)),
        name="out_proj_ln",
    )(a, r, p, x, w_out_bf, g1, ln_g.reshape(1, D_MODEL), ln_b.reshape(1, D_MODEL))


def _ffn_kernel(x_ref, sc_ref, sh_ref, g2_ref, w1_ref, w3_ref, w2_ref, lg_ref, lb_ref, *rest):
    n_cast = (len(rest) - 1) // 2
    o_ref = rest[n_cast]
    for src, dst in zip(rest[:n_cast], rest[n_cast + 1:]):
        dst[...] = src[...].astype(BF16)
    x = x_ref[...]
    h = (x * (1.0 + sc_ref[...]) + sh_ref[...]).astype(BF16)
    acc = jnp.zeros(x.shape, F32)
    step = 4 * MXU_DIM
    for lo in range(0, FFN_DIM, step):
        hi = min(lo + step, FFN_DIM)
        a = jnp.dot(h, w1_ref[:, lo:hi], preferred_element_type=F32)
        b = jnp.dot(h, w3_ref[:, lo:hi], preferred_element_type=F32)
        acc += jnp.dot((_silu(a) * b).astype(BF16), w2_ref[lo:hi, :], preferred_element_type=F32)
    y = ALPHA * x + (1.0 + g2_ref[...]) * acc
    o_ref[...] = _layer_norm(y, lg_ref[...], lb_ref[...])


def _dense_ffn(x, sc, sh, g2, w1, w3, w2, ln_g, ln_b, to_cast=()):
    S = x.shape[0]
    tm = FFN_CAST_TILE if to_cast else ROW_TILE
    steps = S // tm
    row = lambda i: (i, 0)
    fixed = lambda i: (0, 0)
    vec = pl.BlockSpec((1, D_MODEL), fixed)
    once = pl.Buffered(1)
    slabs = [a.reshape(-1, a.shape[-1]) for a in to_cast]
    slab_specs = [pl.BlockSpec((a.shape[0] // steps, a.shape[1]), row) for a in slabs]
    for a in slabs:
        assert a.shape[0] % (steps * 16) == 0
    outs = pl.pallas_call(
        _ffn_kernel,
        grid=(steps,),
        in_specs=[
            pl.BlockSpec((tm, D_MODEL), row),
            vec, vec, vec,
            pl.BlockSpec((D_MODEL, FFN_DIM), fixed, pipeline_mode=once),
            pl.BlockSpec((D_MODEL, FFN_DIM), fixed, pipeline_mode=once),
            pl.BlockSpec((FFN_DIM, D_MODEL), fixed, pipeline_mode=once),
            vec, vec,
        ] + slab_specs,
        out_specs=[pl.BlockSpec((tm, D_MODEL), row)] + slab_specs,
        out_shape=[jax.ShapeDtypeStruct((S, D_MODEL), F32)]
        + [jax.ShapeDtypeStruct(a.shape, BF16) for a in slabs],
        compiler_params=_params(("parallel",)),
        name="dense_ffn_ln",
    )(x, sc, sh, g2, w1, w3, w2, ln_g.reshape(1, D_MODEL), ln_b.reshape(1, D_MODEL), *slabs)
    return outs[0], [o.reshape(a.shape) for o, a in zip(outs[1:], to_cast)]


def _router_kernel(x_ref, sc_ref, sh_ref, rw_ref, h_ref, rank_ref, comb_ref, cum_ref):
    T = x_ref.shape[0]
    tc = MOE_CHUNK
    E = N_EXPERTS
    h = x_ref[...] * (1.0 + sc_ref[...]) + sh_ref[...]
    h_ref[...] = h.astype(BF16)
    logits = lax.dot_general(rw_ref[...], h, (((1,), (1,)), ((), ())),
                             preferred_element_type=F32, precision=HIGHEST)
    eidx = lax.broadcasted_iota(jnp.int32, (E, T), 0)
    v1 = jnp.max(logits, axis=0, keepdims=True)
    i1 = jnp.min(jnp.where(logits == v1, eidx, E), axis=0, keepdims=True)
    m1 = eidx == i1
    rest = jnp.where(m1, -jnp.inf, logits)
    v2 = jnp.max(rest, axis=0, keepdims=True)
    i2 = jnp.min(jnp.where(rest == v2, eidx, E), axis=0, keepdims=True)
    m2 = eidx == i2
    e2 = jnp.exp(v2 - v1)
    g1 = 1.0 / (1.0 + e2)
    g2 = e2 / (1.0 + e2)
    comb = jnp.where(m1, g1, 0.0) + jnp.where(m2, g2, 0.0)
    routed = jnp.logical_or(m1, m2)
    ind = routed.astype(BF16)
    r_i = lax.broadcasted_iota(jnp.int32, (tc, tc), 0)
    c_i = lax.broadcasted_iota(jnp.int32, (tc, tc), 1)
    strict = (r_i < c_i).astype(BF16)
    lane = lax.broadcasted_iota(jnp.int32, (E, LANES), 1)
    running = jnp.zeros((E, 1), F32)
    cum = jnp.zeros((E, LANES), F32)
    for ci in range(T // tc):
        sl = slice(ci * tc, (ci + 1) * tc)
        ind_c = ind[:, sl]
        rank_c = jnp.dot(ind_c, strict, preferred_element_type=F32) + running
        rank_ref[:, ci] = jnp.where(routed[:, sl], rank_c, -1.0).reshape(E, 1, tc)
        comb_ref[:, ci] = comb[:, sl].reshape(E, 1, tc)
        cum = jnp.where(lane == ci, running, cum)
        running = running + jnp.sum(ind_c.astype(F32), axis=1, keepdims=True)
    cum = jnp.where(lane == T // tc, running, cum)
    cum_ref[0] = cum.astype(jnp.int32)


def _router(x, sc, sh, router_w):
    S = x.shape[0]
    T = MOE_BLOCK
    E = N_EXPERTS
    fixed = lambda i: (0, 0)
    vec = pl.BlockSpec((1, D_MODEL), fixed)
    return pl.pallas_call(
        _router_kernel,
        grid=(S // T,),
        in_specs=[
            pl.BlockSpec((T, D_MODEL), lambda i: (i, 0)),
            vec, vec,
            pl.BlockSpec((E, D_MODEL), fixed),
        ],
        out_specs=[
            pl.BlockSpec((T, D_MODEL), lambda i: (i, 0)),
            pl.BlockSpec((E, T // MOE_CHUNK, 1, MOE_CHUNK), lambda i: (0, i, 0, 0)),
            pl.BlockSpec((E, T // MOE_CHUNK, 1, MOE_CHUNK), lambda i: (0, i, 0, 0)),
            pl.BlockSpec((1, E, LANES), lambda i: (i, 0, 0)),
        ],
        out_shape=[
            jax.ShapeDtypeStruct((S, D_MODEL), BF16),
            jax.ShapeDtypeStruct((E, S // MOE_CHUNK, 1, MOE_CHUNK), F32),
            jax.ShapeDtypeStruct((E, S // MOE_CHUNK, 1, MOE_CHUNK), F32),
            jax.ShapeDtypeStruct((S // T, E, LANES), jnp.int32),
        ],
        compiler_params=_params(("parallel",)),
        name="moe_router",
    )(x, sc, sh, router_w.T)


def _moe_kernel(cum_ref, h_ref, rank_ref, comb_ref, w1_ref, w3_ref, w2_ref, y_ref, xe_scr, gate_scr,
                out_scr):
    T = h_ref.shape[0]
    tr, tc, win = MOE_TILE, MOE_CHUNK, MOE_WINDOW
    nchunk = T // tc
    b = pl.program_id(0)
    e = pl.program_id(1)
    f = pl.program_id(2)

    @pl.when(jnp.logical_and(e == 0, f == 0))
    def _():
        y_ref[...] = jnp.zeros(y_ref.shape, F32)

    base = (b * N_EXPERTS + e) * LANES
    count = cum_ref[base + nchunk]
    ntiles = (count + (tr - 1)) // tr

    def token_row(ref, first, n):
        blk = ref[0, pl.ds(first, n)]
        return jnp.concatenate([blk[c] for c in range(n)], axis=1)

    def tile(r, carry):
        lo = r * tr
        rows = pl.ds(pl.multiple_of(lo, 16), tr)

        def hits(first, n):
            want = (lo + lax.broadcasted_iota(jnp.int32, (tr, n * tc), 0)).astype(F32)
            return token_row(rank_ref, first, n) == want

        def token_span(first, n):
            start = first * tc
            if not isinstance(start, int):
                start = pl.multiple_of(start, tc)
            return pl.ds(start, n * tc)

        c_lo = jnp.int32(0)
        c_hi = jnp.int32(-1)
        for c in range(nchunk):
            c_lo += (cum_ref[base + c + 1] <= lo).astype(jnp.int32)
            c_hi += (cum_ref[base + c] < lo + tr).astype(jnp.int32)
        fits = c_hi - c_lo < win
        window = (jnp.minimum(c_lo, nchunk - win), win)
        block = (0, nchunk)

        def gather(first, n):
            hit = hits(first, n)
            xe_scr[rows, :] = jnp.dot(hit.astype(BF16), h_ref[token_span(first, n), :],
                                      preferred_element_type=F32).astype(BF16)
            gate_scr[rows, :] = jnp.sum(jnp.where(hit, token_row(comb_ref, first, n), 0.0),
                                        axis=1, keepdims=True)

        @pl.when(jnp.logical_and(f == 0, fits))
        def _():
            gather(*window)

        @pl.when(jnp.logical_and(f == 0, jnp.logical_not(fits)))
        def _():
            gather(*block)

        xe = xe_scr[rows, :]
        a = jnp.dot(xe, w1_ref[0], preferred_element_type=F32)
        g = jnp.dot(xe, w3_ref[0], preferred_element_type=F32)
        mid = (_silu(a) * g).astype(BF16)
        part = jnp.dot(mid, w2_ref[0], preferred_element_type=F32)

        @pl.when(f == 0)
        def _():
            out_scr[rows, :] = part

        if MOE_FSPLIT > 2:
            @pl.when(jnp.logical_and(f > 0, f < MOE_FSPLIT - 1))
            def _():
                out_scr[rows, :] += part

        @pl.when(f == MOE_FSPLIT - 1)
        def _():
            out_bf = ((out_scr[rows, :] + part) * gate_scr[rows, :]).astype(BF16)

            def scatter(first, n):
                y_ref[token_span(first, n), :] += lax.dot_general(
                    hits(first, n).astype(BF16), out_bf, (((0,), (0,)), ((), ())),
                    preferred_element_type=F32)

            @pl.when(fits)
            def _():
                scatter(*window)

            @pl.when(jnp.logical_not(fits))
            def _():
                scatter(*block)

        return carry

    lax.fori_loop(0, ntiles, tile, 0)


def _moe_ffn(h_bf, rank, comb, cum, w1, w3, w2):
    S = h_bf.shape[0]
    T = MOE_BLOCK
    fw = EXPERT_DIM // MOE_FSPLIT
    grid_spec = pltpu.PrefetchScalarGridSpec(
        num_scalar_prefetch=1,
        grid=(S // T, N_EXPERTS, MOE_FSPLIT),
        in_specs=[
            pl.BlockSpec((T, D_MODEL), lambda b, e, f, c: (b, 0), pipeline_mode=pl.Buffered(1)),
            pl.BlockSpec((1, T // MOE_CHUNK, 1, MOE_CHUNK), lambda b, e, f, c: (e, b, 0, 0)),
            pl.BlockSpec((1, T // MOE_CHUNK, 1, MOE_CHUNK), lambda b, e, f, c: (e, b, 0, 0)),
            pl.BlockSpec((1, D_MODEL, fw), lambda b, e, f, c: (e, 0, f)),
            pl.BlockSpec((1, D_MODEL, fw), lambda b, e, f, c: (e, 0, f)),
            pl.BlockSpec((1, fw, D_MODEL), lambda b, e, f, c: (e, f, 0)),
        ],
        out_specs=pl.BlockSpec((T, D_MODEL), lambda b, e, f, c: (b, 0)),
        scratch_shapes=[
            pltpu.VMEM((T + MOE_TILE, D_MODEL), BF16),
            pltpu.VMEM((T + MOE_TILE, 1), F32),
            pltpu.VMEM((T + MOE_TILE, D_MODEL), F32),
        ],
    )
    return pl.pallas_call(
        _moe_kernel,
        grid_spec=grid_spec,
        out_shape=jax.ShapeDtypeStruct((S, D_MODEL), F32),
        compiler_params=_params(("arbitrary", "arbitrary", "arbitrary")),
        name="moe_experts",
    )(cum.reshape(-1), h_bf, rank, comb, w1, w3, w2)


def _ln_kernel(x_ref, y_ref, g2_ref, lg_ref, lb_ref, o_ref):
    y = ALPHA * x_ref[...] + (1.0 + g2_ref[...]) * y_ref[...]
    o_ref[...] = _layer_norm(y, lg_ref[...], lb_ref[...])


def _residual_ln(x, y, g2, ln_g, ln_b):
    S = x.shape[0]
    tm = ROW_TILE
    row = lambda i: (i, 0)
    vec = pl.BlockSpec((1, D_MODEL), lambda i: (0, 0))
    return pl.pallas_call(
        _ln_kernel,
        grid=(S // tm,),
        in_specs=[pl.BlockSpec((tm, D_MODEL), row), pl.BlockSpec((tm, D_MODEL), row), vec, vec, vec],
        out_specs=pl.BlockSpec((tm, D_MODEL), row),
        out_shape=jax.ShapeDtypeStruct((S, D_MODEL), F32),
        compiler_params=_params(("parallel",)),
        name="residual_ln",
    )(x, y, g2, ln_g.reshape(1, D_MODEL), ln_b.reshape(1, D_MODEL))


def _rope_tables(S):
    d = DA_HEAD_DIM
    inv_freq = 1.0 / (ROPE_THETA ** (jnp.arange(0, d, 2, dtype=F32) / d))
    ang = jnp.arange(S, dtype=F32)[:, None] * inv_freq[None, :]
    cos = jnp.cos(ang)
    sin = jnp.sin(ang)
    reps = LANES // d
    cos_t = jnp.tile(jnp.concatenate([cos, cos], axis=1), (1, reps))
    sin_t = jnp.tile(jnp.concatenate([-sin, sin], axis=1), (1, reps))
    return cos_t, sin_t


def kernel(x, c, w_ada, b_ada, w_in, lam_qk, attn_norm_g, hg_lb_logits, hg_norm_g, pool_w, pool_scale, w_out, ln1_g, ln1_b, ln2_g, ln2_b, ffn_w1, ffn_w3, ffn_w2, router_w, exp_w1, exp_w3, exp_w2):
    B, S, D = x.shape
    assert B == 1 and D == D_MODEL and S % MOE_BLOCK == 0
    xs = x.reshape(S, D)
    mod = _modulation(c, w_ada, b_ada)
    cos_t, sin_t = _rope_tables(S)
    experts_bf = {}
    for l in range(DEPTH):
        sh1, sc1, g1, sh2, sc2, g2 = (mod[l, j] for j in range(6))
        qt, k, vt, hg, pu = _in_projection(xs, sc1, sh1, w_in[l].astype(BF16), cos_t, sin_t)
        a = _diff_attention(qt, k, vt, lam_qk[l], attn_norm_g[l], l)
        r = _hgrn2(hg, hg_lb_logits, hg_norm_g[l], l)
        p = _multiscale_pool(pu, pool_w[l], pool_scale[l])
        xs = _out_projection(a, r, p, xs, w_out[l].astype(BF16), g1, ln1_g[l], ln1_b[l])
        if l % 2 == 0:
            i = l // 2
            ride = (exp_w1[i], exp_w3[i], exp_w2[i]) if l + 1 < DEPTH else ()
            xs, cast = _dense_ffn(xs, sc2, sh2, g2, ffn_w1[i].astype(BF16), ffn_w3[i].astype(BF16),
                                  ffn_w2[i].astype(BF16), ln2_g[l], ln2_b[l], ride)
            if ride:
                experts_bf[i] = cast
        else:
            i = l // 2
            h_bf, rank, comb, cum = _router(xs, sc2, sh2, router_w[i])
            w1, w3, w2 = experts_bf.get(i) or [w[i].astype(BF16) for w in (exp_w1, exp_w3, exp_w2)]
            y = _moe_ffn(h_bf, rank, comb, cum, w1, w3, w2)
            xs = _residual_ln(xs, y, g2, ln2_g[l], ln2_b[l])
    return xs.reshape(B, S, D)
```

```python
import functools
import math

import jax
import jax.numpy as jnp
from jax import lax
from jax.experimental import pallas as pl
from jax.experimental.pallas import tpu as pltpu

F32 = jnp.float32
BF16 = jnp.bfloat16
HIGHEST = lax.Precision.HIGHEST

D_MODEL = 1024
DEPTH = 2
DA_HEADS = 4
DA_WIDTH = D_MODEL // 2
DA_HEAD_DIM = DA_WIDTH // (2 * DA_HEADS)
DA_VAL_DIM = 2 * DA_HEAD_DIM
HG_HEADS = 4
HG_WIDTH = D_MODEL // 4
HG_DIM = HG_WIDTH // HG_HEADS
POOL_WINDOWS = (2, 4, 8, 16)
POOL_WIDTH = D_MODEL // 4
POOL_GROUP = POOL_WIDTH // len(POOL_WINDOWS)
D_MIX = DA_WIDTH + HG_WIDTH + POOL_WIDTH
D_IN = 3 * DA_WIDTH + 4 * HG_WIDTH + POOL_WIDTH
ROPE_THETA = 10000.0
MASK_VALUE = -1e30
TINY = 1e-30
FFN_DIM = 11 * D_MODEL // 4
N_EXPERTS = 8
EXPERT_DIM = 7 * D_MODEL // 2
ALPHA = (2 * DEPTH) ** 0.25
EPS = 1e-5
LOG2_E = math.log2(math.e)

LANES = 128
MXU_DIM = 256
VMEM_LIMIT = 60 * 1024 * 1024

ROW_TILE = 512
FFN_CAST_TILE = 256
ATTN_BLOCK = 512
ATTN_QUERY_GROUP = 256
ATTN_SUM_ROWS = 16
ATTN_VT_ROWS = DA_VAL_DIM + ATTN_SUM_ROWS
HG_BLOCK = 512
HG_CHUNK = 16
HG_UNROLL = 4
POOL_HALO = 16
MOE_BLOCK = 2048
MOE_TILE = 192
MOE_CHUNK = 256
MOE_WINDOW = 5
MOE_FSPLIT = 2


def _params(sem, vmem=VMEM_LIMIT):
    return pltpu.CompilerParams(dimension_semantics=sem, vmem_limit_bytes=vmem)


def _layer_norm(y, g, b):
    mu = jnp.mean(y, axis=-1, keepdims=True)
    d = y - mu
    var = jnp.mean(d * d, axis=-1, keepdims=True)
    return d * lax.rsqrt(var + EPS) * g + b


def _silu(x):
    return x * jax.nn.sigmoid(x)


def _head_block_mask(n, group):
    r = lax.broadcasted_iota(jnp.int32, (n, n), 0) // group
    c = lax.broadcasted_iota(jnp.int32, (n, n), 1) // group
    return r == c


def _mod_kernel(c_ref, w_ref, b_ref, o_ref):
    cond = _silu(c_ref[...])
    cond8 = jnp.broadcast_to(cond, (8, D_MODEL))
    r = jnp.dot(cond8, w_ref[0], preferred_element_type=F32, precision=HIGHEST)
    o_ref[0] = r[0:1] + b_ref[0]


def _modulation(c, w_ada, b_ada):
    n = 6 * DEPTH
    b3 = b_ada.reshape(n, 1, D_MODEL)
    out = pl.pallas_call(
        _mod_kernel,
        grid=(DEPTH, 6),
        in_specs=[
            pl.BlockSpec((1, D_MODEL), lambda l, j: (0, 0)),
            pl.BlockSpec((1, D_MODEL, D_MODEL), lambda l, j: (l, 0, j)),
            pl.BlockSpec((1, 1, D_MODEL), lambda l, j: (l * 6 + j, 0, 0)),
        ],
        out_specs=pl.BlockSpec((1, 1, D_MODEL), lambda l, j: (l * 6 + j, 0, 0)),
        out_shape=jax.ShapeDtypeStruct((n, 1, D_MODEL), F32),
        compiler_params=_params(("parallel", "parallel")),
        name="adaln_mod",
    )(c, w_ada, b3)
    return out.reshape(DEPTH, 6, 1, D_MODEL)


def _inproj_kernel(x_ref, sc_ref, sh_ref, wf_ref, cos_ref, sin_ref,
                   qt_ref, k_ref, vt_ref, hg_ref, pu_ref, w_ref, wvt_ref):
    @pl.when(pl.program_id(0) == 0)
    def _():
        w_ref[...] = wf_ref[...].astype(BF16)
        wvt_ref[...] = wf_ref[:, 2 * DA_WIDTH:3 * DA_WIDTH].T.astype(BF16)

    h = (x_ref[...] * (1.0 + sc_ref[...]) + sh_ref[...]).astype(BF16)
    qk = jnp.dot(h, w_ref[:, : 2 * DA_WIDTH], preferred_element_type=F32)
    reps = 2 * DA_WIDTH // LANES
    cos = jnp.tile(cos_ref[...], (1, reps))
    sin = jnp.tile(sin_ref[...], (1, reps))
    lane = lax.broadcasted_iota(jnp.int32, qk.shape, 1)
    half = DA_HEAD_DIM // 2
    first_half = (lane & half) == 0
    width = 2 * DA_WIDTH
    partner = jnp.where(first_half, pltpu.roll(qk, width - half, 1), pltpu.roll(qk, half, 1))
    rot = qk * cos + partner * sin
    qt_ref[0] = (rot[:, :DA_WIDTH] * (DA_HEAD_DIM ** -0.5 * LOG2_E)).T.astype(BF16)
    k_ref[...] = rot[:, DA_WIDTH:].astype(BF16)
    o = 2 * DA_WIDTH
    vt = lax.dot_general(wvt_ref[...], h, (((1,), (1,)), ((), ())),
                         preferred_element_type=F32).astype(BF16)
    ones = jnp.ones((ATTN_SUM_ROWS, vt.shape[1]), BF16)
    pieces = []
    for hd in range(DA_HEADS):
        pieces += [vt[hd * DA_VAL_DIM:(hd + 1) * DA_VAL_DIM], ones]
    vt_ref[0] = jnp.concatenate(pieces, axis=0)
    o += DA_WIDTH
    hg_ref[...] = jnp.dot(h, w_ref[:, o:o + 4 * HG_WIDTH], preferred_element_type=F32)
    o += 4 * HG_WIDTH
    pu_ref[...] = jnp.dot(h, w_ref[:, o:o + POOL_WIDTH], preferred_element_type=F32)


def _in_projection(x, sc, sh, w_in_f32, cos_t, sin_t):
    S = x.shape[0]
    tm = ATTN_BLOCK
    row = lambda i: (i, 0)
    fixed = lambda i: (0, 0)
    return pl.pallas_call(
        _inproj_kernel,
        grid=(S // tm,),
        in_specs=[
            pl.BlockSpec((tm, D_MODEL), row),
            pl.BlockSpec((1, D_MODEL), fixed),
            pl.BlockSpec((1, D_MODEL), fixed),
            pl.BlockSpec((D_MODEL, D_IN), fixed, pipeline_mode=pl.Buffered(1)),
            pl.BlockSpec((tm, LANES), row),
            pl.BlockSpec((tm, LANES), row),
        ],
        out_specs=[
            pl.BlockSpec((1, DA_WIDTH, tm), lambda i: (i, 0, 0)),
            pl.BlockSpec((tm, DA_WIDTH), row),
            pl.BlockSpec((1, DA_HEADS * ATTN_VT_ROWS, tm), lambda i: (i, 0, 0)),
            pl.BlockSpec((tm, 4 * HG_WIDTH), row),
            pl.BlockSpec((tm, POOL_WIDTH), row),
        ],
        out_shape=[
            jax.ShapeDtypeStruct((S // tm, DA_WIDTH, tm), BF16),
            jax.ShapeDtypeStruct((S, DA_WIDTH), BF16),
            jax.ShapeDtypeStruct((S // tm, DA_HEADS * ATTN_VT_ROWS, tm), BF16),
            jax.ShapeDtypeStruct((S, 4 * HG_WIDTH), F32),
            jax.ShapeDtypeStruct((S, POOL_WIDTH), F32),
        ],
        scratch_shapes=[
            pltpu.VMEM((D_MODEL, D_IN), BF16),
            pltpu.VMEM((DA_WIDTH, D_MODEL), BF16),
        ],
        compiler_params=_params(("arbitrary",)),
        name="in_proj",
    )(x, sc, sh, w_in_f32, cos_t, sin_t)


def _attn_kernel(lam_ref, g_ref, qt_ref, k_ref, vt_ref, o_ref, m_scr, a_scr, c_scr, acc_scr, s_scr,
                 p_scr, *, lam_init):
    bq = qt_ref.shape[2]
    bk = vt_ref.shape[2]
    cg = ATTN_QUERY_GROUP
    i = pl.program_id(1)
    qt = qt_ref[0]
    rowq = lax.broadcasted_iota(jnp.int32, qt.shape, 0)
    zero = jnp.zeros_like(qt)
    maps = (jnp.where(rowq < DA_HEAD_DIM, qt, zero), jnp.where(rowq >= DA_HEAD_DIM, qt, zero))
    groups = [(a * bq + c0, c0, maps[a][:, c0:c0 + cg]) for a in range(2) for c0 in range(0, bq, cg)]
    m_scr[...] = jnp.full(m_scr.shape, -jnp.inf, F32)
    acc_scr[...] = jnp.zeros(acc_scr.shape, F32)
    hk = bk // 2
    key_i = lax.broadcasted_iota(jnp.int32, (hk, cg), 0)
    qry_i = lax.broadcasted_iota(jnp.int32, (hk, cg), 1)
    ng = len(groups)
    last = ng - 1

    def cols_of(g):
        return slice(groups[g][0], groups[g][0] + cg)

    def ccols_of(sset, g):
        return slice(sset * 2 * bq + groups[g][0], sset * 2 * bq + groups[g][0] + cg)

    def mask_mode(key_offset, g):
        c0 = groups[g][1]
        if key_offset + hk - 1 <= c0:
            return None
        if key_offset > c0 + cg - 1:
            return "skip"
        return key_offset

    def score_stage(hb, sset, g, key_offset):
        kb = k_ref[pl.ds(pl.multiple_of(hb * hk, hk), hk), :]
        s = jnp.dot(kb, groups[g][2], preferred_element_type=F32)
        if key_offset is not None:
            s = jnp.where(key_i + key_offset <= qry_i + groups[g][1], s, MASK_VALUE)
        s_scr[sset * ng + g] = s
        c_scr[:, ccols_of(sset, g)] = jnp.max(s, axis=0, keepdims=True)

    def max_stage(sset, g):
        cols = cols_of(g)
        m_prev = m_scr[:, cols]
        m_new = jnp.maximum(m_prev, c_scr[:, ccols_of(sset, g)])
        a_scr[:, cols] = jnp.exp2(m_prev - m_new)
        m_scr[:, cols] = m_new

    def exp_stage(sset, g, vt_half, carried, defer):
        cols = cols_of(g)
        p = jnp.exp2(s_scr[sset * ng + g] - m_scr[:, cols])
        alpha = a_scr[:, cols]
        pb = p.astype(BF16)
        acc = acc_scr[:, cols]
        if carried is not None:
            acc = acc + carried
        if defer:
            acc_scr[:, cols] = alpha * acc
            p_scr[...] = pb
        else:
            acc_scr[:, cols] = alpha * acc + jnp.dot(vt_half, pb, preferred_element_type=F32)

    def iteration(t, look, diagonal):
        vt_a = vt_ref[t, :, 0:hk]
        vt_b = vt_ref[t, :, hk:bk]
        carried = jnp.dot(vt_ref[jnp.maximum(t - 1, 0), :, hk:bk], p_scr[...],
                          preferred_element_type=F32)
        if diagonal:
            b_items = [(g, mask_mode(hk, g)) for g in range(ng) if mask_mode(hk, g) != "skip"]
        else:
            b_items = [(g, None) for g in range(ng)]
        score_stage(2 * t + 1, 1, *b_items[0])
        for g in range(ng):
            if g + 1 < len(b_items):
                score_stage(2 * t + 1, 1, *b_items[g + 1])
            if g + 1 < ng:
                max_stage(0, g + 1)
            else:
                max_stage(1, b_items[0][0])
            exp_stage(0, g, vt_a, carried if g == last else None, False)
        if not diagonal:
            score_stage(2 * t + 2, 0, 0, mask_mode(0, 0) if look == "diag" else None)
        for n, (g, _) in enumerate(b_items):
            if not diagonal and g + 1 < ng:
                score_stage(2 * t + 2, 0, g + 1, mask_mode(0, g + 1) if look == "diag" else None)
            if n + 1 < len(b_items):
                max_stage(1, b_items[n + 1][0])
            elif not diagonal:
                max_stage(0, 0)
            exp_stage(1, g, vt_b, None, not diagonal and g == last)

    p_scr[...] = jnp.zeros(p_scr.shape, BF16)
    for g in range(ng):
        score_stage(0, 0, g, (0 - i) * bq)
    max_stage(0, 0)

    def body(t, carry):
        iteration(t, None, False)
        return carry

    lax.fori_loop(0, i - 1, body, 0)

    @pl.when(i > 0)
    def _():
        iteration(i - 1, "diag", False)

    iteration(i, None, True)

    lq = lam_ref[...]
    lam = (jnp.exp(jnp.sum(lq[0:1] * lq[1:2], axis=1, keepdims=True))
           - jnp.exp(jnp.sum(lq[2:3] * lq[3:4], axis=1, keepdims=True)) + lam_init)
    o = acc_scr[0:DA_VAL_DIM, :] / acc_scr[DA_VAL_DIM:DA_VAL_DIM + 1, :]
    o = o[:, :bq] - lam * o[:, bq:]
    ms = jnp.mean(o * o, axis=0, keepdims=True)
    o = o * lax.rsqrt(ms + EPS) * g_ref[...] * (1.0 - lam_init)
    o_ref[...] = o.T.astype(BF16)


def _diff_attention(qt, k, vt, lam_qk, attn_g, layer):
    S = k.shape[0]
    bq = bk = ATTN_BLOCK
    nk = S // bk
    lam_init = 0.8 - 0.6 * math.exp(-0.3 * layer)
    return pl.pallas_call(
        functools.partial(_attn_kernel, lam_init=lam_init),
        grid=(DA_HEADS, S // bq),
        in_specs=[
            pl.BlockSpec((4, DA_HEAD_DIM), lambda h, i: (0, 0)),
            pl.BlockSpec((DA_VAL_DIM, 1), lambda h, i: (0, 0)),
            pl.BlockSpec((1, LANES, bq), lambda h, i: (i, h, 0)),
            pl.BlockSpec((S, LANES), lambda h, i: (0, h)),
            pl.BlockSpec((nk, ATTN_VT_ROWS, bk), lambda h, i: (0, h, 0)),
        ],
        out_specs=pl.BlockSpec((bq, DA_VAL_DIM), lambda h, i: (i, h)),
        out_shape=jax.ShapeDtypeStruct((S, DA_WIDTH), BF16),
        scratch_shapes=[
            pltpu.VMEM((1, 2 * bq), F32),
            pltpu.VMEM((1, 2 * bq), F32),
            pltpu.VMEM((1, 2 * 2 * bq), F32),
            pltpu.VMEM((ATTN_VT_ROWS, 2 * bq), F32),
            pltpu.VMEM((2 * (2 * bq // ATTN_QUERY_GROUP), bk // 2, ATTN_QUERY_GROUP), F32),
            pltpu.VMEM((bk // 2, ATTN_QUERY_GROUP), BF16),
        ],
        compiler_params=_params(("parallel", "parallel")),
        name="diff_attn",
    )(lam_qk, attn_g.reshape(DA_VAL_DIM, 1), qt, k, vt)


def _hgrn_kernel(lbl_ref, g_ref, hg_ref, o_ref, st_scr, stb_scr, o_scr, *, layer):
    T = hg_ref.shape[0]
    c = HG_CHUNK
    W = HG_WIDTH

    @pl.when(pl.program_id(0) == 0)
    def _():
        st_scr[...] = jnp.zeros(st_scr.shape, F32)
        stb_scr[...] = jnp.zeros(stb_scr.shape, BF16)

    lg = lbl_ref[...]
    ex = jnp.exp(lg - jnp.max(lg, axis=0, keepdims=True))
    prob = ex / jnp.sum(ex, axis=0, keepdims=True)
    lb = jnp.zeros((1, W), F32)
    for li in range(1, layer + 1):
        lb = lb + prob[li:li + 1]

    same_head = _head_block_mask(W, HG_DIM)
    ones_bd = same_head.astype(BF16)
    rowi = lax.broadcasted_iota(jnp.int32, (c, W), 0)

    def body(s, carry):
        sl = pl.ds(pl.multiple_of(s * c, c), c)
        qf = hg_ref[sl, 0:W]
        z = hg_ref[sl, W:2 * W]
        iv = hg_ref[sl, 2 * W:3 * W]
        f = lb + (1.0 - lb) * jax.nn.sigmoid(z)
        logf = jnp.log(jnp.maximum(f, TINY))
        key = (1.0 - lb) * jax.nn.sigmoid(-z)
        val = _silu(iv)
        b = logf
        sft = 1
        while sft < c:
            b = b + jnp.where(rowi >= sft, pltpu.roll(b, sft, 0), 0.0)
            sft *= 2
        b_last = b[c - 1:c]
        kd = (key * jnp.exp(b_last - b)).astype(BF16)
        upd = lax.dot_general(val.astype(BF16), kd, (((0,), (0,)), ((), ())),
                              preferred_element_type=F32)
        es = [(qf * key).astype(BF16)]
        vrs = [val]
        for d in range(1, c):
            kr = pltpu.roll(key, d, 0)
            br = pltpu.roll(b, d, 0)
            e = jnp.where(rowi >= d, qf * kr * jnp.exp(b - br), 0.0)
            es.append(e.astype(BF16))
            vrs.append(pltpu.roll(val, d, 0))
        estack = jnp.concatenate(es, axis=0)
        r = jnp.dot(estack, ones_bd, preferred_element_type=F32)
        o = r[0:c] * vrs[0]
        for d in range(1, c):
            o = o + r[d * c:(d + 1) * c] * vrs[d]
        qd = (qf * jnp.exp(b)).astype(BF16)
        o = o + lax.dot_general(qd, stb_scr[...], (((1,), (1,)), ((), ())),
                                preferred_element_type=F32)
        o_scr[sl, :] = o
        decay = jnp.exp(b_last)
        for hd in range(HG_HEADS):
            rws = slice(hd * HG_DIM, (hd + 1) * HG_DIM)
            lns = slice(hd * HG_DIM // LANES * LANES, (hd * HG_DIM // LANES + 1) * LANES)
            new = decay[:, lns] * st_scr[rws, lns] + jnp.where(same_head[rws, lns], upd[rws, lns], 0.0)
            st_scr[rws, lns] = new
            stb_scr[rws, lns] = new.astype(BF16)
        return carry

    lax.fori_loop(0, T // c, body, 0, unroll=HG_UNROLL)

    o = o_scr[...]
    ms = jnp.dot(o * o, same_head.astype(F32), preferred_element_type=F32,
                 precision=HIGHEST) * (1.0 / HG_DIM)
    gate = hg_ref[:, 3 * W:4 * W]
    o_ref[...] = (o * lax.rsqrt(ms + EPS) * g_ref[...] * _silu(gate)).astype(BF16)


def _hgrn2(hg, lb_logits, norm_g, layer):
    S = hg.shape[0]
    T = HG_BLOCK
    g_t = jnp.tile(norm_g.reshape(1, HG_DIM), (1, HG_HEADS))
    return pl.pallas_call(
        functools.partial(_hgrn_kernel, layer=layer),
        grid=(S // T,),
        in_specs=[
            pl.BlockSpec((DEPTH, HG_WIDTH), lambda i: (0, 0)),
            pl.BlockSpec((1, HG_WIDTH), lambda i: (0, 0)),
            pl.BlockSpec((T, 4 * HG_WIDTH), lambda i: (i, 0)),
        ],
        out_specs=pl.BlockSpec((T, HG_WIDTH), lambda i: (i, 0)),
        out_shape=jax.ShapeDtypeStruct((S, HG_WIDTH), BF16),
        scratch_shapes=[
            pltpu.VMEM((HG_WIDTH, HG_WIDTH), F32),
            pltpu.VMEM((HG_WIDTH, HG_WIDTH), BF16),
            pltpu.VMEM((T, HG_WIDTH), F32),
        ],
        compiler_params=_params(("arbitrary",)),
        name="hgrn2",
    )(lb_logits, g_t, hg)


def _pool_kernel(u_ref, w_ref, sc_ref, o_ref, halo_scr):
    T = u_ref.shape[0]
    H = POOL_HALO
    blk = pl.program_id(0)

    @pl.when(blk == 0)
    def _():
        halo_scr[...] = jnp.zeros(halo_scr.shape, F32)

    u = u_ref[...]
    ext = jnp.concatenate([halo_scr[...], u], axis=0)
    halo_scr[...] = u[T - H:, :]
    sums = []
    s = ext
    w = 1
    while w < POOL_WINDOWS[-1]:
        s = s + pltpu.roll(s, w, 0)
        w *= 2
        sums.append(s[H:, :])
    t1 = (blk * T + 1 + lax.broadcasted_iota(jnp.int32, (T, POOL_WIDTH), 0)).astype(F32)
    lane = lax.broadcasted_iota(jnp.int32, (T, POOL_WIDTH), 1)
    pooled = None
    for gi, win in enumerate(POOL_WINDOWS):
        mean = sums[gi] / jnp.minimum(t1, float(win))
        pooled = mean if pooled is None else jnp.where(lane >= gi * POOL_GROUP, mean, pooled)
    pooled = pooled - u
    y = jnp.dot(pooled.astype(BF16), w_ref[...], preferred_element_type=F32)
    o_ref[...] = (y * sc_ref[...]).astype(BF16)


def _multiscale_pool(pu, pool_w, pool_scale):
    S = pu.shape[0]
    T = ROW_TILE
    w_bd = jax.scipy.linalg.block_diag(*[pool_w[g] for g in range(len(POOL_WINDOWS))]).astype(BF16)
    return pl.pallas_call(
        _pool_kernel,
        grid=(S // T,),
        in_specs=[
            pl.BlockSpec((T, POOL_WIDTH), lambda i: (i, 0)),
            pl.BlockSpec((POOL_WIDTH, POOL_WIDTH), lambda i: (0, 0)),
            pl.BlockSpec((1, POOL_WIDTH), lambda i: (0, 0)),
        ],
        out_specs=pl.BlockSpec((T, POOL_WIDTH), lambda i: (i, 0)),
        out_shape=jax.ShapeDtypeStruct((S, POOL_WIDTH), BF16),
        scratch_shapes=[pltpu.VMEM((POOL_HALO, POOL_WIDTH), F32)],
        compiler_params=_params(("arbitrary",)),
        name="ms_pool",
    )(pu, w_bd, pool_scale.reshape(1, POOL_WIDTH))


def _outproj_kernel(a_ref, r_ref, p_ref, x_ref, wf_ref, g1_ref, lg_ref, lb_ref, o_ref, w_ref):
    @pl.when(pl.program_id(0) == 0)
    def _():
        w_ref[...] = wf_ref[...].astype(BF16)

    m = jnp.dot(a_ref[...], w_ref[0:DA_WIDTH, :], preferred_element_type=F32)
    m += jnp.dot(r_ref[...], w_ref[DA_WIDTH:DA_WIDTH + HG_WIDTH, :], preferred_element_type=F32)
    m += jnp.dot(p_ref[...], w_ref[DA_WIDTH + HG_WIDTH:, :], preferred_element_type=F32)
    y = ALPHA * x_ref[...] + (1.0 + g1_ref[...]) * m
    o_ref[...] = _layer_norm(y, lg_ref[...], lb_ref[...])


def _out_projection(a, r, p, x, w_out_f32, g1, ln_g, ln_b):
    S = x.shape[0]
    tm = ROW_TILE
    row = lambda i: (i, 0)
    fixed = lambda i: (0, 0)
    vec = pl.BlockSpec((1, D_MODEL), fixed)
    return pl.pallas_call(
        _outproj_kernel,
        grid=(S // tm,),
        in_specs=[
            pl.BlockSpec((tm, DA_WIDTH), row),
            pl.BlockSpec((tm, HG_WIDTH), row),
            pl.BlockSpec((tm, POOL_WIDTH), row),
            pl.BlockSpec((tm, D_MODEL), row),
            pl.BlockSpec((D_MIX, D_MODEL), fixed, pipeline_mode=pl.Buffered(1)),
            vec, vec, vec,
        ],
        out_specs=pl.BlockSpec((tm, D_MODEL), row),
        out_shape=jax.ShapeDtypeStruct((S, D_MODEL), F32),
        scratch_shapes=[pltpu.VMEM((D_MIX, D_MODEL), BF16)],
        compiler_params=_params(("arbitrary",)),
        name="out_proj_ln",
    )(a, r, p, x, w_out_f32, g1, ln_g.reshape(1, D_MODEL), ln_b.reshape(1, D_MODEL))


def _ffn_kernel(x_ref, sc_ref, sh_ref, g2_ref, w1_ref, w3_ref, w2_ref, lg_ref, lb_ref, *rest):
    n_cast = (len(rest) - 1) // 2
    o_ref = rest[n_cast]
    for src, dst in zip(rest[:n_cast], rest[n_cast + 1:]):
        dst[...] = src[...].astype(BF16)
    x = x_ref[...]
    h = (x * (1.0 + sc_ref[...]) + sh_ref[...]).astype(BF16)
    acc = jnp.zeros(x.shape, F32)
    step = 4 * MXU_DIM
    for lo in range(0, FFN_DIM, step):
        hi = min(lo + step, FFN_DIM)
        a = jnp.dot(h, w1_ref[:, lo:hi], preferred_element_type=F32)
        b = jnp.dot(h, w3_ref[:, lo:hi], preferred_element_type=F32)
        acc += jnp.dot((_silu(a) * b).astype(BF16), w2_ref[lo:hi, :], preferred_element_type=F32)
    y = ALPHA * x + (1.0 + g2_ref[...]) * acc
    o_ref[...] = _layer_norm(y, lg_ref[...], lb_ref[...])


def _dense_ffn(x, sc, sh, g2, w1, w3, w2, ln_g, ln_b, to_cast=()):
    S = x.shape[0]
    tm = FFN_CAST_TILE if to_cast else ROW_TILE
    steps = S // tm
    row = lambda i: (i, 0)
    fixed = lambda i: (0, 0)
    vec = pl.BlockSpec((1, D_MODEL), fixed)
    once = pl.Buffered(1)
    slabs = [a.reshape(-1, a.shape[-1]) for a in to_cast]
    slab_specs = [pl.BlockSpec((a.shape[0] // steps, a.shape[1]), row) for a in slabs]
    for a in slabs:
        assert a.shape[0] % (steps * 16) == 0
    outs = pl.pallas_call(
        _ffn_kernel,
        grid=(steps,),
        in_specs=[
            pl.BlockSpec((tm, D_MODEL), row),
            vec, vec, vec,
            pl.BlockSpec((D_MODEL, FFN_DIM), fixed, pipeline_mode=once),
            pl.BlockSpec((D_MODEL, FFN_DIM), fixed, pipeline_mode=once),
            pl.BlockSpec((FFN_DIM, D_MODEL), fixed, pipeline_mode=once),
            vec, vec,
        ] + slab_specs,
        out_specs=[pl.BlockSpec((tm, D_MODEL), row)] + slab_specs,
        out_shape=[jax.ShapeDtypeStruct((S, D_MODEL), F32)]
        + [jax.ShapeDtypeStruct(a.shape, BF16) for a in slabs],
        compiler_params=_params(("parallel",)),
        name="dense_ffn_ln",
    )(x, sc, sh, g2, w1, w3, w2, ln_g.reshape(1, D_MODEL), ln_b.reshape(1, D_MODEL), *slabs)
    return outs[0], [o.reshape(a.shape) for o, a in zip(outs[1:], to_cast)]


def _router_kernel(x_ref, sc_ref, sh_ref, rw_ref, h_ref, rank_ref, comb_ref, cum_ref):
    T = x_ref.shape[0]
    tc = MOE_CHUNK
    E = N_EXPERTS
    h = x_ref[...] * (1.0 + sc_ref[...]) + sh_ref[...]
    h_ref[...] = h.astype(BF16)
    logits = lax.dot_general(rw_ref[...], h, (((1,), (1,)), ((), ())),
                             preferred_element_type=F32, precision=HIGHEST)
    eidx = lax.broadcasted_iota(jnp.int32, (E, T), 0)
    v1 = jnp.max(logits, axis=0, keepdims=True)
    i1 = jnp.min(jnp.where(logits == v1, eidx, E), axis=0, keepdims=True)
    m1 = eidx == i1
    rest = jnp.where(m1, -jnp.inf, logits)
    v2 = jnp.max(rest, axis=0, keepdims=True)
    i2 = jnp.min(jnp.where(rest == v2, eidx, E), axis=0, keepdims=True)
    m2 = eidx == i2
    e2 = jnp.exp(v2 - v1)
    g1 = 1.0 / (1.0 + e2)
    g2 = e2 / (1.0 + e2)
    comb = jnp.where(m1, g1, 0.0) + jnp.where(m2, g2, 0.0)
    routed = jnp.logical_or(m1, m2)
    ind = routed.astype(BF16)
    r_i = lax.broadcasted_iota(jnp.int32, (tc, tc), 0)
    c_i = lax.broadcasted_iota(jnp.int32, (tc, tc), 1)
    strict = (r_i < c_i).astype(BF16)
    lane = lax.broadcasted_iota(jnp.int32, (E, LANES), 1)
    running = jnp.zeros((E, 1), F32)
    cum = jnp.zeros((E, LANES), F32)
    for ci in range(T // tc):
        sl = slice(ci * tc, (ci + 1) * tc)
        ind_c = ind[:, sl]
        rank_c = jnp.dot(ind_c, strict, preferred_element_type=F32) + running
        rank_ref[:, ci] = jnp.where(routed[:, sl], rank_c, -1.0).reshape(E, 1, tc)
        comb_ref[:, ci] = comb[:, sl].reshape(E, 1, tc)
        cum = jnp.where(lane == ci, running, cum)
        running = running + jnp.sum(ind_c.astype(F32), axis=1, keepdims=True)
    cum = jnp.where(lane == T // tc, running, cum)
    cum_ref[0] = cum.astype(jnp.int32)


def _router(x, sc, sh, router_w):
    S = x.shape[0]
    T = MOE_BLOCK
    E = N_EXPERTS
    fixed = lambda i: (0, 0)
    vec = pl.BlockSpec((1, D_MODEL), fixed)
    return pl.pallas_call(
        _router_kernel,
        grid=(S // T,),
        in_specs=[
            pl.BlockSpec((T, D_MODEL), lambda i: (i, 0)),
            vec, vec,
            pl.BlockSpec((E, D_MODEL), fixed),
        ],
        out_specs=[
            pl.BlockSpec((T, D_MODEL), lambda i: (i, 0)),
            pl.BlockSpec((E, T // MOE_CHUNK, 1, MOE_CHUNK), lambda i: (0, i, 0, 0)),
            pl.BlockSpec((E, T // MOE_CHUNK, 1, MOE_CHUNK), lambda i: (0, i, 0, 0)),
            pl.BlockSpec((1, E, LANES), lambda i: (i, 0, 0)),
        ],
        out_shape=[
            jax.ShapeDtypeStruct((S, D_MODEL), BF16),
            jax.ShapeDtypeStruct((E, S // MOE_CHUNK, 1, MOE_CHUNK), F32),
            jax.ShapeDtypeStruct((E, S // MOE_CHUNK, 1, MOE_CHUNK), F32),
            jax.ShapeDtypeStruct((S // T, E, LANES), jnp.int32),
        ],
        compiler_params=_params(("parallel",)),
        name="moe_router",
    )(x, sc, sh, router_w.T)


def _moe_kernel(cum_ref, h_ref, rank_ref, comb_ref, w1_ref, w3_ref, w2_ref, y_ref, xe_scr, gate_scr,
                out_scr):
    T = h_ref.shape[0]
    tr, tc, win = MOE_TILE, MOE_CHUNK, MOE_WINDOW
    nchunk = T // tc
    b = pl.program_id(0)
    e = pl.program_id(1)
    f = pl.program_id(2)

    @pl.when(jnp.logical_and(e == 0, f == 0))
    def _():
        y_ref[...] = jnp.zeros(y_ref.shape, F32)

    base = (b * N_EXPERTS + e) * LANES
    count = cum_ref[base + nchunk]
    ntiles = (count + (tr - 1)) // tr

    def token_row(ref, first, n):
        blk = ref[0, pl.ds(first, n)]
        return jnp.concatenate([blk[c] for c in range(n)], axis=1)

    def tile(r, carry):
        lo = r * tr
        rows = pl.ds(pl.multiple_of(lo, 16), tr)

        def hits(first, n):
            want = (lo + lax.broadcasted_iota(jnp.int32, (tr, n * tc), 0)).astype(F32)
            return token_row(rank_ref, first, n) == want

        def token_span(first, n):
            start = first * tc
            if not isinstance(start, int):
                start = pl.multiple_of(start, tc)
            return pl.ds(start, n * tc)

        c_lo = jnp.int32(0)
        c_hi = jnp.int32(-1)
        for c in range(nchunk):
            c_lo += (cum_ref[base + c + 1] <= lo).astype(jnp.int32)
            c_hi += (cum_ref[base + c] < lo + tr).astype(jnp.int32)
        fits = c_hi - c_lo < win
        window = (jnp.minimum(c_lo, nchunk - win), win)
        block = (0, nchunk)

        def gather(first, n):
            hit = hits(first, n)
            xe_scr[rows, :] = jnp.dot(hit.astype(BF16), h_ref[token_span(first, n), :],
                                      preferred_element_type=F32).astype(BF16)
            gate_scr[rows, :] = jnp.sum(jnp.where(hit, token_row(comb_ref, first, n), 0.0),
                                        axis=1, keepdims=True)

        @pl.when(jnp.logical_and(f == 0, fits))
        def _():
            gather(*window)

        @pl.when(jnp.logical_and(f == 0, jnp.logical_not(fits)))
        def _():
            gather(*block)

        xe = xe_scr[rows, :]
        a = jnp.dot(xe, w1_ref[0], preferred_element_type=F32)
        g = jnp.dot(xe, w3_ref[0], preferred_element_type=F32)
        mid = (_silu(a) * g).astype(BF16)
        part = jnp.dot(mid, w2_ref[0], preferred_element_type=F32)

        @pl.when(f == 0)
        def _():
            out_scr[rows, :] = part

        if MOE_FSPLIT > 2:
            @pl.when(jnp.logical_and(f > 0, f < MOE_FSPLIT - 1))
            def _():
                out_scr[rows, :] += part

        @pl.when(f == MOE_FSPLIT - 1)
        def _():
            out_bf = ((out_scr[rows, :] + part) * gate_scr[rows, :]).astype(BF16)

            def scatter(first, n):
                y_ref[token_span(first, n), :] += lax.dot_general(
                    hits(first, n).astype(BF16), out_bf, (((0,), (0,)), ((), ())),
                    preferred_element_type=F32)

            @pl.when(fits)
            def _():
                scatter(*window)

            @pl.when(jnp.logical_not(fits))
            def _():
                scatter(*block)

        return carry

    lax.fori_loop(0, ntiles, tile, 0)


def _moe_ffn(h_bf, rank, comb, cum, w1, w3, w2):
    S = h_bf.shape[0]
    T = MOE_BLOCK
    fw = EXPERT_DIM // MOE_FSPLIT
    grid_spec = pltpu.PrefetchScalarGridSpec(
        num_scalar_prefetch=1,
        grid=(S // T, N_EXPERTS, MOE_FSPLIT),
        in_specs=[
            pl.BlockSpec((T, D_MODEL), lambda b, e, f, c: (b, 0), pipeline_mode=pl.Buffered(1)),
            pl.BlockSpec((1, T // MOE_CHUNK, 1, MOE_CHUNK), lambda b, e, f, c: (e, b, 0, 0)),
            pl.BlockSpec((1, T // MOE_CHUNK, 1, MOE_CHUNK), lambda b, e, f, c: (e, b, 0, 0)),
            pl.BlockSpec((1, D_MODEL, fw), lambda b, e, f, c: (e, 0, f)),
            pl.BlockSpec((1, D_MODEL, fw), lambda b, e, f, c: (e, 0, f)),
            pl.BlockSpec((1, fw, D_MODEL), lambda b, e, f, c: (e, f, 0)),
        ],
        out_specs=pl.BlockSpec((T, D_MODEL), lambda b, e, f, c: (b, 0)),
        scratch_shapes=[
            pltpu.VMEM((T + MOE_TILE, D_MODEL), BF16),
            pltpu.VMEM((T + MOE_TILE, 1), F32),
            pltpu.VMEM((T + MOE_TILE, D_MODEL), F32),
        ],
    )
    return pl.pallas_call(
        _moe_kernel,
        grid_spec=grid_spec,
        out_shape=jax.ShapeDtypeStruct((S, D_MODEL), F32),
        compiler_params=_params(("arbitrary", "arbitrary", "arbitrary")),
        name="moe_experts",
    )(cum.reshape(-1), h_bf, rank, comb, w1, w3, w2)


def _ln_kernel(x_ref, y_ref, g2_ref, lg_ref, lb_ref, o_ref):
    y = ALPHA * x_ref[...] + (1.0 + g2_ref[...]) * y_ref[...]
    o_ref[...] = _layer_norm(y, lg_ref[...], lb_ref[...])


def _residual_ln(x, y, g2, ln_g, ln_b):
    S = x.shape[0]
    tm = ROW_TILE
    row = lambda i: (i, 0)
    vec = pl.BlockSpec((1, D_MODEL), lambda i: (0, 0))
    return pl.pallas_call(
        _ln_kernel,
        grid=(S // tm,),
        in_specs=[pl.BlockSpec((tm, D_MODEL), row), pl.BlockSpec((tm, D_MODEL), row), vec, vec, vec],
        out_specs=pl.BlockSpec((tm, D_MODEL), row),
        out_shape=jax.ShapeDtypeStruct((S, D_MODEL), F32),
        compiler_params=_params(("parallel",)),
        name="residual_ln",
    )(x, y, g2, ln_g.reshape(1, D_MODEL), ln_b.reshape(1, D_MODEL))


def _rope_tables(S):
    d = DA_HEAD_DIM
    inv_freq = 1.0 / (ROPE_THETA ** (jnp.arange(0, d, 2, dtype=F32) / d))
    lane = jnp.arange(LANES)
    ang = jnp.arange(S, dtype=F32)[:, None] * inv_freq[lane % (d // 2)][None, :]
    sign = jnp.where(lane % d < d // 2, -1.0, 1.0).astype(F32)
    return jnp.cos(ang), jnp.sin(ang) * sign[None, :]


def kernel(x, c, w_ada, b_ada, w_in, lam_qk, attn_norm_g, hg_lb_logits, hg_norm_g, pool_w, pool_scale, w_out, ln1_g, ln1_b, ln2_g, ln2_b, ffn_w1, ffn_w3, ffn_w2, router_w, exp_w1, exp_w3, exp_w2):
    B, S, D = x.shape
    assert B == 1 and D == D_MODEL and S % MOE_BLOCK == 0
    xs = x.reshape(S, D)
    mod = _modulation(c, w_ada, b_ada)
    cos_t, sin_t = _rope_tables(S)
    experts_bf = {}
    for l in range(DEPTH):
        sh1, sc1, g1, sh2, sc2, g2 = (mod[l, j] for j in range(6))
        qt, k, vt, hg, pu = _in_projection(xs, sc1, sh1, w_in[l], cos_t, sin_t)
        a = _diff_attention(qt, k, vt, lam_qk[l], attn_norm_g[l], l)
        r = _hgrn2(hg, hg_lb_logits, hg_norm_g[l], l)
        p = _multiscale_pool(pu, pool_w[l], pool_scale[l])
        xs = _out_projection(a, r, p, xs, w_out[l], g1, ln1_g[l], ln1_b[l])
        if l % 2 == 0:
            i = l // 2
            ride = (exp_w1[i], exp_w3[i], exp_w2[i]) if l + 1 < DEPTH else ()
            xs, cast = _dense_ffn(xs, sc2, sh2, g2, ffn_w1[i].astype(BF16), ffn_w3[i].astype(BF16),
                                  ffn_w2[i].astype(BF16), ln2_g[l], ln2_b[l], ride)
            if ride:
                experts_bf[i] = cast
        else:
            i = l // 2
            h_bf, rank, comb, cum = _router(xs, sc2, sh2, router_w[i])
            w1, w3, w2 = experts_bf.get(i) or [w[i].astype(BF16) for w in (exp_w1, exp_w3, exp_w2)]
            y = _moe_ffn(h_bf, rank, comb, cum, w1, w3, w2)
            xs = _residual_ln(xs, y, g2, ln2_g[l], ln2_b[l])
    return xs.reshape(B, S, D)
```

```python
import functools
import math

import jax
import jax.numpy as jnp
from jax import lax
from jax.experimental import pallas as pl
from jax.experimental.pallas import tpu as pltpu

F32 = jnp.float32
BF16 = jnp.bfloat16
HIGHEST = lax.Precision.HIGHEST

D_MODEL = 1024
DEPTH = 2
DA_HEADS = 4
DA_WIDTH = D_MODEL // 2
DA_HEAD_DIM = DA_WIDTH // (2 * DA_HEADS)
DA_VAL_DIM = 2 * DA_HEAD_DIM
HG_HEADS = 4
HG_WIDTH = D_MODEL // 4
HG_DIM = HG_WIDTH // HG_HEADS
POOL_WINDOWS = (2, 4, 8, 16)
POOL_WIDTH = D_MODEL // 4
POOL_GROUP = POOL_WIDTH // len(POOL_WINDOWS)
D_MIX = DA_WIDTH + HG_WIDTH + POOL_WIDTH
D_IN = 3 * DA_WIDTH + 4 * HG_WIDTH + POOL_WIDTH
ROPE_THETA = 10000.0
MASK_VALUE = -1e30
TINY = 1e-30
FFN_DIM = 11 * D_MODEL // 4
N_EXPERTS = 8
EXPERT_DIM = 7 * D_MODEL // 2
ALPHA = (2 * DEPTH) ** 0.25
EPS = 1e-5
LOG2_E = math.log2(math.e)

LANES = 128
MXU_DIM = 256
VMEM_LIMIT = 60 * 1024 * 1024

ROW_TILE = 512
FFN_CAST_TILE = 256
ATTN_BLOCK = 512
ATTN_QUERY_GROUP = 256
ATTN_SUM_ROWS = 16
ATTN_VT_ROWS = DA_VAL_DIM + ATTN_SUM_ROWS
HG_BLOCK = 512
HG_CHUNK = 16
HG_UNROLL = 4
POOL_HALO = 16
MOE_BLOCK = 2048
MOE_TILE = 192
MOE_CHUNK = 256
MOE_WINDOW = 5
MOE_FSPLIT = 2


def _params(sem, vmem=VMEM_LIMIT):
    return pltpu.CompilerParams(dimension_semantics=sem, vmem_limit_bytes=vmem)


def _layer_norm(y, g, b):
    mu = jnp.mean(y, axis=-1, keepdims=True)
    d = y - mu
    var = jnp.mean(d * d, axis=-1, keepdims=True)
    return d * lax.rsqrt(var + EPS) * g + b


def _silu(x):
    return x * jax.nn.sigmoid(x)


def _head_block_mask(n, group):
    r = lax.broadcasted_iota(jnp.int32, (n, n), 0) // group
    c = lax.broadcasted_iota(jnp.int32, (n, n), 1) // group
    return r == c


def _mod_kernel(c_ref, w_ref, b_ref, o_ref):
    cond = _silu(c_ref[...])
    cond8 = jnp.broadcast_to(cond, (8, D_MODEL))
    r = jnp.dot(cond8, w_ref[0], preferred_element_type=F32, precision=HIGHEST)
    o_ref[0] = r[0:1] + b_ref[0]


def _modulation(c, w_ada, b_ada):
    n = 6 * DEPTH
    b3 = b_ada.reshape(n, 1, D_MODEL)
    out = pl.pallas_call(
        _mod_kernel,
        grid=(DEPTH, 6),
        in_specs=[
            pl.BlockSpec((1, D_MODEL), lambda l, j: (0, 0)),
            pl.BlockSpec((1, D_MODEL, D_MODEL), lambda l, j: (l, 0, j)),
            pl.BlockSpec((1, 1, D_MODEL), lambda l, j: (l * 6 + j, 0, 0)),
        ],
        out_specs=pl.BlockSpec((1, 1, D_MODEL), lambda l, j: (l * 6 + j, 0, 0)),
        out_shape=jax.ShapeDtypeStruct((n, 1, D_MODEL), F32),
        compiler_params=_params(("parallel", "parallel")),
        name="adaln_mod",
    )(c, w_ada, b3)
    return out.reshape(DEPTH, 6, 1, D_MODEL)


def _inproj_kernel(x_ref, sc_ref, sh_ref, wf_ref, cosb_ref, sinb_ref, cosr_ref, sinr_ref,
                   qt_ref, k_ref, vt_ref, hg_ref, pu_ref, w_ref, wvt_ref):
    @pl.when(pl.program_id(0) == 0)
    def _():
        w_ref[...] = wf_ref[0].astype(BF16)
        wvt_ref[...] = wf_ref[0, :, 2 * DA_WIDTH:3 * DA_WIDTH].T.astype(BF16)

    h = (x_ref[...] * (1.0 + sc_ref[...]) + sh_ref[...]).astype(BF16)
    qk = jnp.dot(h, w_ref[:, : 2 * DA_WIDTH], preferred_element_type=F32)
    cb, sb, cr, sr = cosb_ref[0], sinb_ref[0], cosr_ref[...], sinr_ref[...]
    cos128 = cb * cr - sb * sr
    sin128 = sb * cr + cb * sr
    half = DA_HEAD_DIM // 2
    lane128 = lax.broadcasted_iota(jnp.int32, cos128.shape, 1)
    sin128 = jnp.where((lane128 & half) == 0, -sin128, sin128)
    reps = 2 * DA_WIDTH // LANES
    cos = jnp.tile(cos128, (1, reps))
    sin = jnp.tile(sin128, (1, reps))
    lane = lax.broadcasted_iota(jnp.int32, qk.shape, 1)
    first_half = (lane & half) == 0
    width = 2 * DA_WIDTH
    partner = jnp.where(first_half, pltpu.roll(qk, width - half, 1), pltpu.roll(qk, half, 1))
    rot = qk * cos + partner * sin
    qt_ref[0] = (rot[:, :DA_WIDTH] * (DA_HEAD_DIM ** -0.5 * LOG2_E)).T.astype(BF16)
    k_ref[...] = rot[:, DA_WIDTH:].astype(BF16)
    o = 2 * DA_WIDTH
    vt = lax.dot_general(wvt_ref[...], h, (((1,), (1,)), ((), ())),
                         preferred_element_type=F32).astype(BF16)
    ones = jnp.ones((ATTN_SUM_ROWS, vt.shape[1]), BF16)
    pieces = []
    for hd in range(DA_HEADS):
        pieces += [vt[hd * DA_VAL_DIM:(hd + 1) * DA_VAL_DIM], ones]
    vt_ref[0] = jnp.concatenate(pieces, axis=0)
    o += DA_WIDTH
    hg_ref[...] = jnp.dot(h, w_ref[:, o:o + 4 * HG_WIDTH], preferred_element_type=F32)
    o += 4 * HG_WIDTH
    pu_ref[...] = jnp.dot(h, w_ref[:, o:o + POOL_WIDTH], preferred_element_type=F32)


def _in_projection(x, sc, sh, w_in, layer, rope):
    S = x.shape[0]
    tm = ATTN_BLOCK
    row = lambda i: (i, 0)
    fixed = lambda i: (0, 0)
    cos_b, sin_b, cos_r, sin_r = rope
    return pl.pallas_call(
        _inproj_kernel,
        grid=(S // tm,),
        in_specs=[
            pl.BlockSpec((tm, D_MODEL), row),
            pl.BlockSpec((1, D_MODEL), fixed),
            pl.BlockSpec((1, D_MODEL), fixed),
            pl.BlockSpec((1, D_MODEL, D_IN), lambda i: (layer, 0, 0), pipeline_mode=pl.Buffered(1)),
            pl.BlockSpec((1, 1, LANES), lambda i: (i, 0, 0)),
            pl.BlockSpec((1, 1, LANES), lambda i: (i, 0, 0)),
            pl.BlockSpec((tm, LANES), fixed),
            pl.BlockSpec((tm, LANES), fixed),
        ],
        out_specs=[
            pl.BlockSpec((1, DA_WIDTH, tm), lambda i: (i, 0, 0)),
            pl.BlockSpec((tm, DA_WIDTH), row),
            pl.BlockSpec((1, DA_HEADS * ATTN_VT_ROWS, tm), lambda i: (i, 0, 0)),
            pl.BlockSpec((tm, 4 * HG_WIDTH), row),
            pl.BlockSpec((tm, POOL_WIDTH), row),
        ],
        out_shape=[
            jax.ShapeDtypeStruct((S // tm, DA_WIDTH, tm), BF16),
            jax.ShapeDtypeStruct((S, DA_WIDTH), BF16),
            jax.ShapeDtypeStruct((S // tm, DA_HEADS * ATTN_VT_ROWS, tm), BF16),
            jax.ShapeDtypeStruct((S, 4 * HG_WIDTH), F32),
            jax.ShapeDtypeStruct((S, POOL_WIDTH), F32),
        ],
        scratch_shapes=[
            pltpu.VMEM((D_MODEL, D_IN), BF16),
            pltpu.VMEM((DA_WIDTH, D_MODEL), BF16),
        ],
        compiler_params=_params(("arbitrary",)),
        name="in_proj",
    )(x, sc, sh, w_in, cos_b, sin_b, cos_r, sin_r)


def _attn_kernel(lam_ref, g_ref, qt_ref, k_ref, vt_ref, o_ref, m_scr, a_scr, c_scr, acc_scr, s_scr,
                 p_scr, *, lam_init):
    bq = qt_ref.shape[2]
    bk = vt_ref.shape[2]
    cg = ATTN_QUERY_GROUP
    i = pl.program_id(1)
    qt = qt_ref[0]
    rowq = lax.broadcasted_iota(jnp.int32, qt.shape, 0)
    zero = jnp.zeros_like(qt)
    maps = (jnp.where(rowq < DA_HEAD_DIM, qt, zero), jnp.where(rowq >= DA_HEAD_DIM, qt, zero))
    groups = [(a * bq + c0, c0, maps[a][:, c0:c0 + cg]) for a in range(2) for c0 in range(0, bq, cg)]
    m_scr[...] = jnp.full(m_scr.shape, -jnp.inf, F32)
    acc_scr[...] = jnp.zeros(acc_scr.shape, F32)
    hk = bk // 2
    key_i = lax.broadcasted_iota(jnp.int32, (hk, cg), 0)
    qry_i = lax.broadcasted_iota(jnp.int32, (hk, cg), 1)
    ng = len(groups)
    last = ng - 1

    def cols_of(g):
        return slice(groups[g][0], groups[g][0] + cg)

    def ccols_of(sset, g):
        return slice(sset * 2 * bq + groups[g][0], sset * 2 * bq + groups[g][0] + cg)

    def mask_mode(key_offset, g):
        c0 = groups[g][1]
        if key_offset + hk - 1 <= c0:
            return None
        if key_offset > c0 + cg - 1:
            return "skip"
        return key_offset

    def score_stage(hb, sset, g, key_offset):
        kb = k_ref[pl.ds(pl.multiple_of(hb * hk, hk), hk), :]
        s = jnp.dot(kb, groups[g][2], preferred_element_type=F32)
        if key_offset is not None:
            s = jnp.where(key_i + key_offset <= qry_i + groups[g][1], s, MASK_VALUE)
        s_scr[sset * ng + g] = s
        c_scr[:, ccols_of(sset, g)] = jnp.max(s, axis=0, keepdims=True)

    def max_stage(sset, g):
        cols = cols_of(g)
        m_prev = m_scr[:, cols]
        m_new = jnp.maximum(m_prev, c_scr[:, ccols_of(sset, g)])
        a_scr[:, cols] = jnp.exp2(m_prev - m_new)
        m_scr[:, cols] = m_new

    def exp_stage(sset, g, vt_half, carried, defer):
        cols = cols_of(g)
        p = jnp.exp2(s_scr[sset * ng + g] - m_scr[:, cols])
        alpha = a_scr[:, cols]
        pb = p.astype(BF16)
        acc = acc_scr[:, cols]
        if carried is not None:
            acc = acc + carried
        if defer:
            acc_scr[:, cols] = alpha * acc
            p_scr[...] = pb
        else:
            acc_scr[:, cols] = alpha * acc + jnp.dot(vt_half, pb, preferred_element_type=F32)

    def iteration(t, look, diagonal):
        vt_a = vt_ref[t, :, 0:hk]
        vt_b = vt_ref[t, :, hk:bk]
        carried = jnp.dot(vt_ref[jnp.maximum(t - 1, 0), :, hk:bk], p_scr[...],
                          preferred_element_type=F32)
        if diagonal:
            b_items = [(g, mask_mode(hk, g)) for g in range(ng) if mask_mode(hk, g) != "skip"]
        else:
            b_items = [(g, None) for g in range(ng)]
        score_stage(2 * t + 1, 1, *b_items[0])
        for g in range(ng):
            if g + 1 < len(b_items):
                score_stage(2 * t + 1, 1, *b_items[g + 1])
            if g + 1 < ng:
                max_stage(0, g + 1)
            else:
                max_stage(1, b_items[0][0])
            exp_stage(0, g, vt_a, carried if g == last else None, False)
        if not diagonal:
            score_stage(2 * t + 2, 0, 0, mask_mode(0, 0) if look == "diag" else None)
        for n, (g, _) in enumerate(b_items):
            if not diagonal and g + 1 < ng:
                score_stage(2 * t + 2, 0, g + 1, mask_mode(0, g + 1) if look == "diag" else None)
            if n + 1 < len(b_items):
                max_stage(1, b_items[n + 1][0])
            elif not diagonal:
                max_stage(0, 0)
            exp_stage(1, g, vt_b, None, not diagonal and g == last)

    p_scr[...] = jnp.zeros(p_scr.shape, BF16)
    for g in range(ng):
        score_stage(0, 0, g, (0 - i) * bq)
    max_stage(0, 0)

    def body(t, carry):
        iteration(t, None, False)
        return carry

    lax.fori_loop(0, i - 1, body, 0)

    @pl.when(i > 0)
    def _():
        iteration(i - 1, "diag", False)

    iteration(i, None, True)

    lq = lam_ref[...]
    lam = (jnp.exp(jnp.sum(lq[0:1] * lq[1:2], axis=1, keepdims=True))
           - jnp.exp(jnp.sum(lq[2:3] * lq[3:4], axis=1, keepdims=True)) + lam_init)
    o = acc_scr[0:DA_VAL_DIM, :] / acc_scr[DA_VAL_DIM:DA_VAL_DIM + 1, :]
    o = o[:, :bq] - lam * o[:, bq:]
    ms = jnp.mean(o * o, axis=0, keepdims=True)
    o = o * lax.rsqrt(ms + EPS) * g_ref[...] * (1.0 - lam_init)
    o_ref[...] = o.T.astype(BF16)


def _diff_attention(qt, k, vt, lam_qk, attn_g, layer):
    S = k.shape[0]
    bq = bk = ATTN_BLOCK
    nk = S // bk
    lam_init = 0.8 - 0.6 * math.exp(-0.3 * layer)
    return pl.pallas_call(
        functools.partial(_attn_kernel, lam_init=lam_init),
        grid=(DA_HEADS, S // bq),
        in_specs=[
            pl.BlockSpec((4, DA_HEAD_DIM), lambda h, i: (0, 0)),
            pl.BlockSpec((DA_VAL_DIM, 1), lambda h, i: (0, 0)),
            pl.BlockSpec((1, LANES, bq), lambda h, i: (i, h, 0)),
            pl.BlockSpec((S, LANES), lambda h, i: (0, h)),
            pl.BlockSpec((nk, ATTN_VT_ROWS, bk), lambda h, i: (0, h, 0)),
        ],
        out_specs=pl.BlockSpec((bq, DA_VAL_DIM), lambda h, i: (i, h)),
        out_shape=jax.ShapeDtypeStruct((S, DA_WIDTH), BF16),
        scratch_shapes=[
            pltpu.VMEM((1, 2 * bq), F32),
            pltpu.VMEM((1, 2 * bq), F32),
            pltpu.VMEM((1, 2 * 2 * bq), F32),
            pltpu.VMEM((ATTN_VT_ROWS, 2 * bq), F32),
            pltpu.VMEM((2 * (2 * bq // ATTN_QUERY_GROUP), bk // 2, ATTN_QUERY_GROUP), F32),
            pltpu.VMEM((bk // 2, ATTN_QUERY_GROUP), BF16),
        ],
        compiler_params=_params(("parallel", "parallel")),
        name="diff_attn",
    )(lam_qk, attn_g.reshape(DA_VAL_DIM, 1), qt, k, vt)


def _hgrn_kernel(lbl_ref, g_ref, hg_ref, o_ref, st_scr, stb_scr, o_scr, *, layer):
    T = hg_ref.shape[0]
    c = HG_CHUNK
    W = HG_WIDTH

    @pl.when(pl.program_id(0) == 0)
    def _():
        st_scr[...] = jnp.zeros(st_scr.shape, F32)
        stb_scr[...] = jnp.zeros(stb_scr.shape, BF16)

    lg = lbl_ref[...]
    ex = jnp.exp(lg - jnp.max(lg, axis=0, keepdims=True))
    prob = ex / jnp.sum(ex, axis=0, keepdims=True)
    lb = jnp.zeros((1, W), F32)
    for li in range(1, layer + 1):
        lb = lb + prob[li:li + 1]

    same_head = _head_block_mask(W, HG_DIM)
    ones_bd = same_head.astype(BF16)
    rowi = lax.broadcasted_iota(jnp.int32, (c, W), 0)

    def body(s, carry):
        sl = pl.ds(pl.multiple_of(s * c, c), c)
        qf = hg_ref[sl, 0:W]
        z = hg_ref[sl, W:2 * W]
        iv = hg_ref[sl, 2 * W:3 * W]
        f = lb + (1.0 - lb) * jax.nn.sigmoid(z)
        logf = jnp.log(jnp.maximum(f, TINY))
        key = (1.0 - lb) * jax.nn.sigmoid(-z)
        val = _silu(iv)
        b = logf
        sft = 1
        while sft < c:
            b = b + jnp.where(rowi >= sft, pltpu.roll(b, sft, 0), 0.0)
            sft *= 2
        b_last = b[c - 1:c]
        kd = (key * jnp.exp(b_last - b)).astype(BF16)
        upd = lax.dot_general(val.astype(BF16), kd, (((0,), (0,)), ((), ())),
                              preferred_element_type=F32)
        es = [(qf * key).astype(BF16)]
        vrs = [val]
        for d in range(1, c):
            kr = pltpu.roll(key, d, 0)
            br = pltpu.roll(b, d, 0)
            e = jnp.where(rowi >= d, qf * kr * jnp.exp(b - br), 0.0)
            es.append(e.astype(BF16))
            vrs.append(pltpu.roll(val, d, 0))
        estack = jnp.concatenate(es, axis=0)
        r = jnp.dot(estack, ones_bd, preferred_element_type=F32)
        o = r[0:c] * vrs[0]
        for d in range(1, c):
            o = o + r[d * c:(d + 1) * c] * vrs[d]
        qd = (qf * jnp.exp(b)).astype(BF16)
        o = o + lax.dot_general(qd, stb_scr[...], (((1,), (1,)), ((), ())),
                                preferred_element_type=F32)
        o_scr[sl, :] = o
        decay = jnp.exp(b_last)
        for hd in range(HG_HEADS):
            rws = slice(hd * HG_DIM, (hd + 1) * HG_DIM)
            lns = slice(hd * HG_DIM // LANES * LANES, (hd * HG_DIM // LANES + 1) * LANES)
            new = decay[:, lns] * st_scr[rws, lns] + jnp.where(same_head[rws, lns], upd[rws, lns], 0.0)
            st_scr[rws, lns] = new
            stb_scr[rws, lns] = new.astype(BF16)
        return carry

    lax.fori_loop(0, T // c, body, 0, unroll=HG_UNROLL)

    o = o_scr[...]
    ms = jnp.dot(o * o, same_head.astype(F32), preferred_element_type=F32,
                 precision=HIGHEST) * (1.0 / HG_DIM)
    gate = hg_ref[:, 3 * W:4 * W]
    o_ref[...] = (o * lax.rsqrt(ms + EPS) * g_ref[...] * _silu(gate)).astype(BF16)


def _hgrn2(hg, lb_logits, norm_g, layer):
    S = hg.shape[0]
    T = HG_BLOCK
    g_t = jnp.tile(norm_g.reshape(1, HG_DIM), (1, HG_HEADS))
    return pl.pallas_call(
        functools.partial(_hgrn_kernel, layer=layer),
        grid=(S // T,),
        in_specs=[
            pl.BlockSpec((DEPTH, HG_WIDTH), lambda i: (0, 0)),
            pl.BlockSpec((1, HG_WIDTH), lambda i: (0, 0)),
            pl.BlockSpec((T, 4 * HG_WIDTH), lambda i: (i, 0)),
        ],
        out_specs=pl.BlockSpec((T, HG_WIDTH), lambda i: (i, 0)),
        out_shape=jax.ShapeDtypeStruct((S, HG_WIDTH), BF16),
        scratch_shapes=[
            pltpu.VMEM((HG_WIDTH, HG_WIDTH), F32),
            pltpu.VMEM((HG_WIDTH, HG_WIDTH), BF16),
            pltpu.VMEM((T, HG_WIDTH), F32),
        ],
        compiler_params=_params(("arbitrary",)),
        name="hgrn2",
    )(lb_logits, g_t, hg)


def _pool_kernel(u_ref, w_ref, sc_ref, o_ref, halo_scr):
    T = u_ref.shape[0]
    H = POOL_HALO
    blk = pl.program_id(0)

    @pl.when(blk == 0)
    def _():
        halo_scr[...] = jnp.zeros(halo_scr.shape, F32)

    u = u_ref[...]
    ext = jnp.concatenate([halo_scr[...], u], axis=0)
    halo_scr[...] = u[T - H:, :]
    sums = []
    s = ext
    w = 1
    while w < POOL_WINDOWS[-1]:
        s = s + pltpu.roll(s, w, 0)
        w *= 2
        sums.append(s[H:, :])
    t1 = (blk * T + 1 + lax.broadcasted_iota(jnp.int32, (T, POOL_WIDTH), 0)).astype(F32)
    lane = lax.broadcasted_iota(jnp.int32, (T, POOL_WIDTH), 1)
    pooled = None
    for gi, win in enumerate(POOL_WINDOWS):
        mean = sums[gi] / jnp.minimum(t1, float(win))
        pooled = mean if pooled is None else jnp.where(lane >= gi * POOL_GROUP, mean, pooled)
    pooled = pooled - u
    y = jnp.dot(pooled.astype(BF16), w_ref[...], preferred_element_type=F32)
    o_ref[...] = (y * sc_ref[...]).astype(BF16)


def _multiscale_pool(pu, pool_w, pool_scale):
    S = pu.shape[0]
    T = ROW_TILE
    w_bd = jax.scipy.linalg.block_diag(*[pool_w[g] for g in range(len(POOL_WINDOWS))]).astype(BF16)
    return pl.pallas_call(
        _pool_kernel,
        grid=(S // T,),
        in_specs=[
            pl.BlockSpec((T, POOL_WIDTH), lambda i: (i, 0)),
            pl.BlockSpec((POOL_WIDTH, POOL_WIDTH), lambda i: (0, 0)),
            pl.BlockSpec((1, POOL_WIDTH), lambda i: (0, 0)),
        ],
        out_specs=pl.BlockSpec((T, POOL_WIDTH), lambda i: (i, 0)),
        out_shape=jax.ShapeDtypeStruct((S, POOL_WIDTH), BF16),
        scratch_shapes=[pltpu.VMEM((POOL_HALO, POOL_WIDTH), F32)],
        compiler_params=_params(("arbitrary",)),
        name="ms_pool",
    )(pu, w_bd, pool_scale.reshape(1, POOL_WIDTH))


def _outproj_kernel(a_ref, r_ref, p_ref, x_ref, wf_ref, g1_ref, lg_ref, lb_ref, o_ref, w_ref):
    @pl.when(pl.program_id(0) == 0)
    def _():
        w_ref[...] = wf_ref[0].astype(BF16)

    m = jnp.dot(a_ref[...], w_ref[0:DA_WIDTH, :], preferred_element_type=F32)
    m += jnp.dot(r_ref[...], w_ref[DA_WIDTH:DA_WIDTH + HG_WIDTH, :], preferred_element_type=F32)
    m += jnp.dot(p_ref[...], w_ref[DA_WIDTH + HG_WIDTH:, :], preferred_element_type=F32)
    y = ALPHA * x_ref[...] + (1.0 + g1_ref[...]) * m
    o_ref[...] = _layer_norm(y, lg_ref[...], lb_ref[...])


def _out_projection(a, r, p, x, w_out, layer, g1, ln_g, ln_b):
    S = x.shape[0]
    tm = ROW_TILE
    row = lambda i: (i, 0)
    fixed = lambda i: (0, 0)
    vec = pl.BlockSpec((1, D_MODEL), fixed)
    return pl.pallas_call(
        _outproj_kernel,
        grid=(S // tm,),
        in_specs=[
            pl.BlockSpec((tm, DA_WIDTH), row),
            pl.BlockSpec((tm, HG_WIDTH), row),
            pl.BlockSpec((tm, POOL_WIDTH), row),
            pl.BlockSpec((tm, D_MODEL), row),
            pl.BlockSpec((1, D_MIX, D_MODEL), lambda i: (layer, 0, 0), pipeline_mode=pl.Buffered(1)),
            vec, vec, vec,
        ],
        out_specs=pl.BlockSpec((tm, D_MODEL), row),
        out_shape=jax.ShapeDtypeStruct((S, D_MODEL), F32),
        scratch_shapes=[pltpu.VMEM((D_MIX, D_MODEL), BF16)],
        compiler_params=_params(("arbitrary",)),
        name="out_proj_ln",
    )(a, r, p, x, w_out, g1, ln_g.reshape(1, D_MODEL), ln_b.reshape(1, D_MODEL))


def _ffn_kernel(x_ref, sc_ref, sh_ref, g2_ref, w1_ref, w3_ref, w2_ref, lg_ref, lb_ref, *rest):
    n_cast = (len(rest) - 1) // 2
    o_ref = rest[n_cast]
    for src, dst in zip(rest[:n_cast], rest[n_cast + 1:]):
        dst[...] = src[...].astype(BF16)
    x = x_ref[...]
    h = (x * (1.0 + sc_ref[...]) + sh_ref[...]).astype(BF16)
    acc = jnp.zeros(x.shape, F32)
    step = 4 * MXU_DIM
    for lo in range(0, FFN_DIM, step):
        hi = min(lo + step, FFN_DIM)
        a = jnp.dot(h, w1_ref[:, lo:hi], preferred_element_type=F32)
        b = jnp.dot(h, w3_ref[:, lo:hi], preferred_element_type=F32)
        acc += jnp.dot((_silu(a) * b).astype(BF16), w2_ref[lo:hi, :], preferred_element_type=F32)
    y = ALPHA * x + (1.0 + g2_ref[...]) * acc
    o_ref[...] = _layer_norm(y, lg_ref[...], lb_ref[...])


def _dense_ffn(x, sc, sh, g2, w1, w3, w2, ln_g, ln_b, to_cast=()):
    S = x.shape[0]
    tm = FFN_CAST_TILE if to_cast else ROW_TILE
    steps = S // tm
    row = lambda i: (i, 0)
    fixed = lambda i: (0, 0)
    vec = pl.BlockSpec((1, D_MODEL), fixed)
    once = pl.Buffered(1)
    slabs = [a.reshape(-1, a.shape[-1]) for a in to_cast]
    slab_specs = [pl.BlockSpec((a.shape[0] // steps, a.shape[1]), row) for a in slabs]
    for a in slabs:
        assert a.shape[0] % (steps * 16) == 0
    outs = pl.pallas_call(
        _ffn_kernel,
        grid=(steps,),
        in_specs=[
            pl.BlockSpec((tm, D_MODEL), row),
            vec, vec, vec,
            pl.BlockSpec((D_MODEL, FFN_DIM), fixed, pipeline_mode=once),
            pl.BlockSpec((D_MODEL, FFN_DIM), fixed, pipeline_mode=once),
            pl.BlockSpec((FFN_DIM, D_MODEL), fixed, pipeline_mode=once),
            vec, vec,
        ] + slab_specs,
        out_specs=[pl.BlockSpec((tm, D_MODEL), row)] + slab_specs,
        out_shape=[jax.ShapeDtypeStruct((S, D_MODEL), F32)]
        + [jax.ShapeDtypeStruct(a.shape, BF16) for a in slabs],
        compiler_params=_params(("parallel",)),
        name="dense_ffn_ln",
    )(x, sc, sh, g2, w1, w3, w2, ln_g.reshape(1, D_MODEL), ln_b.reshape(1, D_MODEL), *slabs)
    return outs[0], [o.reshape(a.shape) for o, a in zip(outs[1:], to_cast)]


def _router_kernel(x_ref, sc_ref, sh_ref, rw_ref, h_ref, rank_ref, comb_ref, cum_ref):
    T = x_ref.shape[0]
    tc = MOE_CHUNK
    E = N_EXPERTS
    h = x_ref[...] * (1.0 + sc_ref[...]) + sh_ref[...]
    h_ref[...] = h.astype(BF16)
    logits = lax.dot_general(rw_ref[...], h, (((1,), (1,)), ((), ())),
                             preferred_element_type=F32, precision=HIGHEST)
    eidx = lax.broadcasted_iota(jnp.int32, (E, T), 0)
    v1 = jnp.max(logits, axis=0, keepdims=True)
    i1 = jnp.min(jnp.where(logits == v1, eidx, E), axis=0, keepdims=True)
    m1 = eidx == i1
    rest = jnp.where(m1, -jnp.inf, logits)
    v2 = jnp.max(rest, axis=0, keepdims=True)
    i2 = jnp.min(jnp.where(rest == v2, eidx, E), axis=0, keepdims=True)
    m2 = eidx == i2
    e2 = jnp.exp(v2 - v1)
    g1 = 1.0 / (1.0 + e2)
    g2 = e2 / (1.0 + e2)
    comb = jnp.where(m1, g1, 0.0) + jnp.where(m2, g2, 0.0)
    routed = jnp.logical_or(m1, m2)
    ind = routed.astype(BF16)
    r_i = lax.broadcasted_iota(jnp.int32, (tc, tc), 0)
    c_i = lax.broadcasted_iota(jnp.int32, (tc, tc), 1)
    strict = (r_i < c_i).astype(BF16)
    lane = lax.broadcasted_iota(jnp.int32, (E, LANES), 1)
    running = jnp.zeros((E, 1), F32)
    cum = jnp.zeros((E, LANES), F32)
    for ci in range(T // tc):
        sl = slice(ci * tc, (ci + 1) * tc)
        ind_c = ind[:, sl]
        rank_c = jnp.dot(ind_c, strict, preferred_element_type=F32) + running
        rank_ref[:, ci] = jnp.where(routed[:, sl], rank_c, -1.0).reshape(E, 1, tc)
        comb_ref[:, ci] = comb[:, sl].reshape(E, 1, tc)
        cum = jnp.where(lane == ci, running, cum)
        running = running + jnp.sum(ind_c.astype(F32), axis=1, keepdims=True)
    cum = jnp.where(lane == T // tc, running, cum)
    cum_ref[0] = cum.astype(jnp.int32)


def _router(x, sc, sh, router_w):
    S = x.shape[0]
    T = MOE_BLOCK
    E = N_EXPERTS
    fixed = lambda i: (0, 0)
    vec = pl.BlockSpec((1, D_MODEL), fixed)
    return pl.pallas_call(
        _router_kernel,
        grid=(S // T,),
        in_specs=[
            pl.BlockSpec((T, D_MODEL), lambda i: (i, 0)),
            vec, vec,
            pl.BlockSpec((E, D_MODEL), fixed),
        ],
        out_specs=[
            pl.BlockSpec((T, D_MODEL), lambda i: (i, 0)),
            pl.BlockSpec((E, T // MOE_CHUNK, 1, MOE_CHUNK), lambda i: (0, i, 0, 0)),
            pl.BlockSpec((E, T // MOE_CHUNK, 1, MOE_CHUNK), lambda i: (0, i, 0, 0)),
            pl.BlockSpec((1, E, LANES), lambda i: (i, 0, 0)),
        ],
        out_shape=[
            jax.ShapeDtypeStruct((S, D_MODEL), BF16),
            jax.ShapeDtypeStruct((E, S // MOE_CHUNK, 1, MOE_CHUNK), F32),
            jax.ShapeDtypeStruct((E, S // MOE_CHUNK, 1, MOE_CHUNK), F32),
            jax.ShapeDtypeStruct((S // T, E, LANES), jnp.int32),
        ],
        compiler_params=_params(("parallel",)),
        name="moe_router",
    )(x, sc, sh, router_w.T)


def _moe_kernel(cum_ref, h_ref, rank_ref, comb_ref, w1_ref, w3_ref, w2_ref, y_ref, xe_scr, gate_scr,
                out_scr):
    T = h_ref.shape[0]
    tr, tc, win = MOE_TILE, MOE_CHUNK, MOE_WINDOW
    nchunk = T // tc
    b = pl.program_id(0)
    e = pl.program_id(1)
    f = pl.program_id(2)

    @pl.when(jnp.logical_and(e == 0, f == 0))
    def _():
        y_ref[...] = jnp.zeros(y_ref.shape, F32)

    base = (b * N_EXPERTS + e) * LANES
    count = cum_ref[base + nchunk]
    ntiles = (count + (tr - 1)) // tr

    def token_row(ref, first, n):
        blk = ref[0, pl.ds(first, n)]
        return jnp.concatenate([blk[c] for c in range(n)], axis=1)

    def tile(r, carry):
        lo = r * tr
        rows = pl.ds(pl.multiple_of(lo, 16), tr)

        def hits(first, n):
            want = (lo + lax.broadcasted_iota(jnp.int32, (tr, n * tc), 0)).astype(F32)
            return token_row(rank_ref, first, n) == want

        def token_span(first, n):
            start = first * tc
            if not isinstance(start, int):
                start = pl.multiple_of(start, tc)
            return pl.ds(start, n * tc)

        c_lo = jnp.int32(0)
        c_hi = jnp.int32(-1)
        for c in range(nchunk):
            c_lo += (cum_ref[base + c + 1] <= lo).astype(jnp.int32)
            c_hi += (cum_ref[base + c] < lo + tr).astype(jnp.int32)
        fits = c_hi - c_lo < win
        window = (jnp.minimum(c_lo, nchunk - win), win)
        block = (0, nchunk)

        def gather(first, n):
            hit = hits(first, n)
            xe_scr[rows, :] = jnp.dot(hit.astype(BF16), h_ref[token_span(first, n), :],
                                      preferred_element_type=F32).astype(BF16)
            gate_scr[rows, :] = jnp.sum(jnp.where(hit, token_row(comb_ref, first, n), 0.0),
                                        axis=1, keepdims=True)

        @pl.when(jnp.logical_and(f == 0, fits))
        def _():
            gather(*window)

        @pl.when(jnp.logical_and(f == 0, jnp.logical_not(fits)))
        def _():
            gather(*block)

        xe = xe_scr[rows, :]
        a = jnp.dot(xe, w1_ref[0], preferred_element_type=F32)
        g = jnp.dot(xe, w3_ref[0], preferred_element_type=F32)
        mid = (_silu(a) * g).astype(BF16)
        part = jnp.dot(mid, w2_ref[0], preferred_element_type=F32)

        @pl.when(f == 0)
        def _():
            out_scr[rows, :] = part

        if MOE_FSPLIT > 2:
            @pl.when(jnp.logical_and(f > 0, f < MOE_FSPLIT - 1))
            def _():
                out_scr[rows, :] += part

        @pl.when(f == MOE_FSPLIT - 1)
        def _():
            out_bf = ((out_scr[rows, :] + part) * gate_scr[rows, :]).astype(BF16)

            def scatter(first, n):
                y_ref[token_span(first, n), :] += lax.dot_general(
                    hits(first, n).astype(BF16), out_bf, (((0,), (0,)), ((), ())),
                    preferred_element_type=F32)

            @pl.when(fits)
            def _():
                scatter(*window)

            @pl.when(jnp.logical_not(fits))
            def _():
                scatter(*block)

        return carry

    lax.fori_loop(0, ntiles, tile, 0)


def _moe_ffn(h_bf, rank, comb, cum, w1, w3, w2):
    S = h_bf.shape[0]
    T = MOE_BLOCK
    fw = EXPERT_DIM // MOE_FSPLIT
    grid_spec = pltpu.PrefetchScalarGridSpec(
        num_scalar_prefetch=1,
        grid=(S // T, N_EXPERTS, MOE_FSPLIT),
        in_specs=[
            pl.BlockSpec((T, D_MODEL), lambda b, e, f, c: (b, 0), pipeline_mode=pl.Buffered(1)),
            pl.BlockSpec((1, T // MOE_CHUNK, 1, MOE_CHUNK), lambda b, e, f, c: (e, b, 0, 0)),
            pl.BlockSpec((1, T // MOE_CHUNK, 1, MOE_CHUNK), lambda b, e, f, c: (e, b, 0, 0)),
            pl.BlockSpec((1, D_MODEL, fw), lambda b, e, f, c: (e, 0, f)),
            pl.BlockSpec((1, D_MODEL, fw), lambda b, e, f, c: (e, 0, f)),
            pl.BlockSpec((1, fw, D_MODEL), lambda b, e, f, c: (e, f, 0)),
        ],
        out_specs=pl.BlockSpec((T, D_MODEL), lambda b, e, f, c: (b, 0)),
        scratch_shapes=[
            pltpu.VMEM((T + MOE_TILE, D_MODEL), BF16),
            pltpu.VMEM((T + MOE_TILE, 1), F32),
            pltpu.VMEM((T + MOE_TILE, D_MODEL), F32),
        ],
    )
    return pl.pallas_call(
        _moe_kernel,
        grid_spec=grid_spec,
        out_shape=jax.ShapeDtypeStruct((S, D_MODEL), F32),
        compiler_params=_params(("arbitrary", "arbitrary", "arbitrary")),
        name="moe_experts",
    )(cum.reshape(-1), h_bf, rank, comb, w1, w3, w2)


def _ln_kernel(x_ref, y_ref, g2_ref, lg_ref, lb_ref, o_ref):
    y = ALPHA * x_ref[...] + (1.0 + g2_ref[...]) * y_ref[...]
    o_ref[...] = _layer_norm(y, lg_ref[...], lb_ref[...])


def _residual_ln(x, y, g2, ln_g, ln_b):
    S = x.shape[0]
    tm = ROW_TILE
    row = lambda i: (i, 0)
    vec = pl.BlockSpec((1, D_MODEL), lambda i: (0, 0))
    return pl.pallas_call(
        _ln_kernel,
        grid=(S // tm,),
        in_specs=[pl.BlockSpec((tm, D_MODEL), row), pl.BlockSpec((tm, D_MODEL), row), vec, vec, vec],
        out_specs=pl.BlockSpec((tm, D_MODEL), row),
        out_shape=jax.ShapeDtypeStruct((S, D_MODEL), F32),
        compiler_params=_params(("parallel",)),
        name="residual_ln",
    )(x, y, g2, ln_g.reshape(1, D_MODEL), ln_b.reshape(1, D_MODEL))


def _rope_tables(S):
    d = DA_HEAD_DIM
    tm = ATTN_BLOCK
    inv_freq = 1.0 / (ROPE_THETA ** (jnp.arange(0, d, 2, dtype=F32) / d))
    inv_lane = inv_freq[jnp.arange(LANES) % (d // 2)][None, :]
    base = (jnp.arange(S // tm, dtype=F32) * tm)[:, None] * inv_lane
    offs = jnp.arange(tm, dtype=F32)[:, None] * inv_lane
    shape_b = (S // tm, 1, LANES)
    return jnp.cos(base).reshape(shape_b), jnp.sin(base).reshape(shape_b), jnp.cos(offs), jnp.sin(offs)


def kernel(x, c, w_ada, b_ada, w_in, lam_qk, attn_norm_g, hg_lb_logits, hg_norm_g, pool_w, pool_scale, w_out, ln1_g, ln1_b, ln2_g, ln2_b, ffn_w1, ffn_w3, ffn_w2, router_w, exp_w1, exp_w3, exp_w2):
    B, S, D = x.shape
    assert B == 1 and D == D_MODEL and S % MOE_BLOCK == 0
    xs = x.reshape(S, D)
    mod = _modulation(c, w_ada, b_ada)
    rope = _rope_tables(S)
    experts_bf = {}
    for l in range(DEPTH):
        sh1, sc1, g1, sh2, sc2, g2 = (mod[l, j] for j in range(6))
        qt, k, vt, hg, pu = _in_projection(xs, sc1, sh1, w_in, l, rope)
        a = _diff_attention(qt, k, vt, lam_qk[l], attn_norm_g[l], l)
        r = _hgrn2(hg, hg_lb_logits, hg_norm_g[l], l)
        p = _multiscale_pool(pu, pool_w[l], pool_scale[l])
        xs = _out_projection(a, r, p, xs, w_out, l, g1, ln1_g[l], ln1_b[l])
        if l % 2 == 0:
            i = l // 2
            ride = (exp_w1[i], exp_w3[i], exp_w2[i]) if l + 1 < DEPTH else ()
            xs, cast = _dense_ffn(xs, sc2, sh2, g2, ffn_w1[i].astype(BF16), ffn_w3[i].astype(BF16),
                                  ffn_w2[i].astype(BF16), ln2_g[l], ln2_b[l], ride)
            if ride:
                experts_bf[i] = cast
        else:
            i = l // 2
            h_bf, rank, comb, cum = _router(xs, sc2, sh2, router_w[i])
            w1, w3, w2 = experts_bf.get(i) or [w[i].astype(BF16) for w in (exp_w1, exp_w3, exp_w2)]
            y = _moe_ffn(h_bf, rank, comb, cum, w1, w3, w2)
            xs = _residual_ln(xs, y, g2, ln2_g[l], ln2_b[l])
    return xs.reshape(B, S, D)
```

```python
import functools
import math

import jax
import jax.numpy as jnp
from jax import lax
from jax.experimental import pallas as pl
from jax.experimental.pallas import tpu as pltpu

F32 = jnp.float32
BF16 = jnp.bfloat16
HIGHEST = lax.Precision.HIGHEST

D_MODEL = 1024
DEPTH = 2
DA_HEADS = 4
DA_WIDTH = D_MODEL // 2
DA_HEAD_DIM = DA_WIDTH // (2 * DA_HEADS)
DA_VAL_DIM = 2 * DA_HEAD_DIM
HG_HEADS = 4
HG_WIDTH = D_MODEL // 4
HG_DIM = HG_WIDTH // HG_HEADS
POOL_WINDOWS = (2, 4, 8, 16)
POOL_WIDTH = D_MODEL // 4
POOL_GROUP = POOL_WIDTH // len(POOL_WINDOWS)
D_MIX = DA_WIDTH + HG_WIDTH + POOL_WIDTH
D_IN = 3 * DA_WIDTH + 4 * HG_WIDTH + POOL_WIDTH
ROPE_THETA = 10000.0
MASK_VALUE = -1e30
TINY = 1e-30
FFN_DIM = 11 * D_MODEL // 4
N_EXPERTS = 8
EXPERT_DIM = 7 * D_MODEL // 2
ALPHA = (2 * DEPTH) ** 0.25
EPS = 1e-5
LOG2_E = math.log2(math.e)

LANES = 128
MXU_DIM = 256
VMEM_LIMIT = 60 * 1024 * 1024

ROW_TILE = 512
FFN_CAST_TILE = 256
ATTN_BLOCK = 512
ATTN_QUERY_GROUP = 256
ATTN_SUM_ROWS = 16
ATTN_VT_ROWS = DA_VAL_DIM + ATTN_SUM_ROWS
HG_BLOCK = 512
HG_CHUNK = 16
HG_UNROLL = 4
POOL_HALO = 16
MOE_BLOCK = 2048
MOE_TILE = 192
MOE_CHUNK = 256
MOE_WINDOW = 5
MOE_FSPLIT = 2


def _params(sem, vmem=VMEM_LIMIT):
    return pltpu.CompilerParams(dimension_semantics=sem, vmem_limit_bytes=vmem)


def _layer_norm(y, g, b):
    mu = jnp.mean(y, axis=-1, keepdims=True)
    d = y - mu
    var = jnp.mean(d * d, axis=-1, keepdims=True)
    return d * lax.rsqrt(var + EPS) * g + b


def _silu(x):
    return x * jax.nn.sigmoid(x)


def _head_block_mask(n, group):
    r = lax.broadcasted_iota(jnp.int32, (n, n), 0) // group
    c = lax.broadcasted_iota(jnp.int32, (n, n), 1) // group
    return r == c


def _mod_kernel(c_ref, w_ref, b_ref, o_ref):
    cond = _silu(c_ref[...])
    cond8 = jnp.broadcast_to(cond, (8, D_MODEL))
    r = jnp.dot(cond8, w_ref[0], preferred_element_type=F32, precision=HIGHEST)
    o_ref[0] = r[0:1] + b_ref[0]


def _modulation(c, w_ada, b_ada):
    n = 6 * DEPTH
    b3 = b_ada.reshape(n, 1, D_MODEL)
    out = pl.pallas_call(
        _mod_kernel,
        grid=(DEPTH, 6),
        in_specs=[
            pl.BlockSpec((1, D_MODEL), lambda l, j: (0, 0)),
            pl.BlockSpec((1, D_MODEL, D_MODEL), lambda l, j: (l, 0, j)),
            pl.BlockSpec((1, 1, D_MODEL), lambda l, j: (l * 6 + j, 0, 0)),
        ],
        out_specs=pl.BlockSpec((1, 1, D_MODEL), lambda l, j: (l * 6 + j, 0, 0)),
        out_shape=jax.ShapeDtypeStruct((n, 1, D_MODEL), F32),
        compiler_params=_params(("parallel", "parallel")),
        name="adaln_mod",
    )(c, w_ada, b3)
    return out.reshape(DEPTH, 6, 1, D_MODEL)


def _inproj_kernel(x_ref, sc_ref, sh_ref, wf_ref, cosb_ref, sinb_ref, cosr_ref, sinr_ref,
                   qt_ref, k_ref, vt_ref, hg_ref, pu_ref, w_ref, wvt_ref):
    @pl.when(pl.program_id(0) == 0)
    def _():
        w_ref[...] = wf_ref[0].astype(BF16)
        wvt_ref[...] = wf_ref[0, :, 2 * DA_WIDTH:3 * DA_WIDTH].T.astype(BF16)

    h = (x_ref[...] * (1.0 + sc_ref[...]) + sh_ref[...]).astype(BF16)
    qk = jnp.dot(h, w_ref[:, : 2 * DA_WIDTH], preferred_element_type=F32)
    cb, sb, cr, sr = cosb_ref[0], sinb_ref[0], cosr_ref[...], sinr_ref[...]
    cos128 = cb * cr - sb * sr
    sin128 = sb * cr + cb * sr
    half = DA_HEAD_DIM // 2
    lane128 = lax.broadcasted_iota(jnp.int32, cos128.shape, 1)
    sin128 = jnp.where((lane128 & half) == 0, -sin128, sin128)
    reps = 2 * DA_WIDTH // LANES
    cos = jnp.tile(cos128, (1, reps))
    sin = jnp.tile(sin128, (1, reps))
    lane = lax.broadcasted_iota(jnp.int32, qk.shape, 1)
    first_half = (lane & half) == 0
    width = 2 * DA_WIDTH
    partner = jnp.where(first_half, pltpu.roll(qk, width - half, 1), pltpu.roll(qk, half, 1))
    rot = qk * cos + partner * sin
    qt_ref[0] = (rot[:, :DA_WIDTH] * (DA_HEAD_DIM ** -0.5 * LOG2_E)).T.astype(BF16)
    k_ref[...] = rot[:, DA_WIDTH:].astype(BF16)
    o = 2 * DA_WIDTH
    vt = lax.dot_general(wvt_ref[...], h, (((1,), (1,)), ((), ())),
                         preferred_element_type=F32).astype(BF16)
    ones = jnp.ones((ATTN_SUM_ROWS, vt.shape[1]), BF16)
    pieces = []
    for hd in range(DA_HEADS):
        pieces += [vt[hd * DA_VAL_DIM:(hd + 1) * DA_VAL_DIM], ones]
    vt_ref[0] = jnp.concatenate(pieces, axis=0)
    o += DA_WIDTH
    hg_ref[...] = jnp.dot(h, w_ref[:, o:o + 4 * HG_WIDTH], preferred_element_type=F32)
    o += 4 * HG_WIDTH
    pu_ref[...] = jnp.dot(h, w_ref[:, o:o + POOL_WIDTH], preferred_element_type=F32)


def _in_projection(x, sc, sh, w_in, layer, rope):
    S = x.shape[0]
    tm = ATTN_BLOCK
    row = lambda i: (i, 0)
    fixed = lambda i: (0, 0)
    cos_b, sin_b, cos_r, sin_r = rope
    return pl.pallas_call(
        _inproj_kernel,
        grid=(S // tm,),
        in_specs=[
            pl.BlockSpec((tm, D_MODEL), row),
            pl.BlockSpec((1, D_MODEL), fixed),
            pl.BlockSpec((1, D_MODEL), fixed),
            pl.BlockSpec((1, D_MODEL, D_IN), lambda i: (layer, 0, 0), pipeline_mode=pl.Buffered(1)),
            pl.BlockSpec((1, 1, LANES), lambda i: (i, 0, 0)),
            pl.BlockSpec((1, 1, LANES), lambda i: (i, 0, 0)),
            pl.BlockSpec((tm, LANES), fixed),
            pl.BlockSpec((tm, LANES), fixed),
        ],
        out_specs=[
            pl.BlockSpec((1, DA_WIDTH, tm), lambda i: (i, 0, 0)),
            pl.BlockSpec((tm, DA_WIDTH), row),
            pl.BlockSpec((1, DA_HEADS * ATTN_VT_ROWS, tm), lambda i: (i, 0, 0)),
            pl.BlockSpec((tm, 4 * HG_WIDTH), row),
            pl.BlockSpec((tm, POOL_WIDTH), row),
        ],
        out_shape=[
            jax.ShapeDtypeStruct((S // tm, DA_WIDTH, tm), BF16),
            jax.ShapeDtypeStruct((S, DA_WIDTH), BF16),
            jax.ShapeDtypeStruct((S // tm, DA_HEADS * ATTN_VT_ROWS, tm), BF16),
            jax.ShapeDtypeStruct((S, 4 * HG_WIDTH), F32),
            jax.ShapeDtypeStruct((S, POOL_WIDTH), F32),
        ],
        scratch_shapes=[
            pltpu.VMEM((D_MODEL, D_IN), BF16),
            pltpu.VMEM((DA_WIDTH, D_MODEL), BF16),
        ],
        compiler_params=_params(("arbitrary",)),
        name="in_proj",
    )(x, sc, sh, w_in, cos_b, sin_b, cos_r, sin_r)


def _attn_kernel(lam_ref, g_ref, qt_ref, k_ref, vt_ref, o_ref, m_scr, a_scr, c_scr, acc_scr, s_scr,
                 p_scr, *, lam_init):
    bq = qt_ref.shape[2]
    bk = vt_ref.shape[2]
    cg = ATTN_QUERY_GROUP
    i = pl.program_id(1)
    qt = qt_ref[0]
    rowq = lax.broadcasted_iota(jnp.int32, qt.shape, 0)
    zero = jnp.zeros_like(qt)
    maps = (jnp.where(rowq < DA_HEAD_DIM, qt, zero), jnp.where(rowq >= DA_HEAD_DIM, qt, zero))
    groups = [(a * bq + c0, c0, maps[a][:, c0:c0 + cg]) for a in range(2) for c0 in range(0, bq, cg)]
    m_scr[...] = jnp.full(m_scr.shape, -jnp.inf, F32)
    acc_scr[...] = jnp.zeros(acc_scr.shape, F32)
    hk = bk // 2
    key_i = lax.broadcasted_iota(jnp.int32, (hk, cg), 0)
    qry_i = lax.broadcasted_iota(jnp.int32, (hk, cg), 1)
    ng = len(groups)
    last = ng - 1

    def cols_of(g):
        return slice(groups[g][0], groups[g][0] + cg)

    def ccols_of(sset, g):
        return slice(sset * 2 * bq + groups[g][0], sset * 2 * bq + groups[g][0] + cg)

    def mask_mode(key_offset, g):
        c0 = groups[g][1]
        if key_offset + hk - 1 <= c0:
            return None
        if key_offset > c0 + cg - 1:
            return "skip"
        return key_offset

    def score_stage(hb, sset, g, key_offset):
        kb = k_ref[pl.ds(pl.multiple_of(hb * hk, hk), hk), :]
        s = jnp.dot(kb, groups[g][2], preferred_element_type=F32)
        if key_offset is not None:
            s = jnp.where(key_i + key_offset <= qry_i + groups[g][1], s, MASK_VALUE)
        s_scr[sset * ng + g] = s
        c_scr[:, ccols_of(sset, g)] = jnp.max(s, axis=0, keepdims=True)

    def max_stage(sset, g):
        cols = cols_of(g)
        m_prev = m_scr[:, cols]
        m_new = jnp.maximum(m_prev, c_scr[:, ccols_of(sset, g)])
        a_scr[:, cols] = jnp.exp2(m_prev - m_new)
        m_scr[:, cols] = m_new

    def exp_stage(sset, g, vt_half, carried, defer):
        cols = cols_of(g)
        p = jnp.exp2(s_scr[sset * ng + g] - m_scr[:, cols])
        alpha = a_scr[:, cols]
        pb = p.astype(BF16)
        acc = acc_scr[:, cols]
        if carried is not None:
            acc = acc + carried
        if defer:
            acc_scr[:, cols] = alpha * acc
            p_scr[...] = pb
        else:
            acc_scr[:, cols] = alpha * acc + jnp.dot(vt_half, pb, preferred_element_type=F32)

    def iteration(t, look, diagonal):
        vt_a = vt_ref[t, :, 0:hk]
        vt_b = vt_ref[t, :, hk:bk]
        carried = jnp.dot(vt_ref[jnp.maximum(t - 1, 0), :, hk:bk], p_scr[...],
                          preferred_element_type=F32)
        if diagonal:
            b_items = [(g, mask_mode(hk, g)) for g in range(ng) if mask_mode(hk, g) != "skip"]
        else:
            b_items = [(g, None) for g in range(ng)]
        score_stage(2 * t + 1, 1, *b_items[0])
        for g in range(ng):
            if g + 1 < len(b_items):
                score_stage(2 * t + 1, 1, *b_items[g + 1])
            if g + 1 < ng:
                max_stage(0, g + 1)
            else:
                max_stage(1, b_items[0][0])
            exp_stage(0, g, vt_a, carried if g == last else None, False)
        if not diagonal:
            score_stage(2 * t + 2, 0, 0, mask_mode(0, 0) if look == "diag" else None)
        for n, (g, _) in enumerate(b_items):
            if not diagonal and g + 1 < ng:
                score_stage(2 * t + 2, 0, g + 1, mask_mode(0, g + 1) if look == "diag" else None)
            if n + 1 < len(b_items):
                max_stage(1, b_items[n + 1][0])
            elif not diagonal:
                max_stage(0, 0)
            exp_stage(1, g, vt_b, None, not diagonal and g == last)

    p_scr[...] = jnp.zeros(p_scr.shape, BF16)
    for g in range(ng):
        score_stage(0, 0, g, (0 - i) * bq)
    max_stage(0, 0)

    def body(u, carry):
        iteration(2 * u, None, False)
        iteration(2 * u + 1, None, False)
        return carry

    n_plain = jnp.maximum(i - 1, 0)
    lax.fori_loop(0, n_plain // 2, body, 0)

    @pl.when(n_plain % 2 == 1)
    def _():
        iteration(i - 2, None, False)

    @pl.when(i > 0)
    def _():
        iteration(i - 1, "diag", False)

    iteration(i, None, True)

    lq = lam_ref[...]
    lam = (jnp.exp(jnp.sum(lq[0:1] * lq[1:2], axis=1, keepdims=True))
           - jnp.exp(jnp.sum(lq[2:3] * lq[3:4], axis=1, keepdims=True)) + lam_init)
    o = acc_scr[0:DA_VAL_DIM, :] / acc_scr[DA_VAL_DIM:DA_VAL_DIM + 1, :]
    o = o[:, :bq] - lam * o[:, bq:]
    ms = jnp.mean(o * o, axis=0, keepdims=True)
    o = o * lax.rsqrt(ms + EPS) * g_ref[...] * (1.0 - lam_init)
    o_ref[...] = o.T.astype(BF16)


def _diff_attention(qt, k, vt, lam_qk, attn_g, layer):
    S = k.shape[0]
    bq = bk = ATTN_BLOCK
    nk = S // bk
    lam_init = 0.8 - 0.6 * math.exp(-0.3 * layer)
    return pl.pallas_call(
        functools.partial(_attn_kernel, lam_init=lam_init),
        grid=(DA_HEADS, S // bq),
        in_specs=[
            pl.BlockSpec((4, DA_HEAD_DIM), lambda h, i: (0, 0)),
            pl.BlockSpec((DA_VAL_DIM, 1), lambda h, i: (0, 0)),
            pl.BlockSpec((1, LANES, bq), lambda h, i: (i, h, 0)),
            pl.BlockSpec((S, LANES), lambda h, i: (0, h)),
            pl.BlockSpec((nk, ATTN_VT_ROWS, bk), lambda h, i: (0, h, 0)),
        ],
        out_specs=pl.BlockSpec((bq, DA_VAL_DIM), lambda h, i: (i, h)),
        out_shape=jax.ShapeDtypeStruct((S, DA_WIDTH), BF16),
        scratch_shapes=[
            pltpu.VMEM((1, 2 * bq), F32),
            pltpu.VMEM((1, 2 * bq), F32),
            pltpu.VMEM((1, 2 * 2 * bq), F32),
            pltpu.VMEM((ATTN_VT_ROWS, 2 * bq), F32),
            pltpu.VMEM((2 * (2 * bq // ATTN_QUERY_GROUP), bk // 2, ATTN_QUERY_GROUP), F32),
            pltpu.VMEM((bk // 2, ATTN_QUERY_GROUP), BF16),
        ],
        compiler_params=_params(("parallel", "parallel")),
        name="diff_attn",
    )(lam_qk, attn_g.reshape(DA_VAL_DIM, 1), qt, k, vt)


def _hgrn_kernel(lbl_ref, g_ref, hg_ref, o_ref, st_scr, stb_scr, o_scr, *, layer):
    T = hg_ref.shape[0]
    c = HG_CHUNK
    W = HG_WIDTH

    @pl.when(pl.program_id(0) == 0)
    def _():
        st_scr[...] = jnp.zeros(st_scr.shape, F32)
        stb_scr[...] = jnp.zeros(stb_scr.shape, BF16)

    lg = lbl_ref[...]
    ex = jnp.exp(lg - jnp.max(lg, axis=0, keepdims=True))
    prob = ex / jnp.sum(ex, axis=0, keepdims=True)
    lb = jnp.zeros((1, W), F32)
    for li in range(1, layer + 1):
        lb = lb + prob[li:li + 1]

    same_head = _head_block_mask(W, HG_DIM)
    ones_bd = same_head.astype(BF16)
    rowi = lax.broadcasted_iota(jnp.int32, (c, W), 0)

    def body(s, carry):
        sl = pl.ds(pl.multiple_of(s * c, c), c)
        qf = hg_ref[sl, 0:W]
        z = hg_ref[sl, W:2 * W]
        iv = hg_ref[sl, 2 * W:3 * W]
        f = lb + (1.0 - lb) * jax.nn.sigmoid(z)
        logf = jnp.log(jnp.maximum(f, TINY))
        key = (1.0 - lb) * jax.nn.sigmoid(-z)
        val = _silu(iv)
        b = logf
        sft = 1
        while sft < c:
            b = b + jnp.where(rowi >= sft, pltpu.roll(b, sft, 0), 0.0)
            sft *= 2
        b_last = b[c - 1:c]
        kd = (key * jnp.exp(b_last - b)).astype(BF16)
        upd = lax.dot_general(val.astype(BF16), kd, (((0,), (0,)), ((), ())),
                              preferred_element_type=F32)
        es = [(qf * key).astype(BF16)]
        vrs = [val]
        for d in range(1, c):
            kr = pltpu.roll(key, d, 0)
            br = pltpu.roll(b, d, 0)
            e = jnp.where(rowi >= d, qf * kr * jnp.exp(b - br), 0.0)
            es.append(e.astype(BF16))
            vrs.append(pltpu.roll(val, d, 0))
        estack = jnp.concatenate(es, axis=0)
        r = jnp.dot(estack, ones_bd, preferred_element_type=F32)
        o = r[0:c] * vrs[0]
        for d in range(1, c):
            o = o + r[d * c:(d + 1) * c] * vrs[d]
        qd = (qf * jnp.exp(b)).astype(BF16)
        o = o + lax.dot_general(qd, stb_scr[...], (((1,), (1,)), ((), ())),
                                preferred_element_type=F32)
        o_scr[sl, :] = o
        decay = jnp.exp(b_last)
        for hd in range(HG_HEADS):
            rws = slice(hd * HG_DIM, (hd + 1) * HG_DIM)
            lns = slice(hd * HG_DIM // LANES * LANES, (hd * HG_DIM // LANES + 1) * LANES)
            new = decay[:, lns] * st_scr[rws, lns] + jnp.where(same_head[rws, lns], upd[rws, lns], 0.0)
            st_scr[rws, lns] = new
            stb_scr[rws, lns] = new.astype(BF16)
        return carry

    lax.fori_loop(0, T // c, body, 0, unroll=HG_UNROLL)

    o = o_scr[...]
    ms = jnp.dot(o * o, same_head.astype(F32), preferred_element_type=F32,
                 precision=HIGHEST) * (1.0 / HG_DIM)
    gate = hg_ref[:, 3 * W:4 * W]
    o_ref[...] = (o * lax.rsqrt(ms + EPS) * g_ref[...] * _silu(gate)).astype(BF16)


def _hgrn2(hg, lb_logits, norm_g, layer):
    S = hg.shape[0]
    T = HG_BLOCK
    g_t = jnp.tile(norm_g.reshape(1, HG_DIM), (1, HG_HEADS))
    return pl.pallas_call(
        functools.partial(_hgrn_kernel, layer=layer),
        grid=(S // T,),
        in_specs=[
            pl.BlockSpec((DEPTH, HG_WIDTH), lambda i: (0, 0)),
            pl.BlockSpec((1, HG_WIDTH), lambda i: (0, 0)),
            pl.BlockSpec((T, 4 * HG_WIDTH), lambda i: (i, 0)),
        ],
        out_specs=pl.BlockSpec((T, HG_WIDTH), lambda i: (i, 0)),
        out_shape=jax.ShapeDtypeStruct((S, HG_WIDTH), BF16),
        scratch_shapes=[
            pltpu.VMEM((HG_WIDTH, HG_WIDTH), F32),
            pltpu.VMEM((HG_WIDTH, HG_WIDTH), BF16),
            pltpu.VMEM((T, HG_WIDTH), F32),
        ],
        compiler_params=_params(("arbitrary",)),
        name="hgrn2",
    )(lb_logits, g_t, hg)


def _pool_kernel(u_ref, w_ref, sc_ref, o_ref, halo_scr):
    T = u_ref.shape[0]
    H = POOL_HALO
    blk = pl.program_id(0)

    @pl.when(blk == 0)
    def _():
        halo_scr[...] = jnp.zeros(halo_scr.shape, F32)

    u = u_ref[...]
    ext = jnp.concatenate([halo_scr[...], u], axis=0)
    halo_scr[...] = u[T - H:, :]
    sums = []
    s = ext
    w = 1
    while w < POOL_WINDOWS[-1]:
        s = s + pltpu.roll(s, w, 0)
        w *= 2
        sums.append(s[H:, :])
    t1 = (blk * T + 1 + lax.broadcasted_iota(jnp.int32, (T, POOL_WIDTH), 0)).astype(F32)
    lane = lax.broadcasted_iota(jnp.int32, (T, POOL_WIDTH), 1)
    pooled = None
    for gi, win in enumerate(POOL_WINDOWS):
        mean = sums[gi] / jnp.minimum(t1, float(win))
        pooled = mean if pooled is None else jnp.where(lane >= gi * POOL_GROUP, mean, pooled)
    pooled = pooled - u
    y = jnp.dot(pooled.astype(BF16), w_ref[...], preferred_element_type=F32)
    o_ref[...] = (y * sc_ref[...]).astype(BF16)


def _multiscale_pool(pu, pool_w, pool_scale):
    S = pu.shape[0]
    T = ROW_TILE
    w_bd = jax.scipy.linalg.block_diag(*[pool_w[g] for g in range(len(POOL_WINDOWS))]).astype(BF16)
    return pl.pallas_call(
        _pool_kernel,
        grid=(S // T,),
        in_specs=[
            pl.BlockSpec((T, POOL_WIDTH), lambda i: (i, 0)),
            pl.BlockSpec((POOL_WIDTH, POOL_WIDTH), lambda i: (0, 0)),
            pl.BlockSpec((1, POOL_WIDTH), lambda i: (0, 0)),
        ],
        out_specs=pl.BlockSpec((T, POOL_WIDTH), lambda i: (i, 0)),
        out_shape=jax.ShapeDtypeStruct((S, POOL_WIDTH), BF16),
        scratch_shapes=[pltpu.VMEM((POOL_HALO, POOL_WIDTH), F32)],
        compiler_params=_params(("arbitrary",)),
        name="ms_pool",
    )(pu, w_bd, pool_scale.reshape(1, POOL_WIDTH))


def _outproj_kernel(a_ref, r_ref, p_ref, x_ref, wf_ref, g1_ref, lg_ref, lb_ref, o_ref, w_ref):
    @pl.when(pl.program_id(0) == 0)
    def _():
        w_ref[...] = wf_ref[0].astype(BF16)

    m = jnp.dot(a_ref[...], w_ref[0:DA_WIDTH, :], preferred_element_type=F32)
    m += jnp.dot(r_ref[...], w_ref[DA_WIDTH:DA_WIDTH + HG_WIDTH, :], preferred_element_type=F32)
    m += jnp.dot(p_ref[...], w_ref[DA_WIDTH + HG_WIDTH:, :], preferred_element_type=F32)
    y = ALPHA * x_ref[...] + (1.0 + g1_ref[...]) * m
    o_ref[...] = _layer_norm(y, lg_ref[...], lb_ref[...])


def _out_projection(a, r, p, x, w_out, layer, g1, ln_g, ln_b):
    S = x.shape[0]
    tm = ROW_TILE
    row = lambda i: (i, 0)
    fixed = lambda i: (0, 0)
    vec = pl.BlockSpec((1, D_MODEL), fixed)
    return pl.pallas_call(
        _outproj_kernel,
        grid=(S // tm,),
        in_specs=[
            pl.BlockSpec((tm, DA_WIDTH), row),
            pl.BlockSpec((tm, HG_WIDTH), row),
            pl.BlockSpec((tm, POOL_WIDTH), row),
            pl.BlockSpec((tm, D_MODEL), row),
            pl.BlockSpec((1, D_MIX, D_MODEL), lambda i: (layer, 0, 0), pipeline_mode=pl.Buffered(1)),
            vec, vec, vec,
        ],
        out_specs=pl.BlockSpec((tm, D_MODEL), row),
        out_shape=jax.ShapeDtypeStruct((S, D_MODEL), F32),
        scratch_shapes=[pltpu.VMEM((D_MIX, D_MODEL), BF16)],
        compiler_params=_params(("arbitrary",)),
        name="out_proj_ln",
    )(a, r, p, x, w_out, g1, ln_g.reshape(1, D_MODEL), ln_b.reshape(1, D_MODEL))


def _ffn_kernel(x_ref, sc_ref, sh_ref, g2_ref, w1_ref, w3_ref, w2_ref, lg_ref, lb_ref, *rest):
    n_cast = (len(rest) - 1) // 2
    o_ref = rest[n_cast]
    for src, dst in zip(rest[:n_cast], rest[n_cast + 1:]):
        dst[...] = src[...].astype(BF16)
    x = x_ref[...]
    h = (x * (1.0 + sc_ref[...]) + sh_ref[...]).astype(BF16)
    acc = jnp.zeros(x.shape, F32)
    step = 4 * MXU_DIM
    for lo in range(0, FFN_DIM, step):
        hi = min(lo + step, FFN_DIM)
        a = jnp.dot(h, w1_ref[:, lo:hi], preferred_element_type=F32)
        b = jnp.dot(h, w3_ref[:, lo:hi], preferred_element_type=F32)
        acc += jnp.dot((_silu(a) * b).astype(BF16), w2_ref[lo:hi, :], preferred_element_type=F32)
    y = ALPHA * x + (1.0 + g2_ref[...]) * acc
    o_ref[...] = _layer_norm(y, lg_ref[...], lb_ref[...])


def _dense_ffn(x, sc, sh, g2, w1, w3, w2, ln_g, ln_b, to_cast=()):
    S = x.shape[0]
    tm = FFN_CAST_TILE if to_cast else ROW_TILE
    steps = S // tm
    row = lambda i: (i, 0)
    fixed = lambda i: (0, 0)
    vec = pl.BlockSpec((1, D_MODEL), fixed)
    once = pl.Buffered(1)
    slabs = [a.reshape(-1, a.shape[-1]) for a in to_cast]
    slab_specs = [pl.BlockSpec((a.shape[0] // steps, a.shape[1]), row) for a in slabs]
    for a in slabs:
        assert a.shape[0] % (steps * 16) == 0
    outs = pl.pallas_call(
        _ffn_kernel,
        grid=(steps,),
        in_specs=[
            pl.BlockSpec((tm, D_MODEL), row),
            vec, vec, vec,
            pl.BlockSpec((D_MODEL, FFN_DIM), fixed, pipeline_mode=once),
            pl.BlockSpec((D_MODEL, FFN_DIM), fixed, pipeline_mode=once),
            pl.BlockSpec((FFN_DIM, D_MODEL), fixed, pipeline_mode=once),
            vec, vec,
        ] + slab_specs,
        out_specs=[pl.BlockSpec((tm, D_MODEL), row)] + slab_specs,
        out_shape=[jax.ShapeDtypeStruct((S, D_MODEL), F32)]
        + [jax.ShapeDtypeStruct(a.shape, BF16) for a in slabs],
        compiler_params=_params(("parallel",)),
        name="dense_ffn_ln",
    )(x, sc, sh, g2, w1, w3, w2, ln_g.reshape(1, D_MODEL), ln_b.reshape(1, D_MODEL), *slabs)
    return outs[0], [o.reshape(a.shape) for o, a in zip(outs[1:], to_cast)]


def _router_kernel(x_ref, sc_ref, sh_ref, rw_ref, h_ref, rank_ref, comb_ref, cum_ref):
    T = x_ref.shape[0]
    tc = MOE_CHUNK
    E = N_EXPERTS
    h = x_ref[...] * (1.0 + sc_ref[...]) + sh_ref[...]
    h_ref[...] = h.astype(BF16)
    logits = lax.dot_general(rw_ref[...], h, (((1,), (1,)), ((), ())),
                             preferred_element_type=F32, precision=HIGHEST)
    eidx = lax.broadcasted_iota(jnp.int32, (E, T), 0)
    v1 = jnp.max(logits, axis=0, keepdims=True)
    i1 = jnp.min(jnp.where(logits == v1, eidx, E), axis=0, keepdims=True)
    m1 = eidx == i1
    rest = jnp.where(m1, -jnp.inf, logits)
    v2 = jnp.max(rest, axis=0, keepdims=True)
    i2 = jnp.min(jnp.where(rest == v2, eidx, E), axis=0, keepdims=True)
    m2 = eidx == i2
    e2 = jnp.exp(v2 - v1)
    g1 = 1.0 / (1.0 + e2)
    g2 = e2 / (1.0 + e2)
    comb = jnp.where(m1, g1, 0.0) + jnp.where(m2, g2, 0.0)
    routed = jnp.logical_or(m1, m2)
    ind = routed.astype(BF16)
    r_i = lax.broadcasted_iota(jnp.int32, (tc, tc), 0)
    c_i = lax.broadcasted_iota(jnp.int32, (tc, tc), 1)
    strict = (r_i < c_i).astype(BF16)
    lane = lax.broadcasted_iota(jnp.int32, (E, LANES), 1)
    running = jnp.zeros((E, 1), F32)
    cum = jnp.zeros((E, LANES), F32)
    for ci in range(T // tc):
        sl = slice(ci * tc, (ci + 1) * tc)
        ind_c = ind[:, sl]
        rank_c = jnp.dot(ind_c, strict, preferred_element_type=F32) + running
        rank_ref[:, ci] = jnp.where(routed[:, sl], rank_c, -1.0).reshape(E, 1, tc)
        comb_ref[:, ci] = comb[:, sl].reshape(E, 1, tc)
        cum = jnp.where(lane == ci, running, cum)
        running = running + jnp.sum(ind_c.astype(F32), axis=1, keepdims=True)
    cum = jnp.where(lane == T // tc, running, cum)
    cum_ref[0] = cum.astype(jnp.int32)


def _router(x, sc, sh, router_w):
    S = x.shape[0]
    T = MOE_BLOCK
    E = N_EXPERTS
    fixed = lambda i: (0, 0)
    vec = pl.BlockSpec((1, D_MODEL), fixed)
    return pl.pallas_call(
        _router_kernel,
        grid=(S // T,),
        in_specs=[
            pl.BlockSpec((T, D_MODEL), lambda i: (i, 0)),
            vec, vec,
            pl.BlockSpec((E, D_MODEL), fixed),
        ],
        out_specs=[
            pl.BlockSpec((T, D_MODEL), lambda i: (i, 0)),
            pl.BlockSpec((E, T // MOE_CHUNK, 1, MOE_CHUNK), lambda i: (0, i, 0, 0)),
            pl.BlockSpec((E, T // MOE_CHUNK, 1, MOE_CHUNK), lambda i: (0, i, 0, 0)),
            pl.BlockSpec((1, E, LANES), lambda i: (i, 0, 0)),
        ],
        out_shape=[
            jax.ShapeDtypeStruct((S, D_MODEL), BF16),
            jax.ShapeDtypeStruct((E, S // MOE_CHUNK, 1, MOE_CHUNK), F32),
            jax.ShapeDtypeStruct((E, S // MOE_CHUNK, 1, MOE_CHUNK), F32),
            jax.ShapeDtypeStruct((S // T, E, LANES), jnp.int32),
        ],
        compiler_params=_params(("parallel",)),
        name="moe_router",
    )(x, sc, sh, router_w.T)


def _moe_kernel(cum_ref, h_ref, rank_ref, comb_ref, w1_ref, w3_ref, w2_ref, y_ref, xe_scr, gate_scr,
                out_scr):
    T = h_ref.shape[0]
    tr, tc, win = MOE_TILE, MOE_CHUNK, MOE_WINDOW
    nchunk = T // tc
    b = pl.program_id(0)
    e = pl.program_id(1)
    f = pl.program_id(2)

    @pl.when(jnp.logical_and(e == 0, f == 0))
    def _():
        y_ref[...] = jnp.zeros(y_ref.shape, F32)

    base = (b * N_EXPERTS + e) * LANES
    count = cum_ref[base + nchunk]
    ntiles = (count + (tr - 1)) // tr

    def token_row(ref, first, n):
        blk = ref[0, pl.ds(first, n)]
        return jnp.concatenate([blk[c] for c in range(n)], axis=1)

    def tile(r, carry):
        lo = r * tr
        rows = pl.ds(pl.multiple_of(lo, 16), tr)

        def hits(first, n):
            want = (lo + lax.broadcasted_iota(jnp.int32, (tr, n * tc), 0)).astype(F32)
            return token_row(rank_ref, first, n) == want

        def token_span(first, n):
            start = first * tc
            if not isinstance(start, int):
                start = pl.multiple_of(start, tc)
            return pl.ds(start, n * tc)

        c_lo = jnp.int32(0)
        c_hi = jnp.int32(-1)
        for c in range(nchunk):
            c_lo += (cum_ref[base + c + 1] <= lo).astype(jnp.int32)
            c_hi += (cum_ref[base + c] < lo + tr).astype(jnp.int32)
        fits = c_hi - c_lo < win
        window = (jnp.minimum(c_lo, nchunk - win), win)
        block = (0, nchunk)

        def gather(first, n):
            hit = hits(first, n)
            xe_scr[rows, :] = jnp.dot(hit.astype(BF16), h_ref[token_span(first, n), :],
                                      preferred_element_type=F32).astype(BF16)
            gate_scr[rows, :] = jnp.sum(jnp.where(hit, token_row(comb_ref, first, n), 0.0),
                                        axis=1, keepdims=True)

        @pl.when(jnp.logical_and(f == 0, fits))
        def _():
            gather(*window)

        @pl.when(jnp.logical_and(f == 0, jnp.logical_not(fits)))
        def _():
            gather(*block)

        xe = xe_scr[rows, :]
        a = jnp.dot(xe, w1_ref[0], preferred_element_type=F32)
        g = jnp.dot(xe, w3_ref[0], preferred_element_type=F32)
        mid = (_silu(a) * g).astype(BF16)
        part = jnp.dot(mid, w2_ref[0], preferred_element_type=F32)

        @pl.when(f == 0)
        def _():
            out_scr[rows, :] = part

        if MOE_FSPLIT > 2:
            @pl.when(jnp.logical_and(f > 0, f < MOE_FSPLIT - 1))
            def _():
                out_scr[rows, :] += part

        @pl.when(f == MOE_FSPLIT - 1)
        def _():
            out_bf = ((out_scr[rows, :] + part) * gate_scr[rows, :]).astype(BF16)

            def scatter(first, n):
                y_ref[token_span(first, n), :] += lax.dot_general(
                    hits(first, n).astype(BF16), out_bf, (((0,), (0,)), ((), ())),
                    preferred_element_type=F32)

            @pl.when(fits)
            def _():
                scatter(*window)

            @pl.when(jnp.logical_not(fits))
            def _():
                scatter(*block)

        return carry

    lax.fori_loop(0, ntiles, tile, 0)


def _moe_ffn(h_bf, rank, comb, cum, w1, w3, w2):
    S = h_bf.shape[0]
    T = MOE_BLOCK
    fw = EXPERT_DIM // MOE_FSPLIT
    grid_spec = pltpu.PrefetchScalarGridSpec(
        num_scalar_prefetch=1,
        grid=(S // T, N_EXPERTS, MOE_FSPLIT),
        in_specs=[
            pl.BlockSpec((T, D_MODEL), lambda b, e, f, c: (b, 0), pipeline_mode=pl.Buffered(1)),
            pl.BlockSpec((1, T // MOE_CHUNK, 1, MOE_CHUNK), lambda b, e, f, c: (e, b, 0, 0)),
            pl.BlockSpec((1, T // MOE_CHUNK, 1, MOE_CHUNK), lambda b, e, f, c: (e, b, 0, 0)),
            pl.BlockSpec((1, D_MODEL, fw), lambda b, e, f, c: (e, 0, f)),
            pl.BlockSpec((1, D_MODEL, fw), lambda b, e, f, c: (e, 0, f)),
            pl.BlockSpec((1, fw, D_MODEL), lambda b, e, f, c: (e, f, 0)),
        ],
        out_specs=pl.BlockSpec((T, D_MODEL), lambda b, e, f, c: (b, 0)),
        scratch_shapes=[
            pltpu.VMEM((T + MOE_TILE, D_MODEL), BF16),
            pltpu.VMEM((T + MOE_TILE, 1), F32),
            pltpu.VMEM((T + MOE_TILE, D_MODEL), F32),
        ],
    )
    return pl.pallas_call(
        _moe_kernel,
        grid_spec=grid_spec,
        out_shape=jax.ShapeDtypeStruct((S, D_MODEL), F32),
        compiler_params=_params(("arbitrary", "arbitrary", "arbitrary")),
        name="moe_experts",
    )(cum.reshape(-1), h_bf, rank, comb, w1, w3, w2)


def _ln_kernel(x_ref, y_ref, g2_ref, lg_ref, lb_ref, o_ref):
    y = ALPHA * x_ref[...] + (1.0 + g2_ref[...]) * y_ref[...]
    o_ref[...] = _layer_norm(y, lg_ref[...], lb_ref[...])


def _residual_ln(x, y, g2, ln_g, ln_b):
    S = x.shape[0]
    tm = ROW_TILE
    row = lambda i: (i, 0)
    vec = pl.BlockSpec((1, D_MODEL), lambda i: (0, 0))
    return pl.pallas_call(
        _ln_kernel,
        grid=(S // tm,),
        in_specs=[pl.BlockSpec((tm, D_MODEL), row), pl.BlockSpec((tm, D_MODEL), row), vec, vec, vec],
        out_specs=pl.BlockSpec((tm, D_MODEL), row),
        out_shape=jax.ShapeDtypeStruct((S, D_MODEL), F32),
        compiler_params=_params(("parallel",)),
        name="residual_ln",
    )(x, y, g2, ln_g.reshape(1, D_MODEL), ln_b.reshape(1, D_MODEL))


def _rope_tables(S):
    d = DA_HEAD_DIM
    tm = ATTN_BLOCK
    inv_freq = 1.0 / (ROPE_THETA ** (jnp.arange(0, d, 2, dtype=F32) / d))
    inv_lane = inv_freq[jnp.arange(LANES) % (d // 2)][None, :]
    base = (jnp.arange(S // tm, dtype=F32) * tm)[:, None] * inv_lane
    offs = jnp.arange(tm, dtype=F32)[:, None] * inv_lane
    shape_b = (S // tm, 1, LANES)
    return jnp.cos(base).reshape(shape_b), jnp.sin(base).reshape(shape_b), jnp.cos(offs), jnp.sin(offs)


def kernel(x, c, w_ada, b_ada, w_in, lam_qk, attn_norm_g, hg_lb_logits, hg_norm_g, pool_w, pool_scale, w_out, ln1_g, ln1_b, ln2_g, ln2_b, ffn_w1, ffn_w3, ffn_w2, router_w, exp_w1, exp_w3, exp_w2):
    B, S, D = x.shape
    assert B == 1 and D == D_MODEL and S % MOE_BLOCK == 0
    xs = x.reshape(S, D)
    mod = _modulation(c, w_ada, b_ada)
    rope = _rope_tables(S)
    experts_bf = {}
    for l in range(DEPTH):
        sh1, sc1, g1, sh2, sc2, g2 = (mod[l, j] for j in range(6))
        qt, k, vt, hg, pu = _in_projection(xs, sc1, sh1, w_in, l, rope)
        a = _diff_attention(qt, k, vt, lam_qk[l], attn_norm_g[l], l)
        r = _hgrn2(hg, hg_lb_logits, hg_norm_g[l], l)
        p = _multiscale_pool(pu, pool_w[l], pool_scale[l])
        xs = _out_projection(a, r, p, xs, w_out, l, g1, ln1_g[l], ln1_b[l])
        if l % 2 == 0:
            i = l // 2
            ride = (exp_w1[i], exp_w3[i], exp_w2[i]) if l + 1 < DEPTH else ()
            xs, cast = _dense_ffn(xs, sc2, sh2, g2, ffn_w1[i].astype(BF16), ffn_w3[i].astype(BF16),
                                  ffn_w2[i].astype(BF16), ln2_g[l], ln2_b[l], ride)
            if ride:
                experts_bf[i] = cast
        else:
            i = l // 2
            h_bf, rank, comb, cum = _router(xs, sc2, sh2, router_w[i])
            w1, w3, w2 = experts_bf.get(i) or [w[i].astype(BF16) for w in (exp_w1, exp_w3, exp_w2)]
            y = _moe_ffn(h_bf, rank, comb, cum, w1, w3, w2)
            xs = _residual_ln(xs, y, g2, ln2_g[l], ln2_b[l])
    return xs.reshape(B, S, D)
```

```python
import functools
import math

import jax
import jax.numpy as jnp
from jax import lax
from jax.experimental import pallas as pl
from jax.experimental.pallas import tpu as pltpu

F32 = jnp.float32
BF16 = jnp.bfloat16
HIGHEST = lax.Precision.HIGHEST

D_MODEL = 1024
DEPTH = 2
DA_HEADS = 4
DA_WIDTH = D_MODEL // 2
DA_HEAD_DIM = DA_WIDTH // (2 * DA_HEADS)
DA_VAL_DIM = 2 * DA_HEAD_DIM
HG_HEADS = 4
HG_WIDTH = D_MODEL // 4
HG_DIM = HG_WIDTH // HG_HEADS
POOL_WINDOWS = (2, 4, 8, 16)
POOL_WIDTH = D_MODEL // 4
POOL_GROUP = POOL_WIDTH // len(POOL_WINDOWS)
D_MIX = DA_WIDTH + HG_WIDTH + POOL_WIDTH
D_IN = 3 * DA_WIDTH + 4 * HG_WIDTH + POOL_WIDTH
ROPE_THETA = 10000.0
MASK_VALUE = -1e30
TINY = 1e-30
FFN_DIM = 11 * D_MODEL // 4
N_EXPERTS = 8
EXPERT_DIM = 7 * D_MODEL // 2
ALPHA = (2 * DEPTH) ** 0.25
EPS = 1e-5
LOG2_E = math.log2(math.e)

LANES = 128
MXU_DIM = 256
VMEM_LIMIT = 60 * 1024 * 1024

ROW_TILE = 512
FFN_CAST_TILE = 256
ATTN_BLOCK = 512
ATTN_QUERY_GROUP = 256
ATTN_UNROLL = 4
ATTN_SUM_ROWS = 16
ATTN_VT_ROWS = DA_VAL_DIM + ATTN_SUM_ROWS
HG_BLOCK = 512
HG_CHUNK = 16
HG_UNROLL = 4
POOL_HALO = 16
MOE_BLOCK = 2048
MOE_TILE = 192
MOE_CHUNK = 256
MOE_WINDOW = 5
MOE_FSPLIT = 2


def _params(sem, vmem=VMEM_LIMIT):
    return pltpu.CompilerParams(dimension_semantics=sem, vmem_limit_bytes=vmem)


def _layer_norm(y, g, b):
    mu = jnp.mean(y, axis=-1, keepdims=True)
    d = y - mu
    var = jnp.mean(d * d, axis=-1, keepdims=True)
    return d * lax.rsqrt(var + EPS) * g + b


def _silu(x):
    return x * jax.nn.sigmoid(x)


def _head_block_mask(n, group):
    r = lax.broadcasted_iota(jnp.int32, (n, n), 0) // group
    c = lax.broadcasted_iota(jnp.int32, (n, n), 1) // group
    return r == c


def _mod_kernel(c_ref, w_ref, b_ref, o_ref):
    cond = _silu(c_ref[...])
    cond8 = jnp.broadcast_to(cond, (8, D_MODEL))
    r = jnp.dot(cond8, w_ref[0], preferred_element_type=F32, precision=HIGHEST)
    o_ref[0] = r[0:1] + b_ref[0]


def _modulation(c, w_ada, b_ada):
    n = 6 * DEPTH
    b3 = b_ada.reshape(n, 1, D_MODEL)
    out = pl.pallas_call(
        _mod_kernel,
        grid=(DEPTH, 6),
        in_specs=[
            pl.BlockSpec((1, D_MODEL), lambda l, j: (0, 0)),
            pl.BlockSpec((1, D_MODEL, D_MODEL), lambda l, j: (l, 0, j)),
            pl.BlockSpec((1, 1, D_MODEL), lambda l, j: (l * 6 + j, 0, 0)),
        ],
        out_specs=pl.BlockSpec((1, 1, D_MODEL), lambda l, j: (l * 6 + j, 0, 0)),
        out_shape=jax.ShapeDtypeStruct((n, 1, D_MODEL), F32),
        compiler_params=_params(("parallel", "parallel")),
        name="adaln_mod",
    )(c, w_ada, b3)
    return out.reshape(DEPTH, 6, 1, D_MODEL)


def _inproj_kernel(x_ref, sc_ref, sh_ref, wf_ref, cosb_ref, sinb_ref, cosr_ref, sinr_ref,
                   qt_ref, k_ref, vt_ref, hg_ref, pu_ref, w_ref, wvt_ref):
    @pl.when(pl.program_id(0) == 0)
    def _():
        w_ref[...] = wf_ref[0].astype(BF16)
        wvt_ref[...] = wf_ref[0, :, 2 * DA_WIDTH:3 * DA_WIDTH].T.astype(BF16)

    h = (x_ref[...] * (1.0 + sc_ref[...]) + sh_ref[...]).astype(BF16)
    qk = jnp.dot(h, w_ref[:, : 2 * DA_WIDTH], preferred_element_type=F32)
    cb, sb, cr, sr = cosb_ref[0], sinb_ref[0], cosr_ref[...], sinr_ref[...]
    cos128 = cb * cr - sb * sr
    sin128 = sb * cr + cb * sr
    half = DA_HEAD_DIM // 2
    lane128 = lax.broadcasted_iota(jnp.int32, cos128.shape, 1)
    sin128 = jnp.where((lane128 & half) == 0, -sin128, sin128)
    reps = 2 * DA_WIDTH // LANES
    cos = jnp.tile(cos128, (1, reps))
    sin = jnp.tile(sin128, (1, reps))
    lane = lax.broadcasted_iota(jnp.int32, qk.shape, 1)
    first_half = (lane & half) == 0
    width = 2 * DA_WIDTH
    partner = jnp.where(first_half, pltpu.roll(qk, width - half, 1), pltpu.roll(qk, half, 1))
    rot = qk * cos + partner * sin
    qt_ref[0] = (rot[:, :DA_WIDTH] * (DA_HEAD_DIM ** -0.5 * LOG2_E)).T.astype(BF16)
    k_ref[...] = rot[:, DA_WIDTH:].astype(BF16)
    o = 2 * DA_WIDTH
    vt = lax.dot_general(wvt_ref[...], h, (((1,), (1,)), ((), ())),
                         preferred_element_type=F32).astype(BF16)
    ones = jnp.ones((ATTN_SUM_ROWS, vt.shape[1]), BF16)
    pieces = []
    for hd in range(DA_HEADS):
        pieces += [vt[hd * DA_VAL_DIM:(hd + 1) * DA_VAL_DIM], ones]
    vt_ref[0] = jnp.concatenate(pieces, axis=0)
    o += DA_WIDTH
    hg_ref[...] = jnp.dot(h, w_ref[:, o:o + 4 * HG_WIDTH], preferred_element_type=F32)
    o += 4 * HG_WIDTH
    pu_ref[...] = jnp.dot(h, w_ref[:, o:o + POOL_WIDTH], preferred_element_type=F32)


def _in_projection(x, sc, sh, w_in, layer, rope):
    S = x.shape[0]
    tm = ATTN_BLOCK
    row = lambda i: (i, 0)
    fixed = lambda i: (0, 0)
    cos_b, sin_b, cos_r, sin_r = rope
    return pl.pallas_call(
        _inproj_kernel,
        grid=(S // tm,),
        in_specs=[
            pl.BlockSpec((tm, D_MODEL), row),
            pl.BlockSpec((1, D_MODEL), fixed),
            pl.BlockSpec((1, D_MODEL), fixed),
            pl.BlockSpec((1, D_MODEL, D_IN), lambda i: (layer, 0, 0), pipeline_mode=pl.Buffered(1)),
            pl.BlockSpec((1, 1, LANES), lambda i: (i, 0, 0)),
            pl.BlockSpec((1, 1, LANES), lambda i: (i, 0, 0)),
            pl.BlockSpec((tm, LANES), fixed),
            pl.BlockSpec((tm, LANES), fixed),
        ],
        out_specs=[
            pl.BlockSpec((1, DA_WIDTH, tm), lambda i: (i, 0, 0)),
            pl.BlockSpec((tm, DA_WIDTH), row),
            pl.BlockSpec((1, DA_HEADS * ATTN_VT_ROWS, tm), lambda i: (i, 0, 0)),
            pl.BlockSpec((tm, 4 * HG_WIDTH), row),
            pl.BlockSpec((tm, POOL_WIDTH), row),
        ],
        out_shape=[
            jax.ShapeDtypeStruct((S // tm, DA_WIDTH, tm), BF16),
            jax.ShapeDtypeStruct((S, DA_WIDTH), BF16),
            jax.ShapeDtypeStruct((S // tm, DA_HEADS * ATTN_VT_ROWS, tm), BF16),
            jax.ShapeDtypeStruct((S, 4 * HG_WIDTH), F32),
            jax.ShapeDtypeStruct((S, POOL_WIDTH), F32),
        ],
        scratch_shapes=[
            pltpu.VMEM((D_MODEL, D_IN), BF16),
            pltpu.VMEM((DA_WIDTH, D_MODEL), BF16),
        ],
        compiler_params=_params(("arbitrary",)),
        name="in_proj",
    )(x, sc, sh, w_in, cos_b, sin_b, cos_r, sin_r)


def _attn_kernel(lam_ref, g_ref, qt_ref, k_ref, vt_ref, o_ref, m_scr, a_scr, c_scr, acc_scr, s_scr,
                 p_scr, *, lam_init):
    bq = qt_ref.shape[2]
    bk = vt_ref.shape[2]
    cg = ATTN_QUERY_GROUP
    i = pl.program_id(1)
    qt = qt_ref[0]
    rowq = lax.broadcasted_iota(jnp.int32, qt.shape, 0)
    zero = jnp.zeros_like(qt)
    maps = (jnp.where(rowq < DA_HEAD_DIM, qt, zero), jnp.where(rowq >= DA_HEAD_DIM, qt, zero))
    groups = [(a * bq + c0, c0, maps[a][:, c0:c0 + cg]) for a in range(2) for c0 in range(0, bq, cg)]
    m_scr[...] = jnp.full(m_scr.shape, -jnp.inf, F32)
    acc_scr[...] = jnp.zeros(acc_scr.shape, F32)
    hk = bk // 2
    key_i = lax.broadcasted_iota(jnp.int32, (hk, cg), 0)
    qry_i = lax.broadcasted_iota(jnp.int32, (hk, cg), 1)
    ng = len(groups)
    last = ng - 1

    def cols_of(g):
        return slice(groups[g][0], groups[g][0] + cg)

    def ccols_of(sset, g):
        return slice(sset * 2 * bq + groups[g][0], sset * 2 * bq + groups[g][0] + cg)

    def mask_mode(key_offset, g):
        c0 = groups[g][1]
        if key_offset + hk - 1 <= c0:
            return None
        if key_offset > c0 + cg - 1:
            return "skip"
        return key_offset

    def score_stage(hb, sset, g, key_offset):
        kb = k_ref[pl.ds(pl.multiple_of(hb * hk, hk), hk), :]
        s = jnp.dot(kb, groups[g][2], preferred_element_type=F32)
        if key_offset is not None:
            s = jnp.where(key_i + key_offset <= qry_i + groups[g][1], s, MASK_VALUE)
        s_scr[sset * ng + g] = s
        c_scr[:, ccols_of(sset, g)] = jnp.max(s, axis=0, keepdims=True)

    def max_stage(sset, g):
        cols = cols_of(g)
        m_prev = m_scr[:, cols]
        m_new = jnp.maximum(m_prev, c_scr[:, ccols_of(sset, g)])
        a_scr[:, cols] = jnp.exp2(m_prev - m_new)
        m_scr[:, cols] = m_new

    def exp_stage(sset, g, vt_half, carried, defer):
        cols = cols_of(g)
        p = jnp.exp2(s_scr[sset * ng + g] - m_scr[:, cols])
        alpha = a_scr[:, cols]
        pb = p.astype(BF16)
        acc = acc_scr[:, cols]
        if carried is not None:
            acc = acc + carried
        if defer:
            acc_scr[:, cols] = alpha * acc
            p_scr[...] = pb
        else:
            acc_scr[:, cols] = alpha * acc + jnp.dot(vt_half, pb, preferred_element_type=F32)

    def iteration(t, look, diagonal):
        vt_a = vt_ref[t, :, 0:hk]
        vt_b = vt_ref[t, :, hk:bk]
        carried = jnp.dot(vt_ref[jnp.maximum(t - 1, 0), :, hk:bk], p_scr[...],
                          preferred_element_type=F32)
        if diagonal:
            b_items = [(g, mask_mode(hk, g)) for g in range(ng) if mask_mode(hk, g) != "skip"]
        else:
            b_items = [(g, None) for g in range(ng)]
        score_stage(2 * t + 1, 1, *b_items[0])
        for g in range(ng):
            if g + 1 < len(b_items):
                score_stage(2 * t + 1, 1, *b_items[g + 1])
            if g + 1 < ng:
                max_stage(0, g + 1)
            else:
                max_stage(1, b_items[0][0])
            exp_stage(0, g, vt_a, carried if g == last else None, False)
        if not diagonal:
            score_stage(2 * t + 2, 0, 0, mask_mode(0, 0) if look == "diag" else None)
        for n, (g, _) in enumerate(b_items):
            if not diagonal and g + 1 < ng:
                score_stage(2 * t + 2, 0, g + 1, mask_mode(0, g + 1) if look == "diag" else None)
            if n + 1 < len(b_items):
                max_stage(1, b_items[n + 1][0])
            elif not diagonal:
                max_stage(0, 0)
            exp_stage(1, g, vt_b, None, not diagonal and g == last)

    p_scr[...] = jnp.zeros(p_scr.shape, BF16)
    for g in range(ng):
        score_stage(0, 0, g, (0 - i) * bq)
    max_stage(0, 0)

    unroll = ATTN_UNROLL

    def body(u, carry):
        for k in range(unroll):
            iteration(unroll * u + k, None, False)
        return carry

    n_plain = jnp.maximum(i - 1, 0)
    trips = n_plain // unroll
    lax.fori_loop(0, trips, body, 0)
    for k in range(unroll - 1):
        @pl.when(trips * unroll + k < n_plain)
        def _():
            iteration(trips * unroll + k, None, False)

    @pl.when(i > 0)
    def _():
        iteration(i - 1, "diag", False)

    iteration(i, None, True)

    lq = lam_ref[...]
    lam = (jnp.exp(jnp.sum(lq[0:1] * lq[1:2], axis=1, keepdims=True))
           - jnp.exp(jnp.sum(lq[2:3] * lq[3:4], axis=1, keepdims=True)) + lam_init)
    o = acc_scr[0:DA_VAL_DIM, :] / acc_scr[DA_VAL_DIM:DA_VAL_DIM + 1, :]
    o = o[:, :bq] - lam * o[:, bq:]
    ms = jnp.mean(o * o, axis=0, keepdims=True)
    o = o * lax.rsqrt(ms + EPS) * g_ref[...] * (1.0 - lam_init)
    o_ref[...] = o.T.astype(BF16)


def _diff_attention(qt, k, vt, lam_qk, attn_g, layer):
    S = k.shape[0]
    bq = bk = ATTN_BLOCK
    nk = S // bk
    lam_init = 0.8 - 0.6 * math.exp(-0.3 * layer)
    return pl.pallas_call(
        functools.partial(_attn_kernel, lam_init=lam_init),
        grid=(DA_HEADS, S // bq),
        in_specs=[
            pl.BlockSpec((4, DA_HEAD_DIM), lambda h, i: (0, 0)),
            pl.BlockSpec((DA_VAL_DIM, 1), lambda h, i: (0, 0)),
            pl.BlockSpec((1, LANES, bq), lambda h, i: (i, h, 0)),
            pl.BlockSpec((S, LANES), lambda h, i: (0, h)),
            pl.BlockSpec((nk, ATTN_VT_ROWS, bk), lambda h, i: (0, h, 0)),
        ],
        out_specs=pl.BlockSpec((bq, DA_VAL_DIM), lambda h, i: (i, h)),
        out_shape=jax.ShapeDtypeStruct((S, DA_WIDTH), BF16),
        scratch_shapes=[
            pltpu.VMEM((1, 2 * bq), F32),
            pltpu.VMEM((1, 2 * bq), F32),
            pltpu.VMEM((1, 2 * 2 * bq), F32),
            pltpu.VMEM((ATTN_VT_ROWS, 2 * bq), F32),
            pltpu.VMEM((2 * (2 * bq // ATTN_QUERY_GROUP), bk // 2, ATTN_QUERY_GROUP), F32),
            pltpu.VMEM((bk // 2, ATTN_QUERY_GROUP), BF16),
        ],
        compiler_params=_params(("parallel", "parallel")),
        name="diff_attn",
    )(lam_qk, attn_g.reshape(DA_VAL_DIM, 1), qt, k, vt)


def _hgrn_kernel(lbl_ref, g_ref, hg_ref, o_ref, st_scr, stb_scr, o_scr, *, layer):
    T = hg_ref.shape[0]
    c = HG_CHUNK
    W = HG_WIDTH

    @pl.when(pl.program_id(0) == 0)
    def _():
        st_scr[...] = jnp.zeros(st_scr.shape, F32)
        stb_scr[...] = jnp.zeros(stb_scr.shape, BF16)

    lg = lbl_ref[...]
    ex = jnp.exp(lg - jnp.max(lg, axis=0, keepdims=True))
    prob = ex / jnp.sum(ex, axis=0, keepdims=True)
    lb = jnp.zeros((1, W), F32)
    for li in range(1, layer + 1):
        lb = lb + prob[li:li + 1]

    same_head = _head_block_mask(W, HG_DIM)
    ones_bd = same_head.astype(BF16)
    rowi = lax.broadcasted_iota(jnp.int32, (c, W), 0)

    def body(s, carry):
        sl = pl.ds(pl.multiple_of(s * c, c), c)
        qf = hg_ref[sl, 0:W]
        z = hg_ref[sl, W:2 * W]
        iv = hg_ref[sl, 2 * W:3 * W]
        f = lb + (1.0 - lb) * jax.nn.sigmoid(z)
        logf = jnp.log(jnp.maximum(f, TINY))
        key = (1.0 - lb) * jax.nn.sigmoid(-z)
        val = _silu(iv)
        b = logf
        sft = 1
        while sft < c:
            b = b + jnp.where(rowi >= sft, pltpu.roll(b, sft, 0), 0.0)
            sft *= 2
        b_last = b[c - 1:c]
        kd = (key * jnp.exp(b_last - b)).astype(BF16)
        upd = lax.dot_general(val.astype(BF16), kd, (((0,), (0,)), ((), ())),
                              preferred_element_type=F32)
        es = [(qf * key).astype(BF16)]
        vrs = [val]
        for d in range(1, c):
            kr = pltpu.roll(key, d, 0)
            br = pltpu.roll(b, d, 0)
            e = jnp.where(rowi >= d, qf * kr * jnp.exp(b - br), 0.0)
            es.append(e.astype(BF16))
            vrs.append(pltpu.roll(val, d, 0))
        estack = jnp.concatenate(es, axis=0)
        r = jnp.dot(estack, ones_bd, preferred_element_type=F32)
        o = r[0:c] * vrs[0]
        for d in range(1, c):
            o = o + r[d * c:(d + 1) * c] * vrs[d]
        qd = (qf * jnp.exp(b)).astype(BF16)
        o = o + lax.dot_general(qd, stb_scr[...], (((1,), (1,)), ((), ())),
                                preferred_element_type=F32)
        o_scr[sl, :] = o
        decay = jnp.exp(b_last)
        for hd in range(HG_HEADS):
            rws = slice(hd * HG_DIM, (hd + 1) * HG_DIM)
            lns = slice(hd * HG_DIM // LANES * LANES, (hd * HG_DIM // LANES + 1) * LANES)
            new = decay[:, lns] * st_scr[rws, lns] + jnp.where(same_head[rws, lns], upd[rws, lns], 0.0)
            st_scr[rws, lns] = new
            stb_scr[rws, lns] = new.astype(BF16)
        return carry

    lax.fori_loop(0, T // c, body, 0, unroll=HG_UNROLL)

    o = o_scr[...]
    ms = jnp.dot(o * o, same_head.astype(F32), preferred_element_type=F32,
                 precision=HIGHEST) * (1.0 / HG_DIM)
    gate = hg_ref[:, 3 * W:4 * W]
    o_ref[...] = (o * lax.rsqrt(ms + EPS) * g_ref[...] * _silu(gate)).astype(BF16)


def _hgrn2(hg, lb_logits, norm_g, layer):
    S = hg.shape[0]
    T = HG_BLOCK
    g_t = jnp.tile(norm_g.reshape(1, HG_DIM), (1, HG_HEADS))
    return pl.pallas_call(
        functools.partial(_hgrn_kernel, layer=layer),
        grid=(S // T,),
        in_specs=[
            pl.BlockSpec((DEPTH, HG_WIDTH), lambda i: (0, 0)),
            pl.BlockSpec((1, HG_WIDTH), lambda i: (0, 0)),
            pl.BlockSpec((T, 4 * HG_WIDTH), lambda i: (i, 0)),
        ],
        out_specs=pl.BlockSpec((T, HG_WIDTH), lambda i: (i, 0)),
        out_shape=jax.ShapeDtypeStruct((S, HG_WIDTH), BF16),
        scratch_shapes=[
            pltpu.VMEM((HG_WIDTH, HG_WIDTH), F32),
            pltpu.VMEM((HG_WIDTH, HG_WIDTH), BF16),
            pltpu.VMEM((T, HG_WIDTH), F32),
        ],
        compiler_params=_params(("arbitrary",)),
        name="hgrn2",
    )(lb_logits, g_t, hg)


def _pool_kernel(u_ref, w_ref, sc_ref, o_ref, halo_scr):
    T = u_ref.shape[0]
    H = POOL_HALO
    blk = pl.program_id(0)

    @pl.when(blk == 0)
    def _():
        halo_scr[...] = jnp.zeros(halo_scr.shape, F32)

    u = u_ref[...]
    ext = jnp.concatenate([halo_scr[...], u], axis=0)
    halo_scr[...] = u[T - H:, :]
    sums = []
    s = ext
    w = 1
    while w < POOL_WINDOWS[-1]:
        s = s + pltpu.roll(s, w, 0)
        w *= 2
        sums.append(s[H:, :])
    t1 = (blk * T + 1 + lax.broadcasted_iota(jnp.int32, (T, POOL_WIDTH), 0)).astype(F32)
    lane = lax.broadcasted_iota(jnp.int32, (T, POOL_WIDTH), 1)
    pooled = None
    for gi, win in enumerate(POOL_WINDOWS):
        mean = sums[gi] / jnp.minimum(t1, float(win))
        pooled = mean if pooled is None else jnp.where(lane >= gi * POOL_GROUP, mean, pooled)
    pooled = pooled - u
    y = jnp.dot(pooled.astype(BF16), w_ref[...], preferred_element_type=F32)
    o_ref[...] = (y * sc_ref[...]).astype(BF16)


def _multiscale_pool(pu, pool_w, pool_scale):
    S = pu.shape[0]
    T = ROW_TILE
    w_bd = jax.scipy.linalg.block_diag(*[pool_w[g] for g in range(len(POOL_WINDOWS))]).astype(BF16)
    return pl.pallas_call(
        _pool_kernel,
        grid=(S // T,),
        in_specs=[
            pl.BlockSpec((T, POOL_WIDTH), lambda i: (i, 0)),
            pl.BlockSpec((POOL_WIDTH, POOL_WIDTH), lambda i: (0, 0)),
            pl.BlockSpec((1, POOL_WIDTH), lambda i: (0, 0)),
        ],
        out_specs=pl.BlockSpec((T, POOL_WIDTH), lambda i: (i, 0)),
        out_shape=jax.ShapeDtypeStruct((S, POOL_WIDTH), BF16),
        scratch_shapes=[pltpu.VMEM((POOL_HALO, POOL_WIDTH), F32)],
        compiler_params=_params(("arbitrary",)),
        name="ms_pool",
    )(pu, w_bd, pool_scale.reshape(1, POOL_WIDTH))


def _outproj_kernel(a_ref, r_ref, p_ref, x_ref, wf_ref, g1_ref, lg_ref, lb_ref, o_ref, w_ref):
    @pl.when(pl.program_id(0) == 0)
    def _():
        w_ref[...] = wf_ref[0].astype(BF16)

    m = jnp.dot(a_ref[...], w_ref[0:DA_WIDTH, :], preferred_element_type=F32)
    m += jnp.dot(r_ref[...], w_ref[DA_WIDTH:DA_WIDTH + HG_WIDTH, :], preferred_element_type=F32)
    m += jnp.dot(p_ref[...], w_ref[DA_WIDTH + HG_WIDTH:, :], preferred_element_type=F32)
    y = ALPHA * x_ref[...] + (1.0 + g1_ref[...]) * m
    o_ref[...] = _layer_norm(y, lg_ref[...], lb_ref[...])


def _out_projection(a, r, p, x, w_out, layer, g1, ln_g, ln_b):
    S = x.shape[0]
    tm = ROW_TILE
    row = lambda i: (i, 0)
    fixed = lambda i: (0, 0)
    vec = pl.BlockSpec((1, D_MODEL), fixed)
    return pl.pallas_call(
        _outproj_kernel,
        grid=(S // tm,),
        in_specs=[
            pl.BlockSpec((tm, DA_WIDTH), row),
            pl.BlockSpec((tm, HG_WIDTH), row),
            pl.BlockSpec((tm, POOL_WIDTH), row),
            pl.BlockSpec((tm, D_MODEL), row),
            pl.BlockSpec((1, D_MIX, D_MODEL), lambda i: (layer, 0, 0), pipeline_mode=pl.Buffered(1)),
            vec, vec, vec,
        ],
        out_specs=pl.BlockSpec((tm, D_MODEL), row),
        out_shape=jax.ShapeDtypeStruct((S, D_MODEL), F32),
        scratch_shapes=[pltpu.VMEM((D_MIX, D_MODEL), BF16)],
        compiler_params=_params(("arbitrary",)),
        name="out_proj_ln",
    )(a, r, p, x, w_out, g1, ln_g.reshape(1, D_MODEL), ln_b.reshape(1, D_MODEL))


def _ffn_kernel(x_ref, sc_ref, sh_ref, g2_ref, w1_ref, w3_ref, w2_ref, lg_ref, lb_ref, *rest):
    n_cast = (len(rest) - 1) // 2
    o_ref = rest[n_cast]
    for src, dst in zip(rest[:n_cast], rest[n_cast + 1:]):
        dst[...] = src[...].astype(BF16)
    x = x_ref[...]
    h = (x * (1.0 + sc_ref[...]) + sh_ref[...]).astype(BF16)
    acc = jnp.zeros(x.shape, F32)
    step = 4 * MXU_DIM
    for lo in range(0, FFN_DIM, step):
        hi = min(lo + step, FFN_DIM)
        a = jnp.dot(h, w1_ref[:, lo:hi], preferred_element_type=F32)
        b = jnp.dot(h, w3_ref[:, lo:hi], preferred_element_type=F32)
        acc += jnp.dot((_silu(a) * b).astype(BF16), w2_ref[lo:hi, :], preferred_element_type=F32)
    y = ALPHA * x + (1.0 + g2_ref[...]) * acc
    o_ref[...] = _layer_norm(y, lg_ref[...], lb_ref[...])


def _dense_ffn(x, sc, sh, g2, w1, w3, w2, ln_g, ln_b, to_cast=()):
    S = x.shape[0]
    tm = FFN_CAST_TILE if to_cast else ROW_TILE
    steps = S // tm
    row = lambda i: (i, 0)
    fixed = lambda i: (0, 0)
    vec = pl.BlockSpec((1, D_MODEL), fixed)
    once = pl.Buffered(1)
    slabs = [a.reshape(-1, a.shape[-1]) for a in to_cast]
    slab_specs = [pl.BlockSpec((a.shape[0] // steps, a.shape[1]), row) for a in slabs]
    for a in slabs:
        assert a.shape[0] % (steps * 16) == 0
    outs = pl.pallas_call(
        _ffn_kernel,
        grid=(steps,),
        in_specs=[
            pl.BlockSpec((tm, D_MODEL), row),
            vec, vec, vec,
            pl.BlockSpec((D_MODEL, FFN_DIM), fixed, pipeline_mode=once),
            pl.BlockSpec((D_MODEL, FFN_DIM), fixed, pipeline_mode=once),
            pl.BlockSpec((FFN_DIM, D_MODEL), fixed, pipeline_mode=once),
            vec, vec,
        ] + slab_specs,
        out_specs=[pl.BlockSpec((tm, D_MODEL), row)] + slab_specs,
        out_shape=[jax.ShapeDtypeStruct((S, D_MODEL), F32)]
        + [jax.ShapeDtypeStruct(a.shape, BF16) for a in slabs],
        compiler_params=_params(("parallel",)),
        name="dense_ffn_ln",
    )(x, sc, sh, g2, w1, w3, w2, ln_g.reshape(1, D_MODEL), ln_b.reshape(1, D_MODEL), *slabs)
    return outs[0], [o.reshape(a.shape) for o, a in zip(outs[1:], to_cast)]


def _router_kernel(x_ref, sc_ref, sh_ref, rw_ref, h_ref, rank_ref, comb_ref, cum_ref):
    T = x_ref.shape[0]
    tc = MOE_CHUNK
    E = N_EXPERTS
    h = x_ref[...] * (1.0 + sc_ref[...]) + sh_ref[...]
    h_ref[...] = h.astype(BF16)
    logits = lax.dot_general(rw_ref[...], h, (((1,), (1,)), ((), ())),
                             preferred_element_type=F32, precision=HIGHEST)
    eidx = lax.broadcasted_iota(jnp.int32, (E, T), 0)
    v1 = jnp.max(logits, axis=0, keepdims=True)
    i1 = jnp.min(jnp.where(logits == v1, eidx, E), axis=0, keepdims=True)
    m1 = eidx == i1
    rest = jnp.where(m1, -jnp.inf, logits)
    v2 = jnp.max(rest, axis=0, keepdims=True)
    i2 = jnp.min(jnp.where(rest == v2, eidx, E), axis=0, keepdims=True)
    m2 = eidx == i2
    e2 = jnp.exp(v2 - v1)
    g1 = 1.0 / (1.0 + e2)
    g2 = e2 / (1.0 + e2)
    comb = jnp.where(m1, g1, 0.0) + jnp.where(m2, g2, 0.0)
    routed = jnp.logical_or(m1, m2)
    ind = routed.astype(BF16)
    r_i = lax.broadcasted_iota(jnp.int32, (tc, tc), 0)
    c_i = lax.broadcasted_iota(jnp.int32, (tc, tc), 1)
    strict = (r_i < c_i).astype(BF16)
    lane = lax.broadcasted_iota(jnp.int32, (E, LANES), 1)
    running = jnp.zeros((E, 1), F32)
    cum = jnp.zeros((E, LANES), F32)
    for ci in range(T // tc):
        sl = slice(ci * tc, (ci + 1) * tc)
        ind_c = ind[:, sl]
        rank_c = jnp.dot(ind_c, strict, preferred_element_type=F32) + running
        rank_ref[:, ci] = jnp.where(routed[:, sl], rank_c, -1.0).reshape(E, 1, tc)
        comb_ref[:, ci] = comb[:, sl].reshape(E, 1, tc)
        cum = jnp.where(lane == ci, running, cum)
        running = running + jnp.sum(ind_c.astype(F32), axis=1, keepdims=True)
    cum = jnp.where(lane == T // tc, running, cum)
    cum_ref[0] = cum.astype(jnp.int32)


def _router(x, sc, sh, router_w):
    S = x.shape[0]
    T = MOE_BLOCK
    E = N_EXPERTS
    fixed = lambda i: (0, 0)
    vec = pl.BlockSpec((1, D_MODEL), fixed)
    return pl.pallas_call(
        _router_kernel,
        grid=(S // T,),
        in_specs=[
            pl.BlockSpec((T, D_MODEL), lambda i: (i, 0)),
            vec, vec,
            pl.BlockSpec((E, D_MODEL), fixed),
        ],
        out_specs=[
            pl.BlockSpec((T, D_MODEL), lambda i: (i, 0)),
            pl.BlockSpec((E, T // MOE_CHUNK, 1, MOE_CHUNK), lambda i: (0, i, 0, 0)),
            pl.BlockSpec((E, T // MOE_CHUNK, 1, MOE_CHUNK), lambda i: (0, i, 0, 0)),
            pl.BlockSpec((1, E, LANES), lambda i: (i, 0, 0)),
        ],
        out_shape=[
            jax.ShapeDtypeStruct((S, D_MODEL), BF16),
            jax.ShapeDtypeStruct((E, S // MOE_CHUNK, 1, MOE_CHUNK), F32),
            jax.ShapeDtypeStruct((E, S // MOE_CHUNK, 1, MOE_CHUNK), F32),
            jax.ShapeDtypeStruct((S // T, E, LANES), jnp.int32),
        ],
        compiler_params=_params(("parallel",)),
        name="moe_router",
    )(x, sc, sh, router_w.T)


def _moe_kernel(cum_ref, h_ref, rank_ref, comb_ref, w1_ref, w3_ref, w2_ref, y_ref, xe_scr, gate_scr,
                out_scr):
    T = h_ref.shape[0]
    tr, tc, win = MOE_TILE, MOE_CHUNK, MOE_WINDOW
    nchunk = T // tc
    b = pl.program_id(0)
    e = pl.program_id(1)
    f = pl.program_id(2)

    @pl.when(jnp.logical_and(e == 0, f == 0))
    def _():
        y_ref[...] = jnp.zeros(y_ref.shape, F32)

    base = (b * N_EXPERTS + e) * LANES
    count = cum_ref[base + nchunk]
    ntiles = (count + (tr - 1)) // tr

    def token_row(ref, first, n):
        blk = ref[0, pl.ds(first, n)]
        return jnp.concatenate([blk[c] for c in range(n)], axis=1)

    def tile(r, carry):
        lo = r * tr
        rows = pl.ds(pl.multiple_of(lo, 16), tr)

        def hits(first, n):
            want = (lo + lax.broadcasted_iota(jnp.int32, (tr, n * tc), 0)).astype(F32)
            return token_row(rank_ref, first, n) == want

        def token_span(first, n):
            start = first * tc
            if not isinstance(start, int):
                start = pl.multiple_of(start, tc)
            return pl.ds(start, n * tc)

        c_lo = jnp.int32(0)
        c_hi = jnp.int32(-1)
        for c in range(nchunk):
            c_lo += (cum_ref[base + c + 1] <= lo).astype(jnp.int32)
            c_hi += (cum_ref[base + c] < lo + tr).astype(jnp.int32)
        fits = c_hi - c_lo < win
        window = (jnp.minimum(c_lo, nchunk - win), win)
        block = (0, nchunk)

        def gather(first, n):
            hit = hits(first, n)
            xe_scr[rows, :] = jnp.dot(hit.astype(BF16), h_ref[token_span(first, n), :],
                                      preferred_element_type=F32).astype(BF16)
            gate_scr[rows, :] = jnp.sum(jnp.where(hit, token_row(comb_ref, first, n), 0.0),
                                        axis=1, keepdims=True)

        @pl.when(jnp.logical_and(f == 0, fits))
        def _():
            gather(*window)

        @pl.when(jnp.logical_and(f == 0, jnp.logical_not(fits)))
        def _():
            gather(*block)

        xe = xe_scr[rows, :]
        a = jnp.dot(xe, w1_ref[0], preferred_element_type=F32)
        g = jnp.dot(xe, w3_ref[0], preferred_element_type=F32)
        mid = (_silu(a) * g).astype(BF16)
        part = jnp.dot(mid, w2_ref[0], preferred_element_type=F32)

        @pl.when(f == 0)
        def _():
            out_scr[rows, :] = part

        if MOE_FSPLIT > 2:
            @pl.when(jnp.logical_and(f > 0, f < MOE_FSPLIT - 1))
            def _():
                out_scr[rows, :] += part

        @pl.when(f == MOE_FSPLIT - 1)
        def _():
            out_bf = ((out_scr[rows, :] + part) * gate_scr[rows, :]).astype(BF16)

            def scatter(first, n):
                y_ref[token_span(first, n), :] += lax.dot_general(
                    hits(first, n).astype(BF16), out_bf, (((0,), (0,)), ((), ())),
                    preferred_element_type=F32)

            @pl.when(fits)
            def _():
                scatter(*window)

            @pl.when(jnp.logical_not(fits))
            def _():
                scatter(*block)

        return carry

    lax.fori_loop(0, ntiles, tile, 0)


def _moe_ffn(h_bf, rank, comb, cum, w1, w3, w2):
    S = h_bf.shape[0]
    T = MOE_BLOCK
    fw = EXPERT_DIM // MOE_FSPLIT
    grid_spec = pltpu.PrefetchScalarGridSpec(
        num_scalar_prefetch=1,
        grid=(S // T, N_EXPERTS, MOE_FSPLIT),
        in_specs=[
            pl.BlockSpec((T, D_MODEL), lambda b, e, f, c: (b, 0), pipeline_mode=pl.Buffered(1)),
            pl.BlockSpec((1, T // MOE_CHUNK, 1, MOE_CHUNK), lambda b, e, f, c: (e, b, 0, 0)),
            pl.BlockSpec((1, T // MOE_CHUNK, 1, MOE_CHUNK), lambda b, e, f, c: (e, b, 0, 0)),
            pl.BlockSpec((1, D_MODEL, fw), lambda b, e, f, c: (e, 0, f)),
            pl.BlockSpec((1, D_MODEL, fw), lambda b, e, f, c: (e, 0, f)),
            pl.BlockSpec((1, fw, D_MODEL), lambda b, e, f, c: (e, f, 0)),
        ],
        out_specs=pl.BlockSpec((T, D_MODEL), lambda b, e, f, c: (b, 0)),
        scratch_shapes=[
            pltpu.VMEM((T + MOE_TILE, D_MODEL), BF16),
            pltpu.VMEM((T + MOE_TILE, 1), F32),
            pltpu.VMEM((T + MOE_TILE, D_MODEL), F32),
        ],
    )
    return pl.pallas_call(
        _moe_kernel,
        grid_spec=grid_spec,
        out_shape=jax.ShapeDtypeStruct((S, D_MODEL), F32),
        compiler_params=_params(("arbitrary", "arbitrary", "arbitrary")),
        name="moe_experts",
    )(cum.reshape(-1), h_bf, rank, comb, w1, w3, w2)


def _ln_kernel(x_ref, y_ref, g2_ref, lg_ref, lb_ref, o_ref):
    y = ALPHA * x_ref[...] + (1.0 + g2_ref[...]) * y_ref[...]
    o_ref[...] = _layer_norm(y, lg_ref[...], lb_ref[...])


def _residual_ln(x, y, g2, ln_g, ln_b):
    S = x.shape[0]
    tm = ROW_TILE
    row = lambda i: (i, 0)
    vec = pl.BlockSpec((1, D_MODEL), lambda i: (0, 0))
    return pl.pallas_call(
        _ln_kernel,
        grid=(S // tm,),
        in_specs=[pl.BlockSpec((tm, D_MODEL), row), pl.BlockSpec((tm, D_MODEL), row), vec, vec, vec],
        out_specs=pl.BlockSpec((tm, D_MODEL), row),
        out_shape=jax.ShapeDtypeStruct((S, D_MODEL), F32),
        compiler_params=_params(("parallel",)),
        name="residual_ln",
    )(x, y, g2, ln_g.reshape(1, D_MODEL), ln_b.reshape(1, D_MODEL))


def _rope_tables(S):
    d = DA_HEAD_DIM
    tm = ATTN_BLOCK
    inv_freq = 1.0 / (ROPE_THETA ** (jnp.arange(0, d, 2, dtype=F32) / d))
    inv_lane = inv_freq[jnp.arange(LANES) % (d // 2)][None, :]
    base = (jnp.arange(S // tm, dtype=F32) * tm)[:, None] * inv_lane
    offs = jnp.arange(tm, dtype=F32)[:, None] * inv_lane
    shape_b = (S // tm, 1, LANES)
    return jnp.cos(base).reshape(shape_b), jnp.sin(base).reshape(shape_b), jnp.cos(offs), jnp.sin(offs)


def kernel(x, c, w_ada, b_ada, w_in, lam_qk, attn_norm_g, hg_lb_logits, hg_norm_g, pool_w, pool_scale, w_out, ln1_g, ln1_b, ln2_g, ln2_b, ffn_w1, ffn_w3, ffn_w2, router_w, exp_w1, exp_w3, exp_w2):
    B, S, D = x.shape
    assert B == 1 and D == D_MODEL and S % MOE_BLOCK == 0
    xs = x.reshape(S, D)
    mod = _modulation(c, w_ada, b_ada)
    rope = _rope_tables(S)
    experts_bf = {}
    for l in range(DEPTH):
        sh1, sc1, g1, sh2, sc2, g2 = (mod[l, j] for j in range(6))
        qt, k, vt, hg, pu = _in_projection(xs, sc1, sh1, w_in, l, rope)
        a = _diff_attention(qt, k, vt, lam_qk[l], attn_norm_g[l], l)
        r = _hgrn2(hg, hg_lb_logits, hg_norm_g[l], l)
        p = _multiscale_pool(pu, pool_w[l], pool_scale[l])
        xs = _out_projection(a, r, p, xs, w_out, l, g1, ln1_g[l], ln1_b[l])
        if l % 2 == 0:
            i = l // 2
            ride = (exp_w1[i], exp_w3[i], exp_w2[i]) if l + 1 < DEPTH else ()
            xs, cast = _dense_ffn(xs, sc2, sh2, g2, ffn_w1[i].astype(BF16), ffn_w3[i].astype(BF16),
                                  ffn_w2[i].astype(BF16), ln2_g[l], ln2_b[l], ride)
            if ride:
                experts_bf[i] = cast
        else:
            i = l // 2
            h_bf, rank, comb, cum = _router(xs, sc2, sh2, router_w[i])
            w1, w3, w2 = experts_bf.get(i) or [w[i].astype(BF16) for w in (exp_w1, exp_w3, exp_w2)]
            y = _moe_ffn(h_bf, rank, comb, cum, w1, w3, w2)
            xs = _residual_ln(xs, y, g2, ln2_g[l], ln2_b[l])
    return xs.reshape(B, S, D)
```

```python
import functools
import math

import jax
import jax.numpy as jnp
from jax import lax
from jax.experimental import pallas as pl
from jax.experimental.pallas import tpu as pltpu

F32 = jnp.float32
BF16 = jnp.bfloat16
HIGHEST = lax.Precision.HIGHEST

D_MODEL = 1024
DEPTH = 2
DA_HEADS = 4
DA_WIDTH = D_MODEL // 2
DA_HEAD_DIM = DA_WIDTH // (2 * DA_HEADS)
DA_VAL_DIM = 2 * DA_HEAD_DIM
HG_HEADS = 4
HG_WIDTH = D_MODEL // 4
HG_DIM = HG_WIDTH // HG_HEADS
POOL_WINDOWS = (2, 4, 8, 16)
POOL_WIDTH = D_MODEL // 4
POOL_GROUP = POOL_WIDTH // len(POOL_WINDOWS)
D_MIX = DA_WIDTH + HG_WIDTH + POOL_WIDTH
D_IN = 3 * DA_WIDTH + 4 * HG_WIDTH + POOL_WIDTH
ROPE_THETA = 10000.0
MASK_VALUE = -1e30
TINY = 1e-30
FFN_DIM = 11 * D_MODEL // 4
N_EXPERTS = 8
EXPERT_DIM = 7 * D_MODEL // 2
ALPHA = (2 * DEPTH) ** 0.25
EPS = 1e-5
LOG2_E = math.log2(math.e)

LANES = 128
MXU_DIM = 256
VMEM_LIMIT = 60 * 1024 * 1024

ROW_TILE = 512
FFN_CAST_TILE = 256
ATTN_BLOCK = 512
ATTN_Q_BLOCK = 1024
ATTN_QUERY_GROUP = 256
ATTN_UNROLL = 2
ATTN_SUM_ROWS = 16
ATTN_VT_ROWS = DA_VAL_DIM + ATTN_SUM_ROWS
HG_BLOCK = 512
HG_CHUNK = 16
HG_UNROLL = 4
POOL_HALO = 16
MOE_BLOCK = 2048
MOE_TILE = 192
MOE_CHUNK = 256
MOE_WINDOW = 5
MOE_FSPLIT = 2


def _params(sem, vmem=VMEM_LIMIT):
    return pltpu.CompilerParams(dimension_semantics=sem, vmem_limit_bytes=vmem)


def _layer_norm(y, g, b):
    mu = jnp.mean(y, axis=-1, keepdims=True)
    d = y - mu
    var = jnp.mean(d * d, axis=-1, keepdims=True)
    return d * lax.rsqrt(var + EPS) * g + b


def _silu(x):
    return x * jax.nn.sigmoid(x)


def _head_block_mask(n, group):
    r = lax.broadcasted_iota(jnp.int32, (n, n), 0) // group
    c = lax.broadcasted_iota(jnp.int32, (n, n), 1) // group
    return r == c


def _mod_kernel(c_ref, w_ref, b_ref, o_ref):
    cond = _silu(c_ref[...])
    cond8 = jnp.broadcast_to(cond, (8, D_MODEL))
    r = jnp.dot(cond8, w_ref[0], preferred_element_type=F32, precision=HIGHEST)
    o_ref[0] = r[0:1] + b_ref[0]


def _modulation(c, w_ada, b_ada):
    n = 6 * DEPTH
    b3 = b_ada.reshape(n, 1, D_MODEL)
    out = pl.pallas_call(
        _mod_kernel,
        grid=(DEPTH, 6),
        in_specs=[
            pl.BlockSpec((1, D_MODEL), lambda l, j: (0, 0)),
            pl.BlockSpec((1, D_MODEL, D_MODEL), lambda l, j: (l, 0, j)),
            pl.BlockSpec((1, 1, D_MODEL), lambda l, j: (l * 6 + j, 0, 0)),
        ],
        out_specs=pl.BlockSpec((1, 1, D_MODEL), lambda l, j: (l * 6 + j, 0, 0)),
        out_shape=jax.ShapeDtypeStruct((n, 1, D_MODEL), F32),
        compiler_params=_params(("parallel", "parallel")),
        name="adaln_mod",
    )(c, w_ada, b3)
    return out.reshape(DEPTH, 6, 1, D_MODEL)


def _inproj_kernel(x_ref, sc_ref, sh_ref, wf_ref, cosb_ref, sinb_ref, cosr_ref, sinr_ref,
                   qt_ref, k_ref, vt_ref, hg_ref, pu_ref, w_ref, wvt_ref):
    @pl.when(pl.program_id(0) == 0)
    def _():
        w_ref[...] = wf_ref[0].astype(BF16)
        wvt_ref[...] = wf_ref[0, :, 2 * DA_WIDTH:3 * DA_WIDTH].T.astype(BF16)

    h = (x_ref[...] * (1.0 + sc_ref[...]) + sh_ref[...]).astype(BF16)
    qk = jnp.dot(h, w_ref[:, : 2 * DA_WIDTH], preferred_element_type=F32)
    cb, sb, cr, sr = cosb_ref[0], sinb_ref[0], cosr_ref[...], sinr_ref[...]
    cos128 = cb * cr - sb * sr
    sin128 = sb * cr + cb * sr
    half = DA_HEAD_DIM // 2
    lane128 = lax.broadcasted_iota(jnp.int32, cos128.shape, 1)
    sin128 = jnp.where((lane128 & half) == 0, -sin128, sin128)
    reps = 2 * DA_WIDTH // LANES
    cos = jnp.tile(cos128, (1, reps))
    sin = jnp.tile(sin128, (1, reps))
    lane = lax.broadcasted_iota(jnp.int32, qk.shape, 1)
    first_half = (lane & half) == 0
    width = 2 * DA_WIDTH
    partner = jnp.where(first_half, pltpu.roll(qk, width - half, 1), pltpu.roll(qk, half, 1))
    rot = qk * cos + partner * sin
    qt_ref[0] = (rot[:, :DA_WIDTH] * (DA_HEAD_DIM ** -0.5 * LOG2_E)).T.astype(BF16)
    k_ref[...] = rot[:, DA_WIDTH:].astype(BF16)
    o = 2 * DA_WIDTH
    vt = lax.dot_general(wvt_ref[...], h, (((1,), (1,)), ((), ())),
                         preferred_element_type=F32).astype(BF16)
    ones = jnp.ones((ATTN_SUM_ROWS, vt.shape[1]), BF16)
    pieces = []
    for hd in range(DA_HEADS):
        pieces += [vt[hd * DA_VAL_DIM:(hd + 1) * DA_VAL_DIM], ones]
    vt_ref[0] = jnp.concatenate(pieces, axis=0)
    o += DA_WIDTH
    hg_ref[...] = jnp.dot(h, w_ref[:, o:o + 4 * HG_WIDTH], preferred_element_type=F32)
    o += 4 * HG_WIDTH
    pu_ref[...] = jnp.dot(h, w_ref[:, o:o + POOL_WIDTH], preferred_element_type=F32)


def _in_projection(x, sc, sh, w_in, layer, rope):
    S = x.shape[0]
    tm = ATTN_BLOCK
    row = lambda i: (i, 0)
    fixed = lambda i: (0, 0)
    cos_b, sin_b, cos_r, sin_r = rope
    return pl.pallas_call(
        _inproj_kernel,
        grid=(S // tm,),
        in_specs=[
            pl.BlockSpec((tm, D_MODEL), row),
            pl.BlockSpec((1, D_MODEL), fixed),
            pl.BlockSpec((1, D_MODEL), fixed),
            pl.BlockSpec((1, D_MODEL, D_IN), lambda i: (layer, 0, 0), pipeline_mode=pl.Buffered(1)),
            pl.BlockSpec((1, 1, LANES), lambda i: (i, 0, 0)),
            pl.BlockSpec((1, 1, LANES), lambda i: (i, 0, 0)),
            pl.BlockSpec((tm, LANES), fixed),
            pl.BlockSpec((tm, LANES), fixed),
        ],
        out_specs=[
            pl.BlockSpec((1, DA_WIDTH, tm), lambda i: (i, 0, 0)),
            pl.BlockSpec((tm, DA_WIDTH), row),
            pl.BlockSpec((1, DA_HEADS * ATTN_VT_ROWS, tm), lambda i: (i, 0, 0)),
            pl.BlockSpec((tm, 4 * HG_WIDTH), row),
            pl.BlockSpec((tm, POOL_WIDTH), row),
        ],
        out_shape=[
            jax.ShapeDtypeStruct((S // tm, DA_WIDTH, tm), BF16),
            jax.ShapeDtypeStruct((S, DA_WIDTH), BF16),
            jax.ShapeDtypeStruct((S // tm, DA_HEADS * ATTN_VT_ROWS, tm), BF16),
            jax.ShapeDtypeStruct((S, 4 * HG_WIDTH), F32),
            jax.ShapeDtypeStruct((S, POOL_WIDTH), F32),
        ],
        scratch_shapes=[
            pltpu.VMEM((D_MODEL, D_IN), BF16),
            pltpu.VMEM((DA_WIDTH, D_MODEL), BF16),
        ],
        compiler_params=_params(("arbitrary",)),
        name="in_proj",
    )(x, sc, sh, w_in, cos_b, sin_b, cos_r, sin_r)


def _attn_kernel(lam_ref, g_ref, qt_ref, k_ref, vt_ref, o_ref, m_scr, a_scr, c_scr, acc_scr, s_scr,
                 p_scr, *, lam_init):
    bq = qt_ref.shape[0] * qt_ref.shape[2]
    bk = vt_ref.shape[2]
    cg = ATTN_QUERY_GROUP
    i = pl.program_id(1)
    qt = jnp.concatenate([qt_ref[n] for n in range(qt_ref.shape[0])], axis=1)
    rowq = lax.broadcasted_iota(jnp.int32, qt.shape, 0)
    zero = jnp.zeros_like(qt)
    maps = (jnp.where(rowq < DA_HEAD_DIM, qt, zero), jnp.where(rowq >= DA_HEAD_DIM, qt, zero))
    groups = [(a * bq + c0, c0, maps[a][:, c0:c0 + cg]) for a in range(2) for c0 in range(0, bq, cg)]
    m_scr[...] = jnp.full(m_scr.shape, -jnp.inf, F32)
    acc_scr[...] = jnp.zeros(acc_scr.shape, F32)
    hk = bk // 2
    key_i = lax.broadcasted_iota(jnp.int32, (hk, cg), 0)
    qry_i = lax.broadcasted_iota(jnp.int32, (hk, cg), 1)
    ng = len(groups)
    last = ng - 1

    def cols_of(g):
        return slice(groups[g][0], groups[g][0] + cg)

    def ccols_of(sset, g):
        return slice(sset * 2 * bq + groups[g][0], sset * 2 * bq + groups[g][0] + cg)

    def mask_mode(key_offset, g):
        c0 = groups[g][1]
        if key_offset + hk - 1 <= c0:
            return None
        if key_offset > c0 + cg - 1:
            return "skip"
        return key_offset

    def score_stage(hb, sset, g, key_offset):
        kb = k_ref[pl.ds(pl.multiple_of(hb * hk, hk), hk), :]
        s = jnp.dot(kb, groups[g][2], preferred_element_type=F32)
        if key_offset is not None:
            s = jnp.where(key_i + key_offset <= qry_i + groups[g][1], s, MASK_VALUE)
        s_scr[sset * ng + g] = s
        c_scr[:, ccols_of(sset, g)] = jnp.max(s, axis=0, keepdims=True)

    def max_stage(sset, g):
        cols = cols_of(g)
        m_prev = m_scr[:, cols]
        m_new = jnp.maximum(m_prev, c_scr[:, ccols_of(sset, g)])
        a_scr[:, cols] = jnp.exp2(m_prev - m_new)
        m_scr[:, cols] = m_new

    def exp_stage(sset, g, vt_half, carried, defer):
        cols = cols_of(g)
        p = jnp.exp2(s_scr[sset * ng + g] - m_scr[:, cols])
        alpha = a_scr[:, cols]
        pb = p.astype(BF16)
        acc = acc_scr[:, cols]
        if carried is not None:
            acc = acc + carried
        if defer:
            acc_scr[:, cols] = alpha * acc
            p_scr[...] = pb
        else:
            acc_scr[:, cols] = alpha * acc + jnp.dot(vt_half, pb, preferred_element_type=F32)

    def items(key_offset):
        if key_offset is None:
            return [(g, None) for g in range(ng)]
        return [(g, mask_mode(key_offset, g)) for g in range(ng) if mask_mode(key_offset, g) != "skip"]

    def iteration(t, offs, look):
        vt_a = vt_ref[t, :, 0:hk]
        vt_b = vt_ref[t, :, hk:bk]
        carried = jnp.dot(vt_ref[jnp.maximum(t - 1, 0), :, hk:bk], p_scr[...],
                          preferred_element_type=F32)
        a_items = items(offs)
        b_items = items(None if offs is None else offs + hk)
        n_items = [] if look == "stop" else items(look)
        score_stage(2 * t + 1, 1, *b_items[0])
        for n, (g, _) in enumerate(a_items):
            if n + 1 < len(b_items):
                score_stage(2 * t + 1, 1, *b_items[n + 1])
            if n + 1 < len(a_items):
                max_stage(0, a_items[n + 1][0])
            else:
                max_stage(1, b_items[0][0])
            exp_stage(0, g, vt_a, carried if g == last else None, False)
        if n_items:
            score_stage(2 * t + 2, 0, *n_items[0])
        for n, (g, _) in enumerate(b_items):
            if n + 1 < len(n_items):
                score_stage(2 * t + 2, 0, *n_items[n + 1])
            if n + 1 < len(b_items):
                max_stage(1, b_items[n + 1][0])
            elif n_items:
                max_stage(0, n_items[0][0])
            exp_stage(1, g, vt_b, None, bool(n_items) and g == last)

    p_scr[...] = jnp.zeros(p_scr.shape, BF16)
    for g in range(ng):
        score_stage(0, 0, g, (0 - i) * bq)
    max_stage(0, 0)

    per_q = bq // bk
    first = i * per_q
    unroll = ATTN_UNROLL

    def body(u, carry):
        for k in range(unroll):
            iteration(unroll * u + k, None, None)
        return carry

    n_plain = jnp.maximum(first - 1, 0)
    trips = n_plain // unroll
    lax.fori_loop(0, trips, body, 0)
    for k in range(unroll - 1):
        @pl.when(trips * unroll + k < n_plain)
        def _():
            iteration(trips * unroll + k, None, None)

    @pl.when(i > 0)
    def _():
        iteration(first - 1, None, 0)

    for d in range(per_q):
        iteration(first + d, d * bk, (d + 1) * bk if d + 1 < per_q else "stop")

    lq = lam_ref[...]
    lam = (jnp.exp(jnp.sum(lq[0:1] * lq[1:2], axis=1, keepdims=True))
           - jnp.exp(jnp.sum(lq[2:3] * lq[3:4], axis=1, keepdims=True)) + lam_init)
    o = acc_scr[0:DA_VAL_DIM, :] / acc_scr[DA_VAL_DIM:DA_VAL_DIM + 1, :]
    o = o[:, :bq] - lam * o[:, bq:]
    ms = jnp.mean(o * o, axis=0, keepdims=True)
    o = o * lax.rsqrt(ms + EPS) * g_ref[...] * (1.0 - lam_init)
    o_ref[...] = o.T.astype(BF16)


def _diff_attention(qt, k, vt, lam_qk, attn_g, layer):
    S = k.shape[0]
    bk = ATTN_BLOCK
    bq = ATTN_Q_BLOCK
    nk = S // bk
    lam_init = 0.8 - 0.6 * math.exp(-0.3 * layer)
    return pl.pallas_call(
        functools.partial(_attn_kernel, lam_init=lam_init),
        grid=(DA_HEADS, S // bq),
        in_specs=[
            pl.BlockSpec((4, DA_HEAD_DIM), lambda h, i: (0, 0)),
            pl.BlockSpec((DA_VAL_DIM, 1), lambda h, i: (0, 0)),
            pl.BlockSpec((bq // bk, LANES, bk), lambda h, i: (i, h, 0)),
            pl.BlockSpec((S, LANES), lambda h, i: (0, h)),
            pl.BlockSpec((nk, ATTN_VT_ROWS, bk), lambda h, i: (0, h, 0)),
        ],
        out_specs=pl.BlockSpec((bq, DA_VAL_DIM), lambda h, i: (i, h)),
        out_shape=jax.ShapeDtypeStruct((S, DA_WIDTH), BF16),
        scratch_shapes=[
            pltpu.VMEM((1, 2 * bq), F32),
            pltpu.VMEM((1, 2 * bq), F32),
            pltpu.VMEM((1, 2 * 2 * bq), F32),
            pltpu.VMEM((ATTN_VT_ROWS, 2 * bq), F32),
            pltpu.VMEM((2 * (2 * bq // ATTN_QUERY_GROUP), bk // 2, ATTN_QUERY_GROUP), F32),
            pltpu.VMEM((bk // 2, ATTN_QUERY_GROUP), BF16),
        ],
        compiler_params=_params(("parallel", "parallel")),
        name="diff_attn",
    )(lam_qk, attn_g.reshape(DA_VAL_DIM, 1), qt, k, vt)


def _hgrn_kernel(lbl_ref, g_ref, hg_ref, o_ref, st_scr, stb_scr, o_scr, *, layer):
    T = hg_ref.shape[0]
    c = HG_CHUNK
    W = HG_WIDTH

    @pl.when(pl.program_id(0) == 0)
    def _():
        st_scr[...] = jnp.zeros(st_scr.shape, F32)
        stb_scr[...] = jnp.zeros(stb_scr.shape, BF16)

    lg = lbl_ref[...]
    ex = jnp.exp(lg - jnp.max(lg, axis=0, keepdims=True))
    prob = ex / jnp.sum(ex, axis=0, keepdims=True)
    lb = jnp.zeros((1, W), F32)
    for li in range(1, layer + 1):
        lb = lb + prob[li:li + 1]

    same_head = _head_block_mask(W, HG_DIM)
    ones_bd = same_head.astype(BF16)
    rowi = lax.broadcasted_iota(jnp.int32, (c, W), 0)

    def body(s, carry):
        sl = pl.ds(pl.multiple_of(s * c, c), c)
        qf = hg_ref[sl, 0:W]
        z = hg_ref[sl, W:2 * W]
        iv = hg_ref[sl, 2 * W:3 * W]
        f = lb + (1.0 - lb) * jax.nn.sigmoid(z)
        logf = jnp.log(jnp.maximum(f, TINY))
        key = (1.0 - lb) * jax.nn.sigmoid(-z)
        val = _silu(iv)
        b = logf
        sft = 1
        while sft < c:
            b = b + jnp.where(rowi >= sft, pltpu.roll(b, sft, 0), 0.0)
            sft *= 2
        b_last = b[c - 1:c]
        kd = (key * jnp.exp(b_last - b)).astype(BF16)
        upd = lax.dot_general(val.astype(BF16), kd, (((0,), (0,)), ((), ())),
                              preferred_element_type=F32)
        es = [(qf * key).astype(BF16)]
        vrs = [val]
        for d in range(1, c):
            kr = pltpu.roll(key, d, 0)
            br = pltpu.roll(b, d, 0)
            e = jnp.where(rowi >= d, qf * kr * jnp.exp(b - br), 0.0)
            es.append(e.astype(BF16))
            vrs.append(pltpu.roll(val, d, 0))
        estack = jnp.concatenate(es, axis=0)
        r = jnp.dot(estack, ones_bd, preferred_element_type=F32)
        o = r[0:c] * vrs[0]
        for d in range(1, c):
            o = o + r[d * c:(d + 1) * c] * vrs[d]
        qd = (qf * jnp.exp(b)).astype(BF16)
        o = o + lax.dot_general(qd, stb_scr[...], (((1,), (1,)), ((), ())),
                                preferred_element_type=F32)
        o_scr[sl, :] = o
        decay = jnp.exp(b_last)
        for hd in range(HG_HEADS):
            rws = slice(hd * HG_DIM, (hd + 1) * HG_DIM)
            lns = slice(hd * HG_DIM // LANES * LANES, (hd * HG_DIM // LANES + 1) * LANES)
            new = decay[:, lns] * st_scr[rws, lns] + jnp.where(same_head[rws, lns], upd[rws, lns], 0.0)
            st_scr[rws, lns] = new
            stb_scr[rws, lns] = new.astype(BF16)
        return carry

    lax.fori_loop(0, T // c, body, 0, unroll=HG_UNROLL)

    o = o_scr[...]
    ms = jnp.dot(o * o, same_head.astype(F32), preferred_element_type=F32,
                 precision=HIGHEST) * (1.0 / HG_DIM)
    gate = hg_ref[:, 3 * W:4 * W]
    o_ref[...] = (o * lax.rsqrt(ms + EPS) * g_ref[...] * _silu(gate)).astype(BF16)


def _hgrn2(hg, lb_logits, norm_g, layer):
    S = hg.shape[0]
    T = HG_BLOCK
    g_t = jnp.tile(norm_g.reshape(1, HG_DIM), (1, HG_HEADS))
    return pl.pallas_call(
        functools.partial(_hgrn_kernel, layer=layer),
        grid=(S // T,),
        in_specs=[
            pl.BlockSpec((DEPTH, HG_WIDTH), lambda i: (0, 0)),
            pl.BlockSpec((1, HG_WIDTH), lambda i: (0, 0)),
            pl.BlockSpec((T, 4 * HG_WIDTH), lambda i: (i, 0)),
        ],
        out_specs=pl.BlockSpec((T, HG_WIDTH), lambda i: (i, 0)),
        out_shape=jax.ShapeDtypeStruct((S, HG_WIDTH), BF16),
        scratch_shapes=[
            pltpu.VMEM((HG_WIDTH, HG_WIDTH), F32),
            pltpu.VMEM((HG_WIDTH, HG_WIDTH), BF16),
            pltpu.VMEM((T, HG_WIDTH), F32),
        ],
        compiler_params=_params(("arbitrary",)),
        name="hgrn2",
    )(lb_logits, g_t, hg)


def _pool_kernel(u_ref, w_ref, sc_ref, o_ref, halo_scr):
    T = u_ref.shape[0]
    H = POOL_HALO
    blk = pl.program_id(0)

    @pl.when(blk == 0)
    def _():
        halo_scr[...] = jnp.zeros(halo_scr.shape, F32)

    u = u_ref[...]
    ext = jnp.concatenate([halo_scr[...], u], axis=0)
    halo_scr[...] = u[T - H:, :]
    sums = []
    s = ext
    w = 1
    while w < POOL_WINDOWS[-1]:
        s = s + pltpu.roll(s, w, 0)
        w *= 2
        sums.append(s[H:, :])
    t1 = (blk * T + 1 + lax.broadcasted_iota(jnp.int32, (T, POOL_WIDTH), 0)).astype(F32)
    lane = lax.broadcasted_iota(jnp.int32, (T, POOL_WIDTH), 1)
    pooled = None
    for gi, win in enumerate(POOL_WINDOWS):
        mean = sums[gi] / jnp.minimum(t1, float(win))
        pooled = mean if pooled is None else jnp.where(lane >= gi * POOL_GROUP, mean, pooled)
    pooled = pooled - u
    y = jnp.dot(pooled.astype(BF16), w_ref[...], preferred_element_type=F32)
    o_ref[...] = (y * sc_ref[...]).astype(BF16)


def _multiscale_pool(pu, pool_w, pool_scale):
    S = pu.shape[0]
    T = ROW_TILE
    w_bd = jax.scipy.linalg.block_diag(*[pool_w[g] for g in range(len(POOL_WINDOWS))]).astype(BF16)
    return pl.pallas_call(
        _pool_kernel,
        grid=(S // T,),
        in_specs=[
            pl.BlockSpec((T, POOL_WIDTH), lambda i: (i, 0)),
            pl.BlockSpec((POOL_WIDTH, POOL_WIDTH), lambda i: (0, 0)),
            pl.BlockSpec((1, POOL_WIDTH), lambda i: (0, 0)),
        ],
        out_specs=pl.BlockSpec((T, POOL_WIDTH), lambda i: (i, 0)),
        out_shape=jax.ShapeDtypeStruct((S, POOL_WIDTH), BF16),
        scratch_shapes=[pltpu.VMEM((POOL_HALO, POOL_WIDTH), F32)],
        compiler_params=_params(("arbitrary",)),
        name="ms_pool",
    )(pu, w_bd, pool_scale.reshape(1, POOL_WIDTH))


def _outproj_kernel(a_ref, r_ref, p_ref, x_ref, wf_ref, g1_ref, lg_ref, lb_ref, o_ref, w_ref):
    @pl.when(pl.program_id(0) == 0)
    def _():
        w_ref[...] = wf_ref[0].astype(BF16)

    m = jnp.dot(a_ref[...], w_ref[0:DA_WIDTH, :], preferred_element_type=F32)
    m += jnp.dot(r_ref[...], w_ref[DA_WIDTH:DA_WIDTH + HG_WIDTH, :], preferred_element_type=F32)
    m += jnp.dot(p_ref[...], w_ref[DA_WIDTH + HG_WIDTH:, :], preferred_element_type=F32)
    y = ALPHA * x_ref[...] + (1.0 + g1_ref[...]) * m
    o_ref[...] = _layer_norm(y, lg_ref[...], lb_ref[...])


def _out_projection(a, r, p, x, w_out, layer, g1, ln_g, ln_b):
    S = x.shape[0]
    tm = ROW_TILE
    row = lambda i: (i, 0)
    fixed = lambda i: (0, 0)
    vec = pl.BlockSpec((1, D_MODEL), fixed)
    return pl.pallas_call(
        _outproj_kernel,
        grid=(S // tm,),
        in_specs=[
            pl.BlockSpec((tm, DA_WIDTH), row),
            pl.BlockSpec((tm, HG_WIDTH), row),
            pl.BlockSpec((tm, POOL_WIDTH), row),
            pl.BlockSpec((tm, D_MODEL), row),
            pl.BlockSpec((1, D_MIX, D_MODEL), lambda i: (layer, 0, 0), pipeline_mode=pl.Buffered(1)),
            vec, vec, vec,
        ],
        out_specs=pl.BlockSpec((tm, D_MODEL), row),
        out_shape=jax.ShapeDtypeStruct((S, D_MODEL), F32),
        scratch_shapes=[pltpu.VMEM((D_MIX, D_MODEL), BF16)],
        compiler_params=_params(("arbitrary",)),
        name="out_proj_ln",
    )(a, r, p, x, w_out, g1, ln_g.reshape(1, D_MODEL), ln_b.reshape(1, D_MODEL))


def _ffn_kernel(x_ref, sc_ref, sh_ref, g2_ref, w1_ref, w3_ref, w2_ref, lg_ref, lb_ref, *rest):
    n_cast = (len(rest) - 1) // 2
    o_ref = rest[n_cast]
    for src, dst in zip(rest[:n_cast], rest[n_cast + 1:]):
        dst[...] = src[...].astype(BF16)
    x = x_ref[...]
    h = (x * (1.0 + sc_ref[...]) + sh_ref[...]).astype(BF16)
    acc = jnp.zeros(x.shape, F32)
    step = 4 * MXU_DIM
    for lo in range(0, FFN_DIM, step):
        hi = min(lo + step, FFN_DIM)
        a = jnp.dot(h, w1_ref[:, lo:hi], preferred_element_type=F32)
        b = jnp.dot(h, w3_ref[:, lo:hi], preferred_element_type=F32)
        acc += jnp.dot((_silu(a) * b).astype(BF16), w2_ref[lo:hi, :], preferred_element_type=F32)
    y = ALPHA * x + (1.0 + g2_ref[...]) * acc
    o_ref[...] = _layer_norm(y, lg_ref[...], lb_ref[...])


def _dense_ffn(x, sc, sh, g2, w1, w3, w2, ln_g, ln_b, to_cast=()):
    S = x.shape[0]
    tm = FFN_CAST_TILE if to_cast else ROW_TILE
    steps = S // tm
    row = lambda i: (i, 0)
    fixed = lambda i: (0, 0)
    vec = pl.BlockSpec((1, D_MODEL), fixed)
    once = pl.Buffered(1)
    slabs = [a.reshape(-1, a.shape[-1]) for a in to_cast]
    slab_specs = [pl.BlockSpec((a.shape[0] // steps, a.shape[1]), row) for a in slabs]
    for a in slabs:
        assert a.shape[0] % (steps * 16) == 0
    outs = pl.pallas_call(
        _ffn_kernel,
        grid=(steps,),
        in_specs=[
            pl.BlockSpec((tm, D_MODEL), row),
            vec, vec, vec,
            pl.BlockSpec((D_MODEL, FFN_DIM), fixed, pipeline_mode=once),
            pl.BlockSpec((D_MODEL, FFN_DIM), fixed, pipeline_mode=once),
            pl.BlockSpec((FFN_DIM, D_MODEL), fixed, pipeline_mode=once),
            vec, vec,
        ] + slab_specs,
        out_specs=[pl.BlockSpec((tm, D_MODEL), row)] + slab_specs,
        out_shape=[jax.ShapeDtypeStruct((S, D_MODEL), F32)]
        + [jax.ShapeDtypeStruct(a.shape, BF16) for a in slabs],
        compiler_params=_params(("parallel",)),
        name="dense_ffn_ln",
    )(x, sc, sh, g2, w1, w3, w2, ln_g.reshape(1, D_MODEL), ln_b.reshape(1, D_MODEL), *slabs)
    return outs[0], [o.reshape(a.shape) for o, a in zip(outs[1:], to_cast)]


def _router_kernel(x_ref, sc_ref, sh_ref, rw_ref, h_ref, rank_ref, comb_ref, cum_ref):
    T = x_ref.shape[0]
    tc = MOE_CHUNK
    E = N_EXPERTS
    h = x_ref[...] * (1.0 + sc_ref[...]) + sh_ref[...]
    h_ref[...] = h.astype(BF16)
    logits = lax.dot_general(rw_ref[...], h, (((1,), (1,)), ((), ())),
                             preferred_element_type=F32, precision=HIGHEST)
    eidx = lax.broadcasted_iota(jnp.int32, (E, T), 0)
    v1 = jnp.max(logits, axis=0, keepdims=True)
    i1 = jnp.min(jnp.where(logits == v1, eidx, E), axis=0, keepdims=True)
    m1 = eidx == i1
    rest = jnp.where(m1, -jnp.inf, logits)
    v2 = jnp.max(rest, axis=0, keepdims=True)
    i2 = jnp.min(jnp.where(rest == v2, eidx, E), axis=0, keepdims=True)
    m2 = eidx == i2
    e2 = jnp.exp(v2 - v1)
    g1 = 1.0 / (1.0 + e2)
    g2 = e2 / (1.0 + e2)
    comb = jnp.where(m1, g1, 0.0) + jnp.where(m2, g2, 0.0)
    routed = jnp.logical_or(m1, m2)
    ind = routed.astype(BF16)
    r_i = lax.broadcasted_iota(jnp.int32, (tc, tc), 0)
    c_i = lax.broadcasted_iota(jnp.int32, (tc, tc), 1)
    strict = (r_i < c_i).astype(BF16)
    lane = lax.broadcasted_iota(jnp.int32, (E, LANES), 1)
    running = jnp.zeros((E, 1), F32)
    cum = jnp.zeros((E, LANES), F32)
    for ci in range(T // tc):
        sl = slice(ci * tc, (ci + 1) * tc)
        ind_c = ind[:, sl]
        rank_c = jnp.dot(ind_c, strict, preferred_element_type=F32) + running
        rank_ref[:, ci] = jnp.where(routed[:, sl], rank_c, -1.0).reshape(E, 1, tc)
        comb_ref[:, ci] = comb[:, sl].reshape(E, 1, tc)
        cum = jnp.where(lane == ci, running, cum)
        running = running + jnp.sum(ind_c.astype(F32), axis=1, keepdims=True)
    cum = jnp.where(lane == T // tc, running, cum)
    cum_ref[0] = cum.astype(jnp.int32)


def _router(x, sc, sh, router_w):
    S = x.shape[0]
    T = MOE_BLOCK
    E = N_EXPERTS
    fixed = lambda i: (0, 0)
    vec = pl.BlockSpec((1, D_MODEL), fixed)
    return pl.pallas_call(
        _router_kernel,
        grid=(S // T,),
        in_specs=[
            pl.BlockSpec((T, D_MODEL), lambda i: (i, 0)),
            vec, vec,
            pl.BlockSpec((E, D_MODEL), fixed),
        ],
        out_specs=[
            pl.BlockSpec((T, D_MODEL), lambda i: (i, 0)),
            pl.BlockSpec((E, T // MOE_CHUNK, 1, MOE_CHUNK), lambda i: (0, i, 0, 0)),
            pl.BlockSpec((E, T // MOE_CHUNK, 1, MOE_CHUNK), lambda i: (0, i, 0, 0)),
            pl.BlockSpec((1, E, LANES), lambda i: (i, 0, 0)),
        ],
        out_shape=[
            jax.ShapeDtypeStruct((S, D_MODEL), BF16),
            jax.ShapeDtypeStruct((E, S // MOE_CHUNK, 1, MOE_CHUNK), F32),
            jax.ShapeDtypeStruct((E, S // MOE_CHUNK, 1, MOE_CHUNK), F32),
            jax.ShapeDtypeStruct((S // T, E, LANES), jnp.int32),
        ],
        compiler_params=_params(("parallel",)),
        name="moe_router",
    )(x, sc, sh, router_w.T)


def _moe_kernel(cum_ref, h_ref, rank_ref, comb_ref, w1_ref, w3_ref, w2_ref, y_ref, xe_scr, gate_scr,
                out_scr):
    T = h_ref.shape[0]
    tr, tc, win = MOE_TILE, MOE_CHUNK, MOE_WINDOW
    nchunk = T // tc
    b = pl.program_id(0)
    e = pl.program_id(1)
    f = pl.program_id(2)

    @pl.when(jnp.logical_and(e == 0, f == 0))
    def _():
        y_ref[...] = jnp.zeros(y_ref.shape, F32)

    base = (b * N_EXPERTS + e) * LANES
    count = cum_ref[base + nchunk]
    ntiles = (count + (tr - 1)) // tr

    def token_row(ref, first, n):
        blk = ref[0, pl.ds(first, n)]
        return jnp.concatenate([blk[c] for c in range(n)], axis=1)

    def tile(r, carry):
        lo = r * tr
        rows = pl.ds(pl.multiple_of(lo, 16), tr)

        def hits(first, n):
            want = (lo + lax.broadcasted_iota(jnp.int32, (tr, n * tc), 0)).astype(F32)
            return token_row(rank_ref, first, n) == want

        def token_span(first, n):
            start = first * tc
            if not isinstance(start, int):
                start = pl.multiple_of(start, tc)
            return pl.ds(start, n * tc)

        c_lo = jnp.int32(0)
        c_hi = jnp.int32(-1)
        for c in range(nchunk):
            c_lo += (cum_ref[base + c + 1] <= lo).astype(jnp.int32)
            c_hi += (cum_ref[base + c] < lo + tr).astype(jnp.int32)
        fits = c_hi - c_lo < win
        window = (jnp.minimum(c_lo, nchunk - win), win)
        block = (0, nchunk)

        def gather(first, n):
            hit = hits(first, n)
            xe_scr[rows, :] = jnp.dot(hit.astype(BF16), h_ref[token_span(first, n), :],
                                      preferred_element_type=F32).astype(BF16)
            gate_scr[rows, :] = jnp.sum(jnp.where(hit, token_row(comb_ref, first, n), 0.0),
                                        axis=1, keepdims=True)

        @pl.when(jnp.logical_and(f == 0, fits))
        def _():
            gather(*window)

        @pl.when(jnp.logical_and(f == 0, jnp.logical_not(fits)))
        def _():
            gather(*block)

        xe = xe_scr[rows, :]
        a = jnp.dot(xe, w1_ref[0], preferred_element_type=F32)
        g = jnp.dot(xe, w3_ref[0], preferred_element_type=F32)
        mid = (_silu(a) * g).astype(BF16)
        part = jnp.dot(mid, w2_ref[0], preferred_element_type=F32)

        @pl.when(f == 0)
        def _():
            out_scr[rows, :] = part

        if MOE_FSPLIT > 2:
            @pl.when(jnp.logical_and(f > 0, f < MOE_FSPLIT - 1))
            def _():
                out_scr[rows, :] += part

        @pl.when(f == MOE_FSPLIT - 1)
        def _():
            out_bf = ((out_scr[rows, :] + part) * gate_scr[rows, :]).astype(BF16)

            def scatter(first, n):
                y_ref[token_span(first, n), :] += lax.dot_general(
                    hits(first, n).astype(BF16), out_bf, (((0,), (0,)), ((), ())),
                    preferred_element_type=F32)

            @pl.when(fits)
            def _():
                scatter(*window)

            @pl.when(jnp.logical_not(fits))
            def _():
                scatter(*block)

        return carry

    lax.fori_loop(0, ntiles, tile, 0)


def _moe_ffn(h_bf, rank, comb, cum, w1, w3, w2):
    S = h_bf.shape[0]
    T = MOE_BLOCK
    fw = EXPERT_DIM // MOE_FSPLIT
    grid_spec = pltpu.PrefetchScalarGridSpec(
        num_scalar_prefetch=1,
        grid=(S // T, N_EXPERTS, MOE_FSPLIT),
        in_specs=[
            pl.BlockSpec((T, D_MODEL), lambda b, e, f, c: (b, 0), pipeline_mode=pl.Buffered(1)),
            pl.BlockSpec((1, T // MOE_CHUNK, 1, MOE_CHUNK), lambda b, e, f, c: (e, b, 0, 0)),
            pl.BlockSpec((1, T // MOE_CHUNK, 1, MOE_CHUNK), lambda b, e, f, c: (e, b, 0, 0)),
            pl.BlockSpec((1, D_MODEL, fw), lambda b, e, f, c: (e, 0, f)),
            pl.BlockSpec((1, D_MODEL, fw), lambda b, e, f, c: (e, 0, f)),
            pl.BlockSpec((1, fw, D_MODEL), lambda b, e, f, c: (e, f, 0)),
        ],
        out_specs=pl.BlockSpec((T, D_MODEL), lambda b, e, f, c: (b, 0)),
        scratch_shapes=[
            pltpu.VMEM((T + MOE_TILE, D_MODEL), BF16),
            pltpu.VMEM((T + MOE_TILE, 1), F32),
            pltpu.VMEM((T + MOE_TILE, D_MODEL), F32),
        ],
    )
    return pl.pallas_call(
        _moe_kernel,
        grid_spec=grid_spec,
        out_shape=jax.ShapeDtypeStruct((S, D_MODEL), F32),
        compiler_params=_params(("arbitrary", "arbitrary", "arbitrary")),
        name="moe_experts",
    )(cum.reshape(-1), h_bf, rank, comb, w1, w3, w2)


def _ln_kernel(x_ref, y_ref, g2_ref, lg_ref, lb_ref, o_ref):
    y = ALPHA * x_ref[...] + (1.0 + g2_ref[...]) * y_ref[...]
    o_ref[...] = _layer_norm(y, lg_ref[...], lb_ref[...])


def _residual_ln(x, y, g2, ln_g, ln_b):
    S = x.shape[0]
    tm = ROW_TILE
    row = lambda i: (i, 0)
    vec = pl.BlockSpec((1, D_MODEL), lambda i: (0, 0))
    return pl.pallas_call(
        _ln_kernel,
        grid=(S // tm,),
        in_specs=[pl.BlockSpec((tm, D_MODEL), row), pl.BlockSpec((tm, D_MODEL), row), vec, vec, vec],
        out_specs=pl.BlockSpec((tm, D_MODEL), row),
        out_shape=jax.ShapeDtypeStruct((S, D_MODEL), F32),
        compiler_params=_params(("parallel",)),
        name="residual_ln",
    )(x, y, g2, ln_g.reshape(1, D_MODEL), ln_b.reshape(1, D_MODEL))


def _rope_tables(S):
    d = DA_HEAD_DIM
    tm = ATTN_BLOCK
    inv_freq = 1.0 / (ROPE_THETA ** (jnp.arange(0, d, 2, dtype=F32) / d))
    inv_lane = inv_freq[jnp.arange(LANES) % (d // 2)][None, :]
    base = (jnp.arange(S // tm, dtype=F32) * tm)[:, None] * inv_lane
    offs = jnp.arange(tm, dtype=F32)[:, None] * inv_lane
    shape_b = (S // tm, 1, LANES)
    return jnp.cos(base).reshape(shape_b), jnp.sin(base).reshape(shape_b), jnp.cos(offs), jnp.sin(offs)


def kernel(x, c, w_ada, b_ada, w_in, lam_qk, attn_norm_g, hg_lb_logits, hg_norm_g, pool_w, pool_scale, w_out, ln1_g, ln1_b, ln2_g, ln2_b, ffn_w1, ffn_w3, ffn_w2, router_w, exp_w1, exp_w3, exp_w2):
    B, S, D = x.shape
    assert B == 1 and D == D_MODEL and S % MOE_BLOCK == 0
    xs = x.reshape(S, D)
    mod = _modulation(c, w_ada, b_ada)
    rope = _rope_tables(S)
    experts_bf = {}
    for l in range(DEPTH):
        sh1, sc1, g1, sh2, sc2, g2 = (mod[l, j] for j in range(6))
        qt, k, vt, hg, pu = _in_projection(xs, sc1, sh1, w_in, l, rope)
        a = _diff_attention(qt, k, vt, lam_qk[l], attn_norm_g[l], l)
        r = _hgrn2(hg, hg_lb_logits, hg_norm_g[l], l)
        p = _multiscale_pool(pu, pool_w[l], pool_scale[l])
        xs = _out_projection(a, r, p, xs, w_out, l, g1, ln1_g[l], ln1_b[l])
        if l % 2 == 0:
            i = l // 2
            ride = (exp_w1[i], exp_w3[i], exp_w2[i]) if l + 1 < DEPTH else ()
            xs, cast = _dense_ffn(xs, sc2, sh2, g2, ffn_w1[i].astype(BF16), ffn_w3[i].astype(BF16),
                                  ffn_w2[i].astype(BF16), ln2_g[l], ln2_b[l], ride)
            if ride:
                experts_bf[i] = cast
        else:
            i = l // 2
            h_bf, rank, comb, cum = _router(xs, sc2, sh2, router_w[i])
            w1, w3, w2 = experts_bf.get(i) or [w[i].astype(BF16) for w in (exp_w1, exp_w3, exp_w2)]
            y = _moe_ffn(h_bf, rank, comb, cum, w1, w3, w2)
            xs = _residual_ln(xs, y, g2, ln2_g[l], ln2_b[l])
    return xs.reshape(B, S, D)
```

```python
import functools
import math

import jax
import jax.numpy as jnp
from jax import lax
from jax.experimental import pallas as pl
from jax.experimental.pallas import tpu as pltpu

F32 = jnp.float32
BF16 = jnp.bfloat16
HIGHEST = lax.Precision.HIGHEST

D_MODEL = 1024
DEPTH = 2
DA_HEADS = 4
DA_WIDTH = D_MODEL // 2
DA_HEAD_DIM = DA_WIDTH // (2 * DA_HEADS)
DA_VAL_DIM = 2 * DA_HEAD_DIM
HG_HEADS = 4
HG_WIDTH = D_MODEL // 4
HG_DIM = HG_WIDTH // HG_HEADS
POOL_WINDOWS = (2, 4, 8, 16)
POOL_WIDTH = D_MODEL // 4
POOL_GROUP = POOL_WIDTH // len(POOL_WINDOWS)
D_MIX = DA_WIDTH + HG_WIDTH + POOL_WIDTH
D_IN = 3 * DA_WIDTH + 4 * HG_WIDTH + POOL_WIDTH
ROPE_THETA = 10000.0
MASK_VALUE = -1e30
TINY = 1e-30
FFN_DIM = 11 * D_MODEL // 4
N_EXPERTS = 8
EXPERT_DIM = 7 * D_MODEL // 2
ALPHA = (2 * DEPTH) ** 0.25
EPS = 1e-5
LOG2_E = math.log2(math.e)

LANES = 128
MXU_DIM = 256
VMEM_LIMIT = 60 * 1024 * 1024

ROW_TILE = 512
FFN_CAST_TILE = 256
ATTN_BLOCK = 512
ATTN_Q_BLOCK = 1024
ATTN_QUERY_GROUP = 256
ATTN_UNROLL = 4
ATTN_SUM_ROWS = 16
ATTN_VT_ROWS = DA_VAL_DIM + ATTN_SUM_ROWS
HG_BLOCK = 512
HG_CHUNK = 16
HG_UNROLL = 4
POOL_HALO = 16
MOE_BLOCK = 2048
MOE_TILE = 192
MOE_CHUNK = 256
MOE_WINDOW = 5
MOE_FSPLIT = 2


def _params(sem, vmem=VMEM_LIMIT):
    return pltpu.CompilerParams(dimension_semantics=sem, vmem_limit_bytes=vmem)


def _layer_norm(y, g, b):
    mu = jnp.mean(y, axis=-1, keepdims=True)
    d = y - mu
    var = jnp.mean(d * d, axis=-1, keepdims=True)
    return d * lax.rsqrt(var + EPS) * g + b


def _silu(x):
    return x * jax.nn.sigmoid(x)


def _head_block_mask(n, group):
    r = lax.broadcasted_iota(jnp.int32, (n, n), 0) // group
    c = lax.broadcasted_iota(jnp.int32, (n, n), 1) // group
    return r == c


def _mod_kernel(c_ref, w_ref, b_ref, o_ref):
    cond = _silu(c_ref[...])
    cond8 = jnp.broadcast_to(cond, (8, D_MODEL))
    r = jnp.dot(cond8, w_ref[0], preferred_element_type=F32, precision=HIGHEST)
    o_ref[0] = r[0:1] + b_ref[0]


def _modulation(c, w_ada, b_ada):
    n = 6 * DEPTH
    b3 = b_ada.reshape(n, 1, D_MODEL)
    out = pl.pallas_call(
        _mod_kernel,
        grid=(DEPTH, 6),
        in_specs=[
            pl.BlockSpec((1, D_MODEL), lambda l, j: (0, 0)),
            pl.BlockSpec((1, D_MODEL, D_MODEL), lambda l, j: (l, 0, j)),
            pl.BlockSpec((1, 1, D_MODEL), lambda l, j: (l * 6 + j, 0, 0)),
        ],
        out_specs=pl.BlockSpec((1, 1, D_MODEL), lambda l, j: (l * 6 + j, 0, 0)),
        out_shape=jax.ShapeDtypeStruct((n, 1, D_MODEL), F32),
        compiler_params=_params(("parallel", "parallel")),
        name="adaln_mod",
    )(c, w_ada, b3)
    return out.reshape(DEPTH, 6, 1, D_MODEL)


def _inproj_kernel(x_ref, sc_ref, sh_ref, wf_ref, cosb_ref, sinb_ref, cosr_ref, sinr_ref,
                   qt_ref, k_ref, vt_ref, hg_ref, pu_ref, w_ref, wvt_ref):
    @pl.when(pl.program_id(0) == 0)
    def _():
        w_ref[...] = wf_ref[0].astype(BF16)
        wvt_ref[...] = wf_ref[0, :, 2 * DA_WIDTH:3 * DA_WIDTH].T.astype(BF16)

    h = (x_ref[...] * (1.0 + sc_ref[...]) + sh_ref[...]).astype(BF16)
    qk = jnp.dot(h, w_ref[:, : 2 * DA_WIDTH], preferred_element_type=F32)
    cb, sb, cr, sr = cosb_ref[0], sinb_ref[0], cosr_ref[...], sinr_ref[...]
    cos128 = cb * cr - sb * sr
    sin128 = sb * cr + cb * sr
    half = DA_HEAD_DIM // 2
    lane128 = lax.broadcasted_iota(jnp.int32, cos128.shape, 1)
    sin128 = jnp.where((lane128 & half) == 0, -sin128, sin128)
    reps = 2 * DA_WIDTH // LANES
    cos = jnp.tile(cos128, (1, reps))
    sin = jnp.tile(sin128, (1, reps))
    lane = lax.broadcasted_iota(jnp.int32, qk.shape, 1)
    first_half = (lane & half) == 0
    width = 2 * DA_WIDTH
    partner = jnp.where(first_half, pltpu.roll(qk, width - half, 1), pltpu.roll(qk, half, 1))
    rot = qk * cos + partner * sin
    qt_ref[0] = (rot[:, :DA_WIDTH] * (DA_HEAD_DIM ** -0.5 * LOG2_E)).T.astype(BF16)
    k_ref[...] = rot[:, DA_WIDTH:].astype(BF16)
    o = 2 * DA_WIDTH
    vt = lax.dot_general(wvt_ref[...], h, (((1,), (1,)), ((), ())),
                         preferred_element_type=F32).astype(BF16)
    ones = jnp.ones((ATTN_SUM_ROWS, vt.shape[1]), BF16)
    pieces = []
    for hd in range(DA_HEADS):
        pieces += [vt[hd * DA_VAL_DIM:(hd + 1) * DA_VAL_DIM], ones]
    vt_ref[0] = jnp.concatenate(pieces, axis=0)
    o += DA_WIDTH
    hg_ref[...] = jnp.dot(h, w_ref[:, o:o + 4 * HG_WIDTH], preferred_element_type=F32)
    o += 4 * HG_WIDTH
    pu_ref[...] = jnp.dot(h, w_ref[:, o:o + POOL_WIDTH], preferred_element_type=F32)


def _in_projection(x, sc, sh, w_in, layer, rope):
    S = x.shape[0]
    tm = ATTN_BLOCK
    row = lambda i: (i, 0)
    fixed = lambda i: (0, 0)
    cos_b, sin_b, cos_r, sin_r = rope
    return pl.pallas_call(
        _inproj_kernel,
        grid=(S // tm,),
        in_specs=[
            pl.BlockSpec((tm, D_MODEL), row),
            pl.BlockSpec((1, D_MODEL), fixed),
            pl.BlockSpec((1, D_MODEL), fixed),
            pl.BlockSpec((1, D_MODEL, D_IN), lambda i: (layer, 0, 0), pipeline_mode=pl.Buffered(1)),
            pl.BlockSpec((1, 1, LANES), lambda i: (i, 0, 0)),
            pl.BlockSpec((1, 1, LANES), lambda i: (i, 0, 0)),
            pl.BlockSpec((tm, LANES), fixed),
            pl.BlockSpec((tm, LANES), fixed),
        ],
        out_specs=[
            pl.BlockSpec((1, DA_WIDTH, tm), lambda i: (i, 0, 0)),
            pl.BlockSpec((tm, DA_WIDTH), row),
            pl.BlockSpec((1, DA_HEADS * ATTN_VT_ROWS, tm), lambda i: (i, 0, 0)),
            pl.BlockSpec((tm, 4 * HG_WIDTH), row),
            pl.BlockSpec((tm, POOL_WIDTH), row),
        ],
        out_shape=[
            jax.ShapeDtypeStruct((S // tm, DA_WIDTH, tm), BF16),
            jax.ShapeDtypeStruct((S, DA_WIDTH), BF16),
            jax.ShapeDtypeStruct((S // tm, DA_HEADS * ATTN_VT_ROWS, tm), BF16),
            jax.ShapeDtypeStruct((S, 4 * HG_WIDTH), F32),
            jax.ShapeDtypeStruct((S, POOL_WIDTH), F32),
        ],
        scratch_shapes=[
            pltpu.VMEM((D_MODEL, D_IN), BF16),
            pltpu.VMEM((DA_WIDTH, D_MODEL), BF16),
        ],
        compiler_params=_params(("arbitrary",)),
        name="in_proj",
    )(x, sc, sh, w_in, cos_b, sin_b, cos_r, sin_r)


def _attn_kernel(lam_ref, g_ref, qt_ref, k_ref, vt_ref, o_ref, m_scr, a_scr, c_scr, acc_scr, s_scr,
                 p_scr, *, lam_init):
    bq = qt_ref.shape[0] * qt_ref.shape[2]
    bk = vt_ref.shape[2]
    cg = ATTN_QUERY_GROUP
    i = pl.program_id(1)
    qt = jnp.concatenate([qt_ref[n] for n in range(qt_ref.shape[0])], axis=1)
    rowq = lax.broadcasted_iota(jnp.int32, qt.shape, 0)
    zero = jnp.zeros_like(qt)
    maps = (jnp.where(rowq < DA_HEAD_DIM, qt, zero), jnp.where(rowq >= DA_HEAD_DIM, qt, zero))
    groups = [(a * bq + c0, c0, maps[a][:, c0:c0 + cg]) for a in range(2) for c0 in range(0, bq, cg)]
    m_scr[...] = jnp.full(m_scr.shape, -jnp.inf, F32)
    acc_scr[...] = jnp.zeros(acc_scr.shape, F32)
    hk = bk // 2
    key_i = lax.broadcasted_iota(jnp.int32, (hk, cg), 0)
    qry_i = lax.broadcasted_iota(jnp.int32, (hk, cg), 1)
    ng = len(groups)
    last = ng - 1

    def cols_of(g):
        return slice(groups[g][0], groups[g][0] + cg)

    def ccols_of(sset, g):
        return slice(sset * 2 * bq + groups[g][0], sset * 2 * bq + groups[g][0] + cg)

    def mask_mode(key_offset, g):
        c0 = groups[g][1]
        if key_offset + hk - 1 <= c0:
            return None
        if key_offset > c0 + cg - 1:
            return "skip"
        return key_offset

    def score_stage(hb, sset, g, key_offset):
        kb = k_ref[pl.ds(pl.multiple_of(hb * hk, hk), hk), :]
        s = jnp.dot(kb, groups[g][2], preferred_element_type=F32)
        if key_offset is not None:
            s = jnp.where(key_i + key_offset <= qry_i + groups[g][1], s, MASK_VALUE)
        s_scr[sset * ng + g] = s
        c_scr[:, ccols_of(sset, g)] = jnp.max(s, axis=0, keepdims=True)

    def max_stage(sset, g):
        cols = cols_of(g)
        m_prev = m_scr[:, cols]
        m_new = jnp.maximum(m_prev, c_scr[:, ccols_of(sset, g)])
        a_scr[:, cols] = jnp.exp2(m_prev - m_new)
        m_scr[:, cols] = m_new

    def exp_stage(sset, g, vt_half, carried, defer):
        cols = cols_of(g)
        p = jnp.exp2(s_scr[sset * ng + g] - m_scr[:, cols])
        alpha = a_scr[:, cols]
        pb = p.astype(BF16)
        acc = acc_scr[:, cols]
        if carried is not None:
            acc = acc + carried
        if defer:
            acc_scr[:, cols] = alpha * acc
            p_scr[...] = pb
        else:
            acc_scr[:, cols] = alpha * acc + jnp.dot(vt_half, pb, preferred_element_type=F32)

    def items(key_offset):
        if key_offset is None:
            return [(g, None) for g in range(ng)]
        return [(g, mask_mode(key_offset, g)) for g in range(ng) if mask_mode(key_offset, g) != "skip"]

    def iteration(t, offs, look):
        vt_a = vt_ref[t, :, 0:hk]
        vt_b = vt_ref[t, :, hk:bk]
        carried = jnp.dot(vt_ref[jnp.maximum(t - 1, 0), :, hk:bk], p_scr[...],
                          preferred_element_type=F32)
        a_items = items(offs)
        b_items = items(None if offs is None else offs + hk)
        n_items = [] if look == "stop" else items(look)
        score_stage(2 * t + 1, 1, *b_items[0])
        for n, (g, _) in enumerate(a_items):
            if n + 1 < len(b_items):
                score_stage(2 * t + 1, 1, *b_items[n + 1])
            if n + 1 < len(a_items):
                max_stage(0, a_items[n + 1][0])
            else:
                max_stage(1, b_items[0][0])
            exp_stage(0, g, vt_a, carried if g == last else None, False)
        if n_items:
            score_stage(2 * t + 2, 0, *n_items[0])
        for n, (g, _) in enumerate(b_items):
            if n + 1 < len(n_items):
                score_stage(2 * t + 2, 0, *n_items[n + 1])
            if n + 1 < len(b_items):
                max_stage(1, b_items[n + 1][0])
            elif n_items:
                max_stage(0, n_items[0][0])
            exp_stage(1, g, vt_b, None, bool(n_items) and g == last)

    p_scr[...] = jnp.zeros(p_scr.shape, BF16)
    for g in range(ng):
        score_stage(0, 0, g, (0 - i) * bq)
    max_stage(0, 0)

    per_q = bq // bk
    first = i * per_q
    unroll = ATTN_UNROLL

    def body(u, carry):
        for k in range(unroll):
            iteration(unroll * u + k, None, None)
        return carry

    n_plain = jnp.maximum(first - 1, 0)
    trips = n_plain // unroll
    lax.fori_loop(0, trips, body, 0)
    for k in range(unroll - 1):
        @pl.when(trips * unroll + k < n_plain)
        def _():
            iteration(trips * unroll + k, None, None)

    @pl.when(i > 0)
    def _():
        iteration(first - 1, None, 0)

    for d in range(per_q):
        iteration(first + d, d * bk, (d + 1) * bk if d + 1 < per_q else "stop")

    lq = lam_ref[...]
    lam = (jnp.exp(jnp.sum(lq[0:1] * lq[1:2], axis=1, keepdims=True))
           - jnp.exp(jnp.sum(lq[2:3] * lq[3:4], axis=1, keepdims=True)) + lam_init)
    o = acc_scr[0:DA_VAL_DIM, :] / acc_scr[DA_VAL_DIM:DA_VAL_DIM + 1, :]
    o = o[:, :bq] - lam * o[:, bq:]
    ms = jnp.mean(o * o, axis=0, keepdims=True)
    o = o * lax.rsqrt(ms + EPS) * g_ref[...] * (1.0 - lam_init)
    o_ref[...] = o.T.astype(BF16)


def _diff_attention(qt, k, vt, lam_qk, attn_g, layer):
    S = k.shape[0]
    bk = ATTN_BLOCK
    bq = ATTN_Q_BLOCK
    nk = S // bk
    lam_init = 0.8 - 0.6 * math.exp(-0.3 * layer)
    return pl.pallas_call(
        functools.partial(_attn_kernel, lam_init=lam_init),
        grid=(DA_HEADS, S // bq),
        in_specs=[
            pl.BlockSpec((4, DA_HEAD_DIM), lambda h, i: (0, 0)),
            pl.BlockSpec((DA_VAL_DIM, 1), lambda h, i: (0, 0)),
            pl.BlockSpec((bq // bk, LANES, bk), lambda h, i: (i, h, 0)),
            pl.BlockSpec((S, LANES), lambda h, i: (0, h)),
            pl.BlockSpec((nk, ATTN_VT_ROWS, bk), lambda h, i: (0, h, 0)),
        ],
        out_specs=pl.BlockSpec((bq, DA_VAL_DIM), lambda h, i: (i, h)),
        out_shape=jax.ShapeDtypeStruct((S, DA_WIDTH), BF16),
        scratch_shapes=[
            pltpu.VMEM((1, 2 * bq), F32),
            pltpu.VMEM((1, 2 * bq), F32),
            pltpu.VMEM((1, 2 * 2 * bq), F32),
            pltpu.VMEM((ATTN_VT_ROWS, 2 * bq), F32),
            pltpu.VMEM((2 * (2 * bq // ATTN_QUERY_GROUP), bk // 2, ATTN_QUERY_GROUP), F32),
            pltpu.VMEM((bk // 2, ATTN_QUERY_GROUP), BF16),
        ],
        compiler_params=_params(("parallel", "parallel")),
        name="diff_attn",
    )(lam_qk, attn_g.reshape(DA_VAL_DIM, 1), qt, k, vt)


def _hgrn_kernel(lbl_ref, g_ref, hg_ref, o_ref, st_scr, stb_scr, o_scr, *, layer):
    T = hg_ref.shape[0]
    c = HG_CHUNK
    W = HG_WIDTH

    @pl.when(pl.program_id(0) == 0)
    def _():
        st_scr[...] = jnp.zeros(st_scr.shape, F32)
        stb_scr[...] = jnp.zeros(stb_scr.shape, BF16)

    lg = lbl_ref[...]
    ex = jnp.exp(lg - jnp.max(lg, axis=0, keepdims=True))
    prob = ex / jnp.sum(ex, axis=0, keepdims=True)
    lb = jnp.zeros((1, W), F32)
    for li in range(1, layer + 1):
        lb = lb + prob[li:li + 1]

    same_head = _head_block_mask(W, HG_DIM)
    ones_bd = same_head.astype(BF16)
    rowi = lax.broadcasted_iota(jnp.int32, (c, W), 0)

    def body(s, carry):
        sl = pl.ds(pl.multiple_of(s * c, c), c)
        qf = hg_ref[sl, 0:W]
        z = hg_ref[sl, W:2 * W]
        iv = hg_ref[sl, 2 * W:3 * W]
        f = lb + (1.0 - lb) * jax.nn.sigmoid(z)
        logf = jnp.log(jnp.maximum(f, TINY))
        key = (1.0 - lb) * jax.nn.sigmoid(-z)
        val = _silu(iv)
        b = logf
        sft = 1
        while sft < c:
            b = b + jnp.where(rowi >= sft, pltpu.roll(b, sft, 0), 0.0)
            sft *= 2
        b_last = b[c - 1:c]
        kd = (key * jnp.exp(b_last - b)).astype(BF16)
        upd = lax.dot_general(val.astype(BF16), kd, (((0,), (0,)), ((), ())),
                              preferred_element_type=F32)
        es = [(qf * key).astype(BF16)]
        vrs = [val]
        for d in range(1, c):
            kr = pltpu.roll(key, d, 0)
            br = pltpu.roll(b, d, 0)
            e = jnp.where(rowi >= d, qf * kr * jnp.exp(b - br), 0.0)
            es.append(e.astype(BF16))
            vrs.append(pltpu.roll(val, d, 0))
        estack = jnp.concatenate(es, axis=0)
        r = jnp.dot(estack, ones_bd, preferred_element_type=F32)
        o = r[0:c] * vrs[0]
        for d in range(1, c):
            o = o + r[d * c:(d + 1) * c] * vrs[d]
        qd = (qf * jnp.exp(b)).astype(BF16)
        o = o + lax.dot_general(qd, stb_scr[...], (((1,), (1,)), ((), ())),
                                preferred_element_type=F32)
        o_scr[sl, :] = o
        decay = jnp.exp(b_last)
        for hd in range(HG_HEADS):
            rws = slice(hd * HG_DIM, (hd + 1) * HG_DIM)
            lns = slice(hd * HG_DIM // LANES * LANES, (hd * HG_DIM // LANES + 1) * LANES)
            new = decay[:, lns] * st_scr[rws, lns] + jnp.where(same_head[rws, lns], upd[rws, lns], 0.0)
            st_scr[rws, lns] = new
            stb_scr[rws, lns] = new.astype(BF16)
        return carry

    lax.fori_loop(0, T // c, body, 0, unroll=HG_UNROLL)

    o = o_scr[...]
    ms = jnp.dot(o * o, same_head.astype(F32), preferred_element_type=F32,
                 precision=HIGHEST) * (1.0 / HG_DIM)
    gate = hg_ref[:, 3 * W:4 * W]
    o_ref[...] = (o * lax.rsqrt(ms + EPS) * g_ref[...] * _silu(gate)).astype(BF16)


def _hgrn2(hg, lb_logits, norm_g, layer):
    S = hg.shape[0]
    T = HG_BLOCK
    g_t = jnp.tile(norm_g.reshape(1, HG_DIM), (1, HG_HEADS))
    return pl.pallas_call(
        functools.partial(_hgrn_kernel, layer=layer),
        grid=(S // T,),
        in_specs=[
            pl.BlockSpec((DEPTH, HG_WIDTH), lambda i: (0, 0)),
            pl.BlockSpec((1, HG_WIDTH), lambda i: (0, 0)),
            pl.BlockSpec((T, 4 * HG_WIDTH), lambda i: (i, 0)),
        ],
        out_specs=pl.BlockSpec((T, HG_WIDTH), lambda i: (i, 0)),
        out_shape=jax.ShapeDtypeStruct((S, HG_WIDTH), BF16),
        scratch_shapes=[
            pltpu.VMEM((HG_WIDTH, HG_WIDTH), F32),
            pltpu.VMEM((HG_WIDTH, HG_WIDTH), BF16),
            pltpu.VMEM((T, HG_WIDTH), F32),
        ],
        compiler_params=_params(("arbitrary",)),
        name="hgrn2",
    )(lb_logits, g_t, hg)


def _pool_kernel(u_ref, w_ref, sc_ref, o_ref, halo_scr):
    T = u_ref.shape[0]
    H = POOL_HALO
    blk = pl.program_id(0)

    @pl.when(blk == 0)
    def _():
        halo_scr[...] = jnp.zeros(halo_scr.shape, F32)

    u = u_ref[...]
    ext = jnp.concatenate([halo_scr[...], u], axis=0)
    halo_scr[...] = u[T - H:, :]
    sums = []
    s = ext
    w = 1
    while w < POOL_WINDOWS[-1]:
        s = s + pltpu.roll(s, w, 0)
        w *= 2
        sums.append(s[H:, :])
    t1 = (blk * T + 1 + lax.broadcasted_iota(jnp.int32, (T, POOL_WIDTH), 0)).astype(F32)
    lane = lax.broadcasted_iota(jnp.int32, (T, POOL_WIDTH), 1)
    pooled = None
    for gi, win in enumerate(POOL_WINDOWS):
        mean = sums[gi] / jnp.minimum(t1, float(win))
        pooled = mean if pooled is None else jnp.where(lane >= gi * POOL_GROUP, mean, pooled)
    pooled = pooled - u
    y = jnp.dot(pooled.astype(BF16), w_ref[...], preferred_element_type=F32)
    o_ref[...] = (y * sc_ref[...]).astype(BF16)


def _multiscale_pool(pu, pool_w, pool_scale):
    S = pu.shape[0]
    T = ROW_TILE
    w_bd = jax.scipy.linalg.block_diag(*[pool_w[g] for g in range(len(POOL_WINDOWS))]).astype(BF16)
    return pl.pallas_call(
        _pool_kernel,
        grid=(S // T,),
        in_specs=[
            pl.BlockSpec((T, POOL_WIDTH), lambda i: (i, 0)),
            pl.BlockSpec((POOL_WIDTH, POOL_WIDTH), lambda i: (0, 0)),
            pl.BlockSpec((1, POOL_WIDTH), lambda i: (0, 0)),
        ],
        out_specs=pl.BlockSpec((T, POOL_WIDTH), lambda i: (i, 0)),
        out_shape=jax.ShapeDtypeStruct((S, POOL_WIDTH), BF16),
        scratch_shapes=[pltpu.VMEM((POOL_HALO, POOL_WIDTH), F32)],
        compiler_params=_params(("arbitrary",)),
        name="ms_pool",
    )(pu, w_bd, pool_scale.reshape(1, POOL_WIDTH))


def _outproj_kernel(a_ref, r_ref, p_ref, x_ref, wf_ref, g1_ref, lg_ref, lb_ref, o_ref, w_ref):
    @pl.when(pl.program_id(0) == 0)
    def _():
        w_ref[...] = wf_ref[0].astype(BF16)

    m = jnp.dot(a_ref[...], w_ref[0:DA_WIDTH, :], preferred_element_type=F32)
    m += jnp.dot(r_ref[...], w_ref[DA_WIDTH:DA_WIDTH + HG_WIDTH, :], preferred_element_type=F32)
    m += jnp.dot(p_ref[...], w_ref[DA_WIDTH + HG_WIDTH:, :], preferred_element_type=F32)
    y = ALPHA * x_ref[...] + (1.0 + g1_ref[...]) * m
    o_ref[...] = _layer_norm(y, lg_ref[...], lb_ref[...])


def _out_projection(a, r, p, x, w_out, layer, g1, ln_g, ln_b):
    S = x.shape[0]
    tm = ROW_TILE
    row = lambda i: (i, 0)
    fixed = lambda i: (0, 0)
    vec = pl.BlockSpec((1, D_MODEL), fixed)
    return pl.pallas_call(
        _outproj_kernel,
        grid=(S // tm,),
        in_specs=[
            pl.BlockSpec((tm, DA_WIDTH), row),
            pl.BlockSpec((tm, HG_WIDTH), row),
            pl.BlockSpec((tm, POOL_WIDTH), row),
            pl.BlockSpec((tm, D_MODEL), row),
            pl.BlockSpec((1, D_MIX, D_MODEL), lambda i: (layer, 0, 0), pipeline_mode=pl.Buffered(1)),
            vec, vec, vec,
        ],
        out_specs=pl.BlockSpec((tm, D_MODEL), row),
        out_shape=jax.ShapeDtypeStruct((S, D_MODEL), F32),
        scratch_shapes=[pltpu.VMEM((D_MIX, D_MODEL), BF16)],
        compiler_params=_params(("arbitrary",)),
        name="out_proj_ln",
    )(a, r, p, x, w_out, g1, ln_g.reshape(1, D_MODEL), ln_b.reshape(1, D_MODEL))


def _ffn_kernel(x_ref, sc_ref, sh_ref, g2_ref, w1_ref, w3_ref, w2_ref, lg_ref, lb_ref, *rest):
    n_cast = (len(rest) - 1) // 2
    o_ref = rest[n_cast]
    for src, dst in zip(rest[:n_cast], rest[n_cast + 1:]):
        dst[...] = src[...].astype(BF16)
    x = x_ref[...]
    h = (x * (1.0 + sc_ref[...]) + sh_ref[...]).astype(BF16)
    acc = jnp.zeros(x.shape, F32)
    step = 4 * MXU_DIM
    for lo in range(0, FFN_DIM, step):
        hi = min(lo + step, FFN_DIM)
        a = jnp.dot(h, w1_ref[:, lo:hi], preferred_element_type=F32)
        b = jnp.dot(h, w3_ref[:, lo:hi], preferred_element_type=F32)
        acc += jnp.dot((_silu(a) * b).astype(BF16), w2_ref[lo:hi, :], preferred_element_type=F32)
    y = ALPHA * x + (1.0 + g2_ref[...]) * acc
    o_ref[...] = _layer_norm(y, lg_ref[...], lb_ref[...])


def _dense_ffn(x, sc, sh, g2, w1, w3, w2, ln_g, ln_b, to_cast=()):
    S = x.shape[0]
    tm = FFN_CAST_TILE if to_cast else ROW_TILE
    steps = S // tm
    row = lambda i: (i, 0)
    fixed = lambda i: (0, 0)
    vec = pl.BlockSpec((1, D_MODEL), fixed)
    once = pl.Buffered(1)
    slabs = [a.reshape(-1, a.shape[-1]) for a in to_cast]
    slab_specs = [pl.BlockSpec((a.shape[0] // steps, a.shape[1]), row) for a in slabs]
    for a in slabs:
        assert a.shape[0] % (steps * 16) == 0
    outs = pl.pallas_call(
        _ffn_kernel,
        grid=(steps,),
        in_specs=[
            pl.BlockSpec((tm, D_MODEL), row),
            vec, vec, vec,
            pl.BlockSpec((D_MODEL, FFN_DIM), fixed, pipeline_mode=once),
            pl.BlockSpec((D_MODEL, FFN_DIM), fixed, pipeline_mode=once),
            pl.BlockSpec((FFN_DIM, D_MODEL), fixed, pipeline_mode=once),
            vec, vec,
        ] + slab_specs,
        out_specs=[pl.BlockSpec((tm, D_MODEL), row)] + slab_specs,
        out_shape=[jax.ShapeDtypeStruct((S, D_MODEL), F32)]
        + [jax.ShapeDtypeStruct(a.shape, BF16) for a in slabs],
        compiler_params=_params(("parallel",)),
        name="dense_ffn_ln",
    )(x, sc, sh, g2, w1, w3, w2, ln_g.reshape(1, D_MODEL), ln_b.reshape(1, D_MODEL), *slabs)
    return outs[0], [o.reshape(a.shape) for o, a in zip(outs[1:], to_cast)]


def _router_kernel(x_ref, sc_ref, sh_ref, rw_ref, h_ref, rank_ref, comb_ref, cum_ref):
    T = x_ref.shape[0]
    tc = MOE_CHUNK
    E = N_EXPERTS
    h = x_ref[...] * (1.0 + sc_ref[...]) + sh_ref[...]
    h_ref[...] = h.astype(BF16)
    logits = lax.dot_general(rw_ref[...], h, (((1,), (1,)), ((), ())),
                             preferred_element_type=F32, precision=HIGHEST)
    eidx = lax.broadcasted_iota(jnp.int32, (E, T), 0)
    v1 = jnp.max(logits, axis=0, keepdims=True)
    i1 = jnp.min(jnp.where(logits == v1, eidx, E), axis=0, keepdims=True)
    m1 = eidx == i1
    rest = jnp.where(m1, -jnp.inf, logits)
    v2 = jnp.max(rest, axis=0, keepdims=True)
    i2 = jnp.min(jnp.where(rest == v2, eidx, E), axis=0, keepdims=True)
    m2 = eidx == i2
    e2 = jnp.exp(v2 - v1)
    g1 = 1.0 / (1.0 + e2)
    g2 = e2 / (1.0 + e2)
    comb = jnp.where(m1, g1, 0.0) + jnp.where(m2, g2, 0.0)
    routed = jnp.logical_or(m1, m2)
    ind = routed.astype(BF16)
    r_i = lax.broadcasted_iota(jnp.int32, (tc, tc), 0)
    c_i = lax.broadcasted_iota(jnp.int32, (tc, tc), 1)
    strict = (r_i < c_i).astype(BF16)
    lane = lax.broadcasted_iota(jnp.int32, (E, LANES), 1)
    running = jnp.zeros((E, 1), F32)
    cum = jnp.zeros((E, LANES), F32)
    for ci in range(T // tc):
        sl = slice(ci * tc, (ci + 1) * tc)
        ind_c = ind[:, sl]
        rank_c = jnp.dot(ind_c, strict, preferred_element_type=F32) + running
        rank_ref[:, ci] = jnp.where(routed[:, sl], rank_c, -1.0).reshape(E, 1, tc)
        comb_ref[:, ci] = comb[:, sl].reshape(E, 1, tc)
        cum = jnp.where(lane == ci, running, cum)
        running = running + jnp.sum(ind_c.astype(F32), axis=1, keepdims=True)
    cum = jnp.where(lane == T // tc, running, cum)
    cum_ref[0] = cum.astype(jnp.int32)


def _router(x, sc, sh, router_w):
    S = x.shape[0]
    T = MOE_BLOCK
    E = N_EXPERTS
    fixed = lambda i: (0, 0)
    vec = pl.BlockSpec((1, D_MODEL), fixed)
    return pl.pallas_call(
        _router_kernel,
        grid=(S // T,),
        in_specs=[
            pl.BlockSpec((T, D_MODEL), lambda i: (i, 0)),
            vec, vec,
            pl.BlockSpec((E, D_MODEL), fixed),
        ],
        out_specs=[
            pl.BlockSpec((T, D_MODEL), lambda i: (i, 0)),
            pl.BlockSpec((E, T // MOE_CHUNK, 1, MOE_CHUNK), lambda i: (0, i, 0, 0)),
            pl.BlockSpec((E, T // MOE_CHUNK, 1, MOE_CHUNK), lambda i: (0, i, 0, 0)),
            pl.BlockSpec((1, E, LANES), lambda i: (i, 0, 0)),
        ],
        out_shape=[
            jax.ShapeDtypeStruct((S, D_MODEL), BF16),
            jax.ShapeDtypeStruct((E, S // MOE_CHUNK, 1, MOE_CHUNK), F32),
            jax.ShapeDtypeStruct((E, S // MOE_CHUNK, 1, MOE_CHUNK), F32),
            jax.ShapeDtypeStruct((S // T, E, LANES), jnp.int32),
        ],
        compiler_params=_params(("parallel",)),
        name="moe_router",
    )(x, sc, sh, router_w.T)


def _moe_kernel(cum_ref, h_ref, rank_ref, comb_ref, w1_ref, w3_ref, w2_ref, y_ref, xe_scr, gate_scr,
                out_scr):
    T = h_ref.shape[0]
    tr, tc, win = MOE_TILE, MOE_CHUNK, MOE_WINDOW
    nchunk = T // tc
    b = pl.program_id(0)
    e = pl.program_id(1)
    f = pl.program_id(2)

    @pl.when(jnp.logical_and(e == 0, f == 0))
    def _():
        y_ref[...] = jnp.zeros(y_ref.shape, F32)

    base = (b * N_EXPERTS + e) * LANES
    count = cum_ref[base + nchunk]
    ntiles = (count + (tr - 1)) // tr

    def token_row(ref, first, n):
        blk = ref[0, pl.ds(first, n)]
        return jnp.concatenate([blk[c] for c in range(n)], axis=1)

    def tile(r, carry):
        lo = r * tr
        rows = pl.ds(pl.multiple_of(lo, 16), tr)

        def hits(first, n):
            want = (lo + lax.broadcasted_iota(jnp.int32, (tr, n * tc), 0)).astype(F32)
            return token_row(rank_ref, first, n) == want

        def token_span(first, n):
            start = first * tc
            if not isinstance(start, int):
                start = pl.multiple_of(start, tc)
            return pl.ds(start, n * tc)

        c_lo = jnp.int32(0)
        c_hi = jnp.int32(-1)
        for c in range(nchunk):
            c_lo += (cum_ref[base + c + 1] <= lo).astype(jnp.int32)
            c_hi += (cum_ref[base + c] < lo + tr).astype(jnp.int32)
        fits = c_hi - c_lo < win
        window = (jnp.minimum(c_lo, nchunk - win), win)
        block = (0, nchunk)

        def gather(first, n):
            hit = hits(first, n)
            xe_scr[rows, :] = jnp.dot(hit.astype(BF16), h_ref[token_span(first, n), :],
                                      preferred_element_type=F32).astype(BF16)
            gate_scr[rows, :] = jnp.sum(jnp.where(hit, token_row(comb_ref, first, n), 0.0),
                                        axis=1, keepdims=True)

        @pl.when(jnp.logical_and(f == 0, fits))
        def _():
            gather(*window)

        @pl.when(jnp.logical_and(f == 0, jnp.logical_not(fits)))
        def _():
            gather(*block)

        xe = xe_scr[rows, :]
        a = jnp.dot(xe, w1_ref[0], preferred_element_type=F32)
        g = jnp.dot(xe, w3_ref[0], preferred_element_type=F32)
        mid = (_silu(a) * g).astype(BF16)
        part = jnp.dot(mid, w2_ref[0], preferred_element_type=F32)

        @pl.when(f == 0)
        def _():
            out_scr[rows, :] = part

        if MOE_FSPLIT > 2:
            @pl.when(jnp.logical_and(f > 0, f < MOE_FSPLIT - 1))
            def _():
                out_scr[rows, :] += part

        @pl.when(f == MOE_FSPLIT - 1)
        def _():
            out_bf = ((out_scr[rows, :] + part) * gate_scr[rows, :]).astype(BF16)

            def scatter(first, n):
                y_ref[token_span(first, n), :] += lax.dot_general(
                    hits(first, n).astype(BF16), out_bf, (((0,), (0,)), ((), ())),
                    preferred_element_type=F32)

            @pl.when(fits)
            def _():
                scatter(*window)

            @pl.when(jnp.logical_not(fits))
            def _():
                scatter(*block)

        return carry

    lax.fori_loop(0, ntiles, tile, 0)


def _moe_ffn(h_bf, rank, comb, cum, w1, w3, w2):
    S = h_bf.shape[0]
    T = MOE_BLOCK
    fw = EXPERT_DIM // MOE_FSPLIT
    grid_spec = pltpu.PrefetchScalarGridSpec(
        num_scalar_prefetch=1,
        grid=(S // T, N_EXPERTS, MOE_FSPLIT),
        in_specs=[
            pl.BlockSpec((T, D_MODEL), lambda b, e, f, c: (b, 0), pipeline_mode=pl.Buffered(1)),
            pl.BlockSpec((1, T // MOE_CHUNK, 1, MOE_CHUNK), lambda b, e, f, c: (e, b, 0, 0)),
            pl.BlockSpec((1, T // MOE_CHUNK, 1, MOE_CHUNK), lambda b, e, f, c: (e, b, 0, 0)),
            pl.BlockSpec((1, D_MODEL, fw), lambda b, e, f, c: (e, 0, f)),
            pl.BlockSpec((1, D_MODEL, fw), lambda b, e, f, c: (e, 0, f)),
            pl.BlockSpec((1, fw, D_MODEL), lambda b, e, f, c: (e, f, 0)),
        ],
        out_specs=pl.BlockSpec((T, D_MODEL), lambda b, e, f, c: (b, 0)),
        scratch_shapes=[
            pltpu.VMEM((T + MOE_TILE, D_MODEL), BF16),
            pltpu.VMEM((T + MOE_TILE, 1), F32),
            pltpu.VMEM((T + MOE_TILE, D_MODEL), F32),
        ],
    )
    return pl.pallas_call(
        _moe_kernel,
        grid_spec=grid_spec,
        out_shape=jax.ShapeDtypeStruct((S, D_MODEL), F32),
        compiler_params=_params(("arbitrary", "arbitrary", "arbitrary")),
        name="moe_experts",
    )(cum.reshape(-1), h_bf, rank, comb, w1, w3, w2)


def _ln_kernel(x_ref, y_ref, g2_ref, lg_ref, lb_ref, o_ref):
    y = ALPHA * x_ref[...] + (1.0 + g2_ref[...]) * y_ref[...]
    o_ref[...] = _layer_norm(y, lg_ref[...], lb_ref[...])


def _residual_ln(x, y, g2, ln_g, ln_b):
    S = x.shape[0]
    tm = ROW_TILE
    row = lambda i: (i, 0)
    vec = pl.BlockSpec((1, D_MODEL), lambda i: (0, 0))
    return pl.pallas_call(
        _ln_kernel,
        grid=(S // tm,),
        in_specs=[pl.BlockSpec((tm, D_MODEL), row), pl.BlockSpec((tm, D_MODEL), row), vec, vec, vec],
        out_specs=pl.BlockSpec((tm, D_MODEL), row),
        out_shape=jax.ShapeDtypeStruct((S, D_MODEL), F32),
        compiler_params=_params(("parallel",)),
        name="residual_ln",
    )(x, y, g2, ln_g.reshape(1, D_MODEL), ln_b.reshape(1, D_MODEL))


def _rope_tables(S):
    d = DA_HEAD_DIM
    tm = ATTN_BLOCK
    inv_freq = 1.0 / (ROPE_THETA ** (jnp.arange(0, d, 2, dtype=F32) / d))
    inv_lane = inv_freq[jnp.arange(LANES) % (d // 2)][None, :]
    base = (jnp.arange(S // tm, dtype=F32) * tm)[:, None] * inv_lane
    offs = jnp.arange(tm, dtype=F32)[:, None] * inv_lane
    shape_b = (S // tm, 1, LANES)
    return jnp.cos(base).reshape(shape_b), jnp.sin(base).reshape(shape_b), jnp.cos(offs), jnp.sin(offs)


def kernel(x, c, w_ada, b_ada, w_in, lam_qk, attn_norm_g, hg_lb_logits, hg_norm_g, pool_w, pool_scale, w_out, ln1_g, ln1_b, ln2_g, ln2_b, ffn_w1, ffn_w3, ffn_w2, router_w, exp_w1, exp_w3, exp_w2):
    B, S, D = x.shape
    assert B == 1 and D == D_MODEL and S % MOE_BLOCK == 0
    xs = x.reshape(S, D)
    mod = _modulation(c, w_ada, b_ada)
    rope = _rope_tables(S)
    experts_bf = {}
    for l in range(DEPTH):
        sh1, sc1, g1, sh2, sc2, g2 = (mod[l, j] for j in range(6))
        qt, k, vt, hg, pu = _in_projection(xs, sc1, sh1, w_in, l, rope)
        a = _diff_attention(qt, k, vt, lam_qk[l], attn_norm_g[l], l)
        r = _hgrn2(hg, hg_lb_logits, hg_norm_g[l], l)
        p = _multiscale_pool(pu, pool_w[l], pool_scale[l])
        xs = _out_projection(a, r, p, xs, w_out, l, g1, ln1_g[l], ln1_b[l])
        if l % 2 == 0:
            i = l // 2
            ride = (exp_w1[i], exp_w3[i], exp_w2[i]) if l + 1 < DEPTH else ()
            xs, cast = _dense_ffn(xs, sc2, sh2, g2, ffn_w1[i].astype(BF16), ffn_w3[i].astype(BF16),
                                  ffn_w2[i].astype(BF16), ln2_g[l], ln2_b[l], ride)
            if ride:
                experts_bf[i] = cast
        else:
            i = l // 2
            h_bf, rank, comb, cum = _router(xs, sc2, sh2, router_w[i])
            w1, w3, w2 = experts_bf.get(i) or [w[i].astype(BF16) for w in (exp_w1, exp_w3, exp_w2)]
            y = _moe_ffn(h_bf, rank, comb, cum, w1, w3, w2)
            xs = _residual_ln(xs, y, g2, ln2_g[l], ln2_b[l])
    return xs.reshape(B, S, D)
```

```python
import functools
import math

import jax
import jax.numpy as jnp
from jax import lax
from jax.experimental import pallas as pl
from jax.experimental.pallas import tpu as pltpu

F32 = jnp.float32
BF16 = jnp.bfloat16
HIGHEST = lax.Precision.HIGHEST

D_MODEL = 1024
DEPTH = 2
DA_HEADS = 4
DA_WIDTH = D_MODEL // 2
DA_HEAD_DIM = DA_WIDTH // (2 * DA_HEADS)
DA_VAL_DIM = 2 * DA_HEAD_DIM
HG_HEADS = 4
HG_WIDTH = D_MODEL // 4
HG_DIM = HG_WIDTH // HG_HEADS
POOL_WINDOWS = (2, 4, 8, 16)
POOL_WIDTH = D_MODEL // 4
POOL_GROUP = POOL_WIDTH // len(POOL_WINDOWS)
D_MIX = DA_WIDTH + HG_WIDTH + POOL_WIDTH
D_IN = 3 * DA_WIDTH + 4 * HG_WIDTH + POOL_WIDTH
ROPE_THETA = 10000.0
MASK_VALUE = -1e30
TINY = 1e-30
FFN_DIM = 11 * D_MODEL // 4
N_EXPERTS = 8
EXPERT_DIM = 7 * D_MODEL // 2
ALPHA = (2 * DEPTH) ** 0.25
EPS = 1e-5
LOG2_E = math.log2(math.e)

LANES = 128
MXU_DIM = 256
VMEM_LIMIT = 60 * 1024 * 1024

ROW_TILE = 512
FFN_CAST_TILE = 256
ATTN_BLOCK = 512
ATTN_Q_BLOCK = 1024
ATTN_QUERY_GROUP = 256
ATTN_UNROLL = 4
ATTN_SUM_ROWS = 16
ATTN_VT_ROWS = DA_VAL_DIM + ATTN_SUM_ROWS
HG_BLOCK = 512
HG_CHUNK = 16
HG_UNROLL = 4
POOL_HALO = 16
MOE_BLOCK = 2048
MOE_TILE = 192
MOE_TAIL_TILE = 128
MOE_CHUNK = 256
MOE_WINDOW = 5
MOE_FSPLIT = 2


def _params(sem, vmem=VMEM_LIMIT):
    return pltpu.CompilerParams(dimension_semantics=sem, vmem_limit_bytes=vmem)


def _layer_norm(y, g, b):
    mu = jnp.mean(y, axis=-1, keepdims=True)
    d = y - mu
    var = jnp.mean(d * d, axis=-1, keepdims=True)
    return d * lax.rsqrt(var + EPS) * g + b


def _silu(x):
    return x * jax.nn.sigmoid(x)


def _head_block_mask(n, group):
    r = lax.broadcasted_iota(jnp.int32, (n, n), 0) // group
    c = lax.broadcasted_iota(jnp.int32, (n, n), 1) // group
    return r == c


def _mod_kernel(c_ref, w_ref, b_ref, o_ref):
    cond = _silu(c_ref[...])
    cond8 = jnp.broadcast_to(cond, (8, D_MODEL))
    r = jnp.dot(cond8, w_ref[0], preferred_element_type=F32, precision=HIGHEST)
    o_ref[0] = r[0:1] + b_ref[0]


def _modulation(c, w_ada, b_ada):
    n = 6 * DEPTH
    b3 = b_ada.reshape(n, 1, D_MODEL)
    out = pl.pallas_call(
        _mod_kernel,
        grid=(DEPTH, 6),
        in_specs=[
            pl.BlockSpec((1, D_MODEL), lambda l, j: (0, 0)),
            pl.BlockSpec((1, D_MODEL, D_MODEL), lambda l, j: (l, 0, j)),
            pl.BlockSpec((1, 1, D_MODEL), lambda l, j: (l * 6 + j, 0, 0)),
        ],
        out_specs=pl.BlockSpec((1, 1, D_MODEL), lambda l, j: (l * 6 + j, 0, 0)),
        out_shape=jax.ShapeDtypeStruct((n, 1, D_MODEL), F32),
        compiler_params=_params(("parallel", "parallel")),
        name="adaln_mod",
    )(c, w_ada, b3)
    return out.reshape(DEPTH, 6, 1, D_MODEL)


def _inproj_kernel(x_ref, sc_ref, sh_ref, wf_ref, cosb_ref, sinb_ref, cosr_ref, sinr_ref,
                   qt_ref, k_ref, vt_ref, hg_ref, pu_ref, w_ref, wvt_ref):
    @pl.when(pl.program_id(0) == 0)
    def _():
        w_ref[...] = wf_ref[0].astype(BF16)
        wvt_ref[...] = wf_ref[0, :, 2 * DA_WIDTH:3 * DA_WIDTH].T.astype(BF16)

    h = (x_ref[...] * (1.0 + sc_ref[...]) + sh_ref[...]).astype(BF16)
    qk = jnp.dot(h, w_ref[:, : 2 * DA_WIDTH], preferred_element_type=F32)
    cb, sb, cr, sr = cosb_ref[0], sinb_ref[0], cosr_ref[...], sinr_ref[...]
    cos128 = cb * cr - sb * sr
    sin128 = sb * cr + cb * sr
    half = DA_HEAD_DIM // 2
    lane128 = lax.broadcasted_iota(jnp.int32, cos128.shape, 1)
    sin128 = jnp.where((lane128 & half) == 0, -sin128, sin128)
    reps = 2 * DA_WIDTH // LANES
    cos = jnp.tile(cos128, (1, reps))
    sin = jnp.tile(sin128, (1, reps))
    lane = lax.broadcasted_iota(jnp.int32, qk.shape, 1)
    first_half = (lane & half) == 0
    width = 2 * DA_WIDTH
    partner = jnp.where(first_half, pltpu.roll(qk, width - half, 1), pltpu.roll(qk, half, 1))
    rot = qk * cos + partner * sin
    qt_ref[0] = (rot[:, :DA_WIDTH] * (DA_HEAD_DIM ** -0.5 * LOG2_E)).T.astype(BF16)
    k_ref[...] = rot[:, DA_WIDTH:].astype(BF16)
    o = 2 * DA_WIDTH
    vt = lax.dot_general(wvt_ref[...], h, (((1,), (1,)), ((), ())),
                         preferred_element_type=F32).astype(BF16)
    ones = jnp.ones((ATTN_SUM_ROWS, vt.shape[1]), BF16)
    pieces = []
    for hd in range(DA_HEADS):
        pieces += [vt[hd * DA_VAL_DIM:(hd + 1) * DA_VAL_DIM], ones]
    vt_ref[0] = jnp.concatenate(pieces, axis=0)
    o += DA_WIDTH
    hg_ref[...] = jnp.dot(h, w_ref[:, o:o + 4 * HG_WIDTH], preferred_element_type=F32)
    o += 4 * HG_WIDTH
    pu_ref[...] = jnp.dot(h, w_ref[:, o:o + POOL_WIDTH], preferred_element_type=F32)


def _in_projection(x, sc, sh, w_in, layer, rope):
    S = x.shape[0]
    tm = ATTN_BLOCK
    row = lambda i: (i, 0)
    fixed = lambda i: (0, 0)
    cos_b, sin_b, cos_r, sin_r = rope
    return pl.pallas_call(
        _inproj_kernel,
        grid=(S // tm,),
        in_specs=[
            pl.BlockSpec((tm, D_MODEL), row),
            pl.BlockSpec((1, D_MODEL), fixed),
            pl.BlockSpec((1, D_MODEL), fixed),
            pl.BlockSpec((1, D_MODEL, D_IN), lambda i: (layer, 0, 0), pipeline_mode=pl.Buffered(1)),
            pl.BlockSpec((1, 1, LANES), lambda i: (i, 0, 0)),
            pl.BlockSpec((1, 1, LANES), lambda i: (i, 0, 0)),
            pl.BlockSpec((tm, LANES), fixed),
            pl.BlockSpec((tm, LANES), fixed),
        ],
        out_specs=[
            pl.BlockSpec((1, DA_WIDTH, tm), lambda i: (i, 0, 0)),
            pl.BlockSpec((tm, DA_WIDTH), row),
            pl.BlockSpec((1, DA_HEADS * ATTN_VT_ROWS, tm), lambda i: (i, 0, 0)),
            pl.BlockSpec((tm, 4 * HG_WIDTH), row),
            pl.BlockSpec((tm, POOL_WIDTH), row),
        ],
        out_shape=[
            jax.ShapeDtypeStruct((S // tm, DA_WIDTH, tm), BF16),
            jax.ShapeDtypeStruct((S, DA_WIDTH), BF16),
            jax.ShapeDtypeStruct((S // tm, DA_HEADS * ATTN_VT_ROWS, tm), BF16),
            jax.ShapeDtypeStruct((S, 4 * HG_WIDTH), F32),
            jax.ShapeDtypeStruct((S, POOL_WIDTH), F32),
        ],
        scratch_shapes=[
            pltpu.VMEM((D_MODEL, D_IN), BF16),
            pltpu.VMEM((DA_WIDTH, D_MODEL), BF16),
        ],
        compiler_params=_params(("arbitrary",)),
        name="in_proj",
    )(x, sc, sh, w_in, cos_b, sin_b, cos_r, sin_r)


def _attn_kernel(lam_ref, g_ref, qt_ref, k_ref, vt_ref, o_ref, m_scr, a_scr, c_scr, acc_scr, s_scr,
                 p_scr, *, lam_init):
    bq = qt_ref.shape[0] * qt_ref.shape[2]
    bk = vt_ref.shape[2]
    cg = ATTN_QUERY_GROUP
    i = pl.program_id(1)
    qt = jnp.concatenate([qt_ref[n] for n in range(qt_ref.shape[0])], axis=1)
    rowq = lax.broadcasted_iota(jnp.int32, qt.shape, 0)
    zero = jnp.zeros_like(qt)
    maps = (jnp.where(rowq < DA_HEAD_DIM, qt, zero), jnp.where(rowq >= DA_HEAD_DIM, qt, zero))
    groups = [(a * bq + c0, c0, maps[a][:, c0:c0 + cg]) for a in range(2) for c0 in range(0, bq, cg)]
    m_scr[...] = jnp.full(m_scr.shape, -jnp.inf, F32)
    acc_scr[...] = jnp.zeros(acc_scr.shape, F32)
    hk = bk // 2
    key_i = lax.broadcasted_iota(jnp.int32, (hk, cg), 0)
    qry_i = lax.broadcasted_iota(jnp.int32, (hk, cg), 1)
    ng = len(groups)
    last = ng - 1

    def cols_of(g):
        return slice(groups[g][0], groups[g][0] + cg)

    def ccols_of(sset, g):
        return slice(sset * 2 * bq + groups[g][0], sset * 2 * bq + groups[g][0] + cg)

    def mask_mode(key_offset, g):
        c0 = groups[g][1]
        if key_offset + hk - 1 <= c0:
            return None
        if key_offset > c0 + cg - 1:
            return "skip"
        return key_offset

    def score_stage(hb, sset, g, key_offset):
        kb = k_ref[pl.ds(pl.multiple_of(hb * hk, hk), hk), :]
        s = jnp.dot(kb, groups[g][2], preferred_element_type=F32)
        if key_offset is not None:
            s = jnp.where(key_i + key_offset <= qry_i + groups[g][1], s, MASK_VALUE)
        s_scr[sset * ng + g] = s
        c_scr[:, ccols_of(sset, g)] = jnp.max(s, axis=0, keepdims=True)

    def max_stage(sset, g):
        cols = cols_of(g)
        m_prev = m_scr[:, cols]
        m_new = jnp.maximum(m_prev, c_scr[:, ccols_of(sset, g)])
        a_scr[:, cols] = jnp.exp2(m_prev - m_new)
        m_scr[:, cols] = m_new

    def exp_stage(sset, g, vt_half, carried, defer):
        cols = cols_of(g)
        p = jnp.exp2(s_scr[sset * ng + g] - m_scr[:, cols])
        alpha = a_scr[:, cols]
        pb = p.astype(BF16)
        acc = acc_scr[:, cols]
        if carried is not None:
            acc = acc + carried
        if defer:
            acc_scr[:, cols] = alpha * acc
            p_scr[...] = pb
        else:
            acc_scr[:, cols] = alpha * acc + jnp.dot(vt_half, pb, preferred_element_type=F32)

    def items(key_offset):
        if key_offset is None:
            return [(g, None) for g in range(ng)]
        return [(g, mask_mode(key_offset, g)) for g in range(ng) if mask_mode(key_offset, g) != "skip"]

    def iteration(t, offs, look):
        vt_a = vt_ref[t, :, 0:hk]
        vt_b = vt_ref[t, :, hk:bk]
        carried = jnp.dot(vt_ref[jnp.maximum(t - 1, 0), :, hk:bk], p_scr[...],
                          preferred_element_type=F32)
        a_items = items(offs)
        b_items = items(None if offs is None else offs + hk)
        n_items = [] if look == "stop" else items(look)
        score_stage(2 * t + 1, 1, *b_items[0])
        for n, (g, _) in enumerate(a_items):
            if n + 1 < len(b_items):
                score_stage(2 * t + 1, 1, *b_items[n + 1])
            if n + 1 < len(a_items):
                max_stage(0, a_items[n + 1][0])
            else:
                max_stage(1, b_items[0][0])
            exp_stage(0, g, vt_a, carried if g == last else None, False)
        if n_items:
            score_stage(2 * t + 2, 0, *n_items[0])
        for n, (g, _) in enumerate(b_items):
            if n + 1 < len(n_items):
                score_stage(2 * t + 2, 0, *n_items[n + 1])
            if n + 1 < len(b_items):
                max_stage(1, b_items[n + 1][0])
            elif n_items:
                max_stage(0, n_items[0][0])
            exp_stage(1, g, vt_b, None, bool(n_items) and g == last)

    p_scr[...] = jnp.zeros(p_scr.shape, BF16)
    for g in range(ng):
        score_stage(0, 0, g, (0 - i) * bq)
    max_stage(0, 0)

    per_q = bq // bk
    first = i * per_q
    unroll = ATTN_UNROLL

    def body(u, carry):
        for k in range(unroll):
            iteration(unroll * u + k, None, None)
        return carry

    n_plain = jnp.maximum(first - 1, 0)
    trips = n_plain // unroll
    lax.fori_loop(0, trips, body, 0)
    for k in range(unroll - 1):
        @pl.when(trips * unroll + k < n_plain)
        def _():
            iteration(trips * unroll + k, None, None)

    @pl.when(i > 0)
    def _():
        iteration(first - 1, None, 0)

    for d in range(per_q):
        iteration(first + d, d * bk, (d + 1) * bk if d + 1 < per_q else "stop")

    lq = lam_ref[...]
    lam = (jnp.exp(jnp.sum(lq[0:1] * lq[1:2], axis=1, keepdims=True))
           - jnp.exp(jnp.sum(lq[2:3] * lq[3:4], axis=1, keepdims=True)) + lam_init)
    o = acc_scr[0:DA_VAL_DIM, :] / acc_scr[DA_VAL_DIM:DA_VAL_DIM + 1, :]
    o = o[:, :bq] - lam * o[:, bq:]
    ms = jnp.mean(o * o, axis=0, keepdims=True)
    o = o * lax.rsqrt(ms + EPS) * g_ref[...] * (1.0 - lam_init)
    o_ref[...] = o.T.astype(BF16)


def _diff_attention(qt, k, vt, lam_qk, attn_g, layer):
    S = k.shape[0]
    bk = ATTN_BLOCK
    bq = ATTN_Q_BLOCK
    nk = S // bk
    lam_init = 0.8 - 0.6 * math.exp(-0.3 * layer)
    return pl.pallas_call(
        functools.partial(_attn_kernel, lam_init=lam_init),
        grid=(DA_HEADS, S // bq),
        in_specs=[
            pl.BlockSpec((4, DA_HEAD_DIM), lambda h, i: (0, 0)),
            pl.BlockSpec((DA_VAL_DIM, 1), lambda h, i: (0, 0)),
            pl.BlockSpec((bq // bk, LANES, bk), lambda h, i: (i, h, 0)),
            pl.BlockSpec((S, LANES), lambda h, i: (0, h)),
            pl.BlockSpec((nk, ATTN_VT_ROWS, bk), lambda h, i: (0, h, 0)),
        ],
        out_specs=pl.BlockSpec((bq, DA_VAL_DIM), lambda h, i: (i, h)),
        out_shape=jax.ShapeDtypeStruct((S, DA_WIDTH), BF16),
        scratch_shapes=[
            pltpu.VMEM((1, 2 * bq), F32),
            pltpu.VMEM((1, 2 * bq), F32),
            pltpu.VMEM((1, 2 * 2 * bq), F32),
            pltpu.VMEM((ATTN_VT_ROWS, 2 * bq), F32),
            pltpu.VMEM((2 * (2 * bq // ATTN_QUERY_GROUP), bk // 2, ATTN_QUERY_GROUP), F32),
            pltpu.VMEM((bk // 2, ATTN_QUERY_GROUP), BF16),
        ],
        compiler_params=_params(("parallel", "parallel")),
        name="diff_attn",
    )(lam_qk, attn_g.reshape(DA_VAL_DIM, 1), qt, k, vt)


def _hgrn_kernel(lbl_ref, g_ref, hg_ref, o_ref, st_scr, stb_scr, o_scr, *, layer):
    T = hg_ref.shape[0]
    c = HG_CHUNK
    W = HG_WIDTH

    @pl.when(pl.program_id(0) == 0)
    def _():
        st_scr[...] = jnp.zeros(st_scr.shape, F32)
        stb_scr[...] = jnp.zeros(stb_scr.shape, BF16)

    lg = lbl_ref[...]
    ex = jnp.exp(lg - jnp.max(lg, axis=0, keepdims=True))
    prob = ex / jnp.sum(ex, axis=0, keepdims=True)
    lb = jnp.zeros((1, W), F32)
    for li in range(1, layer + 1):
        lb = lb + prob[li:li + 1]

    same_head = _head_block_mask(W, HG_DIM)
    ones_bd = same_head.astype(BF16)
    rowi = lax.broadcasted_iota(jnp.int32, (c, W), 0)

    def body(s, carry):
        sl = pl.ds(pl.multiple_of(s * c, c), c)
        qf = hg_ref[sl, 0:W]
        z = hg_ref[sl, W:2 * W]
        iv = hg_ref[sl, 2 * W:3 * W]
        f = lb + (1.0 - lb) * jax.nn.sigmoid(z)
        logf = jnp.log(jnp.maximum(f, TINY))
        key = (1.0 - lb) * jax.nn.sigmoid(-z)
        val = _silu(iv)
        b = logf
        sft = 1
        while sft < c:
            b = b + jnp.where(rowi >= sft, pltpu.roll(b, sft, 0), 0.0)
            sft *= 2
        b_last = b[c - 1:c]
        kd = (key * jnp.exp(b_last - b)).astype(BF16)
        upd = lax.dot_general(val.astype(BF16), kd, (((0,), (0,)), ((), ())),
                              preferred_element_type=F32)
        es = [(qf * key).astype(BF16)]
        vrs = [val]
        for d in range(1, c):
            kr = pltpu.roll(key, d, 0)
            br = pltpu.roll(b, d, 0)
            e = jnp.where(rowi >= d, qf * kr * jnp.exp(b - br), 0.0)
            es.append(e.astype(BF16))
            vrs.append(pltpu.roll(val, d, 0))
        estack = jnp.concatenate(es, axis=0)
        r = jnp.dot(estack, ones_bd, preferred_element_type=F32)
        o = r[0:c] * vrs[0]
        for d in range(1, c):
            o = o + r[d * c:(d + 1) * c] * vrs[d]
        qd = (qf * jnp.exp(b)).astype(BF16)
        o = o + lax.dot_general(qd, stb_scr[...], (((1,), (1,)), ((), ())),
                                preferred_element_type=F32)
        o_scr[sl, :] = o
        decay = jnp.exp(b_last)
        for hd in range(HG_HEADS):
            rws = slice(hd * HG_DIM, (hd + 1) * HG_DIM)
            lns = slice(hd * HG_DIM // LANES * LANES, (hd * HG_DIM // LANES + 1) * LANES)
            new = decay[:, lns] * st_scr[rws, lns] + jnp.where(same_head[rws, lns], upd[rws, lns], 0.0)
            st_scr[rws, lns] = new
            stb_scr[rws, lns] = new.astype(BF16)
        return carry

    lax.fori_loop(0, T // c, body, 0, unroll=HG_UNROLL)

    o = o_scr[...]
    ms = jnp.dot(o * o, same_head.astype(F32), preferred_element_type=F32,
                 precision=HIGHEST) * (1.0 / HG_DIM)
    gate = hg_ref[:, 3 * W:4 * W]
    o_ref[...] = (o * lax.rsqrt(ms + EPS) * g_ref[...] * _silu(gate)).astype(BF16)


def _hgrn2(hg, lb_logits, norm_g, layer):
    S = hg.shape[0]
    T = HG_BLOCK
    g_t = jnp.tile(norm_g.reshape(1, HG_DIM), (1, HG_HEADS))
    return pl.pallas_call(
        functools.partial(_hgrn_kernel, layer=layer),
        grid=(S // T,),
        in_specs=[
            pl.BlockSpec((DEPTH, HG_WIDTH), lambda i: (0, 0)),
            pl.BlockSpec((1, HG_WIDTH), lambda i: (0, 0)),
            pl.BlockSpec((T, 4 * HG_WIDTH), lambda i: (i, 0)),
        ],
        out_specs=pl.BlockSpec((T, HG_WIDTH), lambda i: (i, 0)),
        out_shape=jax.ShapeDtypeStruct((S, HG_WIDTH), BF16),
        scratch_shapes=[
            pltpu.VMEM((HG_WIDTH, HG_WIDTH), F32),
            pltpu.VMEM((HG_WIDTH, HG_WIDTH), BF16),
            pltpu.VMEM((T, HG_WIDTH), F32),
        ],
        compiler_params=_params(("arbitrary",)),
        name="hgrn2",
    )(lb_logits, g_t, hg)


def _pool_kernel(u_ref, w_ref, sc_ref, o_ref, halo_scr):
    T = u_ref.shape[0]
    H = POOL_HALO
    blk = pl.program_id(0)

    @pl.when(blk == 0)
    def _():
        halo_scr[...] = jnp.zeros(halo_scr.shape, F32)

    u = u_ref[...]
    ext = jnp.concatenate([halo_scr[...], u], axis=0)
    halo_scr[...] = u[T - H:, :]
    sums = []
    s = ext
    w = 1
    while w < POOL_WINDOWS[-1]:
        s = s + pltpu.roll(s, w, 0)
        w *= 2
        sums.append(s[H:, :])
    t1 = (blk * T + 1 + lax.broadcasted_iota(jnp.int32, (T, POOL_WIDTH), 0)).astype(F32)
    lane = lax.broadcasted_iota(jnp.int32, (T, POOL_WIDTH), 1)
    pooled = None
    for gi, win in enumerate(POOL_WINDOWS):
        mean = sums[gi] / jnp.minimum(t1, float(win))
        pooled = mean if pooled is None else jnp.where(lane >= gi * POOL_GROUP, mean, pooled)
    pooled = pooled - u
    y = jnp.dot(pooled.astype(BF16), w_ref[...], preferred_element_type=F32)
    o_ref[...] = (y * sc_ref[...]).astype(BF16)


def _multiscale_pool(pu, pool_w, pool_scale):
    S = pu.shape[0]
    T = ROW_TILE
    w_bd = jax.scipy.linalg.block_diag(*[pool_w[g] for g in range(len(POOL_WINDOWS))]).astype(BF16)
    return pl.pallas_call(
        _pool_kernel,
        grid=(S // T,),
        in_specs=[
            pl.BlockSpec((T, POOL_WIDTH), lambda i: (i, 0)),
            pl.BlockSpec((POOL_WIDTH, POOL_WIDTH), lambda i: (0, 0)),
            pl.BlockSpec((1, POOL_WIDTH), lambda i: (0, 0)),
        ],
        out_specs=pl.BlockSpec((T, POOL_WIDTH), lambda i: (i, 0)),
        out_shape=jax.ShapeDtypeStruct((S, POOL_WIDTH), BF16),
        scratch_shapes=[pltpu.VMEM((POOL_HALO, POOL_WIDTH), F32)],
        compiler_params=_params(("arbitrary",)),
        name="ms_pool",
    )(pu, w_bd, pool_scale.reshape(1, POOL_WIDTH))


def _outproj_kernel(a_ref, r_ref, p_ref, x_ref, wf_ref, g1_ref, lg_ref, lb_ref, o_ref, w_ref):
    @pl.when(pl.program_id(0) == 0)
    def _():
        w_ref[...] = wf_ref[0].astype(BF16)

    m = jnp.dot(a_ref[...], w_ref[0:DA_WIDTH, :], preferred_element_type=F32)
    m += jnp.dot(r_ref[...], w_ref[DA_WIDTH:DA_WIDTH + HG_WIDTH, :], preferred_element_type=F32)
    m += jnp.dot(p_ref[...], w_ref[DA_WIDTH + HG_WIDTH:, :], preferred_element_type=F32)
    y = ALPHA * x_ref[...] + (1.0 + g1_ref[...]) * m
    o_ref[...] = _layer_norm(y, lg_ref[...], lb_ref[...])


def _out_projection(a, r, p, x, w_out, layer, g1, ln_g, ln_b):
    S = x.shape[0]
    tm = ROW_TILE
    row = lambda i: (i, 0)
    fixed = lambda i: (0, 0)
    vec = pl.BlockSpec((1, D_MODEL), fixed)
    return pl.pallas_call(
        _outproj_kernel,
        grid=(S // tm,),
        in_specs=[
            pl.BlockSpec((tm, DA_WIDTH), row),
            pl.BlockSpec((tm, HG_WIDTH), row),
            pl.BlockSpec((tm, POOL_WIDTH), row),
            pl.BlockSpec((tm, D_MODEL), row),
            pl.BlockSpec((1, D_MIX, D_MODEL), lambda i: (layer, 0, 0), pipeline_mode=pl.Buffered(1)),
            vec, vec, vec,
        ],
        out_specs=pl.BlockSpec((tm, D_MODEL), row),
        out_shape=jax.ShapeDtypeStruct((S, D_MODEL), F32),
        scratch_shapes=[pltpu.VMEM((D_MIX, D_MODEL), BF16)],
        compiler_params=_params(("arbitrary",)),
        name="out_proj_ln",
    )(a, r, p, x, w_out, g1, ln_g.reshape(1, D_MODEL), ln_b.reshape(1, D_MODEL))


def _ffn_kernel(x_ref, sc_ref, sh_ref, g2_ref, w1_ref, w3_ref, w2_ref, lg_ref, lb_ref, *rest):
    n_cast = (len(rest) - 1) // 2
    o_ref = rest[n_cast]
    for src, dst in zip(rest[:n_cast], rest[n_cast + 1:]):
        dst[...] = src[...].astype(BF16)
    x = x_ref[...]
    h = (x * (1.0 + sc_ref[...]) + sh_ref[...]).astype(BF16)
    acc = jnp.zeros(x.shape, F32)
    step = 4 * MXU_DIM
    for lo in range(0, FFN_DIM, step):
        hi = min(lo + step, FFN_DIM)
        a = jnp.dot(h, w1_ref[:, lo:hi], preferred_element_type=F32)
        b = jnp.dot(h, w3_ref[:, lo:hi], preferred_element_type=F32)
        acc += jnp.dot((_silu(a) * b).astype(BF16), w2_ref[lo:hi, :], preferred_element_type=F32)
    y = ALPHA * x + (1.0 + g2_ref[...]) * acc
    o_ref[...] = _layer_norm(y, lg_ref[...], lb_ref[...])


def _dense_ffn(x, sc, sh, g2, w1, w3, w2, ln_g, ln_b, to_cast=()):
    S = x.shape[0]
    tm = FFN_CAST_TILE if to_cast else ROW_TILE
    steps = S // tm
    row = lambda i: (i, 0)
    fixed = lambda i: (0, 0)
    vec = pl.BlockSpec((1, D_MODEL), fixed)
    once = pl.Buffered(1)
    slabs = [a.reshape(-1, a.shape[-1]) for a in to_cast]
    slab_specs = [pl.BlockSpec((a.shape[0] // steps, a.shape[1]), row) for a in slabs]
    for a in slabs:
        assert a.shape[0] % (steps * 16) == 0
    outs = pl.pallas_call(
        _ffn_kernel,
        grid=(steps,),
        in_specs=[
            pl.BlockSpec((tm, D_MODEL), row),
            vec, vec, vec,
            pl.BlockSpec((D_MODEL, FFN_DIM), fixed, pipeline_mode=once),
            pl.BlockSpec((D_MODEL, FFN_DIM), fixed, pipeline_mode=once),
            pl.BlockSpec((FFN_DIM, D_MODEL), fixed, pipeline_mode=once),
            vec, vec,
        ] + slab_specs,
        out_specs=[pl.BlockSpec((tm, D_MODEL), row)] + slab_specs,
        out_shape=[jax.ShapeDtypeStruct((S, D_MODEL), F32)]
        + [jax.ShapeDtypeStruct(a.shape, BF16) for a in slabs],
        compiler_params=_params(("parallel",)),
        name="dense_ffn_ln",
    )(x, sc, sh, g2, w1, w3, w2, ln_g.reshape(1, D_MODEL), ln_b.reshape(1, D_MODEL), *slabs)
    return outs[0], [o.reshape(a.shape) for o, a in zip(outs[1:], to_cast)]


def _router_kernel(x_ref, sc_ref, sh_ref, rw_ref, h_ref, rank_ref, comb_ref, cum_ref):
    T = x_ref.shape[0]
    tc = MOE_CHUNK
    E = N_EXPERTS
    h = x_ref[...] * (1.0 + sc_ref[...]) + sh_ref[...]
    h_ref[...] = h.astype(BF16)
    logits = lax.dot_general(rw_ref[...], h, (((1,), (1,)), ((), ())),
                             preferred_element_type=F32, precision=HIGHEST)
    eidx = lax.broadcasted_iota(jnp.int32, (E, T), 0)
    v1 = jnp.max(logits, axis=0, keepdims=True)
    i1 = jnp.min(jnp.where(logits == v1, eidx, E), axis=0, keepdims=True)
    m1 = eidx == i1
    rest = jnp.where(m1, -jnp.inf, logits)
    v2 = jnp.max(rest, axis=0, keepdims=True)
    i2 = jnp.min(jnp.where(rest == v2, eidx, E), axis=0, keepdims=True)
    m2 = eidx == i2
    e2 = jnp.exp(v2 - v1)
    g1 = 1.0 / (1.0 + e2)
    g2 = e2 / (1.0 + e2)
    comb = jnp.where(m1, g1, 0.0) + jnp.where(m2, g2, 0.0)
    routed = jnp.logical_or(m1, m2)
    ind = routed.astype(BF16)
    r_i = lax.broadcasted_iota(jnp.int32, (tc, tc), 0)
    c_i = lax.broadcasted_iota(jnp.int32, (tc, tc), 1)
    strict = (r_i < c_i).astype(BF16)
    lane = lax.broadcasted_iota(jnp.int32, (E, LANES), 1)
    running = jnp.zeros((E, 1), F32)
    cum = jnp.zeros((E, LANES), F32)
    for ci in range(T // tc):
        sl = slice(ci * tc, (ci + 1) * tc)
        ind_c = ind[:, sl]
        rank_c = jnp.dot(ind_c, strict, preferred_element_type=F32) + running
        rank_ref[:, ci] = jnp.where(routed[:, sl], rank_c, -1.0).reshape(E, 1, tc)
        comb_ref[:, ci] = comb[:, sl].reshape(E, 1, tc)
        cum = jnp.where(lane == ci, running, cum)
        running = running + jnp.sum(ind_c.astype(F32), axis=1, keepdims=True)
    cum = jnp.where(lane == T // tc, running, cum)
    cum_ref[0] = cum.astype(jnp.int32)


def _router(x, sc, sh, router_w):
    S = x.shape[0]
    T = MOE_BLOCK
    E = N_EXPERTS
    fixed = lambda i: (0, 0)
    vec = pl.BlockSpec((1, D_MODEL), fixed)
    return pl.pallas_call(
        _router_kernel,
        grid=(S // T,),
        in_specs=[
            pl.BlockSpec((T, D_MODEL), lambda i: (i, 0)),
            vec, vec,
            pl.BlockSpec((E, D_MODEL), fixed),
        ],
        out_specs=[
            pl.BlockSpec((T, D_MODEL), lambda i: (i, 0)),
            pl.BlockSpec((E, T // MOE_CHUNK, 1, MOE_CHUNK), lambda i: (0, i, 0, 0)),
            pl.BlockSpec((E, T // MOE_CHUNK, 1, MOE_CHUNK), lambda i: (0, i, 0, 0)),
            pl.BlockSpec((1, E, LANES), lambda i: (i, 0, 0)),
        ],
        out_shape=[
            jax.ShapeDtypeStruct((S, D_MODEL), BF16),
            jax.ShapeDtypeStruct((E, S // MOE_CHUNK, 1, MOE_CHUNK), F32),
            jax.ShapeDtypeStruct((E, S // MOE_CHUNK, 1, MOE_CHUNK), F32),
            jax.ShapeDtypeStruct((S // T, E, LANES), jnp.int32),
        ],
        compiler_params=_params(("parallel",)),
        name="moe_router",
    )(x, sc, sh, router_w.T)


def _moe_kernel(cum_ref, h_ref, rank_ref, comb_ref, w1_ref, w3_ref, w2_ref, y_ref, xe_scr, gate_scr,
                out_scr):
    T = h_ref.shape[0]
    tr, tc, win = MOE_TILE, MOE_CHUNK, MOE_WINDOW
    nchunk = T // tc
    b = pl.program_id(0)
    e = pl.program_id(1)
    f = pl.program_id(2)

    @pl.when(jnp.logical_and(e == 0, f == 0))
    def _():
        y_ref[...] = jnp.zeros(y_ref.shape, F32)

    base = (b * N_EXPERTS + e) * LANES
    count = cum_ref[base + nchunk]

    def token_row(ref, first, n):
        blk = ref[0, pl.ds(first, n)]
        return jnp.concatenate([blk[c] for c in range(n)], axis=1)

    def process(lo, tr):
        rows = pl.ds(pl.multiple_of(lo, 16), tr)

        def hits(first, n):
            want = (lo + lax.broadcasted_iota(jnp.int32, (tr, n * tc), 0)).astype(F32)
            return token_row(rank_ref, first, n) == want

        def token_span(first, n):
            start = first * tc
            if not isinstance(start, int):
                start = pl.multiple_of(start, tc)
            return pl.ds(start, n * tc)

        c_lo = jnp.int32(0)
        c_hi = jnp.int32(-1)
        for c in range(nchunk):
            c_lo += (cum_ref[base + c + 1] <= lo).astype(jnp.int32)
            c_hi += (cum_ref[base + c] < lo + tr).astype(jnp.int32)
        fits = c_hi - c_lo < win
        window = (jnp.minimum(c_lo, nchunk - win), win)
        block = (0, nchunk)

        def gather(first, n):
            hit = hits(first, n)
            xe_scr[rows, :] = jnp.dot(hit.astype(BF16), h_ref[token_span(first, n), :],
                                      preferred_element_type=F32).astype(BF16)
            gate_scr[rows, :] = jnp.sum(jnp.where(hit, token_row(comb_ref, first, n), 0.0),
                                        axis=1, keepdims=True)

        @pl.when(jnp.logical_and(f == 0, fits))
        def _():
            gather(*window)

        @pl.when(jnp.logical_and(f == 0, jnp.logical_not(fits)))
        def _():
            gather(*block)

        xe = xe_scr[rows, :]
        a = jnp.dot(xe, w1_ref[0], preferred_element_type=F32)
        g = jnp.dot(xe, w3_ref[0], preferred_element_type=F32)
        mid = (_silu(a) * g).astype(BF16)
        part = jnp.dot(mid, w2_ref[0], preferred_element_type=F32)

        @pl.when(f == 0)
        def _():
            out_scr[rows, :] = part

        if MOE_FSPLIT > 2:
            @pl.when(jnp.logical_and(f > 0, f < MOE_FSPLIT - 1))
            def _():
                out_scr[rows, :] += part

        @pl.when(f == MOE_FSPLIT - 1)
        def _():
            out_bf = ((out_scr[rows, :] + part) * gate_scr[rows, :]).astype(BF16)

            def scatter(first, n):
                y_ref[token_span(first, n), :] += lax.dot_general(
                    hits(first, n).astype(BF16), out_bf, (((0,), (0,)), ((), ())),
                    preferred_element_type=F32)

            @pl.when(fits)
            def _():
                scatter(*window)

            @pl.when(jnp.logical_not(fits))
            def _():
                scatter(*block)

    full = count // tr
    rem = count - full * tr

    def tile(r, carry):
        process(r * tr, tr)
        return carry

    lax.fori_loop(0, full, tile, 0)

    @pl.when(jnp.logical_and(rem > 0, rem <= MOE_TAIL_TILE))
    def _():
        process(full * tr, MOE_TAIL_TILE)

    @pl.when(rem > MOE_TAIL_TILE)
    def _():
        process(full * tr, tr)


def _moe_ffn(h_bf, rank, comb, cum, w1, w3, w2):
    S = h_bf.shape[0]
    T = MOE_BLOCK
    fw = EXPERT_DIM // MOE_FSPLIT
    grid_spec = pltpu.PrefetchScalarGridSpec(
        num_scalar_prefetch=1,
        grid=(S // T, N_EXPERTS, MOE_FSPLIT),
        in_specs=[
            pl.BlockSpec((T, D_MODEL), lambda b, e, f, c: (b, 0), pipeline_mode=pl.Buffered(1)),
            pl.BlockSpec((1, T // MOE_CHUNK, 1, MOE_CHUNK), lambda b, e, f, c: (e, b, 0, 0)),
            pl.BlockSpec((1, T // MOE_CHUNK, 1, MOE_CHUNK), lambda b, e, f, c: (e, b, 0, 0)),
            pl.BlockSpec((1, D_MODEL, fw), lambda b, e, f, c: (e, 0, f)),
            pl.BlockSpec((1, D_MODEL, fw), lambda b, e, f, c: (e, 0, f)),
            pl.BlockSpec((1, fw, D_MODEL), lambda b, e, f, c: (e, f, 0)),
        ],
        out_specs=pl.BlockSpec((T, D_MODEL), lambda b, e, f, c: (b, 0)),
        scratch_shapes=[
            pltpu.VMEM((T + MOE_TILE, D_MODEL), BF16),
            pltpu.VMEM((T + MOE_TILE, 1), F32),
            pltpu.VMEM((T + MOE_TILE, D_MODEL), F32),
        ],
    )
    return pl.pallas_call(
        _moe_kernel,
        grid_spec=grid_spec,
        out_shape=jax.ShapeDtypeStruct((S, D_MODEL), F32),
        compiler_params=_params(("arbitrary", "arbitrary", "arbitrary")),
        name="moe_experts",
    )(cum.reshape(-1), h_bf, rank, comb, w1, w3, w2)


def _ln_kernel(x_ref, y_ref, g2_ref, lg_ref, lb_ref, o_ref):
    y = ALPHA * x_ref[...] + (1.0 + g2_ref[...]) * y_ref[...]
    o_ref[...] = _layer_norm(y, lg_ref[...], lb_ref[...])


def _residual_ln(x, y, g2, ln_g, ln_b):
    S = x.shape[0]
    tm = ROW_TILE
    row = lambda i: (i, 0)
    vec = pl.BlockSpec((1, D_MODEL), lambda i: (0, 0))
    return pl.pallas_call(
        _ln_kernel,
        grid=(S // tm,),
        in_specs=[pl.BlockSpec((tm, D_MODEL), row), pl.BlockSpec((tm, D_MODEL), row), vec, vec, vec],
        out_specs=pl.BlockSpec((tm, D_MODEL), row),
        out_shape=jax.ShapeDtypeStruct((S, D_MODEL), F32),
        compiler_params=_params(("parallel",)),
        name="residual_ln",
    )(x, y, g2, ln_g.reshape(1, D_MODEL), ln_b.reshape(1, D_MODEL))


def _rope_tables(S):
    d = DA_HEAD_DIM
    tm = ATTN_BLOCK
    inv_freq = 1.0 / (ROPE_THETA ** (jnp.arange(0, d, 2, dtype=F32) / d))
    inv_lane = inv_freq[jnp.arange(LANES) % (d // 2)][None, :]
    base = (jnp.arange(S // tm, dtype=F32) * tm)[:, None] * inv_lane
    offs = jnp.arange(tm, dtype=F32)[:, None] * inv_lane
    shape_b = (S // tm, 1, LANES)
    return jnp.cos(base).reshape(shape_b), jnp.sin(base).reshape(shape_b), jnp.cos(offs), jnp.sin(offs)


def kernel(x, c, w_ada, b_ada, w_in, lam_qk, attn_norm_g, hg_lb_logits, hg_norm_g, pool_w, pool_scale, w_out, ln1_g, ln1_b, ln2_g, ln2_b, ffn_w1, ffn_w3, ffn_w2, router_w, exp_w1, exp_w3, exp_w2):
    B, S, D = x.shape
    assert B == 1 and D == D_MODEL and S % MOE_BLOCK == 0
    xs = x.reshape(S, D)
    mod = _modulation(c, w_ada, b_ada)
    rope = _rope_tables(S)
    experts_bf = {}
    for l in range(DEPTH):
        sh1, sc1, g1, sh2, sc2, g2 = (mod[l, j] for j in range(6))
        qt, k, vt, hg, pu = _in_projection(xs, sc1, sh1, w_in, l, rope)
        a = _diff_attention(qt, k, vt, lam_qk[l], attn_norm_g[l], l)
        r = _hgrn2(hg, hg_lb_logits, hg_norm_g[l], l)
        p = _multiscale_pool(pu, pool_w[l], pool_scale[l])
        xs = _out_projection(a, r, p, xs, w_out, l, g1, ln1_g[l], ln1_b[l])
        if l % 2 == 0:
            i = l // 2
            ride = (exp_w1[i], exp_w3[i], exp_w2[i]) if l + 1 < DEPTH else ()
            xs, cast = _dense_ffn(xs, sc2, sh2, g2, ffn_w1[i].astype(BF16), ffn_w3[i].astype(BF16),
                                  ffn_w2[i].astype(BF16), ln2_g[l], ln2_b[l], ride)
            if ride:
                experts_bf[i] = cast
        else:
            i = l // 2
            h_bf, rank, comb, cum = _router(xs, sc2, sh2, router_w[i])
            w1, w3, w2 = experts_bf.get(i) or [w[i].astype(BF16) for w in (exp_w1, exp_w3, exp_w2)]
            y = _moe_ffn(h_bf, rank, comb, cum, w1, w3, w2)
            xs = _residual_ln(xs, y, g2, ln2_g[l], ln2_b[l])
    return xs.reshape(B, S, D)
```

```python
import functools
import math

import jax
import jax.numpy as jnp
from jax import lax
from jax.experimental import pallas as pl
from jax.experimental.pallas import tpu as pltpu

F32 = jnp.float32
BF16 = jnp.bfloat16
HIGHEST = lax.Precision.HIGHEST

D_MODEL = 1024
DEPTH = 2
DA_HEADS = 4
DA_WIDTH = D_MODEL // 2
DA_HEAD_DIM = DA_WIDTH // (2 * DA_HEADS)
DA_VAL_DIM = 2 * DA_HEAD_DIM
HG_HEADS = 4
HG_WIDTH = D_MODEL // 4
HG_DIM = HG_WIDTH // HG_HEADS
POOL_WINDOWS = (2, 4, 8, 16)
POOL_WIDTH = D_MODEL // 4
POOL_GROUP = POOL_WIDTH // len(POOL_WINDOWS)
D_MIX = DA_WIDTH + HG_WIDTH + POOL_WIDTH
D_IN = 3 * DA_WIDTH + 4 * HG_WIDTH + POOL_WIDTH
ROPE_THETA = 10000.0
MASK_VALUE = -1e30
TINY = 1e-30
FFN_DIM = 11 * D_MODEL // 4
N_EXPERTS = 8
EXPERT_DIM = 7 * D_MODEL // 2
ALPHA = (2 * DEPTH) ** 0.25
EPS = 1e-5
LOG2_E = math.log2(math.e)

LANES = 128
MXU_DIM = 256
VMEM_LIMIT = 60 * 1024 * 1024

ROW_TILE = 512
FFN_CAST_TILE = 256
ATTN_BLOCK = 512
ATTN_Q_BLOCK = 1024
ATTN_QUERY_GROUP = 256
ATTN_UNROLL = 4
ATTN_SUM_ROWS = 16
ATTN_VT_ROWS = DA_VAL_DIM + ATTN_SUM_ROWS
HG_BLOCK = 512
HG_CHUNK = 16
HG_UNROLL = 4
POOL_HALO = 16
MOE_BLOCK = 2048
MOE_TILE = 192
MOE_TAIL_TILE = 128
MOE_CHUNK = 256
MOE_WINDOW = 5
MOE_FSPLIT = 2


def _params(sem, vmem=VMEM_LIMIT):
    return pltpu.CompilerParams(dimension_semantics=sem, vmem_limit_bytes=vmem)


def _layer_norm(y, g, b):
    mu = jnp.mean(y, axis=-1, keepdims=True)
    d = y - mu
    var = jnp.mean(d * d, axis=-1, keepdims=True)
    return d * lax.rsqrt(var + EPS) * g + b


def _silu(x):
    return x * jax.nn.sigmoid(x)


def _head_block_mask(n, group):
    r = lax.broadcasted_iota(jnp.int32, (n, n), 0) // group
    c = lax.broadcasted_iota(jnp.int32, (n, n), 1) // group
    return r == c


def _mod_kernel(c_ref, w_ref, b_ref, o_ref):
    cond = _silu(c_ref[...])
    cond8 = jnp.broadcast_to(cond, (8, D_MODEL))
    r = jnp.dot(cond8, w_ref[0], preferred_element_type=F32, precision=HIGHEST)
    o_ref[0] = r[0:1] + b_ref[0]


def _modulation(c, w_ada, b_ada):
    n = 6 * DEPTH
    b3 = b_ada.reshape(n, 1, D_MODEL)
    out = pl.pallas_call(
        _mod_kernel,
        grid=(DEPTH, 6),
        in_specs=[
            pl.BlockSpec((1, D_MODEL), lambda l, j: (0, 0)),
            pl.BlockSpec((1, D_MODEL, D_MODEL), lambda l, j: (l, 0, j)),
            pl.BlockSpec((1, 1, D_MODEL), lambda l, j: (l * 6 + j, 0, 0)),
        ],
        out_specs=pl.BlockSpec((1, 1, D_MODEL), lambda l, j: (l * 6 + j, 0, 0)),
        out_shape=jax.ShapeDtypeStruct((n, 1, D_MODEL), F32),
        compiler_params=_params(("parallel", "parallel")),
        name="adaln_mod",
    )(c, w_ada, b3)
    return out.reshape(DEPTH, 6, 1, D_MODEL)


def _inproj_kernel(x_ref, sc_ref, sh_ref, wf_ref, cosb_ref, sinb_ref, cosr_ref, sinr_ref, pw_ref, ps_ref,
                   qt_ref, k_ref, vt_ref, hg_ref, p_ref, w_ref, wvt_ref, halo_scr):
    @pl.when(pl.program_id(0) == 0)
    def _():
        w_ref[...] = wf_ref[0].astype(BF16)
        wvt_ref[...] = wf_ref[0, :, 2 * DA_WIDTH:3 * DA_WIDTH].T.astype(BF16)

    h = (x_ref[...] * (1.0 + sc_ref[...]) + sh_ref[...]).astype(BF16)
    qk = jnp.dot(h, w_ref[:, : 2 * DA_WIDTH], preferred_element_type=F32)
    cb, sb, cr, sr = cosb_ref[0], sinb_ref[0], cosr_ref[...], sinr_ref[...]
    cos128 = cb * cr - sb * sr
    sin128 = sb * cr + cb * sr
    half = DA_HEAD_DIM // 2
    lane128 = lax.broadcasted_iota(jnp.int32, cos128.shape, 1)
    sin128 = jnp.where((lane128 & half) == 0, -sin128, sin128)
    reps = 2 * DA_WIDTH // LANES
    cos = jnp.tile(cos128, (1, reps))
    sin = jnp.tile(sin128, (1, reps))
    lane = lax.broadcasted_iota(jnp.int32, qk.shape, 1)
    first_half = (lane & half) == 0
    width = 2 * DA_WIDTH
    partner = jnp.where(first_half, pltpu.roll(qk, width - half, 1), pltpu.roll(qk, half, 1))
    rot = qk * cos + partner * sin
    qt_ref[0] = (rot[:, :DA_WIDTH] * (DA_HEAD_DIM ** -0.5 * LOG2_E)).T.astype(BF16)
    k_ref[...] = rot[:, DA_WIDTH:].astype(BF16)
    o = 2 * DA_WIDTH
    vt = lax.dot_general(wvt_ref[...], h, (((1,), (1,)), ((), ())),
                         preferred_element_type=F32).astype(BF16)
    ones = jnp.ones((ATTN_SUM_ROWS, vt.shape[1]), BF16)
    pieces = []
    for hd in range(DA_HEADS):
        pieces += [vt[hd * DA_VAL_DIM:(hd + 1) * DA_VAL_DIM], ones]
    vt_ref[0] = jnp.concatenate(pieces, axis=0)
    o += DA_WIDTH
    hg_ref[...] = jnp.dot(h, w_ref[:, o:o + 4 * HG_WIDTH], preferred_element_type=F32)
    o += 4 * HG_WIDTH
    pu = jnp.dot(h, w_ref[:, o:o + POOL_WIDTH], preferred_element_type=F32)
    p_ref[...] = _pool_tile(pu, pw_ref[...], ps_ref[...], halo_scr, pl.program_id(0))


def _in_projection(x, sc, sh, w_in, layer, rope, pool_w, pool_scale):
    S = x.shape[0]
    tm = ATTN_BLOCK
    row = lambda i: (i, 0)
    fixed = lambda i: (0, 0)
    cos_b, sin_b, cos_r, sin_r = rope
    w_bd = jax.scipy.linalg.block_diag(*[pool_w[g] for g in range(len(POOL_WINDOWS))]).astype(BF16)
    return pl.pallas_call(
        _inproj_kernel,
        grid=(S // tm,),
        in_specs=[
            pl.BlockSpec((tm, D_MODEL), row),
            pl.BlockSpec((1, D_MODEL), fixed),
            pl.BlockSpec((1, D_MODEL), fixed),
            pl.BlockSpec((1, D_MODEL, D_IN), lambda i: (layer, 0, 0), pipeline_mode=pl.Buffered(1)),
            pl.BlockSpec((1, 1, LANES), lambda i: (i, 0, 0)),
            pl.BlockSpec((1, 1, LANES), lambda i: (i, 0, 0)),
            pl.BlockSpec((tm, LANES), fixed),
            pl.BlockSpec((tm, LANES), fixed),
            pl.BlockSpec((POOL_WIDTH, POOL_WIDTH), fixed),
            pl.BlockSpec((1, POOL_WIDTH), fixed),
        ],
        out_specs=[
            pl.BlockSpec((1, DA_WIDTH, tm), lambda i: (i, 0, 0)),
            pl.BlockSpec((tm, DA_WIDTH), row),
            pl.BlockSpec((1, DA_HEADS * ATTN_VT_ROWS, tm), lambda i: (i, 0, 0)),
            pl.BlockSpec((tm, 4 * HG_WIDTH), row),
            pl.BlockSpec((tm, POOL_WIDTH), row),
        ],
        out_shape=[
            jax.ShapeDtypeStruct((S // tm, DA_WIDTH, tm), BF16),
            jax.ShapeDtypeStruct((S, DA_WIDTH), BF16),
            jax.ShapeDtypeStruct((S // tm, DA_HEADS * ATTN_VT_ROWS, tm), BF16),
            jax.ShapeDtypeStruct((S, 4 * HG_WIDTH), F32),
            jax.ShapeDtypeStruct((S, POOL_WIDTH), BF16),
        ],
        scratch_shapes=[
            pltpu.VMEM((D_MODEL, D_IN), BF16),
            pltpu.VMEM((DA_WIDTH, D_MODEL), BF16),
            pltpu.VMEM((POOL_HALO, POOL_WIDTH), F32),
        ],
        compiler_params=_params(("arbitrary",)),
        name="in_proj",
    )(x, sc, sh, w_in, cos_b, sin_b, cos_r, sin_r, w_bd, pool_scale.reshape(1, POOL_WIDTH))


def _attn_kernel(lam_ref, g_ref, qt_ref, k_ref, vt_ref, o_ref, m_scr, a_scr, c_scr, acc_scr, s_scr,
                 p_scr, *, lam_init):
    bq = qt_ref.shape[0] * qt_ref.shape[2]
    bk = vt_ref.shape[2]
    cg = ATTN_QUERY_GROUP
    i = pl.program_id(1)
    qt = jnp.concatenate([qt_ref[n] for n in range(qt_ref.shape[0])], axis=1)
    rowq = lax.broadcasted_iota(jnp.int32, qt.shape, 0)
    zero = jnp.zeros_like(qt)
    maps = (jnp.where(rowq < DA_HEAD_DIM, qt, zero), jnp.where(rowq >= DA_HEAD_DIM, qt, zero))
    groups = [(a * bq + c0, c0, maps[a][:, c0:c0 + cg]) for a in range(2) for c0 in range(0, bq, cg)]
    m_scr[...] = jnp.full(m_scr.shape, -jnp.inf, F32)
    acc_scr[...] = jnp.zeros(acc_scr.shape, F32)
    hk = bk // 2
    key_i = lax.broadcasted_iota(jnp.int32, (hk, cg), 0)
    qry_i = lax.broadcasted_iota(jnp.int32, (hk, cg), 1)
    ng = len(groups)
    last = ng - 1

    def cols_of(g):
        return slice(groups[g][0], groups[g][0] + cg)

    def ccols_of(sset, g):
        return slice(sset * 2 * bq + groups[g][0], sset * 2 * bq + groups[g][0] + cg)

    def mask_mode(key_offset, g):
        c0 = groups[g][1]
        if key_offset + hk - 1 <= c0:
            return None
        if key_offset > c0 + cg - 1:
            return "skip"
        return key_offset

    def score_stage(hb, sset, g, key_offset):
        kb = k_ref[pl.ds(pl.multiple_of(hb * hk, hk), hk), :]
        s = jnp.dot(kb, groups[g][2], preferred_element_type=F32)
        if key_offset is not None:
            s = jnp.where(key_i + key_offset <= qry_i + groups[g][1], s, MASK_VALUE)
        s_scr[sset * ng + g] = s
        c_scr[:, ccols_of(sset, g)] = jnp.max(s, axis=0, keepdims=True)

    def max_stage(sset, g):
        cols = cols_of(g)
        m_prev = m_scr[:, cols]
        m_new = jnp.maximum(m_prev, c_scr[:, ccols_of(sset, g)])
        a_scr[:, cols] = jnp.exp2(m_prev - m_new)
        m_scr[:, cols] = m_new

    def exp_stage(sset, g, vt_half, carried, defer):
        cols = cols_of(g)
        p = jnp.exp2(s_scr[sset * ng + g] - m_scr[:, cols])
        alpha = a_scr[:, cols]
        pb = p.astype(BF16)
        acc = acc_scr[:, cols]
        if carried is not None:
            acc = acc + carried
        if defer:
            acc_scr[:, cols] = alpha * acc
            p_scr[...] = pb
        else:
            acc_scr[:, cols] = alpha * acc + jnp.dot(vt_half, pb, preferred_element_type=F32)

    def items(key_offset):
        if key_offset is None:
            return [(g, None) for g in range(ng)]
        return [(g, mask_mode(key_offset, g)) for g in range(ng) if mask_mode(key_offset, g) != "skip"]

    def iteration(t, offs, look):
        vt_a = vt_ref[t, :, 0:hk]
        vt_b = vt_ref[t, :, hk:bk]
        carried = jnp.dot(vt_ref[jnp.maximum(t - 1, 0), :, hk:bk], p_scr[...],
                          preferred_element_type=F32)
        a_items = items(offs)
        b_items = items(None if offs is None else offs + hk)
        n_items = [] if look == "stop" else items(look)
        score_stage(2 * t + 1, 1, *b_items[0])
        for n, (g, _) in enumerate(a_items):
            if n + 1 < len(b_items):
                score_stage(2 * t + 1, 1, *b_items[n + 1])
            if n + 1 < len(a_items):
                max_stage(0, a_items[n + 1][0])
            else:
                max_stage(1, b_items[0][0])
            exp_stage(0, g, vt_a, carried if g == last else None, False)
        if n_items:
            score_stage(2 * t + 2, 0, *n_items[0])
        for n, (g, _) in enumerate(b_items):
            if n + 1 < len(n_items):
                score_stage(2 * t + 2, 0, *n_items[n + 1])
            if n + 1 < len(b_items):
                max_stage(1, b_items[n + 1][0])
            elif n_items:
                max_stage(0, n_items[0][0])
            exp_stage(1, g, vt_b, None, bool(n_items) and g == last)

    p_scr[...] = jnp.zeros(p_scr.shape, BF16)
    for g in range(ng):
        score_stage(0, 0, g, (0 - i) * bq)
    max_stage(0, 0)

    per_q = bq // bk
    first = i * per_q
    unroll = ATTN_UNROLL

    def body(u, carry):
        for k in range(unroll):
            iteration(unroll * u + k, None, None)
        return carry

    n_plain = jnp.maximum(first - 1, 0)
    trips = n_plain // unroll
    lax.fori_loop(0, trips, body, 0)
    for k in range(unroll - 1):
        @pl.when(trips * unroll + k < n_plain)
        def _():
            iteration(trips * unroll + k, None, None)

    @pl.when(i > 0)
    def _():
        iteration(first - 1, None, 0)

    for d in range(per_q):
        iteration(first + d, d * bk, (d + 1) * bk if d + 1 < per_q else "stop")

    lq = lam_ref[...]
    lam = (jnp.exp(jnp.sum(lq[0:1] * lq[1:2], axis=1, keepdims=True))
           - jnp.exp(jnp.sum(lq[2:3] * lq[3:4], axis=1, keepdims=True)) + lam_init)
    o = acc_scr[0:DA_VAL_DIM, :] / acc_scr[DA_VAL_DIM:DA_VAL_DIM + 1, :]
    o = o[:, :bq] - lam * o[:, bq:]
    ms = jnp.mean(o * o, axis=0, keepdims=True)
    o = o * lax.rsqrt(ms + EPS) * g_ref[...] * (1.0 - lam_init)
    o_ref[...] = o.T.astype(BF16)


def _diff_attention(qt, k, vt, lam_qk, attn_g, layer):
    S = k.shape[0]
    bk = ATTN_BLOCK
    bq = ATTN_Q_BLOCK
    nk = S // bk
    lam_init = 0.8 - 0.6 * math.exp(-0.3 * layer)
    return pl.pallas_call(
        functools.partial(_attn_kernel, lam_init=lam_init),
        grid=(DA_HEADS, S // bq),
        in_specs=[
            pl.BlockSpec((4, DA_HEAD_DIM), lambda h, i: (0, 0)),
            pl.BlockSpec((DA_VAL_DIM, 1), lambda h, i: (0, 0)),
            pl.BlockSpec((bq // bk, LANES, bk), lambda h, i: (i, h, 0)),
            pl.BlockSpec((S, LANES), lambda h, i: (0, h)),
            pl.BlockSpec((nk, ATTN_VT_ROWS, bk), lambda h, i: (0, h, 0)),
        ],
        out_specs=pl.BlockSpec((bq, DA_VAL_DIM), lambda h, i: (i, h)),
        out_shape=jax.ShapeDtypeStruct((S, DA_WIDTH), BF16),
        scratch_shapes=[
            pltpu.VMEM((1, 2 * bq), F32),
            pltpu.VMEM((1, 2 * bq), F32),
            pltpu.VMEM((1, 2 * 2 * bq), F32),
            pltpu.VMEM((ATTN_VT_ROWS, 2 * bq), F32),
            pltpu.VMEM((2 * (2 * bq // ATTN_QUERY_GROUP), bk // 2, ATTN_QUERY_GROUP), F32),
            pltpu.VMEM((bk // 2, ATTN_QUERY_GROUP), BF16),
        ],
        compiler_params=_params(("parallel", "parallel")),
        name="diff_attn",
    )(lam_qk, attn_g.reshape(DA_VAL_DIM, 1), qt, k, vt)


def _hgrn_kernel(lbl_ref, g_ref, hg_ref, o_ref, st_scr, stb_scr, o_scr, *, layer):
    T = hg_ref.shape[0]
    c = HG_CHUNK
    W = HG_WIDTH

    @pl.when(pl.program_id(0) == 0)
    def _():
        st_scr[...] = jnp.zeros(st_scr.shape, F32)
        stb_scr[...] = jnp.zeros(stb_scr.shape, BF16)

    lg = lbl_ref[...]
    ex = jnp.exp(lg - jnp.max(lg, axis=0, keepdims=True))
    prob = ex / jnp.sum(ex, axis=0, keepdims=True)
    lb = jnp.zeros((1, W), F32)
    for li in range(1, layer + 1):
        lb = lb + prob[li:li + 1]

    same_head = _head_block_mask(W, HG_DIM)
    ones_bd = same_head.astype(BF16)
    rowi = lax.broadcasted_iota(jnp.int32, (c, W), 0)

    def body(s, carry):
        sl = pl.ds(pl.multiple_of(s * c, c), c)
        qf = hg_ref[sl, 0:W]
        z = hg_ref[sl, W:2 * W]
        iv = hg_ref[sl, 2 * W:3 * W]
        f = lb + (1.0 - lb) * jax.nn.sigmoid(z)
        logf = jnp.log(jnp.maximum(f, TINY))
        key = (1.0 - lb) * jax.nn.sigmoid(-z)
        val = _silu(iv)
        b = logf
        sft = 1
        while sft < c:
            b = b + jnp.where(rowi >= sft, pltpu.roll(b, sft, 0), 0.0)
            sft *= 2
        b_last = b[c - 1:c]
        kd = (key * jnp.exp(b_last - b)).astype(BF16)
        upd = lax.dot_general(val.astype(BF16), kd, (((0,), (0,)), ((), ())),
                              preferred_element_type=F32)
        es = [(qf * key).astype(BF16)]
        vrs = [val]
        for d in range(1, c):
            kr = pltpu.roll(key, d, 0)
            br = pltpu.roll(b, d, 0)
            e = jnp.where(rowi >= d, qf * kr * jnp.exp(b - br), 0.0)
            es.append(e.astype(BF16))
            vrs.append(pltpu.roll(val, d, 0))
        estack = jnp.concatenate(es, axis=0)
        r = jnp.dot(estack, ones_bd, preferred_element_type=F32)
        o = r[0:c] * vrs[0]
        for d in range(1, c):
            o = o + r[d * c:(d + 1) * c] * vrs[d]
        qd = (qf * jnp.exp(b)).astype(BF16)
        o = o + lax.dot_general(qd, stb_scr[...], (((1,), (1,)), ((), ())),
                                preferred_element_type=F32)
        o_scr[sl, :] = o
        decay = jnp.exp(b_last)
        for hd in range(HG_HEADS):
            rws = slice(hd * HG_DIM, (hd + 1) * HG_DIM)
            lns = slice(hd * HG_DIM // LANES * LANES, (hd * HG_DIM // LANES + 1) * LANES)
            new = decay[:, lns] * st_scr[rws, lns] + jnp.where(same_head[rws, lns], upd[rws, lns], 0.0)
            st_scr[rws, lns] = new
            stb_scr[rws, lns] = new.astype(BF16)
        return carry

    lax.fori_loop(0, T // c, body, 0, unroll=HG_UNROLL)

    o = o_scr[...]
    ms = jnp.dot(o * o, same_head.astype(F32), preferred_element_type=F32,
                 precision=HIGHEST) * (1.0 / HG_DIM)
    gate = hg_ref[:, 3 * W:4 * W]
    o_ref[...] = (o * lax.rsqrt(ms + EPS) * g_ref[...] * _silu(gate)).astype(BF16)


def _hgrn2(hg, lb_logits, norm_g, layer):
    S = hg.shape[0]
    T = HG_BLOCK
    g_t = jnp.tile(norm_g.reshape(1, HG_DIM), (1, HG_HEADS))
    return pl.pallas_call(
        functools.partial(_hgrn_kernel, layer=layer),
        grid=(S // T,),
        in_specs=[
            pl.BlockSpec((DEPTH, HG_WIDTH), lambda i: (0, 0)),
            pl.BlockSpec((1, HG_WIDTH), lambda i: (0, 0)),
            pl.BlockSpec((T, 4 * HG_WIDTH), lambda i: (i, 0)),
        ],
        out_specs=pl.BlockSpec((T, HG_WIDTH), lambda i: (i, 0)),
        out_shape=jax.ShapeDtypeStruct((S, HG_WIDTH), BF16),
        scratch_shapes=[
            pltpu.VMEM((HG_WIDTH, HG_WIDTH), F32),
            pltpu.VMEM((HG_WIDTH, HG_WIDTH), BF16),
            pltpu.VMEM((T, HG_WIDTH), F32),
        ],
        compiler_params=_params(("arbitrary",)),
        name="hgrn2",
    )(lb_logits, g_t, hg)


def _pool_tile(u, w_bd, scale, halo_scr, blk):
    T = u.shape[0]
    H = POOL_HALO

    @pl.when(blk == 0)
    def _():
        halo_scr[...] = jnp.zeros(halo_scr.shape, F32)

    ext = jnp.concatenate([halo_scr[...], u], axis=0)
    halo_scr[...] = u[T - H:, :]
    sums = []
    s = ext
    w = 1
    while w < POOL_WINDOWS[-1]:
        s = s + pltpu.roll(s, w, 0)
        w *= 2
        sums.append(s[H:, :])
    t1 = (blk * T + 1 + lax.broadcasted_iota(jnp.int32, (T, POOL_WIDTH), 0)).astype(F32)
    lane = lax.broadcasted_iota(jnp.int32, (T, POOL_WIDTH), 1)
    pooled = None
    for gi, win in enumerate(POOL_WINDOWS):
        mean = sums[gi] / jnp.minimum(t1, float(win))
        pooled = mean if pooled is None else jnp.where(lane >= gi * POOL_GROUP, mean, pooled)
    pooled = pooled - u
    y = jnp.dot(pooled.astype(BF16), w_bd, preferred_element_type=F32)
    return (y * scale).astype(BF16)


def _outproj_kernel(a_ref, r_ref, p_ref, x_ref, wf_ref, g1_ref, lg_ref, lb_ref, o_ref, w_ref):
    @pl.when(pl.program_id(0) == 0)
    def _():
        w_ref[...] = wf_ref[0].astype(BF16)

    m = jnp.dot(a_ref[...], w_ref[0:DA_WIDTH, :], preferred_element_type=F32)
    m += jnp.dot(r_ref[...], w_ref[DA_WIDTH:DA_WIDTH + HG_WIDTH, :], preferred_element_type=F32)
    m += jnp.dot(p_ref[...], w_ref[DA_WIDTH + HG_WIDTH:, :], preferred_element_type=F32)
    y = ALPHA * x_ref[...] + (1.0 + g1_ref[...]) * m
    o_ref[...] = _layer_norm(y, lg_ref[...], lb_ref[...])


def _out_projection(a, r, p, x, w_out, layer, g1, ln_g, ln_b):
    S = x.shape[0]
    tm = ROW_TILE
    row = lambda i: (i, 0)
    fixed = lambda i: (0, 0)
    vec = pl.BlockSpec((1, D_MODEL), fixed)
    return pl.pallas_call(
        _outproj_kernel,
        grid=(S // tm,),
        in_specs=[
            pl.BlockSpec((tm, DA_WIDTH), row),
            pl.BlockSpec((tm, HG_WIDTH), row),
            pl.BlockSpec((tm, POOL_WIDTH), row),
            pl.BlockSpec((tm, D_MODEL), row),
            pl.BlockSpec((1, D_MIX, D_MODEL), lambda i: (layer, 0, 0), pipeline_mode=pl.Buffered(1)),
            vec, vec, vec,
        ],
        out_specs=pl.BlockSpec((tm, D_MODEL), row),
        out_shape=jax.ShapeDtypeStruct((S, D_MODEL), F32),
        scratch_shapes=[pltpu.VMEM((D_MIX, D_MODEL), BF16)],
        compiler_params=_params(("arbitrary",)),
        name="out_proj_ln",
    )(a, r, p, x, w_out, g1, ln_g.reshape(1, D_MODEL), ln_b.reshape(1, D_MODEL))


def _ffn_kernel(x_ref, sc_ref, sh_ref, g2_ref, w1_ref, w3_ref, w2_ref, lg_ref, lb_ref, *rest):
    n_cast = (len(rest) - 1) // 2
    o_ref = rest[n_cast]
    for src, dst in zip(rest[:n_cast], rest[n_cast + 1:]):
        dst[...] = src[...].astype(BF16)
    x = x_ref[...]
    h = (x * (1.0 + sc_ref[...]) + sh_ref[...]).astype(BF16)
    acc = jnp.zeros(x.shape, F32)
    step = 4 * MXU_DIM
    for lo in range(0, FFN_DIM, step):
        hi = min(lo + step, FFN_DIM)
        a = jnp.dot(h, w1_ref[:, lo:hi], preferred_element_type=F32)
        b = jnp.dot(h, w3_ref[:, lo:hi], preferred_element_type=F32)
        acc += jnp.dot((_silu(a) * b).astype(BF16), w2_ref[lo:hi, :], preferred_element_type=F32)
    y = ALPHA * x + (1.0 + g2_ref[...]) * acc
    o_ref[...] = _layer_norm(y, lg_ref[...], lb_ref[...])


def _dense_ffn(x, sc, sh, g2, w1, w3, w2, ln_g, ln_b, to_cast=()):
    S = x.shape[0]
    tm = FFN_CAST_TILE if to_cast else ROW_TILE
    steps = S // tm
    row = lambda i: (i, 0)
    fixed = lambda i: (0, 0)
    vec = pl.BlockSpec((1, D_MODEL), fixed)
    once = pl.Buffered(1)
    slabs = [a.reshape(-1, a.shape[-1]) for a in to_cast]
    slab_specs = [pl.BlockSpec((a.shape[0] // steps, a.shape[1]), row) for a in slabs]
    for a in slabs:
        assert a.shape[0] % (steps * 16) == 0
    outs = pl.pallas_call(
        _ffn_kernel,
        grid=(steps,),
        in_specs=[
            pl.BlockSpec((tm, D_MODEL), row),
            vec, vec, vec,
            pl.BlockSpec((D_MODEL, FFN_DIM), fixed, pipeline_mode=once),
            pl.BlockSpec((D_MODEL, FFN_DIM), fixed, pipeline_mode=once),
            pl.BlockSpec((FFN_DIM, D_MODEL), fixed, pipeline_mode=once),
            vec, vec,
        ] + slab_specs,
        out_specs=[pl.BlockSpec((tm, D_MODEL), row)] + slab_specs,
        out_shape=[jax.ShapeDtypeStruct((S, D_MODEL), F32)]
        + [jax.ShapeDtypeStruct(a.shape, BF16) for a in slabs],
        compiler_params=_params(("parallel",)),
        name="dense_ffn_ln",
    )(x, sc, sh, g2, w1, w3, w2, ln_g.reshape(1, D_MODEL), ln_b.reshape(1, D_MODEL), *slabs)
    return outs[0], [o.reshape(a.shape) for o, a in zip(outs[1:], to_cast)]


def _router_kernel(x_ref, sc_ref, sh_ref, rw_ref, h_ref, rank_ref, comb_ref, cum_ref):
    T = x_ref.shape[0]
    tc = MOE_CHUNK
    E = N_EXPERTS
    h = x_ref[...] * (1.0 + sc_ref[...]) + sh_ref[...]
    h_ref[...] = h.astype(BF16)
    logits = lax.dot_general(rw_ref[...], h, (((1,), (1,)), ((), ())),
                             preferred_element_type=F32, precision=HIGHEST)
    eidx = lax.broadcasted_iota(jnp.int32, (E, T), 0)
    v1 = jnp.max(logits, axis=0, keepdims=True)
    i1 = jnp.min(jnp.where(logits == v1, eidx, E), axis=0, keepdims=True)
    m1 = eidx == i1
    rest = jnp.where(m1, -jnp.inf, logits)
    v2 = jnp.max(rest, axis=0, keepdims=True)
    i2 = jnp.min(jnp.where(rest == v2, eidx, E), axis=0, keepdims=True)
    m2 = eidx == i2
    e2 = jnp.exp(v2 - v1)
    g1 = 1.0 / (1.0 + e2)
    g2 = e2 / (1.0 + e2)
    comb = jnp.where(m1, g1, 0.0) + jnp.where(m2, g2, 0.0)
    routed = jnp.logical_or(m1, m2)
    ind = routed.astype(BF16)
    r_i = lax.broadcasted_iota(jnp.int32, (tc, tc), 0)
    c_i = lax.broadcasted_iota(jnp.int32, (tc, tc), 1)
    strict = (r_i < c_i).astype(BF16)
    lane = lax.broadcasted_iota(jnp.int32, (E, LANES), 1)
    running = jnp.zeros((E, 1), F32)
    cum = jnp.zeros((E, LANES), F32)
    for ci in range(T // tc):
        sl = slice(ci * tc, (ci + 1) * tc)
        ind_c = ind[:, sl]
        rank_c = jnp.dot(ind_c, strict, preferred_element_type=F32) + running
        rank_ref[:, ci] = jnp.where(routed[:, sl], rank_c, -1.0).reshape(E, 1, tc)
        comb_ref[:, ci] = comb[:, sl].reshape(E, 1, tc)
        cum = jnp.where(lane == ci, running, cum)
        running = running + jnp.sum(ind_c.astype(F32), axis=1, keepdims=True)
    cum = jnp.where(lane == T // tc, running, cum)
    cum_ref[0] = cum.astype(jnp.int32)


def _router(x, sc, sh, router_w):
    S = x.shape[0]
    T = MOE_BLOCK
    E = N_EXPERTS
    fixed = lambda i: (0, 0)
    vec = pl.BlockSpec((1, D_MODEL), fixed)
    return pl.pallas_call(
        _router_kernel,
        grid=(S // T,),
        in_specs=[
            pl.BlockSpec((T, D_MODEL), lambda i: (i, 0)),
            vec, vec,
            pl.BlockSpec((E, D_MODEL), fixed),
        ],
        out_specs=[
            pl.BlockSpec((T, D_MODEL), lambda i: (i, 0)),
            pl.BlockSpec((E, T // MOE_CHUNK, 1, MOE_CHUNK), lambda i: (0, i, 0, 0)),
            pl.BlockSpec((E, T // MOE_CHUNK, 1, MOE_CHUNK), lambda i: (0, i, 0, 0)),
            pl.BlockSpec((1, E, LANES), lambda i: (i, 0, 0)),
        ],
        out_shape=[
            jax.ShapeDtypeStruct((S, D_MODEL), BF16),
            jax.ShapeDtypeStruct((E, S // MOE_CHUNK, 1, MOE_CHUNK), F32),
            jax.ShapeDtypeStruct((E, S // MOE_CHUNK, 1, MOE_CHUNK), F32),
            jax.ShapeDtypeStruct((S // T, E, LANES), jnp.int32),
        ],
        compiler_params=_params(("parallel",)),
        name="moe_router",
    )(x, sc, sh, router_w.T)


def _moe_kernel(cum_ref, h_ref, rank_ref, comb_ref, w1_ref, w3_ref, w2_ref, y_ref, xe_scr, gate_scr,
                out_scr):
    T = h_ref.shape[0]
    tr, tc, win = MOE_TILE, MOE_CHUNK, MOE_WINDOW
    nchunk = T // tc
    b = pl.program_id(0)
    e = pl.program_id(1)
    f = pl.program_id(2)

    @pl.when(jnp.logical_and(e == 0, f == 0))
    def _():
        y_ref[...] = jnp.zeros(y_ref.shape, F32)

    base = (b * N_EXPERTS + e) * LANES
    count = cum_ref[base + nchunk]

    def token_row(ref, first, n):
        blk = ref[0, pl.ds(first, n)]
        return jnp.concatenate([blk[c] for c in range(n)], axis=1)

    def process(lo, tr):
        rows = pl.ds(pl.multiple_of(lo, 16), tr)

        def hits(first, n):
            want = (lo + lax.broadcasted_iota(jnp.int32, (tr, n * tc), 0)).astype(F32)
            return token_row(rank_ref, first, n) == want

        def token_span(first, n):
            start = first * tc
            if not isinstance(start, int):
                start = pl.multiple_of(start, tc)
            return pl.ds(start, n * tc)

        c_lo = jnp.int32(0)
        c_hi = jnp.int32(-1)
        for c in range(nchunk):
            c_lo += (cum_ref[base + c + 1] <= lo).astype(jnp.int32)
            c_hi += (cum_ref[base + c] < lo + tr).astype(jnp.int32)
        fits = c_hi - c_lo < win
        window = (jnp.minimum(c_lo, nchunk - win), win)
        block = (0, nchunk)

        def gather(first, n):
            hit = hits(first, n)
            xe_scr[rows, :] = jnp.dot(hit.astype(BF16), h_ref[token_span(first, n), :],
                                      preferred_element_type=F32).astype(BF16)
            gate_scr[rows, :] = jnp.sum(jnp.where(hit, token_row(comb_ref, first, n), 0.0),
                                        axis=1, keepdims=True)

        @pl.when(jnp.logical_and(f == 0, fits))
        def _():
            gather(*window)

        @pl.when(jnp.logical_and(f == 0, jnp.logical_not(fits)))
        def _():
            gather(*block)

        xe = xe_scr[rows, :]
        a = jnp.dot(xe, w1_ref[0], preferred_element_type=F32)
        g = jnp.dot(xe, w3_ref[0], preferred_element_type=F32)
        mid = (_silu(a) * g).astype(BF16)
        part = jnp.dot(mid, w2_ref[0], preferred_element_type=F32)

        @pl.when(f == 0)
        def _():
            out_scr[rows, :] = part

        if MOE_FSPLIT > 2:
            @pl.when(jnp.logical_and(f > 0, f < MOE_FSPLIT - 1))
            def _():
                out_scr[rows, :] += part

        @pl.when(f == MOE_FSPLIT - 1)
        def _():
            out_bf = ((out_scr[rows, :] + part) * gate_scr[rows, :]).astype(BF16)

            def scatter(first, n):
                y_ref[token_span(first, n), :] += lax.dot_general(
                    hits(first, n).astype(BF16), out_bf, (((0,), (0,)), ((), ())),
                    preferred_element_type=F32)

            @pl.when(fits)
            def _():
                scatter(*window)

            @pl.when(jnp.logical_not(fits))
            def _():
                scatter(*block)

    full = count // tr
    rem = count - full * tr

    def tile(r, carry):
        process(r * tr, tr)
        return carry

    lax.fori_loop(0, full, tile, 0)

    @pl.when(jnp.logical_and(rem > 0, rem <= MOE_TAIL_TILE))
    def _():
        process(full * tr, MOE_TAIL_TILE)

    @pl.when(rem > MOE_TAIL_TILE)
    def _():
        process(full * tr, tr)


def _moe_ffn(h_bf, rank, comb, cum, w1, w3, w2):
    S = h_bf.shape[0]
    T = MOE_BLOCK
    fw = EXPERT_DIM // MOE_FSPLIT
    grid_spec = pltpu.PrefetchScalarGridSpec(
        num_scalar_prefetch=1,
        grid=(S // T, N_EXPERTS, MOE_FSPLIT),
        in_specs=[
            pl.BlockSpec((T, D_MODEL), lambda b, e, f, c: (b, 0), pipeline_mode=pl.Buffered(1)),
            pl.BlockSpec((1, T // MOE_CHUNK, 1, MOE_CHUNK), lambda b, e, f, c: (e, b, 0, 0)),
            pl.BlockSpec((1, T // MOE_CHUNK, 1, MOE_CHUNK), lambda b, e, f, c: (e, b, 0, 0)),
            pl.BlockSpec((1, D_MODEL, fw), lambda b, e, f, c: (e, 0, f)),
            pl.BlockSpec((1, D_MODEL, fw), lambda b, e, f, c: (e, 0, f)),
            pl.BlockSpec((1, fw, D_MODEL), lambda b, e, f, c: (e, f, 0)),
        ],
        out_specs=pl.BlockSpec((T, D_MODEL), lambda b, e, f, c: (b, 0)),
        scratch_shapes=[
            pltpu.VMEM((T + MOE_TILE, D_MODEL), BF16),
            pltpu.VMEM((T + MOE_TILE, 1), F32),
            pltpu.VMEM((T + MOE_TILE, D_MODEL), F32),
        ],
    )
    return pl.pallas_call(
        _moe_kernel,
        grid_spec=grid_spec,
        out_shape=jax.ShapeDtypeStruct((S, D_MODEL), F32),
        compiler_params=_params(("arbitrary", "arbitrary", "arbitrary")),
        name="moe_experts",
    )(cum.reshape(-1), h_bf, rank, comb, w1, w3, w2)


def _ln_kernel(x_ref, y_ref, g2_ref, lg_ref, lb_ref, o_ref):
    y = ALPHA * x_ref[...] + (1.0 + g2_ref[...]) * y_ref[...]
    o_ref[...] = _layer_norm(y, lg_ref[...], lb_ref[...])


def _residual_ln(x, y, g2, ln_g, ln_b):
    S = x.shape[0]
    tm = ROW_TILE
    row = lambda i: (i, 0)
    vec = pl.BlockSpec((1, D_MODEL), lambda i: (0, 0))
    return pl.pallas_call(
        _ln_kernel,
        grid=(S // tm,),
        in_specs=[pl.BlockSpec((tm, D_MODEL), row), pl.BlockSpec((tm, D_MODEL), row), vec, vec, vec],
        out_specs=pl.BlockSpec((tm, D_MODEL), row),
        out_shape=jax.ShapeDtypeStruct((S, D_MODEL), F32),
        compiler_params=_params(("parallel",)),
        name="residual_ln",
    )(x, y, g2, ln_g.reshape(1, D_MODEL), ln_b.reshape(1, D_MODEL))


def _rope_tables(S):
    d = DA_HEAD_DIM
    tm = ATTN_BLOCK
    inv_freq = 1.0 / (ROPE_THETA ** (jnp.arange(0, d, 2, dtype=F32) / d))
    inv_lane = inv_freq[jnp.arange(LANES) % (d // 2)][None, :]
    base = (jnp.arange(S // tm, dtype=F32) * tm)[:, None] * inv_lane
    offs = jnp.arange(tm, dtype=F32)[:, None] * inv_lane
    shape_b = (S // tm, 1, LANES)
    return jnp.cos(base).reshape(shape_b), jnp.sin(base).reshape(shape_b), jnp.cos(offs), jnp.sin(offs)


def kernel(x, c, w_ada, b_ada, w_in, lam_qk, attn_norm_g, hg_lb_logits, hg_norm_g, pool_w, pool_scale, w_out, ln1_g, ln1_b, ln2_g, ln2_b, ffn_w1, ffn_w3, ffn_w2, router_w, exp_w1, exp_w3, exp_w2):
    B, S, D = x.shape
    assert B == 1 and D == D_MODEL and S % MOE_BLOCK == 0
    xs = x.reshape(S, D)
    mod = _modulation(c, w_ada, b_ada)
    rope = _rope_tables(S)
    experts_bf = {}
    for l in range(DEPTH):
        sh1, sc1, g1, sh2, sc2, g2 = (mod[l, j] for j in range(6))
        qt, k, vt, hg, p = _in_projection(xs, sc1, sh1, w_in, l, rope, pool_w[l], pool_scale[l])
        a = _diff_attention(qt, k, vt, lam_qk[l], attn_norm_g[l], l)
        r = _hgrn2(hg, hg_lb_logits, hg_norm_g[l], l)
        xs = _out_projection(a, r, p, xs, w_out, l, g1, ln1_g[l], ln1_b[l])
        if l % 2 == 0:
            i = l // 2
            ride = (exp_w1[i], exp_w3[i], exp_w2[i]) if l + 1 < DEPTH else ()
            xs, cast = _dense_ffn(xs, sc2, sh2, g2, ffn_w1[i].astype(BF16), ffn_w3[i].astype(BF16),
                                  ffn_w2[i].astype(BF16), ln2_g[l], ln2_b[l], ride)
            if ride:
                experts_bf[i] = cast
        else:
            i = l // 2
            h_bf, rank, comb, cum = _router(xs, sc2, sh2, router_w[i])
            w1, w3, w2 = experts_bf.get(i) or [w[i].astype(BF16) for w in (exp_w1, exp_w3, exp_w2)]
            y = _moe_ffn(h_bf, rank, comb, cum, w1, w3, w2)
            xs = _residual_ln(xs, y, g2, ln2_g[l], ln2_b[l])
    return xs.reshape(B, S, D)
```

```python
import functools
import math

import jax
import jax.numpy as jnp
from jax import lax
from jax.experimental import pallas as pl
from jax.experimental.pallas import tpu as pltpu

F32 = jnp.float32
BF16 = jnp.bfloat16
HIGHEST = lax.Precision.HIGHEST

D_MODEL = 1024
DEPTH = 2
DA_HEADS = 4
DA_WIDTH = D_MODEL // 2
DA_HEAD_DIM = DA_WIDTH // (2 * DA_HEADS)
DA_VAL_DIM = 2 * DA_HEAD_DIM
HG_HEADS = 4
HG_WIDTH = D_MODEL // 4
HG_DIM = HG_WIDTH // HG_HEADS
POOL_WINDOWS = (2, 4, 8, 16)
POOL_WIDTH = D_MODEL // 4
POOL_GROUP = POOL_WIDTH // len(POOL_WINDOWS)
D_MIX = DA_WIDTH + HG_WIDTH + POOL_WIDTH
D_IN = 3 * DA_WIDTH + 4 * HG_WIDTH + POOL_WIDTH
ROPE_THETA = 10000.0
MASK_VALUE = -1e30
TINY = 1e-30
FFN_DIM = 11 * D_MODEL // 4
N_EXPERTS = 8
EXPERT_DIM = 7 * D_MODEL // 2
ALPHA = (2 * DEPTH) ** 0.25
EPS = 1e-5
LOG2_E = math.log2(math.e)

LANES = 128
MXU_DIM = 256
VMEM_LIMIT = 60 * 1024 * 1024

ROW_TILE = 512
FFN_CAST_TILE = 256
ATTN_BLOCK = 512
ATTN_Q_BLOCK = 2048
ATTN_QUERY_GROUP = 256
ATTN_UNROLL = 4
ATTN_SUM_ROWS = 16
ATTN_VT_ROWS = DA_VAL_DIM + ATTN_SUM_ROWS
HG_BLOCK = 512
HG_CHUNK = 16
HG_UNROLL = 4
POOL_HALO = 16
MOE_BLOCK = 2048
MOE_TILE = 192
MOE_TAIL_TILE = 128
MOE_CHUNK = 256
MOE_WINDOW = 5
MOE_FSPLIT = 2


def _params(sem, vmem=VMEM_LIMIT):
    return pltpu.CompilerParams(dimension_semantics=sem, vmem_limit_bytes=vmem)


def _layer_norm(y, g, b):
    mu = jnp.mean(y, axis=-1, keepdims=True)
    d = y - mu
    var = jnp.mean(d * d, axis=-1, keepdims=True)
    return d * lax.rsqrt(var + EPS) * g + b


def _silu(x):
    return x * jax.nn.sigmoid(x)


def _head_block_mask(n, group):
    r = lax.broadcasted_iota(jnp.int32, (n, n), 0) // group
    c = lax.broadcasted_iota(jnp.int32, (n, n), 1) // group
    return r == c


def _mod_kernel(c_ref, w_ref, b_ref, o_ref):
    cond = _silu(c_ref[...])
    cond8 = jnp.broadcast_to(cond, (8, D_MODEL))
    r = jnp.dot(cond8, w_ref[0], preferred_element_type=F32, precision=HIGHEST)
    o_ref[0] = r[0:1] + b_ref[0]


def _modulation(c, w_ada, b_ada):
    n = 6 * DEPTH
    b3 = b_ada.reshape(n, 1, D_MODEL)
    out = pl.pallas_call(
        _mod_kernel,
        grid=(DEPTH, 6),
        in_specs=[
            pl.BlockSpec((1, D_MODEL), lambda l, j: (0, 0)),
            pl.BlockSpec((1, D_MODEL, D_MODEL), lambda l, j: (l, 0, j)),
            pl.BlockSpec((1, 1, D_MODEL), lambda l, j: (l * 6 + j, 0, 0)),
        ],
        out_specs=pl.BlockSpec((1, 1, D_MODEL), lambda l, j: (l * 6 + j, 0, 0)),
        out_shape=jax.ShapeDtypeStruct((n, 1, D_MODEL), F32),
        compiler_params=_params(("parallel", "parallel")),
        name="adaln_mod",
    )(c, w_ada, b3)
    return out.reshape(DEPTH, 6, 1, D_MODEL)


def _inproj_kernel(x_ref, sc_ref, sh_ref, wf_ref, cosb_ref, sinb_ref, cosr_ref, sinr_ref, pw_ref, ps_ref,
                   qt_ref, k_ref, vt_ref, hg_ref, p_ref, w_ref, wvt_ref, halo_scr):
    @pl.when(pl.program_id(0) == 0)
    def _():
        w_ref[...] = wf_ref[0].astype(BF16)
        wvt_ref[...] = wf_ref[0, :, 2 * DA_WIDTH:3 * DA_WIDTH].T.astype(BF16)

    h = (x_ref[...] * (1.0 + sc_ref[...]) + sh_ref[...]).astype(BF16)
    qk = jnp.dot(h, w_ref[:, : 2 * DA_WIDTH], preferred_element_type=F32)
    cb, sb, cr, sr = cosb_ref[0], sinb_ref[0], cosr_ref[...], sinr_ref[...]
    cos128 = cb * cr - sb * sr
    sin128 = sb * cr + cb * sr
    half = DA_HEAD_DIM // 2
    lane128 = lax.broadcasted_iota(jnp.int32, cos128.shape, 1)
    sin128 = jnp.where((lane128 & half) == 0, -sin128, sin128)
    reps = 2 * DA_WIDTH // LANES
    cos = jnp.tile(cos128, (1, reps))
    sin = jnp.tile(sin128, (1, reps))
    lane = lax.broadcasted_iota(jnp.int32, qk.shape, 1)
    first_half = (lane & half) == 0
    width = 2 * DA_WIDTH
    partner = jnp.where(first_half, pltpu.roll(qk, width - half, 1), pltpu.roll(qk, half, 1))
    rot = qk * cos + partner * sin
    qt_ref[0] = (rot[:, :DA_WIDTH] * (DA_HEAD_DIM ** -0.5 * LOG2_E)).T.astype(BF16)
    k_ref[...] = rot[:, DA_WIDTH:].astype(BF16)
    o = 2 * DA_WIDTH
    vt = lax.dot_general(wvt_ref[...], h, (((1,), (1,)), ((), ())),
                         preferred_element_type=F32).astype(BF16)
    ones = jnp.ones((ATTN_SUM_ROWS, vt.shape[1]), BF16)
    pieces = []
    for hd in range(DA_HEADS):
        pieces += [vt[hd * DA_VAL_DIM:(hd + 1) * DA_VAL_DIM], ones]
    vt_ref[0] = jnp.concatenate(pieces, axis=0)
    o += DA_WIDTH
    hg_ref[...] = jnp.dot(h, w_ref[:, o:o + 4 * HG_WIDTH], preferred_element_type=F32)
    o += 4 * HG_WIDTH
    pu = jnp.dot(h, w_ref[:, o:o + POOL_WIDTH], preferred_element_type=F32)
    p_ref[...] = _pool_tile(pu, pw_ref[...], ps_ref[...], halo_scr, pl.program_id(0))


def _in_projection(x, sc, sh, w_in, layer, rope, pool_w, pool_scale):
    S = x.shape[0]
    tm = ATTN_BLOCK
    row = lambda i: (i, 0)
    fixed = lambda i: (0, 0)
    cos_b, sin_b, cos_r, sin_r = rope
    w_bd = jax.scipy.linalg.block_diag(*[pool_w[g] for g in range(len(POOL_WINDOWS))]).astype(BF16)
    return pl.pallas_call(
        _inproj_kernel,
        grid=(S // tm,),
        in_specs=[
            pl.BlockSpec((tm, D_MODEL), row),
            pl.BlockSpec((1, D_MODEL), fixed),
            pl.BlockSpec((1, D_MODEL), fixed),
            pl.BlockSpec((1, D_MODEL, D_IN), lambda i: (layer, 0, 0), pipeline_mode=pl.Buffered(1)),
            pl.BlockSpec((1, 1, LANES), lambda i: (i, 0, 0)),
            pl.BlockSpec((1, 1, LANES), lambda i: (i, 0, 0)),
            pl.BlockSpec((tm, LANES), fixed),
            pl.BlockSpec((tm, LANES), fixed),
            pl.BlockSpec((POOL_WIDTH, POOL_WIDTH), fixed),
            pl.BlockSpec((1, POOL_WIDTH), fixed),
        ],
        out_specs=[
            pl.BlockSpec((1, DA_WIDTH, tm), lambda i: (i, 0, 0)),
            pl.BlockSpec((tm, DA_WIDTH), row),
            pl.BlockSpec((1, DA_HEADS * ATTN_VT_ROWS, tm), lambda i: (i, 0, 0)),
            pl.BlockSpec((tm, 4 * HG_WIDTH), row),
            pl.BlockSpec((tm, POOL_WIDTH), row),
        ],
        out_shape=[
            jax.ShapeDtypeStruct((S // tm, DA_WIDTH, tm), BF16),
            jax.ShapeDtypeStruct((S, DA_WIDTH), BF16),
            jax.ShapeDtypeStruct((S // tm, DA_HEADS * ATTN_VT_ROWS, tm), BF16),
            jax.ShapeDtypeStruct((S, 4 * HG_WIDTH), F32),
            jax.ShapeDtypeStruct((S, POOL_WIDTH), BF16),
        ],
        scratch_shapes=[
            pltpu.VMEM((D_MODEL, D_IN), BF16),
            pltpu.VMEM((DA_WIDTH, D_MODEL), BF16),
            pltpu.VMEM((POOL_HALO, POOL_WIDTH), F32),
        ],
        compiler_params=_params(("arbitrary",)),
        name="in_proj",
    )(x, sc, sh, w_in, cos_b, sin_b, cos_r, sin_r, w_bd, pool_scale.reshape(1, POOL_WIDTH))


def _attn_kernel(lam_ref, g_ref, qt_ref, k_ref, vt_ref, o_ref, m_scr, a_scr, c_scr, acc_scr, s_scr,
                 p_scr, *, lam_init):
    bq = qt_ref.shape[0] * qt_ref.shape[2]
    bk = vt_ref.shape[2]
    cg = ATTN_QUERY_GROUP
    i = pl.program_id(1)
    qt = jnp.concatenate([qt_ref[n] for n in range(qt_ref.shape[0])], axis=1)
    rowq = lax.broadcasted_iota(jnp.int32, qt.shape, 0)
    zero = jnp.zeros_like(qt)
    maps = (jnp.where(rowq < DA_HEAD_DIM, qt, zero), jnp.where(rowq >= DA_HEAD_DIM, qt, zero))
    groups = [(a * bq + c0, c0, maps[a][:, c0:c0 + cg]) for a in range(2) for c0 in range(0, bq, cg)]
    m_scr[...] = jnp.full(m_scr.shape, -jnp.inf, F32)
    acc_scr[...] = jnp.zeros(acc_scr.shape, F32)
    hk = bk // 2
    key_i = lax.broadcasted_iota(jnp.int32, (hk, cg), 0)
    qry_i = lax.broadcasted_iota(jnp.int32, (hk, cg), 1)
    ng = len(groups)
    last = ng - 1

    def cols_of(g):
        return slice(groups[g][0], groups[g][0] + cg)

    def ccols_of(sset, g):
        return slice(sset * 2 * bq + groups[g][0], sset * 2 * bq + groups[g][0] + cg)

    def mask_mode(key_offset, g):
        c0 = groups[g][1]
        if key_offset + hk - 1 <= c0:
            return None
        if key_offset > c0 + cg - 1:
            return "skip"
        return key_offset

    def score_stage(hb, sset, g, key_offset):
        kb = k_ref[pl.ds(pl.multiple_of(hb * hk, hk), hk), :]
        s = jnp.dot(kb, groups[g][2], preferred_element_type=F32)
        if key_offset is not None:
            s = jnp.where(key_i + key_offset <= qry_i + groups[g][1], s, MASK_VALUE)
        s_scr[sset * ng + g] = s
        c_scr[:, ccols_of(sset, g)] = jnp.max(s, axis=0, keepdims=True)

    def max_stage(sset, g):
        cols = cols_of(g)
        m_prev = m_scr[:, cols]
        m_new = jnp.maximum(m_prev, c_scr[:, ccols_of(sset, g)])
        a_scr[:, cols] = jnp.exp2(m_prev - m_new)
        m_scr[:, cols] = m_new

    def exp_stage(sset, g, vt_half, carried, defer):
        cols = cols_of(g)
        p = jnp.exp2(s_scr[sset * ng + g] - m_scr[:, cols])
        alpha = a_scr[:, cols]
        pb = p.astype(BF16)
        acc = acc_scr[:, cols]
        if carried is not None:
            acc = acc + carried
        if defer:
            acc_scr[:, cols] = alpha * acc
            p_scr[...] = pb
        else:
            acc_scr[:, cols] = alpha * acc + jnp.dot(vt_half, pb, preferred_element_type=F32)

    def items(key_offset):
        if key_offset is None:
            return [(g, None) for g in range(ng)]
        return [(g, mask_mode(key_offset, g)) for g in range(ng) if mask_mode(key_offset, g) != "skip"]

    def iteration(t, offs, look):
        vt_a = vt_ref[t, :, 0:hk]
        vt_b = vt_ref[t, :, hk:bk]
        carried = jnp.dot(vt_ref[jnp.maximum(t - 1, 0), :, hk:bk], p_scr[...],
                          preferred_element_type=F32)
        a_items = items(offs)
        b_items = items(None if offs is None else offs + hk)
        n_items = [] if look == "stop" else items(look)
        score_stage(2 * t + 1, 1, *b_items[0])
        for n, (g, _) in enumerate(a_items):
            if n + 1 < len(b_items):
                score_stage(2 * t + 1, 1, *b_items[n + 1])
            if n + 1 < len(a_items):
                max_stage(0, a_items[n + 1][0])
            else:
                max_stage(1, b_items[0][0])
            exp_stage(0, g, vt_a, carried if g == last else None, False)
        if n_items:
            score_stage(2 * t + 2, 0, *n_items[0])
        for n, (g, _) in enumerate(b_items):
            if n + 1 < len(n_items):
                score_stage(2 * t + 2, 0, *n_items[n + 1])
            if n + 1 < len(b_items):
                max_stage(1, b_items[n + 1][0])
            elif n_items:
                max_stage(0, n_items[0][0])
            exp_stage(1, g, vt_b, None, bool(n_items) and g == last)

    p_scr[...] = jnp.zeros(p_scr.shape, BF16)
    for g in range(ng):
        score_stage(0, 0, g, (0 - i) * bq)
    max_stage(0, 0)

    per_q = bq // bk
    first = i * per_q
    unroll = ATTN_UNROLL

    def body(u, carry):
        for k in range(unroll):
            iteration(unroll * u + k, None, None)
        return carry

    n_plain = jnp.maximum(first - 1, 0)
    trips = n_plain // unroll
    lax.fori_loop(0, trips, body, 0)
    for k in range(unroll - 1):
        @pl.when(trips * unroll + k < n_plain)
        def _():
            iteration(trips * unroll + k, None, None)

    @pl.when(i > 0)
    def _():
        iteration(first - 1, None, 0)

    for d in range(per_q):
        iteration(first + d, d * bk, (d + 1) * bk if d + 1 < per_q else "stop")

    lq = lam_ref[...]
    lam = (jnp.exp(jnp.sum(lq[0:1] * lq[1:2], axis=1, keepdims=True))
           - jnp.exp(jnp.sum(lq[2:3] * lq[3:4], axis=1, keepdims=True)) + lam_init)
    o = acc_scr[0:DA_VAL_DIM, :] / acc_scr[DA_VAL_DIM:DA_VAL_DIM + 1, :]
    o = o[:, :bq] - lam * o[:, bq:]
    ms = jnp.mean(o * o, axis=0, keepdims=True)
    o = o * lax.rsqrt(ms + EPS) * g_ref[...] * (1.0 - lam_init)
    o_ref[...] = o.T.astype(BF16)


def _diff_attention(qt, k, vt, lam_qk, attn_g, layer):
    S = k.shape[0]
    bk = ATTN_BLOCK
    bq = ATTN_Q_BLOCK
    nk = S // bk
    lam_init = 0.8 - 0.6 * math.exp(-0.3 * layer)
    return pl.pallas_call(
        functools.partial(_attn_kernel, lam_init=lam_init),
        grid=(DA_HEADS, S // bq),
        in_specs=[
            pl.BlockSpec((4, DA_HEAD_DIM), lambda h, i: (0, 0)),
            pl.BlockSpec((DA_VAL_DIM, 1), lambda h, i: (0, 0)),
            pl.BlockSpec((bq // bk, LANES, bk), lambda h, i: (i, h, 0)),
            pl.BlockSpec((S, LANES), lambda h, i: (0, h)),
            pl.BlockSpec((nk, ATTN_VT_ROWS, bk), lambda h, i: (0, h, 0)),
        ],
        out_specs=pl.BlockSpec((bq, DA_VAL_DIM), lambda h, i: (i, h)),
        out_shape=jax.ShapeDtypeStruct((S, DA_WIDTH), BF16),
        scratch_shapes=[
            pltpu.VMEM((1, 2 * bq), F32),
            pltpu.VMEM((1, 2 * bq), F32),
            pltpu.VMEM((1, 2 * 2 * bq), F32),
            pltpu.VMEM((ATTN_VT_ROWS, 2 * bq), F32),
            pltpu.VMEM((2 * (2 * bq // ATTN_QUERY_GROUP), bk // 2, ATTN_QUERY_GROUP), F32),
            pltpu.VMEM((bk // 2, ATTN_QUERY_GROUP), BF16),
        ],
        compiler_params=_params(("parallel", "parallel")),
        name="diff_attn",
    )(lam_qk, attn_g.reshape(DA_VAL_DIM, 1), qt, k, vt)


def _hgrn_kernel(lbl_ref, g_ref, hg_ref, o_ref, st_scr, stb_scr, o_scr, *, layer):
    T = hg_ref.shape[0]
    c = HG_CHUNK
    W = HG_WIDTH

    @pl.when(pl.program_id(0) == 0)
    def _():
        st_scr[...] = jnp.zeros(st_scr.shape, F32)
        stb_scr[...] = jnp.zeros(stb_scr.shape, BF16)

    lg = lbl_ref[...]
    ex = jnp.exp(lg - jnp.max(lg, axis=0, keepdims=True))
    prob = ex / jnp.sum(ex, axis=0, keepdims=True)
    lb = jnp.zeros((1, W), F32)
    for li in range(1, layer + 1):
        lb = lb + prob[li:li + 1]

    same_head = _head_block_mask(W, HG_DIM)
    ones_bd = same_head.astype(BF16)
    rowi = lax.broadcasted_iota(jnp.int32, (c, W), 0)

    def body(s, carry):
        sl = pl.ds(pl.multiple_of(s * c, c), c)
        qf = hg_ref[sl, 0:W]
        z = hg_ref[sl, W:2 * W]
        iv = hg_ref[sl, 2 * W:3 * W]
        f = lb + (1.0 - lb) * jax.nn.sigmoid(z)
        logf = jnp.log(jnp.maximum(f, TINY))
        key = (1.0 - lb) * jax.nn.sigmoid(-z)
        val = _silu(iv)
        b = logf
        sft = 1
        while sft < c:
            b = b + jnp.where(rowi >= sft, pltpu.roll(b, sft, 0), 0.0)
            sft *= 2
        b_last = b[c - 1:c]
        kd = (key * jnp.exp(b_last - b)).astype(BF16)
        upd = lax.dot_general(val.astype(BF16), kd, (((0,), (0,)), ((), ())),
                              preferred_element_type=F32)
        es = [(qf * key).astype(BF16)]
        vrs = [val]
        for d in range(1, c):
            kr = pltpu.roll(key, d, 0)
            br = pltpu.roll(b, d, 0)
            e = jnp.where(rowi >= d, qf * kr * jnp.exp(b - br), 0.0)
            es.append(e.astype(BF16))
            vrs.append(pltpu.roll(val, d, 0))
        estack = jnp.concatenate(es, axis=0)
        r = jnp.dot(estack, ones_bd, preferred_element_type=F32)
        o = r[0:c] * vrs[0]
        for d in range(1, c):
            o = o + r[d * c:(d + 1) * c] * vrs[d]
        qd = (qf * jnp.exp(b)).astype(BF16)
        o = o + lax.dot_general(qd, stb_scr[...], (((1,), (1,)), ((), ())),
                                preferred_element_type=F32)
        o_scr[sl, :] = o
        decay = jnp.exp(b_last)
        for hd in range(HG_HEADS):
            rws = slice(hd * HG_DIM, (hd + 1) * HG_DIM)
            lns = slice(hd * HG_DIM // LANES * LANES, (hd * HG_DIM // LANES + 1) * LANES)
            new = decay[:, lns] * st_scr[rws, lns] + jnp.where(same_head[rws, lns], upd[rws, lns], 0.0)
            st_scr[rws, lns] = new
            stb_scr[rws, lns] = new.astype(BF16)
        return carry

    lax.fori_loop(0, T // c, body, 0, unroll=HG_UNROLL)

    o = o_scr[...]
    ms = jnp.dot(o * o, same_head.astype(F32), preferred_element_type=F32,
                 precision=HIGHEST) * (1.0 / HG_DIM)
    gate = hg_ref[:, 3 * W:4 * W]
    o_ref[...] = (o * lax.rsqrt(ms + EPS) * g_ref[...] * _silu(gate)).astype(BF16)


def _hgrn2(hg, lb_logits, norm_g, layer):
    S = hg.shape[0]
    T = HG_BLOCK
    g_t = jnp.tile(norm_g.reshape(1, HG_DIM), (1, HG_HEADS))
    return pl.pallas_call(
        functools.partial(_hgrn_kernel, layer=layer),
        grid=(S // T,),
        in_specs=[
            pl.BlockSpec((DEPTH, HG_WIDTH), lambda i: (0, 0)),
            pl.BlockSpec((1, HG_WIDTH), lambda i: (0, 0)),
            pl.BlockSpec((T, 4 * HG_WIDTH), lambda i: (i, 0)),
        ],
        out_specs=pl.BlockSpec((T, HG_WIDTH), lambda i: (i, 0)),
        out_shape=jax.ShapeDtypeStruct((S, HG_WIDTH), BF16),
        scratch_shapes=[
            pltpu.VMEM((HG_WIDTH, HG_WIDTH), F32),
            pltpu.VMEM((HG_WIDTH, HG_WIDTH), BF16),
            pltpu.VMEM((T, HG_WIDTH), F32),
        ],
        compiler_params=_params(("arbitrary",)),
        name="hgrn2",
    )(lb_logits, g_t, hg)


def _pool_tile(u, w_bd, scale, halo_scr, blk):
    T = u.shape[0]
    H = POOL_HALO

    @pl.when(blk == 0)
    def _():
        halo_scr[...] = jnp.zeros(halo_scr.shape, F32)

    ext = jnp.concatenate([halo_scr[...], u], axis=0)
    halo_scr[...] = u[T - H:, :]
    sums = []
    s = ext
    w = 1
    while w < POOL_WINDOWS[-1]:
        s = s + pltpu.roll(s, w, 0)
        w *= 2
        sums.append(s[H:, :])
    t1 = (blk * T + 1 + lax.broadcasted_iota(jnp.int32, (T, POOL_WIDTH), 0)).astype(F32)
    lane = lax.broadcasted_iota(jnp.int32, (T, POOL_WIDTH), 1)
    pooled = None
    for gi, win in enumerate(POOL_WINDOWS):
        mean = sums[gi] / jnp.minimum(t1, float(win))
        pooled = mean if pooled is None else jnp.where(lane >= gi * POOL_GROUP, mean, pooled)
    pooled = pooled - u
    y = jnp.dot(pooled.astype(BF16), w_bd, preferred_element_type=F32)
    return (y * scale).astype(BF16)


def _outproj_kernel(a_ref, r_ref, p_ref, x_ref, wf_ref, g1_ref, lg_ref, lb_ref, o_ref, w_ref):
    @pl.when(pl.program_id(0) == 0)
    def _():
        w_ref[...] = wf_ref[0].astype(BF16)

    m = jnp.dot(a_ref[...], w_ref[0:DA_WIDTH, :], preferred_element_type=F32)
    m += jnp.dot(r_ref[...], w_ref[DA_WIDTH:DA_WIDTH + HG_WIDTH, :], preferred_element_type=F32)
    m += jnp.dot(p_ref[...], w_ref[DA_WIDTH + HG_WIDTH:, :], preferred_element_type=F32)
    y = ALPHA * x_ref[...] + (1.0 + g1_ref[...]) * m
    o_ref[...] = _layer_norm(y, lg_ref[...], lb_ref[...])


def _out_projection(a, r, p, x, w_out, layer, g1, ln_g, ln_b):
    S = x.shape[0]
    tm = ROW_TILE
    row = lambda i: (i, 0)
    fixed = lambda i: (0, 0)
    vec = pl.BlockSpec((1, D_MODEL), fixed)
    return pl.pallas_call(
        _outproj_kernel,
        grid=(S // tm,),
        in_specs=[
            pl.BlockSpec((tm, DA_WIDTH), row),
            pl.BlockSpec((tm, HG_WIDTH), row),
            pl.BlockSpec((tm, POOL_WIDTH), row),
            pl.BlockSpec((tm, D_MODEL), row),
            pl.BlockSpec((1, D_MIX, D_MODEL), lambda i: (layer, 0, 0), pipeline_mode=pl.Buffered(1)),
            vec, vec, vec,
        ],
        out_specs=pl.BlockSpec((tm, D_MODEL), row),
        out_shape=jax.ShapeDtypeStruct((S, D_MODEL), F32),
        scratch_shapes=[pltpu.VMEM((D_MIX, D_MODEL), BF16)],
        compiler_params=_params(("arbitrary",)),
        name="out_proj_ln",
    )(a, r, p, x, w_out, g1, ln_g.reshape(1, D_MODEL), ln_b.reshape(1, D_MODEL))


def _ffn_kernel(x_ref, sc_ref, sh_ref, g2_ref, w1_ref, w3_ref, w2_ref, lg_ref, lb_ref, *rest):
    n_cast = (len(rest) - 1) // 2
    o_ref = rest[n_cast]
    for src, dst in zip(rest[:n_cast], rest[n_cast + 1:]):
        dst[...] = src[...].astype(BF16)
    x = x_ref[...]
    h = (x * (1.0 + sc_ref[...]) + sh_ref[...]).astype(BF16)
    acc = jnp.zeros(x.shape, F32)
    step = 4 * MXU_DIM
    for lo in range(0, FFN_DIM, step):
        hi = min(lo + step, FFN_DIM)
        a = jnp.dot(h, w1_ref[:, lo:hi], preferred_element_type=F32)
        b = jnp.dot(h, w3_ref[:, lo:hi], preferred_element_type=F32)
        acc += jnp.dot((_silu(a) * b).astype(BF16), w2_ref[lo:hi, :], preferred_element_type=F32)
    y = ALPHA * x + (1.0 + g2_ref[...]) * acc
    o_ref[...] = _layer_norm(y, lg_ref[...], lb_ref[...])


def _dense_ffn(x, sc, sh, g2, w1, w3, w2, ln_g, ln_b, to_cast=()):
    S = x.shape[0]
    tm = FFN_CAST_TILE if to_cast else ROW_TILE
    steps = S // tm
    row = lambda i: (i, 0)
    fixed = lambda i: (0, 0)
    vec = pl.BlockSpec((1, D_MODEL), fixed)
    once = pl.Buffered(1)
    slabs = [a.reshape(-1, a.shape[-1]) for a in to_cast]
    slab_specs = [pl.BlockSpec((a.shape[0] // steps, a.shape[1]), row) for a in slabs]
    for a in slabs:
        assert a.shape[0] % (steps * 16) == 0
    outs = pl.pallas_call(
        _ffn_kernel,
        grid=(steps,),
        in_specs=[
            pl.BlockSpec((tm, D_MODEL), row),
            vec, vec, vec,
            pl.BlockSpec((D_MODEL, FFN_DIM), fixed, pipeline_mode=once),
            pl.BlockSpec((D_MODEL, FFN_DIM), fixed, pipeline_mode=once),
            pl.BlockSpec((FFN_DIM, D_MODEL), fixed, pipeline_mode=once),
            vec, vec,
        ] + slab_specs,
        out_specs=[pl.BlockSpec((tm, D_MODEL), row)] + slab_specs,
        out_shape=[jax.ShapeDtypeStruct((S, D_MODEL), F32)]
        + [jax.ShapeDtypeStruct(a.shape, BF16) for a in slabs],
        compiler_params=_params(("parallel",)),
        name="dense_ffn_ln",
    )(x, sc, sh, g2, w1, w3, w2, ln_g.reshape(1, D_MODEL), ln_b.reshape(1, D_MODEL), *slabs)
    return outs[0], [o.reshape(a.shape) for o, a in zip(outs[1:], to_cast)]


def _router_kernel(x_ref, sc_ref, sh_ref, rw_ref, h_ref, rank_ref, comb_ref, cum_ref):
    T = x_ref.shape[0]
    tc = MOE_CHUNK
    E = N_EXPERTS
    h = x_ref[...] * (1.0 + sc_ref[...]) + sh_ref[...]
    h_ref[...] = h.astype(BF16)
    logits = lax.dot_general(rw_ref[...], h, (((1,), (1,)), ((), ())),
                             preferred_element_type=F32, precision=HIGHEST)
    eidx = lax.broadcasted_iota(jnp.int32, (E, T), 0)
    v1 = jnp.max(logits, axis=0, keepdims=True)
    i1 = jnp.min(jnp.where(logits == v1, eidx, E), axis=0, keepdims=True)
    m1 = eidx == i1
    rest = jnp.where(m1, -jnp.inf, logits)
    v2 = jnp.max(rest, axis=0, keepdims=True)
    i2 = jnp.min(jnp.where(rest == v2, eidx, E), axis=0, keepdims=True)
    m2 = eidx == i2
    e2 = jnp.exp(v2 - v1)
    g1 = 1.0 / (1.0 + e2)
    g2 = e2 / (1.0 + e2)
    comb = jnp.where(m1, g1, 0.0) + jnp.where(m2, g2, 0.0)
    routed = jnp.logical_or(m1, m2)
    ind = routed.astype(BF16)
    r_i = lax.broadcasted_iota(jnp.int32, (tc, tc), 0)
    c_i = lax.broadcasted_iota(jnp.int32, (tc, tc), 1)
    strict = (r_i < c_i).astype(BF16)
    lane = lax.broadcasted_iota(jnp.int32, (E, LANES), 1)
    running = jnp.zeros((E, 1), F32)
    cum = jnp.zeros((E, LANES), F32)
    for ci in range(T // tc):
        sl = slice(ci * tc, (ci + 1) * tc)
        ind_c = ind[:, sl]
        rank_c = jnp.dot(ind_c, strict, preferred_element_type=F32) + running
        rank_ref[:, ci] = jnp.where(routed[:, sl], rank_c, -1.0).reshape(E, 1, tc)
        comb_ref[:, ci] = comb[:, sl].reshape(E, 1, tc)
        cum = jnp.where(lane == ci, running, cum)
        running = running + jnp.sum(ind_c.astype(F32), axis=1, keepdims=True)
    cum = jnp.where(lane == T // tc, running, cum)
    cum_ref[0] = cum.astype(jnp.int32)


def _router(x, sc, sh, router_w):
    S = x.shape[0]
    T = MOE_BLOCK
    E = N_EXPERTS
    fixed = lambda i: (0, 0)
    vec = pl.BlockSpec((1, D_MODEL), fixed)
    return pl.pallas_call(
        _router_kernel,
        grid=(S // T,),
        in_specs=[
            pl.BlockSpec((T, D_MODEL), lambda i: (i, 0)),
            vec, vec,
            pl.BlockSpec((E, D_MODEL), fixed),
        ],
        out_specs=[
            pl.BlockSpec((T, D_MODEL), lambda i: (i, 0)),
            pl.BlockSpec((E, T // MOE_CHUNK, 1, MOE_CHUNK), lambda i: (0, i, 0, 0)),
            pl.BlockSpec((E, T // MOE_CHUNK, 1, MOE_CHUNK), lambda i: (0, i, 0, 0)),
            pl.BlockSpec((1, E, LANES), lambda i: (i, 0, 0)),
        ],
        out_shape=[
            jax.ShapeDtypeStruct((S, D_MODEL), BF16),
            jax.ShapeDtypeStruct((E, S // MOE_CHUNK, 1, MOE_CHUNK), F32),
            jax.ShapeDtypeStruct((E, S // MOE_CHUNK, 1, MOE_CHUNK), F32),
            jax.ShapeDtypeStruct((S // T, E, LANES), jnp.int32),
        ],
        compiler_params=_params(("parallel",)),
        name="moe_router",
    )(x, sc, sh, router_w.T)


def _moe_kernel(cum_ref, h_ref, rank_ref, comb_ref, w1_ref, w3_ref, w2_ref, y_ref, xe_scr, gate_scr,
                out_scr):
    T = h_ref.shape[0]
    tr, tc, win = MOE_TILE, MOE_CHUNK, MOE_WINDOW
    nchunk = T // tc
    b = pl.program_id(0)
    e = pl.program_id(1)
    f = pl.program_id(2)

    @pl.when(jnp.logical_and(e == 0, f == 0))
    def _():
        y_ref[...] = jnp.zeros(y_ref.shape, F32)

    base = (b * N_EXPERTS + e) * LANES
    count = cum_ref[base + nchunk]

    def token_row(ref, first, n):
        blk = ref[0, pl.ds(first, n)]
        return jnp.concatenate([blk[c] for c in range(n)], axis=1)

    def process(lo, tr):
        rows = pl.ds(pl.multiple_of(lo, 16), tr)

        def hits(first, n):
            want = (lo + lax.broadcasted_iota(jnp.int32, (tr, n * tc), 0)).astype(F32)
            return token_row(rank_ref, first, n) == want

        def token_span(first, n):
            start = first * tc
            if not isinstance(start, int):
                start = pl.multiple_of(start, tc)
            return pl.ds(start, n * tc)

        c_lo = jnp.int32(0)
        c_hi = jnp.int32(-1)
        for c in range(nchunk):
            c_lo += (cum_ref[base + c + 1] <= lo).astype(jnp.int32)
            c_hi += (cum_ref[base + c] < lo + tr).astype(jnp.int32)
        fits = c_hi - c_lo < win
        window = (jnp.minimum(c_lo, nchunk - win), win)
        block = (0, nchunk)

        def gather(first, n):
            hit = hits(first, n)
            xe_scr[rows, :] = jnp.dot(hit.astype(BF16), h_ref[token_span(first, n), :],
                                      preferred_element_type=F32).astype(BF16)
            gate_scr[rows, :] = jnp.sum(jnp.where(hit, token_row(comb_ref, first, n), 0.0),
                                        axis=1, keepdims=True)

        @pl.when(jnp.logical_and(f == 0, fits))
        def _():
            gather(*window)

        @pl.when(jnp.logical_and(f == 0, jnp.logical_not(fits)))
        def _():
            gather(*block)

        xe = xe_scr[rows, :]
        a = jnp.dot(xe, w1_ref[0], preferred_element_type=F32)
        g = jnp.dot(xe, w3_ref[0], preferred_element_type=F32)
        mid = (_silu(a) * g).astype(BF16)
        part = jnp.dot(mid, w2_ref[0], preferred_element_type=F32)

        @pl.when(f == 0)
        def _():
            out_scr[rows, :] = part

        if MOE_FSPLIT > 2:
            @pl.when(jnp.logical_and(f > 0, f < MOE_FSPLIT - 1))
            def _():
                out_scr[rows, :] += part

        @pl.when(f == MOE_FSPLIT - 1)
        def _():
            out_bf = ((out_scr[rows, :] + part) * gate_scr[rows, :]).astype(BF16)

            def scatter(first, n):
                y_ref[token_span(first, n), :] += lax.dot_general(
                    hits(first, n).astype(BF16), out_bf, (((0,), (0,)), ((), ())),
                    preferred_element_type=F32)

            @pl.when(fits)
            def _():
                scatter(*window)

            @pl.when(jnp.logical_not(fits))
            def _():
                scatter(*block)

    full = count // tr
    rem = count - full * tr

    def tile(r, carry):
        process(r * tr, tr)
        return carry

    lax.fori_loop(0, full, tile, 0)

    @pl.when(jnp.logical_and(rem > 0, rem <= MOE_TAIL_TILE))
    def _():
        process(full * tr, MOE_TAIL_TILE)

    @pl.when(rem > MOE_TAIL_TILE)
    def _():
        process(full * tr, tr)


def _moe_ffn(h_bf, rank, comb, cum, w1, w3, w2):
    S = h_bf.shape[0]
    T = MOE_BLOCK
    fw = EXPERT_DIM // MOE_FSPLIT
    grid_spec = pltpu.PrefetchScalarGridSpec(
        num_scalar_prefetch=1,
        grid=(S // T, N_EXPERTS, MOE_FSPLIT),
        in_specs=[
            pl.BlockSpec((T, D_MODEL), lambda b, e, f, c: (b, 0), pipeline_mode=pl.Buffered(1)),
            pl.BlockSpec((1, T // MOE_CHUNK, 1, MOE_CHUNK), lambda b, e, f, c: (e, b, 0, 0)),
            pl.BlockSpec((1, T // MOE_CHUNK, 1, MOE_CHUNK), lambda b, e, f, c: (e, b, 0, 0)),
            pl.BlockSpec((1, D_MODEL, fw), lambda b, e, f, c: (e, 0, f)),
            pl.BlockSpec((1, D_MODEL, fw), lambda b, e, f, c: (e, 0, f)),
            pl.BlockSpec((1, fw, D_MODEL), lambda b, e, f, c: (e, f, 0)),
        ],
        out_specs=pl.BlockSpec((T, D_MODEL), lambda b, e, f, c: (b, 0)),
        scratch_shapes=[
            pltpu.VMEM((T + MOE_TILE, D_MODEL), BF16),
            pltpu.VMEM((T + MOE_TILE, 1), F32),
            pltpu.VMEM((T + MOE_TILE, D_MODEL), F32),
        ],
    )
    return pl.pallas_call(
        _moe_kernel,
        grid_spec=grid_spec,
        out_shape=jax.ShapeDtypeStruct((S, D_MODEL), F32),
        compiler_params=_params(("arbitrary", "arbitrary", "arbitrary")),
        name="moe_experts",
    )(cum.reshape(-1), h_bf, rank, comb, w1, w3, w2)


def _ln_kernel(x_ref, y_ref, g2_ref, lg_ref, lb_ref, o_ref):
    y = ALPHA * x_ref[...] + (1.0 + g2_ref[...]) * y_ref[...]
    o_ref[...] = _layer_norm(y, lg_ref[...], lb_ref[...])


def _residual_ln(x, y, g2, ln_g, ln_b):
    S = x.shape[0]
    tm = ROW_TILE
    row = lambda i: (i, 0)
    vec = pl.BlockSpec((1, D_MODEL), lambda i: (0, 0))
    return pl.pallas_call(
        _ln_kernel,
        grid=(S // tm,),
        in_specs=[pl.BlockSpec((tm, D_MODEL), row), pl.BlockSpec((tm, D_MODEL), row), vec, vec, vec],
        out_specs=pl.BlockSpec((tm, D_MODEL), row),
        out_shape=jax.ShapeDtypeStruct((S, D_MODEL), F32),
        compiler_params=_params(("parallel",)),
        name="residual_ln",
    )(x, y, g2, ln_g.reshape(1, D_MODEL), ln_b.reshape(1, D_MODEL))


def _rope_tables(S):
    d = DA_HEAD_DIM
    tm = ATTN_BLOCK
    inv_freq = 1.0 / (ROPE_THETA ** (jnp.arange(0, d, 2, dtype=F32) / d))
    inv_lane = inv_freq[jnp.arange(LANES) % (d // 2)][None, :]
    base = (jnp.arange(S // tm, dtype=F32) * tm)[:, None] * inv_lane
    offs = jnp.arange(tm, dtype=F32)[:, None] * inv_lane
    shape_b = (S // tm, 1, LANES)
    return jnp.cos(base).reshape(shape_b), jnp.sin(base).reshape(shape_b), jnp.cos(offs), jnp.sin(offs)


def kernel(x, c, w_ada, b_ada, w_in, lam_qk, attn_norm_g, hg_lb_logits, hg_norm_g, pool_w, pool_scale, w_out, ln1_g, ln1_b, ln2_g, ln2_b, ffn_w1, ffn_w3, ffn_w2, router_w, exp_w1, exp_w3, exp_w2):
    B, S, D = x.shape
    assert B == 1 and D == D_MODEL and S % MOE_BLOCK == 0
    xs = x.reshape(S, D)
    mod = _modulation(c, w_ada, b_ada)
    rope = _rope_tables(S)
    experts_bf = {}
    for l in range(DEPTH):
        sh1, sc1, g1, sh2, sc2, g2 = (mod[l, j] for j in range(6))
        qt, k, vt, hg, p = _in_projection(xs, sc1, sh1, w_in, l, rope, pool_w[l], pool_scale[l])
        a = _diff_attention(qt, k, vt, lam_qk[l], attn_norm_g[l], l)
        r = _hgrn2(hg, hg_lb_logits, hg_norm_g[l], l)
        xs = _out_projection(a, r, p, xs, w_out, l, g1, ln1_g[l], ln1_b[l])
        if l % 2 == 0:
            i = l // 2
            ride = (exp_w1[i], exp_w3[i], exp_w2[i]) if l + 1 < DEPTH else ()
            xs, cast = _dense_ffn(xs, sc2, sh2, g2, ffn_w1[i].astype(BF16), ffn_w3[i].astype(BF16),
                                  ffn_w2[i].astype(BF16), ln2_g[l], ln2_b[l], ride)
            if ride:
                experts_bf[i] = cast
        else:
            i = l // 2
            h_bf, rank, comb, cum = _router(xs, sc2, sh2, router_w[i])
            w1, w3, w2 = experts_bf.get(i) or [w[i].astype(BF16) for w in (exp_w1, exp_w3, exp_w2)]
            y = _moe_ffn(h_bf, rank, comb, cum, w1, w3, w2)
            xs = _residual_ln(xs, y, g2, ln2_g[l], ln2_b[l])
    return xs.reshape(B, S, D)
```

```python
import functools
import math

import jax
import jax.numpy as jnp
from jax import lax
from jax.experimental import pallas as pl
from jax.experimental.pallas import tpu as pltpu

F32 = jnp.float32
BF16 = jnp.bfloat16
HIGHEST = lax.Precision.HIGHEST

D_MODEL = 1024
DEPTH = 2
DA_HEADS = 4
DA_WIDTH = D_MODEL // 2
DA_HEAD_DIM = DA_WIDTH // (2 * DA_HEADS)
DA_VAL_DIM = 2 * DA_HEAD_DIM
HG_HEADS = 4
HG_WIDTH = D_MODEL // 4
HG_DIM = HG_WIDTH // HG_HEADS
POOL_WINDOWS = (2, 4, 8, 16)
POOL_WIDTH = D_MODEL // 4
POOL_GROUP = POOL_WIDTH // len(POOL_WINDOWS)
D_MIX = DA_WIDTH + HG_WIDTH + POOL_WIDTH
D_IN = 3 * DA_WIDTH + 4 * HG_WIDTH + POOL_WIDTH
ROPE_THETA = 10000.0
MASK_VALUE = -1e30
TINY = 1e-30
FFN_DIM = 11 * D_MODEL // 4
N_EXPERTS = 8
EXPERT_DIM = 7 * D_MODEL // 2
ALPHA = (2 * DEPTH) ** 0.25
EPS = 1e-5
LOG2_E = math.log2(math.e)

LANES = 128
BF16_ROWS = 16
MXU_DIM = 256
VMEM_LIMIT = 60 * 1024 * 1024

ROW_TILE = 512
FFN_CAST_TILE = 256
ATTN_BLOCK = 512
ATTN_Q_BLOCK = 2048
ATTN_QUERY_GROUP = 256
ATTN_UNROLL = 4
ATTN_SUM_ROWS = BF16_ROWS
ATTN_VT_ROWS = DA_VAL_DIM + ATTN_SUM_ROWS
HG_BLOCK = 512
HG_CHUNK = 16
HG_UNROLL = 4
POOL_HALO = 16
MOE_BLOCK = 2048
MOE_TILE = 192
MOE_TAIL_TILE = 128
MOE_CHUNK = 128
MOE_WINDOW = 8
MOE_FSPLIT = 2


def _params(sem, vmem=VMEM_LIMIT):
    return pltpu.CompilerParams(dimension_semantics=sem, vmem_limit_bytes=vmem)


def _layer_norm(y, g, b):
    mu = jnp.mean(y, axis=-1, keepdims=True)
    d = y - mu
    var = jnp.mean(d * d, axis=-1, keepdims=True)
    return d * lax.rsqrt(var + EPS) * g + b


def _silu(x):
    return x * jax.nn.sigmoid(x)


def _head_block_mask(n, group):
    r = lax.broadcasted_iota(jnp.int32, (n, n), 0) // group
    c = lax.broadcasted_iota(jnp.int32, (n, n), 1) // group
    return r == c


def _mod_kernel(c_ref, w_ref, b_ref, o_ref):
    cond = _silu(c_ref[...])
    cond8 = jnp.broadcast_to(cond, (8, D_MODEL))
    r = jnp.dot(cond8, w_ref[0], preferred_element_type=F32, precision=HIGHEST)
    o_ref[0] = r[0:1] + b_ref[0]


def _modulation(c, w_ada, b_ada):
    n = 6 * DEPTH
    b3 = b_ada.reshape(n, 1, D_MODEL)
    out = pl.pallas_call(
        _mod_kernel,
        grid=(DEPTH, 6),
        in_specs=[
            pl.BlockSpec((1, D_MODEL), lambda l, j: (0, 0)),
            pl.BlockSpec((1, D_MODEL, D_MODEL), lambda l, j: (l, 0, j)),
            pl.BlockSpec((1, 1, D_MODEL), lambda l, j: (l * 6 + j, 0, 0)),
        ],
        out_specs=pl.BlockSpec((1, 1, D_MODEL), lambda l, j: (l * 6 + j, 0, 0)),
        out_shape=jax.ShapeDtypeStruct((n, 1, D_MODEL), F32),
        compiler_params=_params(("parallel", "parallel")),
        name="adaln_mod",
    )(c, w_ada, b3)
    return out.reshape(DEPTH, 6, 1, D_MODEL)


def _inproj_kernel(x_ref, sc_ref, sh_ref, wf_ref, cosb_ref, sinb_ref, cosr_ref, sinr_ref, pw_ref, ps_ref,
                   qt_ref, k_ref, vt_ref, hg_ref, p_ref, w_ref, wvt_ref, halo_scr):
    @pl.when(pl.program_id(0) == 0)
    def _():
        w_ref[...] = wf_ref[0].astype(BF16)
        wvt_ref[...] = wf_ref[0, :, 2 * DA_WIDTH:3 * DA_WIDTH].T.astype(BF16)

    h = (x_ref[...] * (1.0 + sc_ref[...]) + sh_ref[...]).astype(BF16)
    qk = jnp.dot(h, w_ref[:, : 2 * DA_WIDTH], preferred_element_type=F32)
    cb, sb, cr, sr = cosb_ref[0], sinb_ref[0], cosr_ref[...], sinr_ref[...]
    cos128 = cb * cr - sb * sr
    sin128 = sb * cr + cb * sr
    half = DA_HEAD_DIM // 2
    lane128 = lax.broadcasted_iota(jnp.int32, cos128.shape, 1)
    sin128 = jnp.where((lane128 & half) == 0, -sin128, sin128)
    reps = 2 * DA_WIDTH // LANES
    cos = jnp.tile(cos128, (1, reps))
    sin = jnp.tile(sin128, (1, reps))
    lane = lax.broadcasted_iota(jnp.int32, qk.shape, 1)
    first_half = (lane & half) == 0
    width = 2 * DA_WIDTH
    partner = jnp.where(first_half, pltpu.roll(qk, width - half, 1), pltpu.roll(qk, half, 1))
    rot = qk * cos + partner * sin
    qt_ref[0] = (rot[:, :DA_WIDTH] * (DA_HEAD_DIM ** -0.5 * LOG2_E)).T.astype(BF16)
    k_ref[...] = rot[:, DA_WIDTH:].astype(BF16)
    o = 2 * DA_WIDTH
    vt = lax.dot_general(wvt_ref[...], h, (((1,), (1,)), ((), ())),
                         preferred_element_type=F32).astype(BF16)
    ones = jnp.ones((ATTN_SUM_ROWS, vt.shape[1]), BF16)
    pieces = []
    for hd in range(DA_HEADS):
        pieces += [vt[hd * DA_VAL_DIM:(hd + 1) * DA_VAL_DIM], ones]
    vt_ref[0] = jnp.concatenate(pieces, axis=0)
    o += DA_WIDTH
    hg_ref[...] = jnp.dot(h, w_ref[:, o:o + 4 * HG_WIDTH], preferred_element_type=F32)
    o += 4 * HG_WIDTH
    pu = jnp.dot(h, w_ref[:, o:o + POOL_WIDTH], preferred_element_type=F32)
    p_ref[...] = _pool_tile(pu, pw_ref[...], ps_ref[...], halo_scr, pl.program_id(0))


def _in_projection(x, sc, sh, w_in, layer, rope, pool_w, pool_scale):
    S = x.shape[0]
    tm = ATTN_BLOCK
    row = lambda i: (i, 0)
    fixed = lambda i: (0, 0)
    cos_b, sin_b, cos_r, sin_r = rope
    w_bd = jax.scipy.linalg.block_diag(*[pool_w[g] for g in range(len(POOL_WINDOWS))]).astype(BF16)
    return pl.pallas_call(
        _inproj_kernel,
        grid=(S // tm,),
        in_specs=[
            pl.BlockSpec((tm, D_MODEL), row),
            pl.BlockSpec((1, D_MODEL), fixed),
            pl.BlockSpec((1, D_MODEL), fixed),
            pl.BlockSpec((1, D_MODEL, D_IN), lambda i: (layer, 0, 0), pipeline_mode=pl.Buffered(1)),
            pl.BlockSpec((1, 1, LANES), lambda i: (i, 0, 0)),
            pl.BlockSpec((1, 1, LANES), lambda i: (i, 0, 0)),
            pl.BlockSpec((tm, LANES), fixed),
            pl.BlockSpec((tm, LANES), fixed),
            pl.BlockSpec((POOL_WIDTH, POOL_WIDTH), fixed),
            pl.BlockSpec((1, POOL_WIDTH), fixed),
        ],
        out_specs=[
            pl.BlockSpec((1, DA_WIDTH, tm), lambda i: (i, 0, 0)),
            pl.BlockSpec((tm, DA_WIDTH), row),
            pl.BlockSpec((1, DA_HEADS * ATTN_VT_ROWS, tm), lambda i: (i, 0, 0)),
            pl.BlockSpec((tm, 4 * HG_WIDTH), row),
            pl.BlockSpec((tm, POOL_WIDTH), row),
        ],
        out_shape=[
            jax.ShapeDtypeStruct((S // tm, DA_WIDTH, tm), BF16),
            jax.ShapeDtypeStruct((S, DA_WIDTH), BF16),
            jax.ShapeDtypeStruct((S // tm, DA_HEADS * ATTN_VT_ROWS, tm), BF16),
            jax.ShapeDtypeStruct((S, 4 * HG_WIDTH), F32),
            jax.ShapeDtypeStruct((S, POOL_WIDTH), BF16),
        ],
        scratch_shapes=[
            pltpu.VMEM((D_MODEL, D_IN), BF16),
            pltpu.VMEM((DA_WIDTH, D_MODEL), BF16),
            pltpu.VMEM((POOL_HALO, POOL_WIDTH), F32),
        ],
        compiler_params=_params(("arbitrary",)),
        name="in_proj",
    )(x, sc, sh, w_in, cos_b, sin_b, cos_r, sin_r, w_bd, pool_scale.reshape(1, POOL_WIDTH))


def _attn_kernel(lam_ref, g_ref, qt_ref, k_ref, vt_ref, o_ref, m_scr, a_scr, c_scr, acc_scr, s_scr,
                 p_scr, *, lam_init):
    bq = qt_ref.shape[0] * qt_ref.shape[2]
    bk = vt_ref.shape[2]
    cg = ATTN_QUERY_GROUP
    i = pl.program_id(1)
    qt = jnp.concatenate([qt_ref[n] for n in range(qt_ref.shape[0])], axis=1)
    rowq = lax.broadcasted_iota(jnp.int32, qt.shape, 0)
    zero = jnp.zeros_like(qt)
    maps = (jnp.where(rowq < DA_HEAD_DIM, qt, zero), jnp.where(rowq >= DA_HEAD_DIM, qt, zero))
    groups = [(a * bq + c0, c0, maps[a][:, c0:c0 + cg]) for a in range(2) for c0 in range(0, bq, cg)]
    m_scr[...] = jnp.full(m_scr.shape, -jnp.inf, F32)
    acc_scr[...] = jnp.zeros(acc_scr.shape, F32)
    hk = bk // 2
    key_i = lax.broadcasted_iota(jnp.int32, (hk, cg), 0)
    qry_i = lax.broadcasted_iota(jnp.int32, (hk, cg), 1)
    ng = len(groups)
    last = ng - 1

    def cols_of(g):
        return slice(groups[g][0], groups[g][0] + cg)

    def ccols_of(sset, g):
        return slice(sset * 2 * bq + groups[g][0], sset * 2 * bq + groups[g][0] + cg)

    def mask_mode(key_offset, g):
        c0 = groups[g][1]
        if key_offset + hk - 1 <= c0:
            return None
        if key_offset > c0 + cg - 1:
            return "skip"
        return key_offset

    def score_stage(hb, sset, g, key_offset):
        kb = k_ref[pl.ds(pl.multiple_of(hb * hk, hk), hk), :]
        s = jnp.dot(kb, groups[g][2], preferred_element_type=F32)
        if key_offset is not None:
            s = jnp.where(key_i + key_offset <= qry_i + groups[g][1], s, MASK_VALUE)
        s_scr[sset * ng + g] = s
        c_scr[:, ccols_of(sset, g)] = jnp.max(s, axis=0, keepdims=True)

    def max_stage(sset, g):
        cols = cols_of(g)
        m_prev = m_scr[:, cols]
        m_new = jnp.maximum(m_prev, c_scr[:, ccols_of(sset, g)])
        a_scr[:, cols] = jnp.exp2(m_prev - m_new)
        m_scr[:, cols] = m_new

    def exp_stage(sset, g, vt_half, carried, defer):
        cols = cols_of(g)
        p = jnp.exp2(s_scr[sset * ng + g] - m_scr[:, cols])
        alpha = a_scr[:, cols]
        pb = p.astype(BF16)
        acc = acc_scr[:, cols]
        if carried is not None:
            acc = acc + carried
        if defer:
            acc_scr[:, cols] = alpha * acc
            p_scr[...] = pb
        else:
            acc_scr[:, cols] = alpha * acc + jnp.dot(vt_half, pb, preferred_element_type=F32)

    def items(key_offset):
        if key_offset is None:
            return [(g, None) for g in range(ng)]
        return [(g, mask_mode(key_offset, g)) for g in range(ng) if mask_mode(key_offset, g) != "skip"]

    def iteration(t, offs, look):
        vt_a = vt_ref[t, :, 0:hk]
        vt_b = vt_ref[t, :, hk:bk]
        carried = jnp.dot(vt_ref[jnp.maximum(t - 1, 0), :, hk:bk], p_scr[...],
                          preferred_element_type=F32)
        a_items = items(offs)
        b_items = items(None if offs is None else offs + hk)
        n_items = [] if look == "stop" else items(look)
        score_stage(2 * t + 1, 1, *b_items[0])
        for n, (g, _) in enumerate(a_items):
            if n + 1 < len(b_items):
                score_stage(2 * t + 1, 1, *b_items[n + 1])
            if n + 1 < len(a_items):
                max_stage(0, a_items[n + 1][0])
            else:
                max_stage(1, b_items[0][0])
            exp_stage(0, g, vt_a, carried if g == last else None, False)
        if n_items:
            score_stage(2 * t + 2, 0, *n_items[0])
        for n, (g, _) in enumerate(b_items):
            if n + 1 < len(n_items):
                score_stage(2 * t + 2, 0, *n_items[n + 1])
            if n + 1 < len(b_items):
                max_stage(1, b_items[n + 1][0])
            elif n_items:
                max_stage(0, n_items[0][0])
            exp_stage(1, g, vt_b, None, bool(n_items) and g == last)

    p_scr[...] = jnp.zeros(p_scr.shape, BF16)
    for g in range(ng):
        score_stage(0, 0, g, (0 - i) * bq)
    max_stage(0, 0)

    per_q = bq // bk
    first = i * per_q
    unroll = ATTN_UNROLL

    def body(u, carry):
        for k in range(unroll):
            iteration(unroll * u + k, None, None)
        return carry

    n_plain = jnp.maximum(first - 1, 0)
    trips = n_plain // unroll
    lax.fori_loop(0, trips, body, 0)
    for k in range(unroll - 1):
        @pl.when(trips * unroll + k < n_plain)
        def _():
            iteration(trips * unroll + k, None, None)

    @pl.when(i > 0)
    def _():
        iteration(first - 1, None, 0)

    for d in range(per_q):
        iteration(first + d, d * bk, (d + 1) * bk if d + 1 < per_q else "stop")

    lq = lam_ref[...]
    lam = (jnp.exp(jnp.sum(lq[0:1] * lq[1:2], axis=1, keepdims=True))
           - jnp.exp(jnp.sum(lq[2:3] * lq[3:4], axis=1, keepdims=True)) + lam_init)
    o = acc_scr[0:DA_VAL_DIM, :] / acc_scr[DA_VAL_DIM:DA_VAL_DIM + 1, :]
    o = o[:, :bq] - lam * o[:, bq:]
    ms = jnp.mean(o * o, axis=0, keepdims=True)
    o = o * lax.rsqrt(ms + EPS) * g_ref[...] * (1.0 - lam_init)
    o_ref[...] = o.T.astype(BF16)


def _diff_attention(qt, k, vt, lam_qk, attn_g, layer):
    S = k.shape[0]
    bk = ATTN_BLOCK
    bq = ATTN_Q_BLOCK
    nk = S // bk
    lam_init = 0.8 - 0.6 * math.exp(-0.3 * layer)
    return pl.pallas_call(
        functools.partial(_attn_kernel, lam_init=lam_init),
        grid=(DA_HEADS, S // bq),
        in_specs=[
            pl.BlockSpec((4, DA_HEAD_DIM), lambda h, i: (0, 0)),
            pl.BlockSpec((DA_VAL_DIM, 1), lambda h, i: (0, 0)),
            pl.BlockSpec((bq // bk, LANES, bk), lambda h, i: (i, h, 0)),
            pl.BlockSpec((S, LANES), lambda h, i: (0, h)),
            pl.BlockSpec((nk, ATTN_VT_ROWS, bk), lambda h, i: (0, h, 0)),
        ],
        out_specs=pl.BlockSpec((bq, DA_VAL_DIM), lambda h, i: (i, h)),
        out_shape=jax.ShapeDtypeStruct((S, DA_WIDTH), BF16),
        scratch_shapes=[
            pltpu.VMEM((1, 2 * bq), F32),
            pltpu.VMEM((1, 2 * bq), F32),
            pltpu.VMEM((1, 2 * 2 * bq), F32),
            pltpu.VMEM((ATTN_VT_ROWS, 2 * bq), F32),
            pltpu.VMEM((2 * (2 * bq // ATTN_QUERY_GROUP), bk // 2, ATTN_QUERY_GROUP), F32),
            pltpu.VMEM((bk // 2, ATTN_QUERY_GROUP), BF16),
        ],
        compiler_params=_params(("parallel", "parallel")),
        name="diff_attn",
    )(lam_qk, attn_g.reshape(DA_VAL_DIM, 1), qt, k, vt)


def _hgrn_kernel(lbl_ref, g_ref, hg_ref, o_ref, st_scr, stb_scr, o_scr, *, layer):
    T = hg_ref.shape[0]
    c = HG_CHUNK
    W = HG_WIDTH

    @pl.when(pl.program_id(0) == 0)
    def _():
        st_scr[...] = jnp.zeros(st_scr.shape, F32)
        stb_scr[...] = jnp.zeros(stb_scr.shape, BF16)

    lg = lbl_ref[...]
    ex = jnp.exp(lg - jnp.max(lg, axis=0, keepdims=True))
    prob = ex / jnp.sum(ex, axis=0, keepdims=True)
    lb = jnp.zeros((1, W), F32)
    for li in range(1, layer + 1):
        lb = lb + prob[li:li + 1]

    same_head = _head_block_mask(W, HG_DIM)
    ones_bd = same_head.astype(BF16)
    rowi = lax.broadcasted_iota(jnp.int32, (c, W), 0)

    def body(s, carry):
        sl = pl.ds(pl.multiple_of(s * c, c), c)
        qf = hg_ref[sl, 0:W]
        z = hg_ref[sl, W:2 * W]
        iv = hg_ref[sl, 2 * W:3 * W]
        f = lb + (1.0 - lb) * jax.nn.sigmoid(z)
        logf = jnp.log(jnp.maximum(f, TINY))
        key = (1.0 - lb) * jax.nn.sigmoid(-z)
        val = _silu(iv)
        b = logf
        sft = 1
        while sft < c:
            b = b + jnp.where(rowi >= sft, pltpu.roll(b, sft, 0), 0.0)
            sft *= 2
        b_last = b[c - 1:c]
        kd = (key * jnp.exp(b_last - b)).astype(BF16)
        upd = lax.dot_general(val.astype(BF16), kd, (((0,), (0,)), ((), ())),
                              preferred_element_type=F32)
        es = [(qf * key).astype(BF16)]
        vrs = [val]
        for d in range(1, c):
            kr = pltpu.roll(key, d, 0)
            br = pltpu.roll(b, d, 0)
            e = jnp.where(rowi >= d, qf * kr * jnp.exp(b - br), 0.0)
            es.append(e.astype(BF16))
            vrs.append(pltpu.roll(val, d, 0))
        estack = jnp.concatenate(es, axis=0)
        r = jnp.dot(estack, ones_bd, preferred_element_type=F32)
        o = r[0:c] * vrs[0]
        for d in range(1, c):
            o = o + r[d * c:(d + 1) * c] * vrs[d]
        qd = (qf * jnp.exp(b)).astype(BF16)
        o = o + lax.dot_general(qd, stb_scr[...], (((1,), (1,)), ((), ())),
                                preferred_element_type=F32)
        o_scr[sl, :] = o
        decay = jnp.exp(b_last)
        for hd in range(HG_HEADS):
            rws = slice(hd * HG_DIM, (hd + 1) * HG_DIM)
            lns = slice(hd * HG_DIM // LANES * LANES, (hd * HG_DIM // LANES + 1) * LANES)
            new = decay[:, lns] * st_scr[rws, lns] + jnp.where(same_head[rws, lns], upd[rws, lns], 0.0)
            st_scr[rws, lns] = new
            stb_scr[rws, lns] = new.astype(BF16)
        return carry

    lax.fori_loop(0, T // c, body, 0, unroll=HG_UNROLL)

    o = o_scr[...]
    ms = jnp.dot(o * o, same_head.astype(F32), preferred_element_type=F32,
                 precision=HIGHEST) * (1.0 / HG_DIM)
    gate = hg_ref[:, 3 * W:4 * W]
    o_ref[...] = (o * lax.rsqrt(ms + EPS) * g_ref[...] * _silu(gate)).astype(BF16)


def _hgrn2(hg, lb_logits, norm_g, layer):
    S = hg.shape[0]
    T = HG_BLOCK
    g_t = jnp.tile(norm_g.reshape(1, HG_DIM), (1, HG_HEADS))
    return pl.pallas_call(
        functools.partial(_hgrn_kernel, layer=layer),
        grid=(S // T,),
        in_specs=[
            pl.BlockSpec((DEPTH, HG_WIDTH), lambda i: (0, 0)),
            pl.BlockSpec((1, HG_WIDTH), lambda i: (0, 0)),
            pl.BlockSpec((T, 4 * HG_WIDTH), lambda i: (i, 0)),
        ],
        out_specs=pl.BlockSpec((T, HG_WIDTH), lambda i: (i, 0)),
        out_shape=jax.ShapeDtypeStruct((S, HG_WIDTH), BF16),
        scratch_shapes=[
            pltpu.VMEM((HG_WIDTH, HG_WIDTH), F32),
            pltpu.VMEM((HG_WIDTH, HG_WIDTH), BF16),
            pltpu.VMEM((T, HG_WIDTH), F32),
        ],
        compiler_params=_params(("arbitrary",)),
        name="hgrn2",
    )(lb_logits, g_t, hg)


def _pool_tile(u, w_bd, scale, halo_scr, blk):
    T = u.shape[0]
    H = POOL_HALO

    @pl.when(blk == 0)
    def _():
        halo_scr[...] = jnp.zeros(halo_scr.shape, F32)

    ext = jnp.concatenate([halo_scr[...], u], axis=0)
    halo_scr[...] = u[T - H:, :]
    sums = []
    s = ext
    w = 1
    while w < POOL_WINDOWS[-1]:
        s = s + pltpu.roll(s, w, 0)
        w *= 2
        sums.append(s[H:, :])
    t1 = (blk * T + 1 + lax.broadcasted_iota(jnp.int32, (T, POOL_WIDTH), 0)).astype(F32)
    lane = lax.broadcasted_iota(jnp.int32, (T, POOL_WIDTH), 1)
    pooled = None
    for gi, win in enumerate(POOL_WINDOWS):
        mean = sums[gi] / jnp.minimum(t1, float(win))
        pooled = mean if pooled is None else jnp.where(lane >= gi * POOL_GROUP, mean, pooled)
    pooled = pooled - u
    y = jnp.dot(pooled.astype(BF16), w_bd, preferred_element_type=F32)
    return (y * scale).astype(BF16)


def _outproj_kernel(a_ref, r_ref, p_ref, x_ref, wf_ref, g1_ref, lg_ref, lb_ref, o_ref, w_ref):
    @pl.when(pl.program_id(0) == 0)
    def _():
        w_ref[...] = wf_ref[0].astype(BF16)

    m = jnp.dot(a_ref[...], w_ref[0:DA_WIDTH, :], preferred_element_type=F32)
    m += jnp.dot(r_ref[...], w_ref[DA_WIDTH:DA_WIDTH + HG_WIDTH, :], preferred_element_type=F32)
    m += jnp.dot(p_ref[...], w_ref[DA_WIDTH + HG_WIDTH:, :], preferred_element_type=F32)
    y = ALPHA * x_ref[...] + (1.0 + g1_ref[...]) * m
    o_ref[...] = _layer_norm(y, lg_ref[...], lb_ref[...])


def _out_projection(a, r, p, x, w_out, layer, g1, ln_g, ln_b):
    S = x.shape[0]
    tm = ROW_TILE
    row = lambda i: (i, 0)
    fixed = lambda i: (0, 0)
    vec = pl.BlockSpec((1, D_MODEL), fixed)
    return pl.pallas_call(
        _outproj_kernel,
        grid=(S // tm,),
        in_specs=[
            pl.BlockSpec((tm, DA_WIDTH), row),
            pl.BlockSpec((tm, HG_WIDTH), row),
            pl.BlockSpec((tm, POOL_WIDTH), row),
            pl.BlockSpec((tm, D_MODEL), row),
            pl.BlockSpec((1, D_MIX, D_MODEL), lambda i: (layer, 0, 0), pipeline_mode=pl.Buffered(1)),
            vec, vec, vec,
        ],
        out_specs=pl.BlockSpec((tm, D_MODEL), row),
        out_shape=jax.ShapeDtypeStruct((S, D_MODEL), F32),
        scratch_shapes=[pltpu.VMEM((D_MIX, D_MODEL), BF16)],
        compiler_params=_params(("arbitrary",)),
        name="out_proj_ln",
    )(a, r, p, x, w_out, g1, ln_g.reshape(1, D_MODEL), ln_b.reshape(1, D_MODEL))


def _ffn_kernel(x_ref, sc_ref, sh_ref, g2_ref, w1_ref, w3_ref, w2_ref, lg_ref, lb_ref, *rest):
    n_cast = (len(rest) - 1) // 2
    o_ref = rest[n_cast]
    for src, dst in zip(rest[:n_cast], rest[n_cast + 1:]):
        dst[...] = src[...].astype(BF16)
    x = x_ref[...]
    h = (x * (1.0 + sc_ref[...]) + sh_ref[...]).astype(BF16)
    acc = jnp.zeros(x.shape, F32)
    step = 4 * MXU_DIM
    for lo in range(0, FFN_DIM, step):
        hi = min(lo + step, FFN_DIM)
        a = jnp.dot(h, w1_ref[:, lo:hi], preferred_element_type=F32)
        b = jnp.dot(h, w3_ref[:, lo:hi], preferred_element_type=F32)
        acc += jnp.dot((_silu(a) * b).astype(BF16), w2_ref[lo:hi, :], preferred_element_type=F32)
    y = ALPHA * x + (1.0 + g2_ref[...]) * acc
    o_ref[...] = _layer_norm(y, lg_ref[...], lb_ref[...])


def _dense_ffn(x, sc, sh, g2, w1, w3, w2, ln_g, ln_b, to_cast=()):
    S = x.shape[0]
    tm = FFN_CAST_TILE if to_cast else ROW_TILE
    steps = S // tm
    row = lambda i: (i, 0)
    fixed = lambda i: (0, 0)
    vec = pl.BlockSpec((1, D_MODEL), fixed)
    once = pl.Buffered(1)
    slabs = [a.reshape(-1, a.shape[-1]) for a in to_cast]
    slab_specs = [pl.BlockSpec((a.shape[0] // steps, a.shape[1]), row) for a in slabs]
    for a in slabs:
        assert a.shape[0] % (steps * BF16_ROWS) == 0
    outs = pl.pallas_call(
        _ffn_kernel,
        grid=(steps,),
        in_specs=[
            pl.BlockSpec((tm, D_MODEL), row),
            vec, vec, vec,
            pl.BlockSpec((D_MODEL, FFN_DIM), fixed, pipeline_mode=once),
            pl.BlockSpec((D_MODEL, FFN_DIM), fixed, pipeline_mode=once),
            pl.BlockSpec((FFN_DIM, D_MODEL), fixed, pipeline_mode=once),
            vec, vec,
        ] + slab_specs,
        out_specs=[pl.BlockSpec((tm, D_MODEL), row)] + slab_specs,
        out_shape=[jax.ShapeDtypeStruct((S, D_MODEL), F32)]
        + [jax.ShapeDtypeStruct(a.shape, BF16) for a in slabs],
        compiler_params=_params(("parallel",)),
        name="dense_ffn_ln",
    )(x, sc, sh, g2, w1, w3, w2, ln_g.reshape(1, D_MODEL), ln_b.reshape(1, D_MODEL), *slabs)
    return outs[0], [o.reshape(a.shape) for o, a in zip(outs[1:], to_cast)]


def _router_kernel(x_ref, sc_ref, sh_ref, rw_ref, h_ref, rank_ref, comb_ref, cum_ref):
    T = x_ref.shape[0]
    tc = MOE_CHUNK
    E = N_EXPERTS
    h = x_ref[...] * (1.0 + sc_ref[...]) + sh_ref[...]
    h_ref[...] = h.astype(BF16)
    logits = lax.dot_general(rw_ref[...], h, (((1,), (1,)), ((), ())),
                             preferred_element_type=F32, precision=HIGHEST)
    eidx = lax.broadcasted_iota(jnp.int32, (E, T), 0)
    v1 = jnp.max(logits, axis=0, keepdims=True)
    i1 = jnp.min(jnp.where(logits == v1, eidx, E), axis=0, keepdims=True)
    m1 = eidx == i1
    rest = jnp.where(m1, -jnp.inf, logits)
    v2 = jnp.max(rest, axis=0, keepdims=True)
    i2 = jnp.min(jnp.where(rest == v2, eidx, E), axis=0, keepdims=True)
    m2 = eidx == i2
    e2 = jnp.exp(v2 - v1)
    g1 = 1.0 / (1.0 + e2)
    g2 = e2 / (1.0 + e2)
    comb = jnp.where(m1, g1, 0.0) + jnp.where(m2, g2, 0.0)
    routed = jnp.logical_or(m1, m2)
    ind = routed.astype(BF16)
    r_i = lax.broadcasted_iota(jnp.int32, (tc, tc), 0)
    c_i = lax.broadcasted_iota(jnp.int32, (tc, tc), 1)
    strict = (r_i < c_i).astype(BF16)
    lane = lax.broadcasted_iota(jnp.int32, (E, LANES), 1)
    running = jnp.zeros((E, 1), F32)
    cum = jnp.zeros((E, LANES), F32)
    for ci in range(T // tc):
        sl = slice(ci * tc, (ci + 1) * tc)
        ind_c = ind[:, sl]
        rank_c = jnp.dot(ind_c, strict, preferred_element_type=F32) + running
        rank_ref[:, ci] = jnp.where(routed[:, sl], rank_c, -1.0).reshape(E, 1, tc)
        comb_ref[:, ci] = comb[:, sl].reshape(E, 1, tc)
        cum = jnp.where(lane == ci, running, cum)
        running = running + jnp.sum(ind_c.astype(F32), axis=1, keepdims=True)
    cum = jnp.where(lane == T // tc, running, cum)
    cum_ref[0] = cum.astype(jnp.int32)


def _router(x, sc, sh, router_w):
    S = x.shape[0]
    T = MOE_BLOCK
    E = N_EXPERTS
    fixed = lambda i: (0, 0)
    vec = pl.BlockSpec((1, D_MODEL), fixed)
    return pl.pallas_call(
        _router_kernel,
        grid=(S // T,),
        in_specs=[
            pl.BlockSpec((T, D_MODEL), lambda i: (i, 0)),
            vec, vec,
            pl.BlockSpec((E, D_MODEL), fixed),
        ],
        out_specs=[
            pl.BlockSpec((T, D_MODEL), lambda i: (i, 0)),
            pl.BlockSpec((E, T // MOE_CHUNK, 1, MOE_CHUNK), lambda i: (0, i, 0, 0)),
            pl.BlockSpec((E, T // MOE_CHUNK, 1, MOE_CHUNK), lambda i: (0, i, 0, 0)),
            pl.BlockSpec((1, E, LANES), lambda i: (i, 0, 0)),
        ],
        out_shape=[
            jax.ShapeDtypeStruct((S, D_MODEL), BF16),
            jax.ShapeDtypeStruct((E, S // MOE_CHUNK, 1, MOE_CHUNK), F32),
            jax.ShapeDtypeStruct((E, S // MOE_CHUNK, 1, MOE_CHUNK), F32),
            jax.ShapeDtypeStruct((S // T, E, LANES), jnp.int32),
        ],
        compiler_params=_params(("parallel",)),
        name="moe_router",
    )(x, sc, sh, router_w.T)


def _moe_kernel(cum_ref, h_ref, rank_ref, comb_ref, w1_ref, w3_ref, w2_ref, y_ref, xe_scr, gate_scr,
                out_scr):
    T = h_ref.shape[0]
    tr, tc, win = MOE_TILE, MOE_CHUNK, MOE_WINDOW
    nchunk = T // tc
    b = pl.program_id(0)
    e = pl.program_id(1)
    f = pl.program_id(2)

    @pl.when(jnp.logical_and(e == 0, f == 0))
    def _():
        y_ref[...] = jnp.zeros(y_ref.shape, F32)

    base = (b * N_EXPERTS + e) * LANES
    count = cum_ref[base + nchunk]

    def token_row(ref, first, n):
        blk = ref[0, pl.ds(first, n)]
        return jnp.concatenate([blk[c] for c in range(n)], axis=1)

    def process(lo, tr):
        rows = pl.ds(pl.multiple_of(lo, BF16_ROWS), tr)

        def hits(first, n):
            want = (lo + lax.broadcasted_iota(jnp.int32, (tr, n * tc), 0)).astype(F32)
            return token_row(rank_ref, first, n) == want

        def token_span(first, n):
            start = first * tc
            if not isinstance(start, int):
                start = pl.multiple_of(start, tc)
            return pl.ds(start, n * tc)

        c_lo = jnp.int32(0)
        c_hi = jnp.int32(-1)
        for c in range(nchunk):
            c_lo += (cum_ref[base + c + 1] <= lo).astype(jnp.int32)
            c_hi += (cum_ref[base + c] < lo + tr).astype(jnp.int32)
        fits = c_hi - c_lo < win
        window = (jnp.minimum(c_lo, nchunk - win), win)
        block = (0, nchunk)

        def gather(first, n):
            hit = hits(first, n)
            xe_scr[rows, :] = jnp.dot(hit.astype(BF16), h_ref[token_span(first, n), :],
                                      preferred_element_type=F32).astype(BF16)
            gate_scr[rows, :] = jnp.sum(jnp.where(hit, token_row(comb_ref, first, n), 0.0),
                                        axis=1, keepdims=True)

        @pl.when(jnp.logical_and(f == 0, fits))
        def _():
            gather(*window)

        @pl.when(jnp.logical_and(f == 0, jnp.logical_not(fits)))
        def _():
            gather(*block)

        xe = xe_scr[rows, :]
        a = jnp.dot(xe, w1_ref[0], preferred_element_type=F32)
        g = jnp.dot(xe, w3_ref[0], preferred_element_type=F32)
        mid = (_silu(a) * g).astype(BF16)
        part = jnp.dot(mid, w2_ref[0], preferred_element_type=F32)

        @pl.when(f == 0)
        def _():
            out_scr[rows, :] = part

        if MOE_FSPLIT > 2:
            @pl.when(jnp.logical_and(f > 0, f < MOE_FSPLIT - 1))
            def _():
                out_scr[rows, :] += part

        @pl.when(f == MOE_FSPLIT - 1)
        def _():
            out_bf = ((out_scr[rows, :] + part) * gate_scr[rows, :]).astype(BF16)

            def scatter(first, n):
                y_ref[token_span(first, n), :] += lax.dot_general(
                    hits(first, n).astype(BF16), out_bf, (((0,), (0,)), ((), ())),
                    preferred_element_type=F32)

            @pl.when(fits)
            def _():
                scatter(*window)

            @pl.when(jnp.logical_not(fits))
            def _():
                scatter(*block)

    full = count // tr
    rem = count - full * tr

    def tile(r, carry):
        process(r * tr, tr)
        return carry

    lax.fori_loop(0, full, tile, 0)

    @pl.when(jnp.logical_and(rem > 0, rem <= MOE_TAIL_TILE))
    def _():
        process(full * tr, MOE_TAIL_TILE)

    @pl.when(rem > MOE_TAIL_TILE)
    def _():
        process(full * tr, tr)


def _moe_ffn(h_bf, rank, comb, cum, w1, w3, w2):
    S = h_bf.shape[0]
    T = MOE_BLOCK
    fw = EXPERT_DIM // MOE_FSPLIT
    grid_spec = pltpu.PrefetchScalarGridSpec(
        num_scalar_prefetch=1,
        grid=(S // T, N_EXPERTS, MOE_FSPLIT),
        in_specs=[
            pl.BlockSpec((T, D_MODEL), lambda b, e, f, c: (b, 0), pipeline_mode=pl.Buffered(1)),
            pl.BlockSpec((1, T // MOE_CHUNK, 1, MOE_CHUNK), lambda b, e, f, c: (e, b, 0, 0)),
            pl.BlockSpec((1, T // MOE_CHUNK, 1, MOE_CHUNK), lambda b, e, f, c: (e, b, 0, 0)),
            pl.BlockSpec((1, D_MODEL, fw), lambda b, e, f, c: (e, 0, f)),
            pl.BlockSpec((1, D_MODEL, fw), lambda b, e, f, c: (e, 0, f)),
            pl.BlockSpec((1, fw, D_MODEL), lambda b, e, f, c: (e, f, 0)),
        ],
        out_specs=pl.BlockSpec((T, D_MODEL), lambda b, e, f, c: (b, 0)),
        scratch_shapes=[
            pltpu.VMEM((T + MOE_TILE, D_MODEL), BF16),
            pltpu.VMEM((T + MOE_TILE, 1), F32),
            pltpu.VMEM((T + MOE_TILE, D_MODEL), F32),
        ],
    )
    return pl.pallas_call(
        _moe_kernel,
        grid_spec=grid_spec,
        out_shape=jax.ShapeDtypeStruct((S, D_MODEL), F32),
        compiler_params=_params(("arbitrary", "arbitrary", "arbitrary")),
        name="moe_experts",
    )(cum.reshape(-1), h_bf, rank, comb, w1, w3, w2)


def _ln_kernel(x_ref, y_ref, g2_ref, lg_ref, lb_ref, o_ref):
    y = ALPHA * x_ref[...] + (1.0 + g2_ref[...]) * y_ref[...]
    o_ref[...] = _layer_norm(y, lg_ref[...], lb_ref[...])


def _residual_ln(x, y, g2, ln_g, ln_b):
    S = x.shape[0]
    tm = ROW_TILE
    row = lambda i: (i, 0)
    vec = pl.BlockSpec((1, D_MODEL), lambda i: (0, 0))
    return pl.pallas_call(
        _ln_kernel,
        grid=(S // tm,),
        in_specs=[pl.BlockSpec((tm, D_MODEL), row), pl.BlockSpec((tm, D_MODEL), row), vec, vec, vec],
        out_specs=pl.BlockSpec((tm, D_MODEL), row),
        out_shape=jax.ShapeDtypeStruct((S, D_MODEL), F32),
        compiler_params=_params(("parallel",)),
        name="residual_ln",
    )(x, y, g2, ln_g.reshape(1, D_MODEL), ln_b.reshape(1, D_MODEL))


def _rope_tables(S):
    d = DA_HEAD_DIM
    tm = ATTN_BLOCK
    inv_freq = 1.0 / (ROPE_THETA ** (jnp.arange(0, d, 2, dtype=F32) / d))
    inv_lane = inv_freq[jnp.arange(LANES) % (d // 2)][None, :]
    base = (jnp.arange(S // tm, dtype=F32) * tm)[:, None] * inv_lane
    offs = jnp.arange(tm, dtype=F32)[:, None] * inv_lane
    shape_b = (S // tm, 1, LANES)
    return jnp.cos(base).reshape(shape_b), jnp.sin(base).reshape(shape_b), jnp.cos(offs), jnp.sin(offs)


def kernel(x, c, w_ada, b_ada, w_in, lam_qk, attn_norm_g, hg_lb_logits, hg_norm_g, pool_w, pool_scale, w_out, ln1_g, ln1_b, ln2_g, ln2_b, ffn_w1, ffn_w3, ffn_w2, router_w, exp_w1, exp_w3, exp_w2):
    B, S, D = x.shape
    assert B == 1 and D == D_MODEL and S % MOE_BLOCK == 0
    xs = x.reshape(S, D)
    mod = _modulation(c, w_ada, b_ada)
    rope = _rope_tables(S)
    experts_bf = {}
    for l in range(DEPTH):
        sh1, sc1, g1, sh2, sc2, g2 = (mod[l, j] for j in range(6))
        qt, k, vt, hg, p = _in_projection(xs, sc1, sh1, w_in, l, rope, pool_w[l], pool_scale[l])
        a = _diff_attention(qt, k, vt, lam_qk[l], attn_norm_g[l], l)
        r = _hgrn2(hg, hg_lb_logits, hg_norm_g[l], l)
        xs = _out_projection(a, r, p, xs, w_out, l, g1, ln1_g[l], ln1_b[l])
        if l % 2 == 0:
            i = l // 2
            ride = (exp_w1[i], exp_w3[i], exp_w2[i]) if l + 1 < DEPTH else ()
            xs, cast = _dense_ffn(xs, sc2, sh2, g2, ffn_w1[i].astype(BF16), ffn_w3[i].astype(BF16),
                                  ffn_w2[i].astype(BF16), ln2_g[l], ln2_b[l], ride)
            if ride:
                experts_bf[i] = cast
        else:
            i = l // 2
            h_bf, rank, comb, cum = _router(xs, sc2, sh2, router_w[i])
            w1, w3, w2 = experts_bf.get(i) or [w[i].astype(BF16) for w in (exp_w1, exp_w3, exp_w2)]
            y = _moe_ffn(h_bf, rank, comb, cum, w1, w3, w2)
            xs = _residual_ln(xs, y, g2, ln2_g[l], ln2_b[l])
    return xs.reshape(B, S, D)
```

```python
import functools
import math

import jax
import jax.numpy as jnp
from jax import lax
from jax.experimental import pallas as pl
from jax.experimental.pallas import tpu as pltpu

F32 = jnp.float32
BF16 = jnp.bfloat16
HIGHEST = lax.Precision.HIGHEST

D_MODEL = 1024
DEPTH = 2
DA_HEADS = 4
DA_WIDTH = D_MODEL // 2
DA_HEAD_DIM = DA_WIDTH // (2 * DA_HEADS)
DA_VAL_DIM = 2 * DA_HEAD_DIM
HG_HEADS = 4
HG_WIDTH = D_MODEL // 4
HG_DIM = HG_WIDTH // HG_HEADS
POOL_WINDOWS = (2, 4, 8, 16)
POOL_WIDTH = D_MODEL // 4
POOL_GROUP = POOL_WIDTH // len(POOL_WINDOWS)
D_MIX = DA_WIDTH + HG_WIDTH + POOL_WIDTH
D_IN = 3 * DA_WIDTH + 4 * HG_WIDTH + POOL_WIDTH
ROPE_THETA = 10000.0
MASK_VALUE = -1e30
TINY = 1e-30
FFN_DIM = 11 * D_MODEL // 4
N_EXPERTS = 8
EXPERT_DIM = 7 * D_MODEL // 2
ALPHA = (2 * DEPTH) ** 0.25
EPS = 1e-5
LOG2_E = math.log2(math.e)

LANES = 128
BF16_ROWS = 16
MXU_DIM = 256
VMEM_LIMIT = 60 * 1024 * 1024

ROW_TILE = 1024
FFN_CAST_TILE = 256
ATTN_BLOCK = 512
ATTN_Q_BLOCK = 2048
ATTN_QUERY_GROUP = 256
ATTN_UNROLL = 4
ATTN_SUM_ROWS = BF16_ROWS
ATTN_VT_ROWS = DA_VAL_DIM + ATTN_SUM_ROWS
HG_BLOCK = 512
HG_CHUNK = 16
HG_UNROLL = 4
POOL_HALO = 16
MOE_BLOCK = 2048
MOE_TILE = 192
MOE_TAIL_TILE = 128
MOE_CHUNK = 128
MOE_WINDOW = 8
MOE_FSPLIT = 2


def _params(sem, vmem=VMEM_LIMIT):
    return pltpu.CompilerParams(dimension_semantics=sem, vmem_limit_bytes=vmem)


def _layer_norm(y, g, b):
    mu = jnp.mean(y, axis=-1, keepdims=True)
    d = y - mu
    var = jnp.mean(d * d, axis=-1, keepdims=True)
    return d * lax.rsqrt(var + EPS) * g + b


def _silu(x):
    return x * jax.nn.sigmoid(x)


def _head_block_mask(n, group):
    r = lax.broadcasted_iota(jnp.int32, (n, n), 0) // group
    c = lax.broadcasted_iota(jnp.int32, (n, n), 1) // group
    return r == c


def _mod_kernel(c_ref, w_ref, b_ref, o_ref):
    cond = _silu(c_ref[...])
    cond8 = jnp.broadcast_to(cond, (8, D_MODEL))
    r = jnp.dot(cond8, w_ref[0], preferred_element_type=F32, precision=HIGHEST)
    o_ref[0] = r[0:1] + b_ref[0]


def _modulation(c, w_ada, b_ada):
    n = 6 * DEPTH
    b3 = b_ada.reshape(n, 1, D_MODEL)
    out = pl.pallas_call(
        _mod_kernel,
        grid=(DEPTH, 6),
        in_specs=[
            pl.BlockSpec((1, D_MODEL), lambda l, j: (0, 0)),
            pl.BlockSpec((1, D_MODEL, D_MODEL), lambda l, j: (l, 0, j)),
            pl.BlockSpec((1, 1, D_MODEL), lambda l, j: (l * 6 + j, 0, 0)),
        ],
        out_specs=pl.BlockSpec((1, 1, D_MODEL), lambda l, j: (l * 6 + j, 0, 0)),
        out_shape=jax.ShapeDtypeStruct((n, 1, D_MODEL), F32),
        compiler_params=_params(("parallel", "parallel")),
        name="adaln_mod",
    )(c, w_ada, b3)
    return out.reshape(DEPTH, 6, 1, D_MODEL)


def _inproj_kernel(x_ref, sc_ref, sh_ref, wf_ref, cosb_ref, sinb_ref, cosr_ref, sinr_ref, pw_ref, ps_ref,
                   qt_ref, k_ref, vt_ref, hg_ref, p_ref, w_ref, wvt_ref, halo_scr):
    @pl.when(pl.program_id(0) == 0)
    def _():
        w_ref[...] = wf_ref[0].astype(BF16)
        wvt_ref[...] = wf_ref[0, :, 2 * DA_WIDTH:3 * DA_WIDTH].T.astype(BF16)

    h = (x_ref[...] * (1.0 + sc_ref[...]) + sh_ref[...]).astype(BF16)
    qk = jnp.dot(h, w_ref[:, : 2 * DA_WIDTH], preferred_element_type=F32)
    cb, sb, cr, sr = cosb_ref[0], sinb_ref[0], cosr_ref[...], sinr_ref[...]
    cos128 = cb * cr - sb * sr
    sin128 = sb * cr + cb * sr
    half = DA_HEAD_DIM // 2
    lane128 = lax.broadcasted_iota(jnp.int32, cos128.shape, 1)
    sin128 = jnp.where((lane128 & half) == 0, -sin128, sin128)
    reps = 2 * DA_WIDTH // LANES
    cos = jnp.tile(cos128, (1, reps))
    sin = jnp.tile(sin128, (1, reps))
    lane = lax.broadcasted_iota(jnp.int32, qk.shape, 1)
    first_half = (lane & half) == 0
    width = 2 * DA_WIDTH
    partner = jnp.where(first_half, pltpu.roll(qk, width - half, 1), pltpu.roll(qk, half, 1))
    rot = qk * cos + partner * sin
    qt_ref[0] = (rot[:, :DA_WIDTH] * (DA_HEAD_DIM ** -0.5 * LOG2_E)).T.astype(BF16)
    k_ref[...] = rot[:, DA_WIDTH:].astype(BF16)
    o = 2 * DA_WIDTH
    vt = lax.dot_general(wvt_ref[...], h, (((1,), (1,)), ((), ())),
                         preferred_element_type=F32).astype(BF16)
    ones = jnp.ones((ATTN_SUM_ROWS, vt.shape[1]), BF16)
    pieces = []
    for hd in range(DA_HEADS):
        pieces += [vt[hd * DA_VAL_DIM:(hd + 1) * DA_VAL_DIM], ones]
    vt_ref[0] = jnp.concatenate(pieces, axis=0)
    o += DA_WIDTH
    hg_ref[...] = jnp.dot(h, w_ref[:, o:o + 4 * HG_WIDTH], preferred_element_type=F32)
    o += 4 * HG_WIDTH
    pu = jnp.dot(h, w_ref[:, o:o + POOL_WIDTH], preferred_element_type=F32)
    p_ref[...] = _pool_tile(pu, pw_ref[...], ps_ref[...], halo_scr, pl.program_id(0))


def _in_projection(x, sc, sh, w_in, layer, rope, pool_w, pool_scale):
    S = x.shape[0]
    tm = ATTN_BLOCK
    row = lambda i: (i, 0)
    fixed = lambda i: (0, 0)
    cos_b, sin_b, cos_r, sin_r = rope
    w_bd = jax.scipy.linalg.block_diag(*[pool_w[g] for g in range(len(POOL_WINDOWS))]).astype(BF16)
    return pl.pallas_call(
        _inproj_kernel,
        grid=(S // tm,),
        in_specs=[
            pl.BlockSpec((tm, D_MODEL), row),
            pl.BlockSpec((1, D_MODEL), fixed),
            pl.BlockSpec((1, D_MODEL), fixed),
            pl.BlockSpec((1, D_MODEL, D_IN), lambda i: (layer, 0, 0), pipeline_mode=pl.Buffered(1)),
            pl.BlockSpec((1, 1, LANES), lambda i: (i, 0, 0)),
            pl.BlockSpec((1, 1, LANES), lambda i: (i, 0, 0)),
            pl.BlockSpec((tm, LANES), fixed),
            pl.BlockSpec((tm, LANES), fixed),
            pl.BlockSpec((POOL_WIDTH, POOL_WIDTH), fixed),
            pl.BlockSpec((1, POOL_WIDTH), fixed),
        ],
        out_specs=[
            pl.BlockSpec((1, DA_WIDTH, tm), lambda i: (i, 0, 0)),
            pl.BlockSpec((tm, DA_WIDTH), row),
            pl.BlockSpec((1, DA_HEADS * ATTN_VT_ROWS, tm), lambda i: (i, 0, 0)),
            pl.BlockSpec((tm, 4 * HG_WIDTH), row),
            pl.BlockSpec((tm, POOL_WIDTH), row),
        ],
        out_shape=[
            jax.ShapeDtypeStruct((S // tm, DA_WIDTH, tm), BF16),
            jax.ShapeDtypeStruct((S, DA_WIDTH), BF16),
            jax.ShapeDtypeStruct((S // tm, DA_HEADS * ATTN_VT_ROWS, tm), BF16),
            jax.ShapeDtypeStruct((S, 4 * HG_WIDTH), F32),
            jax.ShapeDtypeStruct((S, POOL_WIDTH), BF16),
        ],
        scratch_shapes=[
            pltpu.VMEM((D_MODEL, D_IN), BF16),
            pltpu.VMEM((DA_WIDTH, D_MODEL), BF16),
            pltpu.VMEM((POOL_HALO, POOL_WIDTH), F32),
        ],
        compiler_params=_params(("arbitrary",)),
        name="in_proj",
    )(x, sc, sh, w_in, cos_b, sin_b, cos_r, sin_r, w_bd, pool_scale.reshape(1, POOL_WIDTH))


def _attn_kernel(lam_ref, g_ref, qt_ref, k_ref, vt_ref, o_ref, m_scr, a_scr, c_scr, acc_scr, s_scr,
                 p_scr, *, lam_init):
    bq = qt_ref.shape[0] * qt_ref.shape[2]
    bk = vt_ref.shape[2]
    cg = ATTN_QUERY_GROUP
    i = pl.program_id(1)
    qt = jnp.concatenate([qt_ref[n] for n in range(qt_ref.shape[0])], axis=1)
    rowq = lax.broadcasted_iota(jnp.int32, qt.shape, 0)
    zero = jnp.zeros_like(qt)
    maps = (jnp.where(rowq < DA_HEAD_DIM, qt, zero), jnp.where(rowq >= DA_HEAD_DIM, qt, zero))
    groups = [(a * bq + c0, c0, maps[a][:, c0:c0 + cg]) for a in range(2) for c0 in range(0, bq, cg)]
    m_scr[...] = jnp.full(m_scr.shape, -jnp.inf, F32)
    acc_scr[...] = jnp.zeros(acc_scr.shape, F32)
    hk = bk // 2
    key_i = lax.broadcasted_iota(jnp.int32, (hk, cg), 0)
    qry_i = lax.broadcasted_iota(jnp.int32, (hk, cg), 1)
    ng = len(groups)
    last = ng - 1

    def cols_of(g):
        return slice(groups[g][0], groups[g][0] + cg)

    def ccols_of(sset, g):
        return slice(sset * 2 * bq + groups[g][0], sset * 2 * bq + groups[g][0] + cg)

    def mask_mode(key_offset, g):
        c0 = groups[g][1]
        if key_offset + hk - 1 <= c0:
            return None
        if key_offset > c0 + cg - 1:
            return "skip"
        return key_offset

    def score_stage(hb, sset, g, key_offset):
        kb = k_ref[pl.ds(pl.multiple_of(hb * hk, hk), hk), :]
        s = jnp.dot(kb, groups[g][2], preferred_element_type=F32)
        if key_offset is not None:
            s = jnp.where(key_i + key_offset <= qry_i + groups[g][1], s, MASK_VALUE)
        s_scr[sset * ng + g] = s
        c_scr[:, ccols_of(sset, g)] = jnp.max(s, axis=0, keepdims=True)

    def max_stage(sset, g):
        cols = cols_of(g)
        m_prev = m_scr[:, cols]
        m_new = jnp.maximum(m_prev, c_scr[:, ccols_of(sset, g)])
        a_scr[:, cols] = jnp.exp2(m_prev - m_new)
        m_scr[:, cols] = m_new

    def exp_stage(sset, g, vt_half, carried, defer):
        cols = cols_of(g)
        p = jnp.exp2(s_scr[sset * ng + g] - m_scr[:, cols])
        alpha = a_scr[:, cols]
        pb = p.astype(BF16)
        acc = acc_scr[:, cols]
        if carried is not None:
            acc = acc + carried
        if defer:
            acc_scr[:, cols] = alpha * acc
            p_scr[...] = pb
        else:
            acc_scr[:, cols] = alpha * acc + jnp.dot(vt_half, pb, preferred_element_type=F32)

    def items(key_offset):
        if key_offset is None:
            return [(g, None) for g in range(ng)]
        return [(g, mask_mode(key_offset, g)) for g in range(ng) if mask_mode(key_offset, g) != "skip"]

    def iteration(t, offs, look):
        vt_a = vt_ref[t, :, 0:hk]
        vt_b = vt_ref[t, :, hk:bk]
        carried = jnp.dot(vt_ref[jnp.maximum(t - 1, 0), :, hk:bk], p_scr[...],
                          preferred_element_type=F32)
        a_items = items(offs)
        b_items = items(None if offs is None else offs + hk)
        n_items = [] if look == "stop" else items(look)
        score_stage(2 * t + 1, 1, *b_items[0])
        for n, (g, _) in enumerate(a_items):
            if n + 1 < len(b_items):
                score_stage(2 * t + 1, 1, *b_items[n + 1])
            if n + 1 < len(a_items):
                max_stage(0, a_items[n + 1][0])
            else:
                max_stage(1, b_items[0][0])
            exp_stage(0, g, vt_a, carried if g == last else None, False)
        if n_items:
            score_stage(2 * t + 2, 0, *n_items[0])
        for n, (g, _) in enumerate(b_items):
            if n + 1 < len(n_items):
                score_stage(2 * t + 2, 0, *n_items[n + 1])
            if n + 1 < len(b_items):
                max_stage(1, b_items[n + 1][0])
            elif n_items:
                max_stage(0, n_items[0][0])
            exp_stage(1, g, vt_b, None, bool(n_items) and g == last)

    p_scr[...] = jnp.zeros(p_scr.shape, BF16)
    for g in range(ng):
        score_stage(0, 0, g, (0 - i) * bq)
    max_stage(0, 0)

    per_q = bq // bk
    first = i * per_q
    unroll = ATTN_UNROLL

    def body(u, carry):
        for k in range(unroll):
            iteration(unroll * u + k, None, None)
        return carry

    n_plain = jnp.maximum(first - 1, 0)
    trips = n_plain // unroll
    lax.fori_loop(0, trips, body, 0)
    for k in range(unroll - 1):
        @pl.when(trips * unroll + k < n_plain)
        def _():
            iteration(trips * unroll + k, None, None)

    @pl.when(i > 0)
    def _():
        iteration(first - 1, None, 0)

    for d in range(per_q):
        iteration(first + d, d * bk, (d + 1) * bk if d + 1 < per_q else "stop")

    lq = lam_ref[...]
    lam = (jnp.exp(jnp.sum(lq[0:1] * lq[1:2], axis=1, keepdims=True))
           - jnp.exp(jnp.sum(lq[2:3] * lq[3:4], axis=1, keepdims=True)) + lam_init)
    o = acc_scr[0:DA_VAL_DIM, :] / acc_scr[DA_VAL_DIM:DA_VAL_DIM + 1, :]
    o = o[:, :bq] - lam * o[:, bq:]
    ms = jnp.mean(o * o, axis=0, keepdims=True)
    o = o * lax.rsqrt(ms + EPS) * g_ref[...] * (1.0 - lam_init)
    o_ref[...] = o.T.astype(BF16)


def _diff_attention(qt, k, vt, lam_qk, attn_g, layer):
    S = k.shape[0]
    bk = ATTN_BLOCK
    bq = ATTN_Q_BLOCK
    nk = S // bk
    lam_init = 0.8 - 0.6 * math.exp(-0.3 * layer)
    return pl.pallas_call(
        functools.partial(_attn_kernel, lam_init=lam_init),
        grid=(DA_HEADS, S // bq),
        in_specs=[
            pl.BlockSpec((4, DA_HEAD_DIM), lambda h, i: (0, 0)),
            pl.BlockSpec((DA_VAL_DIM, 1), lambda h, i: (0, 0)),
            pl.BlockSpec((bq // bk, LANES, bk), lambda h, i: (i, h, 0)),
            pl.BlockSpec((S, LANES), lambda h, i: (0, h)),
            pl.BlockSpec((nk, ATTN_VT_ROWS, bk), lambda h, i: (0, h, 0)),
        ],
        out_specs=pl.BlockSpec((bq, DA_VAL_DIM), lambda h, i: (i, h)),
        out_shape=jax.ShapeDtypeStruct((S, DA_WIDTH), BF16),
        scratch_shapes=[
            pltpu.VMEM((1, 2 * bq), F32),
            pltpu.VMEM((1, 2 * bq), F32),
            pltpu.VMEM((1, 2 * 2 * bq), F32),
            pltpu.VMEM((ATTN_VT_ROWS, 2 * bq), F32),
            pltpu.VMEM((2 * (2 * bq // ATTN_QUERY_GROUP), bk // 2, ATTN_QUERY_GROUP), F32),
            pltpu.VMEM((bk // 2, ATTN_QUERY_GROUP), BF16),
        ],
        compiler_params=_params(("parallel", "parallel")),
        name="diff_attn",
    )(lam_qk, attn_g.reshape(DA_VAL_DIM, 1), qt, k, vt)


def _hgrn_kernel(lbl_ref, g_ref, hg_ref, o_ref, st_scr, stb_scr, o_scr, *, layer):
    T = hg_ref.shape[0]
    c = HG_CHUNK
    W = HG_WIDTH

    @pl.when(pl.program_id(0) == 0)
    def _():
        st_scr[...] = jnp.zeros(st_scr.shape, F32)
        stb_scr[...] = jnp.zeros(stb_scr.shape, BF16)

    lg = lbl_ref[...]
    ex = jnp.exp(lg - jnp.max(lg, axis=0, keepdims=True))
    prob = ex / jnp.sum(ex, axis=0, keepdims=True)
    lb = jnp.zeros((1, W), F32)
    for li in range(1, layer + 1):
        lb = lb + prob[li:li + 1]

    same_head = _head_block_mask(W, HG_DIM)
    ones_bd = same_head.astype(BF16)
    rowi = lax.broadcasted_iota(jnp.int32, (c, W), 0)

    def body(s, carry):
        sl = pl.ds(pl.multiple_of(s * c, c), c)
        qf = hg_ref[sl, 0:W]
        z = hg_ref[sl, W:2 * W]
        iv = hg_ref[sl, 2 * W:3 * W]
        f = lb + (1.0 - lb) * jax.nn.sigmoid(z)
        logf = jnp.log(jnp.maximum(f, TINY))
        key = (1.0 - lb) * jax.nn.sigmoid(-z)
        val = _silu(iv)
        b = logf
        sft = 1
        while sft < c:
            b = b + jnp.where(rowi >= sft, pltpu.roll(b, sft, 0), 0.0)
            sft *= 2
        b_last = b[c - 1:c]
        kd = (key * jnp.exp(b_last - b)).astype(BF16)
        upd = lax.dot_general(val.astype(BF16), kd, (((0,), (0,)), ((), ())),
                              preferred_element_type=F32)
        es = [(qf * key).astype(BF16)]
        vrs = [val]
        for d in range(1, c):
            kr = pltpu.roll(key, d, 0)
            br = pltpu.roll(b, d, 0)
            e = jnp.where(rowi >= d, qf * kr * jnp.exp(b - br), 0.0)
            es.append(e.astype(BF16))
            vrs.append(pltpu.roll(val, d, 0))
        estack = jnp.concatenate(es, axis=0)
        r = jnp.dot(estack, ones_bd, preferred_element_type=F32)
        o = r[0:c] * vrs[0]
        for d in range(1, c):
            o = o + r[d * c:(d + 1) * c] * vrs[d]
        qd = (qf * jnp.exp(b)).astype(BF16)
        o = o + lax.dot_general(qd, stb_scr[...], (((1,), (1,)), ((), ())),
                                preferred_element_type=F32)
        o_scr[sl, :] = o
        decay = jnp.exp(b_last)
        for hd in range(HG_HEADS):
            rws = slice(hd * HG_DIM, (hd + 1) * HG_DIM)
            lns = slice(hd * HG_DIM // LANES * LANES, (hd * HG_DIM // LANES + 1) * LANES)
            new = decay[:, lns] * st_scr[rws, lns] + jnp.where(same_head[rws, lns], upd[rws, lns], 0.0)
            st_scr[rws, lns] = new
            stb_scr[rws, lns] = new.astype(BF16)
        return carry

    lax.fori_loop(0, T // c, body, 0, unroll=HG_UNROLL)

    o = o_scr[...]
    ms = jnp.dot(o * o, same_head.astype(F32), preferred_element_type=F32,
                 precision=HIGHEST) * (1.0 / HG_DIM)
    gate = hg_ref[:, 3 * W:4 * W]
    o_ref[...] = (o * lax.rsqrt(ms + EPS) * g_ref[...] * _silu(gate)).astype(BF16)


def _hgrn2(hg, lb_logits, norm_g, layer):
    S = hg.shape[0]
    T = HG_BLOCK
    g_t = jnp.tile(norm_g.reshape(1, HG_DIM), (1, HG_HEADS))
    return pl.pallas_call(
        functools.partial(_hgrn_kernel, layer=layer),
        grid=(S // T,),
        in_specs=[
            pl.BlockSpec((DEPTH, HG_WIDTH), lambda i: (0, 0)),
            pl.BlockSpec((1, HG_WIDTH), lambda i: (0, 0)),
            pl.BlockSpec((T, 4 * HG_WIDTH), lambda i: (i, 0)),
        ],
        out_specs=pl.BlockSpec((T, HG_WIDTH), lambda i: (i, 0)),
        out_shape=jax.ShapeDtypeStruct((S, HG_WIDTH), BF16),
        scratch_shapes=[
            pltpu.VMEM((HG_WIDTH, HG_WIDTH), F32),
            pltpu.VMEM((HG_WIDTH, HG_WIDTH), BF16),
            pltpu.VMEM((T, HG_WIDTH), F32),
        ],
        compiler_params=_params(("arbitrary",)),
        name="hgrn2",
    )(lb_logits, g_t, hg)


def _pool_tile(u, w_bd, scale, halo_scr, blk):
    T = u.shape[0]
    H = POOL_HALO

    @pl.when(blk == 0)
    def _():
        halo_scr[...] = jnp.zeros(halo_scr.shape, F32)

    ext = jnp.concatenate([halo_scr[...], u], axis=0)
    halo_scr[...] = u[T - H:, :]
    sums = []
    s = ext
    w = 1
    while w < POOL_WINDOWS[-1]:
        s = s + pltpu.roll(s, w, 0)
        w *= 2
        sums.append(s[H:, :])
    t1 = (blk * T + 1 + lax.broadcasted_iota(jnp.int32, (T, POOL_WIDTH), 0)).astype(F32)
    lane = lax.broadcasted_iota(jnp.int32, (T, POOL_WIDTH), 1)
    pooled = None
    for gi, win in enumerate(POOL_WINDOWS):
        mean = sums[gi] / jnp.minimum(t1, float(win))
        pooled = mean if pooled is None else jnp.where(lane >= gi * POOL_GROUP, mean, pooled)
    pooled = pooled - u
    y = jnp.dot(pooled.astype(BF16), w_bd, preferred_element_type=F32)
    return (y * scale).astype(BF16)


def _outproj_kernel(a_ref, r_ref, p_ref, x_ref, wf_ref, g1_ref, lg_ref, lb_ref, o_ref, w_ref):
    @pl.when(pl.program_id(0) == 0)
    def _():
        w_ref[...] = wf_ref[0].astype(BF16)

    m = jnp.dot(a_ref[...], w_ref[0:DA_WIDTH, :], preferred_element_type=F32)
    m += jnp.dot(r_ref[...], w_ref[DA_WIDTH:DA_WIDTH + HG_WIDTH, :], preferred_element_type=F32)
    m += jnp.dot(p_ref[...], w_ref[DA_WIDTH + HG_WIDTH:, :], preferred_element_type=F32)
    y = ALPHA * x_ref[...] + (1.0 + g1_ref[...]) * m
    o_ref[...] = _layer_norm(y, lg_ref[...], lb_ref[...])


def _out_projection(a, r, p, x, w_out, layer, g1, ln_g, ln_b):
    S = x.shape[0]
    tm = ROW_TILE
    row = lambda i: (i, 0)
    fixed = lambda i: (0, 0)
    vec = pl.BlockSpec((1, D_MODEL), fixed)
    return pl.pallas_call(
        _outproj_kernel,
        grid=(S // tm,),
        in_specs=[
            pl.BlockSpec((tm, DA_WIDTH), row),
            pl.BlockSpec((tm, HG_WIDTH), row),
            pl.BlockSpec((tm, POOL_WIDTH), row),
            pl.BlockSpec((tm, D_MODEL), row),
            pl.BlockSpec((1, D_MIX, D_MODEL), lambda i: (layer, 0, 0), pipeline_mode=pl.Buffered(1)),
            vec, vec, vec,
        ],
        out_specs=pl.BlockSpec((tm, D_MODEL), row),
        out_shape=jax.ShapeDtypeStruct((S, D_MODEL), F32),
        scratch_shapes=[pltpu.VMEM((D_MIX, D_MODEL), BF16)],
        compiler_params=_params(("arbitrary",)),
        name="out_proj_ln",
    )(a, r, p, x, w_out, g1, ln_g.reshape(1, D_MODEL), ln_b.reshape(1, D_MODEL))


def _ffn_kernel(x_ref, sc_ref, sh_ref, g2_ref, w1_ref, w3_ref, w2_ref, lg_ref, lb_ref, *rest):
    n_cast = (len(rest) - 1) // 2
    o_ref = rest[n_cast]
    for src, dst in zip(rest[:n_cast], rest[n_cast + 1:]):
        dst[...] = src[...].astype(BF16)
    x = x_ref[...]
    h = (x * (1.0 + sc_ref[...]) + sh_ref[...]).astype(BF16)
    acc = jnp.zeros(x.shape, F32)
    step = 4 * MXU_DIM
    for lo in range(0, FFN_DIM, step):
        hi = min(lo + step, FFN_DIM)
        a = jnp.dot(h, w1_ref[:, lo:hi], preferred_element_type=F32)
        b = jnp.dot(h, w3_ref[:, lo:hi], preferred_element_type=F32)
        acc += jnp.dot((_silu(a) * b).astype(BF16), w2_ref[lo:hi, :], preferred_element_type=F32)
    y = ALPHA * x + (1.0 + g2_ref[...]) * acc
    o_ref[...] = _layer_norm(y, lg_ref[...], lb_ref[...])


def _dense_ffn(x, sc, sh, g2, w1, w3, w2, ln_g, ln_b, to_cast=()):
    S = x.shape[0]
    tm = FFN_CAST_TILE if to_cast else ROW_TILE
    steps = S // tm
    row = lambda i: (i, 0)
    fixed = lambda i: (0, 0)
    vec = pl.BlockSpec((1, D_MODEL), fixed)
    once = pl.Buffered(1)
    slabs = [a.reshape(-1, a.shape[-1]) for a in to_cast]
    slab_specs = [pl.BlockSpec((a.shape[0] // steps, a.shape[1]), row) for a in slabs]
    for a in slabs:
        assert a.shape[0] % (steps * BF16_ROWS) == 0
    outs = pl.pallas_call(
        _ffn_kernel,
        grid=(steps,),
        in_specs=[
            pl.BlockSpec((tm, D_MODEL), row),
            vec, vec, vec,
            pl.BlockSpec((D_MODEL, FFN_DIM), fixed, pipeline_mode=once),
            pl.BlockSpec((D_MODEL, FFN_DIM), fixed, pipeline_mode=once),
            pl.BlockSpec((FFN_DIM, D_MODEL), fixed, pipeline_mode=once),
            vec, vec,
        ] + slab_specs,
        out_specs=[pl.BlockSpec((tm, D_MODEL), row)] + slab_specs,
        out_shape=[jax.ShapeDtypeStruct((S, D_MODEL), F32)]
        + [jax.ShapeDtypeStruct(a.shape, BF16) for a in slabs],
        compiler_params=_params(("parallel",)),
        name="dense_ffn_ln",
    )(x, sc, sh, g2, w1, w3, w2, ln_g.reshape(1, D_MODEL), ln_b.reshape(1, D_MODEL), *slabs)
    return outs[0], [o.reshape(a.shape) for o, a in zip(outs[1:], to_cast)]


def _router_kernel(x_ref, sc_ref, sh_ref, rw_ref, h_ref, rank_ref, comb_ref, cum_ref):
    T = x_ref.shape[0]
    tc = MOE_CHUNK
    E = N_EXPERTS
    h = x_ref[...] * (1.0 + sc_ref[...]) + sh_ref[...]
    h_ref[...] = h.astype(BF16)
    logits = lax.dot_general(rw_ref[...], h, (((1,), (1,)), ((), ())),
                             preferred_element_type=F32, precision=HIGHEST)
    eidx = lax.broadcasted_iota(jnp.int32, (E, T), 0)
    v1 = jnp.max(logits, axis=0, keepdims=True)
    i1 = jnp.min(jnp.where(logits == v1, eidx, E), axis=0, keepdims=True)
    m1 = eidx == i1
    rest = jnp.where(m1, -jnp.inf, logits)
    v2 = jnp.max(rest, axis=0, keepdims=True)
    i2 = jnp.min(jnp.where(rest == v2, eidx, E), axis=0, keepdims=True)
    m2 = eidx == i2
    e2 = jnp.exp(v2 - v1)
    g1 = 1.0 / (1.0 + e2)
    g2 = e2 / (1.0 + e2)
    comb = jnp.where(m1, g1, 0.0) + jnp.where(m2, g2, 0.0)
    routed = jnp.logical_or(m1, m2)
    ind = routed.astype(BF16)
    r_i = lax.broadcasted_iota(jnp.int32, (tc, tc), 0)
    c_i = lax.broadcasted_iota(jnp.int32, (tc, tc), 1)
    strict = (r_i < c_i).astype(BF16)
    lane = lax.broadcasted_iota(jnp.int32, (E, LANES), 1)
    running = jnp.zeros((E, 1), F32)
    cum = jnp.zeros((E, LANES), F32)
    for ci in range(T // tc):
        sl = slice(ci * tc, (ci + 1) * tc)
        ind_c = ind[:, sl]
        rank_c = jnp.dot(ind_c, strict, preferred_element_type=F32) + running
        rank_ref[:, ci] = jnp.where(routed[:, sl], rank_c, -1.0).reshape(E, 1, tc)
        comb_ref[:, ci] = comb[:, sl].reshape(E, 1, tc)
        cum = jnp.where(lane == ci, running, cum)
        running = running + jnp.sum(ind_c.astype(F32), axis=1, keepdims=True)
    cum = jnp.where(lane == T // tc, running, cum)
    cum_ref[0] = cum.astype(jnp.int32)


def _router(x, sc, sh, router_w):
    S = x.shape[0]
    T = MOE_BLOCK
    E = N_EXPERTS
    fixed = lambda i: (0, 0)
    vec = pl.BlockSpec((1, D_MODEL), fixed)
    return pl.pallas_call(
        _router_kernel,
        grid=(S // T,),
        in_specs=[
            pl.BlockSpec((T, D_MODEL), lambda i: (i, 0)),
            vec, vec,
            pl.BlockSpec((E, D_MODEL), fixed),
        ],
        out_specs=[
            pl.BlockSpec((T, D_MODEL), lambda i: (i, 0)),
            pl.BlockSpec((E, T // MOE_CHUNK, 1, MOE_CHUNK), lambda i: (0, i, 0, 0)),
            pl.BlockSpec((E, T // MOE_CHUNK, 1, MOE_CHUNK), lambda i: (0, i, 0, 0)),
            pl.BlockSpec((1, E, LANES), lambda i: (i, 0, 0)),
        ],
        out_shape=[
            jax.ShapeDtypeStruct((S, D_MODEL), BF16),
            jax.ShapeDtypeStruct((E, S // MOE_CHUNK, 1, MOE_CHUNK), F32),
            jax.ShapeDtypeStruct((E, S // MOE_CHUNK, 1, MOE_CHUNK), F32),
            jax.ShapeDtypeStruct((S // T, E, LANES), jnp.int32),
        ],
        compiler_params=_params(("parallel",)),
        name="moe_router",
    )(x, sc, sh, router_w.T)


def _moe_kernel(cum_ref, h_ref, rank_ref, comb_ref, w1_ref, w3_ref, w2_ref, y_ref, xe_scr, gate_scr,
                out_scr):
    T = h_ref.shape[0]
    tr, tc, win = MOE_TILE, MOE_CHUNK, MOE_WINDOW
    nchunk = T // tc
    b = pl.program_id(0)
    e = pl.program_id(1)
    f = pl.program_id(2)

    @pl.when(jnp.logical_and(e == 0, f == 0))
    def _():
        y_ref[...] = jnp.zeros(y_ref.shape, F32)

    base = (b * N_EXPERTS + e) * LANES
    count = cum_ref[base + nchunk]

    def token_row(ref, first, n):
        blk = ref[0, pl.ds(first, n)]
        return jnp.concatenate([blk[c] for c in range(n)], axis=1)

    def process(lo, tr):
        rows = pl.ds(pl.multiple_of(lo, BF16_ROWS), tr)

        def hits(first, n):
            want = (lo + lax.broadcasted_iota(jnp.int32, (tr, n * tc), 0)).astype(F32)
            return token_row(rank_ref, first, n) == want

        def token_span(first, n):
            start = first * tc
            if not isinstance(start, int):
                start = pl.multiple_of(start, tc)
            return pl.ds(start, n * tc)

        c_lo = jnp.int32(0)
        c_hi = jnp.int32(-1)
        for c in range(nchunk):
            c_lo += (cum_ref[base + c + 1] <= lo).astype(jnp.int32)
            c_hi += (cum_ref[base + c] < lo + tr).astype(jnp.int32)
        fits = c_hi - c_lo < win
        window = (jnp.minimum(c_lo, nchunk - win), win)
        block = (0, nchunk)

        def gather(first, n):
            hit = hits(first, n)
            xe_scr[rows, :] = jnp.dot(hit.astype(BF16), h_ref[token_span(first, n), :],
                                      preferred_element_type=F32).astype(BF16)
            gate_scr[rows, :] = jnp.sum(jnp.where(hit, token_row(comb_ref, first, n), 0.0),
                                        axis=1, keepdims=True)

        @pl.when(jnp.logical_and(f == 0, fits))
        def _():
            gather(*window)

        @pl.when(jnp.logical_and(f == 0, jnp.logical_not(fits)))
        def _():
            gather(*block)

        xe = xe_scr[rows, :]
        a = jnp.dot(xe, w1_ref[0], preferred_element_type=F32)
        g = jnp.dot(xe, w3_ref[0], preferred_element_type=F32)
        mid = (_silu(a) * g).astype(BF16)
        part = jnp.dot(mid, w2_ref[0], preferred_element_type=F32)

        @pl.when(f == 0)
        def _():
            out_scr[rows, :] = part

        if MOE_FSPLIT > 2:
            @pl.when(jnp.logical_and(f > 0, f < MOE_FSPLIT - 1))
            def _():
                out_scr[rows, :] += part

        @pl.when(f == MOE_FSPLIT - 1)
        def _():
            out_bf = ((out_scr[rows, :] + part) * gate_scr[rows, :]).astype(BF16)

            def scatter(first, n):
                y_ref[token_span(first, n), :] += lax.dot_general(
                    hits(first, n).astype(BF16), out_bf, (((0,), (0,)), ((), ())),
                    preferred_element_type=F32)

            @pl.when(fits)
            def _():
                scatter(*window)

            @pl.when(jnp.logical_not(fits))
            def _():
                scatter(*block)

    full = count // tr
    rem = count - full * tr

    def tile(r, carry):
        process(r * tr, tr)
        return carry

    lax.fori_loop(0, full, tile, 0)

    @pl.when(jnp.logical_and(rem > 0, rem <= MOE_TAIL_TILE))
    def _():
        process(full * tr, MOE_TAIL_TILE)

    @pl.when(rem > MOE_TAIL_TILE)
    def _():
        process(full * tr, tr)


def _moe_ffn(h_bf, rank, comb, cum, w1, w3, w2):
    S = h_bf.shape[0]
    T = MOE_BLOCK
    fw = EXPERT_DIM // MOE_FSPLIT
    grid_spec = pltpu.PrefetchScalarGridSpec(
        num_scalar_prefetch=1,
        grid=(S // T, N_EXPERTS, MOE_FSPLIT),
        in_specs=[
            pl.BlockSpec((T, D_MODEL), lambda b, e, f, c: (b, 0), pipeline_mode=pl.Buffered(1)),
            pl.BlockSpec((1, T // MOE_CHUNK, 1, MOE_CHUNK), lambda b, e, f, c: (e, b, 0, 0)),
            pl.BlockSpec((1, T // MOE_CHUNK, 1, MOE_CHUNK), lambda b, e, f, c: (e, b, 0, 0)),
            pl.BlockSpec((1, D_MODEL, fw), lambda b, e, f, c: (e, 0, f)),
            pl.BlockSpec((1, D_MODEL, fw), lambda b, e, f, c: (e, 0, f)),
            pl.BlockSpec((1, fw, D_MODEL), lambda b, e, f, c: (e, f, 0)),
        ],
        out_specs=pl.BlockSpec((T, D_MODEL), lambda b, e, f, c: (b, 0)),
        scratch_shapes=[
            pltpu.VMEM((T + MOE_TILE, D_MODEL), BF16),
            pltpu.VMEM((T + MOE_TILE, 1), F32),
            pltpu.VMEM((T + MOE_TILE, D_MODEL), F32),
        ],
    )
    return pl.pallas_call(
        _moe_kernel,
        grid_spec=grid_spec,
        out_shape=jax.ShapeDtypeStruct((S, D_MODEL), F32),
        compiler_params=_params(("arbitrary", "arbitrary", "arbitrary")),
        name="moe_experts",
    )(cum.reshape(-1), h_bf, rank, comb, w1, w3, w2)


def _ln_kernel(x_ref, y_ref, g2_ref, lg_ref, lb_ref, o_ref):
    y = ALPHA * x_ref[...] + (1.0 + g2_ref[...]) * y_ref[...]
    o_ref[...] = _layer_norm(y, lg_ref[...], lb_ref[...])


def _residual_ln(x, y, g2, ln_g, ln_b):
    S = x.shape[0]
    tm = ROW_TILE
    row = lambda i: (i, 0)
    vec = pl.BlockSpec((1, D_MODEL), lambda i: (0, 0))
    return pl.pallas_call(
        _ln_kernel,
        grid=(S // tm,),
        in_specs=[pl.BlockSpec((tm, D_MODEL), row), pl.BlockSpec((tm, D_MODEL), row), vec, vec, vec],
        out_specs=pl.BlockSpec((tm, D_MODEL), row),
        out_shape=jax.ShapeDtypeStruct((S, D_MODEL), F32),
        compiler_params=_params(("parallel",)),
        name="residual_ln",
    )(x, y, g2, ln_g.reshape(1, D_MODEL), ln_b.reshape(1, D_MODEL))


def _rope_tables(S):
    d = DA_HEAD_DIM
    tm = ATTN_BLOCK
    inv_freq = 1.0 / (ROPE_THETA ** (jnp.arange(0, d, 2, dtype=F32) / d))
    inv_lane = inv_freq[jnp.arange(LANES) % (d // 2)][None, :]
    base = (jnp.arange(S // tm, dtype=F32) * tm)[:, None] * inv_lane
    offs = jnp.arange(tm, dtype=F32)[:, None] * inv_lane
    shape_b = (S // tm, 1, LANES)
    return jnp.cos(base).reshape(shape_b), jnp.sin(base).reshape(shape_b), jnp.cos(offs), jnp.sin(offs)


def kernel(x, c, w_ada, b_ada, w_in, lam_qk, attn_norm_g, hg_lb_logits, hg_norm_g, pool_w, pool_scale, w_out, ln1_g, ln1_b, ln2_g, ln2_b, ffn_w1, ffn_w3, ffn_w2, router_w, exp_w1, exp_w3, exp_w2):
    B, S, D = x.shape
    assert B == 1 and D == D_MODEL and S % MOE_BLOCK == 0 and S % ATTN_Q_BLOCK == 0
    xs = x.reshape(S, D)
    mod = _modulation(c, w_ada, b_ada)
    rope = _rope_tables(S)
    experts_bf = {}
    for l in range(DEPTH):
        sh1, sc1, g1, sh2, sc2, g2 = (mod[l, j] for j in range(6))
        qt, k, vt, hg, p = _in_projection(xs, sc1, sh1, w_in, l, rope, pool_w[l], pool_scale[l])
        a = _diff_attention(qt, k, vt, lam_qk[l], attn_norm_g[l], l)
        r = _hgrn2(hg, hg_lb_logits, hg_norm_g[l], l)
        xs = _out_projection(a, r, p, xs, w_out, l, g1, ln1_g[l], ln1_b[l])
        if l % 2 == 0:
            i = l // 2
            ride = (exp_w1[i], exp_w3[i], exp_w2[i]) if l + 1 < DEPTH else ()
            xs, cast = _dense_ffn(xs, sc2, sh2, g2, ffn_w1[i].astype(BF16), ffn_w3[i].astype(BF16),
                                  ffn_w2[i].astype(BF16), ln2_g[l], ln2_b[l], ride)
            if ride:
                experts_bf[i] = cast
        else:
            i = l // 2
            h_bf, rank, comb, cum = _router(xs, sc2, sh2, router_w[i])
            w1, w3, w2 = experts_bf.get(i) or [w[i].astype(BF16) for w in (exp_w1, exp_w3, exp_w2)]
            y = _moe_ffn(h_bf, rank, comb, cum, w1, w3, w2)
            xs = _residual_ln(xs, y, g2, ln2_g[l], ln2_b[l])
    return xs.reshape(B, S, D)
```

```python
import functools
import math

import jax
import jax.numpy as jnp
from jax import lax
from jax.experimental import pallas as pl
from jax.experimental.pallas import tpu as pltpu

F32 = jnp.float32
BF16 = jnp.bfloat16
HIGHEST = lax.Precision.HIGHEST

D_MODEL = 1024
DEPTH = 2
DA_HEADS = 4
DA_WIDTH = D_MODEL // 2
DA_HEAD_DIM = DA_WIDTH // (2 * DA_HEADS)
DA_VAL_DIM = 2 * DA_HEAD_DIM
HG_HEADS = 4
HG_WIDTH = D_MODEL // 4
HG_DIM = HG_WIDTH // HG_HEADS
POOL_WINDOWS = (2, 4, 8, 16)
POOL_WIDTH = D_MODEL // 4
POOL_GROUP = POOL_WIDTH // len(POOL_WINDOWS)
D_MIX = DA_WIDTH + HG_WIDTH + POOL_WIDTH
D_IN = 3 * DA_WIDTH + 4 * HG_WIDTH + POOL_WIDTH
ROPE_THETA = 10000.0
MASK_VALUE = -1e30
TINY = 1e-30
FFN_DIM = 11 * D_MODEL // 4
N_EXPERTS = 8
EXPERT_DIM = 7 * D_MODEL // 2
ALPHA = (2 * DEPTH) ** 0.25
EPS = 1e-5
LOG2_E = math.log2(math.e)

LANES = 128
BF16_ROWS = 16
MXU_DIM = 256
VMEM_LIMIT = 60 * 1024 * 1024

ROW_TILE = 1024
FFN_CAST_TILE = 256
ATTN_BLOCK = 512
ATTN_Q_BLOCK = 2048
ATTN_QUERY_GROUP = 256
ATTN_UNROLL = 4
ATTN_SUM_ROWS = BF16_ROWS
ATTN_VT_ROWS = DA_VAL_DIM + ATTN_SUM_ROWS
HG_BLOCK = 512
HG_CHUNK = 16
HG_UNROLL = 4
POOL_HALO = 16
MOE_BLOCK = 2048
MOE_TILE = 192
MOE_TAIL_TILE = 128
MOE_CHUNK = 128
MOE_WINDOW = 8
MOE_FSPLIT = 2


def _params(sem, vmem=VMEM_LIMIT):
    return pltpu.CompilerParams(dimension_semantics=sem, vmem_limit_bytes=vmem)


def _layer_norm(y, g, b):
    mu = jnp.mean(y, axis=-1, keepdims=True)
    d = y - mu
    var = jnp.mean(d * d, axis=-1, keepdims=True)
    return d * lax.rsqrt(var + EPS) * g + b


def _silu(x):
    return x * jax.nn.sigmoid(x)


def _head_block_mask(n, group):
    r = lax.broadcasted_iota(jnp.int32, (n, n), 0) // group
    c = lax.broadcasted_iota(jnp.int32, (n, n), 1) // group
    return r == c


def _mod_kernel(c_ref, w_ref, b_ref, o_ref):
    cond = _silu(c_ref[...])
    cond8 = jnp.broadcast_to(cond, (8, D_MODEL))
    r = jnp.dot(cond8, w_ref[0], preferred_element_type=F32, precision=HIGHEST)
    o_ref[0] = r[0:1] + b_ref[0]


def _modulation(c, w_ada, b_ada):
    n = 6 * DEPTH
    b3 = b_ada.reshape(n, 1, D_MODEL)
    out = pl.pallas_call(
        _mod_kernel,
        grid=(DEPTH, 6),
        in_specs=[
            pl.BlockSpec((1, D_MODEL), lambda l, j: (0, 0)),
            pl.BlockSpec((1, D_MODEL, D_MODEL), lambda l, j: (l, 0, j)),
            pl.BlockSpec((1, 1, D_MODEL), lambda l, j: (l * 6 + j, 0, 0)),
        ],
        out_specs=pl.BlockSpec((1, 1, D_MODEL), lambda l, j: (l * 6 + j, 0, 0)),
        out_shape=jax.ShapeDtypeStruct((n, 1, D_MODEL), F32),
        compiler_params=_params(("parallel", "parallel")),
        name="adaln_mod",
    )(c, w_ada, b3)
    return out.reshape(DEPTH, 6, 1, D_MODEL)


def _inproj_kernel(x_ref, sc_ref, sh_ref, wf_ref, cosb_ref, sinb_ref, cosr_ref, sinr_ref, pw_ref, ps_ref,
                   qt_ref, k_ref, vt_ref, hg_ref, p_ref, w_ref, wvt_ref, halo_scr):
    @pl.when(pl.program_id(0) == 0)
    def _():
        w_ref[...] = wf_ref[0].astype(BF16)
        wvt_ref[...] = wf_ref[0, :, 2 * DA_WIDTH:3 * DA_WIDTH].T.astype(BF16)

    h = (x_ref[...] * (1.0 + sc_ref[...]) + sh_ref[...]).astype(BF16)
    qk = jnp.dot(h, w_ref[:, : 2 * DA_WIDTH], preferred_element_type=F32)
    cb, sb, cr, sr = cosb_ref[0], sinb_ref[0], cosr_ref[...], sinr_ref[...]
    cos128 = cb * cr - sb * sr
    sin128 = sb * cr + cb * sr
    half = DA_HEAD_DIM // 2
    lane128 = lax.broadcasted_iota(jnp.int32, cos128.shape, 1)
    sin128 = jnp.where((lane128 & half) == 0, -sin128, sin128)
    reps = 2 * DA_WIDTH // LANES
    cos = jnp.tile(cos128, (1, reps))
    sin = jnp.tile(sin128, (1, reps))
    lane = lax.broadcasted_iota(jnp.int32, qk.shape, 1)
    first_half = (lane & half) == 0
    width = 2 * DA_WIDTH
    partner = jnp.where(first_half, pltpu.roll(qk, width - half, 1), pltpu.roll(qk, half, 1))
    rot = qk * cos + partner * sin
    qt_ref[0] = (rot[:, :DA_WIDTH] * (DA_HEAD_DIM ** -0.5 * LOG2_E)).T.astype(BF16)
    k_ref[...] = rot[:, DA_WIDTH:].astype(BF16)
    o = 2 * DA_WIDTH
    vt = lax.dot_general(wvt_ref[...], h, (((1,), (1,)), ((), ())),
                         preferred_element_type=F32).astype(BF16)
    ones = jnp.ones((ATTN_SUM_ROWS, vt.shape[1]), BF16)
    pieces = []
    for hd in range(DA_HEADS):
        pieces += [vt[hd * DA_VAL_DIM:(hd + 1) * DA_VAL_DIM], ones]
    vt_ref[0] = jnp.concatenate(pieces, axis=0)
    o += DA_WIDTH
    hg_ref[...] = jnp.dot(h, w_ref[:, o:o + 4 * HG_WIDTH], preferred_element_type=F32)
    o += 4 * HG_WIDTH
    pu = jnp.dot(h, w_ref[:, o:o + POOL_WIDTH], preferred_element_type=F32)
    p_ref[...] = _pool_tile(pu, pw_ref[...], ps_ref[...], halo_scr, pl.program_id(0))


def _in_projection(x, sc, sh, w_in, layer, rope, pool_w, pool_scale):
    S = x.shape[0]
    tm = ATTN_BLOCK
    row = lambda i: (i, 0)
    fixed = lambda i: (0, 0)
    cos_b, sin_b, cos_r, sin_r = rope
    w_bd = jax.scipy.linalg.block_diag(*[pool_w[g] for g in range(len(POOL_WINDOWS))]).astype(BF16)
    return pl.pallas_call(
        _inproj_kernel,
        grid=(S // tm,),
        in_specs=[
            pl.BlockSpec((tm, D_MODEL), row),
            pl.BlockSpec((1, D_MODEL), fixed),
            pl.BlockSpec((1, D_MODEL), fixed),
            pl.BlockSpec((1, D_MODEL, D_IN), lambda i: (layer, 0, 0), pipeline_mode=pl.Buffered(1)),
            pl.BlockSpec((1, 1, LANES), lambda i: (i, 0, 0)),
            pl.BlockSpec((1, 1, LANES), lambda i: (i, 0, 0)),
            pl.BlockSpec((tm, LANES), fixed),
            pl.BlockSpec((tm, LANES), fixed),
            pl.BlockSpec((POOL_WIDTH, POOL_WIDTH), fixed),
            pl.BlockSpec((1, POOL_WIDTH), fixed),
        ],
        out_specs=[
            pl.BlockSpec((1, DA_WIDTH, tm), lambda i: (i, 0, 0)),
            pl.BlockSpec((tm, DA_WIDTH), row),
            pl.BlockSpec((1, DA_HEADS * ATTN_VT_ROWS, tm), lambda i: (i, 0, 0)),
            pl.BlockSpec((tm, 4 * HG_WIDTH), row),
            pl.BlockSpec((tm, POOL_WIDTH), row),
        ],
        out_shape=[
            jax.ShapeDtypeStruct((S // tm, DA_WIDTH, tm), BF16),
            jax.ShapeDtypeStruct((S, DA_WIDTH), BF16),
            jax.ShapeDtypeStruct((S // tm, DA_HEADS * ATTN_VT_ROWS, tm), BF16),
            jax.ShapeDtypeStruct((S, 4 * HG_WIDTH), F32),
            jax.ShapeDtypeStruct((S, POOL_WIDTH), BF16),
        ],
        scratch_shapes=[
            pltpu.VMEM((D_MODEL, D_IN), BF16),
            pltpu.VMEM((DA_WIDTH, D_MODEL), BF16),
            pltpu.VMEM((POOL_HALO, POOL_WIDTH), F32),
        ],
        compiler_params=_params(("arbitrary",)),
        name="in_proj",
    )(x, sc, sh, w_in, cos_b, sin_b, cos_r, sin_r, w_bd, pool_scale.reshape(1, POOL_WIDTH))


def _attn_kernel(lam_ref, g_ref, qt_ref, k_ref, vt_ref, o_ref, m_scr, a_scr, c_scr, acc_scr, s_scr,
                 p_scr, *, lam_init):
    bq = qt_ref.shape[0] * qt_ref.shape[2]
    bk = vt_ref.shape[2]
    cg = ATTN_QUERY_GROUP
    i = pl.program_id(1)
    qt = jnp.concatenate([qt_ref[n] for n in range(qt_ref.shape[0])], axis=1)
    rowq = lax.broadcasted_iota(jnp.int32, qt.shape, 0)
    zero = jnp.zeros_like(qt)
    maps = (jnp.where(rowq < DA_HEAD_DIM, qt, zero), jnp.where(rowq >= DA_HEAD_DIM, qt, zero))
    groups = [(a * bq + c0, c0, maps[a][:, c0:c0 + cg]) for a in range(2) for c0 in range(0, bq, cg)]
    m_scr[...] = jnp.full(m_scr.shape, -jnp.inf, F32)
    acc_scr[...] = jnp.zeros(acc_scr.shape, F32)
    hk = bk // 2
    key_i = lax.broadcasted_iota(jnp.int32, (hk, cg), 0)
    qry_i = lax.broadcasted_iota(jnp.int32, (hk, cg), 1)
    ng = len(groups)
    last = ng - 1

    def cols_of(g):
        return slice(groups[g][0], groups[g][0] + cg)

    def ccols_of(sset, g):
        return slice(sset * 2 * bq + groups[g][0], sset * 2 * bq + groups[g][0] + cg)

    def mask_mode(key_offset, g):
        c0 = groups[g][1]
        if key_offset + hk - 1 <= c0:
            return None
        if key_offset > c0 + cg - 1:
            return "skip"
        return key_offset

    def score_stage(hb, sset, g, key_offset):
        kb = k_ref[pl.ds(pl.multiple_of(hb * hk, hk), hk), :]
        s = jnp.dot(kb, groups[g][2], preferred_element_type=F32)
        if key_offset is not None:
            s = jnp.where(key_i + key_offset <= qry_i + groups[g][1], s, MASK_VALUE)
        s_scr[sset * ng + g] = s
        c_scr[:, ccols_of(sset, g)] = jnp.max(s, axis=0, keepdims=True)

    def max_stage(sset, g):
        cols = cols_of(g)
        m_prev = m_scr[:, cols]
        m_new = jnp.maximum(m_prev, c_scr[:, ccols_of(sset, g)])
        a_scr[:, cols] = jnp.exp2(m_prev - m_new)
        m_scr[:, cols] = m_new

    def exp_stage(sset, g, vt_half, carried, defer):
        cols = cols_of(g)
        p = jnp.exp2(s_scr[sset * ng + g] - m_scr[:, cols])
        alpha = a_scr[:, cols]
        pb = p.astype(BF16)
        acc = acc_scr[:, cols]
        if carried is not None:
            acc = acc + carried
        if defer:
            acc_scr[:, cols] = alpha * acc
            p_scr[...] = pb
        else:
            acc_scr[:, cols] = alpha * acc + jnp.dot(vt_half, pb, preferred_element_type=F32)

    def items(key_offset):
        if key_offset is None:
            return [(g, None) for g in range(ng)]
        return [(g, mask_mode(key_offset, g)) for g in range(ng) if mask_mode(key_offset, g) != "skip"]

    def iteration(t, offs, look):
        vt_a = vt_ref[t, :, 0:hk]
        vt_b = vt_ref[t, :, hk:bk]
        carried = jnp.dot(vt_ref[jnp.maximum(t - 1, 0), :, hk:bk], p_scr[...],
                          preferred_element_type=F32)
        a_items = items(offs)
        b_items = items(None if offs is None else offs + hk)
        n_items = [] if look == "stop" else items(look)
        score_stage(2 * t + 1, 1, *b_items[0])
        for n, (g, _) in enumerate(a_items):
            if n + 1 < len(b_items):
                score_stage(2 * t + 1, 1, *b_items[n + 1])
            if n + 1 < len(a_items):
                max_stage(0, a_items[n + 1][0])
            else:
                max_stage(1, b_items[0][0])
            exp_stage(0, g, vt_a, carried if g == last else None, False)
        if n_items:
            score_stage(2 * t + 2, 0, *n_items[0])
        for n, (g, _) in enumerate(b_items):
            if n + 1 < len(n_items):
                score_stage(2 * t + 2, 0, *n_items[n + 1])
            if n + 1 < len(b_items):
                max_stage(1, b_items[n + 1][0])
            elif n_items:
                max_stage(0, n_items[0][0])
            exp_stage(1, g, vt_b, None, bool(n_items) and g == last)

    p_scr[...] = jnp.zeros(p_scr.shape, BF16)
    for g in range(ng):
        score_stage(0, 0, g, (0 - i) * bq)
    max_stage(0, 0)

    per_q = bq // bk
    first = i * per_q
    unroll = ATTN_UNROLL

    def body(u, carry):
        for k in range(unroll):
            iteration(unroll * u + k, None, None)
        return carry

    n_plain = jnp.maximum(first - 1, 0)
    trips = n_plain // unroll
    lax.fori_loop(0, trips, body, 0)
    for k in range(unroll - 1):
        @pl.when(trips * unroll + k < n_plain)
        def _():
            iteration(trips * unroll + k, None, None)

    @pl.when(i > 0)
    def _():
        iteration(first - 1, None, 0)

    for d in range(per_q):
        iteration(first + d, d * bk, (d + 1) * bk if d + 1 < per_q else "stop")

    lq = lam_ref[...]
    lam = (jnp.exp(jnp.sum(lq[0:1] * lq[1:2], axis=1, keepdims=True))
           - jnp.exp(jnp.sum(lq[2:3] * lq[3:4], axis=1, keepdims=True)) + lam_init)
    o = acc_scr[0:DA_VAL_DIM, :] / acc_scr[DA_VAL_DIM:DA_VAL_DIM + 1, :]
    o = o[:, :bq] - lam * o[:, bq:]
    ms = jnp.mean(o * o, axis=0, keepdims=True)
    o = o * lax.rsqrt(ms + EPS) * g_ref[...] * (1.0 - lam_init)
    o_ref[...] = o.T.astype(BF16)


def _diff_attention(qt, k, vt, lam_qk, attn_g, layer):
    S = k.shape[0]
    bk = ATTN_BLOCK
    bq = ATTN_Q_BLOCK
    nk = S // bk
    lam_init = 0.8 - 0.6 * math.exp(-0.3 * layer)
    return pl.pallas_call(
        functools.partial(_attn_kernel, lam_init=lam_init),
        grid=(DA_HEADS, S // bq),
        in_specs=[
            pl.BlockSpec((4, DA_HEAD_DIM), lambda h, i: (0, 0)),
            pl.BlockSpec((DA_VAL_DIM, 1), lambda h, i: (0, 0)),
            pl.BlockSpec((bq // bk, LANES, bk), lambda h, i: (i, h, 0)),
            pl.BlockSpec((S, LANES), lambda h, i: (0, h)),
            pl.BlockSpec((nk, ATTN_VT_ROWS, bk), lambda h, i: (0, h, 0)),
        ],
        out_specs=pl.BlockSpec((bq, DA_VAL_DIM), lambda h, i: (i, h)),
        out_shape=jax.ShapeDtypeStruct((S, DA_WIDTH), BF16),
        scratch_shapes=[
            pltpu.VMEM((1, 2 * bq), F32),
            pltpu.VMEM((1, 2 * bq), F32),
            pltpu.VMEM((1, 2 * 2 * bq), F32),
            pltpu.VMEM((ATTN_VT_ROWS, 2 * bq), F32),
            pltpu.VMEM((2 * (2 * bq // ATTN_QUERY_GROUP), bk // 2, ATTN_QUERY_GROUP), F32),
            pltpu.VMEM((bk // 2, ATTN_QUERY_GROUP), BF16),
        ],
        compiler_params=_params(("parallel", "parallel")),
        name="diff_attn",
    )(lam_qk, attn_g.reshape(DA_VAL_DIM, 1), qt, k, vt)


def _hgrn_kernel(lbl_ref, g_ref, hg_ref, o_ref, st_scr, stb_scr, o_scr, *, layer):
    T = hg_ref.shape[0]
    c = HG_CHUNK
    W = HG_WIDTH

    @pl.when(pl.program_id(0) == 0)
    def _():
        st_scr[...] = jnp.zeros(st_scr.shape, F32)
        stb_scr[...] = jnp.zeros(stb_scr.shape, BF16)

    lg = lbl_ref[...]
    ex = jnp.exp(lg - jnp.max(lg, axis=0, keepdims=True))
    prob = ex / jnp.sum(ex, axis=0, keepdims=True)
    lb = jnp.zeros((1, W), F32)
    for li in range(1, layer + 1):
        lb = lb + prob[li:li + 1]

    same_head = _head_block_mask(W, HG_DIM)
    ones_bd = same_head.astype(BF16)
    rowi = lax.broadcasted_iota(jnp.int32, (c, W), 0)

    def body(s, carry):
        sl = pl.ds(pl.multiple_of(s * c, c), c)
        qf = hg_ref[sl, 0:W]
        z = hg_ref[sl, W:2 * W]
        iv = hg_ref[sl, 2 * W:3 * W]
        f = lb + (1.0 - lb) * jax.nn.sigmoid(z)
        logf = jnp.log(jnp.maximum(f, TINY))
        key = (1.0 - lb) * jax.nn.sigmoid(-z)
        val = _silu(iv)
        b = logf
        sft = 1
        while sft < c:
            b = b + jnp.where(rowi >= sft, pltpu.roll(b, sft, 0), 0.0)
            sft *= 2
        b_last = b[c - 1:c]
        kd = (key * jnp.exp(b_last - b)).astype(BF16)
        upd = lax.dot_general(val.astype(BF16), kd, (((0,), (0,)), ((), ())),
                              preferred_element_type=F32)
        es = [(qf * key).astype(BF16)]
        vrs = [val]
        for d in range(1, c):
            kr = pltpu.roll(key, d, 0)
            br = pltpu.roll(b, d, 0)
            e = jnp.where(rowi >= d, qf * kr * jnp.exp(b - br), 0.0)
            es.append(e.astype(BF16))
            vrs.append(pltpu.roll(val, d, 0))
        estack = jnp.concatenate(es, axis=0)
        r = jnp.dot(estack, ones_bd, preferred_element_type=F32)
        o = r[0:c] * vrs[0]
        for d in range(1, c):
            o = o + r[d * c:(d + 1) * c] * vrs[d]
        qd = (qf * jnp.exp(b)).astype(BF16)
        o = o + lax.dot_general(qd, stb_scr[...], (((1,), (1,)), ((), ())),
                                preferred_element_type=F32)
        o_scr[sl, :] = o
        decay = jnp.exp(b_last)
        for hd in range(HG_HEADS):
            rws = slice(hd * HG_DIM, (hd + 1) * HG_DIM)
            lns = slice(hd * HG_DIM // LANES * LANES, (hd * HG_DIM // LANES + 1) * LANES)
            new = decay[:, lns] * st_scr[rws, lns] + jnp.where(same_head[rws, lns], upd[rws, lns], 0.0)
            st_scr[rws, lns] = new
            stb_scr[rws, lns] = new.astype(BF16)
        return carry

    lax.fori_loop(0, T // c, body, 0, unroll=HG_UNROLL)

    o = o_scr[...]
    sq = o * o
    sq_hi = sq.astype(BF16)
    sq_lo = (sq - sq_hi.astype(F32)).astype(BF16)
    ms = (jnp.dot(sq_hi, ones_bd, preferred_element_type=F32)
          + jnp.dot(sq_lo, ones_bd, preferred_element_type=F32)) * (1.0 / HG_DIM)
    gate = hg_ref[:, 3 * W:4 * W]
    o_ref[...] = (o * lax.rsqrt(ms + EPS) * g_ref[...] * _silu(gate)).astype(BF16)


def _hgrn2(hg, lb_logits, norm_g, layer):
    S = hg.shape[0]
    T = HG_BLOCK
    g_t = jnp.tile(norm_g.reshape(1, HG_DIM), (1, HG_HEADS))
    return pl.pallas_call(
        functools.partial(_hgrn_kernel, layer=layer),
        grid=(S // T,),
        in_specs=[
            pl.BlockSpec((DEPTH, HG_WIDTH), lambda i: (0, 0)),
            pl.BlockSpec((1, HG_WIDTH), lambda i: (0, 0)),
            pl.BlockSpec((T, 4 * HG_WIDTH), lambda i: (i, 0)),
        ],
        out_specs=pl.BlockSpec((T, HG_WIDTH), lambda i: (i, 0)),
        out_shape=jax.ShapeDtypeStruct((S, HG_WIDTH), BF16),
        scratch_shapes=[
            pltpu.VMEM((HG_WIDTH, HG_WIDTH), F32),
            pltpu.VMEM((HG_WIDTH, HG_WIDTH), BF16),
            pltpu.VMEM((T, HG_WIDTH), F32),
        ],
        compiler_params=_params(("arbitrary",)),
        name="hgrn2",
    )(lb_logits, g_t, hg)


def _pool_tile(u, w_bd, scale, halo_scr, blk):
    T = u.shape[0]
    H = POOL_HALO

    @pl.when(blk == 0)
    def _():
        halo_scr[...] = jnp.zeros(halo_scr.shape, F32)

    ext = jnp.concatenate([halo_scr[...], u], axis=0)
    halo_scr[...] = u[T - H:, :]
    sums = []
    s = ext
    w = 1
    while w < POOL_WINDOWS[-1]:
        s = s + pltpu.roll(s, w, 0)
        w *= 2
        sums.append(s[H:, :])
    t1 = (blk * T + 1 + lax.broadcasted_iota(jnp.int32, (T, POOL_WIDTH), 0)).astype(F32)
    lane = lax.broadcasted_iota(jnp.int32, (T, POOL_WIDTH), 1)
    pooled = None
    for gi, win in enumerate(POOL_WINDOWS):
        mean = sums[gi] / jnp.minimum(t1, float(win))
        pooled = mean if pooled is None else jnp.where(lane >= gi * POOL_GROUP, mean, pooled)
    pooled = pooled - u
    y = jnp.dot(pooled.astype(BF16), w_bd, preferred_element_type=F32)
    return (y * scale).astype(BF16)


def _outproj_kernel(a_ref, r_ref, p_ref, x_ref, wf_ref, g1_ref, lg_ref, lb_ref, o_ref, w_ref):
    @pl.when(pl.program_id(0) == 0)
    def _():
        w_ref[...] = wf_ref[0].astype(BF16)

    m = jnp.dot(a_ref[...], w_ref[0:DA_WIDTH, :], preferred_element_type=F32)
    m += jnp.dot(r_ref[...], w_ref[DA_WIDTH:DA_WIDTH + HG_WIDTH, :], preferred_element_type=F32)
    m += jnp.dot(p_ref[...], w_ref[DA_WIDTH + HG_WIDTH:, :], preferred_element_type=F32)
    y = ALPHA * x_ref[...] + (1.0 + g1_ref[...]) * m
    o_ref[...] = _layer_norm(y, lg_ref[...], lb_ref[...])


def _out_projection(a, r, p, x, w_out, layer, g1, ln_g, ln_b):
    S = x.shape[0]
    tm = ROW_TILE
    row = lambda i: (i, 0)
    fixed = lambda i: (0, 0)
    vec = pl.BlockSpec((1, D_MODEL), fixed)
    return pl.pallas_call(
        _outproj_kernel,
        grid=(S // tm,),
        in_specs=[
            pl.BlockSpec((tm, DA_WIDTH), row),
            pl.BlockSpec((tm, HG_WIDTH), row),
            pl.BlockSpec((tm, POOL_WIDTH), row),
            pl.BlockSpec((tm, D_MODEL), row),
            pl.BlockSpec((1, D_MIX, D_MODEL), lambda i: (layer, 0, 0), pipeline_mode=pl.Buffered(1)),
            vec, vec, vec,
        ],
        out_specs=pl.BlockSpec((tm, D_MODEL), row),
        out_shape=jax.ShapeDtypeStruct((S, D_MODEL), F32),
        scratch_shapes=[pltpu.VMEM((D_MIX, D_MODEL), BF16)],
        compiler_params=_params(("arbitrary",)),
        name="out_proj_ln",
    )(a, r, p, x, w_out, g1, ln_g.reshape(1, D_MODEL), ln_b.reshape(1, D_MODEL))


def _ffn_kernel(x_ref, sc_ref, sh_ref, g2_ref, w1_ref, w3_ref, w2_ref, lg_ref, lb_ref, *rest):
    n_cast = (len(rest) - 1) // 2
    o_ref = rest[n_cast]
    for src, dst in zip(rest[:n_cast], rest[n_cast + 1:]):
        dst[...] = src[...].astype(BF16)
    x = x_ref[...]
    h = (x * (1.0 + sc_ref[...]) + sh_ref[...]).astype(BF16)
    acc = jnp.zeros(x.shape, F32)
    step = 4 * MXU_DIM
    for lo in range(0, FFN_DIM, step):
        hi = min(lo + step, FFN_DIM)
        a = jnp.dot(h, w1_ref[:, lo:hi], preferred_element_type=F32)
        b = jnp.dot(h, w3_ref[:, lo:hi], preferred_element_type=F32)
        acc += jnp.dot((_silu(a) * b).astype(BF16), w2_ref[lo:hi, :], preferred_element_type=F32)
    y = ALPHA * x + (1.0 + g2_ref[...]) * acc
    o_ref[...] = _layer_norm(y, lg_ref[...], lb_ref[...])


def _dense_ffn(x, sc, sh, g2, w1, w3, w2, ln_g, ln_b, to_cast=()):
    S = x.shape[0]
    tm = FFN_CAST_TILE if to_cast else ROW_TILE
    steps = S // tm
    row = lambda i: (i, 0)
    fixed = lambda i: (0, 0)
    vec = pl.BlockSpec((1, D_MODEL), fixed)
    once = pl.Buffered(1)
    slabs = [a.reshape(-1, a.shape[-1]) for a in to_cast]
    slab_specs = [pl.BlockSpec((a.shape[0] // steps, a.shape[1]), row) for a in slabs]
    for a in slabs:
        assert a.shape[0] % (steps * BF16_ROWS) == 0
    outs = pl.pallas_call(
        _ffn_kernel,
        grid=(steps,),
        in_specs=[
            pl.BlockSpec((tm, D_MODEL), row),
            vec, vec, vec,
            pl.BlockSpec((D_MODEL, FFN_DIM), fixed, pipeline_mode=once),
            pl.BlockSpec((D_MODEL, FFN_DIM), fixed, pipeline_mode=once),
            pl.BlockSpec((FFN_DIM, D_MODEL), fixed, pipeline_mode=once),
            vec, vec,
        ] + slab_specs,
        out_specs=[pl.BlockSpec((tm, D_MODEL), row)] + slab_specs,
        out_shape=[jax.ShapeDtypeStruct((S, D_MODEL), F32)]
        + [jax.ShapeDtypeStruct(a.shape, BF16) for a in slabs],
        compiler_params=_params(("parallel",)),
        name="dense_ffn_ln",
    )(x, sc, sh, g2, w1, w3, w2, ln_g.reshape(1, D_MODEL), ln_b.reshape(1, D_MODEL), *slabs)
    return outs[0], [o.reshape(a.shape) for o, a in zip(outs[1:], to_cast)]


def _router_kernel(x_ref, sc_ref, sh_ref, rw_ref, h_ref, rank_ref, comb_ref, cum_ref):
    T = x_ref.shape[0]
    tc = MOE_CHUNK
    E = N_EXPERTS
    h = x_ref[...] * (1.0 + sc_ref[...]) + sh_ref[...]
    h_ref[...] = h.astype(BF16)
    logits = lax.dot_general(rw_ref[...], h, (((1,), (1,)), ((), ())),
                             preferred_element_type=F32, precision=HIGHEST)
    eidx = lax.broadcasted_iota(jnp.int32, (E, T), 0)
    v1 = jnp.max(logits, axis=0, keepdims=True)
    i1 = jnp.min(jnp.where(logits == v1, eidx, E), axis=0, keepdims=True)
    m1 = eidx == i1
    rest = jnp.where(m1, -jnp.inf, logits)
    v2 = jnp.max(rest, axis=0, keepdims=True)
    i2 = jnp.min(jnp.where(rest == v2, eidx, E), axis=0, keepdims=True)
    m2 = eidx == i2
    e2 = jnp.exp(v2 - v1)
    g1 = 1.0 / (1.0 + e2)
    g2 = e2 / (1.0 + e2)
    comb = jnp.where(m1, g1, 0.0) + jnp.where(m2, g2, 0.0)
    routed = jnp.logical_or(m1, m2)
    ind = routed.astype(BF16)
    r_i = lax.broadcasted_iota(jnp.int32, (tc, tc), 0)
    c_i = lax.broadcasted_iota(jnp.int32, (tc, tc), 1)
    strict = (r_i < c_i).astype(BF16)
    lane = lax.broadcasted_iota(jnp.int32, (E, LANES), 1)
    running = jnp.zeros((E, 1), F32)
    cum = jnp.zeros((E, LANES), F32)
    for ci in range(T // tc):
        sl = slice(ci * tc, (ci + 1) * tc)
        ind_c = ind[:, sl]
        rank_c = jnp.dot(ind_c, strict, preferred_element_type=F32) + running
        rank_ref[:, ci] = jnp.where(routed[:, sl], rank_c, -1.0).reshape(E, 1, tc)
        comb_ref[:, ci] = comb[:, sl].reshape(E, 1, tc)
        cum = jnp.where(lane == ci, running, cum)
        running = running + jnp.sum(ind_c.astype(F32), axis=1, keepdims=True)
    cum = jnp.where(lane == T // tc, running, cum)
    cum_ref[0] = cum.astype(jnp.int32)


def _router(x, sc, sh, router_w):
    S = x.shape[0]
    T = MOE_BLOCK
    E = N_EXPERTS
    fixed = lambda i: (0, 0)
    vec = pl.BlockSpec((1, D_MODEL), fixed)
    return pl.pallas_call(
        _router_kernel,
        grid=(S // T,),
        in_specs=[
            pl.BlockSpec((T, D_MODEL), lambda i: (i, 0)),
            vec, vec,
            pl.BlockSpec((E, D_MODEL), fixed),
        ],
        out_specs=[
            pl.BlockSpec((T, D_MODEL), lambda i: (i, 0)),
            pl.BlockSpec((E, T // MOE_CHUNK, 1, MOE_CHUNK), lambda i: (0, i, 0, 0)),
            pl.BlockSpec((E, T // MOE_CHUNK, 1, MOE_CHUNK), lambda i: (0, i, 0, 0)),
            pl.BlockSpec((1, E, LANES), lambda i: (i, 0, 0)),
        ],
        out_shape=[
            jax.ShapeDtypeStruct((S, D_MODEL), BF16),
            jax.ShapeDtypeStruct((E, S // MOE_CHUNK, 1, MOE_CHUNK), F32),
            jax.ShapeDtypeStruct((E, S // MOE_CHUNK, 1, MOE_CHUNK), F32),
            jax.ShapeDtypeStruct((S // T, E, LANES), jnp.int32),
        ],
        compiler_params=_params(("parallel",)),
        name="moe_router",
    )(x, sc, sh, router_w.T)


def _moe_kernel(cum_ref, h_ref, rank_ref, comb_ref, w1_ref, w3_ref, w2_ref, y_ref, xe_scr, gate_scr,
                out_scr):
    T = h_ref.shape[0]
    tr, tc, win = MOE_TILE, MOE_CHUNK, MOE_WINDOW
    nchunk = T // tc
    b = pl.program_id(0)
    e = pl.program_id(1)
    f = pl.program_id(2)

    @pl.when(jnp.logical_and(e == 0, f == 0))
    def _():
        y_ref[...] = jnp.zeros(y_ref.shape, F32)

    base = (b * N_EXPERTS + e) * LANES
    count = cum_ref[base + nchunk]

    def token_row(ref, first, n):
        blk = ref[0, pl.ds(first, n)]
        return jnp.concatenate([blk[c] for c in range(n)], axis=1)

    def process(lo, tr):
        rows = pl.ds(pl.multiple_of(lo, BF16_ROWS), tr)

        def hits(first, n):
            want = (lo + lax.broadcasted_iota(jnp.int32, (tr, n * tc), 0)).astype(F32)
            return token_row(rank_ref, first, n) == want

        def token_span(first, n):
            start = first * tc
            if not isinstance(start, int):
                start = pl.multiple_of(start, tc)
            return pl.ds(start, n * tc)

        c_lo = jnp.int32(0)
        c_hi = jnp.int32(-1)
        for c in range(nchunk):
            c_lo += (cum_ref[base + c + 1] <= lo).astype(jnp.int32)
            c_hi += (cum_ref[base + c] < lo + tr).astype(jnp.int32)
        fits = c_hi - c_lo < win
        window = (jnp.minimum(c_lo, nchunk - win), win)
        block = (0, nchunk)

        def gather(first, n):
            hit = hits(first, n)
            xe_scr[rows, :] = jnp.dot(hit.astype(BF16), h_ref[token_span(first, n), :],
                                      preferred_element_type=F32).astype(BF16)
            gate_scr[rows, :] = jnp.sum(jnp.where(hit, token_row(comb_ref, first, n), 0.0),
                                        axis=1, keepdims=True)

        @pl.when(jnp.logical_and(f == 0, fits))
        def _():
            gather(*window)

        @pl.when(jnp.logical_and(f == 0, jnp.logical_not(fits)))
        def _():
            gather(*block)

        xe = xe_scr[rows, :]
        a = jnp.dot(xe, w1_ref[0], preferred_element_type=F32)
        g = jnp.dot(xe, w3_ref[0], preferred_element_type=F32)
        mid = (_silu(a) * g).astype(BF16)
        part = jnp.dot(mid, w2_ref[0], preferred_element_type=F32)

        @pl.when(f == 0)
        def _():
            out_scr[rows, :] = part

        if MOE_FSPLIT > 2:
            @pl.when(jnp.logical_and(f > 0, f < MOE_FSPLIT - 1))
            def _():
                out_scr[rows, :] += part

        @pl.when(f == MOE_FSPLIT - 1)
        def _():
            out_bf = ((out_scr[rows, :] + part) * gate_scr[rows, :]).astype(BF16)

            def scatter(first, n):
                y_ref[token_span(first, n), :] += lax.dot_general(
                    hits(first, n).astype(BF16), out_bf, (((0,), (0,)), ((), ())),
                    preferred_element_type=F32)

            @pl.when(fits)
            def _():
                scatter(*window)

            @pl.when(jnp.logical_not(fits))
            def _():
                scatter(*block)

    full = count // tr
    rem = count - full * tr

    def tile(r, carry):
        process(r * tr, tr)
        return carry

    lax.fori_loop(0, full, tile, 0)

    @pl.when(jnp.logical_and(rem > 0, rem <= MOE_TAIL_TILE))
    def _():
        process(full * tr, MOE_TAIL_TILE)

    @pl.when(rem > MOE_TAIL_TILE)
    def _():
        process(full * tr, tr)


def _moe_ffn(h_bf, rank, comb, cum, w1, w3, w2):
    S = h_bf.shape[0]
    T = MOE_BLOCK
    fw = EXPERT_DIM // MOE_FSPLIT
    grid_spec = pltpu.PrefetchScalarGridSpec(
        num_scalar_prefetch=1,
        grid=(S // T, N_EXPERTS, MOE_FSPLIT),
        in_specs=[
            pl.BlockSpec((T, D_MODEL), lambda b, e, f, c: (b, 0), pipeline_mode=pl.Buffered(1)),
            pl.BlockSpec((1, T // MOE_CHUNK, 1, MOE_CHUNK), lambda b, e, f, c: (e, b, 0, 0)),
            pl.BlockSpec((1, T // MOE_CHUNK, 1, MOE_CHUNK), lambda b, e, f, c: (e, b, 0, 0)),
            pl.BlockSpec((1, D_MODEL, fw), lambda b, e, f, c: (e, 0, f)),
            pl.BlockSpec((1, D_MODEL, fw), lambda b, e, f, c: (e, 0, f)),
            pl.BlockSpec((1, fw, D_MODEL), lambda b, e, f, c: (e, f, 0)),
        ],
        out_specs=pl.BlockSpec((T, D_MODEL), lambda b, e, f, c: (b, 0)),
        scratch_shapes=[
            pltpu.VMEM((T + MOE_TILE, D_MODEL), BF16),
            pltpu.VMEM((T + MOE_TILE, 1), F32),
            pltpu.VMEM((T + MOE_TILE, D_MODEL), F32),
        ],
    )
    return pl.pallas_call(
        _moe_kernel,
        grid_spec=grid_spec,
        out_shape=jax.ShapeDtypeStruct((S, D_MODEL), F32),
        compiler_params=_params(("arbitrary", "arbitrary", "arbitrary")),
        name="moe_experts",
    )(cum.reshape(-1), h_bf, rank, comb, w1, w3, w2)


def _ln_kernel(x_ref, y_ref, g2_ref, lg_ref, lb_ref, o_ref):
    y = ALPHA * x_ref[...] + (1.0 + g2_ref[...]) * y_ref[...]
    o_ref[...] = _layer_norm(y, lg_ref[...], lb_ref[...])


def _residual_ln(x, y, g2, ln_g, ln_b):
    S = x.shape[0]
    tm = ROW_TILE
    row = lambda i: (i, 0)
    vec = pl.BlockSpec((1, D_MODEL), lambda i: (0, 0))
    return pl.pallas_call(
        _ln_kernel,
        grid=(S // tm,),
        in_specs=[pl.BlockSpec((tm, D_MODEL), row), pl.BlockSpec((tm, D_MODEL), row), vec, vec, vec],
        out_specs=pl.BlockSpec((tm, D_MODEL), row),
        out_shape=jax.ShapeDtypeStruct((S, D_MODEL), F32),
        compiler_params=_params(("parallel",)),
        name="residual_ln",
    )(x, y, g2, ln_g.reshape(1, D_MODEL), ln_b.reshape(1, D_MODEL))


def _rope_tables(S):
    d = DA_HEAD_DIM
    tm = ATTN_BLOCK
    inv_freq = 1.0 / (ROPE_THETA ** (jnp.arange(0, d, 2, dtype=F32) / d))
    inv_lane = inv_freq[jnp.arange(LANES) % (d // 2)][None, :]
    base = (jnp.arange(S // tm, dtype=F32) * tm)[:, None] * inv_lane
    offs = jnp.arange(tm, dtype=F32)[:, None] * inv_lane
    shape_b = (S // tm, 1, LANES)
    return jnp.cos(base).reshape(shape_b), jnp.sin(base).reshape(shape_b), jnp.cos(offs), jnp.sin(offs)


def kernel(x, c, w_ada, b_ada, w_in, lam_qk, attn_norm_g, hg_lb_logits, hg_norm_g, pool_w, pool_scale, w_out, ln1_g, ln1_b, ln2_g, ln2_b, ffn_w1, ffn_w3, ffn_w2, router_w, exp_w1, exp_w3, exp_w2):
    B, S, D = x.shape
    assert B == 1 and D == D_MODEL and S % MOE_BLOCK == 0 and S % ATTN_Q_BLOCK == 0
    xs = x.reshape(S, D)
    mod = _modulation(c, w_ada, b_ada)
    rope = _rope_tables(S)
    experts_bf = {}
    for l in range(DEPTH):
        sh1, sc1, g1, sh2, sc2, g2 = (mod[l, j] for j in range(6))
        qt, k, vt, hg, p = _in_projection(xs, sc1, sh1, w_in, l, rope, pool_w[l], pool_scale[l])
        a = _diff_attention(qt, k, vt, lam_qk[l], attn_norm_g[l], l)
        r = _hgrn2(hg, hg_lb_logits, hg_norm_g[l], l)
        xs = _out_projection(a, r, p, xs, w_out, l, g1, ln1_g[l], ln1_b[l])
        if l % 2 == 0:
            i = l // 2
            ride = (exp_w1[i], exp_w3[i], exp_w2[i]) if l + 1 < DEPTH else ()
            xs, cast = _dense_ffn(xs, sc2, sh2, g2, ffn_w1[i].astype(BF16), ffn_w3[i].astype(BF16),
                                  ffn_w2[i].astype(BF16), ln2_g[l], ln2_b[l], ride)
            if ride:
                experts_bf[i] = cast
        else:
            i = l // 2
            h_bf, rank, comb, cum = _router(xs, sc2, sh2, router_w[i])
            w1, w3, w2 = experts_bf.get(i) or [w[i].astype(BF16) for w in (exp_w1, exp_w3, exp_w2)]
            y = _moe_ffn(h_bf, rank, comb, cum, w1, w3, w2)
            xs = _residual_ln(xs, y, g2, ln2_g[l], ln2_b[l])
    return xs.reshape(B, S, D)
```

```python
import functools
import math

import jax
import jax.numpy as jnp
from jax import lax
from jax.experimental import pallas as pl
from jax.experimental.pallas import tpu as pltpu

F32 = jnp.float32
BF16 = jnp.bfloat16
HIGHEST = lax.Precision.HIGHEST

D_MODEL = 1024
DEPTH = 2
DA_HEADS = 4
DA_WIDTH = D_MODEL // 2
DA_HEAD_DIM = DA_WIDTH // (2 * DA_HEADS)
DA_VAL_DIM = 2 * DA_HEAD_DIM
HG_HEADS = 4
HG_WIDTH = D_MODEL // 4
HG_DIM = HG_WIDTH // HG_HEADS
POOL_WINDOWS = (2, 4, 8, 16)
POOL_WIDTH = D_MODEL // 4
POOL_GROUP = POOL_WIDTH // len(POOL_WINDOWS)
D_MIX = DA_WIDTH + HG_WIDTH + POOL_WIDTH
D_IN = 3 * DA_WIDTH + 4 * HG_WIDTH + POOL_WIDTH
ROPE_THETA = 10000.0
MASK_VALUE = -1e30
TINY = 1e-30
FFN_DIM = 11 * D_MODEL // 4
N_EXPERTS = 8
EXPERT_DIM = 7 * D_MODEL // 2
ALPHA = (2 * DEPTH) ** 0.25
EPS = 1e-5
LOG2_E = math.log2(math.e)

LANES = 128
BF16_ROWS = 16
MXU_DIM = 256
VMEM_LIMIT = 60 * 1024 * 1024

ROW_TILE = 1024
FFN_CAST_TILE = 256
ATTN_BLOCK = 512
ATTN_Q_BLOCK = 2048
ATTN_QUERY_GROUP = 256
ATTN_UNROLL = 4
ATTN_SUM_ROWS = BF16_ROWS
ATTN_VT_ROWS = DA_VAL_DIM + ATTN_SUM_ROWS
HG_BLOCK = 512
HG_CHUNK = 16
HG_UNROLL = 4
POOL_HALO = 16
MOE_BLOCK = 2048
MOE_TILE = 192
MOE_TAIL_TILE = 128
MOE_CHUNK = 128
MOE_WINDOW = 8
MOE_FSPLIT = 2


def _params(sem, vmem=VMEM_LIMIT):
    return pltpu.CompilerParams(dimension_semantics=sem, vmem_limit_bytes=vmem)


def _layer_norm(y, g, b):
    mu = jnp.mean(y, axis=-1, keepdims=True)
    d = y - mu
    var = jnp.mean(d * d, axis=-1, keepdims=True)
    return d * lax.rsqrt(var + EPS) * g + b


def _silu(x):
    return x * jax.nn.sigmoid(x)


def _head_block_mask(n, group):
    r = lax.broadcasted_iota(jnp.int32, (n, n), 0) // group
    c = lax.broadcasted_iota(jnp.int32, (n, n), 1) // group
    return r == c


def _mod_kernel(c_ref, w_ref, b_ref, o_ref):
    cond = _silu(c_ref[...])
    cond8 = jnp.broadcast_to(cond, (8, D_MODEL))
    r = jnp.dot(cond8, w_ref[0], preferred_element_type=F32, precision=HIGHEST)
    o_ref[0] = r[0:1] + b_ref[0]


def _modulation(c, w_ada, b_ada):
    parts = 2
    wide = 6 * D_MODEL // parts
    n = parts * DEPTH
    b3 = b_ada.reshape(n, 1, wide)
    out = pl.pallas_call(
        _mod_kernel,
        grid=(DEPTH, parts),
        in_specs=[
            pl.BlockSpec((1, D_MODEL), lambda l, j: (0, 0)),
            pl.BlockSpec((1, D_MODEL, wide), lambda l, j: (l, 0, j)),
            pl.BlockSpec((1, 1, wide), lambda l, j: (l * parts + j, 0, 0)),
        ],
        out_specs=pl.BlockSpec((1, 1, wide), lambda l, j: (l * parts + j, 0, 0)),
        out_shape=jax.ShapeDtypeStruct((n, 1, wide), F32),
        compiler_params=_params(("parallel", "parallel")),
        name="adaln_mod",
    )(c, w_ada, b3)
    return out.reshape(DEPTH, 6, 1, D_MODEL)


def _inproj_kernel(x_ref, sc_ref, sh_ref, wf_ref, cosb_ref, sinb_ref, cosr_ref, sinr_ref, pw_ref, ps_ref,
                   qt_ref, k_ref, vt_ref, hg_ref, p_ref, w_ref, wvt_ref, halo_scr):
    @pl.when(pl.program_id(0) == 0)
    def _():
        w_ref[...] = wf_ref[0].astype(BF16)
        wvt_ref[...] = wf_ref[0, :, 2 * DA_WIDTH:3 * DA_WIDTH].T.astype(BF16)

    h = (x_ref[...] * (1.0 + sc_ref[...]) + sh_ref[...]).astype(BF16)
    qk = jnp.dot(h, w_ref[:, : 2 * DA_WIDTH], preferred_element_type=F32)
    cb, sb, cr, sr = cosb_ref[0], sinb_ref[0], cosr_ref[...], sinr_ref[...]
    cos128 = cb * cr - sb * sr
    sin128 = sb * cr + cb * sr
    half = DA_HEAD_DIM // 2
    lane128 = lax.broadcasted_iota(jnp.int32, cos128.shape, 1)
    sin128 = jnp.where((lane128 & half) == 0, -sin128, sin128)
    reps = 2 * DA_WIDTH // LANES
    cos = jnp.tile(cos128, (1, reps))
    sin = jnp.tile(sin128, (1, reps))
    lane = lax.broadcasted_iota(jnp.int32, qk.shape, 1)
    first_half = (lane & half) == 0
    width = 2 * DA_WIDTH
    partner = jnp.where(first_half, pltpu.roll(qk, width - half, 1), pltpu.roll(qk, half, 1))
    rot = qk * cos + partner * sin
    qt_ref[0] = (rot[:, :DA_WIDTH] * (DA_HEAD_DIM ** -0.5 * LOG2_E)).T.astype(BF16)
    k_ref[...] = rot[:, DA_WIDTH:].astype(BF16)
    o = 2 * DA_WIDTH
    vt = lax.dot_general(wvt_ref[...], h, (((1,), (1,)), ((), ())),
                         preferred_element_type=F32).astype(BF16)
    ones = jnp.ones((ATTN_SUM_ROWS, vt.shape[1]), BF16)
    pieces = []
    for hd in range(DA_HEADS):
        pieces += [vt[hd * DA_VAL_DIM:(hd + 1) * DA_VAL_DIM], ones]
    vt_ref[0] = jnp.concatenate(pieces, axis=0)
    o += DA_WIDTH
    hg_ref[...] = jnp.dot(h, w_ref[:, o:o + 4 * HG_WIDTH], preferred_element_type=F32)
    o += 4 * HG_WIDTH
    pu = jnp.dot(h, w_ref[:, o:o + POOL_WIDTH], preferred_element_type=F32)
    p_ref[...] = _pool_tile(pu, pw_ref[...], ps_ref[...], halo_scr, pl.program_id(0))


def _in_projection(x, sc, sh, w_in, layer, rope, pool_w, pool_scale):
    S = x.shape[0]
    tm = ATTN_BLOCK
    row = lambda i: (i, 0)
    fixed = lambda i: (0, 0)
    cos_b, sin_b, cos_r, sin_r = rope
    w_bd = jax.scipy.linalg.block_diag(*[pool_w[g] for g in range(len(POOL_WINDOWS))]).astype(BF16)
    return pl.pallas_call(
        _inproj_kernel,
        grid=(S // tm,),
        in_specs=[
            pl.BlockSpec((tm, D_MODEL), row),
            pl.BlockSpec((1, D_MODEL), fixed),
            pl.BlockSpec((1, D_MODEL), fixed),
            pl.BlockSpec((1, D_MODEL, D_IN), lambda i: (layer, 0, 0), pipeline_mode=pl.Buffered(1)),
            pl.BlockSpec((1, 1, LANES), lambda i: (i, 0, 0)),
            pl.BlockSpec((1, 1, LANES), lambda i: (i, 0, 0)),
            pl.BlockSpec((tm, LANES), fixed),
            pl.BlockSpec((tm, LANES), fixed),
            pl.BlockSpec((POOL_WIDTH, POOL_WIDTH), fixed),
            pl.BlockSpec((1, POOL_WIDTH), fixed),
        ],
        out_specs=[
            pl.BlockSpec((1, DA_WIDTH, tm), lambda i: (i, 0, 0)),
            pl.BlockSpec((tm, DA_WIDTH), row),
            pl.BlockSpec((1, DA_HEADS * ATTN_VT_ROWS, tm), lambda i: (i, 0, 0)),
            pl.BlockSpec((tm, 4 * HG_WIDTH), row),
            pl.BlockSpec((tm, POOL_WIDTH), row),
        ],
        out_shape=[
            jax.ShapeDtypeStruct((S // tm, DA_WIDTH, tm), BF16),
            jax.ShapeDtypeStruct((S, DA_WIDTH), BF16),
            jax.ShapeDtypeStruct((S // tm, DA_HEADS * ATTN_VT_ROWS, tm), BF16),
            jax.ShapeDtypeStruct((S, 4 * HG_WIDTH), F32),
            jax.ShapeDtypeStruct((S, POOL_WIDTH), BF16),
        ],
        scratch_shapes=[
            pltpu.VMEM((D_MODEL, D_IN), BF16),
            pltpu.VMEM((DA_WIDTH, D_MODEL), BF16),
            pltpu.VMEM((POOL_HALO, POOL_WIDTH), F32),
        ],
        compiler_params=_params(("arbitrary",)),
        name="in_proj",
    )(x, sc, sh, w_in, cos_b, sin_b, cos_r, sin_r, w_bd, pool_scale.reshape(1, POOL_WIDTH))


def _attn_kernel(lam_ref, g_ref, qt_ref, k_ref, vt_ref, o_ref, m_scr, a_scr, c_scr, acc_scr, s_scr,
                 p_scr, *, lam_init):
    bq = qt_ref.shape[0] * qt_ref.shape[2]
    bk = vt_ref.shape[2]
    cg = ATTN_QUERY_GROUP
    i = pl.program_id(1)
    qt = jnp.concatenate([qt_ref[n] for n in range(qt_ref.shape[0])], axis=1)
    rowq = lax.broadcasted_iota(jnp.int32, qt.shape, 0)
    zero = jnp.zeros_like(qt)
    maps = (jnp.where(rowq < DA_HEAD_DIM, qt, zero), jnp.where(rowq >= DA_HEAD_DIM, qt, zero))
    groups = [(a * bq + c0, c0, maps[a][:, c0:c0 + cg]) for a in range(2) for c0 in range(0, bq, cg)]
    m_scr[...] = jnp.full(m_scr.shape, -jnp.inf, F32)
    acc_scr[...] = jnp.zeros(acc_scr.shape, F32)
    hk = bk // 2
    key_i = lax.broadcasted_iota(jnp.int32, (hk, cg), 0)
    qry_i = lax.broadcasted_iota(jnp.int32, (hk, cg), 1)
    ng = len(groups)
    last = ng - 1

    def cols_of(g):
        return slice(groups[g][0], groups[g][0] + cg)

    def ccols_of(sset, g):
        return slice(sset * 2 * bq + groups[g][0], sset * 2 * bq + groups[g][0] + cg)

    def mask_mode(key_offset, g):
        c0 = groups[g][1]
        if key_offset + hk - 1 <= c0:
            return None
        if key_offset > c0 + cg - 1:
            return "skip"
        return key_offset

    def score_stage(hb, sset, g, key_offset):
        kb = k_ref[pl.ds(pl.multiple_of(hb * hk, hk), hk), :]
        s = jnp.dot(kb, groups[g][2], preferred_element_type=F32)
        if key_offset is not None:
            s = jnp.where(key_i + key_offset <= qry_i + groups[g][1], s, MASK_VALUE)
        s_scr[sset * ng + g] = s
        c_scr[:, ccols_of(sset, g)] = jnp.max(s, axis=0, keepdims=True)

    def max_stage(sset, g):
        cols = cols_of(g)
        m_prev = m_scr[:, cols]
        m_new = jnp.maximum(m_prev, c_scr[:, ccols_of(sset, g)])
        a_scr[:, cols] = jnp.exp2(m_prev - m_new)
        m_scr[:, cols] = m_new

    def exp_stage(sset, g, vt_half, carried, defer):
        cols = cols_of(g)
        p = jnp.exp2(s_scr[sset * ng + g] - m_scr[:, cols])
        alpha = a_scr[:, cols]
        pb = p.astype(BF16)
        acc = acc_scr[:, cols]
        if carried is not None:
            acc = acc + carried
        if defer:
            acc_scr[:, cols] = alpha * acc
            p_scr[...] = pb
        else:
            acc_scr[:, cols] = alpha * acc + jnp.dot(vt_half, pb, preferred_element_type=F32)

    def items(key_offset):
        if key_offset is None:
            return [(g, None) for g in range(ng)]
        return [(g, mask_mode(key_offset, g)) for g in range(ng) if mask_mode(key_offset, g) != "skip"]

    def iteration(t, offs, look):
        vt_a = vt_ref[t, :, 0:hk]
        vt_b = vt_ref[t, :, hk:bk]
        carried = jnp.dot(vt_ref[jnp.maximum(t - 1, 0), :, hk:bk], p_scr[...],
                          preferred_element_type=F32)
        a_items = items(offs)
        b_items = items(None if offs is None else offs + hk)
        n_items = [] if look == "stop" else items(look)
        score_stage(2 * t + 1, 1, *b_items[0])
        for n, (g, _) in enumerate(a_items):
            if n + 1 < len(b_items):
                score_stage(2 * t + 1, 1, *b_items[n + 1])
            if n + 1 < len(a_items):
                max_stage(0, a_items[n + 1][0])
            else:
                max_stage(1, b_items[0][0])
            exp_stage(0, g, vt_a, carried if g == last else None, False)
        if n_items:
            score_stage(2 * t + 2, 0, *n_items[0])
        for n, (g, _) in enumerate(b_items):
            if n + 1 < len(n_items):
                score_stage(2 * t + 2, 0, *n_items[n + 1])
            if n + 1 < len(b_items):
                max_stage(1, b_items[n + 1][0])
            elif n_items:
                max_stage(0, n_items[0][0])
            exp_stage(1, g, vt_b, None, bool(n_items) and g == last)

    p_scr[...] = jnp.zeros(p_scr.shape, BF16)
    for g in range(ng):
        score_stage(0, 0, g, (0 - i) * bq)
    max_stage(0, 0)

    per_q = bq // bk
    first = i * per_q
    unroll = ATTN_UNROLL

    def body(u, carry):
        for k in range(unroll):
            iteration(unroll * u + k, None, None)
        return carry

    assert per_q % unroll == 0
    lax.fori_loop(0, jnp.maximum(first // unroll - 1, 0), body, 0)

    @pl.when(i > 0)
    def _():
        for k in range(unroll - 1):
            iteration(first - unroll + k, None, None)
        iteration(first - 1, None, 0)

    for d in range(per_q):
        iteration(first + d, d * bk, (d + 1) * bk if d + 1 < per_q else "stop")

    lq = lam_ref[...]
    lam = (jnp.exp(jnp.sum(lq[0:1] * lq[1:2], axis=1, keepdims=True))
           - jnp.exp(jnp.sum(lq[2:3] * lq[3:4], axis=1, keepdims=True)) + lam_init)
    o = acc_scr[0:DA_VAL_DIM, :] / acc_scr[DA_VAL_DIM:DA_VAL_DIM + 1, :]
    o = o[:, :bq] - lam * o[:, bq:]
    ms = jnp.mean(o * o, axis=0, keepdims=True)
    o = o * lax.rsqrt(ms + EPS) * g_ref[...] * (1.0 - lam_init)
    o_ref[...] = o.T.astype(BF16)


def _diff_attention(qt, k, vt, lam_qk, attn_g, layer):
    S = k.shape[0]
    bk = ATTN_BLOCK
    bq = ATTN_Q_BLOCK
    nk = S // bk
    lam_init = 0.8 - 0.6 * math.exp(-0.3 * layer)
    return pl.pallas_call(
        functools.partial(_attn_kernel, lam_init=lam_init),
        grid=(DA_HEADS, S // bq),
        in_specs=[
            pl.BlockSpec((4, DA_HEAD_DIM), lambda h, i: (0, 0)),
            pl.BlockSpec((DA_VAL_DIM, 1), lambda h, i: (0, 0)),
            pl.BlockSpec((bq // bk, LANES, bk), lambda h, i: (i, h, 0)),
            pl.BlockSpec((S, LANES), lambda h, i: (0, h)),
            pl.BlockSpec((nk, ATTN_VT_ROWS, bk), lambda h, i: (0, h, 0)),
        ],
        out_specs=pl.BlockSpec((bq, DA_VAL_DIM), lambda h, i: (i, h)),
        out_shape=jax.ShapeDtypeStruct((S, DA_WIDTH), BF16),
        scratch_shapes=[
            pltpu.VMEM((1, 2 * bq), F32),
            pltpu.VMEM((1, 2 * bq), F32),
            pltpu.VMEM((1, 2 * 2 * bq), F32),
            pltpu.VMEM((ATTN_VT_ROWS, 2 * bq), F32),
            pltpu.VMEM((2 * (2 * bq // ATTN_QUERY_GROUP), bk // 2, ATTN_QUERY_GROUP), F32),
            pltpu.VMEM((bk // 2, ATTN_QUERY_GROUP), BF16),
        ],
        compiler_params=_params(("parallel", "parallel")),
        name="diff_attn",
    )(lam_qk, attn_g.reshape(DA_VAL_DIM, 1), qt, k, vt)


def _hgrn_kernel(lbl_ref, g_ref, hg_ref, o_ref, st_scr, stb_scr, o_scr, *, layer):
    T = hg_ref.shape[0]
    c = HG_CHUNK
    W = HG_WIDTH

    @pl.when(pl.program_id(0) == 0)
    def _():
        st_scr[...] = jnp.zeros(st_scr.shape, F32)
        stb_scr[...] = jnp.zeros(stb_scr.shape, BF16)

    lg = lbl_ref[...]
    ex = jnp.exp(lg - jnp.max(lg, axis=0, keepdims=True))
    prob = ex / jnp.sum(ex, axis=0, keepdims=True)
    lb = jnp.zeros((1, W), F32)
    for li in range(1, layer + 1):
        lb = lb + prob[li:li + 1]

    same_head = _head_block_mask(W, HG_DIM)
    ones_bd = same_head.astype(BF16)
    rowi = lax.broadcasted_iota(jnp.int32, (c, W), 0)

    def body(s, carry):
        sl = pl.ds(pl.multiple_of(s * c, c), c)
        qf = hg_ref[sl, 0:W]
        z = hg_ref[sl, W:2 * W]
        iv = hg_ref[sl, 2 * W:3 * W]
        f = lb + (1.0 - lb) * jax.nn.sigmoid(z)
        logf = jnp.log(jnp.maximum(f, TINY))
        key = (1.0 - lb) * jax.nn.sigmoid(-z)
        val = _silu(iv)
        b = logf
        sft = 1
        while sft < c:
            b = b + jnp.where(rowi >= sft, pltpu.roll(b, sft, 0), 0.0)
            sft *= 2
        b_last = b[c - 1:c]
        kd = (key * jnp.exp(b_last - b)).astype(BF16)
        upd = lax.dot_general(val.astype(BF16), kd, (((0,), (0,)), ((), ())),
                              preferred_element_type=F32)
        es = [(qf * key).astype(BF16)]
        vrs = [val]
        for d in range(1, c):
            kr = pltpu.roll(key, d, 0)
            br = pltpu.roll(b, d, 0)
            e = jnp.where(rowi >= d, qf * kr * jnp.exp(b - br), 0.0)
            es.append(e.astype(BF16))
            vrs.append(pltpu.roll(val, d, 0))
        estack = jnp.concatenate(es, axis=0)
        r = jnp.dot(estack, ones_bd, preferred_element_type=F32)
        o = r[0:c] * vrs[0]
        for d in range(1, c):
            o = o + r[d * c:(d + 1) * c] * vrs[d]
        qd = (qf * jnp.exp(b)).astype(BF16)
        o = o + lax.dot_general(qd, stb_scr[...], (((1,), (1,)), ((), ())),
                                preferred_element_type=F32)
        o_scr[sl, :] = o
        decay = jnp.exp(b_last)
        for hd in range(HG_HEADS):
            rws = slice(hd * HG_DIM, (hd + 1) * HG_DIM)
            lns = slice(hd * HG_DIM // LANES * LANES, (hd * HG_DIM // LANES + 1) * LANES)
            new = decay[:, lns] * st_scr[rws, lns] + jnp.where(same_head[rws, lns], upd[rws, lns], 0.0)
            st_scr[rws, lns] = new
            stb_scr[rws, lns] = new.astype(BF16)
        return carry

    lax.fori_loop(0, T // c, body, 0, unroll=HG_UNROLL)

    o = o_scr[...]
    sq = o * o
    sq_hi = sq.astype(BF16)
    sq_lo = (sq - sq_hi.astype(F32)).astype(BF16)
    ms = (jnp.dot(sq_hi, ones_bd, preferred_element_type=F32)
          + jnp.dot(sq_lo, ones_bd, preferred_element_type=F32)) * (1.0 / HG_DIM)
    gate = hg_ref[:, 3 * W:4 * W]
    o_ref[...] = (o * lax.rsqrt(ms + EPS) * g_ref[...] * _silu(gate)).astype(BF16)


def _hgrn2(hg, lb_logits, norm_g, layer):
    S = hg.shape[0]
    T = HG_BLOCK
    g_t = jnp.tile(norm_g.reshape(1, HG_DIM), (1, HG_HEADS))
    return pl.pallas_call(
        functools.partial(_hgrn_kernel, layer=layer),
        grid=(S // T,),
        in_specs=[
            pl.BlockSpec((DEPTH, HG_WIDTH), lambda i: (0, 0)),
            pl.BlockSpec((1, HG_WIDTH), lambda i: (0, 0)),
            pl.BlockSpec((T, 4 * HG_WIDTH), lambda i: (i, 0)),
        ],
        out_specs=pl.BlockSpec((T, HG_WIDTH), lambda i: (i, 0)),
        out_shape=jax.ShapeDtypeStruct((S, HG_WIDTH), BF16),
        scratch_shapes=[
            pltpu.VMEM((HG_WIDTH, HG_WIDTH), F32),
            pltpu.VMEM((HG_WIDTH, HG_WIDTH), BF16),
            pltpu.VMEM((T, HG_WIDTH), F32),
        ],
        compiler_params=_params(("arbitrary",)),
        name="hgrn2",
    )(lb_logits, g_t, hg)


def _pool_tile(u, w_bd, scale, halo_scr, blk):
    T = u.shape[0]
    H = POOL_HALO

    @pl.when(blk == 0)
    def _():
        halo_scr[...] = jnp.zeros(halo_scr.shape, F32)

    ext = jnp.concatenate([halo_scr[...], u], axis=0)
    halo_scr[...] = u[T - H:, :]
    sums = []
    s = ext
    w = 1
    while w < POOL_WINDOWS[-1]:
        s = s + pltpu.roll(s, w, 0)
        w *= 2
        sums.append(s[H:, :])
    t1 = (blk * T + 1 + lax.broadcasted_iota(jnp.int32, (T, POOL_WIDTH), 0)).astype(F32)
    lane = lax.broadcasted_iota(jnp.int32, (T, POOL_WIDTH), 1)
    pooled = None
    for gi, win in enumerate(POOL_WINDOWS):
        mean = sums[gi] / jnp.minimum(t1, float(win))
        pooled = mean if pooled is None else jnp.where(lane >= gi * POOL_GROUP, mean, pooled)
    pooled = pooled - u
    y = jnp.dot(pooled.astype(BF16), w_bd, preferred_element_type=F32)
    return (y * scale).astype(BF16)


def _outproj_kernel(a_ref, r_ref, p_ref, x_ref, wf_ref, g1_ref, lg_ref, lb_ref, o_ref, w_ref):
    @pl.when(pl.program_id(0) == 0)
    def _():
        w_ref[...] = wf_ref[0].astype(BF16)

    m = jnp.dot(a_ref[...], w_ref[0:DA_WIDTH, :], preferred_element_type=F32)
    m += jnp.dot(r_ref[...], w_ref[DA_WIDTH:DA_WIDTH + HG_WIDTH, :], preferred_element_type=F32)
    m += jnp.dot(p_ref[...], w_ref[DA_WIDTH + HG_WIDTH:, :], preferred_element_type=F32)
    y = ALPHA * x_ref[...] + (1.0 + g1_ref[...]) * m
    o_ref[...] = _layer_norm(y, lg_ref[...], lb_ref[...])


def _out_projection(a, r, p, x, w_out, layer, g1, ln_g, ln_b):
    S = x.shape[0]
    tm = ROW_TILE
    row = lambda i: (i, 0)
    fixed = lambda i: (0, 0)
    vec = pl.BlockSpec((1, D_MODEL), fixed)
    return pl.pallas_call(
        _outproj_kernel,
        grid=(S // tm,),
        in_specs=[
            pl.BlockSpec((tm, DA_WIDTH), row),
            pl.BlockSpec((tm, HG_WIDTH), row),
            pl.BlockSpec((tm, POOL_WIDTH), row),
            pl.BlockSpec((tm, D_MODEL), row),
            pl.BlockSpec((1, D_MIX, D_MODEL), lambda i: (layer, 0, 0), pipeline_mode=pl.Buffered(1)),
            vec, vec, vec,
        ],
        out_specs=pl.BlockSpec((tm, D_MODEL), row),
        out_shape=jax.ShapeDtypeStruct((S, D_MODEL), F32),
        scratch_shapes=[pltpu.VMEM((D_MIX, D_MODEL), BF16)],
        compiler_params=_params(("arbitrary",)),
        name="out_proj_ln",
    )(a, r, p, x, w_out, g1, ln_g.reshape(1, D_MODEL), ln_b.reshape(1, D_MODEL))


def _ffn_kernel(x_ref, sc_ref, sh_ref, g2_ref, w1_ref, w3_ref, w2_ref, lg_ref, lb_ref, *rest):
    n_cast = (len(rest) - 1) // 2
    o_ref = rest[n_cast]
    for src, dst in zip(rest[:n_cast], rest[n_cast + 1:]):
        dst[...] = src[...].astype(BF16)
    x = x_ref[...]
    h = (x * (1.0 + sc_ref[...]) + sh_ref[...]).astype(BF16)
    acc = jnp.zeros(x.shape, F32)
    step = 4 * MXU_DIM
    for lo in range(0, FFN_DIM, step):
        hi = min(lo + step, FFN_DIM)
        a = jnp.dot(h, w1_ref[:, lo:hi], preferred_element_type=F32)
        b = jnp.dot(h, w3_ref[:, lo:hi], preferred_element_type=F32)
        acc += jnp.dot((_silu(a) * b).astype(BF16), w2_ref[lo:hi, :], preferred_element_type=F32)
    y = ALPHA * x + (1.0 + g2_ref[...]) * acc
    o_ref[...] = _layer_norm(y, lg_ref[...], lb_ref[...])


def _dense_ffn(x, sc, sh, g2, w1, w3, w2, ln_g, ln_b, to_cast=()):
    S = x.shape[0]
    tm = FFN_CAST_TILE if to_cast else ROW_TILE
    steps = S // tm
    row = lambda i: (i, 0)
    fixed = lambda i: (0, 0)
    vec = pl.BlockSpec((1, D_MODEL), fixed)
    once = pl.Buffered(1)
    slabs = [a.reshape(-1, a.shape[-1]) for a in to_cast]
    slab_specs = [pl.BlockSpec((a.shape[0] // steps, a.shape[1]), row) for a in slabs]
    for a in slabs:
        assert a.shape[0] % (steps * BF16_ROWS) == 0
    outs = pl.pallas_call(
        _ffn_kernel,
        grid=(steps,),
        in_specs=[
            pl.BlockSpec((tm, D_MODEL), row),
            vec, vec, vec,
            pl.BlockSpec((D_MODEL, FFN_DIM), fixed, pipeline_mode=once),
            pl.BlockSpec((D_MODEL, FFN_DIM), fixed, pipeline_mode=once),
            pl.BlockSpec((FFN_DIM, D_MODEL), fixed, pipeline_mode=once),
            vec, vec,
        ] + slab_specs,
        out_specs=[pl.BlockSpec((tm, D_MODEL), row)] + slab_specs,
        out_shape=[jax.ShapeDtypeStruct((S, D_MODEL), F32)]
        + [jax.ShapeDtypeStruct(a.shape, BF16) for a in slabs],
        compiler_params=_params(("parallel",)),
        name="dense_ffn_ln",
    )(x, sc, sh, g2, w1, w3, w2, ln_g.reshape(1, D_MODEL), ln_b.reshape(1, D_MODEL), *slabs)
    return outs[0], [o.reshape(a.shape) for o, a in zip(outs[1:], to_cast)]


def _router_kernel(x_ref, sc_ref, sh_ref, rw_ref, h_ref, rank_ref, comb_ref, cum_ref):
    T = x_ref.shape[0]
    tc = MOE_CHUNK
    E = N_EXPERTS
    h = x_ref[...] * (1.0 + sc_ref[...]) + sh_ref[...]
    h_ref[...] = h.astype(BF16)
    logits = lax.dot_general(rw_ref[...], h, (((1,), (1,)), ((), ())),
                             preferred_element_type=F32, precision=HIGHEST)
    eidx = lax.broadcasted_iota(jnp.int32, (E, T), 0)
    v1 = jnp.max(logits, axis=0, keepdims=True)
    i1 = jnp.min(jnp.where(logits == v1, eidx, E), axis=0, keepdims=True)
    m1 = eidx == i1
    rest = jnp.where(m1, -jnp.inf, logits)
    v2 = jnp.max(rest, axis=0, keepdims=True)
    i2 = jnp.min(jnp.where(rest == v2, eidx, E), axis=0, keepdims=True)
    m2 = eidx == i2
    e2 = jnp.exp(v2 - v1)
    g1 = 1.0 / (1.0 + e2)
    g2 = e2 / (1.0 + e2)
    comb = jnp.where(m1, g1, 0.0) + jnp.where(m2, g2, 0.0)
    routed = jnp.logical_or(m1, m2)
    ind = routed.astype(BF16)
    r_i = lax.broadcasted_iota(jnp.int32, (tc, tc), 0)
    c_i = lax.broadcasted_iota(jnp.int32, (tc, tc), 1)
    strict = (r_i < c_i).astype(BF16)
    lane = lax.broadcasted_iota(jnp.int32, (E, LANES), 1)
    running = jnp.zeros((E, 1), F32)
    cum = jnp.zeros((E, LANES), F32)
    for ci in range(T // tc):
        sl = slice(ci * tc, (ci + 1) * tc)
        ind_c = ind[:, sl]
        rank_c = jnp.dot(ind_c, strict, preferred_element_type=F32) + running
        rank_ref[:, ci] = jnp.where(routed[:, sl], rank_c, -1.0).reshape(E, 1, tc)
        comb_ref[:, ci] = comb[:, sl].reshape(E, 1, tc)
        cum = jnp.where(lane == ci, running, cum)
        running = running + jnp.sum(ind_c.astype(F32), axis=1, keepdims=True)
    cum = jnp.where(lane == T // tc, running, cum)
    cum_ref[0] = cum.astype(jnp.int32)


def _router(x, sc, sh, router_w):
    S = x.shape[0]
    T = MOE_BLOCK
    E = N_EXPERTS
    fixed = lambda i: (0, 0)
    vec = pl.BlockSpec((1, D_MODEL), fixed)
    return pl.pallas_call(
        _router_kernel,
        grid=(S // T,),
        in_specs=[
            pl.BlockSpec((T, D_MODEL), lambda i: (i, 0)),
            vec, vec,
            pl.BlockSpec((E, D_MODEL), fixed),
        ],
        out_specs=[
            pl.BlockSpec((T, D_MODEL), lambda i: (i, 0)),
            pl.BlockSpec((E, T // MOE_CHUNK, 1, MOE_CHUNK), lambda i: (0, i, 0, 0)),
            pl.BlockSpec((E, T // MOE_CHUNK, 1, MOE_CHUNK), lambda i: (0, i, 0, 0)),
            pl.BlockSpec((1, E, LANES), lambda i: (i, 0, 0)),
        ],
        out_shape=[
            jax.ShapeDtypeStruct((S, D_MODEL), BF16),
            jax.ShapeDtypeStruct((E, S // MOE_CHUNK, 1, MOE_CHUNK), F32),
            jax.ShapeDtypeStruct((E, S // MOE_CHUNK, 1, MOE_CHUNK), F32),
            jax.ShapeDtypeStruct((S // T, E, LANES), jnp.int32),
        ],
        compiler_params=_params(("parallel",)),
        name="moe_router",
    )(x, sc, sh, router_w.T)


def _moe_kernel(cum_ref, h_ref, rank_ref, comb_ref, w1_ref, w3_ref, w2_ref, y_ref, xe_scr, gate_scr,
                out_scr):
    T = h_ref.shape[0]
    tr, tc, win = MOE_TILE, MOE_CHUNK, MOE_WINDOW
    nchunk = T // tc
    b = pl.program_id(0)
    e = pl.program_id(1)
    f = pl.program_id(2)

    @pl.when(jnp.logical_and(e == 0, f == 0))
    def _():
        y_ref[...] = jnp.zeros(y_ref.shape, F32)

    base = (b * N_EXPERTS + e) * LANES
    count = cum_ref[base + nchunk]

    def token_row(ref, first, n):
        blk = ref[0, pl.ds(first, n)]
        return jnp.concatenate([blk[c] for c in range(n)], axis=1)

    def process(lo, tr):
        rows = pl.ds(pl.multiple_of(lo, BF16_ROWS), tr)

        def hits(first, n):
            want = (lo + lax.broadcasted_iota(jnp.int32, (tr, n * tc), 0)).astype(F32)
            return token_row(rank_ref, first, n) == want

        def token_span(first, n):
            start = first * tc
            if not isinstance(start, int):
                start = pl.multiple_of(start, tc)
            return pl.ds(start, n * tc)

        c_lo = jnp.int32(0)
        c_hi = jnp.int32(-1)
        for c in range(nchunk):
            c_lo += (cum_ref[base + c + 1] <= lo).astype(jnp.int32)
            c_hi += (cum_ref[base + c] < lo + tr).astype(jnp.int32)
        fits = c_hi - c_lo < win
        window = (jnp.minimum(c_lo, nchunk - win), win)
        block = (0, nchunk)

        def gather(first, n):
            hit = hits(first, n)
            xe_scr[rows, :] = jnp.dot(hit.astype(BF16), h_ref[token_span(first, n), :],
                                      preferred_element_type=F32).astype(BF16)
            gate_scr[rows, :] = jnp.sum(jnp.where(hit, token_row(comb_ref, first, n), 0.0),
                                        axis=1, keepdims=True)

        @pl.when(jnp.logical_and(f == 0, fits))
        def _():
            gather(*window)

        @pl.when(jnp.logical_and(f == 0, jnp.logical_not(fits)))
        def _():
            gather(*block)

        xe = xe_scr[rows, :]
        a = jnp.dot(xe, w1_ref[0], preferred_element_type=F32)
        g = jnp.dot(xe, w3_ref[0], preferred_element_type=F32)
        mid = (_silu(a) * g).astype(BF16)
        part = jnp.dot(mid, w2_ref[0], preferred_element_type=F32)

        @pl.when(f == 0)
        def _():
            out_scr[rows, :] = part

        if MOE_FSPLIT > 2:
            @pl.when(jnp.logical_and(f > 0, f < MOE_FSPLIT - 1))
            def _():
                out_scr[rows, :] += part

        @pl.when(f == MOE_FSPLIT - 1)
        def _():
            out_bf = ((out_scr[rows, :] + part) * gate_scr[rows, :]).astype(BF16)

            def scatter(first, n):
                y_ref[token_span(first, n), :] += lax.dot_general(
                    hits(first, n).astype(BF16), out_bf, (((0,), (0,)), ((), ())),
                    preferred_element_type=F32)

            @pl.when(fits)
            def _():
                scatter(*window)

            @pl.when(jnp.logical_not(fits))
            def _():
                scatter(*block)

    full = count // tr
    rem = count - full * tr

    def tile(r, carry):
        process(r * tr, tr)
        return carry

    lax.fori_loop(0, full, tile, 0)

    @pl.when(jnp.logical_and(rem > 0, rem <= MOE_TAIL_TILE))
    def _():
        process(full * tr, MOE_TAIL_TILE)

    @pl.when(rem > MOE_TAIL_TILE)
    def _():
        process(full * tr, tr)


def _moe_ffn(h_bf, rank, comb, cum, w1, w3, w2):
    S = h_bf.shape[0]
    T = MOE_BLOCK
    fw = EXPERT_DIM // MOE_FSPLIT
    grid_spec = pltpu.PrefetchScalarGridSpec(
        num_scalar_prefetch=1,
        grid=(S // T, N_EXPERTS, MOE_FSPLIT),
        in_specs=[
            pl.BlockSpec((T, D_MODEL), lambda b, e, f, c: (b, 0), pipeline_mode=pl.Buffered(1)),
            pl.BlockSpec((1, T // MOE_CHUNK, 1, MOE_CHUNK), lambda b, e, f, c: (e, b, 0, 0)),
            pl.BlockSpec((1, T // MOE_CHUNK, 1, MOE_CHUNK), lambda b, e, f, c: (e, b, 0, 0)),
            pl.BlockSpec((1, D_MODEL, fw), lambda b, e, f, c: (e, 0, f)),
            pl.BlockSpec((1, D_MODEL, fw), lambda b, e, f, c: (e, 0, f)),
            pl.BlockSpec((1, fw, D_MODEL), lambda b, e, f, c: (e, f, 0)),
        ],
        out_specs=pl.BlockSpec((T, D_MODEL), lambda b, e, f, c: (b, 0)),
        scratch_shapes=[
            pltpu.VMEM((T + MOE_TILE, D_MODEL), BF16),
            pltpu.VMEM((T + MOE_TILE, 1), F32),
            pltpu.VMEM((T + MOE_TILE, D_MODEL), F32),
        ],
    )
    return pl.pallas_call(
        _moe_kernel,
        grid_spec=grid_spec,
        out_shape=jax.ShapeDtypeStruct((S, D_MODEL), F32),
        compiler_params=_params(("arbitrary", "arbitrary", "arbitrary")),
        name="moe_experts",
    )(cum.reshape(-1), h_bf, rank, comb, w1, w3, w2)


def _ln_kernel(x_ref, y_ref, g2_ref, lg_ref, lb_ref, o_ref):
    y = ALPHA * x_ref[...] + (1.0 + g2_ref[...]) * y_ref[...]
    o_ref[...] = _layer_norm(y, lg_ref[...], lb_ref[...])


def _residual_ln(x, y, g2, ln_g, ln_b):
    S = x.shape[0]
    tm = ROW_TILE
    row = lambda i: (i, 0)
    vec = pl.BlockSpec((1, D_MODEL), lambda i: (0, 0))
    return pl.pallas_call(
        _ln_kernel,
        grid=(S // tm,),
        in_specs=[pl.BlockSpec((tm, D_MODEL), row), pl.BlockSpec((tm, D_MODEL), row), vec, vec, vec],
        out_specs=pl.BlockSpec((tm, D_MODEL), row),
        out_shape=jax.ShapeDtypeStruct((S, D_MODEL), F32),
        compiler_params=_params(("parallel",)),
        name="residual_ln",
    )(x, y, g2, ln_g.reshape(1, D_MODEL), ln_b.reshape(1, D_MODEL))


def _rope_tables(S):
    d = DA_HEAD_DIM
    tm = ATTN_BLOCK
    inv_freq = 1.0 / (ROPE_THETA ** (jnp.arange(0, d, 2, dtype=F32) / d))
    inv_lane = inv_freq[jnp.arange(LANES) % (d // 2)][None, :]
    base = (jnp.arange(S // tm, dtype=F32) * tm)[:, None] * inv_lane
    offs = jnp.arange(tm, dtype=F32)[:, None] * inv_lane
    shape_b = (S // tm, 1, LANES)
    return jnp.cos(base).reshape(shape_b), jnp.sin(base).reshape(shape_b), jnp.cos(offs), jnp.sin(offs)


def kernel(x, c, w_ada, b_ada, w_in, lam_qk, attn_norm_g, hg_lb_logits, hg_norm_g, pool_w, pool_scale, w_out, ln1_g, ln1_b, ln2_g, ln2_b, ffn_w1, ffn_w3, ffn_w2, router_w, exp_w1, exp_w3, exp_w2):
    B, S, D = x.shape
    assert B == 1 and D == D_MODEL and S % MOE_BLOCK == 0 and S % ATTN_Q_BLOCK == 0
    xs = x.reshape(S, D)
    mod = _modulation(c, w_ada, b_ada)
    rope = _rope_tables(S)
    experts_bf = {}
    for l in range(DEPTH):
        sh1, sc1, g1, sh2, sc2, g2 = (mod[l, j] for j in range(6))
        qt, k, vt, hg, p = _in_projection(xs, sc1, sh1, w_in, l, rope, pool_w[l], pool_scale[l])
        a = _diff_attention(qt, k, vt, lam_qk[l], attn_norm_g[l], l)
        r = _hgrn2(hg, hg_lb_logits, hg_norm_g[l], l)
        xs = _out_projection(a, r, p, xs, w_out, l, g1, ln1_g[l], ln1_b[l])
        if l % 2 == 0:
            i = l // 2
            ride = (exp_w1[i], exp_w3[i], exp_w2[i]) if l + 1 < DEPTH else ()
            xs, cast = _dense_ffn(xs, sc2, sh2, g2, ffn_w1[i].astype(BF16), ffn_w3[i].astype(BF16),
                                  ffn_w2[i].astype(BF16), ln2_g[l], ln2_b[l], ride)
            if ride:
                experts_bf[i] = cast
        else:
            i = l // 2
            h_bf, rank, comb, cum = _router(xs, sc2, sh2, router_w[i])
            w1, w3, w2 = experts_bf.get(i) or [w[i].astype(BF16) for w in (exp_w1, exp_w3, exp_w2)]
            y = _moe_ffn(h_bf, rank, comb, cum, w1, w3, w2)
            xs = _residual_ln(xs, y, g2, ln2_g[l], ln2_b[l])
    return xs.reshape(B, S, D)
```

```python
import functools
import math

import jax
import jax.numpy as jnp
from jax import lax
from jax.experimental import pallas as pl
from jax.experimental.pallas import tpu as pltpu

F32 = jnp.float32
BF16 = jnp.bfloat16
HIGHEST = lax.Precision.HIGHEST

D_MODEL = 1024
DEPTH = 2
DA_HEADS = 4
DA_WIDTH = D_MODEL // 2
DA_HEAD_DIM = DA_WIDTH // (2 * DA_HEADS)
DA_VAL_DIM = 2 * DA_HEAD_DIM
HG_HEADS = 4
HG_WIDTH = D_MODEL // 4
HG_DIM = HG_WIDTH // HG_HEADS
POOL_WINDOWS = (2, 4, 8, 16)
POOL_WIDTH = D_MODEL // 4
POOL_GROUP = POOL_WIDTH // len(POOL_WINDOWS)
D_MIX = DA_WIDTH + HG_WIDTH + POOL_WIDTH
D_IN = 3 * DA_WIDTH + 4 * HG_WIDTH + POOL_WIDTH
ROPE_THETA = 10000.0
MASK_VALUE = -1e30
TINY = 1e-30
FFN_DIM = 11 * D_MODEL // 4
N_EXPERTS = 8
EXPERT_DIM = 7 * D_MODEL // 2
ALPHA = (2 * DEPTH) ** 0.25
EPS = 1e-5
LOG2_E = math.log2(math.e)

LANES = 128
BF16_ROWS = 16
MXU_DIM = 256
VMEM_LIMIT = 60 * 1024 * 1024

ROW_TILE = 1024
FFN_CAST_TILE = 256
ATTN_BLOCK = 512
ATTN_Q_BLOCK = 2048
ATTN_QUERY_GROUP = 256
ATTN_UNROLL = 4
ATTN_SUM_ROWS = BF16_ROWS
ATTN_VT_ROWS = DA_VAL_DIM + ATTN_SUM_ROWS
HG_BLOCK = 512
HG_CHUNK = 16
HG_UNROLL = 4
POOL_HALO = 16
MOE_BLOCK = 2048
MOE_TILE = 192
MOE_TAIL_TILE = 128
MOE_CHUNK = 128
MOE_WINDOW = 8
MOE_FSPLIT = 2


def _params(sem, vmem=VMEM_LIMIT):
    return pltpu.CompilerParams(dimension_semantics=sem, vmem_limit_bytes=vmem)


def _layer_norm(y, g, b):
    mu = jnp.mean(y, axis=-1, keepdims=True)
    d = y - mu
    var = jnp.mean(d * d, axis=-1, keepdims=True)
    return d * lax.rsqrt(var + EPS) * g + b


def _silu(x):
    return x * jax.nn.sigmoid(x)


def _head_block_mask(n, group):
    r = lax.broadcasted_iota(jnp.int32, (n, n), 0) // group
    c = lax.broadcasted_iota(jnp.int32, (n, n), 1) // group
    return r == c


def _mod_kernel(c_ref, w_ref, b_ref, o_ref):
    cond = _silu(c_ref[...])
    cond8 = jnp.broadcast_to(cond, (8, D_MODEL))
    r = jnp.dot(cond8, w_ref[0], preferred_element_type=F32, precision=HIGHEST)
    o_ref[0] = r[0:1] + b_ref[0]


def _modulation(c, w_ada, b_ada):
    parts = 2
    wide = 6 * D_MODEL // parts
    n = parts * DEPTH
    b3 = b_ada.reshape(n, 1, wide)
    out = pl.pallas_call(
        _mod_kernel,
        grid=(DEPTH, parts),
        in_specs=[
            pl.BlockSpec((1, D_MODEL), lambda l, j: (0, 0)),
            pl.BlockSpec((1, D_MODEL, wide), lambda l, j: (l, 0, j)),
            pl.BlockSpec((1, 1, wide), lambda l, j: (l * parts + j, 0, 0)),
        ],
        out_specs=pl.BlockSpec((1, 1, wide), lambda l, j: (l * parts + j, 0, 0)),
        out_shape=jax.ShapeDtypeStruct((n, 1, wide), F32),
        compiler_params=_params(("parallel", "parallel")),
        name="adaln_mod",
    )(c, w_ada, b3)
    return out.reshape(DEPTH, 6, 1, D_MODEL)


def _inproj_kernel(x_ref, sc_ref, sh_ref, wf_ref, cosb_ref, sinb_ref, cosr_ref, sinr_ref, pw_ref, ps_ref,
                   qt_ref, k_ref, vt_ref, hg_ref, p_ref, w_ref, wvt_ref, halo_scr):
    @pl.when(pl.program_id(0) == 0)
    def _():
        w_ref[...] = wf_ref[0].astype(BF16)
        wvt_ref[...] = wf_ref[0, :, 2 * DA_WIDTH:3 * DA_WIDTH].T.astype(BF16)

    h = (x_ref[...] * (1.0 + sc_ref[...]) + sh_ref[...]).astype(BF16)
    qk = jnp.dot(h, w_ref[:, : 2 * DA_WIDTH], preferred_element_type=F32)
    cb, sb, cr, sr = cosb_ref[0], sinb_ref[0], cosr_ref[...], sinr_ref[...]
    cos128 = cb * cr - sb * sr
    sin128 = sb * cr + cb * sr
    half = DA_HEAD_DIM // 2
    lane128 = lax.broadcasted_iota(jnp.int32, cos128.shape, 1)
    sin128 = jnp.where((lane128 & half) == 0, -sin128, sin128)
    reps = 2 * DA_WIDTH // LANES
    cos = jnp.tile(cos128, (1, reps))
    sin = jnp.tile(sin128, (1, reps))
    lane = lax.broadcasted_iota(jnp.int32, qk.shape, 1)
    first_half = (lane & half) == 0
    width = 2 * DA_WIDTH
    partner = jnp.where(first_half, pltpu.roll(qk, width - half, 1), pltpu.roll(qk, half, 1))
    rot = qk * cos + partner * sin
    qt_ref[0] = (rot[:, :DA_WIDTH] * (DA_HEAD_DIM ** -0.5 * LOG2_E)).T.astype(BF16)
    k_ref[...] = rot[:, DA_WIDTH:].astype(BF16)
    o = 2 * DA_WIDTH
    vt = lax.dot_general(wvt_ref[...], h, (((1,), (1,)), ((), ())),
                         preferred_element_type=F32).astype(BF16)
    ones = jnp.ones((ATTN_SUM_ROWS, vt.shape[1]), BF16)
    pieces = []
    for hd in range(DA_HEADS):
        pieces += [vt[hd * DA_VAL_DIM:(hd + 1) * DA_VAL_DIM], ones]
    vt_ref[0] = jnp.concatenate(pieces, axis=0)
    o += DA_WIDTH
    hg_ref[...] = jnp.dot(h, w_ref[:, o:o + 4 * HG_WIDTH], preferred_element_type=F32)
    o += 4 * HG_WIDTH
    pu = jnp.dot(h, w_ref[:, o:o + POOL_WIDTH], preferred_element_type=F32)
    p_ref[...] = _pool_tile(pu, pw_ref[...], ps_ref[...], halo_scr, pl.program_id(0))


def _in_projection(x, sc, sh, w_in, layer, rope, pool_w, pool_scale):
    S = x.shape[0]
    tm = ATTN_BLOCK
    row = lambda i: (i, 0)
    fixed = lambda i: (0, 0)
    cos_b, sin_b, cos_r, sin_r = rope
    w_bd = jax.scipy.linalg.block_diag(*[pool_w[g] for g in range(len(POOL_WINDOWS))]).astype(BF16)
    return pl.pallas_call(
        _inproj_kernel,
        grid=(S // tm,),
        in_specs=[
            pl.BlockSpec((tm, D_MODEL), row),
            pl.BlockSpec((1, D_MODEL), fixed),
            pl.BlockSpec((1, D_MODEL), fixed),
            pl.BlockSpec((1, D_MODEL, D_IN), lambda i: (layer, 0, 0), pipeline_mode=pl.Buffered(1)),
            pl.BlockSpec((1, 1, LANES), lambda i: (i, 0, 0)),
            pl.BlockSpec((1, 1, LANES), lambda i: (i, 0, 0)),
            pl.BlockSpec((tm, LANES), fixed),
            pl.BlockSpec((tm, LANES), fixed),
            pl.BlockSpec((POOL_WIDTH, POOL_WIDTH), fixed),
            pl.BlockSpec((1, POOL_WIDTH), fixed),
        ],
        out_specs=[
            pl.BlockSpec((1, DA_WIDTH, tm), lambda i: (i, 0, 0)),
            pl.BlockSpec((tm, DA_WIDTH), row),
            pl.BlockSpec((1, DA_HEADS * ATTN_VT_ROWS, tm), lambda i: (i, 0, 0)),
            pl.BlockSpec((tm, 4 * HG_WIDTH), row),
            pl.BlockSpec((tm, POOL_WIDTH), row),
        ],
        out_shape=[
            jax.ShapeDtypeStruct((S // tm, DA_WIDTH, tm), BF16),
            jax.ShapeDtypeStruct((S, DA_WIDTH), BF16),
            jax.ShapeDtypeStruct((S // tm, DA_HEADS * ATTN_VT_ROWS, tm), BF16),
            jax.ShapeDtypeStruct((S, 4 * HG_WIDTH), F32),
            jax.ShapeDtypeStruct((S, POOL_WIDTH), BF16),
        ],
        scratch_shapes=[
            pltpu.VMEM((D_MODEL, D_IN), BF16),
            pltpu.VMEM((DA_WIDTH, D_MODEL), BF16),
            pltpu.VMEM((POOL_HALO, POOL_WIDTH), F32),
        ],
        compiler_params=_params(("arbitrary",)),
        name="in_proj",
    )(x, sc, sh, w_in, cos_b, sin_b, cos_r, sin_r, w_bd, pool_scale.reshape(1, POOL_WIDTH))


def _attn_kernel(lam_ref, g_ref, qt_ref, k_ref, vt_ref, o_ref, m_scr, a_scr, c_scr, acc_scr, s_scr,
                 p_scr, *, lam_init):
    bq = qt_ref.shape[0] * qt_ref.shape[2]
    bk = vt_ref.shape[2]
    cg = ATTN_QUERY_GROUP
    i = pl.program_id(1)
    qt = jnp.concatenate([qt_ref[n] for n in range(qt_ref.shape[0])], axis=1)
    rowq = lax.broadcasted_iota(jnp.int32, qt.shape, 0)
    zero = jnp.zeros_like(qt)
    maps = (jnp.where(rowq < DA_HEAD_DIM, qt, zero), jnp.where(rowq >= DA_HEAD_DIM, qt, zero))
    groups = [(a * bq + c0, c0, maps[a][:, c0:c0 + cg]) for a in range(2) for c0 in range(0, bq, cg)]
    m_scr[...] = jnp.full(m_scr.shape, -jnp.inf, F32)
    acc_scr[...] = jnp.zeros(acc_scr.shape, F32)
    hk = bk // 2
    key_i = lax.broadcasted_iota(jnp.int32, (hk, cg), 0)
    qry_i = lax.broadcasted_iota(jnp.int32, (hk, cg), 1)
    ng = len(groups)
    last = ng - 1

    def cols_of(g):
        return slice(groups[g][0], groups[g][0] + cg)

    def ccols_of(sset, g):
        return slice(sset * 2 * bq + groups[g][0], sset * 2 * bq + groups[g][0] + cg)

    def mask_mode(key_offset, g):
        c0 = groups[g][1]
        if key_offset + hk - 1 <= c0:
            return None
        if key_offset > c0 + cg - 1:
            return "skip"
        return key_offset

    def score_stage(hb, sset, g, key_offset):
        kb = k_ref[pl.ds(pl.multiple_of(hb * hk, hk), hk), :]
        s = jnp.dot(kb, groups[g][2], preferred_element_type=F32)
        if key_offset is not None:
            s = jnp.where(key_i + key_offset <= qry_i + groups[g][1], s, MASK_VALUE)
        s_scr[sset * ng + g] = s
        c_scr[:, ccols_of(sset, g)] = jnp.max(s, axis=0, keepdims=True)

    def max_stage(sset, g):
        cols = cols_of(g)
        m_prev = m_scr[:, cols]
        m_new = jnp.maximum(m_prev, c_scr[:, ccols_of(sset, g)])
        a_scr[:, cols] = jnp.exp2(m_prev - m_new)
        m_scr[:, cols] = m_new

    def exp_stage(sset, g, vt_half, carried, defer):
        cols = cols_of(g)
        p = jnp.exp2(s_scr[sset * ng + g] - m_scr[:, cols])
        alpha = a_scr[:, cols]
        pb = p.astype(BF16)
        acc = acc_scr[:, cols]
        if carried is not None:
            acc = acc + carried
        if defer:
            acc_scr[:, cols] = alpha * acc
            p_scr[...] = pb
        else:
            acc_scr[:, cols] = alpha * acc + jnp.dot(vt_half, pb, preferred_element_type=F32)

    def items(key_offset):
        if key_offset is None:
            return [(g, None) for g in range(ng)]
        return [(g, mask_mode(key_offset, g)) for g in range(ng) if mask_mode(key_offset, g) != "skip"]

    def iteration(t, offs, look):
        vt_a = vt_ref[t, :, 0:hk]
        vt_b = vt_ref[t, :, hk:bk]
        carried = jnp.dot(vt_ref[jnp.maximum(t - 1, 0), :, hk:bk], p_scr[...],
                          preferred_element_type=F32)
        a_items = items(offs)
        b_items = items(None if offs is None else offs + hk)
        n_items = [] if look == "stop" else items(look)
        score_stage(2 * t + 1, 1, *b_items[0])
        for n, (g, _) in enumerate(a_items):
            if n + 1 < len(b_items):
                score_stage(2 * t + 1, 1, *b_items[n + 1])
            if n + 1 < len(a_items):
                max_stage(0, a_items[n + 1][0])
            else:
                max_stage(1, b_items[0][0])
            exp_stage(0, g, vt_a, carried if g == last else None, False)
        if n_items:
            score_stage(2 * t + 2, 0, *n_items[0])
        for n, (g, _) in enumerate(b_items):
            if n + 1 < len(n_items):
                score_stage(2 * t + 2, 0, *n_items[n + 1])
            if n + 1 < len(b_items):
                max_stage(1, b_items[n + 1][0])
            elif n_items:
                max_stage(0, n_items[0][0])
            exp_stage(1, g, vt_b, None, bool(n_items) and g == last)

    p_scr[...] = jnp.zeros(p_scr.shape, BF16)
    for g in range(ng):
        score_stage(0, 0, g, (0 - i) * bq)
    max_stage(0, 0)

    per_q = bq // bk
    first = i * per_q
    unroll = ATTN_UNROLL

    def body(u, carry):
        for k in range(unroll):
            iteration(unroll * u + k, None, None)
        return carry

    assert per_q % unroll == 0
    lax.fori_loop(0, jnp.maximum(first // unroll - 1, 0), body, 0)

    @pl.when(i > 0)
    def _():
        for k in range(unroll - 1):
            iteration(first - unroll + k, None, None)
        iteration(first - 1, None, 0)

    for d in range(per_q):
        iteration(first + d, d * bk, (d + 1) * bk if d + 1 < per_q else "stop")

    lq = lam_ref[...]
    lam = (jnp.exp(jnp.sum(lq[0:1] * lq[1:2], axis=1, keepdims=True))
           - jnp.exp(jnp.sum(lq[2:3] * lq[3:4], axis=1, keepdims=True)) + lam_init)
    o = acc_scr[0:DA_VAL_DIM, :] / acc_scr[DA_VAL_DIM:DA_VAL_DIM + 1, :]
    o = o[:, :bq] - lam * o[:, bq:]
    ms = jnp.mean(o * o, axis=0, keepdims=True)
    o = o * lax.rsqrt(ms + EPS) * g_ref[...] * (1.0 - lam_init)
    o_ref[...] = o.T.astype(BF16)


def _diff_attention(qt, k, vt, lam_qk, attn_g, layer):
    S = k.shape[0]
    bk = ATTN_BLOCK
    bq = ATTN_Q_BLOCK
    nk = S // bk
    lam_init = 0.8 - 0.6 * math.exp(-0.3 * layer)
    return pl.pallas_call(
        functools.partial(_attn_kernel, lam_init=lam_init),
        grid=(DA_HEADS, S // bq),
        in_specs=[
            pl.BlockSpec((4, DA_HEAD_DIM), lambda h, i: (0, 0)),
            pl.BlockSpec((DA_VAL_DIM, 1), lambda h, i: (0, 0)),
            pl.BlockSpec((bq // bk, LANES, bk), lambda h, i: (i, h, 0)),
            pl.BlockSpec((S, LANES), lambda h, i: (0, h)),
            pl.BlockSpec((nk, ATTN_VT_ROWS, bk), lambda h, i: (0, h, 0)),
        ],
        out_specs=pl.BlockSpec((bq, DA_VAL_DIM), lambda h, i: (i, h)),
        out_shape=jax.ShapeDtypeStruct((S, DA_WIDTH), BF16),
        scratch_shapes=[
            pltpu.VMEM((1, 2 * bq), F32),
            pltpu.VMEM((1, 2 * bq), F32),
            pltpu.VMEM((1, 2 * 2 * bq), F32),
            pltpu.VMEM((ATTN_VT_ROWS, 2 * bq), F32),
            pltpu.VMEM((2 * (2 * bq // ATTN_QUERY_GROUP), bk // 2, ATTN_QUERY_GROUP), F32),
            pltpu.VMEM((bk // 2, ATTN_QUERY_GROUP), BF16),
        ],
        compiler_params=_params(("parallel", "parallel")),
        name="diff_attn",
    )(lam_qk, attn_g.reshape(DA_VAL_DIM, 1), qt, k, vt)


def _hgrn_kernel(lbl_ref, g_ref, hg_ref, o_ref, st_scr, stb_scr, o_scr, *, layer):
    T = hg_ref.shape[0]
    c = HG_CHUNK
    W = HG_WIDTH

    @pl.when(pl.program_id(0) == 0)
    def _():
        st_scr[...] = jnp.zeros(st_scr.shape, F32)
        stb_scr[...] = jnp.zeros(stb_scr.shape, BF16)

    lg = lbl_ref[...]
    ex = jnp.exp(lg - jnp.max(lg, axis=0, keepdims=True))
    prob = ex / jnp.sum(ex, axis=0, keepdims=True)
    lb = jnp.zeros((1, W), F32)
    for li in range(1, layer + 1):
        lb = lb + prob[li:li + 1]

    same_head = _head_block_mask(W, HG_DIM)
    ones_bd = same_head.astype(BF16)
    rowi = lax.broadcasted_iota(jnp.int32, (c, W), 0)

    def body(s, carry):
        sl = pl.ds(pl.multiple_of(s * c, c), c)
        qf = hg_ref[sl, 0:W]
        z = hg_ref[sl, W:2 * W]
        iv = hg_ref[sl, 2 * W:3 * W]
        f = lb + (1.0 - lb) * jax.nn.sigmoid(z)
        logf = jnp.log(jnp.maximum(f, TINY))
        key = (1.0 - lb) * jax.nn.sigmoid(-z)
        val = _silu(iv)
        b = logf
        sft = 1
        while sft < c:
            b = b + jnp.where(rowi >= sft, pltpu.roll(b, sft, 0), 0.0)
            sft *= 2
        b_last = b[c - 1:c]
        kd = (key * jnp.exp(b_last - b)).astype(BF16)
        upd = lax.dot_general(val.astype(BF16), kd, (((0,), (0,)), ((), ())),
                              preferred_element_type=F32)
        es = [(qf * key).astype(BF16)]
        vrs = [val]
        for d in range(1, c):
            kr = pltpu.roll(key, d, 0)
            br = pltpu.roll(b, d, 0)
            e = jnp.where(rowi >= d, qf * kr * jnp.exp(b - br), 0.0)
            es.append(e.astype(BF16))
            vrs.append(pltpu.roll(val, d, 0))
        estack = jnp.concatenate(es, axis=0)
        r = jnp.dot(estack, ones_bd, preferred_element_type=F32)
        o = r[0:c] * vrs[0]
        for d in range(1, c):
            o = o + r[d * c:(d + 1) * c] * vrs[d]
        qd = (qf * jnp.exp(b)).astype(BF16)
        o = o + lax.dot_general(qd, stb_scr[...], (((1,), (1,)), ((), ())),
                                preferred_element_type=F32)
        o_scr[sl, :] = o
        decay = jnp.exp(b_last)
        for hd in range(HG_HEADS):
            rws = slice(hd * HG_DIM, (hd + 1) * HG_DIM)
            lns = slice(hd * HG_DIM // LANES * LANES, (hd * HG_DIM // LANES + 1) * LANES)
            new = decay[:, lns] * st_scr[rws, lns] + jnp.where(same_head[rws, lns], upd[rws, lns], 0.0)
            st_scr[rws, lns] = new
            stb_scr[rws, lns] = new.astype(BF16)
        return carry

    lax.fori_loop(0, T // c, body, 0, unroll=HG_UNROLL)

    o = o_scr[...]
    sq = o * o
    sq_hi = sq.astype(BF16)
    sq_lo = (sq - sq_hi.astype(F32)).astype(BF16)
    ms = (jnp.dot(sq_hi, ones_bd, preferred_element_type=F32)
          + jnp.dot(sq_lo, ones_bd, preferred_element_type=F32)) * (1.0 / HG_DIM)
    gate = hg_ref[:, 3 * W:4 * W]
    o_ref[...] = (o * lax.rsqrt(ms + EPS) * g_ref[...] * _silu(gate)).astype(BF16)


def _hgrn2(hg, lb_logits, norm_g, layer):
    S = hg.shape[0]
    T = HG_BLOCK
    g_t = jnp.tile(norm_g.reshape(1, HG_DIM), (1, HG_HEADS))
    return pl.pallas_call(
        functools.partial(_hgrn_kernel, layer=layer),
        grid=(S // T,),
        in_specs=[
            pl.BlockSpec((DEPTH, HG_WIDTH), lambda i: (0, 0)),
            pl.BlockSpec((1, HG_WIDTH), lambda i: (0, 0)),
            pl.BlockSpec((T, 4 * HG_WIDTH), lambda i: (i, 0)),
        ],
        out_specs=pl.BlockSpec((T, HG_WIDTH), lambda i: (i, 0)),
        out_shape=jax.ShapeDtypeStruct((S, HG_WIDTH), BF16),
        scratch_shapes=[
            pltpu.VMEM((HG_WIDTH, HG_WIDTH), F32),
            pltpu.VMEM((HG_WIDTH, HG_WIDTH), BF16),
            pltpu.VMEM((T, HG_WIDTH), F32),
        ],
        compiler_params=_params(("arbitrary",)),
        name="hgrn2",
    )(lb_logits, g_t, hg)


def _pool_tile(u, w_bd, scale, halo_scr, blk):
    T = u.shape[0]
    H = POOL_HALO

    @pl.when(blk == 0)
    def _():
        halo_scr[...] = jnp.zeros(halo_scr.shape, F32)

    ext = jnp.concatenate([halo_scr[...], u], axis=0)
    halo_scr[...] = u[T - H:, :]
    sums = []
    s = ext
    w = 1
    while w < POOL_WINDOWS[-1]:
        s = s + pltpu.roll(s, w, 0)
        w *= 2
        sums.append(s[H:, :])
    t1 = (blk * T + 1 + lax.broadcasted_iota(jnp.int32, (T, POOL_WIDTH), 0)).astype(F32)
    lane = lax.broadcasted_iota(jnp.int32, (T, POOL_WIDTH), 1)
    pooled = None
    for gi, win in enumerate(POOL_WINDOWS):
        mean = sums[gi] / jnp.minimum(t1, float(win))
        pooled = mean if pooled is None else jnp.where(lane >= gi * POOL_GROUP, mean, pooled)
    pooled = pooled - u
    y = jnp.dot(pooled.astype(BF16), w_bd, preferred_element_type=F32)
    return (y * scale).astype(BF16)


def _outproj_kernel(a_ref, r_ref, p_ref, x_ref, wf_ref, g1_ref, lg_ref, lb_ref, o_ref, w_ref):
    @pl.when(pl.program_id(0) == 0)
    def _():
        w_ref[...] = wf_ref[0].astype(BF16)

    m = jnp.dot(a_ref[...], w_ref[0:DA_WIDTH, :], preferred_element_type=F32)
    m += jnp.dot(r_ref[...], w_ref[DA_WIDTH:DA_WIDTH + HG_WIDTH, :], preferred_element_type=F32)
    m += jnp.dot(p_ref[...], w_ref[DA_WIDTH + HG_WIDTH:, :], preferred_element_type=F32)
    y = ALPHA * x_ref[...] + (1.0 + g1_ref[...]) * m
    o_ref[...] = _layer_norm(y, lg_ref[...], lb_ref[...])


def _out_projection(a, r, p, x, w_out, layer, g1, ln_g, ln_b):
    S = x.shape[0]
    tm = ROW_TILE
    row = lambda i: (i, 0)
    fixed = lambda i: (0, 0)
    vec = pl.BlockSpec((1, D_MODEL), fixed)
    return pl.pallas_call(
        _outproj_kernel,
        grid=(S // tm,),
        in_specs=[
            pl.BlockSpec((tm, DA_WIDTH), row),
            pl.BlockSpec((tm, HG_WIDTH), row),
            pl.BlockSpec((tm, POOL_WIDTH), row),
            pl.BlockSpec((tm, D_MODEL), row),
            pl.BlockSpec((1, D_MIX, D_MODEL), lambda i: (layer, 0, 0), pipeline_mode=pl.Buffered(1)),
            vec, vec, vec,
        ],
        out_specs=pl.BlockSpec((tm, D_MODEL), row),
        out_shape=jax.ShapeDtypeStruct((S, D_MODEL), F32),
        scratch_shapes=[pltpu.VMEM((D_MIX, D_MODEL), BF16)],
        compiler_params=_params(("arbitrary",)),
        name="out_proj_ln",
    )(a, r, p, x, w_out, g1, ln_g.reshape(1, D_MODEL), ln_b.reshape(1, D_MODEL))


def _ffn_kernel(x_ref, sc_ref, sh_ref, g2_ref, w1_ref, w3_ref, w2_ref, lg_ref, lb_ref, *rest):
    n_cast = (len(rest) - 1) // 2
    o_ref = rest[n_cast]
    for src, dst in zip(rest[:n_cast], rest[n_cast + 1:]):
        dst[...] = src[...].astype(BF16)
    x = x_ref[...]
    h = (x * (1.0 + sc_ref[...]) + sh_ref[...]).astype(BF16)
    acc = jnp.zeros(x.shape, F32)
    step = 4 * MXU_DIM
    for lo in range(0, FFN_DIM, step):
        hi = min(lo + step, FFN_DIM)
        a = jnp.dot(h, w1_ref[:, lo:hi], preferred_element_type=F32)
        b = jnp.dot(h, w3_ref[:, lo:hi], preferred_element_type=F32)
        acc += jnp.dot((_silu(a) * b).astype(BF16), w2_ref[lo:hi, :], preferred_element_type=F32)
    y = ALPHA * x + (1.0 + g2_ref[...]) * acc
    o_ref[...] = _layer_norm(y, lg_ref[...], lb_ref[...])


def _dense_ffn(x, sc, sh, g2, w1, w3, w2, ln_g, ln_b, to_cast=()):
    S = x.shape[0]
    tm = FFN_CAST_TILE if to_cast else ROW_TILE
    steps = S // tm
    row = lambda i: (i, 0)
    fixed = lambda i: (0, 0)
    vec = pl.BlockSpec((1, D_MODEL), fixed)
    once = pl.Buffered(1)
    slabs = [a.reshape(-1, a.shape[-1]) for a in to_cast]
    slab_specs = [pl.BlockSpec((a.shape[0] // steps, a.shape[1]), row) for a in slabs]
    for a in slabs:
        assert a.shape[0] % (steps * BF16_ROWS) == 0
    outs = pl.pallas_call(
        _ffn_kernel,
        grid=(steps,),
        in_specs=[
            pl.BlockSpec((tm, D_MODEL), row),
            vec, vec, vec,
            pl.BlockSpec((D_MODEL, FFN_DIM), fixed, pipeline_mode=once),
            pl.BlockSpec((D_MODEL, FFN_DIM), fixed, pipeline_mode=once),
            pl.BlockSpec((FFN_DIM, D_MODEL), fixed, pipeline_mode=once),
            vec, vec,
        ] + slab_specs,
        out_specs=[pl.BlockSpec((tm, D_MODEL), row)] + slab_specs,
        out_shape=[jax.ShapeDtypeStruct((S, D_MODEL), F32)]
        + [jax.ShapeDtypeStruct(a.shape, BF16) for a in slabs],
        compiler_params=_params(("parallel",)),
        name="dense_ffn_ln",
    )(x, sc, sh, g2, w1, w3, w2, ln_g.reshape(1, D_MODEL), ln_b.reshape(1, D_MODEL), *slabs)
    return outs[0], [o.reshape(a.shape) for o, a in zip(outs[1:], to_cast)]


def _router_kernel(x_ref, sc_ref, sh_ref, rw_ref, h_ref, rank_ref, comb_ref, cum_ref):
    T = x_ref.shape[0]
    tc = MOE_CHUNK
    E = N_EXPERTS
    h = x_ref[...] * (1.0 + sc_ref[...]) + sh_ref[...]
    h_hi = h.astype(BF16)
    h_ref[...] = h_hi
    h_lo = (h - h_hi.astype(F32)).astype(BF16)
    rw = rw_ref[...]
    rw_hi = rw.astype(BF16)
    rw_lo = (rw - rw_hi.astype(F32)).astype(BF16)
    nt = (((1,), (1,)), ((), ()))
    first = lax.dot_general(jnp.concatenate([rw_hi, rw_lo], axis=0), h_hi, nt,
                            preferred_element_type=F32)
    second = lax.dot_general(jnp.concatenate([rw_hi, jnp.zeros_like(rw_hi)], axis=0), h_lo, nt,
                             preferred_element_type=F32)
    logits = first[0:E] + first[E:2 * E] + second[0:E]
    eidx = lax.broadcasted_iota(jnp.int32, (E, T), 0)
    v1 = jnp.max(logits, axis=0, keepdims=True)
    i1 = jnp.min(jnp.where(logits == v1, eidx, E), axis=0, keepdims=True)
    m1 = eidx == i1
    rest = jnp.where(m1, -jnp.inf, logits)
    v2 = jnp.max(rest, axis=0, keepdims=True)
    i2 = jnp.min(jnp.where(rest == v2, eidx, E), axis=0, keepdims=True)
    m2 = eidx == i2
    e2 = jnp.exp(v2 - v1)
    g1 = 1.0 / (1.0 + e2)
    g2 = e2 / (1.0 + e2)
    comb = jnp.where(m1, g1, 0.0) + jnp.where(m2, g2, 0.0)
    routed = jnp.logical_or(m1, m2)
    ind = routed.astype(BF16)
    r_i = lax.broadcasted_iota(jnp.int32, (tc, tc), 0)
    c_i = lax.broadcasted_iota(jnp.int32, (tc, tc), 1)
    strict = (r_i < c_i).astype(BF16)
    lane = lax.broadcasted_iota(jnp.int32, (E, LANES), 1)
    running = jnp.zeros((E, 1), F32)
    cum = jnp.zeros((E, LANES), F32)
    for ci in range(T // tc):
        sl = slice(ci * tc, (ci + 1) * tc)
        ind_c = ind[:, sl]
        rank_c = jnp.dot(ind_c, strict, preferred_element_type=F32) + running
        rank_ref[:, ci] = jnp.where(routed[:, sl], rank_c, -1.0).reshape(E, 1, tc)
        comb_ref[:, ci] = comb[:, sl].reshape(E, 1, tc)
        cum = jnp.where(lane == ci, running, cum)
        running = running + jnp.sum(ind_c.astype(F32), axis=1, keepdims=True)
    cum = jnp.where(lane == T // tc, running, cum)
    cum_ref[0] = cum.astype(jnp.int32)


def _router(x, sc, sh, router_w):
    S = x.shape[0]
    T = MOE_BLOCK
    E = N_EXPERTS
    fixed = lambda i: (0, 0)
    vec = pl.BlockSpec((1, D_MODEL), fixed)
    return pl.pallas_call(
        _router_kernel,
        grid=(S // T,),
        in_specs=[
            pl.BlockSpec((T, D_MODEL), lambda i: (i, 0)),
            vec, vec,
            pl.BlockSpec((E, D_MODEL), fixed),
        ],
        out_specs=[
            pl.BlockSpec((T, D_MODEL), lambda i: (i, 0)),
            pl.BlockSpec((E, T // MOE_CHUNK, 1, MOE_CHUNK), lambda i: (0, i, 0, 0)),
            pl.BlockSpec((E, T // MOE_CHUNK, 1, MOE_CHUNK), lambda i: (0, i, 0, 0)),
            pl.BlockSpec((1, E, LANES), lambda i: (i, 0, 0)),
        ],
        out_shape=[
            jax.ShapeDtypeStruct((S, D_MODEL), BF16),
            jax.ShapeDtypeStruct((E, S // MOE_CHUNK, 1, MOE_CHUNK), F32),
            jax.ShapeDtypeStruct((E, S // MOE_CHUNK, 1, MOE_CHUNK), F32),
            jax.ShapeDtypeStruct((S // T, E, LANES), jnp.int32),
        ],
        compiler_params=_params(("parallel",)),
        name="moe_router",
    )(x, sc, sh, router_w.T)


def _moe_kernel(cum_ref, h_ref, rank_ref, comb_ref, w1_ref, w3_ref, w2_ref, y_ref, xe_scr, gate_scr,
                out_scr):
    T = h_ref.shape[0]
    tr, tc, win = MOE_TILE, MOE_CHUNK, MOE_WINDOW
    nchunk = T // tc
    b = pl.program_id(0)
    e = pl.program_id(1)
    f = pl.program_id(2)

    @pl.when(jnp.logical_and(e == 0, f == 0))
    def _():
        y_ref[...] = jnp.zeros(y_ref.shape, F32)

    base = (b * N_EXPERTS + e) * LANES
    count = cum_ref[base + nchunk]

    def token_row(ref, first, n):
        blk = ref[0, pl.ds(first, n)]
        return jnp.concatenate([blk[c] for c in range(n)], axis=1)

    def process(lo, tr):
        rows = pl.ds(pl.multiple_of(lo, BF16_ROWS), tr)

        def hits(first, n):
            want = (lo + lax.broadcasted_iota(jnp.int32, (tr, n * tc), 0)).astype(F32)
            return token_row(rank_ref, first, n) == want

        def token_span(first, n):
            start = first * tc
            if not isinstance(start, int):
                start = pl.multiple_of(start, tc)
            return pl.ds(start, n * tc)

        c_lo = jnp.int32(0)
        c_hi = jnp.int32(-1)
        for c in range(nchunk):
            c_lo += (cum_ref[base + c + 1] <= lo).astype(jnp.int32)
            c_hi += (cum_ref[base + c] < lo + tr).astype(jnp.int32)
        fits = c_hi - c_lo < win
        window = (jnp.minimum(c_lo, nchunk - win), win)
        block = (0, nchunk)

        def gather(first, n):
            hit = hits(first, n)
            xe_scr[rows, :] = jnp.dot(hit.astype(BF16), h_ref[token_span(first, n), :],
                                      preferred_element_type=F32).astype(BF16)
            gate_scr[rows, :] = jnp.sum(jnp.where(hit, token_row(comb_ref, first, n), 0.0),
                                        axis=1, keepdims=True)

        @pl.when(jnp.logical_and(f == 0, fits))
        def _():
            gather(*window)

        @pl.when(jnp.logical_and(f == 0, jnp.logical_not(fits)))
        def _():
            gather(*block)

        xe = xe_scr[rows, :]
        a = jnp.dot(xe, w1_ref[0], preferred_element_type=F32)
        g = jnp.dot(xe, w3_ref[0], preferred_element_type=F32)
        mid = (_silu(a) * g).astype(BF16)
        part = jnp.dot(mid, w2_ref[0], preferred_element_type=F32)

        @pl.when(f == 0)
        def _():
            out_scr[rows, :] = part

        if MOE_FSPLIT > 2:
            @pl.when(jnp.logical_and(f > 0, f < MOE_FSPLIT - 1))
            def _():
                out_scr[rows, :] += part

        @pl.when(f == MOE_FSPLIT - 1)
        def _():
            out_bf = ((out_scr[rows, :] + part) * gate_scr[rows, :]).astype(BF16)

            def scatter(first, n):
                y_ref[token_span(first, n), :] += lax.dot_general(
                    hits(first, n).astype(BF16), out_bf, (((0,), (0,)), ((), ())),
                    preferred_element_type=F32)

            @pl.when(fits)
            def _():
                scatter(*window)

            @pl.when(jnp.logical_not(fits))
            def _():
                scatter(*block)

    full = count // tr
    rem = count - full * tr

    def tile(r, carry):
        process(r * tr, tr)
        return carry

    lax.fori_loop(0, full, tile, 0)

    @pl.when(jnp.logical_and(rem > 0, rem <= MOE_TAIL_TILE))
    def _():
        process(full * tr, MOE_TAIL_TILE)

    @pl.when(rem > MOE_TAIL_TILE)
    def _():
        process(full * tr, tr)


def _moe_ffn(h_bf, rank, comb, cum, w1, w3, w2):
    S = h_bf.shape[0]
    T = MOE_BLOCK
    fw = EXPERT_DIM // MOE_FSPLIT
    grid_spec = pltpu.PrefetchScalarGridSpec(
        num_scalar_prefetch=1,
        grid=(S // T, N_EXPERTS, MOE_FSPLIT),
        in_specs=[
            pl.BlockSpec((T, D_MODEL), lambda b, e, f, c: (b, 0), pipeline_mode=pl.Buffered(1)),
            pl.BlockSpec((1, T // MOE_CHUNK, 1, MOE_CHUNK), lambda b, e, f, c: (e, b, 0, 0)),
            pl.BlockSpec((1, T // MOE_CHUNK, 1, MOE_CHUNK), lambda b, e, f, c: (e, b, 0, 0)),
            pl.BlockSpec((1, D_MODEL, fw), lambda b, e, f, c: (e, 0, f)),
            pl.BlockSpec((1, D_MODEL, fw), lambda b, e, f, c: (e, 0, f)),
            pl.BlockSpec((1, fw, D_MODEL), lambda b, e, f, c: (e, f, 0)),
        ],
        out_specs=pl.BlockSpec((T, D_MODEL), lambda b, e, f, c: (b, 0)),
        scratch_shapes=[
            pltpu.VMEM((T + MOE_TILE, D_MODEL), BF16),
            pltpu.VMEM((T + MOE_TILE, 1), F32),
            pltpu.VMEM((T + MOE_TILE, D_MODEL), F32),
        ],
    )
    return pl.pallas_call(
        _moe_kernel,
        grid_spec=grid_spec,
        out_shape=jax.ShapeDtypeStruct((S, D_MODEL), F32),
        compiler_params=_params(("arbitrary", "arbitrary", "arbitrary")),
        name="moe_experts",
    )(cum.reshape(-1), h_bf, rank, comb, w1, w3, w2)


def _ln_kernel(x_ref, y_ref, g2_ref, lg_ref, lb_ref, o_ref):
    y = ALPHA * x_ref[...] + (1.0 + g2_ref[...]) * y_ref[...]
    o_ref[...] = _layer_norm(y, lg_ref[...], lb_ref[...])


def _residual_ln(x, y, g2, ln_g, ln_b):
    S = x.shape[0]
    tm = ROW_TILE
    row = lambda i: (i, 0)
    vec = pl.BlockSpec((1, D_MODEL), lambda i: (0, 0))
    return pl.pallas_call(
        _ln_kernel,
        grid=(S // tm,),
        in_specs=[pl.BlockSpec((tm, D_MODEL), row), pl.BlockSpec((tm, D_MODEL), row), vec, vec, vec],
        out_specs=pl.BlockSpec((tm, D_MODEL), row),
        out_shape=jax.ShapeDtypeStruct((S, D_MODEL), F32),
        compiler_params=_params(("parallel",)),
        name="residual_ln",
    )(x, y, g2, ln_g.reshape(1, D_MODEL), ln_b.reshape(1, D_MODEL))


def _rope_tables(S):
    d = DA_HEAD_DIM
    tm = ATTN_BLOCK
    inv_freq = 1.0 / (ROPE_THETA ** (jnp.arange(0, d, 2, dtype=F32) / d))
    inv_lane = inv_freq[jnp.arange(LANES) % (d // 2)][None, :]
    base = (jnp.arange(S // tm, dtype=F32) * tm)[:, None] * inv_lane
    offs = jnp.arange(tm, dtype=F32)[:, None] * inv_lane
    shape_b = (S // tm, 1, LANES)
    return jnp.cos(base).reshape(shape_b), jnp.sin(base).reshape(shape_b), jnp.cos(offs), jnp.sin(offs)


def kernel(x, c, w_ada, b_ada, w_in, lam_qk, attn_norm_g, hg_lb_logits, hg_norm_g, pool_w, pool_scale, w_out, ln1_g, ln1_b, ln2_g, ln2_b, ffn_w1, ffn_w3, ffn_w2, router_w, exp_w1, exp_w3, exp_w2):
    B, S, D = x.shape
    assert B == 1 and D == D_MODEL and S % MOE_BLOCK == 0 and S % ATTN_Q_BLOCK == 0
    xs = x.reshape(S, D)
    mod = _modulation(c, w_ada, b_ada)
    rope = _rope_tables(S)
    experts_bf = {}
    for l in range(DEPTH):
        sh1, sc1, g1, sh2, sc2, g2 = (mod[l, j] for j in range(6))
        qt, k, vt, hg, p = _in_projection(xs, sc1, sh1, w_in, l, rope, pool_w[l], pool_scale[l])
        a = _diff_attention(qt, k, vt, lam_qk[l], attn_norm_g[l], l)
        r = _hgrn2(hg, hg_lb_logits, hg_norm_g[l], l)
        xs = _out_projection(a, r, p, xs, w_out, l, g1, ln1_g[l], ln1_b[l])
        if l % 2 == 0:
            i = l // 2
            ride = (exp_w1[i], exp_w3[i], exp_w2[i]) if l + 1 < DEPTH else ()
            xs, cast = _dense_ffn(xs, sc2, sh2, g2, ffn_w1[i].astype(BF16), ffn_w3[i].astype(BF16),
                                  ffn_w2[i].astype(BF16), ln2_g[l], ln2_b[l], ride)
            if ride:
                experts_bf[i] = cast
        else:
            i = l // 2
            h_bf, rank, comb, cum = _router(xs, sc2, sh2, router_w[i])
            w1, w3, w2 = experts_bf.get(i) or [w[i].astype(BF16) for w in (exp_w1, exp_w3, exp_w2)]
            y = _moe_ffn(h_bf, rank, comb, cum, w1, w3, w2)
            xs = _residual_ln(xs, y, g2, ln2_g[l], ln2_b[l])
    return xs.reshape(B, S, D)
```

```python
import functools
import math

import jax
import jax.numpy as jnp
from jax import lax
from jax.experimental import pallas as pl
from jax.experimental.pallas import tpu as pltpu

F32 = jnp.float32
BF16 = jnp.bfloat16
HIGHEST = lax.Precision.HIGHEST

D_MODEL = 1024
DEPTH = 2
DA_HEADS = 4
DA_WIDTH = D_MODEL // 2
DA_HEAD_DIM = DA_WIDTH // (2 * DA_HEADS)
DA_VAL_DIM = 2 * DA_HEAD_DIM
HG_HEADS = 4
HG_WIDTH = D_MODEL // 4
HG_DIM = HG_WIDTH // HG_HEADS
POOL_WINDOWS = (2, 4, 8, 16)
POOL_WIDTH = D_MODEL // 4
POOL_GROUP = POOL_WIDTH // len(POOL_WINDOWS)
D_MIX = DA_WIDTH + HG_WIDTH + POOL_WIDTH
D_IN = 3 * DA_WIDTH + 4 * HG_WIDTH + POOL_WIDTH
ROPE_THETA = 10000.0
MASK_VALUE = -1e30
TINY = 1e-30
FFN_DIM = 11 * D_MODEL // 4
N_EXPERTS = 8
EXPERT_DIM = 7 * D_MODEL // 2
ALPHA = (2 * DEPTH) ** 0.25
EPS = 1e-5
LOG2_E = math.log2(math.e)

LANES = 128
BF16_ROWS = 16
MXU_DIM = 256
VMEM_LIMIT = 60 * 1024 * 1024

ROW_TILE = 1024
FFN_CAST_TILE = 256
ATTN_BLOCK = 512
ATTN_Q_BLOCK = 2048
ATTN_QUERY_GROUP = 256
ATTN_UNROLL = 4
ATTN_SUM_ROWS = BF16_ROWS
ATTN_VT_ROWS = DA_VAL_DIM + ATTN_SUM_ROWS
HG_BLOCK = 512
HG_CHUNK = 16
HG_UNROLL = 4
POOL_HALO = 16
MOE_BLOCK = 2048
MOE_TILE = 192
MOE_TAIL_TILE = 128
MOE_CHUNK = 128
MOE_WINDOW = 8
MOE_TAIL_WINDOW = 6
MOE_FSPLIT = 2


def _params(sem, vmem=VMEM_LIMIT):
    return pltpu.CompilerParams(dimension_semantics=sem, vmem_limit_bytes=vmem)


def _layer_norm(y, g, b):
    mu = jnp.mean(y, axis=-1, keepdims=True)
    d = y - mu
    var = jnp.mean(d * d, axis=-1, keepdims=True)
    return d * lax.rsqrt(var + EPS) * g + b


def _silu(x):
    return x * jax.nn.sigmoid(x)


def _head_block_mask(n, group):
    r = lax.broadcasted_iota(jnp.int32, (n, n), 0) // group
    c = lax.broadcasted_iota(jnp.int32, (n, n), 1) // group
    return r == c


def _mod_kernel(c_ref, w_ref, b_ref, o_ref):
    cond = _silu(c_ref[...])
    cond8 = jnp.broadcast_to(cond, (8, D_MODEL))
    r = jnp.dot(cond8, w_ref[0], preferred_element_type=F32, precision=HIGHEST)
    o_ref[0] = r[0:1] + b_ref[0]


def _modulation(c, w_ada, b_ada):
    parts = 2
    wide = 6 * D_MODEL // parts
    n = parts * DEPTH
    b3 = b_ada.reshape(n, 1, wide)
    out = pl.pallas_call(
        _mod_kernel,
        grid=(DEPTH, parts),
        in_specs=[
            pl.BlockSpec((1, D_MODEL), lambda l, j: (0, 0)),
            pl.BlockSpec((1, D_MODEL, wide), lambda l, j: (l, 0, j)),
            pl.BlockSpec((1, 1, wide), lambda l, j: (l * parts + j, 0, 0)),
        ],
        out_specs=pl.BlockSpec((1, 1, wide), lambda l, j: (l * parts + j, 0, 0)),
        out_shape=jax.ShapeDtypeStruct((n, 1, wide), F32),
        compiler_params=_params(("parallel", "parallel")),
        name="adaln_mod",
    )(c, w_ada, b3)
    return out.reshape(DEPTH, 6, 1, D_MODEL)


def _inproj_kernel(x_ref, sc_ref, sh_ref, wf_ref, cosb_ref, sinb_ref, cosr_ref, sinr_ref, pw_ref, ps_ref,
                   qt_ref, k_ref, vt_ref, hg_ref, p_ref, w_ref, wvt_ref, halo_scr):
    @pl.when(pl.program_id(0) == 0)
    def _():
        w_ref[...] = wf_ref[0].astype(BF16)
        wvt_ref[...] = wf_ref[0, :, 2 * DA_WIDTH:3 * DA_WIDTH].T.astype(BF16)

    h = (x_ref[...] * (1.0 + sc_ref[...]) + sh_ref[...]).astype(BF16)
    qk = jnp.dot(h, w_ref[:, : 2 * DA_WIDTH], preferred_element_type=F32)
    cb, sb, cr, sr = cosb_ref[0], sinb_ref[0], cosr_ref[...], sinr_ref[...]
    cos128 = cb * cr - sb * sr
    sin128 = sb * cr + cb * sr
    half = DA_HEAD_DIM // 2
    lane128 = lax.broadcasted_iota(jnp.int32, cos128.shape, 1)
    sin128 = jnp.where((lane128 & half) == 0, -sin128, sin128)
    reps = 2 * DA_WIDTH // LANES
    cos = jnp.tile(cos128, (1, reps))
    sin = jnp.tile(sin128, (1, reps))
    lane = lax.broadcasted_iota(jnp.int32, qk.shape, 1)
    first_half = (lane & half) == 0
    width = 2 * DA_WIDTH
    partner = jnp.where(first_half, pltpu.roll(qk, width - half, 1), pltpu.roll(qk, half, 1))
    rot = qk * cos + partner * sin
    qt_ref[0] = (rot[:, :DA_WIDTH] * (DA_HEAD_DIM ** -0.5 * LOG2_E)).T.astype(BF16)
    k_ref[...] = rot[:, DA_WIDTH:].astype(BF16)
    o = 2 * DA_WIDTH
    vt = lax.dot_general(wvt_ref[...], h, (((1,), (1,)), ((), ())),
                         preferred_element_type=F32).astype(BF16)
    ones = jnp.ones((ATTN_SUM_ROWS, vt.shape[1]), BF16)
    pieces = []
    for hd in range(DA_HEADS):
        pieces += [vt[hd * DA_VAL_DIM:(hd + 1) * DA_VAL_DIM], ones]
    vt_ref[0] = jnp.concatenate(pieces, axis=0)
    o += DA_WIDTH
    hg_ref[...] = jnp.dot(h, w_ref[:, o:o + 4 * HG_WIDTH], preferred_element_type=F32)
    o += 4 * HG_WIDTH
    pu = jnp.dot(h, w_ref[:, o:o + POOL_WIDTH], preferred_element_type=F32)
    p_ref[...] = _pool_tile(pu, pw_ref[...], ps_ref[...], halo_scr, pl.program_id(0))


def _in_projection(x, sc, sh, w_in, layer, rope, pool_w, pool_scale):
    S = x.shape[0]
    tm = ATTN_BLOCK
    row = lambda i: (i, 0)
    fixed = lambda i: (0, 0)
    cos_b, sin_b, cos_r, sin_r = rope
    w_bd = jax.scipy.linalg.block_diag(*[pool_w[g] for g in range(len(POOL_WINDOWS))]).astype(BF16)
    return pl.pallas_call(
        _inproj_kernel,
        grid=(S // tm,),
        in_specs=[
            pl.BlockSpec((tm, D_MODEL), row),
            pl.BlockSpec((1, D_MODEL), fixed),
            pl.BlockSpec((1, D_MODEL), fixed),
            pl.BlockSpec((1, D_MODEL, D_IN), lambda i: (layer, 0, 0), pipeline_mode=pl.Buffered(1)),
            pl.BlockSpec((1, 1, LANES), lambda i: (i, 0, 0)),
            pl.BlockSpec((1, 1, LANES), lambda i: (i, 0, 0)),
            pl.BlockSpec((tm, LANES), fixed),
            pl.BlockSpec((tm, LANES), fixed),
            pl.BlockSpec((POOL_WIDTH, POOL_WIDTH), fixed),
            pl.BlockSpec((1, POOL_WIDTH), fixed),
        ],
        out_specs=[
            pl.BlockSpec((1, DA_WIDTH, tm), lambda i: (i, 0, 0)),
            pl.BlockSpec((tm, DA_WIDTH), row),
            pl.BlockSpec((1, DA_HEADS * ATTN_VT_ROWS, tm), lambda i: (i, 0, 0)),
            pl.BlockSpec((tm, 4 * HG_WIDTH), row),
            pl.BlockSpec((tm, POOL_WIDTH), row),
        ],
        out_shape=[
            jax.ShapeDtypeStruct((S // tm, DA_WIDTH, tm), BF16),
            jax.ShapeDtypeStruct((S, DA_WIDTH), BF16),
            jax.ShapeDtypeStruct((S // tm, DA_HEADS * ATTN_VT_ROWS, tm), BF16),
            jax.ShapeDtypeStruct((S, 4 * HG_WIDTH), F32),
            jax.ShapeDtypeStruct((S, POOL_WIDTH), BF16),
        ],
        scratch_shapes=[
            pltpu.VMEM((D_MODEL, D_IN), BF16),
            pltpu.VMEM((DA_WIDTH, D_MODEL), BF16),
            pltpu.VMEM((POOL_HALO, POOL_WIDTH), F32),
        ],
        compiler_params=_params(("arbitrary",)),
        name="in_proj",
    )(x, sc, sh, w_in, cos_b, sin_b, cos_r, sin_r, w_bd, pool_scale.reshape(1, POOL_WIDTH))


def _attn_kernel(lam_ref, g_ref, qt_ref, k_ref, vt_ref, o_ref, m_scr, a_scr, c_scr, acc_scr, s_scr,
                 p_scr, *, lam_init):
    bq = qt_ref.shape[0] * qt_ref.shape[2]
    bk = vt_ref.shape[2]
    cg = ATTN_QUERY_GROUP
    i = pl.program_id(1)
    qt = jnp.concatenate([qt_ref[n] for n in range(qt_ref.shape[0])], axis=1)
    rowq = lax.broadcasted_iota(jnp.int32, qt.shape, 0)
    zero = jnp.zeros_like(qt)
    maps = (jnp.where(rowq < DA_HEAD_DIM, qt, zero), jnp.where(rowq >= DA_HEAD_DIM, qt, zero))
    groups = [(a * bq + c0, c0, maps[a][:, c0:c0 + cg]) for a in range(2) for c0 in range(0, bq, cg)]
    m_scr[...] = jnp.full(m_scr.shape, -jnp.inf, F32)
    acc_scr[...] = jnp.zeros(acc_scr.shape, F32)
    hk = bk // 2
    key_i = lax.broadcasted_iota(jnp.int32, (hk, cg), 0)
    qry_i = lax.broadcasted_iota(jnp.int32, (hk, cg), 1)
    ng = len(groups)
    last = ng - 1

    def cols_of(g):
        return slice(groups[g][0], groups[g][0] + cg)

    def ccols_of(sset, g):
        return slice(sset * 2 * bq + groups[g][0], sset * 2 * bq + groups[g][0] + cg)

    def mask_mode(key_offset, g):
        c0 = groups[g][1]
        if key_offset + hk - 1 <= c0:
            return None
        if key_offset > c0 + cg - 1:
            return "skip"
        return key_offset

    def score_stage(hb, sset, g, key_offset):
        kb = k_ref[pl.ds(pl.multiple_of(hb * hk, hk), hk), :]
        s = jnp.dot(kb, groups[g][2], preferred_element_type=F32)
        if key_offset is not None:
            s = jnp.where(key_i + key_offset <= qry_i + groups[g][1], s, MASK_VALUE)
        s_scr[sset * ng + g] = s
        c_scr[:, ccols_of(sset, g)] = jnp.max(s, axis=0, keepdims=True)

    def max_stage(sset, g):
        cols = cols_of(g)
        m_prev = m_scr[:, cols]
        m_new = jnp.maximum(m_prev, c_scr[:, ccols_of(sset, g)])
        a_scr[:, cols] = jnp.exp2(m_prev - m_new)
        m_scr[:, cols] = m_new

    def exp_stage(sset, g, vt_half, carried, defer):
        cols = cols_of(g)
        p = jnp.exp2(s_scr[sset * ng + g] - m_scr[:, cols])
        alpha = a_scr[:, cols]
        pb = p.astype(BF16)
        acc = acc_scr[:, cols]
        if carried is not None:
            acc = acc + carried
        if defer:
            acc_scr[:, cols] = alpha * acc
            p_scr[...] = pb
        else:
            acc_scr[:, cols] = alpha * acc + jnp.dot(vt_half, pb, preferred_element_type=F32)

    def items(key_offset):
        if key_offset is None:
            return [(g, None) for g in range(ng)]
        return [(g, mask_mode(key_offset, g)) for g in range(ng) if mask_mode(key_offset, g) != "skip"]

    def iteration(t, offs, look):
        vt_a = vt_ref[t, :, 0:hk]
        vt_b = vt_ref[t, :, hk:bk]
        carried = jnp.dot(vt_ref[jnp.maximum(t - 1, 0), :, hk:bk], p_scr[...],
                          preferred_element_type=F32)
        a_items = items(offs)
        b_items = items(None if offs is None else offs + hk)
        n_items = [] if look == "stop" else items(look)
        score_stage(2 * t + 1, 1, *b_items[0])
        for n, (g, _) in enumerate(a_items):
            if n + 1 < len(b_items):
                score_stage(2 * t + 1, 1, *b_items[n + 1])
            if n + 1 < len(a_items):
                max_stage(0, a_items[n + 1][0])
            else:
                max_stage(1, b_items[0][0])
            exp_stage(0, g, vt_a, carried if g == last else None, False)
        if n_items:
            score_stage(2 * t + 2, 0, *n_items[0])
        for n, (g, _) in enumerate(b_items):
            if n + 1 < len(n_items):
                score_stage(2 * t + 2, 0, *n_items[n + 1])
            if n + 1 < len(b_items):
                max_stage(1, b_items[n + 1][0])
            elif n_items:
                max_stage(0, n_items[0][0])
            exp_stage(1, g, vt_b, None, bool(n_items) and g == last)

    p_scr[...] = jnp.zeros(p_scr.shape, BF16)
    for g in range(ng):
        score_stage(0, 0, g, (0 - i) * bq)
    max_stage(0, 0)

    per_q = bq // bk
    first = i * per_q
    unroll = ATTN_UNROLL

    def body(u, carry):
        for k in range(unroll):
            iteration(unroll * u + k, None, None)
        return carry

    assert per_q % unroll == 0
    lax.fori_loop(0, jnp.maximum(first // unroll - 1, 0), body, 0)

    @pl.when(i > 0)
    def _():
        for k in range(unroll - 1):
            iteration(first - unroll + k, None, None)
        iteration(first - 1, None, 0)

    for d in range(per_q):
        iteration(first + d, d * bk, (d + 1) * bk if d + 1 < per_q else "stop")

    lq = lam_ref[...]
    lam = (jnp.exp(jnp.sum(lq[0:1] * lq[1:2], axis=1, keepdims=True))
           - jnp.exp(jnp.sum(lq[2:3] * lq[3:4], axis=1, keepdims=True)) + lam_init)
    o = acc_scr[0:DA_VAL_DIM, :] / acc_scr[DA_VAL_DIM:DA_VAL_DIM + 1, :]
    o = o[:, :bq] - lam * o[:, bq:]
    ms = jnp.mean(o * o, axis=0, keepdims=True)
    o = o * lax.rsqrt(ms + EPS) * g_ref[...] * (1.0 - lam_init)
    o_ref[...] = o.T.astype(BF16)


def _diff_attention(qt, k, vt, lam_qk, attn_g, layer):
    S = k.shape[0]
    bk = ATTN_BLOCK
    bq = ATTN_Q_BLOCK
    nk = S // bk
    lam_init = 0.8 - 0.6 * math.exp(-0.3 * layer)
    return pl.pallas_call(
        functools.partial(_attn_kernel, lam_init=lam_init),
        grid=(DA_HEADS, S // bq),
        in_specs=[
            pl.BlockSpec((4, DA_HEAD_DIM), lambda h, i: (0, 0)),
            pl.BlockSpec((DA_VAL_DIM, 1), lambda h, i: (0, 0)),
            pl.BlockSpec((bq // bk, LANES, bk), lambda h, i: (i, h, 0)),
            pl.BlockSpec((S, LANES), lambda h, i: (0, h)),
            pl.BlockSpec((nk, ATTN_VT_ROWS, bk), lambda h, i: (0, h, 0)),
        ],
        out_specs=pl.BlockSpec((bq, DA_VAL_DIM), lambda h, i: (i, h)),
        out_shape=jax.ShapeDtypeStruct((S, DA_WIDTH), BF16),
        scratch_shapes=[
            pltpu.VMEM((1, 2 * bq), F32),
            pltpu.VMEM((1, 2 * bq), F32),
            pltpu.VMEM((1, 2 * 2 * bq), F32),
            pltpu.VMEM((ATTN_VT_ROWS, 2 * bq), F32),
            pltpu.VMEM((2 * (2 * bq // ATTN_QUERY_GROUP), bk // 2, ATTN_QUERY_GROUP), F32),
            pltpu.VMEM((bk // 2, ATTN_QUERY_GROUP), BF16),
        ],
        compiler_params=_params(("parallel", "parallel")),
        name="diff_attn",
    )(lam_qk, attn_g.reshape(DA_VAL_DIM, 1), qt, k, vt)


def _hgrn_kernel(lbl_ref, g_ref, hg_ref, o_ref, st_scr, stb_scr, o_scr, *, layer):
    T = hg_ref.shape[0]
    c = HG_CHUNK
    W = HG_WIDTH

    @pl.when(pl.program_id(0) == 0)
    def _():
        st_scr[...] = jnp.zeros(st_scr.shape, F32)
        stb_scr[...] = jnp.zeros(stb_scr.shape, BF16)

    lg = lbl_ref[...]
    ex = jnp.exp(lg - jnp.max(lg, axis=0, keepdims=True))
    prob = ex / jnp.sum(ex, axis=0, keepdims=True)
    lb = jnp.zeros((1, W), F32)
    for li in range(1, layer + 1):
        lb = lb + prob[li:li + 1]

    same_head = _head_block_mask(W, HG_DIM)
    ones_bd = same_head.astype(BF16)
    rowi = lax.broadcasted_iota(jnp.int32, (c, W), 0)

    def body(s, carry):
        sl = pl.ds(pl.multiple_of(s * c, c), c)
        qf = hg_ref[sl, 0:W]
        z = hg_ref[sl, W:2 * W]
        iv = hg_ref[sl, 2 * W:3 * W]
        f = lb + (1.0 - lb) * jax.nn.sigmoid(z)
        logf = jnp.log(jnp.maximum(f, TINY))
        key = (1.0 - lb) * jax.nn.sigmoid(-z)
        val = _silu(iv)
        b = logf
        sft = 1
        while sft < c:
            b = b + jnp.where(rowi >= sft, pltpu.roll(b, sft, 0), 0.0)
            sft *= 2
        b_last = b[c - 1:c]
        kd = (key * jnp.exp(b_last - b)).astype(BF16)
        upd = lax.dot_general(val.astype(BF16), kd, (((0,), (0,)), ((), ())),
                              preferred_element_type=F32)
        es = [(qf * key).astype(BF16)]
        vrs = [val]
        for d in range(1, c):
            kr = pltpu.roll(key, d, 0)
            br = pltpu.roll(b, d, 0)
            e = jnp.where(rowi >= d, qf * kr * jnp.exp(b - br), 0.0)
            es.append(e.astype(BF16))
            vrs.append(pltpu.roll(val, d, 0))
        estack = jnp.concatenate(es, axis=0)
        r = jnp.dot(estack, ones_bd, preferred_element_type=F32)
        o = r[0:c] * vrs[0]
        for d in range(1, c):
            o = o + r[d * c:(d + 1) * c] * vrs[d]
        qd = (qf * jnp.exp(b)).astype(BF16)
        o = o + lax.dot_general(qd, stb_scr[...], (((1,), (1,)), ((), ())),
                                preferred_element_type=F32)
        o_scr[sl, :] = o
        decay = jnp.exp(b_last)
        for hd in range(HG_HEADS):
            rws = slice(hd * HG_DIM, (hd + 1) * HG_DIM)
            lns = slice(hd * HG_DIM // LANES * LANES, (hd * HG_DIM // LANES + 1) * LANES)
            new = decay[:, lns] * st_scr[rws, lns] + jnp.where(same_head[rws, lns], upd[rws, lns], 0.0)
            st_scr[rws, lns] = new
            stb_scr[rws, lns] = new.astype(BF16)
        return carry

    lax.fori_loop(0, T // c, body, 0, unroll=HG_UNROLL)

    o = o_scr[...]
    sq = o * o
    sq_hi = sq.astype(BF16)
    sq_lo = (sq - sq_hi.astype(F32)).astype(BF16)
    ms = (jnp.dot(sq_hi, ones_bd, preferred_element_type=F32)
          + jnp.dot(sq_lo, ones_bd, preferred_element_type=F32)) * (1.0 / HG_DIM)
    gate = hg_ref[:, 3 * W:4 * W]
    o_ref[...] = (o * lax.rsqrt(ms + EPS) * g_ref[...] * _silu(gate)).astype(BF16)


def _hgrn2(hg, lb_logits, norm_g, layer):
    S = hg.shape[0]
    T = HG_BLOCK
    g_t = jnp.tile(norm_g.reshape(1, HG_DIM), (1, HG_HEADS))
    return pl.pallas_call(
        functools.partial(_hgrn_kernel, layer=layer),
        grid=(S // T,),
        in_specs=[
            pl.BlockSpec((DEPTH, HG_WIDTH), lambda i: (0, 0)),
            pl.BlockSpec((1, HG_WIDTH), lambda i: (0, 0)),
            pl.BlockSpec((T, 4 * HG_WIDTH), lambda i: (i, 0)),
        ],
        out_specs=pl.BlockSpec((T, HG_WIDTH), lambda i: (i, 0)),
        out_shape=jax.ShapeDtypeStruct((S, HG_WIDTH), BF16),
        scratch_shapes=[
            pltpu.VMEM((HG_WIDTH, HG_WIDTH), F32),
            pltpu.VMEM((HG_WIDTH, HG_WIDTH), BF16),
            pltpu.VMEM((T, HG_WIDTH), F32),
        ],
        compiler_params=_params(("arbitrary",)),
        name="hgrn2",
    )(lb_logits, g_t, hg)


def _pool_tile(u, w_bd, scale, halo_scr, blk):
    T = u.shape[0]
    H = POOL_HALO

    @pl.when(blk == 0)
    def _():
        halo_scr[...] = jnp.zeros(halo_scr.shape, F32)

    ext = jnp.concatenate([halo_scr[...], u], axis=0)
    halo_scr[...] = u[T - H:, :]
    sums = []
    s = ext
    w = 1
    while w < POOL_WINDOWS[-1]:
        s = s + pltpu.roll(s, w, 0)
        w *= 2
        sums.append(s[H:, :])
    t1 = (blk * T + 1 + lax.broadcasted_iota(jnp.int32, (T, POOL_WIDTH), 0)).astype(F32)
    lane = lax.broadcasted_iota(jnp.int32, (T, POOL_WIDTH), 1)
    pooled = None
    for gi, win in enumerate(POOL_WINDOWS):
        mean = sums[gi] / jnp.minimum(t1, float(win))
        pooled = mean if pooled is None else jnp.where(lane >= gi * POOL_GROUP, mean, pooled)
    pooled = pooled - u
    y = jnp.dot(pooled.astype(BF16), w_bd, preferred_element_type=F32)
    return (y * scale).astype(BF16)


def _outproj_kernel(a_ref, r_ref, p_ref, x_ref, wf_ref, g1_ref, lg_ref, lb_ref, o_ref, w_ref):
    @pl.when(pl.program_id(0) == 0)
    def _():
        w_ref[...] = wf_ref[0].astype(BF16)

    m = jnp.dot(a_ref[...], w_ref[0:DA_WIDTH, :], preferred_element_type=F32)
    m += jnp.dot(r_ref[...], w_ref[DA_WIDTH:DA_WIDTH + HG_WIDTH, :], preferred_element_type=F32)
    m += jnp.dot(p_ref[...], w_ref[DA_WIDTH + HG_WIDTH:, :], preferred_element_type=F32)
    y = ALPHA * x_ref[...] + (1.0 + g1_ref[...]) * m
    o_ref[...] = _layer_norm(y, lg_ref[...], lb_ref[...])


def _out_projection(a, r, p, x, w_out, layer, g1, ln_g, ln_b):
    S = x.shape[0]
    tm = ROW_TILE
    row = lambda i: (i, 0)
    fixed = lambda i: (0, 0)
    vec = pl.BlockSpec((1, D_MODEL), fixed)
    return pl.pallas_call(
        _outproj_kernel,
        grid=(S // tm,),
        in_specs=[
            pl.BlockSpec((tm, DA_WIDTH), row),
            pl.BlockSpec((tm, HG_WIDTH), row),
            pl.BlockSpec((tm, POOL_WIDTH), row),
            pl.BlockSpec((tm, D_MODEL), row),
            pl.BlockSpec((1, D_MIX, D_MODEL), lambda i: (layer, 0, 0), pipeline_mode=pl.Buffered(1)),
            vec, vec, vec,
        ],
        out_specs=pl.BlockSpec((tm, D_MODEL), row),
        out_shape=jax.ShapeDtypeStruct((S, D_MODEL), F32),
        scratch_shapes=[pltpu.VMEM((D_MIX, D_MODEL), BF16)],
        compiler_params=_params(("arbitrary",)),
        name="out_proj_ln",
    )(a, r, p, x, w_out, g1, ln_g.reshape(1, D_MODEL), ln_b.reshape(1, D_MODEL))


def _ffn_kernel(x_ref, sc_ref, sh_ref, g2_ref, w1_ref, w3_ref, w2_ref, lg_ref, lb_ref, *rest):
    n_cast = (len(rest) - 1) // 2
    o_ref = rest[n_cast]
    for src, dst in zip(rest[:n_cast], rest[n_cast + 1:]):
        dst[...] = src[...].astype(BF16)
    x = x_ref[...]
    h = (x * (1.0 + sc_ref[...]) + sh_ref[...]).astype(BF16)
    acc = jnp.zeros(x.shape, F32)
    step = 4 * MXU_DIM
    for lo in range(0, FFN_DIM, step):
        hi = min(lo + step, FFN_DIM)
        a = jnp.dot(h, w1_ref[:, lo:hi], preferred_element_type=F32)
        b = jnp.dot(h, w3_ref[:, lo:hi], preferred_element_type=F32)
        acc += jnp.dot((_silu(a) * b).astype(BF16), w2_ref[lo:hi, :], preferred_element_type=F32)
    y = ALPHA * x + (1.0 + g2_ref[...]) * acc
    o_ref[...] = _layer_norm(y, lg_ref[...], lb_ref[...])


def _dense_ffn(x, sc, sh, g2, w1, w3, w2, ln_g, ln_b, to_cast=()):
    S = x.shape[0]
    tm = FFN_CAST_TILE if to_cast else ROW_TILE
    steps = S // tm
    row = lambda i: (i, 0)
    fixed = lambda i: (0, 0)
    vec = pl.BlockSpec((1, D_MODEL), fixed)
    once = pl.Buffered(1)
    slabs = [a.reshape(-1, a.shape[-1]) for a in to_cast]
    slab_specs = [pl.BlockSpec((a.shape[0] // steps, a.shape[1]), row) for a in slabs]
    for a in slabs:
        assert a.shape[0] % (steps * BF16_ROWS) == 0
    outs = pl.pallas_call(
        _ffn_kernel,
        grid=(steps,),
        in_specs=[
            pl.BlockSpec((tm, D_MODEL), row),
            vec, vec, vec,
            pl.BlockSpec((D_MODEL, FFN_DIM), fixed, pipeline_mode=once),
            pl.BlockSpec((D_MODEL, FFN_DIM), fixed, pipeline_mode=once),
            pl.BlockSpec((FFN_DIM, D_MODEL), fixed, pipeline_mode=once),
            vec, vec,
        ] + slab_specs,
        out_specs=[pl.BlockSpec((tm, D_MODEL), row)] + slab_specs,
        out_shape=[jax.ShapeDtypeStruct((S, D_MODEL), F32)]
        + [jax.ShapeDtypeStruct(a.shape, BF16) for a in slabs],
        compiler_params=_params(("parallel",)),
        name="dense_ffn_ln",
    )(x, sc, sh, g2, w1, w3, w2, ln_g.reshape(1, D_MODEL), ln_b.reshape(1, D_MODEL), *slabs)
    return outs[0], [o.reshape(a.shape) for o, a in zip(outs[1:], to_cast)]


def _router_kernel(x_ref, sc_ref, sh_ref, rw_ref, h_ref, rank_ref, comb_ref, cum_ref):
    T = x_ref.shape[0]
    tc = MOE_CHUNK
    E = N_EXPERTS
    h = x_ref[...] * (1.0 + sc_ref[...]) + sh_ref[...]
    h_hi = h.astype(BF16)
    h_ref[...] = h_hi
    h_lo = (h - h_hi.astype(F32)).astype(BF16)
    rw = rw_ref[...]
    rw_hi = rw.astype(BF16)
    rw_lo = (rw - rw_hi.astype(F32)).astype(BF16)
    nt = (((1,), (1,)), ((), ()))
    first = lax.dot_general(jnp.concatenate([rw_hi, rw_lo], axis=0), h_hi, nt,
                            preferred_element_type=F32)
    second = lax.dot_general(jnp.concatenate([rw_hi, jnp.zeros_like(rw_hi)], axis=0), h_lo, nt,
                             preferred_element_type=F32)
    logits = first[0:E] + first[E:2 * E] + second[0:E]
    eidx = lax.broadcasted_iota(jnp.int32, (E, T), 0)
    v1 = jnp.max(logits, axis=0, keepdims=True)
    i1 = jnp.min(jnp.where(logits == v1, eidx, E), axis=0, keepdims=True)
    m1 = eidx == i1
    rest = jnp.where(m1, -jnp.inf, logits)
    v2 = jnp.max(rest, axis=0, keepdims=True)
    i2 = jnp.min(jnp.where(rest == v2, eidx, E), axis=0, keepdims=True)
    m2 = eidx == i2
    e2 = jnp.exp(v2 - v1)
    g1 = 1.0 / (1.0 + e2)
    g2 = e2 / (1.0 + e2)
    comb = jnp.where(m1, g1, 0.0) + jnp.where(m2, g2, 0.0)
    routed = jnp.logical_or(m1, m2)
    ind = routed.astype(BF16)
    r_i = lax.broadcasted_iota(jnp.int32, (tc, tc), 0)
    c_i = lax.broadcasted_iota(jnp.int32, (tc, tc), 1)
    strict = (r_i < c_i).astype(BF16)
    lane = lax.broadcasted_iota(jnp.int32, (E, LANES), 1)
    running = jnp.zeros((E, 1), F32)
    cum = jnp.zeros((E, LANES), F32)
    for ci in range(T // tc):
        sl = slice(ci * tc, (ci + 1) * tc)
        ind_c = ind[:, sl]
        rank_c = jnp.dot(ind_c, strict, preferred_element_type=F32) + running
        rank_ref[:, ci] = jnp.where(routed[:, sl], rank_c, -1.0).reshape(E, 1, tc)
        comb_ref[:, ci] = comb[:, sl].reshape(E, 1, tc)
        cum = jnp.where(lane == ci, running, cum)
        running = running + jnp.sum(ind_c.astype(F32), axis=1, keepdims=True)
    cum = jnp.where(lane == T // tc, running, cum)
    cum_ref[0] = cum.astype(jnp.int32)


def _router(x, sc, sh, router_w):
    S = x.shape[0]
    T = MOE_BLOCK
    E = N_EXPERTS
    fixed = lambda i: (0, 0)
    vec = pl.BlockSpec((1, D_MODEL), fixed)
    return pl.pallas_call(
        _router_kernel,
        grid=(S // T,),
        in_specs=[
            pl.BlockSpec((T, D_MODEL), lambda i: (i, 0)),
            vec, vec,
            pl.BlockSpec((E, D_MODEL), fixed),
        ],
        out_specs=[
            pl.BlockSpec((T, D_MODEL), lambda i: (i, 0)),
            pl.BlockSpec((E, T // MOE_CHUNK, 1, MOE_CHUNK), lambda i: (0, i, 0, 0)),
            pl.BlockSpec((E, T // MOE_CHUNK, 1, MOE_CHUNK), lambda i: (0, i, 0, 0)),
            pl.BlockSpec((1, E, LANES), lambda i: (i, 0, 0)),
        ],
        out_shape=[
            jax.ShapeDtypeStruct((S, D_MODEL), BF16),
            jax.ShapeDtypeStruct((E, S // MOE_CHUNK, 1, MOE_CHUNK), F32),
            jax.ShapeDtypeStruct((E, S // MOE_CHUNK, 1, MOE_CHUNK), F32),
            jax.ShapeDtypeStruct((S // T, E, LANES), jnp.int32),
        ],
        compiler_params=_params(("parallel",)),
        name="moe_router",
    )(x, sc, sh, router_w.T)


def _moe_kernel(cum_ref, h_ref, rank_ref, comb_ref, w1_ref, w3_ref, w2_ref, y_ref, xe_scr, gate_scr,
                out_scr):
    T = h_ref.shape[0]
    tr, tc = MOE_TILE, MOE_CHUNK
    nchunk = T // tc
    b = pl.program_id(0)
    e = pl.program_id(1)
    f = pl.program_id(2)

    @pl.when(jnp.logical_and(e == 0, f == 0))
    def _():
        y_ref[...] = jnp.zeros(y_ref.shape, F32)

    base = (b * N_EXPERTS + e) * LANES
    count = cum_ref[base + nchunk]

    def token_row(ref, first, n):
        blk = ref[0, pl.ds(first, n)]
        return jnp.concatenate([blk[c] for c in range(n)], axis=1)

    def process(lo, tr):
        rows = pl.ds(pl.multiple_of(lo, BF16_ROWS), tr)
        win = MOE_WINDOW if tr == MOE_TILE else MOE_TAIL_WINDOW

        def hits(first, n):
            want = (lo + lax.broadcasted_iota(jnp.int32, (tr, n * tc), 0)).astype(F32)
            return token_row(rank_ref, first, n) == want

        def token_span(first, n):
            start = first * tc
            if not isinstance(start, int):
                start = pl.multiple_of(start, tc)
            return pl.ds(start, n * tc)

        c_lo = jnp.int32(0)
        c_hi = jnp.int32(-1)
        for c in range(nchunk):
            c_lo += (cum_ref[base + c + 1] <= lo).astype(jnp.int32)
            c_hi += (cum_ref[base + c] < lo + tr).astype(jnp.int32)
        fits = c_hi - c_lo < win
        window = (jnp.minimum(c_lo, nchunk - win), win)
        block = (0, nchunk)

        def gather(first, n):
            hit = hits(first, n)
            xe_scr[rows, :] = jnp.dot(hit.astype(BF16), h_ref[token_span(first, n), :],
                                      preferred_element_type=F32).astype(BF16)
            gate_scr[rows, :] = jnp.sum(jnp.where(hit, token_row(comb_ref, first, n), 0.0),
                                        axis=1, keepdims=True)

        @pl.when(jnp.logical_and(f == 0, fits))
        def _():
            gather(*window)

        @pl.when(jnp.logical_and(f == 0, jnp.logical_not(fits)))
        def _():
            gather(*block)

        xe = xe_scr[rows, :]
        a = jnp.dot(xe, w1_ref[0], preferred_element_type=F32)
        g = jnp.dot(xe, w3_ref[0], preferred_element_type=F32)
        mid = (_silu(a) * g).astype(BF16)
        part = jnp.dot(mid, w2_ref[0], preferred_element_type=F32)

        @pl.when(f == 0)
        def _():
            out_scr[rows, :] = part

        if MOE_FSPLIT > 2:
            @pl.when(jnp.logical_and(f > 0, f < MOE_FSPLIT - 1))
            def _():
                out_scr[rows, :] += part

        @pl.when(f == MOE_FSPLIT - 1)
        def _():
            out_bf = ((out_scr[rows, :] + part) * gate_scr[rows, :]).astype(BF16)

            def scatter(first, n):
                y_ref[token_span(first, n), :] += lax.dot_general(
                    hits(first, n).astype(BF16), out_bf, (((0,), (0,)), ((), ())),
                    preferred_element_type=F32)

            @pl.when(fits)
            def _():
                scatter(*window)

            @pl.when(jnp.logical_not(fits))
            def _():
                scatter(*block)

    full = count // tr
    rem = count - full * tr

    def tile(r, carry):
        process(r * tr, tr)
        return carry

    lax.fori_loop(0, full, tile, 0)

    @pl.when(jnp.logical_and(rem > 0, rem <= MOE_TAIL_TILE))
    def _():
        process(full * tr, MOE_TAIL_TILE)

    @pl.when(rem > MOE_TAIL_TILE)
    def _():
        process(full * tr, tr)


def _moe_ffn(h_bf, rank, comb, cum, w1, w3, w2):
    S = h_bf.shape[0]
    T = MOE_BLOCK
    fw = EXPERT_DIM // MOE_FSPLIT
    grid_spec = pltpu.PrefetchScalarGridSpec(
        num_scalar_prefetch=1,
        grid=(S // T, N_EXPERTS, MOE_FSPLIT),
        in_specs=[
            pl.BlockSpec((T, D_MODEL), lambda b, e, f, c: (b, 0), pipeline_mode=pl.Buffered(1)),
            pl.BlockSpec((1, T // MOE_CHUNK, 1, MOE_CHUNK), lambda b, e, f, c: (e, b, 0, 0)),
            pl.BlockSpec((1, T // MOE_CHUNK, 1, MOE_CHUNK), lambda b, e, f, c: (e, b, 0, 0)),
            pl.BlockSpec((1, D_MODEL, fw), lambda b, e, f, c: (e, 0, f)),
            pl.BlockSpec((1, D_MODEL, fw), lambda b, e, f, c: (e, 0, f)),
            pl.BlockSpec((1, fw, D_MODEL), lambda b, e, f, c: (e, f, 0)),
        ],
        out_specs=pl.BlockSpec((T, D_MODEL), lambda b, e, f, c: (b, 0)),
        scratch_shapes=[
            pltpu.VMEM((T + MOE_TILE, D_MODEL), BF16),
            pltpu.VMEM((T + MOE_TILE, 1), F32),
            pltpu.VMEM((T + MOE_TILE, D_MODEL), F32),
        ],
    )
    return pl.pallas_call(
        _moe_kernel,
        grid_spec=grid_spec,
        out_shape=jax.ShapeDtypeStruct((S, D_MODEL), F32),
        compiler_params=_params(("arbitrary", "arbitrary", "arbitrary")),
        name="moe_experts",
    )(cum.reshape(-1), h_bf, rank, comb, w1, w3, w2)


def _ln_kernel(x_ref, y_ref, g2_ref, lg_ref, lb_ref, o_ref):
    y = ALPHA * x_ref[...] + (1.0 + g2_ref[...]) * y_ref[...]
    o_ref[...] = _layer_norm(y, lg_ref[...], lb_ref[...])


def _residual_ln(x, y, g2, ln_g, ln_b):
    S = x.shape[0]
    tm = ROW_TILE
    row = lambda i: (i, 0)
    vec = pl.BlockSpec((1, D_MODEL), lambda i: (0, 0))
    return pl.pallas_call(
        _ln_kernel,
        grid=(S // tm,),
        in_specs=[pl.BlockSpec((tm, D_MODEL), row), pl.BlockSpec((tm, D_MODEL), row), vec, vec, vec],
        out_specs=pl.BlockSpec((tm, D_MODEL), row),
        out_shape=jax.ShapeDtypeStruct((S, D_MODEL), F32),
        compiler_params=_params(("parallel",)),
        name="residual_ln",
    )(x, y, g2, ln_g.reshape(1, D_MODEL), ln_b.reshape(1, D_MODEL))


def _rope_tables(S):
    d = DA_HEAD_DIM
    tm = ATTN_BLOCK
    inv_freq = 1.0 / (ROPE_THETA ** (jnp.arange(0, d, 2, dtype=F32) / d))
    inv_lane = inv_freq[jnp.arange(LANES) % (d // 2)][None, :]
    base = (jnp.arange(S // tm, dtype=F32) * tm)[:, None] * inv_lane
    offs = jnp.arange(tm, dtype=F32)[:, None] * inv_lane
    shape_b = (S // tm, 1, LANES)
    return jnp.cos(base).reshape(shape_b), jnp.sin(base).reshape(shape_b), jnp.cos(offs), jnp.sin(offs)


def kernel(x, c, w_ada, b_ada, w_in, lam_qk, attn_norm_g, hg_lb_logits, hg_norm_g, pool_w, pool_scale, w_out, ln1_g, ln1_b, ln2_g, ln2_b, ffn_w1, ffn_w3, ffn_w2, router_w, exp_w1, exp_w3, exp_w2):
    B, S, D = x.shape
    assert B == 1 and D == D_MODEL and S % MOE_BLOCK == 0 and S % ATTN_Q_BLOCK == 0
    xs = x.reshape(S, D)
    mod = _modulation(c, w_ada, b_ada)
    rope = _rope_tables(S)
    experts_bf = {}
    for l in range(DEPTH):
        sh1, sc1, g1, sh2, sc2, g2 = (mod[l, j] for j in range(6))
        qt, k, vt, hg, p = _in_projection(xs, sc1, sh1, w_in, l, rope, pool_w[l], pool_scale[l])
        a = _diff_attention(qt, k, vt, lam_qk[l], attn_norm_g[l], l)
        r = _hgrn2(hg, hg_lb_logits, hg_norm_g[l], l)
        xs = _out_projection(a, r, p, xs, w_out, l, g1, ln1_g[l], ln1_b[l])
        if l % 2 == 0:
            i = l // 2
            ride = (exp_w1[i], exp_w3[i], exp_w2[i]) if l + 1 < DEPTH else ()
            xs, cast = _dense_ffn(xs, sc2, sh2, g2, ffn_w1[i].astype(BF16), ffn_w3[i].astype(BF16),
                                  ffn_w2[i].astype(BF16), ln2_g[l], ln2_b[l], ride)
            if ride:
                experts_bf[i] = cast
        else:
            i = l // 2
            h_bf, rank, comb, cum = _router(xs, sc2, sh2, router_w[i])
            w1, w3, w2 = experts_bf.get(i) or [w[i].astype(BF16) for w in (exp_w1, exp_w3, exp_w2)]
            y = _moe_ffn(h_bf, rank, comb, cum, w1, w3, w2)
            xs = _residual_ln(xs, y, g2, ln2_g[l], ln2_b[l])
    return xs.reshape(B, S, D)
```
